```python
import math
import jax, jax.numpy as jnp
from jax import lax
import numpy as np

D_MODEL = 1024
BATCH = 8
SEQ = 16384
DEPTH = 4

N_EVEN = (DEPTH + 1) // 2
N_ODD = DEPTH // 2
A_GROUPS = 4
A_CHUNK = 128
A_WIDTH = D_MODEL // 2
A_GROUP_CH = A_WIDTH // A_GROUPS
B_HEADS = 8
B_HEAD_DIM = 64
B_WIDTH = B_HEADS * B_HEAD_DIM
Q_BLOCK = 128
MIX_WIDTH = A_WIDTH + B_WIDTH
IN_COLS = 2 * A_WIDTH + 3 * B_WIDTH + B_HEADS
S5_GROUP_CH = 16
S5_GROUPS = D_MODEL // S5_GROUP_CH
S5_STATE = 64
D_FF = 2816
CONV_W = 3
PLE_DIM = 256
EPS = 1e-6
NEG_INF = -1e30

kernel_name = "hybrid_gmlp_fox_s5_convffn_ple"


def rms_norm(x, g=None):
    xf = x.astype(jnp.float32)
    y = xf * lax.rsqrt(jnp.mean(xf * xf, axis=-1, keepdims=True) + EPS)
    if g is not None:
        y = y * g.astype(jnp.float32)
    return y.astype(x.dtype)


def gmlp_mixer(u, v, v_gain, w_s, b_s):
    bsz, seq = u.shape[0], u.shape[1]
    u = jax.nn.gelu(u)
    v = jax.nn.gelu(v).reshape(bsz, seq, A_GROUPS, A_GROUP_CH)
    v = rms_norm(v, v_gain.reshape(A_GROUPS, A_GROUP_CH))
    v = v.reshape(bsz, seq // A_CHUNK, A_CHUNK, A_GROUPS, A_GROUP_CH)
    tri = jnp.tril(jnp.ones((A_CHUNK, A_CHUNK), dtype=bool))
    w = jnp.where(tri[None], w_s, jnp.zeros_like(w_s))
    sv = jnp.einsum('gts,bnsgc->bntgc', w, v) + b_s.T[None, None, :, :, None]
    return u * sv.reshape(bsz, seq, A_WIDTH)


def fox_attention(q, k, v, f_logit, q_gain, k_gain):
    bsz, seq = q.shape[0], q.shape[1]
    q = rms_norm(q, q_gain)
    k = rms_norm(k, k_gain)
    c = jnp.cumsum(jax.nn.log_sigmoid(f_logit.astype(jnp.float32)), axis=1)
    nb = seq // Q_BLOCK
    qb = q.reshape(bsz, nb, Q_BLOCK, B_HEADS, B_HEAD_DIM).transpose(1, 0, 2, 3, 4)
    cb = c.reshape(bsz, nb, Q_BLOCK, B_HEADS).transpose(1, 0, 2, 3)
    pos_b = jnp.arange(seq, dtype=jnp.int32).reshape(nb, Q_BLOCK)
    kpos = jnp.arange(seq, dtype=jnp.int32)
    ck = c.transpose(0, 2, 1)
    scale = B_HEAD_DIM ** -0.5

    def block(args):
        qi, ci, pi = args
        s = jnp.einsum('bqhd,bkhd->bhqk', qi, k).astype(jnp.float32) * scale
        s = s + ci.transpose(0, 2, 1)[..., None] - ck[:, :, None, :]
        s = jnp.where(kpos[None, :] <= pi[:, None], s, NEG_INF)
        pr = jax.nn.softmax(s, axis=-1)
        return jnp.einsum('bhqk,bkhd->bqhd', pr.astype(v.dtype), v)

    o = lax.map(block, (qb, cb, pos_b))
    return o.transpose(1, 0, 2, 3, 4).reshape(bsz, seq, B_WIDTH)


def _complex_affine_combine(e1, e2):
    a1r, a1i, b1r, b1i = e1
    a2r, a2i, b2r, b2i = e2
    ar = a2r * a1r - a2i * a1i
    ai = a2r * a1i + a2i * a1r
    br = a2r * b1r - a2i * b1i + b2r
    bi = a2r * b1i + a2i * b1r + b2i
    return (ar, ai, br, bi)


def s5_mixer(u, a_re, a_im, log_dt, b_re, b_im, c_re, c_im, d):
    bsz, seq, _ = u.shape
    f32 = jnp.float32
    uf = u.astype(f32).reshape(bsz, seq, S5_GROUPS, S5_GROUP_CH)
    dt = jnp.exp(log_dt.astype(f32))[:, None]
    lr, li = a_re.astype(f32), a_im.astype(f32)
    mag = jnp.exp(lr * dt)
    ab_re, ab_im = mag * jnp.cos(li * dt), mag * jnp.sin(li * dt)
    den = lr * lr + li * li
    nr, ni = ab_re - 1.0, ab_im
    cr = (nr * lr + ni * li) / den
    ci = (ni * lr - nr * li) / den
    br, bi = b_re.astype(f32), b_im.astype(f32)
    bb_re = cr[..., None] * br - ci[..., None] * bi
    bb_im = cr[..., None] * bi + ci[..., None] * br
    bu_re = jnp.einsum('gpc,bsgc->bsgp', bb_re, uf)
    bu_im = jnp.einsum('gpc,bsgc->bsgp', bb_im, uf)
    a_r = jnp.broadcast_to(ab_re, bu_re.shape)
    a_i = jnp.broadcast_to(ab_im, bu_im.shape)
    _, _, xr, xi = lax.associative_scan(_complex_affine_combine, (a_r, a_i, bu_re, bu_im), axis=1)
    y = (jnp.einsum('gcp,bsgp->bsgc', c_re.astype(f32), xr)
         - jnp.einsum('gcp,bsgp->bsgc', c_im.astype(f32), xi)
         + d.astype(f32).reshape(S5_GROUPS, S5_GROUP_CH) * uf)
    return y.reshape(bsz, seq, D_MODEL).astype(u.dtype)


def conv_ffn(x, w_up, conv_w, conv_b, w_down):
    seq = x.shape[1]
    h = x @ w_up
    hp = jnp.pad(h, ((0, 0), (CONV_W - 1, 0), (0, 0)))
    hc = conv_b + conv_w[0] * hp[:, 0:seq]
    for j in range(1, CONV_W):
        hc = hc + conv_w[j] * hp[:, j:j + seq]
    g, up = jnp.split(hc, 2, axis=-1)
    return (jax.nn.silu(g) * up) @ w_down


def _fwd_setup_inputs(seed: int = 0) -> dict:
    key = jax.random.key(seed)
    ks = jax.random.split(key, 32)
    f32 = jnp.float32

    def nrm(k, shape, scale):
        return scale * jax.random.normal(k, shape, f32)

    a_im_base = jnp.broadcast_to(math.pi * jnp.arange(S5_STATE, dtype=f32), (N_ODD, S5_GROUPS, S5_STATE))
    return {
        "x": nrm(ks[0], (BATCH, SEQ, D_MODEL), 1.0),
        "p": nrm(ks[1], (DEPTH, BATCH, SEQ, PLE_DIM), 1.0),
        "norm_mix": 1.0 + nrm(ks[2], (DEPTH, D_MODEL), 0.05),
        "norm_ffn": 1.0 + nrm(ks[3], (DEPTH, D_MODEL), 0.05),
        "ev_w_in": nrm(ks[4], (N_EVEN, D_MODEL, IN_COLS), D_MODEL ** -0.5),
        "ev_b_fgate": 4.0 + nrm(ks[5], (N_EVEN, B_HEADS), 0.5),
        "ev_q_norm": 1.0 + nrm(ks[6], (N_EVEN, B_HEAD_DIM), 0.05),
        "ev_k_norm": 1.0 + nrm(ks[7], (N_EVEN, B_HEAD_DIM), 0.05),
        "ev_v_norm": 1.0 + nrm(ks[8], (N_EVEN, A_WIDTH), 0.05),
        "ev_w_spatial": nrm(ks[9], (N_EVEN, A_GROUPS, A_CHUNK, A_CHUNK), A_CHUNK ** -0.5),
        "ev_b_spatial": 1.0 + nrm(ks[10], (N_EVEN, A_GROUPS, A_CHUNK), 0.1),
        "ev_w_out": nrm(ks[11], (N_EVEN, MIX_WIDTH, D_MODEL), MIX_WIDTH ** -0.5),
        "od_a_re": -0.5 + nrm(ks[12], (N_ODD, S5_GROUPS, S5_STATE), 0.01),
        "od_a_im": a_im_base + nrm(ks[13], (N_ODD, S5_GROUPS, S5_STATE), 0.01),
        "od_log_dt": jax.random.uniform(ks[14], (N_ODD, S5_GROUPS), f32, math.log(1e-3), math.log(1e-1)),
        "od_b_re": nrm(ks[15], (N_ODD, S5_GROUPS, S5_STATE, S5_GROUP_CH), (2 * S5_GROUP_CH) ** -0.5),
        "od_b_im": nrm(ks[16], (N_ODD, S5_GROUPS, S5_STATE, S5_GROUP_CH), (2 * S5_GROUP_CH) ** -0.5),
        "od_c_re": nrm(ks[17], (N_ODD, S5_GROUPS, S5_GROUP_CH, S5_STATE), S5_STATE ** -0.5),
        "od_c_im": nrm(ks[18], (N_ODD, S5_GROUPS, S5_GROUP_CH, S5_STATE), S5_STATE ** -0.5),
        "od_d": nrm(ks[19], (N_ODD, D_MODEL), 1.0),
        "od_w_glu": nrm(ks[20], (N_ODD, D_MODEL, 2 * D_MODEL), D_MODEL ** -0.5),
        "ffn_w_up": nrm(ks[21], (DEPTH, D_MODEL, 2 * D_FF), D_MODEL ** -0.5),
        "ffn_conv_w": nrm(ks[22], (DEPTH, CONV_W, 2 * D_FF), CONV_W ** -0.5),
        "ffn_conv_b": nrm(ks[23], (DEPTH, 2 * D_FF), 0.02),
        "ffn_w_down": nrm(ks[24], (DEPTH, D_FF, D_MODEL), D_FF ** -0.5),
        "ple_w_proj": nrm(ks[25], (DEPTH, PLE_DIM, D_MODEL), PLE_DIM ** -0.5),
        "ple_w_gate": nrm(ks[26], (DEPTH, D_MODEL, D_MODEL), D_MODEL ** -0.5),
    }


def _fwd_reference(x, p, norm_mix, norm_ffn, ev_w_in, ev_b_fgate, ev_q_norm, ev_k_norm, ev_v_norm,
              ev_w_spatial, ev_b_spatial, ev_w_out, od_a_re, od_a_im, od_log_dt, od_b_re, od_b_im,
              od_c_re, od_c_im, od_d, od_w_glu, ffn_w_up, ffn_conv_w, ffn_conv_b, ffn_w_down,
              ple_w_proj, ple_w_gate):
    bsz, seq = x.shape[0], x.shape[1]
    splits = [A_WIDTH, 2 * A_WIDTH, 2 * A_WIDTH + B_WIDTH, 2 * A_WIDTH + 2 * B_WIDTH,
              2 * A_WIDTH + 3 * B_WIDTH]
    for i in range(DEPTH):
        h = rms_norm(x, norm_mix[i])
        if i % 2 == 0:
            e = i // 2
            z = h @ ev_w_in[e]
            u_a, v_a, q, k, v_b, f = jnp.split(z, splits, axis=-1)
            y_a = gmlp_mixer(u_a, v_a, ev_v_norm[e], ev_w_spatial[e], ev_b_spatial[e])
            shp = (bsz, seq, B_HEADS, B_HEAD_DIM)
            y_b = fox_attention(q.reshape(shp), k.reshape(shp), v_b.reshape(shp),
                                f + ev_b_fgate[e], ev_q_norm[e], ev_k_norm[e])
            x = x + jnp.concatenate([y_a, y_b], axis=-1) @ ev_w_out[e]
        else:
            o = i // 2
            y = s5_mixer(h, od_a_re[o], od_a_im[o], od_log_dt[o], od_b_re[o], od_b_im[o],
                         od_c_re[o], od_c_im[o], od_d[o])
            g_a, g_b = jnp.split(jax.nn.gelu(y) @ od_w_glu[o], 2, axis=-1)
            x = x + g_a * jax.nn.sigmoid(g_b)
        x = x + conv_ffn(rms_norm(x, norm_ffn[i]), ffn_w_up[i], ffn_conv_w[i], ffn_conv_b[i], ffn_w_down[i])
        gate = jax.nn.sigmoid(rms_norm(x) @ ple_w_gate[i])
        x = x + gate * (p[i] @ ple_w_proj[i])
    return x


import jax as _jax
import jax.numpy as _jnp

TWIN_FORMAT = 'train_step'
FWD_PARAMS = ['x', 'p', 'norm_mix', 'norm_ffn', 'ev_w_in', 'ev_b_fgate', 'ev_q_norm', 'ev_k_norm', 'ev_v_norm', 'ev_w_spatial', 'ev_b_spatial', 'ev_w_out', 'od_a_re', 'od_a_im', 'od_log_dt', 'od_b_re', 'od_b_im', 'od_c_re', 'od_c_im', 'od_d', 'od_w_glu', 'ffn_w_up', 'ffn_conv_w', 'ffn_conv_b', 'ffn_w_down', 'ple_w_proj', 'ple_w_gate']
TWIN_WEIGHTS = ['norm_mix', 'norm_ffn', 'ev_w_in', 'ev_b_fgate', 'ev_q_norm', 'ev_k_norm', 'ev_v_norm', 'ev_w_spatial', 'ev_b_spatial', 'ev_w_out', 'od_a_re', 'od_a_im', 'od_log_dt', 'od_b_re', 'od_b_im', 'od_c_re', 'od_c_im', 'od_d', 'od_w_glu', 'ffn_w_up', 'ffn_conv_w', 'ffn_conv_b', 'ffn_w_down', 'ple_w_proj', 'ple_w_gate']
TWIN_DIFF_INPUT = 'x'
TWIN_INPUTS = ['x', 'p', 'norm_mix', 'norm_ffn', 'ev_w_in', 'ev_b_fgate', 'ev_q_norm', 'ev_k_norm', 'ev_v_norm', 'ev_w_spatial', 'ev_b_spatial', 'ev_w_out', 'od_a_re', 'od_a_im', 'od_log_dt', 'od_b_re', 'od_b_im', 'od_c_re', 'od_c_im', 'od_d', 'od_w_glu', 'ffn_w_up', 'ffn_conv_w', 'ffn_conv_b', 'ffn_w_down', 'ple_w_proj', 'ple_w_gate', 'loss_target', 'm_norm_mix', 'm_norm_ffn', 'm_ev_w_in', 'm_ev_b_fgate', 'm_ev_q_norm', 'm_ev_k_norm', 'm_ev_v_norm', 'm_ev_w_spatial', 'm_ev_b_spatial', 'm_ev_w_out', 'm_od_a_re', 'm_od_a_im', 'm_od_log_dt', 'm_od_b_re', 'm_od_b_im', 'm_od_c_re', 'm_od_c_im', 'm_od_d', 'm_od_w_glu', 'm_ffn_w_up', 'm_ffn_conv_w', 'm_ffn_conv_b', 'm_ffn_w_down', 'm_ple_w_proj', 'm_ple_w_gate', 'v_norm_mix', 'v_norm_ffn', 'v_ev_w_in', 'v_ev_b_fgate', 'v_ev_q_norm', 'v_ev_k_norm', 'v_ev_v_norm', 'v_ev_w_spatial', 'v_ev_b_spatial', 'v_ev_w_out', 'v_od_a_re', 'v_od_a_im', 'v_od_log_dt', 'v_od_b_re', 'v_od_b_im', 'v_od_c_re', 'v_od_c_im', 'v_od_d', 'v_od_w_glu', 'v_ffn_w_up', 'v_ffn_conv_w', 'v_ffn_conv_b', 'v_ffn_w_down', 'v_ple_w_proj', 'v_ple_w_gate']
TWIN_OUTPUTS = ['loss', 'grad_x', 'grad_norm_mix', 'grad_norm_ffn', 'grad_ev_w_in', 'grad_ev_b_fgate', 'grad_ev_q_norm', 'grad_ev_k_norm', 'grad_ev_v_norm', 'grad_ev_w_spatial', 'grad_ev_b_spatial', 'grad_ev_w_out', 'grad_od_a_re', 'grad_od_a_im', 'grad_od_log_dt', 'grad_od_b_re', 'grad_od_b_im', 'grad_od_c_re', 'grad_od_c_im', 'grad_od_d', 'grad_od_w_glu', 'grad_ffn_w_up', 'grad_ffn_conv_w', 'grad_ffn_conv_b', 'grad_ffn_w_down', 'grad_ple_w_proj', 'grad_ple_w_gate', 'delta_norm_mix', 'delta_norm_ffn', 'delta_ev_w_in', 'delta_ev_b_fgate', 'delta_ev_q_norm', 'delta_ev_k_norm', 'delta_ev_v_norm', 'delta_ev_w_spatial', 'delta_ev_b_spatial', 'delta_ev_w_out', 'delta_od_a_re', 'delta_od_a_im', 'delta_od_log_dt', 'delta_od_b_re', 'delta_od_b_im', 'delta_od_c_re', 'delta_od_c_im', 'delta_od_d', 'delta_od_w_glu', 'delta_ffn_w_up', 'delta_ffn_conv_w', 'delta_ffn_conv_b', 'delta_ffn_w_down', 'delta_ple_w_proj', 'delta_ple_w_gate', 'new_m_norm_mix', 'new_m_norm_ffn', 'new_m_ev_w_in', 'new_m_ev_b_fgate', 'new_m_ev_q_norm', 'new_m_ev_k_norm', 'new_m_ev_v_norm', 'new_m_ev_w_spatial', 'new_m_ev_b_spatial', 'new_m_ev_w_out', 'new_m_od_a_re', 'new_m_od_a_im', 'new_m_od_log_dt', 'new_m_od_b_re', 'new_m_od_b_im', 'new_m_od_c_re', 'new_m_od_c_im', 'new_m_od_d', 'new_m_od_w_glu', 'new_m_ffn_w_up', 'new_m_ffn_conv_w', 'new_m_ffn_conv_b', 'new_m_ffn_w_down', 'new_m_ple_w_proj', 'new_m_ple_w_gate', 'new_v_norm_mix', 'new_v_norm_ffn', 'new_v_ev_w_in', 'new_v_ev_b_fgate', 'new_v_ev_q_norm', 'new_v_ev_k_norm', 'new_v_ev_v_norm', 'new_v_ev_w_spatial', 'new_v_ev_b_spatial', 'new_v_ev_w_out', 'new_v_od_a_re', 'new_v_od_a_im', 'new_v_od_log_dt', 'new_v_od_b_re', 'new_v_od_b_im', 'new_v_od_c_re', 'new_v_od_c_im', 'new_v_od_d', 'new_v_od_w_glu', 'new_v_ffn_w_up', 'new_v_ffn_conv_w', 'new_v_ffn_conv_b', 'new_v_ffn_w_down', 'new_v_ple_w_proj', 'new_v_ple_w_gate']
TWIN_LEAF_KINDS = {'loss': 'loss', 'grad_x': 'grad_x', 'grad_norm_mix': 'grad_w', 'grad_norm_ffn': 'grad_w', 'grad_ev_w_in': 'grad_w', 'grad_ev_b_fgate': 'grad_w', 'grad_ev_q_norm': 'grad_w', 'grad_ev_k_norm': 'grad_w', 'grad_ev_v_norm': 'grad_w', 'grad_ev_w_spatial': 'grad_w', 'grad_ev_b_spatial': 'grad_w', 'grad_ev_w_out': 'grad_w', 'grad_od_a_re': 'grad_w', 'grad_od_a_im': 'grad_w', 'grad_od_log_dt': 'grad_w', 'grad_od_b_re': 'grad_w', 'grad_od_b_im': 'grad_w', 'grad_od_c_re': 'grad_w', 'grad_od_c_im': 'grad_w', 'grad_od_d': 'grad_w', 'grad_od_w_glu': 'grad_w', 'grad_ffn_w_up': 'grad_w', 'grad_ffn_conv_w': 'grad_w', 'grad_ffn_conv_b': 'grad_w', 'grad_ffn_w_down': 'grad_w', 'grad_ple_w_proj': 'grad_w', 'grad_ple_w_gate': 'grad_w', 'delta_norm_mix': 'delta_w', 'delta_norm_ffn': 'delta_w', 'delta_ev_w_in': 'delta_w', 'delta_ev_b_fgate': 'delta_w', 'delta_ev_q_norm': 'delta_w', 'delta_ev_k_norm': 'delta_w', 'delta_ev_v_norm': 'delta_w', 'delta_ev_w_spatial': 'delta_w', 'delta_ev_b_spatial': 'delta_w', 'delta_ev_w_out': 'delta_w', 'delta_od_a_re': 'delta_w', 'delta_od_a_im': 'delta_w', 'delta_od_log_dt': 'delta_w', 'delta_od_b_re': 'delta_w', 'delta_od_b_im': 'delta_w', 'delta_od_c_re': 'delta_w', 'delta_od_c_im': 'delta_w', 'delta_od_d': 'delta_w', 'delta_od_w_glu': 'delta_w', 'delta_ffn_w_up': 'delta_w', 'delta_ffn_conv_w': 'delta_w', 'delta_ffn_conv_b': 'delta_w', 'delta_ffn_w_down': 'delta_w', 'delta_ple_w_proj': 'delta_w', 'delta_ple_w_gate': 'delta_w', 'new_m_norm_mix': 'new_m', 'new_m_norm_ffn': 'new_m', 'new_m_ev_w_in': 'new_m', 'new_m_ev_b_fgate': 'new_m', 'new_m_ev_q_norm': 'new_m', 'new_m_ev_k_norm': 'new_m', 'new_m_ev_v_norm': 'new_m', 'new_m_ev_w_spatial': 'new_m', 'new_m_ev_b_spatial': 'new_m', 'new_m_ev_w_out': 'new_m', 'new_m_od_a_re': 'new_m', 'new_m_od_a_im': 'new_m', 'new_m_od_log_dt': 'new_m', 'new_m_od_b_re': 'new_m', 'new_m_od_b_im': 'new_m', 'new_m_od_c_re': 'new_m', 'new_m_od_c_im': 'new_m', 'new_m_od_d': 'new_m', 'new_m_od_w_glu': 'new_m', 'new_m_ffn_w_up': 'new_m', 'new_m_ffn_conv_w': 'new_m', 'new_m_ffn_conv_b': 'new_m', 'new_m_ffn_w_down': 'new_m', 'new_m_ple_w_proj': 'new_m', 'new_m_ple_w_gate': 'new_m', 'new_v_norm_mix': 'new_v', 'new_v_norm_ffn': 'new_v', 'new_v_ev_w_in': 'new_v', 'new_v_ev_b_fgate': 'new_v', 'new_v_ev_q_norm': 'new_v', 'new_v_ev_k_norm': 'new_v', 'new_v_ev_v_norm': 'new_v', 'new_v_ev_w_spatial': 'new_v', 'new_v_ev_b_spatial': 'new_v', 'new_v_ev_w_out': 'new_v', 'new_v_od_a_re': 'new_v', 'new_v_od_a_im': 'new_v', 'new_v_od_log_dt': 'new_v', 'new_v_od_b_re': 'new_v', 'new_v_od_b_im': 'new_v', 'new_v_od_c_re': 'new_v', 'new_v_od_c_im': 'new_v', 'new_v_od_d': 'new_v', 'new_v_od_w_glu': 'new_v', 'new_v_ffn_w_up': 'new_v', 'new_v_ffn_conv_w': 'new_v', 'new_v_ffn_conv_b': 'new_v', 'new_v_ffn_w_down': 'new_v', 'new_v_ple_w_proj': 'new_v', 'new_v_ple_w_gate': 'new_v'}


def _forward(args):
    return _fwd_reference(*[args[k] for k in FWD_PARAMS])


def _output_shape():
    def fwd():
        inp = _fwd_setup_inputs(0)
        return _fwd_reference(*[inp[k] for k in FWD_PARAMS])
    out = _jax.eval_shape(fwd)
    return out.shape, out.dtype

N_MICROBATCH = 1
ADAM_LR = 0.001
ADAM_B1 = 0.9
ADAM_B2 = 0.999
ADAM_EPS = 1e-08
ADAM_WD = 0.01
ADAM_STEP = 10
PER_EXAMPLE_BATCH_AXIS = {'x': 0, 'p': 1, 'loss_target': 0}
SHARED_INPUTS = []
_WEIGHT_DTYPES = {'norm_mix': _jnp.float32, 'norm_ffn': _jnp.float32, 'ev_w_in': _jnp.float32, 'ev_b_fgate': _jnp.float32, 'ev_q_norm': _jnp.float32, 'ev_k_norm': _jnp.float32, 'ev_v_norm': _jnp.float32, 'ev_w_spatial': _jnp.float32, 'ev_b_spatial': _jnp.float32, 'ev_w_out': _jnp.float32, 'od_a_re': _jnp.float32, 'od_a_im': _jnp.float32, 'od_log_dt': _jnp.float32, 'od_b_re': _jnp.float32, 'od_b_im': _jnp.float32, 'od_c_re': _jnp.float32, 'od_c_im': _jnp.float32, 'od_d': _jnp.float32, 'od_w_glu': _jnp.float32, 'ffn_w_up': _jnp.float32, 'ffn_conv_w': _jnp.float32, 'ffn_conv_b': _jnp.float32, 'ffn_w_down': _jnp.float32, 'ple_w_proj': _jnp.float32, 'ple_w_gate': _jnp.float32}
MOMENT_SCALE = {'norm_mix': 4.378157e+01, 'norm_ffn': 1.047608e+02, 'ev_w_in': 3.630204e+00, 'ev_b_fgate': 9.390007e+01, 'ev_q_norm': 1.669755e+01, 'ev_k_norm': 1.696420e+01, 'ev_v_norm': 2.798079e+01, 'ev_w_spatial': 1.755878e+01, 'ev_b_spatial': 5.809237e+01, 'ev_w_out': 1.700380e+01, 'od_a_re': 3.439019e-01, 'od_a_im': 4.010155e-01, 'od_log_dt': 2.871555e+01, 'od_b_re': 3.415967e-01, 'od_b_im': 3.343259e-01, 'od_c_re': 4.602548e-01, 'od_c_im': 4.427530e-01, 'od_d': 2.324269e+01, 'od_w_glu': 8.866751e+00, 'ffn_w_up': 2.978497e+00, 'ffn_conv_w': 1.499722e+01, 'ffn_conv_b': 1.556012e+01, 'ffn_w_down': 2.517454e+00, 'ple_w_proj': 1.443724e+00, 'ple_w_gate': 1.714383e+00}


def _to_microbatches(a, axis):
    t = _jnp.moveaxis(a, axis, 0)
    t = t.reshape((N_MICROBATCH, t.shape[0] // N_MICROBATCH) + t.shape[1:])
    return _jnp.moveaxis(t, 1, axis + 1)


def setup_inputs(seed: int = 0) -> dict:
    inp = _fwd_setup_inputs(seed)
    key = _jax.random.fold_in(_jax.random.key(seed), 7919)
    shape, _ = _output_shape()
    out = dict(inp)
    out["loss_target"] = _jax.random.normal(_jax.random.fold_in(key, 0), shape, _jnp.float32)
    for i, name in enumerate(TWIN_WEIGHTS):
        w = inp[name].astype(_jnp.float32)
        if MOMENT_SCALE is None:
            s = _jnp.sqrt(_jnp.mean(_jnp.square(w)) + 1e-30)
        else:
            s = MOMENT_SCALE[name]
        km, kv = _jax.random.split(_jax.random.fold_in(key, i + 1))
        out[name] = w
        out["m_" + name] = s * _jax.random.normal(km, w.shape, _jnp.float32)
        out["v_" + name] = (s * s) * _jax.random.uniform(kv, w.shape, _jnp.float32, 0.5, 1.5)
    if N_MICROBATCH > 1:
        for name, axis in PER_EXAMPLE_BATCH_AXIS.items():
            out[name] = _to_microbatches(out[name], axis)
    return {'x': out['x'], 'p': out['p'], 'norm_mix': out['norm_mix'], 'norm_ffn': out['norm_ffn'], 'ev_w_in': out['ev_w_in'], 'ev_b_fgate': out['ev_b_fgate'], 'ev_q_norm': out['ev_q_norm'], 'ev_k_norm': out['ev_k_norm'], 'ev_v_norm': out['ev_v_norm'], 'ev_w_spatial': out['ev_w_spatial'], 'ev_b_spatial': out['ev_b_spatial'], 'ev_w_out': out['ev_w_out'], 'od_a_re': out['od_a_re'], 'od_a_im': out['od_a_im'], 'od_log_dt': out['od_log_dt'], 'od_b_re': out['od_b_re'], 'od_b_im': out['od_b_im'], 'od_c_re': out['od_c_re'], 'od_c_im': out['od_c_im'], 'od_d': out['od_d'], 'od_w_glu': out['od_w_glu'], 'ffn_w_up': out['ffn_w_up'], 'ffn_conv_w': out['ffn_conv_w'], 'ffn_conv_b': out['ffn_conv_b'], 'ffn_w_down': out['ffn_w_down'], 'ple_w_proj': out['ple_w_proj'], 'ple_w_gate': out['ple_w_gate'], 'loss_target': out['loss_target'], 'm_norm_mix': out['m_norm_mix'], 'm_norm_ffn': out['m_norm_ffn'], 'm_ev_w_in': out['m_ev_w_in'], 'm_ev_b_fgate': out['m_ev_b_fgate'], 'm_ev_q_norm': out['m_ev_q_norm'], 'm_ev_k_norm': out['m_ev_k_norm'], 'm_ev_v_norm': out['m_ev_v_norm'], 'm_ev_w_spatial': out['m_ev_w_spatial'], 'm_ev_b_spatial': out['m_ev_b_spatial'], 'm_ev_w_out': out['m_ev_w_out'], 'm_od_a_re': out['m_od_a_re'], 'm_od_a_im': out['m_od_a_im'], 'm_od_log_dt': out['m_od_log_dt'], 'm_od_b_re': out['m_od_b_re'], 'm_od_b_im': out['m_od_b_im'], 'm_od_c_re': out['m_od_c_re'], 'm_od_c_im': out['m_od_c_im'], 'm_od_d': out['m_od_d'], 'm_od_w_glu': out['m_od_w_glu'], 'm_ffn_w_up': out['m_ffn_w_up'], 'm_ffn_conv_w': out['m_ffn_conv_w'], 'm_ffn_conv_b': out['m_ffn_conv_b'], 'm_ffn_w_down': out['m_ffn_w_down'], 'm_ple_w_proj': out['m_ple_w_proj'], 'm_ple_w_gate': out['m_ple_w_gate'], 'v_norm_mix': out['v_norm_mix'], 'v_norm_ffn': out['v_norm_ffn'], 'v_ev_w_in': out['v_ev_w_in'], 'v_ev_b_fgate': out['v_ev_b_fgate'], 'v_ev_q_norm': out['v_ev_q_norm'], 'v_ev_k_norm': out['v_ev_k_norm'], 'v_ev_v_norm': out['v_ev_v_norm'], 'v_ev_w_spatial': out['v_ev_w_spatial'], 'v_ev_b_spatial': out['v_ev_b_spatial'], 'v_ev_w_out': out['v_ev_w_out'], 'v_od_a_re': out['v_od_a_re'], 'v_od_a_im': out['v_od_a_im'], 'v_od_log_dt': out['v_od_log_dt'], 'v_od_b_re': out['v_od_b_re'], 'v_od_b_im': out['v_od_b_im'], 'v_od_c_re': out['v_od_c_re'], 'v_od_c_im': out['v_od_c_im'], 'v_od_d': out['v_od_d'], 'v_od_w_glu': out['v_od_w_glu'], 'v_ffn_w_up': out['v_ffn_w_up'], 'v_ffn_conv_w': out['v_ffn_conv_w'], 'v_ffn_conv_b': out['v_ffn_conv_b'], 'v_ffn_w_down': out['v_ffn_w_down'], 'v_ple_w_proj': out['v_ple_w_proj'], 'v_ple_w_gate': out['v_ple_w_gate']}


def _loss(weights, diff, rest, loss_target):
    with _jax.named_scope("forward"):
        args = {**rest, TWIN_DIFF_INPUT: diff, **{k: w.astype(_WEIGHT_DTYPES[k]) for k, w in weights.items()}}
        y = _forward(args)
    with _jax.named_scope("loss_head"):
        err = _jnp.square(y.astype(_jnp.float32) - loss_target)
        return 0.5 * _jnp.sum(_jnp.mean(err, axis=-1)) if err.ndim else 0.5 * err


def _adamw(w, g, m, v):
    m = ADAM_B1 * m + (1.0 - ADAM_B1) * g
    v = ADAM_B2 * v + (1.0 - ADAM_B2) * _jnp.square(g)
    m_hat = m / (1.0 - ADAM_B1 ** ADAM_STEP)
    v_hat = v / (1.0 - ADAM_B2 ** ADAM_STEP)
    delta = -ADAM_LR * (m_hat / (_jnp.sqrt(v_hat) + ADAM_EPS) + ADAM_WD * w)
    return delta, m, v


def reference(x, p, norm_mix, norm_ffn, ev_w_in, ev_b_fgate, ev_q_norm, ev_k_norm, ev_v_norm, ev_w_spatial, ev_b_spatial, ev_w_out, od_a_re, od_a_im, od_log_dt, od_b_re, od_b_im, od_c_re, od_c_im, od_d, od_w_glu, ffn_w_up, ffn_conv_w, ffn_conv_b, ffn_w_down, ple_w_proj, ple_w_gate, loss_target, m_norm_mix, m_norm_ffn, m_ev_w_in, m_ev_b_fgate, m_ev_q_norm, m_ev_k_norm, m_ev_v_norm, m_ev_w_spatial, m_ev_b_spatial, m_ev_w_out, m_od_a_re, m_od_a_im, m_od_log_dt, m_od_b_re, m_od_b_im, m_od_c_re, m_od_c_im, m_od_d, m_od_w_glu, m_ffn_w_up, m_ffn_conv_w, m_ffn_conv_b, m_ffn_w_down, m_ple_w_proj, m_ple_w_gate, v_norm_mix, v_norm_ffn, v_ev_w_in, v_ev_b_fgate, v_ev_q_norm, v_ev_k_norm, v_ev_v_norm, v_ev_w_spatial, v_ev_b_spatial, v_ev_w_out, v_od_a_re, v_od_a_im, v_od_log_dt, v_od_b_re, v_od_b_im, v_od_c_re, v_od_c_im, v_od_d, v_od_w_glu, v_ffn_w_up, v_ffn_conv_w, v_ffn_conv_b, v_ffn_w_down, v_ple_w_proj, v_ple_w_gate):
    given = dict(x=x, p=p, norm_mix=norm_mix, norm_ffn=norm_ffn, ev_w_in=ev_w_in, ev_b_fgate=ev_b_fgate, ev_q_norm=ev_q_norm, ev_k_norm=ev_k_norm, ev_v_norm=ev_v_norm, ev_w_spatial=ev_w_spatial, ev_b_spatial=ev_b_spatial, ev_w_out=ev_w_out, od_a_re=od_a_re, od_a_im=od_a_im, od_log_dt=od_log_dt, od_b_re=od_b_re, od_b_im=od_b_im, od_c_re=od_c_re, od_c_im=od_c_im, od_d=od_d, od_w_glu=od_w_glu, ffn_w_up=ffn_w_up, ffn_conv_w=ffn_conv_w, ffn_conv_b=ffn_conv_b, ffn_w_down=ffn_w_down, ple_w_proj=ple_w_proj, ple_w_gate=ple_w_gate, loss_target=loss_target, m_norm_mix=m_norm_mix, m_norm_ffn=m_norm_ffn, m_ev_w_in=m_ev_w_in, m_ev_b_fgate=m_ev_b_fgate, m_ev_q_norm=m_ev_q_norm, m_ev_k_norm=m_ev_k_norm, m_ev_v_norm=m_ev_v_norm, m_ev_w_spatial=m_ev_w_spatial, m_ev_b_spatial=m_ev_b_spatial, m_ev_w_out=m_ev_w_out, m_od_a_re=m_od_a_re, m_od_a_im=m_od_a_im, m_od_log_dt=m_od_log_dt, m_od_b_re=m_od_b_re, m_od_b_im=m_od_b_im, m_od_c_re=m_od_c_re, m_od_c_im=m_od_c_im, m_od_d=m_od_d, m_od_w_glu=m_od_w_glu, m_ffn_w_up=m_ffn_w_up, m_ffn_conv_w=m_ffn_conv_w, m_ffn_conv_b=m_ffn_conv_b, m_ffn_w_down=m_ffn_w_down, m_ple_w_proj=m_ple_w_proj, m_ple_w_gate=m_ple_w_gate, v_norm_mix=v_norm_mix, v_norm_ffn=v_norm_ffn, v_ev_w_in=v_ev_w_in, v_ev_b_fgate=v_ev_b_fgate, v_ev_q_norm=v_ev_q_norm, v_ev_k_norm=v_ev_k_norm, v_ev_v_norm=v_ev_v_norm, v_ev_w_spatial=v_ev_w_spatial, v_ev_b_spatial=v_ev_b_spatial, v_ev_w_out=v_ev_w_out, v_od_a_re=v_od_a_re, v_od_a_im=v_od_a_im, v_od_log_dt=v_od_log_dt, v_od_b_re=v_od_b_re, v_od_b_im=v_od_b_im, v_od_c_re=v_od_c_re, v_od_c_im=v_od_c_im, v_od_d=v_od_d, v_od_w_glu=v_od_w_glu, v_ffn_w_up=v_ffn_w_up, v_ffn_conv_w=v_ffn_conv_w, v_ffn_conv_b=v_ffn_conv_b, v_ffn_w_down=v_ffn_w_down, v_ple_w_proj=v_ple_w_proj, v_ple_w_gate=v_ple_w_gate)
    weights = {n: given[n] for n in TWIN_WEIGHTS}
    shared = {n: given[n] for n in SHARED_INPUTS}
    per_example = {n: given[n] for n in ['x', 'p']}
    grad_fn = _jax.value_and_grad(_loss, argnums=(0, 1))

    def one_microbatch(ex, loss_target):
        ex = dict(ex)
        diff = ex.pop(TWIN_DIFF_INPUT)
        return grad_fn(weights, diff, {**shared, **ex}, loss_target)

    if N_MICROBATCH == 1:
        loss, (grad_w, grad_x) = one_microbatch(per_example, given["loss_target"])
    else:
        def body(carry, xs):
            loss_sum, grad_sum = carry
            l_k, (gw_k, gx_k) = one_microbatch(xs[0], xs[1])
            with _jax.named_scope("update"):
                return (loss_sum + l_k, _jax.tree.map(_jnp.add, grad_sum, gw_k)), gx_k

        init = (_jnp.zeros((), _jnp.float32), _jax.tree.map(_jnp.zeros_like, weights))
        (loss, grad_w), grad_x = _jax.lax.scan(body, init, (per_example, given["loss_target"]))
    with _jax.named_scope("update"):
        delta_w, new_m, new_v = {}, {}, {}
        for n in TWIN_WEIGHTS:
            delta_w[n], new_m[n], new_v[n] = _adamw(weights[n], grad_w[n], given["m_" + n], given["v_" + n])
    return (loss, grad_x, *[grad_w[n] for n in TWIN_WEIGHTS], *[delta_w[n] for n in TWIN_WEIGHTS],
            *[new_m[n] for n in TWIN_WEIGHTS], *[new_v[n] for n in TWIN_WEIGHTS])
```

```python
import functools
import math

import jax
import jax.numpy as jnp
from jax import lax
from jax.experimental import pallas as pl
from jax.experimental.pallas import tpu as pltpu

F32 = jnp.float32
BF16 = jnp.bfloat16

D_MODEL = 1024
DEPTH = 4
A_GROUPS = 4
A_CHUNK = 128
A_WIDTH = 512
B_HEADS = 8
B_HEAD_DIM = 64
B_WIDTH = 512
IN_COLS = 2 * A_WIDTH + 3 * B_WIDTH + B_HEADS
IN_COLS_PAD = 2688
S5_GROUP_CH = 16
S5_GROUPS = 64
S5_STATE = 64
S5_N = S5_GROUPS * S5_STATE
S5_BLOCKS = 8
D_FF = 2816
PLE_DIM = 256
EPS = 1e-6
NEG_INF = -1e30
N_DEV = 8

ADAM_LR = 0.001
ADAM_B1 = 0.9
ADAM_B2 = 0.999
ADAM_EPS = 1e-08
ADAM_WD = 0.01
ADAM_STEP = 10

VMEM_LIMIT_BYTES = 56 * 1024 * 1024
MESH = pl.DeviceIdType.MESH


def _params(sem, vmem=VMEM_LIMIT_BYTES):
    return pltpu.CompilerParams(dimension_semantics=sem, vmem_limit_bytes=vmem)


_GELU_K = math.sqrt(2.0 / math.pi)


def _gelu(x):
    return x * (0.5 * (1.0 + jnp.tanh(_GELU_K * (x + 0.044715 * (x * x * x)))))


def _gelu_grad(x):
    t = jnp.tanh(_GELU_K * (x + 0.044715 * (x * x * x)))
    return 0.5 * (1.0 + t) + 0.5 * x * (1.0 - t * t) * (_GELU_K * (1.0 + 3.0 * 0.044715 * (x * x)))


def _sigmoid(x):
    return 1.0 / (1.0 + jnp.exp(-x))


def _rstd(x):
    return lax.rsqrt(jnp.mean(x * x, axis=-1, keepdims=True) + EPS)


def _rms_bwd(x, dy, g):
    r = _rstd(x)
    xh = x * r
    dyg = dy if g is None else dy * g
    dx = r * (dyg - xh * jnp.mean(dyg * xh, axis=-1, keepdims=True))
    return dx, dy * xh


def _shift_down(blk, halo, k):
    tr = blk.shape[0]
    r = pltpu.roll(blk, k, 0)
    hr = pltpu.roll(halo, k, 0)
    first = jnp.where(lax.broadcasted_iota(jnp.int32, hr.shape, 0) < k, hr, r[0:8])
    return jnp.concatenate([first, r[8:tr]], axis=0)


def _shift_up(blk, halo, k):
    tr = blk.shape[0]
    r = pltpu.roll(blk, tr - k, 0)
    hr = pltpu.roll(halo, 8 - k, 0)
    last = jnp.where(lax.broadcasted_iota(jnp.int32, hr.shape, 0) >= 8 - k, hr, r[tr - 8:tr])
    return jnp.concatenate([r[0:tr - 8], last], axis=0)


def _rw(fn, rows, consts, outs, accs=(), *, tr, name, prev=(), nxt=(), widths=None):
    s = rows[0].shape[0]
    tr = min(tr, s)
    n = s // tr
    nr, nc, npv, nnx, no, na = len(rows), len(consts), len(prev), len(nxt), len(outs), len(accs)
    widths = widths or [None] * nr

    def body(*refs):
        ins, out_refs = refs[:nr + nc + npv + nnx], refs[nr + nc + npv + nnx:]
        i = pl.program_id(0)
        vals = [r[...] for r in ins[:nr + nc]]
        vals += [jnp.where(i == 0, 0.0, r[...]) for r in ins[nr + nc:nr + nc + npv]]
        vals += [jnp.where(i == n - 1, 0.0, r[...]) for r in ins[nr + nc + npv:]]
        res = fn(*vals)
        if not isinstance(res, (tuple, list)):
            res = (res,)
        for k in range(no):
            out_refs[k][...] = res[k].astype(out_refs[k].dtype)
        if na:
            @pl.when(i == 0)
            def _():
                for k in range(na):
                    out_refs[no + k][...] = jnp.zeros_like(out_refs[no + k])

            for k in range(na):
                out_refs[no + k][...] += res[no + k]

    in_specs = []
    for a, w in zip(rows, widths):
        if w is None:
            in_specs.append(pl.BlockSpec((tr, a.shape[1]), lambda i: (i, 0)))
        else:
            in_specs.append(pl.BlockSpec((tr, w[1]), functools.partial(lambda i, cb: (i, cb), cb=w[0])))
    for c in consts:
        in_specs.append(pl.BlockSpec(c.shape, functools.partial(lambda i, nd: (0,) * nd, nd=c.ndim)))
    t8 = tr // 8
    for a in prev:
        in_specs.append(pl.BlockSpec((8, a.shape[1]), lambda i: (jnp.maximum(i * t8 - 1, 0), 0)))
    for a in nxt:
        in_specs.append(pl.BlockSpec((8, a.shape[1]), lambda i: (jnp.minimum((i + 1) * t8, s // 8 - 1), 0)))
    out_shape = [jax.ShapeDtypeStruct((s, w), dt) for w, dt in outs]
    out_specs = [pl.BlockSpec((tr, w), lambda i: (i, 0)) for w, _ in outs]
    out_shape += [jax.ShapeDtypeStruct(a, F32) for a in accs]
    out_specs += [pl.BlockSpec(a, lambda i: (0, 0)) for a in accs]
    res = pl.pallas_call(
        body, grid=(n,), in_specs=in_specs, out_specs=out_specs, out_shape=out_shape, name=name,
        compiler_params=_params(("arbitrary",) if na else ("parallel",)),
    )(*rows, *consts, *prev, *nxt)
    return res


def _pick(n, cap):
    if n <= cap:
        return n
    best = None
    for d in range(128, cap + 1, 128):
        if n % d == 0:
            best = d
    assert best is not None, (n, cap)
    return best


def _mm(a, b, *, ta=False, tb=False, out_dtype=F32, add=None, name, bm=1024, bn=512, bk=1024):
    (k_dim, m) = a.shape if ta else a.shape[::-1]
    n = b.shape[0] if tb else b.shape[1]
    assert (b.shape[1] if tb else b.shape[0]) == k_dim
    bm, bn, bk = _pick(m, bm), _pick(n, bn), _pick(k_dim, bk)
    nk = k_dim // bk
    dims = (((0 if ta else 1,), (1 if tb else 0,)), ((), ()))
    has_add = add is not None

    def body(*refs):
        a_ref, b_ref = refs[0], refs[1]
        add_ref = refs[2] if has_add else None
        o_ref = refs[2 + has_add]
        p = lax.dot_general(a_ref[...].astype(BF16), b_ref[...].astype(BF16), dims, preferred_element_type=F32)

        def finish(acc):
            if has_add:
                acc = acc + add_ref[...]
            o_ref[...] = acc.astype(o_ref.dtype)

        if nk == 1:
            finish(p)
        else:
            acc_ref = refs[3 + has_add]
            kk = pl.program_id(2)

            @pl.when(kk == 0)
            def _():
                acc_ref[...] = p

            @pl.when(kk > 0)
            def _():
                acc_ref[...] += p

            @pl.when(kk == nk - 1)
            def _():
                finish(acc_ref[...])

    a_spec = pl.BlockSpec((bk, bm), lambda i, j, k: (k, i)) if ta else pl.BlockSpec((bm, bk), lambda i, j, k: (i, k))
    b_spec = pl.BlockSpec((bn, bk), lambda i, j, k: (j, k)) if tb else pl.BlockSpec((bk, bn), lambda i, j, k: (k, j))
    in_specs = [a_spec, b_spec]
    args = [a, b]
    if has_add:
        in_specs.append(pl.BlockSpec((bm, bn), lambda i, j, k: (i, j)))
        args.append(add)
    return pl.pallas_call(
        body, grid=(m // bm, n // bn, nk), in_specs=in_specs,
        out_specs=pl.BlockSpec((bm, bn), lambda i, j, k: (i, j)),
        out_shape=jax.ShapeDtypeStruct((m, n), out_dtype),
        scratch_shapes=[pltpu.VMEM((bm, bn), F32)] if nk > 1 else [],
        name=name, compiler_params=_params(("parallel", "parallel", "arbitrary")),
    )(*args)


def _bdmm(a, w, *, add=None, name, bm=1024):
    s = a.shape[0]
    nb, ka, nw = w.shape
    bm = min(bm, s)
    has_add = add is not None

    def body(*refs):
        a_ref, w_ref = refs[0], refs[1]
        o_ref = refs[2 + has_add]
        p = jnp.dot(a_ref[...].astype(BF16), w_ref[0].astype(BF16), preferred_element_type=F32)
        if has_add:
            p = p + refs[2][...]
        o_ref[...] = p

    in_specs = [pl.BlockSpec((bm, ka), lambda i, j: (i, j)), pl.BlockSpec((1, ka, nw), lambda i, j: (j, 0, 0))]
    args = [a, w]
    if has_add:
        in_specs.append(pl.BlockSpec((bm, nw), lambda i, j: (i, j)))
        args.append(add)
    return pl.pallas_call(
        body, grid=(s // bm, nb), in_specs=in_specs, out_specs=pl.BlockSpec((bm, nw), lambda i, j: (i, j)),
        out_shape=jax.ShapeDtypeStruct((s, nb * nw), F32), name=name, compiler_params=_params(("parallel", "parallel")),
    )(*args)


def _bdwg(a, b, ka, nw, *, name, bk=1024):
    s = a.shape[0]
    nb = a.shape[1] // ka
    bk = min(bk, s)

    def body(a_ref, b_ref, o_ref):
        p = lax.dot_general(a_ref[...].astype(BF16), b_ref[...].astype(BF16), (((0,), (0,)), ((), ())),
                            preferred_element_type=F32)

        @pl.when(pl.program_id(1) == 0)
        def _():
            o_ref[0] = p

        @pl.when(pl.program_id(1) > 0)
        def _():
            o_ref[0] += p

    return pl.pallas_call(
        body, grid=(nb, s // bk),
        in_specs=[pl.BlockSpec((bk, ka), lambda j, k: (k, j)), pl.BlockSpec((bk, nw), lambda j, k: (k, j))],
        out_specs=pl.BlockSpec((1, ka, nw), lambda j, k: (j, 0, 0)),
        out_shape=jax.ShapeDtypeStruct((nb, ka, nw), F32), name=name, compiler_params=_params(("parallel", "arbitrary")),
    )(a, b)


def _rms_fwd(x, g, *, name):
    if g is None:
        return _rw(lambda xv: xv * _rstd(xv), [x], [], [(x.shape[1], BF16)], tr=512, name=name)[0]
    return _rw(lambda xv, gv: xv * _rstd(xv) * gv, [x], [g], [(x.shape[1], BF16)], tr=512, name=name)[0]


def _rms_bwd_call(x, dy, g, dx_add, *, name, tr=512, scale=None):
    w = x.shape[1]
    has_g, has_add = g is not None, dx_add is not None

    def fn(*v):
        xv, dyv = v[0], v[1]
        if scale is not None:
            dyv = dyv * scale
        gv = v[2 + has_add] if has_g else None
        dx, dg = _rms_bwd(xv, dyv, gv)
        if has_add:
            dx = dx + v[2]
        return (dx, jnp.sum(dg, axis=0, keepdims=True)) if has_g else (dx,)

    rows = [x, dy] + ([dx_add] if has_add else [])
    res = _rw(fn, rows, [g] if has_g else [], [(w, F32)], [(1, w)] if has_g else [], tr=tr, name=name)
    return (res[0], res[1]) if has_g else (res[0], None)


def _conv_taps(hup, halo, cw):
    h1 = _shift_down(hup, halo, 1)
    h2 = _shift_down(hup, halo, 2)
    return h1, h2


def _ffn_fwd(x1, w, i):
    h2b = _rms_fwd(x1, w["norm_ffn"][i:i + 1], name="ffn_norm")
    hup = _mm(h2b, w["ffn_w_up"][i], name="ffn_up")
    cw, cb = w["ffn_conv_w"][i], w["ffn_conv_b"][i:i + 1]

    def fn(hv, cwv, cbv, halo):
        h1, h2 = _conv_taps(hv, halo, cwv)
        hc = cbv + cwv[0:1] * h2
        hc = hc + cwv[1:2] * h1
        hc = hc + cwv[2:3] * hv
        g, up = hc[:, :D_FF], hc[:, D_FF:]
        return (g * _sigmoid(g)) * up

    a = _rw(fn, [hup], [cw, cb], [(D_FF, BF16)], tr=128, name="ffn_conv_gate", prev=[hup])[0]
    x2 = _mm(a, w["ffn_w_down"][i], add=x1, name="ffn_down")
    return x2, (h2b, hup, a)


def _ffn_bwd(dx2, x1, saved, w, i):
    h2b, hup, a = saved
    cw, cb = w["ffn_conv_w"][i], w["ffn_conv_b"][i:i + 1]
    da = _mm(dx2, w["ffn_w_down"][i], tb=True, name="ffn_down_dx")
    d_wdown = _mm(a, dx2, ta=True, name="ffn_down_dw")

    def fn1(hv, dav, cwv, cbv, halo):
        h1, h2 = _conv_taps(hv, halo, cwv)
        hc = cbv + cwv[0:1] * h2
        hc = hc + cwv[1:2] * h1
        hc = hc + cwv[2:3] * hv
        g, up = hc[:, :D_FF], hc[:, D_FF:]
        sg = _sigmoid(g)
        d_up = dav * (g * sg)
        d_g = dav * up * (sg * (1.0 + g * (1.0 - sg)))
        d_hc = jnp.concatenate([d_g, d_up], axis=1)
        col = lambda v: jnp.sum(v, axis=0, keepdims=True)
        return d_hc, col(d_hc), col(d_hc * h2), col(d_hc * h1), col(d_hc * hv)

    w2 = 2 * D_FF
    d_hc, d_cb, d_cw0, d_cw1, d_cw2 = _rw(fn1, [hup, da], [cw, cb], [(w2, F32)], [(1, w2)] * 4, tr=128,
                                          name="ffn_conv_gate_bwd", prev=[hup])

    def fn2(dv, cwv, halo):
        return cwv[2:3] * dv + cwv[1:2] * _shift_up(dv, halo, 1) + cwv[0:1] * _shift_up(dv, halo, 2)

    d_hup = _rw(fn2, [d_hc], [cw], [(w2, BF16)], tr=128, name="ffn_conv_bwd", nxt=[d_hc])[0]
    d_h2 = _mm(d_hup, w["ffn_w_up"][i], tb=True, name="ffn_up_dx")
    d_wup = _mm(h2b, d_hup, ta=True, name="ffn_up_dw")
    dx1, d_norm = _rms_bwd_call(x1, d_h2, w["norm_ffn"][i:i + 1], dx2, name="ffn_norm_bwd")
    grads = {"ffn_w_up": d_wup, "ffn_w_down": d_wdown, "ffn_conv_b": d_cb[0],
             "ffn_conv_w": jnp.concatenate([d_cw0, d_cw1, d_cw2], axis=0), "norm_ffn": d_norm[0]}
    return dx1, grads


def _ple_fwd(x2, p_i, w, i):
    rn = _rms_fwd(x2, None, name="ple_norm")
    gl = _mm(rn, w["ple_w_gate"][i], name="ple_gate")
    pp = _mm(p_i, w["ple_w_proj"][i], name="ple_proj")
    x3 = _rw(lambda xv, g, q: xv + _sigmoid(g) * q, [x2, gl, pp], [], [(D_MODEL, F32)], tr=512, name="ple_out")[0]
    return x3, (rn, gl, pp)


def _ple_bwd(dx3, x2, p_i, saved, w, i):
    rn, gl, pp = saved

    def fn(dv, g, q):
        sg = _sigmoid(g)
        return dv * sg, dv * q * (sg * (1.0 - sg))

    d_pp, d_pre = _rw(fn, [dx3, gl, pp], [], [(D_MODEL, BF16)] * 2, tr=512, name="ple_out_bwd")
    d_wproj = _mm(p_i, d_pp, ta=True, name="ple_proj_dw")
    d_wgate = _mm(rn, d_pre, ta=True, name="ple_gate_dw")
    d_rn = _mm(d_pre, w["ple_w_gate"][i], tb=True, name="ple_gate_dx")
    dx2, _ = _rms_bwd_call(x2, d_rn, None, dx3, name="ple_norm_bwd")
    return dx2, {"ple_w_proj": d_wproj, "ple_w_gate": d_wgate}


def _loss_and_grad(xf, tgt):
    def fn(xv, tv):
        d = xv - tv
        part = 0.5 * jnp.sum(jnp.mean(d * d, axis=-1, keepdims=True), axis=0, keepdims=True)
        return d * (1.0 / D_MODEL), part

    dx, loss = _rw(fn, [xf, tgt], [], [(D_MODEL, F32)], [(1, 1)], tr=512, name="loss_head")
    return loss[0, 0], dx


def _tril_mask():
    shp = (A_CHUNK, A_CHUNK)
    return lax.broadcasted_iota(jnp.int32, shp, 0) >= lax.broadcasted_iota(jnp.int32, shp, 1)


def _gmlp_fwd(z, w_s, b_s, v_gain, *, tr=512):
    s = z.shape[0]
    tr = min(tr, s)
    gw = A_WIDTH // A_GROUPS

    def body(z_ref, w_ref, b_ref, g_ref, y_ref):
        tril = _tril_mask()
        for g in range(A_GROUPS):
            wg = jnp.where(tril, w_ref[g], 0.0).astype(BF16)
            cols = slice(g * gw, (g + 1) * gw)
            vcols = slice(A_WIDTH + g * gw, A_WIDTH + (g + 1) * gw)
            for c in range(tr // A_CHUNK):
                rows = slice(c * A_CHUNK, (c + 1) * A_CHUNK)
                v = _gelu(z_ref[rows, vcols])
                vn = v * _rstd(v) * g_ref[:, cols]
                sv = jnp.dot(wg, vn.astype(BF16), preferred_element_type=F32) + b_ref[g]
                y_ref[rows, cols] = (_gelu(z_ref[rows, cols]) * sv).astype(BF16)

    return pl.pallas_call(
        body, grid=(s // tr,),
        in_specs=[pl.BlockSpec((tr, 2 * A_WIDTH), lambda i: (i, 0)),
                  pl.BlockSpec(w_s.shape, lambda i: (0, 0, 0)), pl.BlockSpec(b_s.shape, lambda i: (0, 0, 0)),
                  pl.BlockSpec(v_gain.shape, lambda i: (0, 0))],
        out_specs=pl.BlockSpec((tr, A_WIDTH), lambda i: (i, 0)),
        out_shape=jax.ShapeDtypeStruct((s, A_WIDTH), BF16), name="gmlp_fwd", compiler_params=_params(("parallel",)),
    )(z, w_s, b_s, v_gain)


def _gmlp_bwd(z, d_ymix, w_s, b_s, v_gain, *, tr=512):
    s = z.shape[0]
    tr = min(tr, s)
    gw = A_WIDTH // A_GROUPS

    def body(z_ref, dy_ref, w_ref, b_ref, g_ref, dz_ref, dw_ref, db_ref, dg_ref):
        @pl.when(pl.program_id(0) == 0)
        def _():
            dw_ref[...] = jnp.zeros_like(dw_ref)
            db_ref[...] = jnp.zeros_like(db_ref)
            dg_ref[...] = jnp.zeros_like(dg_ref)

        tril = _tril_mask()
        for g in range(A_GROUPS):
            wg = jnp.where(tril, w_ref[g], 0.0).astype(BF16)
            cols = slice(g * gw, (g + 1) * gw)
            vcols = slice(A_WIDTH + g * gw, A_WIDTH + (g + 1) * gw)
            gain = g_ref[:, cols]
            for c in range(tr // A_CHUNK):
                rows = slice(c * A_CHUNK, (c + 1) * A_CHUNK)
                va, ua = z_ref[rows, vcols], z_ref[rows, cols]
                v = _gelu(va)
                r = _rstd(v)
                vh = v * r
                vnb = (vh * gain).astype(BF16)
                sv = jnp.dot(wg, vnb, preferred_element_type=F32) + b_ref[g]
                dy = dy_ref[rows, cols]
                d_sv = dy * _gelu(ua)
                dz_ref[rows, cols] = (dy * sv * _gelu_grad(ua)).astype(BF16)
                d_svb = d_sv.astype(BF16)
                d_vn = lax.dot_general(wg, d_svb, (((0,), (0,)), ((), ())), preferred_element_type=F32)
                dwp = lax.dot_general(d_svb, vnb, (((1,), (1,)), ((), ())), preferred_element_type=F32)
                dw_ref[g] += jnp.where(tril, dwp, 0.0)
                db_ref[g] += jnp.sum(d_sv, axis=1, keepdims=True)
                dg_ref[:, cols] += jnp.sum(d_vn * vh, axis=0, keepdims=True)
                d_vh = d_vn * gain
                d_v = r * (d_vh - vh * jnp.mean(d_vh * vh, axis=-1, keepdims=True))
                dz_ref[rows, vcols] = (d_v * _gelu_grad(va)).astype(BF16)

    return pl.pallas_call(
        body, grid=(s // tr,),
        in_specs=[pl.BlockSpec((tr, 2 * A_WIDTH), lambda i: (i, 0)), pl.BlockSpec((tr, A_WIDTH), lambda i: (i, 0)),
                  pl.BlockSpec(w_s.shape, lambda i: (0, 0, 0)), pl.BlockSpec(b_s.shape, lambda i: (0, 0, 0)),
                  pl.BlockSpec(v_gain.shape, lambda i: (0, 0))],
        out_specs=[pl.BlockSpec((tr, 2 * A_WIDTH), lambda i: (i, 0)), pl.BlockSpec(w_s.shape, lambda i: (0, 0, 0)),
                   pl.BlockSpec(b_s.shape, lambda i: (0, 0, 0)), pl.BlockSpec(v_gain.shape, lambda i: (0, 0))],
        out_shape=[jax.ShapeDtypeStruct((s, 2 * A_WIDTH), BF16), jax.ShapeDtypeStruct(w_s.shape, F32),
                   jax.ShapeDtypeStruct(b_s.shape, F32), jax.ShapeDtypeStruct(v_gain.shape, F32)],
        name="gmlp_bwd", compiler_params=_params(("arbitrary",)),
    )(z, d_ymix, w_s, b_s, v_gain)


def _dot3(x, ub):
    x1 = x.astype(BF16)
    r1 = x - x1.astype(F32)
    x2 = r1.astype(BF16)
    x3 = (r1 - x2.astype(F32)).astype(BF16)
    d = lambda a: jnp.dot(a, ub, preferred_element_type=F32)
    return d(x1) + d(x2) + d(x3)


def _log_sigmoid(x):
    return jnp.minimum(x, 0.0) - jnp.log(1.0 + jnp.exp(-jnp.abs(x)))


def _fgate_fwd(f_t, b_col, *, tb=256):
    h, s = f_t.shape
    tb = min(tb, s)

    def body(f_ref, b_ref, c_ref, carry):
        @pl.when(pl.program_id(0) == 0)
        def _():
            carry[...] = jnp.zeros_like(carry)

        lf = _log_sigmoid(f_ref[...] + b_ref[...])
        shp = (tb, tb)
        upper = (lax.broadcasted_iota(jnp.int32, shp, 0) <= lax.broadcasted_iota(jnp.int32, shp, 1)).astype(BF16)
        c_ref[...] = _dot3(lf, upper) + carry[...]
        carry[...] += jnp.sum(lf, axis=1, keepdims=True)

    return pl.pallas_call(
        body, grid=(s // tb,),
        in_specs=[pl.BlockSpec((h, tb), lambda i: (0, i)), pl.BlockSpec((h, 1), lambda i: (0, 0))],
        out_specs=pl.BlockSpec((h, tb), lambda i: (0, i)), out_shape=jax.ShapeDtypeStruct((h, s), F32),
        scratch_shapes=[pltpu.VMEM((h, 1), F32)], name="fgate_fwd", compiler_params=_params(("arbitrary",)),
    )(f_t, b_col)


def _fgate_bwd(f_t, b_col, dc_a, dc_b, *, tb=256):
    h, s = f_t.shape
    tb = min(tb, s)
    n = s // tb

    def body(f_ref, b_ref, da_ref, db_ref, df_ref, dbias_ref, carry):
        @pl.when(pl.program_id(0) == 0)
        def _():
            carry[...] = jnp.zeros_like(carry)
            dbias_ref[...] = jnp.zeros_like(dbias_ref)

        dc = da_ref[...] + db_ref[...]
        shp = (tb, tb)
        lower = (lax.broadcasted_iota(jnp.int32, shp, 0) >= lax.broadcasted_iota(jnp.int32, shp, 1)).astype(BF16)
        d_lf = _dot3(dc, lower) + carry[...]
        carry[...] += jnp.sum(dc, axis=1, keepdims=True)
        df = d_lf * (1.0 - _sigmoid(f_ref[...] + b_ref[...]))
        df_ref[...] = df
        dbias_ref[...] += jnp.sum(df, axis=1, keepdims=True)

    blk = pl.BlockSpec((h, tb), lambda i: (0, n - 1 - i))
    return pl.pallas_call(
        body, grid=(n,), in_specs=[blk, pl.BlockSpec((h, 1), lambda i: (0, 0)), blk, blk],
        out_specs=[blk, pl.BlockSpec((h, 1), lambda i: (0, 0))],
        out_shape=[jax.ShapeDtypeStruct((h, s), F32), jax.ShapeDtypeStruct((h, 1), F32)],
        scratch_shapes=[pltpu.VMEM((h, 1), F32)], name="fgate_bwd", compiler_params=_params(("arbitrary",)),
    )(f_t, b_col, dc_a, dc_b)


_NT = (((1,), (1,)), ((), ()))


def _causal(shape, row0, col0, transposed):
    r = lax.broadcasted_iota(jnp.int32, shape, 0) + row0
    c = lax.broadcasted_iota(jnp.int32, shape, 1) + col0
    return (r <= c) if transposed else (c <= r)


def _attn_fwd(qs, kn, vb, c_col, c_row, *, blk=512):
    h, s, d = qs.shape
    b = min(blk, s)
    n = s // b

    def body(q_ref, k_ref, v_ref, cq_ref, ck_ref, o_ref, lse_ref, m_s, l_s, acc_s):
        i, j = pl.program_id(1), pl.program_id(2)

        @pl.when(j == 0)
        def _():
            m_s[...] = jnp.full_like(m_s, NEG_INF)
            l_s[...] = jnp.zeros_like(l_s)
            acc_s[...] = jnp.zeros_like(acc_s)

        @pl.when(j <= i)
        def _():
            sc = lax.dot_general(q_ref[0], k_ref[0], _NT, preferred_element_type=F32) + cq_ref[0] - ck_ref[0]
            sc = jnp.where(_causal((b, b), i * b, j * b, False), sc, NEG_INF)
            m_new = jnp.maximum(m_s[...], jnp.max(sc, axis=1, keepdims=True))
            alpha = jnp.exp(m_s[...] - m_new)
            p = jnp.exp(sc - m_new)
            l_s[...] = alpha * l_s[...] + jnp.sum(p, axis=1, keepdims=True)
            acc_s[...] = alpha * acc_s[...] + jnp.dot(p.astype(BF16), v_ref[0], preferred_element_type=F32)
            m_s[...] = m_new

        @pl.when(j == i)
        def _():
            o_ref[0] = acc_s[...] / l_s[...]
            lse_ref[0] = m_s[...] + jnp.log(l_s[...])

    qspec = pl.BlockSpec((1, b, d), lambda hh, i, j: (hh, i, 0))
    kspec = pl.BlockSpec((1, b, d), lambda hh, i, j: (hh, jnp.minimum(j, i), 0))
    return pl.pallas_call(
        body, grid=(h, n, n),
        in_specs=[qspec, kspec, kspec, pl.BlockSpec((1, b, 1), lambda hh, i, j: (hh, i, 0)),
                  pl.BlockSpec((1, 1, b), lambda hh, i, j: (hh, 0, jnp.minimum(j, i)))],
        out_specs=[qspec, pl.BlockSpec((1, b, 1), lambda hh, i, j: (hh, i, 0))],
        out_shape=[jax.ShapeDtypeStruct((h, s, d), F32), jax.ShapeDtypeStruct((h, s, 1), F32)],
        scratch_shapes=[pltpu.VMEM((b, 1), F32), pltpu.VMEM((b, 1), F32), pltpu.VMEM((b, d), F32)],
        name="attn_fwd", compiler_params=_params(("parallel", "parallel", "arbitrary")),
    )(qs, kn, vb, c_col, c_row)


def _attn_bwd_dq(qs, kn, vb, do, c_col, c_row, lse, delta, *, blk=512):
    h, s, d = qs.shape
    b = min(blk, s)
    n = s // b

    def body(q_ref, k_ref, v_ref, do_ref, cq_ref, ck_ref, lse_ref, dl_ref, dq_ref, dc_ref, dq_s, dc_s):
        i, j = pl.program_id(1), pl.program_id(2)

        @pl.when(j == 0)
        def _():
            dq_s[...] = jnp.zeros_like(dq_s)
            dc_s[...] = jnp.zeros_like(dc_s)

        @pl.when(j <= i)
        def _():
            sc = lax.dot_general(q_ref[0], k_ref[0], _NT, preferred_element_type=F32) + cq_ref[0] - ck_ref[0]
            sc = jnp.where(_causal((b, b), i * b, j * b, False), sc, NEG_INF)
            p = jnp.exp(sc - lse_ref[0])
            dp = lax.dot_general(do_ref[0].astype(BF16), v_ref[0], _NT, preferred_element_type=F32)
            ds = p * (dp - dl_ref[0])
            dq_s[...] += jnp.dot(ds.astype(BF16), k_ref[0], preferred_element_type=F32)
            dc_s[...] += jnp.sum(ds, axis=1, keepdims=True)

        @pl.when(j == i)
        def _():
            dq_ref[0] = dq_s[...]
            dc_ref[0] = dc_s[...]

    qspec = pl.BlockSpec((1, b, d), lambda hh, i, j: (hh, i, 0))
    kspec = pl.BlockSpec((1, b, d), lambda hh, i, j: (hh, jnp.minimum(j, i), 0))
    col = pl.BlockSpec((1, b, 1), lambda hh, i, j: (hh, i, 0))
    return pl.pallas_call(
        body, grid=(h, n, n),
        in_specs=[qspec, kspec, kspec, qspec, col, pl.BlockSpec((1, 1, b), lambda hh, i, j: (hh, 0, jnp.minimum(j, i))),
                  col, col],
        out_specs=[qspec, col],
        out_shape=[jax.ShapeDtypeStruct((h, s, d), F32), jax.ShapeDtypeStruct((h, s, 1), F32)],
        scratch_shapes=[pltpu.VMEM((b, d), F32), pltpu.VMEM((b, 1), F32)],
        name="attn_bwd_dq", compiler_params=_params(("parallel", "parallel", "arbitrary")),
    )(qs, kn, vb, do, c_col, c_row, lse, delta)


def _attn_bwd_dkv(qs, kn, vb, do, c_col, c_row, lse_row, delta_row, *, blk=512):
    h, s, d = qs.shape
    b = min(blk, s)
    n = s // b

    def body(q_ref, k_ref, v_ref, do_ref, cq_ref, ck_ref, lse_ref, dl_ref, dk_ref, dv_ref, dc_ref, dk_s, dv_s, dc_s):
        j, i = pl.program_id(1), pl.program_id(2)

        @pl.when(i == 0)
        def _():
            dk_s[...] = jnp.zeros_like(dk_s)
            dv_s[...] = jnp.zeros_like(dv_s)
            dc_s[...] = jnp.zeros_like(dc_s)

        @pl.when(i >= j)
        def _():
            st = lax.dot_general(k_ref[0], q_ref[0], _NT, preferred_element_type=F32) + cq_ref[0] - ck_ref[0]
            st = jnp.where(_causal((b, b), j * b, i * b, True), st, NEG_INF)
            pt = jnp.exp(st - lse_ref[0])
            dob = do_ref[0].astype(BF16)
            dv_s[...] += jnp.dot(pt.astype(BF16), dob, preferred_element_type=F32)
            dpt = lax.dot_general(v_ref[0], dob, _NT, preferred_element_type=F32)
            dst = pt * (dpt - dl_ref[0])
            dk_s[...] += jnp.dot(dst.astype(BF16), q_ref[0], preferred_element_type=F32)
            dc_s[...] -= jnp.sum(dst, axis=1, keepdims=True)

        @pl.when(i == n - 1)
        def _():
            dk_ref[0] = dk_s[...]
            dv_ref[0] = dv_s[...]
            dc_ref[0] = dc_s[...]

    kspec = pl.BlockSpec((1, b, d), lambda hh, j, i: (hh, j, 0))
    qspec = pl.BlockSpec((1, b, d), lambda hh, j, i: (hh, jnp.maximum(i, j), 0))
    row = pl.BlockSpec((1, 1, b), lambda hh, j, i: (hh, 0, jnp.maximum(i, j)))
    col = pl.BlockSpec((1, b, 1), lambda hh, j, i: (hh, j, 0))
    return pl.pallas_call(
        body, grid=(h, n, n), in_specs=[qspec, kspec, kspec, qspec, row, col, row, row],
        out_specs=[kspec, kspec, col],
        out_shape=[jax.ShapeDtypeStruct((h, s, d), F32), jax.ShapeDtypeStruct((h, s, d), F32),
                   jax.ShapeDtypeStruct((h, s, 1), F32)],
        scratch_shapes=[pltpu.VMEM((b, d), F32), pltpu.VMEM((b, d), F32), pltpu.VMEM((b, 1), F32)],
        name="attn_bwd_dkv", compiler_params=_params(("parallel", "parallel", "arbitrary")),
    )(qs, kn, vb, do, c_row, c_col, lse_row, delta_row)


def _heads(a):
    s = a.shape[0]
    return a.reshape(s, B_HEADS, B_HEAD_DIM).transpose(1, 0, 2)


def _unheads(a):
    s = a.shape[1]
    return a.transpose(1, 0, 2).reshape(s, B_WIDTH)


_ATT_SCALE = B_HEAD_DIM ** -0.5


def _even_fwd(x0, w, e, i):
    s = x0.shape[0]
    hs = B_HEADS * s
    hb = _rms_fwd(x0, w["norm_mix"][i:i + 1], name="mix_norm")
    z = _mm(hb, w["ev_w_in"][e], name="ev_in", bn=896)
    b_s = w["ev_b_spatial"][e][:, :, None]
    v_gain = w["ev_v_norm"][e:e + 1]
    ya = _gmlp_fwd(z, w["ev_w_spatial"][e], b_s, v_gain)
    q_raw = _heads(z[:, 2 * A_WIDTH:2 * A_WIDTH + B_WIDTH]).reshape(hs, B_HEAD_DIM)
    k_raw = _heads(z[:, 2 * A_WIDTH + B_WIDTH:2 * A_WIDTH + 2 * B_WIDTH]).reshape(hs, B_HEAD_DIM)
    v_raw = _heads(z[:, 2 * A_WIDTH + 2 * B_WIDTH:2 * A_WIDTH + 3 * B_WIDTH]).reshape(hs, B_HEAD_DIM)

    def prep(q, k, v, gq, gk):
        return q * _rstd(q) * gq * _ATT_SCALE, k * _rstd(k) * gk, v

    qs, kn, vb = _rw(prep, [q_raw, k_raw, v_raw], [w["ev_q_norm"][e:e + 1], w["ev_k_norm"][e:e + 1]],
                     [(B_HEAD_DIM, BF16)] * 3, tr=2048, name="attn_prep")
    shp = (B_HEADS, s, B_HEAD_DIM)
    qs, kn, vb = qs.reshape(shp), kn.reshape(shp), vb.reshape(shp)
    f_t = z[:, IN_COLS - B_HEADS:IN_COLS].T
    b_col = w["ev_b_fgate"][e][:, None]
    c = _fgate_fwd(f_t, b_col)
    o, lse = _attn_fwd(qs, kn, vb, c.reshape(B_HEADS, s, 1), c.reshape(B_HEADS, 1, s))
    ymix = jnp.concatenate([ya, _unheads(o).astype(BF16)], axis=1)
    x1 = _mm(ymix, w["ev_w_out"][e], add=x0, name="ev_out")
    return x1, (hb, z, ymix, q_raw, k_raw, qs, kn, vb, f_t, c, o, lse)


def _even_bwd(dx1, x0, saved, w, e, i):
    hb, z, ymix, q_raw, k_raw, qs, kn, vb, f_t, c, o, lse = saved
    s = x0.shape[0]
    hs = B_HEADS * s
    d_ymix = _mm(dx1, w["ev_w_out"][e], tb=True, name="ev_out_dx")
    d_wout = _mm(ymix, dx1, ta=True, name="ev_out_dw")
    do = _heads(d_ymix[:, A_WIDTH:])
    delta = _rw(lambda a, bb: jnp.sum(a * bb, axis=-1, keepdims=True),
                [do.reshape(hs, B_HEAD_DIM), o.reshape(hs, B_HEAD_DIM)], [], [(1, F32)], tr=2048, name="attn_delta")[0]
    c_col, c_row = c.reshape(B_HEADS, s, 1), c.reshape(B_HEADS, 1, s)
    dqs, dcq = _attn_bwd_dq(qs, kn, vb, do, c_col, c_row, lse, delta.reshape(B_HEADS, s, 1))
    dkn, dv, dck = _attn_bwd_dkv(qs, kn, vb, do, c_col, c_row, lse.reshape(B_HEADS, 1, s),
                                 delta.reshape(B_HEADS, 1, s))
    b_col = w["ev_b_fgate"][e][:, None]
    d_ft, d_bf = _fgate_bwd(f_t, b_col, dcq.reshape(B_HEADS, s), dck.reshape(B_HEADS, s))
    dq_raw, d_gq = _rms_bwd_call(q_raw, dqs.reshape(hs, B_HEAD_DIM), w["ev_q_norm"][e:e + 1], None,
                                 name="attn_qnorm_bwd", tr=2048, scale=_ATT_SCALE)
    dk_raw, d_gk = _rms_bwd_call(k_raw, dkn.reshape(hs, B_HEAD_DIM), w["ev_k_norm"][e:e + 1], None,
                                 name="attn_knorm_bwd", tr=2048)
    b_s = w["ev_b_spatial"][e][:, :, None]
    v_gain = w["ev_v_norm"][e:e + 1]
    dz_uv, d_ws, d_bs, d_vg = _gmlp_bwd(z, d_ymix, w["ev_w_spatial"][e], b_s, v_gain)
    shp = (B_HEADS, s, B_HEAD_DIM)
    dz = jnp.concatenate(
        [dz_uv, _unheads(dq_raw.reshape(shp)).astype(BF16), _unheads(dk_raw.reshape(shp)).astype(BF16),
         _unheads(dv).astype(BF16), d_ft.T.astype(BF16), jnp.zeros((s, IN_COLS_PAD - IN_COLS), BF16)], axis=1)
    d_h = _mm(dz, w["ev_w_in"][e], tb=True, name="ev_in_dx", bk=896)
    d_win = _mm(hb, dz, ta=True, name="ev_in_dw", bn=896)[:, :IN_COLS]
    dx0, d_nm = _rms_bwd_call(x0, d_h, w["norm_mix"][i:i + 1], dx1, name="mix_norm_bwd")
    grads = {"ev_w_in": d_win, "ev_w_out": d_wout, "ev_b_fgate": d_bf[:, 0], "ev_q_norm": d_gq[0],
             "ev_k_norm": d_gk[0], "ev_v_norm": d_vg[0], "ev_w_spatial": d_ws, "ev_b_spatial": d_bs[:, :, 0],
             "norm_mix": d_nm[0]}
    return dx0, grads


def _s5_disc(a_re, a_im, log_dt, b_re, b_im):
    dt = jnp.exp(log_dt)[:, None]
    lr, li = a_re, a_im
    mag = jnp.exp(lr * dt)
    ab_re, ab_im = mag * jnp.cos(li * dt), mag * jnp.sin(li * dt)
    den = lr * lr + li * li
    nr, ni = ab_re - 1.0, ab_im
    cr = (nr * lr + ni * li) / den
    ci = (ni * lr - nr * li) / den
    bb_re = cr[..., None] * b_re - ci[..., None] * b_im
    bb_im = cr[..., None] * b_im + ci[..., None] * b_re
    return ab_re, ab_im, bb_re, bb_im


_GPB = S5_GROUPS // S5_BLOCKS


def _blockdiag(t):
    a, b = t.shape[1:]
    eye = jnp.eye(_GPB, dtype=t.dtype)
    t = t.reshape(S5_BLOCKS, _GPB, a, 1, b) * eye[None, :, None, :, None]
    return t.reshape(S5_BLOCKS, _GPB * a, _GPB * b)


def _blockdiag_extract(m, a, b):
    eye = jnp.eye(_GPB, dtype=m.dtype)
    m = m.reshape(S5_BLOCKS, _GPB, a, _GPB, b) * eye[None, :, None, :, None]
    return jnp.sum(m, axis=3).reshape(S5_GROUPS, a, b)


def _scan(br, bi, ar, ai, *, rev, xs=None, tb=512, wl=512, name):
    s, n = br.shape
    tb = min(tb, s)
    nt = s // tb
    with_acc = xs is not None

    def body(*refs):
        br_ref, bi_ref, ar_ref, ai_ref = refs[:4]
        k = 4
        if with_acc:
            xr_ref, xi_ref = refs[4:6]
            k = 6
        or_ref, oi_ref = refs[k:k + 2]
        k += 2
        if with_acc:
            dar_ref, dai_ref = refs[k:k + 2]
            k += 2
        cr, ci = refs[k:k + 2]

        @pl.when(pl.program_id(1) == 0)
        def _():
            cr[...] = jnp.zeros_like(cr)
            ci[...] = jnp.zeros_like(ci)
            if with_acc:
                dar_ref[...] = jnp.zeros_like(dar_ref)
                dai_ref[...] = jnp.zeros_like(dai_ref)

        a_r, a_i = ar_ref[...], ai_ref[...]

        def step(q, carry):
            row = (tb - 1 - q) if rev else q
            xr, xi = carry[0], carry[1]
            b_r, b_i = br_ref[pl.ds(row, 1), :], bi_ref[pl.ds(row, 1), :]
            extra = ()
            if rev:
                if with_acc:
                    p_r, p_i = xr_ref[pl.ds(row, 1), :], xi_ref[pl.ds(row, 1), :]
                    extra = (carry[2] + (xr * p_r + xi * p_i), carry[3] + (xi * p_r - xr * p_i))
                n_r = a_r * xr + a_i * xi + b_r
                n_i = a_r * xi - a_i * xr + b_i
            else:
                n_r = a_r * xr - a_i * xi + b_r
                n_i = a_r * xi + a_i * xr + b_i
            or_ref[pl.ds(row, 1), :] = n_r
            oi_ref[pl.ds(row, 1), :] = n_i
            return (n_r, n_i) + extra

        init = (cr[...], ci[...])
        if with_acc:
            init += (jnp.zeros_like(a_r), jnp.zeros_like(a_r))
        out = lax.fori_loop(0, tb, step, init, unroll=8)
        cr[...] = out[0]
        ci[...] = out[1]
        if with_acc:
            dar_ref[...] += out[2]
            dai_ref[...] += out[3]

    tmap = (lambda c, t: (nt - 1 - t, c)) if rev else (lambda c, t: (t, c))
    blk = pl.BlockSpec((tb, wl), tmap)
    vec = pl.BlockSpec((1, wl), lambda c, t: (0, c))
    in_specs = [blk, blk, vec, vec] + ([blk, blk] if with_acc else [])
    out_specs = [blk, blk] + ([vec, vec] if with_acc else [])
    out_shape = [jax.ShapeDtypeStruct((s, n), F32)] * 2 + ([jax.ShapeDtypeStruct((1, n), F32)] * 2 if with_acc else [])
    return pl.pallas_call(
        body, grid=(n // wl, nt), in_specs=in_specs, out_specs=out_specs, out_shape=out_shape,
        scratch_shapes=[pltpu.VMEM((1, wl), F32), pltpu.VMEM((1, wl), F32)], name=name,
        compiler_params=_params(("parallel", "arbitrary")),
    )(br, bi, ar, ai, *(xs or ()))


def _s5_mats(w, o):
    ab_re, ab_im, bb_re, bb_im = _s5_disc(w["od_a_re"][o], w["od_a_im"][o], w["od_log_dt"][o], w["od_b_re"][o],
                                          w["od_b_im"][o])
    c_re, c_im = w["od_c_re"][o], w["od_c_im"][o]
    tr = lambda t: t.transpose(0, 2, 1)
    return {
        "ar": ab_re.reshape(1, S5_N), "ai": ab_im.reshape(1, S5_N),
        "wb_re": _blockdiag(tr(bb_re)), "wb_im": _blockdiag(tr(bb_im)),
        "wc_re": _blockdiag(tr(c_re)), "wc_imn": _blockdiag(-tr(c_im)),
        "wct_re": _blockdiag(c_re), "wct_imn": _blockdiag(-c_im),
        "wbt_re": _blockdiag(bb_re), "wbt_im": _blockdiag(bb_im),
    }


def _odd_fwd(x0, w, o, i):
    mats = _s5_mats(w, o)
    nm = w["norm_mix"][i:i + 1]
    d_row = w["od_d"][o:o + 1]
    hb = _rms_fwd(x0, nm, name="mix_norm")
    bur = _bdmm(hb, mats["wb_re"], name="s5_b_re")
    bui = _bdmm(hb, mats["wb_im"], name="s5_b_im")
    xr, xi = _scan(bur, bui, mats["ar"], mats["ai"], rev=False, name="s5_scan")
    cp = _bdmm(xi, mats["wc_imn"], add=_bdmm(xr, mats["wc_re"], name="s5_c_re"), name="s5_c_im")

    def fn(xv, cpv, gv, dv):
        y = cpv + dv * (xv * _rstd(xv) * gv)
        return y, _gelu(y)

    y, gy = _rw(fn, [x0, cp], [nm, d_row], [(D_MODEL, F32), (D_MODEL, BF16)], tr=512, name="s5_out")
    gg = _mm(gy, w["od_w_glu"][o], name="od_glu")
    x1 = _rw(lambda xv, g: xv + g[:, :D_MODEL] * _sigmoid(g[:, D_MODEL:]), [x0, gg], [], [(D_MODEL, F32)], tr=512,
             name="od_glu_out")[0]
    return x1, (hb, xr, xi, y, gy, gg)


def _odd_bwd(dx1, x0, saved, w, o, i):
    hb, xr, xi, y, gy, gg = saved
    mats = _s5_mats(w, o)
    nm = w["norm_mix"][i:i + 1]
    d_row = w["od_d"][o:o + 1]

    def fn_glu(dv, g):
        ga, gb = g[:, :D_MODEL], g[:, D_MODEL:]
        sg = _sigmoid(gb)
        return jnp.concatenate([dv * sg, dv * ga * (sg * (1.0 - sg))], axis=1)

    dgg = _rw(fn_glu, [dx1, gg], [], [(2 * D_MODEL, BF16)], tr=512, name="od_glu_out_bwd")[0]
    d_wglu = _mm(gy, dgg, ta=True, name="od_glu_dw")
    d_gy = _mm(dgg, w["od_w_glu"][o], tb=True, name="od_glu_dx")

    def fn_y(dg, yv, xv, gv, dv):
        dy = dg * _gelu_grad(yv)
        h = xv * _rstd(xv) * gv
        return dy, dv * dy, jnp.sum(dy * h, axis=0, keepdims=True)

    dyb, dhd, d_d = _rw(fn_y, [d_gy, y, x0], [nm, d_row], [(D_MODEL, BF16), (D_MODEL, F32)], [(1, D_MODEL)], tr=512,
                        name="s5_out_bwd")
    dxr = _bdmm(dyb, mats["wct_re"], name="s5_c_re_dx")
    dxi = _bdmm(dyb, mats["wct_imn"], name="s5_c_im_dx")
    gc, gp = S5_GROUP_CH, S5_STATE
    d_c_re = _blockdiag_extract(_bdwg(dyb, xr, _GPB * gc, _GPB * gp, name="s5_c_re_dw"), gc, gp)
    d_c_im = -_blockdiag_extract(_bdwg(dyb, xi, _GPB * gc, _GPB * gp, name="s5_c_im_dw"), gc, gp)
    lr, li, d_ar, d_ai = _scan(dxr, dxi, mats["ar"], mats["ai"], rev=True, xs=(xr, xi), name="s5_scan_bwd")
    d_bb_re = _blockdiag_extract(_bdwg(hb, lr, _GPB * gc, _GPB * gp, name="s5_b_re_dw"), gc, gp).transpose(0, 2, 1)
    d_bb_im = _blockdiag_extract(_bdwg(hb, li, _GPB * gc, _GPB * gp, name="s5_b_im_dw"), gc, gp).transpose(0, 2, 1)
    dh = _bdmm(li, mats["wbt_im"], add=_bdmm(lr, mats["wbt_re"], add=dhd, name="s5_b_re_dx"), name="s5_b_im_dx")
    dx0, d_nm = _rms_bwd_call(x0, dh, nm, dx1, name="mix_norm_bwd")
    _, vjp = jax.vjp(_s5_disc, w["od_a_re"][o], w["od_a_im"][o], w["od_log_dt"][o], w["od_b_re"][o], w["od_b_im"][o])
    d_a_re, d_a_im, d_log_dt, d_b_re, d_b_im = vjp(
        (d_ar.reshape(S5_GROUPS, S5_STATE), d_ai.reshape(S5_GROUPS, S5_STATE), d_bb_re, d_bb_im))
    grads = {"od_a_re": d_a_re, "od_a_im": d_a_im, "od_log_dt": d_log_dt, "od_b_re": d_b_re, "od_b_im": d_b_im,
             "od_c_re": d_c_re, "od_c_im": d_c_im, "od_d": d_d[0], "od_w_glu": d_wglu, "norm_mix": d_nm[0]}
    return dx0, grads


_HBM = pl.BlockSpec(memory_space=pltpu.HBM)


def _mesh_pos():
    return lax.axis_index("x"), lax.axis_index("y"), lax.axis_index("c")


def _slot(px, py, pc):
    return 4 * px + 2 * py + pc


def _all_gather(x, *, name):
    def body(x_ref, out_ref, send_sems, recv_sems, local_sem):
        mx, my, mc = _mesh_pos()
        me, sibling = (mx, my, mc), (mx, my, 1 - mc)
        chips = [(1 - mx, my), (mx, 1 - my), (1 - mx, 1 - my)]

        def copy(k, block, to, src=None):
            dst = out_ref.at[_slot(*block)]
            return pltpu.make_async_remote_copy(
                src_ref=dst if src is None else src, dst_ref=dst, send_sem=send_sems.at[k], recv_sem=recv_sems.at[k],
                device_id=to, device_id_type=MESH)

        mine = pltpu.make_async_copy(x_ref, out_ref.at[_slot(*me)], local_sem)
        mine.start()
        first = [copy(0, me, sibling, src=x_ref)]
        first += [copy(1 + j, me, (*chip, mc), src=x_ref) for j, chip in enumerate(chips)]
        for cp in first:
            cp.start()
        passed = [copy(4 + j, (*chip, mc), sibling) for j, chip in enumerate(chips)]
        for j, chip in enumerate(chips):
            copy(1 + j, (*chip, mc), me).wait_recv()
            passed[j].start()
        copy(0, sibling, me).wait_recv()
        for j, chip in enumerate(chips):
            copy(4 + j, (*chip, 1 - mc), me).wait_recv()
        for cp in first + passed:
            cp.wait_send()
        mine.wait()

    return pl.pallas_call(
        body, out_shape=jax.ShapeDtypeStruct((N_DEV,) + x.shape, x.dtype), in_specs=[_HBM], out_specs=_HBM,
        scratch_shapes=[pltpu.SemaphoreType.DMA((7,)), pltpu.SemaphoreType.DMA((7,)), pltpu.SemaphoreType.DMA],
        name=name,
    )(x)


def _all_to_all(x, *, name):
    def body(x_ref, out_ref, send_sems, recv_sems, local_sem):
        mx, my, mc = _mesh_pos()
        my_slot = _slot(mx, my, mc)
        mine = pltpu.make_async_copy(x_ref.at[my_slot], out_ref.at[my_slot], local_sem)
        mine.start()
        copies = []
        for k in range(1, N_DEV):
            peer = (1 - mx if k & 4 else mx, 1 - my if k & 2 else my, 1 - mc if k & 1 else mc)
            copies.append(pltpu.make_async_remote_copy(
                src_ref=x_ref.at[_slot(*peer)], dst_ref=out_ref.at[my_slot], send_sem=send_sems.at[k - 1],
                recv_sem=recv_sems.at[k - 1], device_id=peer, device_id_type=MESH))
        for cp in copies:
            cp.start()
        for cp in copies:
            cp.wait()
        mine.wait()

    return pl.pallas_call(
        body, out_shape=jax.ShapeDtypeStruct(x.shape, x.dtype), in_specs=[_HBM], out_specs=_HBM,
        scratch_shapes=[pltpu.SemaphoreType.DMA((7,)), pltpu.SemaphoreType.DMA((7,)), pltpu.SemaphoreType.DMA],
        name=name,
    )(x)


def _adamw(parts, w, m, v, *, name, tr=128):
    r, c = w.shape

    def body(p_ref, w_ref, m_ref, v_ref, g_ref, d_ref, nm_ref, nv_ref):
        g = p_ref[0]
        for k in range(1, N_DEV):
            g = g + p_ref[k]
        nm = ADAM_B1 * m_ref[...] + (1.0 - ADAM_B1) * g
        nv = ADAM_B2 * v_ref[...] + (1.0 - ADAM_B2) * (g * g)
        m_hat = nm / (1.0 - ADAM_B1 ** ADAM_STEP)
        v_hat = nv / (1.0 - ADAM_B2 ** ADAM_STEP)
        g_ref[...] = g
        d_ref[...] = -ADAM_LR * (m_hat / (jnp.sqrt(v_hat) + ADAM_EPS) + ADAM_WD * w_ref[...])
        nm_ref[...] = nm
        nv_ref[...] = nv

    blk = pl.BlockSpec((tr, c), lambda i: (i, 0))
    return pl.pallas_call(
        body, grid=(r // tr,), in_specs=[pl.BlockSpec((N_DEV, tr, c), lambda i: (0, i, 0)), blk, blk, blk],
        out_specs=[blk] * 4, out_shape=[jax.ShapeDtypeStruct((r, c), F32)] * 4, name=name,
        compiler_params=_params(("parallel",)),
    )(parts, w, m, v)


_FLAT_COLS = 1024
_FLAT_ROW_ALIGN = 128

_SHARD_AXIS = {
    "norm_mix": None, "norm_ffn": None, "ev_w_in": 2, "ev_b_fgate": None, "ev_q_norm": None, "ev_k_norm": None,
    "ev_v_norm": None, "ev_w_spatial": None, "ev_b_spatial": None, "ev_w_out": 1, "od_a_re": None, "od_a_im": None,
    "od_log_dt": None, "od_b_re": None, "od_b_im": None, "od_c_re": None, "od_c_im": None, "od_d": 1, "od_w_glu": 2,
    "ffn_w_up": 2, "ffn_conv_w": 2, "ffn_conv_b": None, "ffn_w_down": 1, "ple_w_proj": 2, "ple_w_gate": 1,
}
_WEIGHTS = list(_SHARD_AXIS)
_SHARDED = [n for n in _WEIGHTS if _SHARD_AXIS[n] is not None]
_REPLICATED = [n for n in _WEIGHTS if _SHARD_AXIS[n] is None]
_MATMUL_WEIGHTS = ["ev_w_in", "ev_w_out", "od_w_glu", "ffn_w_up", "ffn_w_down", "ple_w_proj", "ple_w_gate"]
_SMALL_SHARDED = ["od_d", "ffn_conv_w"]


def _pack(arrays, lead=()):
    nl = len(lead)
    flat = jnp.concatenate([a.reshape(lead + (-1,)) for a in arrays], axis=nl)
    n = flat.shape[nl]
    chunk = _FLAT_COLS * _FLAT_ROW_ALIGN
    total = -(-n // chunk) * chunk
    flat = jnp.pad(flat, [(0, 0)] * nl + [(0, total - n)])
    return flat.reshape(lead + (total // _FLAT_COLS, _FLAT_COLS))


def _unpack(buf, shapes, lead=()):
    nl = len(lead)
    flat = buf.reshape(lead + (-1,))
    out, off = [], 0
    for shp in shapes:
        n = math.prod(shp)
        out.append(lax.slice_in_dim(flat, off, off + n, axis=nl).reshape(lead + tuple(shp)))
        off += n
    return out


def _join_shards(g, axis):
    g = jnp.moveaxis(g, 0, axis)
    shp = g.shape
    return g.reshape(shp[:axis] + (shp[axis] * shp[axis + 1],) + shp[axis + 2:])


def _split_shards(full, axis):
    shp = full.shape
    g = full.reshape(shp[:axis] + (N_DEV, shp[axis] // N_DEV) + shp[axis + 1:])
    return jnp.moveaxis(g, axis, 0)


def _local_step(x, p, tgt, w):
    saved = []
    h = x
    for i in range(DEPTH):
        x0 = h
        if i % 2 == 0:
            x1, sv_mix = _even_fwd(x0, w, i // 2, i)
        else:
            x1, sv_mix = _odd_fwd(x0, w, i // 2, i)
        x2, sv_ffn = _ffn_fwd(x1, w, i)
        x3, sv_ple = _ple_fwd(x2, p[i], w, i)
        saved.append((x0, x1, x2, sv_mix, sv_ffn, sv_ple))
        h = x3
    loss, dh = _loss_and_grad(h, tgt)
    per_layer = {}
    for i in reversed(range(DEPTH)):
        x0, x1, x2, sv_mix, sv_ffn, sv_ple = saved[i]
        dx2, g_ple = _ple_bwd(dh, x2, p[i], sv_ple, w, i)
        dx1, g_ffn = _ffn_bwd(dx2, x1, sv_ffn, w, i)
        if i % 2 == 0:
            dh, g_mix = _even_bwd(dx1, x0, sv_mix, w, i // 2, i)
        else:
            dh, g_mix = _odd_bwd(dx1, x0, sv_mix, w, i // 2, i)
        for name, g in {**g_ple, **g_ffn, **g_mix}.items():
            per_layer.setdefault(name, []).append(g)
    grads = {name: jnp.stack(gs[::-1]) for name, gs in per_layer.items()}
    return loss, dh, grads


def kernel(x, p, norm_mix, norm_ffn, ev_w_in, ev_b_fgate, ev_q_norm, ev_k_norm, ev_v_norm, ev_w_spatial, ev_b_spatial, ev_w_out, od_a_re, od_a_im, od_log_dt, od_b_re, od_b_im, od_c_re, od_c_im, od_d, od_w_glu, ffn_w_up, ffn_conv_w, ffn_conv_b, ffn_w_down, ple_w_proj, ple_w_gate, loss_target, m_norm_mix, m_norm_ffn, m_ev_w_in, m_ev_b_fgate, m_ev_q_norm, m_ev_k_norm, m_ev_v_norm, m_ev_w_spatial, m_ev_b_spatial, m_ev_w_out, m_od_a_re, m_od_a_im, m_od_log_dt, m_od_b_re, m_od_b_im, m_od_c_re, m_od_c_im, m_od_d, m_od_w_glu, m_ffn_w_up, m_ffn_conv_w, m_ffn_conv_b, m_ffn_w_down, m_ple_w_proj, m_ple_w_gate, v_norm_mix, v_norm_ffn, v_ev_w_in, v_ev_b_fgate, v_ev_q_norm, v_ev_k_norm, v_ev_v_norm, v_ev_w_spatial, v_ev_b_spatial, v_ev_w_out, v_od_a_re, v_od_a_im, v_od_log_dt, v_od_b_re, v_od_b_im, v_od_c_re, v_od_c_im, v_od_d, v_od_w_glu, v_ffn_w_up, v_ffn_conv_w, v_ffn_conv_b, v_ffn_w_down, v_ple_w_proj, v_ple_w_gate):
    args = locals()
    wl = {n: args[n] for n in _WEIGHTS}
    ml = {n: args["m_" + n] for n in _WEIGHTS}
    vl = {n: args["v_" + n] for n in _WEIGHTS}

    big = _all_gather(_pack([wl[n].astype(BF16) for n in _MATMUL_WEIGHTS]), name="gather_matmul_weights")
    small = _all_gather(_pack([wl[n] for n in _SMALL_SHARDED]), name="gather_small_weights")
    w = {n: wl[n] for n in _REPLICATED}
    for names, buf in ((_MATMUL_WEIGHTS, big), (_SMALL_SHARDED, small)):
        for n, g in zip(names, _unpack(buf, [wl[n].shape for n in names], lead=(N_DEV,))):
            w[n] = _join_shards(g, _SHARD_AXIS[n] + 1 - 1)
    w["ev_w_in"] = jnp.pad(w["ev_w_in"], ((0, 0), (0, 0), (0, IN_COLS_PAD - IN_COLS)))

    loss, grad_x, grads = _local_step(x[0], p[:, 0], loss_target[0], w)
    loss = lax.psum(loss, ("x", "y", "c"))

    send = _pack([_split_shards(grads[n], _SHARD_AXIS[n]) for n in _SHARDED], lead=(N_DEV,))
    parts = _all_to_all(send, name="scatter_weight_grads")
    shapes = [wl[n].shape for n in _SHARDED]
    res = _adamw(parts, _pack([wl[n] for n in _SHARDED]), _pack([ml[n] for n in _SHARDED]),
                 _pack([vl[n] for n in _SHARDED]), name="adamw_sharded")
    out = {kind: dict(zip(_SHARDED, _unpack(buf, shapes))) for kind, buf in zip(("g", "d", "m", "v"), res)}

    parts = _all_gather(_pack([grads[n] for n in _REPLICATED]), name="gather_replicated_grads")
    shapes = [wl[n].shape for n in _REPLICATED]
    res = _adamw(parts, _pack([wl[n] for n in _REPLICATED]), _pack([ml[n] for n in _REPLICATED]),
                 _pack([vl[n] for n in _REPLICATED]), name="adamw_replicated")
    for kind, buf in zip(("g", "d", "m", "v"), res):
        out[kind].update(zip(_REPLICATED, _unpack(buf, shapes)))

    return (loss, grad_x[None], *[out["g"][n] for n in _WEIGHTS], *[out["d"][n] for n in _WEIGHTS],
            *[out["m"][n] for n in _WEIGHTS], *[out["v"][n] for n in _WEIGHTS])
```

```python
import functools
import math

import jax
import jax.numpy as jnp
from jax import lax
from jax.experimental import pallas as pl
from jax.experimental.pallas import tpu as pltpu

F32 = jnp.float32
BF16 = jnp.bfloat16

D_MODEL = 1024
DEPTH = 4
A_GROUPS = 4
A_CHUNK = 128
A_WIDTH = 512
B_HEADS = 8
B_HEAD_DIM = 64
B_WIDTH = 512
IN_COLS = 2 * A_WIDTH + 3 * B_WIDTH + B_HEADS
IN_COLS_PAD = 2688
S5_GROUP_CH = 16
S5_GROUPS = 64
S5_STATE = 64
S5_N = S5_GROUPS * S5_STATE
S5_BLOCKS = 8
D_FF = 2816
PLE_DIM = 256
EPS = 1e-6
NEG_INF = -1e30
N_DEV = 8

ADAM_LR = 0.001
ADAM_B1 = 0.9
ADAM_B2 = 0.999
ADAM_EPS = 1e-08
ADAM_WD = 0.01
ADAM_STEP = 10

VMEM_LIMIT_BYTES = 56 * 1024 * 1024
MESH = pl.DeviceIdType.MESH


def _params(sem, vmem=VMEM_LIMIT_BYTES):
    return pltpu.CompilerParams(dimension_semantics=sem, vmem_limit_bytes=vmem)


_GELU_K = math.sqrt(2.0 / math.pi)


def _gelu(x):
    return x * (0.5 * (1.0 + jnp.tanh(_GELU_K * (x + 0.044715 * (x * x * x)))))


def _gelu_grad(x):
    t = jnp.tanh(_GELU_K * (x + 0.044715 * (x * x * x)))
    return 0.5 * (1.0 + t) + 0.5 * x * (1.0 - t * t) * (_GELU_K * (1.0 + 3.0 * 0.044715 * (x * x)))


def _sigmoid(x):
    return 1.0 / (1.0 + jnp.exp(-x))


def _rstd(x):
    return lax.rsqrt(jnp.mean(x * x, axis=-1, keepdims=True) + EPS)


def _rms_bwd(x, dy, g):
    r = _rstd(x)
    xh = x * r
    dyg = dy if g is None else dy * g
    dx = r * (dyg - xh * jnp.mean(dyg * xh, axis=-1, keepdims=True))
    return dx, dy * xh


def _shift_down(blk, halo, k):
    tr = blk.shape[0]
    r = pltpu.roll(blk, k, 0)
    hr = pltpu.roll(halo, k, 0)
    first = jnp.where(lax.broadcasted_iota(jnp.int32, hr.shape, 0) < k, hr, r[0:8])
    return jnp.concatenate([first, r[8:tr]], axis=0)


def _shift_up(blk, halo, k):
    tr = blk.shape[0]
    r = pltpu.roll(blk, tr - k, 0)
    hr = pltpu.roll(halo, 8 - k, 0)
    last = jnp.where(lax.broadcasted_iota(jnp.int32, hr.shape, 0) >= 8 - k, hr, r[tr - 8:tr])
    return jnp.concatenate([r[0:tr - 8], last], axis=0)


def _rw(fn, rows, consts, outs, accs=(), *, tr, name, prev=(), nxt=(), widths=None):
    s = rows[0].shape[0]
    tr = min(tr, s)
    n = s // tr
    nr, nc, npv, nnx, no, na = len(rows), len(consts), len(prev), len(nxt), len(outs), len(accs)
    widths = widths or [None] * nr

    def body(*refs):
        ins, out_refs = refs[:nr + nc + npv + nnx], refs[nr + nc + npv + nnx:]
        i = pl.program_id(0)
        vals = [r[...] for r in ins[:nr + nc]]
        vals += [jnp.where(i == 0, 0.0, r[...]) for r in ins[nr + nc:nr + nc + npv]]
        vals += [jnp.where(i == n - 1, 0.0, r[...]) for r in ins[nr + nc + npv:]]
        res = fn(*vals)
        if not isinstance(res, (tuple, list)):
            res = (res,)
        for k in range(no):
            out_refs[k][...] = res[k].astype(out_refs[k].dtype)
        if na:
            @pl.when(i == 0)
            def _():
                for k in range(na):
                    out_refs[no + k][...] = jnp.zeros_like(out_refs[no + k])

            for k in range(na):
                out_refs[no + k][...] += res[no + k]

    in_specs = []
    for a, w in zip(rows, widths):
        if w is None:
            in_specs.append(pl.BlockSpec((tr, a.shape[1]), lambda i: (i, 0)))
        else:
            in_specs.append(pl.BlockSpec((tr, w[1]), functools.partial(lambda i, cb: (i, cb), cb=w[0])))
    for c in consts:
        in_specs.append(pl.BlockSpec(c.shape, functools.partial(lambda i, nd: (0,) * nd, nd=c.ndim)))
    t8 = tr // 8
    for a in prev:
        in_specs.append(pl.BlockSpec((8, a.shape[1]), lambda i: (jnp.maximum(i * t8 - 1, 0), 0)))
    for a in nxt:
        in_specs.append(pl.BlockSpec((8, a.shape[1]), lambda i: (jnp.minimum((i + 1) * t8, s // 8 - 1), 0)))
    out_shape = [jax.ShapeDtypeStruct((s, w), dt) for w, dt in outs]
    out_specs = [pl.BlockSpec((tr, w), lambda i: (i, 0)) for w, _ in outs]
    out_shape += [jax.ShapeDtypeStruct(a, F32) for a in accs]
    out_specs += [pl.BlockSpec(a, lambda i: (0, 0)) for a in accs]
    res = pl.pallas_call(
        body, grid=(n,), in_specs=in_specs, out_specs=out_specs, out_shape=out_shape, name=name,
        compiler_params=_params(("arbitrary",) if na else ("parallel",)),
    )(*rows, *consts, *prev, *nxt)
    return res


def _pick(n, cap):
    if n <= cap:
        return n
    best = None
    for d in range(128, cap + 1, 128):
        if n % d == 0:
            best = d
    assert best is not None, (n, cap)
    return best


def _mm(a, b, *, ta=False, tb=False, out_dtype=F32, add=None, name, bm=1024, bn=512, bk=None):
    (k_dim, m) = a.shape if ta else a.shape[::-1]
    n = b.shape[0] if tb else b.shape[1]
    assert (b.shape[1] if tb else b.shape[0]) == k_dim
    if bk is None:
        bk = 1024 if ta else 2816
    bm, bn, bk = _pick(m, bm), _pick(n, bn), _pick(k_dim, bk)
    nk = k_dim // bk
    dims = (((0 if ta else 1,), (1 if tb else 0,)), ((), ()))
    has_add = add is not None

    def body(*refs):
        a_ref, b_ref = refs[0], refs[1]
        add_ref = refs[2] if has_add else None
        o_ref = refs[2 + has_add]
        p = lax.dot_general(a_ref[...].astype(BF16), b_ref[...].astype(BF16), dims, preferred_element_type=F32)

        def finish(acc):
            if has_add:
                acc = acc + add_ref[...]
            o_ref[...] = acc.astype(o_ref.dtype)

        if nk == 1:
            finish(p)
        else:
            acc_ref = refs[3 + has_add]
            kk = pl.program_id(2)

            @pl.when(kk == 0)
            def _():
                acc_ref[...] = p

            @pl.when(kk > 0)
            def _():
                acc_ref[...] += p

            @pl.when(kk == nk - 1)
            def _():
                finish(acc_ref[...])

    a_spec = pl.BlockSpec((bk, bm), lambda i, j, k: (k, i)) if ta else pl.BlockSpec((bm, bk), lambda i, j, k: (i, k))
    b_spec = pl.BlockSpec((bn, bk), lambda i, j, k: (j, k)) if tb else pl.BlockSpec((bk, bn), lambda i, j, k: (k, j))
    in_specs = [a_spec, b_spec]
    args = [a, b]
    if has_add:
        in_specs.append(pl.BlockSpec((bm, bn), lambda i, j, k: (i, j)))
        args.append(add)
    return pl.pallas_call(
        body, grid=(m // bm, n // bn, nk), in_specs=in_specs,
        out_specs=pl.BlockSpec((bm, bn), lambda i, j, k: (i, j)),
        out_shape=jax.ShapeDtypeStruct((m, n), out_dtype),
        scratch_shapes=[pltpu.VMEM((bm, bn), F32)] if nk > 1 else [],
        name=name, compiler_params=_params(("parallel", "parallel", "arbitrary")),
    )(*args)


def _bdmm(a, w, *, add=None, name, bm=1024):
    s = a.shape[0]
    nb, ka, nw = w.shape
    bm = min(bm, s)
    has_add = add is not None

    def body(*refs):
        a_ref, w_ref = refs[0], refs[1]
        o_ref = refs[2 + has_add]
        p = jnp.dot(a_ref[...].astype(BF16), w_ref[0].astype(BF16), preferred_element_type=F32)
        if has_add:
            p = p + refs[2][...]
        o_ref[...] = p

    in_specs = [pl.BlockSpec((bm, ka), lambda i, j: (i, j)), pl.BlockSpec((1, ka, nw), lambda i, j: (j, 0, 0))]
    args = [a, w]
    if has_add:
        in_specs.append(pl.BlockSpec((bm, nw), lambda i, j: (i, j)))
        args.append(add)
    return pl.pallas_call(
        body, grid=(s // bm, nb), in_specs=in_specs, out_specs=pl.BlockSpec((bm, nw), lambda i, j: (i, j)),
        out_shape=jax.ShapeDtypeStruct((s, nb * nw), F32), name=name, compiler_params=_params(("parallel", "parallel")),
    )(*args)


def _bdwg(a, b, ka, nw, *, name, bk=1024):
    s = a.shape[0]
    nb = a.shape[1] // ka
    bk = min(bk, s)

    def body(a_ref, b_ref, o_ref):
        p = lax.dot_general(a_ref[...].astype(BF16), b_ref[...].astype(BF16), (((0,), (0,)), ((), ())),
                            preferred_element_type=F32)

        @pl.when(pl.program_id(1) == 0)
        def _():
            o_ref[0] = p

        @pl.when(pl.program_id(1) > 0)
        def _():
            o_ref[0] += p

    return pl.pallas_call(
        body, grid=(nb, s // bk),
        in_specs=[pl.BlockSpec((bk, ka), lambda j, k: (k, j)), pl.BlockSpec((bk, nw), lambda j, k: (k, j))],
        out_specs=pl.BlockSpec((1, ka, nw), lambda j, k: (j, 0, 0)),
        out_shape=jax.ShapeDtypeStruct((nb, ka, nw), F32), name=name, compiler_params=_params(("parallel", "arbitrary")),
    )(a, b)


def _rms_fwd(x, g, *, name):
    if g is None:
        return _rw(lambda xv: xv * _rstd(xv), [x], [], [(x.shape[1], BF16)], tr=512, name=name)[0]
    return _rw(lambda xv, gv: xv * _rstd(xv) * gv, [x], [g], [(x.shape[1], BF16)], tr=512, name=name)[0]


def _rms_bwd_call(x, dy, g, dx_add, *, name, tr=512, scale=None, bf16_copy=False):
    w = x.shape[1]
    has_g, has_add = g is not None, dx_add is not None

    def fn(*v):
        xv, dyv = v[0], v[1]
        if scale is not None:
            dyv = dyv * scale
        gv = v[2 + has_add] if has_g else None
        dx, dg = _rms_bwd(xv, dyv, gv)
        if has_add:
            dx = dx + v[2]
        return (dx,) + ((dx,) if bf16_copy else ()) + ((jnp.sum(dg, axis=0, keepdims=True),) if has_g else ())

    rows = [x, dy] + ([dx_add] if has_add else [])
    outs = [(w, F32)] + ([(w, BF16)] if bf16_copy else [])
    res = _rw(fn, rows, [g] if has_g else [], outs, [(1, w)] if has_g else [], tr=tr, name=name)
    return tuple(res) if has_g else tuple(res) + (None,)


def _conv_taps(hup, halo, cw):
    h1 = _shift_down(hup, halo, 1)
    h2 = _shift_down(hup, halo, 2)
    return h1, h2


def _ffn_fwd(x1, w, i):
    h2b = _rms_fwd(x1, w["norm_ffn"][i:i + 1], name="ffn_norm")
    hup = _mm(h2b, w["ffn_w_up"][i], tb=True, name="ffn_up")
    cw, cb = w["ffn_conv_w"][i], w["ffn_conv_b"][i:i + 1]

    def fn(hv, cwv, cbv, halo):
        h1, h2 = _conv_taps(hv, halo, cwv)
        hc = cbv + cwv[0:1] * h2
        hc = hc + cwv[1:2] * h1
        hc = hc + cwv[2:3] * hv
        g, up = hc[:, :D_FF], hc[:, D_FF:]
        return (g * _sigmoid(g)) * up

    a = _rw(fn, [hup], [cw, cb], [(D_FF, BF16)], tr=128, name="ffn_conv_gate", prev=[hup])[0]
    x2 = _mm(a, w["ffn_w_down"][i], add=x1, name="ffn_down")
    return x2, (h2b, hup, a)


def _ffn_bwd(dx2, dx2b, x1, saved, w, i):
    h2b, hup, a = saved
    cw, cb = w["ffn_conv_w"][i], w["ffn_conv_b"][i:i + 1]
    da = _mm(dx2b, w["ffn_w_down"][i], tb=True, name="ffn_down_dx")
    d_wdown = _mm(a, dx2b, ta=True, name="ffn_down_dw", bm=1408)

    def fn1(hv, dav, cwv, cbv, halo):
        h1, h2 = _conv_taps(hv, halo, cwv)
        hc = cbv + cwv[0:1] * h2
        hc = hc + cwv[1:2] * h1
        hc = hc + cwv[2:3] * hv
        g, up = hc[:, :D_FF], hc[:, D_FF:]
        sg = _sigmoid(g)
        d_up = dav * (g * sg)
        d_g = dav * up * (sg * (1.0 + g * (1.0 - sg)))
        d_hc = jnp.concatenate([d_g, d_up], axis=1)
        col = lambda v: jnp.sum(v, axis=0, keepdims=True)
        return d_hc, col(d_hc), col(d_hc * h2), col(d_hc * h1), col(d_hc * hv)

    w2 = 2 * D_FF
    d_hc, d_cb, d_cw0, d_cw1, d_cw2 = _rw(fn1, [hup, da], [cw, cb], [(w2, F32)], [(1, w2)] * 4, tr=128,
                                          name="ffn_conv_gate_bwd", prev=[hup])

    def fn2(dv, cwv, halo):
        return cwv[2:3] * dv + cwv[1:2] * _shift_up(dv, halo, 1) + cwv[0:1] * _shift_up(dv, halo, 2)

    d_hup = _rw(fn2, [d_hc], [cw], [(w2, BF16)], tr=128, name="ffn_conv_bwd", nxt=[d_hc])[0]
    d_h2 = _mm(d_hup, w["ffn_w_up"][i], name="ffn_up_dx")
    d_wup_t = _mm(d_hup, h2b, ta=True, name="ffn_up_dw")
    dx1, dx1b, d_norm = _rms_bwd_call(x1, d_h2, w["norm_ffn"][i:i + 1], dx2, name="ffn_norm_bwd", bf16_copy=True)
    grads = {"ffn_w_up": d_wup_t, "ffn_w_down": d_wdown, "ffn_conv_b": d_cb[0],
             "ffn_conv_w": jnp.concatenate([d_cw0, d_cw1, d_cw2], axis=0), "norm_ffn": d_norm[0]}
    return dx1, dx1b, grads


def _ple_fwd(x2, p_i, w, i):
    rn = _rms_fwd(x2, None, name="ple_norm")
    gl = _mm(rn, w["ple_w_gate"][i], name="ple_gate")
    pp = _mm(p_i, w["ple_w_proj"][i], tb=True, name="ple_proj")
    x3 = _rw(lambda xv, g, q: xv + _sigmoid(g) * q, [x2, gl, pp], [], [(D_MODEL, F32)], tr=512, name="ple_out")[0]
    return x3, (rn, gl, pp)


def _ple_bwd(dx3, x2, p_i, saved, w, i):
    rn, gl, pp = saved

    def fn(dv, g, q):
        sg = _sigmoid(g)
        return dv * sg, dv * q * (sg * (1.0 - sg))

    d_pp, d_pre = _rw(fn, [dx3, gl, pp], [], [(D_MODEL, BF16)] * 2, tr=512, name="ple_out_bwd")
    d_wproj_t = _mm(d_pp, p_i, ta=True, name="ple_proj_dw")
    d_wgate = _mm(rn, d_pre, ta=True, name="ple_gate_dw")
    d_rn = _mm(d_pre, w["ple_w_gate"][i], tb=True, name="ple_gate_dx")
    dx2, dx2b, _ = _rms_bwd_call(x2, d_rn, None, dx3, name="ple_norm_bwd", bf16_copy=True)
    return dx2, dx2b, {"ple_w_proj": d_wproj_t, "ple_w_gate": d_wgate}


def _loss_and_grad(xf, tgt):
    def fn(xv, tv):
        d = xv - tv
        part = 0.5 * jnp.sum(jnp.mean(d * d, axis=-1, keepdims=True), axis=0, keepdims=True)
        return d * (1.0 / D_MODEL), part

    dx, loss = _rw(fn, [xf, tgt], [], [(D_MODEL, F32)], [(1, 1)], tr=512, name="loss_head")
    return loss[0, 0], dx


def _tril_mask():
    shp = (A_CHUNK, A_CHUNK)
    return lax.broadcasted_iota(jnp.int32, shp, 0) >= lax.broadcasted_iota(jnp.int32, shp, 1)


def _gmlp_fwd(z, w_s, b_s, v_gain, *, tr=512):
    s = z.shape[0]
    tr = min(tr, s)
    gw = A_WIDTH // A_GROUPS

    def body(z_ref, w_ref, b_ref, g_ref, y_ref):
        tril = _tril_mask()
        for g in range(A_GROUPS):
            wg = jnp.where(tril, w_ref[g], 0.0).astype(BF16)
            cols = slice(g * gw, (g + 1) * gw)
            vcols = slice(A_WIDTH + g * gw, A_WIDTH + (g + 1) * gw)
            for c in range(tr // A_CHUNK):
                rows = slice(c * A_CHUNK, (c + 1) * A_CHUNK)
                v = _gelu(z_ref[rows, vcols])
                vn = v * _rstd(v) * g_ref[:, cols]
                sv = jnp.dot(wg, vn.astype(BF16), preferred_element_type=F32) + b_ref[g]
                y_ref[rows, cols] = (_gelu(z_ref[rows, cols]) * sv).astype(BF16)

    return pl.pallas_call(
        body, grid=(s // tr,),
        in_specs=[pl.BlockSpec((tr, 2 * A_WIDTH), lambda i: (i, 0)),
                  pl.BlockSpec(w_s.shape, lambda i: (0, 0, 0)), pl.BlockSpec(b_s.shape, lambda i: (0, 0, 0)),
                  pl.BlockSpec(v_gain.shape, lambda i: (0, 0))],
        out_specs=pl.BlockSpec((tr, A_WIDTH), lambda i: (i, 0)),
        out_shape=jax.ShapeDtypeStruct((s, A_WIDTH), BF16), name="gmlp_fwd", compiler_params=_params(("parallel",)),
    )(z, w_s, b_s, v_gain)


def _gmlp_bwd(z, d_ymix, w_s, b_s, v_gain, *, tr=512):
    s = z.shape[0]
    tr = min(tr, s)
    gw = A_WIDTH // A_GROUPS

    def body(z_ref, dy_ref, w_ref, b_ref, g_ref, dz_ref, dw_ref, db_ref, dg_ref):
        @pl.when(pl.program_id(0) == 0)
        def _():
            dw_ref[...] = jnp.zeros_like(dw_ref)
            db_ref[...] = jnp.zeros_like(db_ref)
            dg_ref[...] = jnp.zeros_like(dg_ref)

        tril = _tril_mask()
        for g in range(A_GROUPS):
            wg = jnp.where(tril, w_ref[g], 0.0).astype(BF16)
            cols = slice(g * gw, (g + 1) * gw)
            vcols = slice(A_WIDTH + g * gw, A_WIDTH + (g + 1) * gw)
            gain = g_ref[:, cols]
            for c in range(tr // A_CHUNK):
                rows = slice(c * A_CHUNK, (c + 1) * A_CHUNK)
                va, ua = z_ref[rows, vcols], z_ref[rows, cols]
                v = _gelu(va)
                r = _rstd(v)
                vh = v * r
                vnb = (vh * gain).astype(BF16)
                sv = jnp.dot(wg, vnb, preferred_element_type=F32) + b_ref[g]
                dy = dy_ref[rows, cols]
                d_sv = dy * _gelu(ua)
                dz_ref[rows, cols] = (dy * sv * _gelu_grad(ua)).astype(BF16)
                d_svb = d_sv.astype(BF16)
                d_vn = lax.dot_general(wg, d_svb, (((0,), (0,)), ((), ())), preferred_element_type=F32)
                dwp = lax.dot_general(d_svb, vnb, (((1,), (1,)), ((), ())), preferred_element_type=F32)
                dw_ref[g] += jnp.where(tril, dwp, 0.0)
                db_ref[g] += jnp.sum(d_sv, axis=1, keepdims=True)
                dg_ref[:, cols] += jnp.sum(d_vn * vh, axis=0, keepdims=True)
                d_vh = d_vn * gain
                d_v = r * (d_vh - vh * jnp.mean(d_vh * vh, axis=-1, keepdims=True))
                dz_ref[rows, vcols] = (d_v * _gelu_grad(va)).astype(BF16)

    return pl.pallas_call(
        body, grid=(s // tr,),
        in_specs=[pl.BlockSpec((tr, 2 * A_WIDTH), lambda i: (i, 0)), pl.BlockSpec((tr, A_WIDTH), lambda i: (i, 0)),
                  pl.BlockSpec(w_s.shape, lambda i: (0, 0, 0)), pl.BlockSpec(b_s.shape, lambda i: (0, 0, 0)),
                  pl.BlockSpec(v_gain.shape, lambda i: (0, 0))],
        out_specs=[pl.BlockSpec((tr, 2 * A_WIDTH), lambda i: (i, 0)), pl.BlockSpec(w_s.shape, lambda i: (0, 0, 0)),
                   pl.BlockSpec(b_s.shape, lambda i: (0, 0, 0)), pl.BlockSpec(v_gain.shape, lambda i: (0, 0))],
        out_shape=[jax.ShapeDtypeStruct((s, 2 * A_WIDTH), BF16), jax.ShapeDtypeStruct(w_s.shape, F32),
                   jax.ShapeDtypeStruct(b_s.shape, F32), jax.ShapeDtypeStruct(v_gain.shape, F32)],
        name="gmlp_bwd", compiler_params=_params(("arbitrary",)),
    )(z, d_ymix, w_s, b_s, v_gain)


def _dot3(x, ub):
    x1 = x.astype(BF16)
    r1 = x - x1.astype(F32)
    x2 = r1.astype(BF16)
    x3 = (r1 - x2.astype(F32)).astype(BF16)
    d = lambda a: jnp.dot(a, ub, preferred_element_type=F32)
    return d(x1) + d(x2) + d(x3)


def _log_sigmoid(x):
    return jnp.minimum(x, 0.0) - jnp.log(1.0 + jnp.exp(-jnp.abs(x)))


def _fgate_fwd(f_t, b_col, *, tb=256):
    h, s = f_t.shape
    tb = min(tb, s)

    def body(f_ref, b_ref, c_ref, carry):
        @pl.when(pl.program_id(0) == 0)
        def _():
            carry[...] = jnp.zeros_like(carry)

        lf = _log_sigmoid(f_ref[...] + b_ref[...])
        shp = (tb, tb)
        upper = (lax.broadcasted_iota(jnp.int32, shp, 0) <= lax.broadcasted_iota(jnp.int32, shp, 1)).astype(BF16)
        c_ref[...] = _dot3(lf, upper) + carry[...]
        carry[...] += jnp.sum(lf, axis=1, keepdims=True)

    return pl.pallas_call(
        body, grid=(s // tb,),
        in_specs=[pl.BlockSpec((h, tb), lambda i: (0, i)), pl.BlockSpec((h, 1), lambda i: (0, 0))],
        out_specs=pl.BlockSpec((h, tb), lambda i: (0, i)), out_shape=jax.ShapeDtypeStruct((h, s), F32),
        scratch_shapes=[pltpu.VMEM((h, 1), F32)], name="fgate_fwd", compiler_params=_params(("arbitrary",)),
    )(f_t, b_col)


def _fgate_bwd(f_t, b_col, dc_a, dc_b, *, tb=256):
    h, s = f_t.shape
    tb = min(tb, s)
    n = s // tb

    def body(f_ref, b_ref, da_ref, db_ref, df_ref, dbias_ref, carry):
        @pl.when(pl.program_id(0) == 0)
        def _():
            carry[...] = jnp.zeros_like(carry)
            dbias_ref[...] = jnp.zeros_like(dbias_ref)

        dc = da_ref[...] + db_ref[...]
        shp = (tb, tb)
        lower = (lax.broadcasted_iota(jnp.int32, shp, 0) >= lax.broadcasted_iota(jnp.int32, shp, 1)).astype(BF16)
        d_lf = _dot3(dc, lower) + carry[...]
        carry[...] += jnp.sum(dc, axis=1, keepdims=True)
        df = d_lf * (1.0 - _sigmoid(f_ref[...] + b_ref[...]))
        df_ref[...] = df
        dbias_ref[...] += jnp.sum(df, axis=1, keepdims=True)

    blk = pl.BlockSpec((h, tb), lambda i: (0, n - 1 - i))
    return pl.pallas_call(
        body, grid=(n,), in_specs=[blk, pl.BlockSpec((h, 1), lambda i: (0, 0)), blk, blk],
        out_specs=[blk, pl.BlockSpec((h, 1), lambda i: (0, 0))],
        out_shape=[jax.ShapeDtypeStruct((h, s), F32), jax.ShapeDtypeStruct((h, 1), F32)],
        scratch_shapes=[pltpu.VMEM((h, 1), F32)], name="fgate_bwd", compiler_params=_params(("arbitrary",)),
    )(f_t, b_col, dc_a, dc_b)


_NT = (((1,), (1,)), ((), ()))


def _causal(shape, row0, col0, transposed):
    r = lax.broadcasted_iota(jnp.int32, shape, 0) + row0
    c = lax.broadcasted_iota(jnp.int32, shape, 1) + col0
    return (r <= c) if transposed else (c <= r)


_ATT_SCALE = B_HEAD_DIM ** -0.5
ATT_W = 128
_COL_CQ, _COL_CK, _COL_LSE, _COL_DELTA = 64, 67, 70, 64
_ATT_SUB = 256


def _split3(x):
    h = x.astype(BF16).astype(F32)
    r = x - h
    m = r.astype(BF16).astype(F32)
    return h, m, (r - m).astype(BF16).astype(F32)


def _put_cols(base, lane, col0, parts):
    for t, pv in enumerate(parts):
        base = jnp.where(lane == col0 + t, pv, base)
    return base


def _attn_prep(q_raw, k_raw, v_raw, c_col, gq, gk):
    def fn(q, k, v, c, gqv, gkv):
        lane = lax.broadcasted_iota(jnp.int32, q.shape, 1)
        rq = lax.rsqrt(jnp.sum(q * q, axis=-1, keepdims=True) * (1.0 / B_HEAD_DIM) + EPS)
        rk = lax.rsqrt(jnp.sum(k * k, axis=-1, keepdims=True) * (1.0 / B_HEAD_DIM) + EPS)
        ch, cm, cl = _split3(c)
        one = jnp.ones_like(c)
        qq = _put_cols(q * rq * gqv * _ATT_SCALE, lane, _COL_CQ, (ch, cm, cl))
        qq = _put_cols(qq, lane, _COL_CK, (one, one, one))
        kk = _put_cols(k * rk * gkv, lane, _COL_CQ, (one, one, one))
        kk = _put_cols(kk, lane, _COL_CK, (-ch, -cm, -cl))
        kk = _put_cols(kk, lane, _COL_LSE, (one, one, one))
        vv = _put_cols(v, lane, _COL_DELTA, (one, one, one))
        return qq, kk, vv

    return _rw(fn, [q_raw, k_raw, v_raw, c_col], [gq, gk], [(ATT_W, BF16)] * 3, tr=2048, name="attn_prep")


def _attn_bwd_prep(do_raw, o, qp, lse):
    def fn(dov, ov, qv, lv):
        lane = lax.broadcasted_iota(jnp.int32, dov.shape, 1)
        dh, dm, dl = _split3(jnp.sum(dov * ov, axis=-1, keepdims=True))
        lh, lm, ll = _split3(lv)
        return (_put_cols(dov, lane, _COL_DELTA, (-dh, -dm, -dl)),
                _put_cols(qv.astype(F32), lane, _COL_LSE, (-lh, -lm, -ll)))

    return _rw(fn, [do_raw, o, qp, lse], [], [(ATT_W, BF16)] * 2, tr=2048, name="attn_bwd_prep")


def _attn_fwd(qp, kp, vp, *, blk=512):
    h, s, d = qp.shape
    b = min(blk, s)
    n = s // b
    sub = min(_ATT_SUB, b)
    ns = b // sub

    def body(q_ref, k_ref, v_ref, o_ref, lse_ref):
        i = pl.program_id(1)
        qs = [q_ref[0, t * sub:(t + 1) * sub, :] for t in range(ns)]

        def tile(j, state, masked):
            off = pl.multiple_of(j * b, b)
            kk = k_ref[0, pl.ds(off, b), :]
            vv = v_ref[0, pl.ds(off, b), :]
            new = []
            for t in range(ns):
                m, l, acc = state[3 * t:3 * t + 3]
                sc = lax.dot_general(qs[t], kk, _NT, preferred_element_type=F32)
                if masked:
                    sc = jnp.where(_causal((sub, b), t * sub, 0, False), sc, NEG_INF)
                m_new = jnp.maximum(m, jnp.max(sc, axis=1, keepdims=True))
                alpha = jnp.exp(m - m_new)
                p = jnp.exp(sc - m_new)
                new += [m_new, alpha * l + jnp.sum(p, axis=1, keepdims=True),
                        alpha * acc + jnp.dot(p.astype(BF16), vv, preferred_element_type=F32)]
            return tuple(new)

        init = (jnp.full((sub, 1), NEG_INF, F32), jnp.zeros((sub, 1), F32), jnp.zeros((sub, d), F32)) * ns
        state = lax.fori_loop(0, i, lambda j, st: tile(j, st, False), init)
        state = tile(i, state, True)
        for t in range(ns):
            m, l, acc = state[3 * t:3 * t + 3]
            o_ref[0, t * sub:(t + 1) * sub, :] = acc / l
            lse_ref[0, t * sub:(t + 1) * sub, :] = m + jnp.log(l)

    blk_spec = pl.BlockSpec((1, b, d), lambda hh, i: (hh, i, 0))
    full = pl.BlockSpec((1, s, d), lambda hh, i: (hh, 0, 0))
    return pl.pallas_call(
        body, grid=(h, n), in_specs=[blk_spec, full, full],
        out_specs=[blk_spec, pl.BlockSpec((1, b, 1), lambda hh, i: (hh, i, 0))],
        out_shape=[jax.ShapeDtypeStruct((h, s, d), F32), jax.ShapeDtypeStruct((h, s, 1), F32)],
        name="attn_fwd", compiler_params=_params(("parallel", "arbitrary")),
    )(qp, kp, vp)


def _attn_bwd_dq(qpp, kp, vp, dop, *, blk=512):
    h, s, d = qpp.shape
    b = min(blk, s)
    n = s // b
    sub = min(_ATT_SUB, b)
    ns = b // sub

    def body(q_ref, do_ref, k_ref, v_ref, dq_ref):
        i = pl.program_id(1)
        qs = [q_ref[0, t * sub:(t + 1) * sub, :] for t in range(ns)]
        dos = [do_ref[0, t * sub:(t + 1) * sub, :] for t in range(ns)]

        def tile(j, state, masked):
            off = pl.multiple_of(j * b, b)
            kk = k_ref[0, pl.ds(off, b), :]
            vv = v_ref[0, pl.ds(off, b), :]
            new = []
            for t in range(ns):
                p = jnp.exp(lax.dot_general(qs[t], kk, _NT, preferred_element_type=F32))
                if masked:
                    p = jnp.where(_causal((sub, b), t * sub, 0, False), p, 0.0)
                ds = p * lax.dot_general(dos[t], vv, _NT, preferred_element_type=F32)
                new.append(state[t] + jnp.dot(ds.astype(BF16), kk, preferred_element_type=F32))
            return tuple(new)

        state = lax.fori_loop(0, i, lambda j, st: tile(j, st, False), (jnp.zeros((sub, d), F32),) * ns)
        state = tile(i, state, True)
        for t in range(ns):
            dq_ref[0, t * sub:(t + 1) * sub, :] = state[t]

    blk_spec = pl.BlockSpec((1, b, d), lambda hh, i: (hh, i, 0))
    full = pl.BlockSpec((1, s, d), lambda hh, i: (hh, 0, 0))
    return pl.pallas_call(
        body, grid=(h, n), in_specs=[blk_spec, blk_spec, full, full], out_specs=blk_spec,
        out_shape=jax.ShapeDtypeStruct((h, s, d), F32), name="attn_bwd_dq",
        compiler_params=_params(("parallel", "arbitrary")),
    )(qpp, dop, kp, vp)


def _attn_bwd_dkv(qpp, kp, vp, dop, *, blk=512):
    h, s, d = qpp.shape
    b = min(blk, s)
    n = s // b
    sub = min(_ATT_SUB, b)
    ns = b // sub

    def body(k_ref, v_ref, q_ref, do_ref, dk_ref, dv_ref):
        j = pl.program_id(1)
        ks = [k_ref[0, t * sub:(t + 1) * sub, :] for t in range(ns)]
        vs = [v_ref[0, t * sub:(t + 1) * sub, :] for t in range(ns)]

        def tile(i, state, masked):
            off = pl.multiple_of(i * b, b)
            qq = q_ref[0, pl.ds(off, b), :]
            dd = do_ref[0, pl.ds(off, b), :]
            new = []
            for t in range(ns):
                pt = jnp.exp(lax.dot_general(ks[t], qq, _NT, preferred_element_type=F32))
                if masked:
                    pt = jnp.where(_causal((sub, b), t * sub, 0, True), pt, 0.0)
                dst = pt * lax.dot_general(vs[t], dd, _NT, preferred_element_type=F32)
                new += [state[2 * t] + jnp.dot(dst.astype(BF16), qq, preferred_element_type=F32),
                        state[2 * t + 1] + jnp.dot(pt.astype(BF16), dd, preferred_element_type=F32)]
            return tuple(new)

        state = tile(j, (jnp.zeros((sub, d), F32),) * (2 * ns), True)
        state = lax.fori_loop(j + 1, n, lambda i, st: tile(i, st, False), state)
        for t in range(ns):
            dk_ref[0, t * sub:(t + 1) * sub, :] = state[2 * t]
            dv_ref[0, t * sub:(t + 1) * sub, :] = state[2 * t + 1]

    blk_spec = pl.BlockSpec((1, b, d), lambda hh, j: (hh, j, 0))
    full = pl.BlockSpec((1, s, d), lambda hh, j: (hh, 0, 0))
    return pl.pallas_call(
        body, grid=(h, n), in_specs=[blk_spec, blk_spec, full, full], out_specs=[blk_spec, blk_spec],
        out_shape=[jax.ShapeDtypeStruct((h, s, d), F32)] * 2, name="attn_bwd_dkv",
        compiler_params=_params(("parallel", "arbitrary")),
    )(kp, vp, qpp, dop)


def _heads_padded(a):
    s = a.shape[0]
    t = a.reshape(s, B_HEADS, B_HEAD_DIM).transpose(1, 0, 2)
    return jnp.pad(t, ((0, 0), (0, 0), (0, ATT_W - B_HEAD_DIM))).reshape(B_HEADS * s, ATT_W)


def _unheads(a):
    s = a.shape[1]
    return a.transpose(1, 0, 2).reshape(s, B_WIDTH)


def _pad_head(g):
    return jnp.pad(g, ((0, 0), (0, ATT_W - B_HEAD_DIM)))


def _even_fwd(x0, w, e, i):
    s = x0.shape[0]
    hs = B_HEADS * s
    hb = _rms_fwd(x0, w["norm_mix"][i:i + 1], name="mix_norm")
    z = _mm(hb, w["ev_w_in"][e], tb=True, name="ev_in", bn=896)
    b_s = w["ev_b_spatial"][e][:, :, None]
    v_gain = w["ev_v_norm"][e:e + 1]
    ya = _gmlp_fwd(z, w["ev_w_spatial"][e], b_s, v_gain)
    q_raw = _heads_padded(z[:, 2 * A_WIDTH:2 * A_WIDTH + B_WIDTH])
    k_raw = _heads_padded(z[:, 2 * A_WIDTH + B_WIDTH:2 * A_WIDTH + 2 * B_WIDTH])
    v_raw = _heads_padded(z[:, 2 * A_WIDTH + 2 * B_WIDTH:2 * A_WIDTH + 3 * B_WIDTH])
    f_t = z[:, IN_COLS - B_HEADS:IN_COLS].T
    c = _fgate_fwd(f_t, w["ev_b_fgate"][e][:, None])
    qp, kp, vp = _attn_prep(q_raw, k_raw, v_raw, c.reshape(hs, 1), _pad_head(w["ev_q_norm"][e:e + 1]),
                            _pad_head(w["ev_k_norm"][e:e + 1]))
    shp = (B_HEADS, s, ATT_W)
    o, lse = _attn_fwd(qp.reshape(shp), kp.reshape(shp), vp.reshape(shp))
    ymix = jnp.concatenate([ya, _unheads(o[:, :, :B_HEAD_DIM]).astype(BF16)], axis=1)
    x1 = _mm(ymix, w["ev_w_out"][e], add=x0, name="ev_out")
    return x1, (hb, z, ymix, q_raw, k_raw, qp, kp, vp, f_t, o, lse)


def _even_bwd(dx1, dx1b, x0, saved, w, e, i):
    hb, z, ymix, q_raw, k_raw, qp, kp, vp, f_t, o, lse = saved
    s = x0.shape[0]
    hs = B_HEADS * s
    hd = B_HEAD_DIM
    d_ymix = _mm(dx1b, w["ev_w_out"][e], tb=True, name="ev_out_dx")
    d_wout = _mm(ymix, dx1b, ta=True, name="ev_out_dw")
    do_raw = _heads_padded(d_ymix[:, A_WIDTH:])
    dop, qpp = _attn_bwd_prep(do_raw, o.reshape(hs, ATT_W), qp, lse.reshape(hs, 1))
    shp = (B_HEADS, s, ATT_W)
    qpp, dop, kp3, vp3 = qpp.reshape(shp), dop.reshape(shp), kp.reshape(shp), vp.reshape(shp)
    dqp = _attn_bwd_dq(qpp, kp3, vp3, dop)
    dkp, dvp = _attn_bwd_dkv(qpp, kp3, vp3, dop)
    d_ft, d_bf = _fgate_bwd(f_t, w["ev_b_fgate"][e][:, None], dqp[:, :, _COL_CQ], -dkp[:, :, _COL_CK])
    dq_raw, d_gq = _rms_bwd_call(q_raw[:, :hd], dqp.reshape(hs, ATT_W)[:, :hd], w["ev_q_norm"][e:e + 1], None,
                                 name="attn_qnorm_bwd", tr=2048, scale=_ATT_SCALE)
    dk_raw, d_gk = _rms_bwd_call(k_raw[:, :hd], dkp.reshape(hs, ATT_W)[:, :hd], w["ev_k_norm"][e:e + 1], None,
                                 name="attn_knorm_bwd", tr=2048)
    b_s = w["ev_b_spatial"][e][:, :, None]
    v_gain = w["ev_v_norm"][e:e + 1]
    dz_uv, d_ws, d_bs, d_vg = _gmlp_bwd(z, d_ymix, w["ev_w_spatial"][e], b_s, v_gain)
    shp64 = (B_HEADS, s, hd)
    dz = jnp.concatenate(
        [dz_uv, _unheads(dq_raw.reshape(shp64)).astype(BF16), _unheads(dk_raw.reshape(shp64)).astype(BF16),
         _unheads(dvp[:, :, :hd]).astype(BF16), d_ft.T.astype(BF16), jnp.zeros((s, IN_COLS_PAD - IN_COLS), BF16)],
        axis=1)
    d_h = _mm(dz, w["ev_w_in"][e], name="ev_in_dx")
    d_win_t = _mm(dz, hb, ta=True, name="ev_in_dw", bm=896)[:IN_COLS]
    dx0, d_nm = _rms_bwd_call(x0, d_h, w["norm_mix"][i:i + 1], dx1, name="mix_norm_bwd")
    grads = {"ev_w_in": d_win_t, "ev_w_out": d_wout, "ev_b_fgate": d_bf[:, 0], "ev_q_norm": d_gq[0],
             "ev_k_norm": d_gk[0], "ev_v_norm": d_vg[0], "ev_w_spatial": d_ws, "ev_b_spatial": d_bs[:, :, 0],
             "norm_mix": d_nm[0]}
    return dx0, grads


def _s5_disc(a_re, a_im, log_dt, b_re, b_im):
    dt = jnp.exp(log_dt)[:, None]
    lr, li = a_re, a_im
    mag = jnp.exp(lr * dt)
    ab_re, ab_im = mag * jnp.cos(li * dt), mag * jnp.sin(li * dt)
    den = lr * lr + li * li
    nr, ni = ab_re - 1.0, ab_im
    cr = (nr * lr + ni * li) / den
    ci = (ni * lr - nr * li) / den
    bb_re = cr[..., None] * b_re - ci[..., None] * b_im
    bb_im = cr[..., None] * b_im + ci[..., None] * b_re
    return ab_re, ab_im, bb_re, bb_im


_GPB = S5_GROUPS // S5_BLOCKS


def _blockdiag(t):
    a, b = t.shape[1:]
    eye = jnp.eye(_GPB, dtype=t.dtype)
    t = t.reshape(S5_BLOCKS, _GPB, a, 1, b) * eye[None, :, None, :, None]
    return t.reshape(S5_BLOCKS, _GPB * a, _GPB * b)


def _blockdiag_extract(m, a, b):
    eye = jnp.eye(_GPB, dtype=m.dtype)
    m = m.reshape(S5_BLOCKS, _GPB, a, _GPB, b) * eye[None, :, None, :, None]
    return jnp.sum(m, axis=3).reshape(S5_GROUPS, a, b)


def _scan(br, bi, ar, ai, *, rev, xs=None, tb=512, wl=512, name):
    s, n = br.shape
    tb = min(tb, s)
    nt = s // tb
    with_acc = xs is not None

    def body(*refs):
        br_ref, bi_ref, ar_ref, ai_ref = refs[:4]
        k = 4
        if with_acc:
            xr_ref, xi_ref = refs[4:6]
            k = 6
        or_ref, oi_ref = refs[k:k + 2]
        k += 2
        if with_acc:
            dar_ref, dai_ref = refs[k:k + 2]
            k += 2
        cr, ci = refs[k:k + 2]

        @pl.when(pl.program_id(1) == 0)
        def _():
            cr[...] = jnp.zeros_like(cr)
            ci[...] = jnp.zeros_like(ci)
            if with_acc:
                dar_ref[...] = jnp.zeros_like(dar_ref)
                dai_ref[...] = jnp.zeros_like(dai_ref)

        a_r, a_i = ar_ref[...], ai_ref[...]

        def step(q, carry):
            row = (tb - 1 - q) if rev else q
            xr, xi = carry[0], carry[1]
            b_r, b_i = br_ref[pl.ds(row, 1), :], bi_ref[pl.ds(row, 1), :]
            extra = ()
            if rev:
                if with_acc:
                    p_r, p_i = xr_ref[pl.ds(row, 1), :], xi_ref[pl.ds(row, 1), :]
                    extra = (carry[2] + (xr * p_r + xi * p_i), carry[3] + (xi * p_r - xr * p_i))
                n_r = a_r * xr + a_i * xi + b_r
                n_i = a_r * xi - a_i * xr + b_i
            else:
                n_r = a_r * xr - a_i * xi + b_r
                n_i = a_r * xi + a_i * xr + b_i
            or_ref[pl.ds(row, 1), :] = n_r
            oi_ref[pl.ds(row, 1), :] = n_i
            return (n_r, n_i) + extra

        init = (cr[...], ci[...])
        if with_acc:
            init += (jnp.zeros_like(a_r), jnp.zeros_like(a_r))
        out = lax.fori_loop(0, tb, step, init, unroll=8)
        cr[...] = out[0]
        ci[...] = out[1]
        if with_acc:
            dar_ref[...] += out[2]
            dai_ref[...] += out[3]

    tmap = (lambda c, t: (nt - 1 - t, c)) if rev else (lambda c, t: (t, c))
    blk = pl.BlockSpec((tb, wl), tmap)
    vec = pl.BlockSpec((1, wl), lambda c, t: (0, c))
    in_specs = [blk, blk, vec, vec] + ([blk, blk] if with_acc else [])
    out_specs = [blk, blk] + ([vec, vec] if with_acc else [])
    out_shape = [jax.ShapeDtypeStruct((s, n), F32)] * 2 + ([jax.ShapeDtypeStruct((1, n), F32)] * 2 if with_acc else [])
    return pl.pallas_call(
        body, grid=(n // wl, nt), in_specs=in_specs, out_specs=out_specs, out_shape=out_shape,
        scratch_shapes=[pltpu.VMEM((1, wl), F32), pltpu.VMEM((1, wl), F32)], name=name,
        compiler_params=_params(("parallel", "arbitrary")),
    )(br, bi, ar, ai, *(xs or ()))


def _s5_mats(w, o):
    ab_re, ab_im, bb_re, bb_im = _s5_disc(w["od_a_re"][o], w["od_a_im"][o], w["od_log_dt"][o], w["od_b_re"][o],
                                          w["od_b_im"][o])
    c_re, c_im = w["od_c_re"][o], w["od_c_im"][o]
    tr = lambda t: t.transpose(0, 2, 1)
    return {
        "ar": ab_re.reshape(1, S5_N), "ai": ab_im.reshape(1, S5_N),
        "wb_re": _blockdiag(tr(bb_re)), "wb_im": _blockdiag(tr(bb_im)),
        "wc_re": _blockdiag(tr(c_re)), "wc_imn": _blockdiag(-tr(c_im)),
        "wct_re": _blockdiag(c_re), "wct_imn": _blockdiag(-c_im),
        "wbt_re": _blockdiag(bb_re), "wbt_im": _blockdiag(bb_im),
    }


def _odd_fwd(x0, w, o, i):
    mats = _s5_mats(w, o)
    nm = w["norm_mix"][i:i + 1]
    d_row = w["od_d"][o:o + 1]
    hb = _rms_fwd(x0, nm, name="mix_norm")
    bur = _bdmm(hb, mats["wb_re"], name="s5_b_re")
    bui = _bdmm(hb, mats["wb_im"], name="s5_b_im")
    xr, xi = _scan(bur, bui, mats["ar"], mats["ai"], rev=False, name="s5_scan")
    cp = _bdmm(xi, mats["wc_imn"], add=_bdmm(xr, mats["wc_re"], name="s5_c_re"), name="s5_c_im")

    def fn(xv, cpv, gv, dv):
        y = cpv + dv * (xv * _rstd(xv) * gv)
        return y, _gelu(y)

    y, gy = _rw(fn, [x0, cp], [nm, d_row], [(D_MODEL, F32), (D_MODEL, BF16)], tr=512, name="s5_out")
    gg = _mm(gy, w["od_w_glu"][o], tb=True, name="od_glu")
    x1 = _rw(lambda xv, g: xv + g[:, :D_MODEL] * _sigmoid(g[:, D_MODEL:]), [x0, gg], [], [(D_MODEL, F32)], tr=512,
             name="od_glu_out")[0]
    return x1, (hb, xr, xi, y, gy, gg)


def _odd_bwd(dx1, x0, saved, w, o, i):
    hb, xr, xi, y, gy, gg = saved
    mats = _s5_mats(w, o)
    nm = w["norm_mix"][i:i + 1]
    d_row = w["od_d"][o:o + 1]

    def fn_glu(dv, g):
        ga, gb = g[:, :D_MODEL], g[:, D_MODEL:]
        sg = _sigmoid(gb)
        return jnp.concatenate([dv * sg, dv * ga * (sg * (1.0 - sg))], axis=1)

    dgg = _rw(fn_glu, [dx1, gg], [], [(2 * D_MODEL, BF16)], tr=512, name="od_glu_out_bwd")[0]
    d_wglu = _mm(dgg, gy, ta=True, name="od_glu_dw")
    d_gy = _mm(dgg, w["od_w_glu"][o], name="od_glu_dx")

    def fn_y(dg, yv, xv, gv, dv):
        dy = dg * _gelu_grad(yv)
        h = xv * _rstd(xv) * gv
        return dy, dv * dy, jnp.sum(dy * h, axis=0, keepdims=True)

    dyb, dhd, d_d = _rw(fn_y, [d_gy, y, x0], [nm, d_row], [(D_MODEL, BF16), (D_MODEL, F32)], [(1, D_MODEL)], tr=512,
                        name="s5_out_bwd")
    dxr = _bdmm(dyb, mats["wct_re"], name="s5_c_re_dx")
    dxi = _bdmm(dyb, mats["wct_imn"], name="s5_c_im_dx")
    gc, gp = S5_GROUP_CH, S5_STATE
    d_c_re = _blockdiag_extract(_bdwg(dyb, xr, _GPB * gc, _GPB * gp, name="s5_c_re_dw"), gc, gp)
    d_c_im = -_blockdiag_extract(_bdwg(dyb, xi, _GPB * gc, _GPB * gp, name="s5_c_im_dw"), gc, gp)
    lr, li, d_ar, d_ai = _scan(dxr, dxi, mats["ar"], mats["ai"], rev=True, xs=(xr, xi), name="s5_scan_bwd")
    d_bb_re = _blockdiag_extract(_bdwg(hb, lr, _GPB * gc, _GPB * gp, name="s5_b_re_dw"), gc, gp).transpose(0, 2, 1)
    d_bb_im = _blockdiag_extract(_bdwg(hb, li, _GPB * gc, _GPB * gp, name="s5_b_im_dw"), gc, gp).transpose(0, 2, 1)
    dh = _bdmm(li, mats["wbt_im"], add=_bdmm(lr, mats["wbt_re"], add=dhd, name="s5_b_re_dx"), name="s5_b_im_dx")
    dx0, d_nm = _rms_bwd_call(x0, dh, nm, dx1, name="mix_norm_bwd")
    _, vjp = jax.vjp(_s5_disc, w["od_a_re"][o], w["od_a_im"][o], w["od_log_dt"][o], w["od_b_re"][o], w["od_b_im"][o])
    d_a_re, d_a_im, d_log_dt, d_b_re, d_b_im = vjp(
        (d_ar.reshape(S5_GROUPS, S5_STATE), d_ai.reshape(S5_GROUPS, S5_STATE), d_bb_re, d_bb_im))
    grads = {"od_a_re": d_a_re, "od_a_im": d_a_im, "od_log_dt": d_log_dt, "od_b_re": d_b_re, "od_b_im": d_b_im,
             "od_c_re": d_c_re, "od_c_im": d_c_im, "od_d": d_d[0], "od_w_glu": d_wglu, "norm_mix": d_nm[0]}
    return dx0, grads


_HBM = pl.BlockSpec(memory_space=pltpu.HBM)


def _mesh_pos():
    return lax.axis_index("x"), lax.axis_index("y"), lax.axis_index("c")


def _slot(px, py, pc):
    return 4 * px + 2 * py + pc


def _all_gather(x, *, name):
    def body(x_ref, out_ref, send_sems, recv_sems, local_sem):
        mx, my, mc = _mesh_pos()
        me, sibling = (mx, my, mc), (mx, my, 1 - mc)
        chips = [(1 - mx, my), (mx, 1 - my), (1 - mx, 1 - my)]

        def copy(k, block, to, src=None):
            dst = out_ref.at[_slot(*block)]
            return pltpu.make_async_remote_copy(
                src_ref=dst if src is None else src, dst_ref=dst, send_sem=send_sems.at[k], recv_sem=recv_sems.at[k],
                device_id=to, device_id_type=MESH)

        mine = pltpu.make_async_copy(x_ref, out_ref.at[_slot(*me)], local_sem)
        mine.start()
        first = [copy(0, me, sibling, src=x_ref)]
        first += [copy(1 + j, me, (*chip, mc), src=x_ref) for j, chip in enumerate(chips)]
        for cp in first:
            cp.start()
        passed = [copy(4 + j, (*chip, mc), sibling) for j, chip in enumerate(chips)]
        for j, chip in enumerate(chips):
            copy(1 + j, (*chip, mc), me).wait_recv()
            passed[j].start()
        copy(0, sibling, me).wait_recv()
        for j, chip in enumerate(chips):
            copy(4 + j, (*chip, 1 - mc), me).wait_recv()
        for cp in first + passed:
            cp.wait_send()
        mine.wait()

    return pl.pallas_call(
        body, out_shape=jax.ShapeDtypeStruct((N_DEV,) + x.shape, x.dtype), in_specs=[_HBM], out_specs=_HBM,
        scratch_shapes=[pltpu.SemaphoreType.DMA((7,)), pltpu.SemaphoreType.DMA((7,)), pltpu.SemaphoreType.DMA],
        name=name,
    )(x)


def _all_to_all(x, *, name):
    def body(x_ref, out_ref, send_sems, recv_sems, local_sem):
        mx, my, mc = _mesh_pos()
        my_slot = _slot(mx, my, mc)
        mine = pltpu.make_async_copy(x_ref.at[my_slot], out_ref.at[my_slot], local_sem)
        mine.start()
        copies = []
        for k in range(1, N_DEV):
            peer = (1 - mx if k & 4 else mx, 1 - my if k & 2 else my, 1 - mc if k & 1 else mc)
            copies.append(pltpu.make_async_remote_copy(
                src_ref=x_ref.at[_slot(*peer)], dst_ref=out_ref.at[my_slot], send_sem=send_sems.at[k - 1],
                recv_sem=recv_sems.at[k - 1], device_id=peer, device_id_type=MESH))
        for cp in copies:
            cp.start()
        for cp in copies:
            cp.wait()
        mine.wait()

    return pl.pallas_call(
        body, out_shape=jax.ShapeDtypeStruct(x.shape, x.dtype), in_specs=[_HBM], out_specs=_HBM,
        scratch_shapes=[pltpu.SemaphoreType.DMA((7,)), pltpu.SemaphoreType.DMA((7,)), pltpu.SemaphoreType.DMA],
        name=name,
    )(x)


_ADAMW_PARTS_BLOCK_BYTES = 8 * 1024 * 1024


def _adamw(parts, w, m, v, *, name):
    r, c = w.shape
    tr = r
    for cand in range(8, r + 1, 8):
        if r % cand == 0 and N_DEV * cand * c * 4 <= _ADAMW_PARTS_BLOCK_BYTES:
            tr = cand

    def body(p_ref, w_ref, m_ref, v_ref, g_ref, d_ref, nm_ref, nv_ref):
        g = p_ref[0]
        for k in range(1, N_DEV):
            g = g + p_ref[k]
        nm = ADAM_B1 * m_ref[...] + (1.0 - ADAM_B1) * g
        nv = ADAM_B2 * v_ref[...] + (1.0 - ADAM_B2) * (g * g)
        m_hat = nm / (1.0 - ADAM_B1 ** ADAM_STEP)
        v_hat = nv / (1.0 - ADAM_B2 ** ADAM_STEP)
        g_ref[...] = g
        d_ref[...] = -ADAM_LR * (m_hat / (jnp.sqrt(v_hat) + ADAM_EPS) + ADAM_WD * w_ref[...])
        nm_ref[...] = nm
        nv_ref[...] = nv

    blk = pl.BlockSpec((tr, c), lambda i: (i, 0))
    return pl.pallas_call(
        body, grid=(r // tr,), in_specs=[pl.BlockSpec((N_DEV, tr, c), lambda i: (0, i, 0)), blk, blk, blk],
        out_specs=[blk] * 4, out_shape=[jax.ShapeDtypeStruct((r, c), F32)] * 4, name=name,
        compiler_params=_params(("parallel",)),
    )(parts, w, m, v)


_FLAT_COLS = 1024
_FLAT_ROW_ALIGN = 128

_SHARD_AXIS = {
    "norm_mix": None, "norm_ffn": None, "ev_w_in": 2, "ev_b_fgate": None, "ev_q_norm": None, "ev_k_norm": None,
    "ev_v_norm": None, "ev_w_spatial": None, "ev_b_spatial": None, "ev_w_out": 1, "od_a_re": None, "od_a_im": None,
    "od_log_dt": None, "od_b_re": None, "od_b_im": None, "od_c_re": None, "od_c_im": None, "od_d": 1, "od_w_glu": 2,
    "ffn_w_up": 2, "ffn_conv_w": 2, "ffn_conv_b": None, "ffn_w_down": 1, "ple_w_proj": 2, "ple_w_gate": 1,
}
_WEIGHTS = list(_SHARD_AXIS)
_REPLICATED = [n for n in _WEIGHTS if _SHARD_AXIS[n] is None]
_COL_SHARDED = ["ev_w_in", "od_w_glu", "ffn_w_up", "ple_w_proj"]
_ROW_SHARDED = ["ev_w_out", "ffn_w_down", "ple_w_gate"]
_SMALL_SHARDED = ["od_d", "ffn_conv_w"]


def _pack(arrays, lead=()):
    nl = len(lead)
    flat = jnp.concatenate([a.reshape(lead + (-1,)) for a in arrays], axis=nl)
    n = flat.shape[nl]
    chunk = _FLAT_COLS * _FLAT_ROW_ALIGN
    total = -(-n // chunk) * chunk
    flat = jnp.pad(flat, [(0, 0)] * nl + [(0, total - n)])
    return flat.reshape(lead + (total // _FLAT_COLS, _FLAT_COLS))


def _unpack(buf, shapes, lead=()):
    nl = len(lead)
    flat = buf.reshape(lead + (-1,))
    out, off = [], 0
    for shp in shapes:
        n = math.prod(shp)
        out.append(lax.slice_in_dim(flat, off, off + n, axis=nl).reshape(lead + tuple(shp)))
        off += n
    return out


def _join_shards(g, axis):
    g = jnp.moveaxis(g, 0, axis)
    shp = g.shape
    return g.reshape(shp[:axis] + (shp[axis] * shp[axis + 1],) + shp[axis + 2:])


def _split_shards(full, axis):
    shp = full.shape
    g = full.reshape(shp[:axis] + (N_DEV, shp[axis] // N_DEV) + shp[axis + 1:])
    return jnp.moveaxis(g, axis, 0)


def _local_step(x, p, tgt, w):
    saved = []
    h = x
    for i in range(DEPTH):
        x0 = h
        if i % 2 == 0:
            x1, sv_mix = _even_fwd(x0, w, i // 2, i)
        else:
            x1, sv_mix = _odd_fwd(x0, w, i // 2, i)
        x2, sv_ffn = _ffn_fwd(x1, w, i)
        x3, sv_ple = _ple_fwd(x2, p[i], w, i)
        saved.append((x0, x1, x2, sv_mix, sv_ffn, sv_ple))
        h = x3
    loss, dh = _loss_and_grad(h, tgt)
    per_layer = {}
    for i in reversed(range(DEPTH)):
        x0, x1, x2, sv_mix, sv_ffn, sv_ple = saved[i]
        dx2, dx2b, g_ple = _ple_bwd(dh, x2, p[i], sv_ple, w, i)
        dx1, dx1b, g_ffn = _ffn_bwd(dx2, dx2b, x1, sv_ffn, w, i)
        if i % 2 == 0:
            dh, g_mix = _even_bwd(dx1, dx1b, x0, sv_mix, w, i // 2, i)
        else:
            dh, g_mix = _odd_bwd(dx1, x0, sv_mix, w, i // 2, i)
        for name, g in {**g_ple, **g_ffn, **g_mix}.items():
            per_layer.setdefault(name, []).append(g)
    grads = {name: gs[::-1] for name, gs in per_layer.items()}
    return loss, dh, grads


def kernel(x, p, norm_mix, norm_ffn, ev_w_in, ev_b_fgate, ev_q_norm, ev_k_norm, ev_v_norm, ev_w_spatial, ev_b_spatial, ev_w_out, od_a_re, od_a_im, od_log_dt, od_b_re, od_b_im, od_c_re, od_c_im, od_d, od_w_glu, ffn_w_up, ffn_conv_w, ffn_conv_b, ffn_w_down, ple_w_proj, ple_w_gate, loss_target, m_norm_mix, m_norm_ffn, m_ev_w_in, m_ev_b_fgate, m_ev_q_norm, m_ev_k_norm, m_ev_v_norm, m_ev_w_spatial, m_ev_b_spatial, m_ev_w_out, m_od_a_re, m_od_a_im, m_od_log_dt, m_od_b_re, m_od_b_im, m_od_c_re, m_od_c_im, m_od_d, m_od_w_glu, m_ffn_w_up, m_ffn_conv_w, m_ffn_conv_b, m_ffn_w_down, m_ple_w_proj, m_ple_w_gate, v_norm_mix, v_norm_ffn, v_ev_w_in, v_ev_b_fgate, v_ev_q_norm, v_ev_k_norm, v_ev_v_norm, v_ev_w_spatial, v_ev_b_spatial, v_ev_w_out, v_od_a_re, v_od_a_im, v_od_log_dt, v_od_b_re, v_od_b_im, v_od_c_re, v_od_c_im, v_od_d, v_od_w_glu, v_ffn_w_up, v_ffn_conv_w, v_ffn_conv_b, v_ffn_w_down, v_ple_w_proj, v_ple_w_gate):
    args = locals()
    wl = {n: args[n] for n in _WEIGHTS}
    ml = {n: args["m_" + n] for n in _WEIGHTS}
    vl = {n: args["v_" + n] for n in _WEIGHTS}

    kinds = ("g", "d", "m", "v")

    def local(n, layer, a):
        return a[n][layer].T if n in _COL_SHARDED else a[n][layer]

    w = {n: wl[n] for n in _REPLICATED}
    for n in _COL_SHARDED + _ROW_SHARDED:
        w[n] = []
        for layer in range(wl[n].shape[0]):
            g = _all_gather(local(n, layer, wl).astype(BF16), name="gather_" + n)
            full = g.reshape(N_DEV * g.shape[1], g.shape[2])
            if n == "ev_w_in":
                full = jnp.pad(full, ((0, IN_COLS_PAD - IN_COLS), (0, 0)))
            w[n].append(full)
    small = _all_gather(_pack([wl[n] for n in _SMALL_SHARDED]), name="gather_small_weights")
    for n, g in zip(_SMALL_SHARDED, _unpack(small, [wl[n].shape for n in _SMALL_SHARDED], lead=(N_DEV,))):
        w[n] = _join_shards(g, _SHARD_AXIS[n])

    loss, grad_x, grads = _local_step(x[0], p[:, 0], loss_target[0], w)
    loss = lax.psum(loss, ("x", "y", "c"))

    out = {kind: {} for kind in kinds}
    for n in _COL_SHARDED + _ROW_SHARDED:
        layers = []
        for layer, g in enumerate(grads[n]):
            parts = _all_to_all(g.reshape(N_DEV, g.shape[0] // N_DEV, g.shape[1]), name="scatter_" + n)
            res = _adamw(parts, local(n, layer, wl), local(n, layer, ml), local(n, layer, vl), name="adamw_" + n)
            layers.append([r.T if n in _COL_SHARDED else r for r in res])
        for k, kind in enumerate(kinds):
            out[kind][n] = jnp.stack([lay[k] for lay in layers])

    send = _pack([_split_shards(jnp.stack(grads[n]), _SHARD_AXIS[n]) for n in _SMALL_SHARDED], lead=(N_DEV,))
    parts = _all_to_all(send, name="scatter_small_grads")
    shapes = [wl[n].shape for n in _SMALL_SHARDED]
    res = _adamw(parts, _pack([wl[n] for n in _SMALL_SHARDED]), _pack([ml[n] for n in _SMALL_SHARDED]),
                 _pack([vl[n] for n in _SMALL_SHARDED]), name="adamw_small")
    for kind, buf in zip(kinds, res):
        out[kind].update(zip(_SMALL_SHARDED, _unpack(buf, shapes)))

    parts = _all_gather(_pack([jnp.stack(grads[n]) for n in _REPLICATED]), name="gather_replicated_grads")
    shapes = [wl[n].shape for n in _REPLICATED]
    res = _adamw(parts, _pack([wl[n] for n in _REPLICATED]), _pack([ml[n] for n in _REPLICATED]),
                 _pack([vl[n] for n in _REPLICATED]), name="adamw_replicated")
    for kind, buf in zip(kinds, res):
        out[kind].update(zip(_REPLICATED, _unpack(buf, shapes)))

    return (loss, grad_x[None], *[out["g"][n] for n in _WEIGHTS], *[out["d"][n] for n in _WEIGHTS],
            *[out["m"][n] for n in _WEIGHTS], *[out["v"][n] for n in _WEIGHTS])
```

```python
import functools
import math

import jax
import jax.numpy as jnp
from jax import lax
from jax.experimental import pallas as pl
from jax.experimental.pallas import tpu as pltpu

F32 = jnp.float32
BF16 = jnp.bfloat16

D_MODEL = 1024
DEPTH = 4
A_GROUPS = 4
A_CHUNK = 128
A_WIDTH = 512
B_HEADS = 8
B_HEAD_DIM = 64
B_WIDTH = 512
IN_COLS = 2 * A_WIDTH + 3 * B_WIDTH + B_HEADS
IN_COLS_PAD = 2688
S5_GROUP_CH = 16
S5_GROUPS = 64
S5_STATE = 64
S5_N = S5_GROUPS * S5_STATE
S5_BLOCKS = 8
D_FF = 2816
PLE_DIM = 256
EPS = 1e-6
NEG_INF = -1e30
N_DEV = 8

ADAM_LR = 0.001
ADAM_B1 = 0.9
ADAM_B2 = 0.999
ADAM_EPS = 1e-08
ADAM_WD = 0.01
ADAM_STEP = 10

VMEM_LIMIT_BYTES = 56 * 1024 * 1024
MESH = pl.DeviceIdType.MESH


def _params(sem, vmem=VMEM_LIMIT_BYTES):
    return pltpu.CompilerParams(dimension_semantics=sem, vmem_limit_bytes=vmem)


_GELU_K = math.sqrt(2.0 / math.pi)


def _gelu(x):
    return x * (0.5 * (1.0 + jnp.tanh(_GELU_K * (x + 0.044715 * (x * x * x)))))


def _gelu_grad(x):
    t = jnp.tanh(_GELU_K * (x + 0.044715 * (x * x * x)))
    return 0.5 * (1.0 + t) + 0.5 * x * (1.0 - t * t) * (_GELU_K * (1.0 + 3.0 * 0.044715 * (x * x)))


def _sigmoid(x):
    return 1.0 / (1.0 + jnp.exp(-x))


def _rstd(x):
    return lax.rsqrt(jnp.mean(x * x, axis=-1, keepdims=True) + EPS)


def _rms_bwd(x, dy, g):
    r = _rstd(x)
    xh = x * r
    dyg = dy if g is None else dy * g
    dx = r * (dyg - xh * jnp.mean(dyg * xh, axis=-1, keepdims=True))
    return dx, dy * xh


def _shift_down(blk, halo, k):
    tr = blk.shape[0]
    r = pltpu.roll(blk, k, 0)
    hr = pltpu.roll(halo, k, 0)
    first = jnp.where(lax.broadcasted_iota(jnp.int32, hr.shape, 0) < k, hr, r[0:8])
    return jnp.concatenate([first, r[8:tr]], axis=0)


def _shift_up(blk, halo, k):
    tr = blk.shape[0]
    r = pltpu.roll(blk, tr - k, 0)
    hr = pltpu.roll(halo, 8 - k, 0)
    last = jnp.where(lax.broadcasted_iota(jnp.int32, hr.shape, 0) >= 8 - k, hr, r[tr - 8:tr])
    return jnp.concatenate([r[0:tr - 8], last], axis=0)


def _rw(fn, rows, consts, outs, accs=(), *, tr, name, prev=(), nxt=(), widths=None):
    s = rows[0].shape[0]
    tr = min(tr, s)
    n = s // tr
    nr, nc, npv, nnx, no, na = len(rows), len(consts), len(prev), len(nxt), len(outs), len(accs)
    widths = widths or [None] * nr

    def body(*refs):
        ins, out_refs = refs[:nr + nc + npv + nnx], refs[nr + nc + npv + nnx:]
        i = pl.program_id(0)
        vals = [r[...] for r in ins[:nr + nc]]
        vals += [jnp.where(i == 0, 0.0, r[...]) for r in ins[nr + nc:nr + nc + npv]]
        vals += [jnp.where(i == n - 1, 0.0, r[...]) for r in ins[nr + nc + npv:]]
        res = fn(*vals)
        if not isinstance(res, (tuple, list)):
            res = (res,)
        for k in range(no):
            out_refs[k][...] = res[k].astype(out_refs[k].dtype)
        if na:
            @pl.when(i == 0)
            def _():
                for k in range(na):
                    out_refs[no + k][...] = jnp.zeros_like(out_refs[no + k])

            for k in range(na):
                out_refs[no + k][...] += res[no + k]

    in_specs = []
    for a, w in zip(rows, widths):
        if w is None:
            in_specs.append(pl.BlockSpec((tr, a.shape[1]), lambda i: (i, 0)))
        else:
            in_specs.append(pl.BlockSpec((tr, w[1]), functools.partial(lambda i, cb: (i, cb), cb=w[0])))
    for c in consts:
        in_specs.append(pl.BlockSpec(c.shape, functools.partial(lambda i, nd: (0,) * nd, nd=c.ndim)))
    t8 = tr // 8
    for a in prev:
        in_specs.append(pl.BlockSpec((8, a.shape[1]), lambda i: (jnp.maximum(i * t8 - 1, 0), 0)))
    for a in nxt:
        in_specs.append(pl.BlockSpec((8, a.shape[1]), lambda i: (jnp.minimum((i + 1) * t8, s // 8 - 1), 0)))
    out_shape = [jax.ShapeDtypeStruct((s, w), dt) for w, dt in outs]
    out_specs = [pl.BlockSpec((tr, w), lambda i: (i, 0)) for w, _ in outs]
    out_shape += [jax.ShapeDtypeStruct(a, F32) for a in accs]
    out_specs += [pl.BlockSpec(a, lambda i: (0, 0)) for a in accs]
    res = pl.pallas_call(
        body, grid=(n,), in_specs=in_specs, out_specs=out_specs, out_shape=out_shape, name=name,
        compiler_params=_params(("arbitrary",) if na else ("parallel",)),
    )(*rows, *consts, *prev, *nxt)
    return res


def _pick(n, cap):
    if n <= cap:
        return n
    best = None
    for d in range(128, cap + 1, 128):
        if n % d == 0:
            best = d
    assert best is not None, (n, cap)
    return best


def _mm(a, b, *, ta=False, tb=False, out_dtype=F32, add=None, name, bm=1024, bn=512, bk=None):
    (k_dim, m) = a.shape if ta else a.shape[::-1]
    n = b.shape[0] if tb else b.shape[1]
    assert (b.shape[1] if tb else b.shape[0]) == k_dim
    if bk is None:
        bk = 1024 if ta else 2816
    bm, bn, bk = _pick(m, bm), _pick(n, bn), _pick(k_dim, bk)
    nk = k_dim // bk
    dims = (((0 if ta else 1,), (1 if tb else 0,)), ((), ()))
    has_add = add is not None

    def body(*refs):
        a_ref, b_ref = refs[0], refs[1]
        add_ref = refs[2] if has_add else None
        o_ref = refs[2 + has_add]
        p = lax.dot_general(a_ref[...].astype(BF16), b_ref[...].astype(BF16), dims, preferred_element_type=F32)

        def finish(acc):
            if has_add:
                acc = acc + add_ref[...]
            o_ref[...] = acc.astype(o_ref.dtype)

        if nk == 1:
            finish(p)
        else:
            acc_ref = refs[3 + has_add]
            kk = pl.program_id(2)

            @pl.when(kk == 0)
            def _():
                acc_ref[...] = p

            @pl.when(kk > 0)
            def _():
                acc_ref[...] += p

            @pl.when(kk == nk - 1)
            def _():
                finish(acc_ref[...])

    a_spec = pl.BlockSpec((bk, bm), lambda i, j, k: (k, i)) if ta else pl.BlockSpec((bm, bk), lambda i, j, k: (i, k))
    b_spec = pl.BlockSpec((bn, bk), lambda i, j, k: (j, k)) if tb else pl.BlockSpec((bk, bn), lambda i, j, k: (k, j))
    in_specs = [a_spec, b_spec]
    args = [a, b]
    if has_add:
        in_specs.append(pl.BlockSpec((bm, bn), lambda i, j, k: (i, j)))
        args.append(add)
    return pl.pallas_call(
        body, grid=(m // bm, n // bn, nk), in_specs=in_specs,
        out_specs=pl.BlockSpec((bm, bn), lambda i, j, k: (i, j)),
        out_shape=jax.ShapeDtypeStruct((m, n), out_dtype),
        scratch_shapes=[pltpu.VMEM((bm, bn), F32)] if nk > 1 else [],
        name=name, compiler_params=_params(("parallel", "parallel", "arbitrary")),
    )(*args)


def _bdmm(a, w, *, add=None, name, bm=1024):
    s = a.shape[0]
    nb, ka, nw = w.shape
    bm = min(bm, s)
    has_add = add is not None

    def body(*refs):
        a_ref, w_ref = refs[0], refs[1]
        o_ref = refs[2 + has_add]
        p = jnp.dot(a_ref[...].astype(BF16), w_ref[0].astype(BF16), preferred_element_type=F32)
        if has_add:
            p = p + refs[2][...]
        o_ref[...] = p

    in_specs = [pl.BlockSpec((bm, ka), lambda i, j: (i, j)), pl.BlockSpec((1, ka, nw), lambda i, j: (j, 0, 0))]
    args = [a, w]
    if has_add:
        in_specs.append(pl.BlockSpec((bm, nw), lambda i, j: (i, j)))
        args.append(add)
    return pl.pallas_call(
        body, grid=(s // bm, nb), in_specs=in_specs, out_specs=pl.BlockSpec((bm, nw), lambda i, j: (i, j)),
        out_shape=jax.ShapeDtypeStruct((s, nb * nw), F32), name=name, compiler_params=_params(("parallel", "parallel")),
    )(*args)


def _bdwg(a, b, ka, nw, *, name, bk=1024):
    s = a.shape[0]
    nb = a.shape[1] // ka
    bk = min(bk, s)

    def body(a_ref, b_ref, o_ref):
        p = lax.dot_general(a_ref[...].astype(BF16), b_ref[...].astype(BF16), (((0,), (0,)), ((), ())),
                            preferred_element_type=F32)

        @pl.when(pl.program_id(1) == 0)
        def _():
            o_ref[0] = p

        @pl.when(pl.program_id(1) > 0)
        def _():
            o_ref[0] += p

    return pl.pallas_call(
        body, grid=(nb, s // bk),
        in_specs=[pl.BlockSpec((bk, ka), lambda j, k: (k, j)), pl.BlockSpec((bk, nw), lambda j, k: (k, j))],
        out_specs=pl.BlockSpec((1, ka, nw), lambda j, k: (j, 0, 0)),
        out_shape=jax.ShapeDtypeStruct((nb, ka, nw), F32), name=name, compiler_params=_params(("parallel", "arbitrary")),
    )(a, b)


def _rms_fwd(x, g, *, name):
    if g is None:
        return _rw(lambda xv: xv * _rstd(xv), [x], [], [(x.shape[1], BF16)], tr=512, name=name)[0]
    return _rw(lambda xv, gv: xv * _rstd(xv) * gv, [x], [g], [(x.shape[1], BF16)], tr=512, name=name)[0]


def _rms_bwd_call(x, dy, g, dx_add, *, name, tr=512, scale=None, bf16_copy=False):
    w = x.shape[1]
    has_g, has_add = g is not None, dx_add is not None

    def fn(*v):
        xv, dyv = v[0], v[1]
        if scale is not None:
            dyv = dyv * scale
        gv = v[2 + has_add] if has_g else None
        dx, dg = _rms_bwd(xv, dyv, gv)
        if has_add:
            dx = dx + v[2]
        return (dx,) + ((dx,) if bf16_copy else ()) + ((jnp.sum(dg, axis=0, keepdims=True),) if has_g else ())

    rows = [x, dy] + ([dx_add] if has_add else [])
    outs = [(w, F32)] + ([(w, BF16)] if bf16_copy else [])
    res = _rw(fn, rows, [g] if has_g else [], outs, [(1, w)] if has_g else [], tr=tr, name=name)
    return tuple(res) if has_g else tuple(res) + (None,)


def _conv_taps(hup, halo, cw):
    h1 = _shift_down(hup, halo, 1)
    h2 = _shift_down(hup, halo, 2)
    return h1, h2


def _ffn_fwd(x1, w, i):
    h2b = _rms_fwd(x1, w["norm_ffn"][i:i + 1], name="ffn_norm")
    hup = _mm(h2b, w["ffn_w_up"][i], tb=True, name="ffn_up")
    cw, cb = w["ffn_conv_w"][i], w["ffn_conv_b"][i:i + 1]

    def fn(hv, cwv, cbv, halo):
        h1, h2 = _conv_taps(hv, halo, cwv)
        hc = cbv + cwv[0:1] * h2
        hc = hc + cwv[1:2] * h1
        hc = hc + cwv[2:3] * hv
        g, up = hc[:, :D_FF], hc[:, D_FF:]
        return (g * _sigmoid(g)) * up

    a = _rw(fn, [hup], [cw, cb], [(D_FF, BF16)], tr=128, name="ffn_conv_gate", prev=[hup])[0]
    x2 = _mm(a, w["ffn_w_down"][i], add=x1, name="ffn_down")
    return x2, (h2b, hup, a)


def _ffn_bwd(dx2, dx2b, x1, saved, w, i):
    h2b, hup, a = saved
    cw, cb = w["ffn_conv_w"][i], w["ffn_conv_b"][i:i + 1]
    da = _mm(dx2b, w["ffn_w_down"][i], tb=True, name="ffn_down_dx")
    d_wdown = _mm(a, dx2b, ta=True, name="ffn_down_dw", bm=1408)

    def fn1(hv, dav, cwv, cbv, h_prev, h_next, da_next):
        tr = hv.shape[0]
        he = jnp.concatenate([hv, h_next], axis=0)
        dae = jnp.concatenate([dav, da_next], axis=0)
        h1, h2 = _conv_taps(he, h_prev, cwv)
        hc = cbv + cwv[0:1] * h2
        hc = hc + cwv[1:2] * h1
        hc = hc + cwv[2:3] * he
        g, up = hc[:, :D_FF], hc[:, D_FF:]
        sg = _sigmoid(g)
        d_up = dae * (g * sg)
        d_g = dae * up * (sg * (1.0 + g * (1.0 - sg)))
        d_hce = jnp.concatenate([d_g, d_up], axis=1)
        rows_e = tr + 8
        d_hup = (cwv[2:3] * d_hce[:tr] + cwv[1:2] * pltpu.roll(d_hce, rows_e - 1, 0)[:tr]
                 + cwv[0:1] * pltpu.roll(d_hce, rows_e - 2, 0)[:tr])
        d_hc = d_hce[:tr]
        col = lambda v: jnp.sum(v, axis=0, keepdims=True)
        return d_hup, col(d_hc), col(d_hc * h2[:tr]), col(d_hc * h1[:tr]), col(d_hc * hv)

    w2 = 2 * D_FF
    d_hup, d_cb, d_cw0, d_cw1, d_cw2 = _rw(fn1, [hup, da], [cw, cb], [(w2, BF16)], [(1, w2)] * 4, tr=128,
                                           name="ffn_conv_gate_bwd", prev=[hup], nxt=[hup, da])
    d_h2 = _mm(d_hup, w["ffn_w_up"][i], name="ffn_up_dx")
    d_wup_t = _mm(h2b, d_hup, ta=True, name="ffn_up_dw", bn=1408).T
    dx1, dx1b, d_norm = _rms_bwd_call(x1, d_h2, w["norm_ffn"][i:i + 1], dx2, name="ffn_norm_bwd", bf16_copy=True)
    grads = {"ffn_w_up": d_wup_t, "ffn_w_down": d_wdown, "ffn_conv_b": d_cb[0],
             "ffn_conv_w": jnp.concatenate([d_cw0, d_cw1, d_cw2], axis=0), "norm_ffn": d_norm[0]}
    return dx1, dx1b, grads


def _ple_fwd(x2, p_i, w, i):
    rn = _rms_fwd(x2, None, name="ple_norm")
    gl = _mm(rn, w["ple_w_gate"][i], name="ple_gate")
    pp = _mm(p_i, w["ple_w_proj"][i], tb=True, name="ple_proj")
    x3 = _rw(lambda xv, g, q: xv + _sigmoid(g) * q, [x2, gl, pp], [], [(D_MODEL, F32)], tr=512, name="ple_out")[0]
    return x3, (rn, gl, pp)


def _ple_bwd(dx3, x2, p_i, saved, w, i):
    rn, gl, pp = saved

    def fn(dv, g, q):
        sg = _sigmoid(g)
        return dv * sg, dv * q * (sg * (1.0 - sg))

    d_pp, d_pre = _rw(fn, [dx3, gl, pp], [], [(D_MODEL, BF16)] * 2, tr=512, name="ple_out_bwd")
    d_wproj_t = _mm(p_i, d_pp, ta=True, name="ple_proj_dw", bn=1024).T
    d_wgate = _mm(rn, d_pre, ta=True, name="ple_gate_dw")
    d_rn = _mm(d_pre, w["ple_w_gate"][i], tb=True, name="ple_gate_dx")
    dx2, dx2b, _ = _rms_bwd_call(x2, d_rn, None, dx3, name="ple_norm_bwd", bf16_copy=True)
    return dx2, dx2b, {"ple_w_proj": d_wproj_t, "ple_w_gate": d_wgate}


def _loss_and_grad(xf, tgt):
    def fn(xv, tv):
        d = xv - tv
        part = 0.5 * jnp.sum(jnp.mean(d * d, axis=-1, keepdims=True), axis=0, keepdims=True)
        return d * (1.0 / D_MODEL), part

    dx, loss = _rw(fn, [xf, tgt], [], [(D_MODEL, F32)], [(1, 1)], tr=512, name="loss_head")
    return loss[0, 0], dx


def _tril_mask():
    shp = (A_CHUNK, A_CHUNK)
    return lax.broadcasted_iota(jnp.int32, shp, 0) >= lax.broadcasted_iota(jnp.int32, shp, 1)


def _gmlp_fwd(z, w_s, b_s, v_gain, *, tr=512):
    s = z.shape[0]
    tr = min(tr, s)
    gw = A_WIDTH // A_GROUPS

    def body(z_ref, w_ref, b_ref, g_ref, y_ref):
        tril = _tril_mask()
        for g in range(A_GROUPS):
            wg = jnp.where(tril, w_ref[g], 0.0).astype(BF16)
            cols = slice(g * gw, (g + 1) * gw)
            vcols = slice(A_WIDTH + g * gw, A_WIDTH + (g + 1) * gw)
            for c in range(tr // A_CHUNK):
                rows = slice(c * A_CHUNK, (c + 1) * A_CHUNK)
                v = _gelu(z_ref[rows, vcols])
                vn = v * _rstd(v) * g_ref[:, cols]
                sv = jnp.dot(wg, vn.astype(BF16), preferred_element_type=F32) + b_ref[g]
                y_ref[rows, cols] = (_gelu(z_ref[rows, cols]) * sv).astype(BF16)

    return pl.pallas_call(
        body, grid=(s // tr,),
        in_specs=[pl.BlockSpec((tr, 2 * A_WIDTH), lambda i: (i, 0)),
                  pl.BlockSpec(w_s.shape, lambda i: (0, 0, 0)), pl.BlockSpec(b_s.shape, lambda i: (0, 0, 0)),
                  pl.BlockSpec(v_gain.shape, lambda i: (0, 0))],
        out_specs=pl.BlockSpec((tr, A_WIDTH), lambda i: (i, 0)),
        out_shape=jax.ShapeDtypeStruct((s, A_WIDTH), BF16), name="gmlp_fwd", compiler_params=_params(("parallel",)),
    )(z, w_s, b_s, v_gain)


def _gmlp_bwd(z, d_ymix, w_s, b_s, v_gain, *, tr=512):
    s = z.shape[0]
    tr = min(tr, s)
    gw = A_WIDTH // A_GROUPS

    def body(z_ref, dy_ref, w_ref, b_ref, g_ref, dz_ref, dw_ref, db_ref, dg_ref):
        @pl.when(pl.program_id(0) == 0)
        def _():
            dw_ref[...] = jnp.zeros_like(dw_ref)
            db_ref[...] = jnp.zeros_like(db_ref)
            dg_ref[...] = jnp.zeros_like(dg_ref)

        tril = _tril_mask()
        for g in range(A_GROUPS):
            wg = jnp.where(tril, w_ref[g], 0.0).astype(BF16)
            cols = slice(g * gw, (g + 1) * gw)
            vcols = slice(A_WIDTH + g * gw, A_WIDTH + (g + 1) * gw)
            gain = g_ref[:, cols]
            for c in range(tr // A_CHUNK):
                rows = slice(c * A_CHUNK, (c + 1) * A_CHUNK)
                va, ua = z_ref[rows, vcols], z_ref[rows, cols]
                v = _gelu(va)
                r = _rstd(v)
                vh = v * r
                vnb = (vh * gain).astype(BF16)
                sv = jnp.dot(wg, vnb, preferred_element_type=F32) + b_ref[g]
                dy = dy_ref[rows, cols]
                d_sv = dy * _gelu(ua)
                dz_ref[rows, cols] = (dy * sv * _gelu_grad(ua)).astype(BF16)
                d_svb = d_sv.astype(BF16)
                d_vn = lax.dot_general(wg, d_svb, (((0,), (0,)), ((), ())), preferred_element_type=F32)
                dwp = lax.dot_general(d_svb, vnb, (((1,), (1,)), ((), ())), preferred_element_type=F32)
                dw_ref[g] += jnp.where(tril, dwp, 0.0)
                db_ref[g] += jnp.sum(d_sv, axis=1, keepdims=True)
                dg_ref[:, cols] += jnp.sum(d_vn * vh, axis=0, keepdims=True)
                d_vh = d_vn * gain
                d_v = r * (d_vh - vh * jnp.mean(d_vh * vh, axis=-1, keepdims=True))
                dz_ref[rows, vcols] = (d_v * _gelu_grad(va)).astype(BF16)

    return pl.pallas_call(
        body, grid=(s // tr,),
        in_specs=[pl.BlockSpec((tr, 2 * A_WIDTH), lambda i: (i, 0)), pl.BlockSpec((tr, A_WIDTH), lambda i: (i, 0)),
                  pl.BlockSpec(w_s.shape, lambda i: (0, 0, 0)), pl.BlockSpec(b_s.shape, lambda i: (0, 0, 0)),
                  pl.BlockSpec(v_gain.shape, lambda i: (0, 0))],
        out_specs=[pl.BlockSpec((tr, 2 * A_WIDTH), lambda i: (i, 0)), pl.BlockSpec(w_s.shape, lambda i: (0, 0, 0)),
                   pl.BlockSpec(b_s.shape, lambda i: (0, 0, 0)), pl.BlockSpec(v_gain.shape, lambda i: (0, 0))],
        out_shape=[jax.ShapeDtypeStruct((s, 2 * A_WIDTH), BF16), jax.ShapeDtypeStruct(w_s.shape, F32),
                   jax.ShapeDtypeStruct(b_s.shape, F32), jax.ShapeDtypeStruct(v_gain.shape, F32)],
        name="gmlp_bwd", compiler_params=_params(("arbitrary",)),
    )(z, d_ymix, w_s, b_s, v_gain)


def _dot3(x, ub):
    x1 = x.astype(BF16)
    r1 = x - x1.astype(F32)
    x2 = r1.astype(BF16)
    x3 = (r1 - x2.astype(F32)).astype(BF16)
    d = lambda a: jnp.dot(a, ub, preferred_element_type=F32)
    return d(x1) + d(x2) + d(x3)


def _log_sigmoid(x):
    return jnp.minimum(x, 0.0) - jnp.log(1.0 + jnp.exp(-jnp.abs(x)))


def _fgate_fwd(f_t, b_col, *, tb=256):
    h, s = f_t.shape
    tb = min(tb, s)

    def body(f_ref, b_ref, c_ref, carry):
        @pl.when(pl.program_id(0) == 0)
        def _():
            carry[...] = jnp.zeros_like(carry)

        lf = _log_sigmoid(f_ref[...] + b_ref[...])
        shp = (tb, tb)
        upper = (lax.broadcasted_iota(jnp.int32, shp, 0) <= lax.broadcasted_iota(jnp.int32, shp, 1)).astype(BF16)
        c_ref[...] = _dot3(lf, upper) + carry[...]
        carry[...] += jnp.sum(lf, axis=1, keepdims=True)

    return pl.pallas_call(
        body, grid=(s // tb,),
        in_specs=[pl.BlockSpec((h, tb), lambda i: (0, i)), pl.BlockSpec((h, 1), lambda i: (0, 0))],
        out_specs=pl.BlockSpec((h, tb), lambda i: (0, i)), out_shape=jax.ShapeDtypeStruct((h, s), F32),
        scratch_shapes=[pltpu.VMEM((h, 1), F32)], name="fgate_fwd", compiler_params=_params(("arbitrary",)),
    )(f_t, b_col)


def _fgate_bwd(f_t, b_col, dc_a, dc_b, *, tb=256):
    h, s = f_t.shape
    tb = min(tb, s)
    n = s // tb

    def body(f_ref, b_ref, da_ref, db_ref, df_ref, dbias_ref, carry):
        @pl.when(pl.program_id(0) == 0)
        def _():
            carry[...] = jnp.zeros_like(carry)
            dbias_ref[...] = jnp.zeros_like(dbias_ref)

        dc = da_ref[...] + db_ref[...]
        shp = (tb, tb)
        lower = (lax.broadcasted_iota(jnp.int32, shp, 0) >= lax.broadcasted_iota(jnp.int32, shp, 1)).astype(BF16)
        d_lf = _dot3(dc, lower) + carry[...]
        carry[...] += jnp.sum(dc, axis=1, keepdims=True)
        df = d_lf * (1.0 - _sigmoid(f_ref[...] + b_ref[...]))
        df_ref[...] = df
        dbias_ref[...] += jnp.sum(df, axis=1, keepdims=True)

    blk = pl.BlockSpec((h, tb), lambda i: (0, n - 1 - i))
    return pl.pallas_call(
        body, grid=(n,), in_specs=[blk, pl.BlockSpec((h, 1), lambda i: (0, 0)), blk, blk],
        out_specs=[blk, pl.BlockSpec((h, 1), lambda i: (0, 0))],
        out_shape=[jax.ShapeDtypeStruct((h, s), F32), jax.ShapeDtypeStruct((h, 1), F32)],
        scratch_shapes=[pltpu.VMEM((h, 1), F32)], name="fgate_bwd", compiler_params=_params(("arbitrary",)),
    )(f_t, b_col, dc_a, dc_b)


_NT = (((1,), (1,)), ((), ()))


def _causal(shape, row0, col0, transposed):
    r = lax.broadcasted_iota(jnp.int32, shape, 0) + row0
    c = lax.broadcasted_iota(jnp.int32, shape, 1) + col0
    return (r <= c) if transposed else (c <= r)


_ATT_SCALE = B_HEAD_DIM ** -0.5
ATT_W = 128
_COL_CQ, _COL_CK, _COL_LSE, _COL_DELTA = 64, 67, 70, 64
_ATT_BLK = 1024
_ATT_SUB = 512


def _split3(x):
    h = x.astype(BF16).astype(F32)
    r = x - h
    m = r.astype(BF16).astype(F32)
    return h, m, (r - m).astype(BF16).astype(F32)


def _put_cols(base, lane, col0, parts):
    for t, pv in enumerate(parts):
        base = jnp.where(lane == col0 + t, pv, base)
    return base


def _attn_prep(q_raw, k_raw, v_raw, c_col, gq, gk):
    def fn(q, k, v, c, gqv, gkv):
        lane = lax.broadcasted_iota(jnp.int32, q.shape, 1)
        rq = lax.rsqrt(jnp.sum(q * q, axis=-1, keepdims=True) * (1.0 / B_HEAD_DIM) + EPS)
        rk = lax.rsqrt(jnp.sum(k * k, axis=-1, keepdims=True) * (1.0 / B_HEAD_DIM) + EPS)
        ch, cm, cl = _split3(c)
        one = jnp.ones_like(c)
        qq = _put_cols(q * rq * gqv * _ATT_SCALE, lane, _COL_CQ, (ch, cm, cl))
        qq = _put_cols(qq, lane, _COL_CK, (one, one, one))
        kk = _put_cols(k * rk * gkv, lane, _COL_CQ, (one, one, one))
        kk = _put_cols(kk, lane, _COL_CK, (-ch, -cm, -cl))
        kk = _put_cols(kk, lane, _COL_LSE, (one, one, one))
        vv = _put_cols(v, lane, _COL_DELTA, (one, one, one))
        return qq, kk, vv

    return _rw(fn, [q_raw, k_raw, v_raw, c_col], [gq, gk], [(ATT_W, BF16)] * 3, tr=2048, name="attn_prep")


def _attn_bwd_prep(do_raw, o, qp, lse):
    def fn(dov, ov, qv, lv):
        lane = lax.broadcasted_iota(jnp.int32, dov.shape, 1)
        dh, dm, dl = _split3(jnp.sum(dov * ov, axis=-1, keepdims=True))
        lh, lm, ll = _split3(lv)
        return (_put_cols(dov, lane, _COL_DELTA, (-dh, -dm, -dl)),
                _put_cols(qv.astype(F32), lane, _COL_LSE, (-lh, -lm, -ll)))

    return _rw(fn, [do_raw, o, qp, lse], [], [(ATT_W, BF16)] * 2, tr=2048, name="attn_bwd_prep")


def _attn_fwd(qp, kp, vp, *, blk=_ATT_BLK):
    h, s, d = qp.shape
    b = min(blk, s)
    n = s // b
    sub = min(_ATT_SUB, b)
    ns = b // sub

    def body(q_ref, k_ref, v_ref, o_ref, lse_ref):
        i = pl.program_id(1)
        qs = [q_ref[0, t * sub:(t + 1) * sub, :] for t in range(ns)]

        def tile(j, state, masked):
            off = pl.multiple_of(j * b, b)
            kk = k_ref[0, pl.ds(off, b), :]
            vv = v_ref[0, pl.ds(off, b), :]
            new = []
            for t in range(ns):
                m, l, acc = state[3 * t:3 * t + 3]
                sc = lax.dot_general(qs[t], kk, _NT, preferred_element_type=F32)
                if masked:
                    sc = jnp.where(_causal((sub, b), t * sub, 0, False), sc, NEG_INF)
                m_new = jnp.maximum(m, jnp.max(sc, axis=1, keepdims=True))
                alpha = jnp.exp(m - m_new)
                p = jnp.exp(sc - m_new)
                new += [m_new, alpha * l + jnp.sum(p, axis=1, keepdims=True),
                        alpha * acc + jnp.dot(p.astype(BF16), vv, preferred_element_type=F32)]
            return tuple(new)

        init = (jnp.full((sub, 1), NEG_INF, F32), jnp.zeros((sub, 1), F32), jnp.zeros((sub, d), F32)) * ns
        state = lax.fori_loop(0, i, lambda j, st: tile(j, st, False), init)
        state = tile(i, state, True)
        for t in range(ns):
            m, l, acc = state[3 * t:3 * t + 3]
            o_ref[0, t * sub:(t + 1) * sub, :] = acc / l
            lse_ref[0, t * sub:(t + 1) * sub, :] = m + jnp.log(l)

    blk_spec = pl.BlockSpec((1, b, d), lambda hh, i: (hh, i, 0))
    full = pl.BlockSpec((1, s, d), lambda hh, i: (hh, 0, 0))
    return pl.pallas_call(
        body, grid=(h, n), in_specs=[blk_spec, full, full],
        out_specs=[blk_spec, pl.BlockSpec((1, b, 1), lambda hh, i: (hh, i, 0))],
        out_shape=[jax.ShapeDtypeStruct((h, s, d), F32), jax.ShapeDtypeStruct((h, s, 1), F32)],
        name="attn_fwd", compiler_params=_params(("parallel", "arbitrary")),
    )(qp, kp, vp)


def _attn_bwd_dq(qpp, kp, vp, dop, *, blk=_ATT_BLK):
    h, s, d = qpp.shape
    b = min(blk, s)
    n = s // b
    sub = min(_ATT_SUB, b)
    ns = b // sub

    def body(q_ref, do_ref, k_ref, v_ref, dq_ref):
        i = pl.program_id(1)
        qs = [q_ref[0, t * sub:(t + 1) * sub, :] for t in range(ns)]
        dos = [do_ref[0, t * sub:(t + 1) * sub, :] for t in range(ns)]

        def tile(j, state, masked):
            off = pl.multiple_of(j * b, b)
            kk = k_ref[0, pl.ds(off, b), :]
            vv = v_ref[0, pl.ds(off, b), :]
            new = []
            for t in range(ns):
                p = jnp.exp(lax.dot_general(qs[t], kk, _NT, preferred_element_type=F32))
                if masked:
                    p = jnp.where(_causal((sub, b), t * sub, 0, False), p, 0.0)
                ds = p * lax.dot_general(dos[t], vv, _NT, preferred_element_type=F32)
                new.append(state[t] + jnp.dot(ds.astype(BF16), kk, preferred_element_type=F32))
            return tuple(new)

        state = lax.fori_loop(0, i, lambda j, st: tile(j, st, False), (jnp.zeros((sub, d), F32),) * ns)
        state = tile(i, state, True)
        for t in range(ns):
            dq_ref[0, t * sub:(t + 1) * sub, :] = state[t]

    blk_spec = pl.BlockSpec((1, b, d), lambda hh, i: (hh, i, 0))
    full = pl.BlockSpec((1, s, d), lambda hh, i: (hh, 0, 0))
    return pl.pallas_call(
        body, grid=(h, n), in_specs=[blk_spec, blk_spec, full, full], out_specs=blk_spec,
        out_shape=jax.ShapeDtypeStruct((h, s, d), F32), name="attn_bwd_dq",
        compiler_params=_params(("parallel", "arbitrary")),
    )(qpp, dop, kp, vp)


def _attn_bwd_dkv(qpp, kp, vp, dop, *, blk=_ATT_BLK):
    h, s, d = qpp.shape
    b = min(blk, s)
    n = s // b
    sub = min(_ATT_SUB, b)
    ns = b // sub

    def body(k_ref, v_ref, q_ref, do_ref, dk_ref, dv_ref):
        j = pl.program_id(1)
        ks = [k_ref[0, t * sub:(t + 1) * sub, :] for t in range(ns)]
        vs = [v_ref[0, t * sub:(t + 1) * sub, :] for t in range(ns)]

        def tile(i, state, masked):
            off = pl.multiple_of(i * b, b)
            qq = q_ref[0, pl.ds(off, b), :]
            dd = do_ref[0, pl.ds(off, b), :]
            new = []
            for t in range(ns):
                pt = jnp.exp(lax.dot_general(ks[t], qq, _NT, preferred_element_type=F32))
                if masked:
                    pt = jnp.where(_causal((sub, b), t * sub, 0, True), pt, 0.0)
                dst = pt * lax.dot_general(vs[t], dd, _NT, preferred_element_type=F32)
                new += [state[2 * t] + jnp.dot(dst.astype(BF16), qq, preferred_element_type=F32),
                        state[2 * t + 1] + jnp.dot(pt.astype(BF16), dd, preferred_element_type=F32)]
            return tuple(new)

        state = tile(j, (jnp.zeros((sub, d), F32),) * (2 * ns), True)
        state = lax.fori_loop(j + 1, n, lambda i, st: tile(i, st, False), state)
        for t in range(ns):
            dk_ref[0, t * sub:(t + 1) * sub, :] = state[2 * t]
            dv_ref[0, t * sub:(t + 1) * sub, :] = state[2 * t + 1]

    blk_spec = pl.BlockSpec((1, b, d), lambda hh, j: (hh, j, 0))
    full = pl.BlockSpec((1, s, d), lambda hh, j: (hh, 0, 0))
    return pl.pallas_call(
        body, grid=(h, n), in_specs=[blk_spec, blk_spec, full, full], out_specs=[blk_spec, blk_spec],
        out_shape=[jax.ShapeDtypeStruct((h, s, d), F32)] * 2, name="attn_bwd_dkv",
        compiler_params=_params(("parallel", "arbitrary")),
    )(kp, vp, qpp, dop)


def _heads_padded(a):
    s = a.shape[0]
    t = a.reshape(s, B_HEADS, B_HEAD_DIM).transpose(1, 0, 2)
    return jnp.pad(t, ((0, 0), (0, 0), (0, ATT_W - B_HEAD_DIM))).reshape(B_HEADS * s, ATT_W)


def _unheads(a):
    s = a.shape[1]
    return a.transpose(1, 0, 2).reshape(s, B_WIDTH)


def _pad_head(g):
    return jnp.pad(g, ((0, 0), (0, ATT_W - B_HEAD_DIM)))


def _even_fwd(x0, w, e, i):
    s = x0.shape[0]
    hs = B_HEADS * s
    hb = _rms_fwd(x0, w["norm_mix"][i:i + 1], name="mix_norm")
    z = _mm(hb, w["ev_w_in"][e], tb=True, name="ev_in", bn=896)
    b_s = w["ev_b_spatial"][e][:, :, None]
    v_gain = w["ev_v_norm"][e:e + 1]
    ya = _gmlp_fwd(z, w["ev_w_spatial"][e], b_s, v_gain)
    q_raw = _heads_padded(z[:, 2 * A_WIDTH:2 * A_WIDTH + B_WIDTH])
    k_raw = _heads_padded(z[:, 2 * A_WIDTH + B_WIDTH:2 * A_WIDTH + 2 * B_WIDTH])
    v_raw = _heads_padded(z[:, 2 * A_WIDTH + 2 * B_WIDTH:2 * A_WIDTH + 3 * B_WIDTH])
    f_t = z[:, IN_COLS - B_HEADS:IN_COLS].T
    c = _fgate_fwd(f_t, w["ev_b_fgate"][e][:, None])
    qp, kp, vp = _attn_prep(q_raw, k_raw, v_raw, c.reshape(hs, 1), _pad_head(w["ev_q_norm"][e:e + 1]),
                            _pad_head(w["ev_k_norm"][e:e + 1]))
    shp = (B_HEADS, s, ATT_W)
    o, lse = _attn_fwd(qp.reshape(shp), kp.reshape(shp), vp.reshape(shp))
    ymix = jnp.concatenate([ya, _unheads(o[:, :, :B_HEAD_DIM]).astype(BF16)], axis=1)
    x1 = _mm(ymix, w["ev_w_out"][e], add=x0, name="ev_out")
    return x1, (hb, z, ymix, q_raw, k_raw, qp, kp, vp, f_t, o, lse)


def _even_bwd(dx1, dx1b, x0, saved, w, e, i):
    hb, z, ymix, q_raw, k_raw, qp, kp, vp, f_t, o, lse = saved
    s = x0.shape[0]
    hs = B_HEADS * s
    hd = B_HEAD_DIM
    d_ymix = _mm(dx1b, w["ev_w_out"][e], tb=True, name="ev_out_dx")
    d_wout = _mm(ymix, dx1b, ta=True, name="ev_out_dw")
    do_raw = _heads_padded(d_ymix[:, A_WIDTH:])
    dop, qpp = _attn_bwd_prep(do_raw, o.reshape(hs, ATT_W), qp, lse.reshape(hs, 1))
    shp = (B_HEADS, s, ATT_W)
    qpp, dop, kp3, vp3 = qpp.reshape(shp), dop.reshape(shp), kp.reshape(shp), vp.reshape(shp)
    dqp = _attn_bwd_dq(qpp, kp3, vp3, dop)
    dkp, dvp = _attn_bwd_dkv(qpp, kp3, vp3, dop)
    d_ft, d_bf = _fgate_bwd(f_t, w["ev_b_fgate"][e][:, None], dqp[:, :, _COL_CQ], -dkp[:, :, _COL_CK])
    dq_raw, d_gq = _rms_bwd_call(q_raw[:, :hd], dqp.reshape(hs, ATT_W)[:, :hd], w["ev_q_norm"][e:e + 1], None,
                                 name="attn_qnorm_bwd", tr=2048, scale=_ATT_SCALE)
    dk_raw, d_gk = _rms_bwd_call(k_raw[:, :hd], dkp.reshape(hs, ATT_W)[:, :hd], w["ev_k_norm"][e:e + 1], None,
                                 name="attn_knorm_bwd", tr=2048)
    b_s = w["ev_b_spatial"][e][:, :, None]
    v_gain = w["ev_v_norm"][e:e + 1]
    dz_uv, d_ws, d_bs, d_vg = _gmlp_bwd(z, d_ymix, w["ev_w_spatial"][e], b_s, v_gain)
    shp64 = (B_HEADS, s, hd)
    dz = jnp.concatenate(
        [dz_uv, _unheads(dq_raw.reshape(shp64)).astype(BF16), _unheads(dk_raw.reshape(shp64)).astype(BF16),
         _unheads(dvp[:, :, :hd]).astype(BF16), d_ft.T.astype(BF16), jnp.zeros((s, IN_COLS_PAD - IN_COLS), BF16)],
        axis=1)
    d_h = _mm(dz, w["ev_w_in"][e], name="ev_in_dx")
    d_win_t = _mm(hb, dz, ta=True, name="ev_in_dw", bn=1344)[:, :IN_COLS].T
    dx0, d_nm = _rms_bwd_call(x0, d_h, w["norm_mix"][i:i + 1], dx1, name="mix_norm_bwd")
    grads = {"ev_w_in": d_win_t, "ev_w_out": d_wout, "ev_b_fgate": d_bf[:, 0], "ev_q_norm": d_gq[0],
             "ev_k_norm": d_gk[0], "ev_v_norm": d_vg[0], "ev_w_spatial": d_ws, "ev_b_spatial": d_bs[:, :, 0],
             "norm_mix": d_nm[0]}
    return dx0, grads


def _s5_disc(a_re, a_im, log_dt, b_re, b_im):
    dt = jnp.exp(log_dt)[:, None]
    lr, li = a_re, a_im
    mag = jnp.exp(lr * dt)
    ab_re, ab_im = mag * jnp.cos(li * dt), mag * jnp.sin(li * dt)
    den = lr * lr + li * li
    nr, ni = ab_re - 1.0, ab_im
    cr = (nr * lr + ni * li) / den
    ci = (ni * lr - nr * li) / den
    bb_re = cr[..., None] * b_re - ci[..., None] * b_im
    bb_im = cr[..., None] * b_im + ci[..., None] * b_re
    return ab_re, ab_im, bb_re, bb_im


_GPB = S5_GROUPS // S5_BLOCKS


def _blockdiag(t):
    a, b = t.shape[1:]
    eye = jnp.eye(_GPB, dtype=t.dtype)
    t = t.reshape(S5_BLOCKS, _GPB, a, 1, b) * eye[None, :, None, :, None]
    return t.reshape(S5_BLOCKS, _GPB * a, _GPB * b)


def _blockdiag_extract(m, a, b):
    eye = jnp.eye(_GPB, dtype=m.dtype)
    m = m.reshape(S5_BLOCKS, _GPB, a, _GPB, b) * eye[None, :, None, :, None]
    return jnp.sum(m, axis=3).reshape(S5_GROUPS, a, b)


_SCAN_ROWS = 8


def _cmul(a_r, a_i, b_r, b_i):
    return a_r * b_r - a_i * b_i, a_r * b_i + a_i * b_r


def _scan(br, bi, ar, ai, *, rev, xs=None, tb=1024, wl=512, name):
    s, n = br.shape
    tb = min(tb, s)
    nt = s // tb
    ntile = tb // _SCAN_ROWS
    with_acc = xs is not None

    def body(*refs):
        br_ref, bi_ref, ar_ref, ai_ref = refs[:4]
        k = 4
        if with_acc:
            xr_ref, xi_ref = refs[4:6]
            k = 6
        or_ref, oi_ref = refs[k:k + 2]
        k += 2
        if with_acc:
            dar_ref, dai_ref = refs[k:k + 2]
            k += 2
        cr, ci = refs[k:k + 2]

        @pl.when(pl.program_id(1) == 0)
        def _():
            cr[...] = jnp.zeros_like(cr)
            ci[...] = jnp.zeros_like(ci)
            if with_acc:
                dar_ref[...] = jnp.zeros_like(dar_ref)
                dai_ref[...] = jnp.zeros_like(dai_ref)

        a_r = ar_ref[...]
        a_i = -ai_ref[...] if rev else ai_ref[...]
        pw = [(a_r, a_i)]
        for _ in range(_SCAN_ROWS - 1):
            pw.append(_cmul(*pw[-1], a_r, a_i))
        sub = lax.broadcasted_iota(jnp.int32, (_SCAN_ROWS, wl), 0)
        dist = (_SCAN_ROWS - 1 - sub) if rev else sub
        zero = jnp.zeros((_SCAN_ROWS, wl), F32)
        steps = []
        for kk in (1, 2, 4):
            steps.append(((_SCAN_ROWS - kk) if rev else kk, jnp.where(dist >= kk, pw[kk - 1][0], zero),
                          jnp.where(dist >= kk, pw[kk - 1][1], zero)))
        e_r, e_i = zero, zero
        for d in range(_SCAN_ROWS):
            e_r = jnp.where(dist == d, pw[d][0], e_r)
            e_i = jnp.where(dist == d, pw[d][1], e_i)
        exit_row = 0 if rev else _SCAN_ROWS - 1

        def tile(q, carry):
            c_r, c_i = carry[0], carry[1]
            idx = (ntile - 1 - q) if rev else q
            rows = pl.ds(pl.multiple_of(idx * _SCAN_ROWS, _SCAN_ROWS), _SCAN_ROWS)
            y_r, y_i = br_ref[rows, :], bi_ref[rows, :]
            for sh, k_r, k_i in steps:
                t_r, t_i = _cmul(k_r, k_i, pltpu.roll(y_r, sh, 0), pltpu.roll(y_i, sh, 0))
                y_r, y_i = y_r + t_r, y_i + t_i
            cb_r = jnp.broadcast_to(c_r, (_SCAN_ROWS, wl))
            cb_i = jnp.broadcast_to(c_i, (_SCAN_ROWS, wl))
            t_r, t_i = _cmul(e_r, e_i, cb_r, cb_i)
            y_r, y_i = y_r + t_r, y_i + t_i
            or_ref[rows, :] = y_r
            oi_ref[rows, :] = y_i
            out = (y_r[exit_row:exit_row + 1], y_i[exit_row:exit_row + 1])
            if with_acc:
                n_r = jnp.where(sub == _SCAN_ROWS - 1, cb_r, pltpu.roll(y_r, _SCAN_ROWS - 1, 0))
                n_i = jnp.where(sub == _SCAN_ROWS - 1, cb_i, pltpu.roll(y_i, _SCAN_ROWS - 1, 0))
                s_r, s_i = xr_ref[rows, :], xi_ref[rows, :]
                out += (carry[2] + (n_r * s_r + n_i * s_i), carry[3] + (n_i * s_r - n_r * s_i))
            return out

        init = (cr[...], ci[...]) + ((zero, zero) if with_acc else ())
        out = lax.fori_loop(0, ntile, tile, init, unroll=2)
        cr[...] = out[0]
        ci[...] = out[1]
        if with_acc:
            dar_ref[...] += jnp.sum(out[2], axis=0, keepdims=True)
            dai_ref[...] += jnp.sum(out[3], axis=0, keepdims=True)

    tmap = (lambda c, t: (nt - 1 - t, c)) if rev else (lambda c, t: (t, c))
    blk = pl.BlockSpec((tb, wl), tmap)
    vec = pl.BlockSpec((1, wl), lambda c, t: (0, c))
    in_specs = [blk, blk, vec, vec] + ([blk, blk] if with_acc else [])
    out_specs = [blk, blk] + ([vec, vec] if with_acc else [])
    out_shape = [jax.ShapeDtypeStruct((s, n), F32)] * 2 + ([jax.ShapeDtypeStruct((1, n), F32)] * 2 if with_acc else [])
    return pl.pallas_call(
        body, grid=(n // wl, nt), in_specs=in_specs, out_specs=out_specs, out_shape=out_shape,
        scratch_shapes=[pltpu.VMEM((1, wl), F32), pltpu.VMEM((1, wl), F32)], name=name,
        compiler_params=_params(("parallel", "arbitrary")),
    )(br, bi, ar, ai, *(xs or ()))


def _s5_mats(w, o):
    ab_re, ab_im, bb_re, bb_im = _s5_disc(w["od_a_re"][o], w["od_a_im"][o], w["od_log_dt"][o], w["od_b_re"][o],
                                          w["od_b_im"][o])
    c_re, c_im = w["od_c_re"][o], w["od_c_im"][o]
    tr = lambda t: t.transpose(0, 2, 1)
    return {
        "ar": ab_re.reshape(1, S5_N), "ai": ab_im.reshape(1, S5_N),
        "wb_re": _blockdiag(tr(bb_re)), "wb_im": _blockdiag(tr(bb_im)),
        "wc_re": _blockdiag(tr(c_re)), "wc_imn": _blockdiag(-tr(c_im)),
        "wct_re": _blockdiag(c_re), "wct_imn": _blockdiag(-c_im),
        "wbt_re": _blockdiag(bb_re), "wbt_im": _blockdiag(bb_im),
    }


def _odd_fwd(x0, w, o, i):
    mats = _s5_mats(w, o)
    nm = w["norm_mix"][i:i + 1]
    d_row = w["od_d"][o:o + 1]
    hb = _rms_fwd(x0, nm, name="mix_norm")
    bur = _bdmm(hb, mats["wb_re"], name="s5_b_re")
    bui = _bdmm(hb, mats["wb_im"], name="s5_b_im")
    xr, xi = _scan(bur, bui, mats["ar"], mats["ai"], rev=False, name="s5_scan")
    cp = _bdmm(xi, mats["wc_imn"], add=_bdmm(xr, mats["wc_re"], name="s5_c_re"), name="s5_c_im")

    def fn(xv, cpv, gv, dv):
        y = cpv + dv * (xv * _rstd(xv) * gv)
        return y, _gelu(y)

    y, gy = _rw(fn, [x0, cp], [nm, d_row], [(D_MODEL, F32), (D_MODEL, BF16)], tr=512, name="s5_out")
    gg = _mm(gy, w["od_w_glu"][o], tb=True, name="od_glu")
    x1 = _rw(lambda xv, g: xv + g[:, :D_MODEL] * _sigmoid(g[:, D_MODEL:]), [x0, gg], [], [(D_MODEL, F32)], tr=512,
             name="od_glu_out")[0]
    return x1, (hb, xr, xi, y, gy, gg)


def _odd_bwd(dx1, x0, saved, w, o, i):
    hb, xr, xi, y, gy, gg = saved
    mats = _s5_mats(w, o)
    nm = w["norm_mix"][i:i + 1]
    d_row = w["od_d"][o:o + 1]

    def fn_glu(dv, g):
        ga, gb = g[:, :D_MODEL], g[:, D_MODEL:]
        sg = _sigmoid(gb)
        return jnp.concatenate([dv * sg, dv * ga * (sg * (1.0 - sg))], axis=1)

    dgg = _rw(fn_glu, [dx1, gg], [], [(2 * D_MODEL, BF16)], tr=512, name="od_glu_out_bwd")[0]
    d_wglu = _mm(gy, dgg, ta=True, name="od_glu_dw", bn=1024).T
    d_gy = _mm(dgg, w["od_w_glu"][o], name="od_glu_dx")

    def fn_y(dg, yv, xv, gv, dv):
        dy = dg * _gelu_grad(yv)
        h = xv * _rstd(xv) * gv
        return dy, dv * dy, jnp.sum(dy * h, axis=0, keepdims=True)

    dyb, dhd, d_d = _rw(fn_y, [d_gy, y, x0], [nm, d_row], [(D_MODEL, BF16), (D_MODEL, F32)], [(1, D_MODEL)], tr=512,
                        name="s5_out_bwd")
    dxr = _bdmm(dyb, mats["wct_re"], name="s5_c_re_dx")
    dxi = _bdmm(dyb, mats["wct_imn"], name="s5_c_im_dx")
    gc, gp = S5_GROUP_CH, S5_STATE
    d_c_re = _blockdiag_extract(_bdwg(dyb, xr, _GPB * gc, _GPB * gp, name="s5_c_re_dw"), gc, gp)
    d_c_im = -_blockdiag_extract(_bdwg(dyb, xi, _GPB * gc, _GPB * gp, name="s5_c_im_dw"), gc, gp)
    lr, li, d_ar, d_ai = _scan(dxr, dxi, mats["ar"], mats["ai"], rev=True, xs=(xr, xi), name="s5_scan_bwd")
    d_bb_re = _blockdiag_extract(_bdwg(hb, lr, _GPB * gc, _GPB * gp, name="s5_b_re_dw"), gc, gp).transpose(0, 2, 1)
    d_bb_im = _blockdiag_extract(_bdwg(hb, li, _GPB * gc, _GPB * gp, name="s5_b_im_dw"), gc, gp).transpose(0, 2, 1)
    dh = _bdmm(li, mats["wbt_im"], add=_bdmm(lr, mats["wbt_re"], add=dhd, name="s5_b_re_dx"), name="s5_b_im_dx")
    dx0, d_nm = _rms_bwd_call(x0, dh, nm, dx1, name="mix_norm_bwd")
    _, vjp = jax.vjp(_s5_disc, w["od_a_re"][o], w["od_a_im"][o], w["od_log_dt"][o], w["od_b_re"][o], w["od_b_im"][o])
    d_a_re, d_a_im, d_log_dt, d_b_re, d_b_im = vjp(
        (d_ar.reshape(S5_GROUPS, S5_STATE), d_ai.reshape(S5_GROUPS, S5_STATE), d_bb_re, d_bb_im))
    grads = {"od_a_re": d_a_re, "od_a_im": d_a_im, "od_log_dt": d_log_dt, "od_b_re": d_b_re, "od_b_im": d_b_im,
             "od_c_re": d_c_re, "od_c_im": d_c_im, "od_d": d_d[0], "od_w_glu": d_wglu, "norm_mix": d_nm[0]}
    return dx0, grads


_HBM = pl.BlockSpec(memory_space=pltpu.HBM)


def _mesh_pos():
    return lax.axis_index("x"), lax.axis_index("y"), lax.axis_index("c")


def _slot(px, py, pc):
    return 4 * px + 2 * py + pc


def _all_gather(x, *, name):
    def body(x_ref, out_ref, send_sems, recv_sems, local_sem):
        mx, my, mc = _mesh_pos()
        me, sibling = (mx, my, mc), (mx, my, 1 - mc)
        chips = [(1 - mx, my), (mx, 1 - my), (1 - mx, 1 - my)]

        def copy(k, block, to, src=None):
            dst = out_ref.at[_slot(*block)]
            return pltpu.make_async_remote_copy(
                src_ref=dst if src is None else src, dst_ref=dst, send_sem=send_sems.at[k], recv_sem=recv_sems.at[k],
                device_id=to, device_id_type=MESH)

        mine = pltpu.make_async_copy(x_ref, out_ref.at[_slot(*me)], local_sem)
        mine.start()
        first = [copy(0, me, sibling, src=x_ref)]
        first += [copy(1 + j, me, (*chip, mc), src=x_ref) for j, chip in enumerate(chips)]
        for cp in first:
            cp.start()
        passed = [copy(4 + j, (*chip, mc), sibling) for j, chip in enumerate(chips)]
        for j, chip in enumerate(chips):
            copy(1 + j, (*chip, mc), me).wait_recv()
            passed[j].start()
        copy(0, sibling, me).wait_recv()
        for j, chip in enumerate(chips):
            copy(4 + j, (*chip, 1 - mc), me).wait_recv()
        for cp in first + passed:
            cp.wait_send()
        mine.wait()

    return pl.pallas_call(
        body, out_shape=jax.ShapeDtypeStruct((N_DEV,) + x.shape, x.dtype), in_specs=[_HBM], out_specs=_HBM,
        scratch_shapes=[pltpu.SemaphoreType.DMA((7,)), pltpu.SemaphoreType.DMA((7,)), pltpu.SemaphoreType.DMA],
        name=name,
    )(x)


def _all_to_all(x, *, name):
    def body(x_ref, out_ref, send_sems, recv_sems, local_sem):
        mx, my, mc = _mesh_pos()
        my_slot = _slot(mx, my, mc)
        mine = pltpu.make_async_copy(x_ref.at[my_slot], out_ref.at[my_slot], local_sem)
        mine.start()
        copies = []
        for k in range(1, N_DEV):
            peer = (1 - mx if k & 4 else mx, 1 - my if k & 2 else my, 1 - mc if k & 1 else mc)
            copies.append(pltpu.make_async_remote_copy(
                src_ref=x_ref.at[_slot(*peer)], dst_ref=out_ref.at[my_slot], send_sem=send_sems.at[k - 1],
                recv_sem=recv_sems.at[k - 1], device_id=peer, device_id_type=MESH))
        for cp in copies:
            cp.start()
        for cp in copies:
            cp.wait()
        mine.wait()

    return pl.pallas_call(
        body, out_shape=jax.ShapeDtypeStruct(x.shape, x.dtype), in_specs=[_HBM], out_specs=_HBM,
        scratch_shapes=[pltpu.SemaphoreType.DMA((7,)), pltpu.SemaphoreType.DMA((7,)), pltpu.SemaphoreType.DMA],
        name=name,
    )(x)


_ADAMW_PARTS_BLOCK_BYTES = 8 * 1024 * 1024


def _adamw(parts, w, m, v, *, name):
    r, c = w.shape
    tr = r
    for cand in range(8, r + 1, 8):
        if r % cand == 0 and N_DEV * cand * c * 4 <= _ADAMW_PARTS_BLOCK_BYTES:
            tr = cand

    def body(p_ref, w_ref, m_ref, v_ref, g_ref, d_ref, nm_ref, nv_ref):
        g = p_ref[0]
        for k in range(1, N_DEV):
            g = g + p_ref[k]
        nm = ADAM_B1 * m_ref[...] + (1.0 - ADAM_B1) * g
        nv = ADAM_B2 * v_ref[...] + (1.0 - ADAM_B2) * (g * g)
        m_hat = nm / (1.0 - ADAM_B1 ** ADAM_STEP)
        v_hat = nv / (1.0 - ADAM_B2 ** ADAM_STEP)
        g_ref[...] = g
        d_ref[...] = -ADAM_LR * (m_hat / (jnp.sqrt(v_hat) + ADAM_EPS) + ADAM_WD * w_ref[...])
        nm_ref[...] = nm
        nv_ref[...] = nv

    blk = pl.BlockSpec((tr, c), lambda i: (i, 0))
    return pl.pallas_call(
        body, grid=(r // tr,), in_specs=[pl.BlockSpec((N_DEV, tr, c), lambda i: (0, i, 0)), blk, blk, blk],
        out_specs=[blk] * 4, out_shape=[jax.ShapeDtypeStruct((r, c), F32)] * 4, name=name,
        compiler_params=_params(("parallel",)),
    )(parts, w, m, v)


_FLAT_COLS = 1024
_FLAT_ROW_ALIGN = 128

_SHARD_AXIS = {
    "norm_mix": None, "norm_ffn": None, "ev_w_in": 2, "ev_b_fgate": None, "ev_q_norm": None, "ev_k_norm": None,
    "ev_v_norm": None, "ev_w_spatial": None, "ev_b_spatial": None, "ev_w_out": 1, "od_a_re": None, "od_a_im": None,
    "od_log_dt": None, "od_b_re": None, "od_b_im": None, "od_c_re": None, "od_c_im": None, "od_d": 1, "od_w_glu": 2,
    "ffn_w_up": 2, "ffn_conv_w": 2, "ffn_conv_b": None, "ffn_w_down": 1, "ple_w_proj": 2, "ple_w_gate": 1,
}
_WEIGHTS = list(_SHARD_AXIS)
_REPLICATED = [n for n in _WEIGHTS if _SHARD_AXIS[n] is None]
_COL_SHARDED = ["ev_w_in", "od_w_glu", "ffn_w_up", "ple_w_proj"]
_ROW_SHARDED = ["ev_w_out", "ffn_w_down", "ple_w_gate"]
_SMALL_SHARDED = ["od_d", "ffn_conv_w"]


def _pack(arrays, lead=()):
    nl = len(lead)
    flat = jnp.concatenate([a.reshape(lead + (-1,)) for a in arrays], axis=nl)
    n = flat.shape[nl]
    chunk = _FLAT_COLS * _FLAT_ROW_ALIGN
    total = -(-n // chunk) * chunk
    flat = jnp.pad(flat, [(0, 0)] * nl + [(0, total - n)])
    return flat.reshape(lead + (total // _FLAT_COLS, _FLAT_COLS))


def _unpack(buf, shapes, lead=()):
    nl = len(lead)
    flat = buf.reshape(lead + (-1,))
    out, off = [], 0
    for shp in shapes:
        n = math.prod(shp)
        out.append(lax.slice_in_dim(flat, off, off + n, axis=nl).reshape(lead + tuple(shp)))
        off += n
    return out


def _join_shards(g, axis):
    g = jnp.moveaxis(g, 0, axis)
    shp = g.shape
    return g.reshape(shp[:axis] + (shp[axis] * shp[axis + 1],) + shp[axis + 2:])


def _split_shards(full, axis):
    shp = full.shape
    g = full.reshape(shp[:axis] + (N_DEV, shp[axis] // N_DEV) + shp[axis + 1:])
    return jnp.moveaxis(g, axis, 0)


def _local_step(x, p, tgt, w):
    saved = []
    h = x
    for i in range(DEPTH):
        x0 = h
        if i % 2 == 0:
            x1, sv_mix = _even_fwd(x0, w, i // 2, i)
        else:
            x1, sv_mix = _odd_fwd(x0, w, i // 2, i)
        x2, sv_ffn = _ffn_fwd(x1, w, i)
        x3, sv_ple = _ple_fwd(x2, p[i], w, i)
        saved.append((x0, x1, x2, sv_mix, sv_ffn, sv_ple))
        h = x3
    loss, dh = _loss_and_grad(h, tgt)
    per_layer = {}
    for i in reversed(range(DEPTH)):
        x0, x1, x2, sv_mix, sv_ffn, sv_ple = saved[i]
        dx2, dx2b, g_ple = _ple_bwd(dh, x2, p[i], sv_ple, w, i)
        dx1, dx1b, g_ffn = _ffn_bwd(dx2, dx2b, x1, sv_ffn, w, i)
        if i % 2 == 0:
            dh, g_mix = _even_bwd(dx1, dx1b, x0, sv_mix, w, i // 2, i)
        else:
            dh, g_mix = _odd_bwd(dx1, x0, sv_mix, w, i // 2, i)
        for name, g in {**g_ple, **g_ffn, **g_mix}.items():
            per_layer.setdefault(name, []).append(g)
    grads = {name: gs[::-1] for name, gs in per_layer.items()}
    return loss, dh, grads


def kernel(x, p, norm_mix, norm_ffn, ev_w_in, ev_b_fgate, ev_q_norm, ev_k_norm, ev_v_norm, ev_w_spatial, ev_b_spatial, ev_w_out, od_a_re, od_a_im, od_log_dt, od_b_re, od_b_im, od_c_re, od_c_im, od_d, od_w_glu, ffn_w_up, ffn_conv_w, ffn_conv_b, ffn_w_down, ple_w_proj, ple_w_gate, loss_target, m_norm_mix, m_norm_ffn, m_ev_w_in, m_ev_b_fgate, m_ev_q_norm, m_ev_k_norm, m_ev_v_norm, m_ev_w_spatial, m_ev_b_spatial, m_ev_w_out, m_od_a_re, m_od_a_im, m_od_log_dt, m_od_b_re, m_od_b_im, m_od_c_re, m_od_c_im, m_od_d, m_od_w_glu, m_ffn_w_up, m_ffn_conv_w, m_ffn_conv_b, m_ffn_w_down, m_ple_w_proj, m_ple_w_gate, v_norm_mix, v_norm_ffn, v_ev_w_in, v_ev_b_fgate, v_ev_q_norm, v_ev_k_norm, v_ev_v_norm, v_ev_w_spatial, v_ev_b_spatial, v_ev_w_out, v_od_a_re, v_od_a_im, v_od_log_dt, v_od_b_re, v_od_b_im, v_od_c_re, v_od_c_im, v_od_d, v_od_w_glu, v_ffn_w_up, v_ffn_conv_w, v_ffn_conv_b, v_ffn_w_down, v_ple_w_proj, v_ple_w_gate):
    args = locals()
    wl = {n: args[n] for n in _WEIGHTS}
    ml = {n: args["m_" + n] for n in _WEIGHTS}
    vl = {n: args["v_" + n] for n in _WEIGHTS}

    kinds = ("g", "d", "m", "v")

    def local(n, layer, a):
        return a[n][layer].T if n in _COL_SHARDED else a[n][layer]

    w = {n: wl[n] for n in _REPLICATED}
    for n in _COL_SHARDED + _ROW_SHARDED:
        w[n] = []
        for layer in range(wl[n].shape[0]):
            g = _all_gather(local(n, layer, wl).astype(BF16), name="gather_" + n)
            full = g.reshape(N_DEV * g.shape[1], g.shape[2])
            if n == "ev_w_in":
                full = jnp.pad(full, ((0, IN_COLS_PAD - IN_COLS), (0, 0)))
            w[n].append(full)
    small = _all_gather(_pack([wl[n] for n in _SMALL_SHARDED]), name="gather_small_weights")
    for n, g in zip(_SMALL_SHARDED, _unpack(small, [wl[n].shape for n in _SMALL_SHARDED], lead=(N_DEV,))):
        w[n] = _join_shards(g, _SHARD_AXIS[n])

    loss, grad_x, grads = _local_step(x[0], p[:, 0], loss_target[0], w)
    loss = lax.psum(loss, ("x", "y", "c"))

    out = {kind: {} for kind in kinds}
    for n in _COL_SHARDED + _ROW_SHARDED:
        layers = []
        for layer, g in enumerate(grads[n]):
            parts = _all_to_all(g.reshape(N_DEV, g.shape[0] // N_DEV, g.shape[1]), name="scatter_" + n)
            res = _adamw(parts, local(n, layer, wl), local(n, layer, ml), local(n, layer, vl), name="adamw_" + n)
            layers.append([r.T if n in _COL_SHARDED else r for r in res])
        for k, kind in enumerate(kinds):
            out[kind][n] = jnp.stack([lay[k] for lay in layers])

    send = _pack([_split_shards(jnp.stack(grads[n]), _SHARD_AXIS[n]) for n in _SMALL_SHARDED], lead=(N_DEV,))
    parts = _all_to_all(send, name="scatter_small_grads")
    shapes = [wl[n].shape for n in _SMALL_SHARDED]
    res = _adamw(parts, _pack([wl[n] for n in _SMALL_SHARDED]), _pack([ml[n] for n in _SMALL_SHARDED]),
                 _pack([vl[n] for n in _SMALL_SHARDED]), name="adamw_small")
    for kind, buf in zip(kinds, res):
        out[kind].update(zip(_SMALL_SHARDED, _unpack(buf, shapes)))

    parts = _all_gather(_pack([jnp.stack(grads[n]) for n in _REPLICATED]), name="gather_replicated_grads")
    shapes = [wl[n].shape for n in _REPLICATED]
    res = _adamw(parts, _pack([wl[n] for n in _REPLICATED]), _pack([ml[n] for n in _REPLICATED]),
                 _pack([vl[n] for n in _REPLICATED]), name="adamw_replicated")
    for kind, buf in zip(kinds, res):
        out[kind].update(zip(_REPLICATED, _unpack(buf, shapes)))

    return (loss, grad_x[None], *[out["g"][n] for n in _WEIGHTS], *[out["d"][n] for n in _WEIGHTS],
            *[out["m"][n] for n in _WEIGHTS], *[out["v"][n] for n in _WEIGHTS])
```

```python
import functools
import math

import jax
import jax.numpy as jnp
from jax import lax
from jax.experimental import pallas as pl
from jax.experimental.pallas import tpu as pltpu

F32 = jnp.float32
BF16 = jnp.bfloat16

D_MODEL = 1024
DEPTH = 4
A_GROUPS = 4
A_CHUNK = 128
A_WIDTH = 512
B_HEADS = 8
B_HEAD_DIM = 64
B_WIDTH = 512
IN_COLS = 2 * A_WIDTH + 3 * B_WIDTH + B_HEADS
IN_COLS_PAD = 2688
S5_GROUP_CH = 16
S5_GROUPS = 64
S5_STATE = 64
S5_N = S5_GROUPS * S5_STATE
S5_BLOCKS = 8
D_FF = 2816
PLE_DIM = 256
EPS = 1e-6
NEG_INF = -1e30
N_DEV = 8

ADAM_LR = 0.001
ADAM_B1 = 0.9
ADAM_B2 = 0.999
ADAM_EPS = 1e-08
ADAM_WD = 0.01
ADAM_STEP = 10

VMEM_LIMIT_BYTES = 56 * 1024 * 1024
MESH = pl.DeviceIdType.MESH


def _params(sem, vmem=VMEM_LIMIT_BYTES):
    return pltpu.CompilerParams(dimension_semantics=sem, vmem_limit_bytes=vmem)


_GELU_K = math.sqrt(2.0 / math.pi)


def _gelu(x):
    return x * (0.5 * (1.0 + jnp.tanh(_GELU_K * (x + 0.044715 * (x * x * x)))))


def _gelu_grad(x):
    t = jnp.tanh(_GELU_K * (x + 0.044715 * (x * x * x)))
    return 0.5 * (1.0 + t) + 0.5 * x * (1.0 - t * t) * (_GELU_K * (1.0 + 3.0 * 0.044715 * (x * x)))


def _sigmoid(x):
    return 1.0 / (1.0 + jnp.exp(-x))


def _rstd(x):
    return lax.rsqrt(jnp.mean(x * x, axis=-1, keepdims=True) + EPS)


def _rms_bwd(x, dy, g):
    r = _rstd(x)
    xh = x * r
    dyg = dy if g is None else dy * g
    dx = r * (dyg - xh * jnp.mean(dyg * xh, axis=-1, keepdims=True))
    return dx, dy * xh


def _shift_down(blk, halo, k):
    tr = blk.shape[0]
    r = pltpu.roll(blk, k, 0)
    hr = pltpu.roll(halo, k, 0)
    first = jnp.where(lax.broadcasted_iota(jnp.int32, hr.shape, 0) < k, hr, r[0:8])
    return jnp.concatenate([first, r[8:tr]], axis=0)


def _shift_up(blk, halo, k):
    tr = blk.shape[0]
    r = pltpu.roll(blk, tr - k, 0)
    hr = pltpu.roll(halo, 8 - k, 0)
    last = jnp.where(lax.broadcasted_iota(jnp.int32, hr.shape, 0) >= 8 - k, hr, r[tr - 8:tr])
    return jnp.concatenate([r[0:tr - 8], last], axis=0)


def _rw(fn, rows, consts, outs, accs=(), *, tr, name, prev=(), nxt=(), widths=None):
    s = rows[0].shape[0]
    tr = min(tr, s)
    n = s // tr
    nr, nc, npv, nnx, no, na = len(rows), len(consts), len(prev), len(nxt), len(outs), len(accs)
    widths = widths or [None] * nr

    def body(*refs):
        ins, out_refs = refs[:nr + nc + npv + nnx], refs[nr + nc + npv + nnx:]
        i = pl.program_id(0)
        vals = [r[...] for r in ins[:nr + nc]]
        vals += [jnp.where(i == 0, 0.0, r[...]) for r in ins[nr + nc:nr + nc + npv]]
        vals += [jnp.where(i == n - 1, 0.0, r[...]) for r in ins[nr + nc + npv:]]
        res = fn(*vals)
        if not isinstance(res, (tuple, list)):
            res = (res,)
        for k in range(no):
            out_refs[k][...] = res[k].astype(out_refs[k].dtype)
        if na:
            @pl.when(i == 0)
            def _():
                for k in range(na):
                    out_refs[no + k][...] = jnp.zeros_like(out_refs[no + k])

            for k in range(na):
                out_refs[no + k][...] += res[no + k]

    in_specs = []
    for a, w in zip(rows, widths):
        if w is None:
            in_specs.append(pl.BlockSpec((tr, a.shape[1]), lambda i: (i, 0)))
        else:
            in_specs.append(pl.BlockSpec((tr, w[1]), functools.partial(lambda i, cb: (i, cb), cb=w[0])))
    for c in consts:
        in_specs.append(pl.BlockSpec(c.shape, functools.partial(lambda i, nd: (0,) * nd, nd=c.ndim)))
    t8 = tr // 8
    for a in prev:
        in_specs.append(pl.BlockSpec((8, a.shape[1]), lambda i: (jnp.maximum(i * t8 - 1, 0), 0)))
    for a in nxt:
        in_specs.append(pl.BlockSpec((8, a.shape[1]), lambda i: (jnp.minimum((i + 1) * t8, s // 8 - 1), 0)))
    out_shape = [jax.ShapeDtypeStruct((s, w), dt) for w, dt in outs]
    out_specs = [pl.BlockSpec((tr, w), lambda i: (i, 0)) for w, _ in outs]
    out_shape += [jax.ShapeDtypeStruct(a, F32) for a in accs]
    out_specs += [pl.BlockSpec(a, lambda i: (0, 0)) for a in accs]
    res = pl.pallas_call(
        body, grid=(n,), in_specs=in_specs, out_specs=out_specs, out_shape=out_shape, name=name,
        compiler_params=_params(("arbitrary",) if na else ("parallel",)),
    )(*rows, *consts, *prev, *nxt)
    return res


def _pick(n, cap):
    if n <= cap:
        return n
    best = None
    for d in range(128, cap + 1, 128):
        if n % d == 0:
            best = d
    assert best is not None, (n, cap)
    return best


def _mm(a, b, *, ta=False, tb=False, out_dtype=F32, add=None, name, bm=1024, bn=512, bk=None):
    (k_dim, m) = a.shape if ta else a.shape[::-1]
    n = b.shape[0] if tb else b.shape[1]
    assert (b.shape[1] if tb else b.shape[0]) == k_dim
    if bk is None:
        bk = 1024 if ta else 2816
    bm, bn, bk = _pick(m, bm), _pick(n, bn), _pick(k_dim, bk)
    nk = k_dim // bk
    dims = (((0 if ta else 1,), (1 if tb else 0,)), ((), ()))
    has_add = add is not None

    def body(*refs):
        a_ref, b_ref = refs[0], refs[1]
        add_ref = refs[2] if has_add else None
        o_ref = refs[2 + has_add]
        p = lax.dot_general(a_ref[...].astype(BF16), b_ref[...].astype(BF16), dims, preferred_element_type=F32)

        def finish(acc):
            if has_add:
                acc = acc + add_ref[...]
            o_ref[...] = acc.astype(o_ref.dtype)

        if nk == 1:
            finish(p)
        else:
            acc_ref = refs[3 + has_add]
            kk = pl.program_id(2)

            @pl.when(kk == 0)
            def _():
                acc_ref[...] = p

            @pl.when(kk > 0)
            def _():
                acc_ref[...] += p

            @pl.when(kk == nk - 1)
            def _():
                finish(acc_ref[...])

    a_spec = pl.BlockSpec((bk, bm), lambda i, j, k: (k, i)) if ta else pl.BlockSpec((bm, bk), lambda i, j, k: (i, k))
    b_spec = pl.BlockSpec((bn, bk), lambda i, j, k: (j, k)) if tb else pl.BlockSpec((bk, bn), lambda i, j, k: (k, j))
    in_specs = [a_spec, b_spec]
    args = [a, b]
    if has_add:
        in_specs.append(pl.BlockSpec((bm, bn), lambda i, j, k: (i, j)))
        args.append(add)
    return pl.pallas_call(
        body, grid=(m // bm, n // bn, nk), in_specs=in_specs,
        out_specs=pl.BlockSpec((bm, bn), lambda i, j, k: (i, j)),
        out_shape=jax.ShapeDtypeStruct((m, n), out_dtype),
        scratch_shapes=[pltpu.VMEM((bm, bn), F32)] if nk > 1 else [],
        name=name, compiler_params=_params(("parallel", "parallel", "arbitrary")),
    )(*args)


def _rms_fwd(x, g, *, name):
    if g is None:
        return _rw(lambda xv: xv * _rstd(xv), [x], [], [(x.shape[1], BF16)], tr=512, name=name)[0]
    return _rw(lambda xv, gv: xv * _rstd(xv) * gv, [x], [g], [(x.shape[1], BF16)], tr=512, name=name)[0]


def _rms_bwd_call(x, dy, g, dx_add, *, name, tr=512, scale=None, bf16_copy=False):
    w = x.shape[1]
    has_g, has_add = g is not None, dx_add is not None

    def fn(*v):
        xv, dyv = v[0], v[1]
        if scale is not None:
            dyv = dyv * scale
        gv = v[2 + has_add] if has_g else None
        dx, dg = _rms_bwd(xv, dyv, gv)
        if has_add:
            dx = dx + v[2]
        return (dx,) + ((dx,) if bf16_copy else ()) + ((jnp.sum(dg, axis=0, keepdims=True),) if has_g else ())

    rows = [x, dy] + ([dx_add] if has_add else [])
    outs = [(w, F32)] + ([(w, BF16)] if bf16_copy else [])
    res = _rw(fn, rows, [g] if has_g else [], outs, [(1, w)] if has_g else [], tr=tr, name=name)
    return tuple(res) if has_g else tuple(res) + (None,)


def _conv_taps(hup, halo, cw):
    h1 = _shift_down(hup, halo, 1)
    h2 = _shift_down(hup, halo, 2)
    return h1, h2


def _ffn_fwd(x1, w, i):
    h2b = _rms_fwd(x1, w["norm_ffn"][i:i + 1], name="ffn_norm")
    hup = _mm(h2b, w["ffn_w_up"][i], tb=True, name="ffn_up")
    cw, cb = w["ffn_conv_w"][i], w["ffn_conv_b"][i:i + 1]

    def fn(hv, cwv, cbv, halo):
        h1, h2 = _conv_taps(hv, halo, cwv)
        hc = cbv + cwv[0:1] * h2
        hc = hc + cwv[1:2] * h1
        hc = hc + cwv[2:3] * hv
        g, up = hc[:, :D_FF], hc[:, D_FF:]
        return (g * _sigmoid(g)) * up

    a = _rw(fn, [hup], [cw, cb], [(D_FF, BF16)], tr=128, name="ffn_conv_gate", prev=[hup])[0]
    x2 = _mm(a, w["ffn_w_down"][i], add=x1, name="ffn_down")
    return x2, (h2b, hup, a)


def _ffn_bwd(dx2, dx2b, x1, saved, w, i):
    h2b, hup, a = saved
    cw, cb = w["ffn_conv_w"][i], w["ffn_conv_b"][i:i + 1]
    da = _mm(dx2b, w["ffn_w_down"][i], tb=True, name="ffn_down_dx")
    d_wdown = _mm(a, dx2b, ta=True, name="ffn_down_dw", bm=1408)

    def fn1(hv, dav, cwv, cbv, h_prev, h_next, da_next):
        tr = hv.shape[0]
        he = jnp.concatenate([hv, h_next], axis=0)
        dae = jnp.concatenate([dav, da_next], axis=0)
        h1, h2 = _conv_taps(he, h_prev, cwv)
        hc = cbv + cwv[0:1] * h2
        hc = hc + cwv[1:2] * h1
        hc = hc + cwv[2:3] * he
        g, up = hc[:, :D_FF], hc[:, D_FF:]
        sg = _sigmoid(g)
        d_up = dae * (g * sg)
        d_g = dae * up * (sg * (1.0 + g * (1.0 - sg)))
        d_hce = jnp.concatenate([d_g, d_up], axis=1)
        rows_e = tr + 8
        d_hup = (cwv[2:3] * d_hce[:tr] + cwv[1:2] * pltpu.roll(d_hce, rows_e - 1, 0)[:tr]
                 + cwv[0:1] * pltpu.roll(d_hce, rows_e - 2, 0)[:tr])
        d_hc = d_hce[:tr]
        col = lambda v: jnp.sum(v, axis=0, keepdims=True)
        return d_hup, col(d_hc), col(d_hc * h2[:tr]), col(d_hc * h1[:tr]), col(d_hc * hv)

    w2 = 2 * D_FF
    d_hup, d_cb, d_cw0, d_cw1, d_cw2 = _rw(fn1, [hup, da], [cw, cb], [(w2, BF16)], [(1, w2)] * 4, tr=128,
                                           name="ffn_conv_gate_bwd", prev=[hup], nxt=[hup, da])
    d_h2 = _mm(d_hup, w["ffn_w_up"][i], name="ffn_up_dx")
    d_wup_t = _mm(h2b, d_hup, ta=True, name="ffn_up_dw", bn=1408).T
    dx1, dx1b, d_norm = _rms_bwd_call(x1, d_h2, w["norm_ffn"][i:i + 1], dx2, name="ffn_norm_bwd", bf16_copy=True)
    grads = {"ffn_w_up": d_wup_t, "ffn_w_down": d_wdown, "ffn_conv_b": d_cb[0],
             "ffn_conv_w": jnp.concatenate([d_cw0, d_cw1, d_cw2], axis=0), "norm_ffn": d_norm[0]}
    return dx1, dx1b, grads


def _ple_fwd(x2, p_i, w, i):
    rn = _rms_fwd(x2, None, name="ple_norm")
    gl = _mm(rn, w["ple_w_gate"][i], name="ple_gate")
    pp = _mm(p_i, w["ple_w_proj"][i], tb=True, name="ple_proj")
    x3 = _rw(lambda xv, g, q: xv + _sigmoid(g) * q, [x2, gl, pp], [], [(D_MODEL, F32)], tr=512, name="ple_out")[0]
    return x3, (rn, gl, pp)


def _ple_bwd(dx3, x2, p_i, saved, w, i):
    rn, gl, pp = saved

    def fn(dv, g, q):
        sg = _sigmoid(g)
        return dv * sg, dv * q * (sg * (1.0 - sg))

    d_pp, d_pre = _rw(fn, [dx3, gl, pp], [], [(D_MODEL, BF16)] * 2, tr=512, name="ple_out_bwd")
    d_wproj_t = _mm(p_i, d_pp, ta=True, name="ple_proj_dw", bn=1024).T
    d_wgate = _mm(rn, d_pre, ta=True, name="ple_gate_dw")
    d_rn = _mm(d_pre, w["ple_w_gate"][i], tb=True, name="ple_gate_dx")
    dx2, dx2b, _ = _rms_bwd_call(x2, d_rn, None, dx3, name="ple_norm_bwd", bf16_copy=True)
    return dx2, dx2b, {"ple_w_proj": d_wproj_t, "ple_w_gate": d_wgate}


def _loss_and_grad(xf, tgt):
    def fn(xv, tv):
        d = xv - tv
        part = 0.5 * jnp.sum(jnp.mean(d * d, axis=-1, keepdims=True), axis=0, keepdims=True)
        return d * (1.0 / D_MODEL), part

    dx, loss = _rw(fn, [xf, tgt], [], [(D_MODEL, F32)], [(1, 1)], tr=512, name="loss_head")
    return loss[0, 0], dx


def _tril_mask():
    shp = (A_CHUNK, A_CHUNK)
    return lax.broadcasted_iota(jnp.int32, shp, 0) >= lax.broadcasted_iota(jnp.int32, shp, 1)


def _gmlp_fwd(z, w_s, b_s, v_gain, *, tr=512):
    s = z.shape[0]
    tr = min(tr, s)
    gw = A_WIDTH // A_GROUPS

    def body(z_ref, w_ref, b_ref, g_ref, y_ref):
        tril = _tril_mask()
        for g in range(A_GROUPS):
            wg = jnp.where(tril, w_ref[g], 0.0).astype(BF16)
            cols = slice(g * gw, (g + 1) * gw)
            vcols = slice(A_WIDTH + g * gw, A_WIDTH + (g + 1) * gw)
            for c in range(tr // A_CHUNK):
                rows = slice(c * A_CHUNK, (c + 1) * A_CHUNK)
                v = _gelu(z_ref[rows, vcols])
                vn = v * _rstd(v) * g_ref[:, cols]
                sv = jnp.dot(wg, vn.astype(BF16), preferred_element_type=F32) + b_ref[g]
                y_ref[rows, cols] = (_gelu(z_ref[rows, cols]) * sv).astype(BF16)

    return pl.pallas_call(
        body, grid=(s // tr,),
        in_specs=[pl.BlockSpec((tr, 2 * A_WIDTH), lambda i: (i, 0)),
                  pl.BlockSpec(w_s.shape, lambda i: (0, 0, 0)), pl.BlockSpec(b_s.shape, lambda i: (0, 0, 0)),
                  pl.BlockSpec(v_gain.shape, lambda i: (0, 0))],
        out_specs=pl.BlockSpec((tr, A_WIDTH), lambda i: (i, 0)),
        out_shape=jax.ShapeDtypeStruct((s, A_WIDTH), BF16), name="gmlp_fwd", compiler_params=_params(("parallel",)),
    )(z, w_s, b_s, v_gain)


def _gmlp_bwd(z, d_ymix, w_s, b_s, v_gain, *, tr=512):
    s = z.shape[0]
    tr = min(tr, s)
    gw = A_WIDTH // A_GROUPS

    def body(z_ref, dy_ref, w_ref, b_ref, g_ref, dz_ref, dw_ref, db_ref, dg_ref):
        @pl.when(pl.program_id(0) == 0)
        def _():
            dw_ref[...] = jnp.zeros_like(dw_ref)
            db_ref[...] = jnp.zeros_like(db_ref)
            dg_ref[...] = jnp.zeros_like(dg_ref)

        tril = _tril_mask()
        for g in range(A_GROUPS):
            wg = jnp.where(tril, w_ref[g], 0.0).astype(BF16)
            cols = slice(g * gw, (g + 1) * gw)
            vcols = slice(A_WIDTH + g * gw, A_WIDTH + (g + 1) * gw)
            gain = g_ref[:, cols]
            for c in range(tr // A_CHUNK):
                rows = slice(c * A_CHUNK, (c + 1) * A_CHUNK)
                va, ua = z_ref[rows, vcols], z_ref[rows, cols]
                v = _gelu(va)
                r = _rstd(v)
                vh = v * r
                vnb = (vh * gain).astype(BF16)
                sv = jnp.dot(wg, vnb, preferred_element_type=F32) + b_ref[g]
                dy = dy_ref[rows, cols]
                d_sv = dy * _gelu(ua)
                dz_ref[rows, cols] = (dy * sv * _gelu_grad(ua)).astype(BF16)
                d_svb = d_sv.astype(BF16)
                d_vn = lax.dot_general(wg, d_svb, (((0,), (0,)), ((), ())), preferred_element_type=F32)
                dwp = lax.dot_general(d_svb, vnb, (((1,), (1,)), ((), ())), preferred_element_type=F32)
                dw_ref[g] += jnp.where(tril, dwp, 0.0)
                db_ref[g] += jnp.sum(d_sv, axis=1, keepdims=True)
                dg_ref[:, cols] += jnp.sum(d_vn * vh, axis=0, keepdims=True)
                d_vh = d_vn * gain
                d_v = r * (d_vh - vh * jnp.mean(d_vh * vh, axis=-1, keepdims=True))
                dz_ref[rows, vcols] = (d_v * _gelu_grad(va)).astype(BF16)

    return pl.pallas_call(
        body, grid=(s // tr,),
        in_specs=[pl.BlockSpec((tr, 2 * A_WIDTH), lambda i: (i, 0)), pl.BlockSpec((tr, A_WIDTH), lambda i: (i, 0)),
                  pl.BlockSpec(w_s.shape, lambda i: (0, 0, 0)), pl.BlockSpec(b_s.shape, lambda i: (0, 0, 0)),
                  pl.BlockSpec(v_gain.shape, lambda i: (0, 0))],
        out_specs=[pl.BlockSpec((tr, 2 * A_WIDTH), lambda i: (i, 0)), pl.BlockSpec(w_s.shape, lambda i: (0, 0, 0)),
                   pl.BlockSpec(b_s.shape, lambda i: (0, 0, 0)), pl.BlockSpec(v_gain.shape, lambda i: (0, 0))],
        out_shape=[jax.ShapeDtypeStruct((s, 2 * A_WIDTH), BF16), jax.ShapeDtypeStruct(w_s.shape, F32),
                   jax.ShapeDtypeStruct(b_s.shape, F32), jax.ShapeDtypeStruct(v_gain.shape, F32)],
        name="gmlp_bwd", compiler_params=_params(("arbitrary",)),
    )(z, d_ymix, w_s, b_s, v_gain)


def _dot3(x, ub):
    x1 = x.astype(BF16)
    r1 = x - x1.astype(F32)
    x2 = r1.astype(BF16)
    x3 = (r1 - x2.astype(F32)).astype(BF16)
    d = lambda a: jnp.dot(a, ub, preferred_element_type=F32)
    return d(x1) + d(x2) + d(x3)


def _log_sigmoid(x):
    return jnp.minimum(x, 0.0) - jnp.log(1.0 + jnp.exp(-jnp.abs(x)))


def _fgate_fwd(f_t, b_col, *, tb=256):
    h, s = f_t.shape
    tb = min(tb, s)

    def body(f_ref, b_ref, c_ref, carry):
        @pl.when(pl.program_id(0) == 0)
        def _():
            carry[...] = jnp.zeros_like(carry)

        lf = _log_sigmoid(f_ref[...] + b_ref[...])
        shp = (tb, tb)
        upper = (lax.broadcasted_iota(jnp.int32, shp, 0) <= lax.broadcasted_iota(jnp.int32, shp, 1)).astype(BF16)
        c_ref[...] = _dot3(lf, upper) + carry[...]
        carry[...] += jnp.sum(lf, axis=1, keepdims=True)

    return pl.pallas_call(
        body, grid=(s // tb,),
        in_specs=[pl.BlockSpec((h, tb), lambda i: (0, i)), pl.BlockSpec((h, 1), lambda i: (0, 0))],
        out_specs=pl.BlockSpec((h, tb), lambda i: (0, i)), out_shape=jax.ShapeDtypeStruct((h, s), F32),
        scratch_shapes=[pltpu.VMEM((h, 1), F32)], name="fgate_fwd", compiler_params=_params(("arbitrary",)),
    )(f_t, b_col)


def _fgate_bwd(f_t, b_col, dc_a, dc_b, *, tb=256):
    h, s = f_t.shape
    tb = min(tb, s)
    n = s // tb

    def body(f_ref, b_ref, da_ref, db_ref, df_ref, dbias_ref, carry):
        @pl.when(pl.program_id(0) == 0)
        def _():
            carry[...] = jnp.zeros_like(carry)
            dbias_ref[...] = jnp.zeros_like(dbias_ref)

        dc = da_ref[...] + db_ref[...]
        shp = (tb, tb)
        lower = (lax.broadcasted_iota(jnp.int32, shp, 0) >= lax.broadcasted_iota(jnp.int32, shp, 1)).astype(BF16)
        d_lf = _dot3(dc, lower) + carry[...]
        carry[...] += jnp.sum(dc, axis=1, keepdims=True)
        df = d_lf * (1.0 - _sigmoid(f_ref[...] + b_ref[...]))
        df_ref[...] = df
        dbias_ref[...] += jnp.sum(df, axis=1, keepdims=True)

    blk = pl.BlockSpec((h, tb), lambda i: (0, n - 1 - i))
    return pl.pallas_call(
        body, grid=(n,), in_specs=[blk, pl.BlockSpec((h, 1), lambda i: (0, 0)), blk, blk],
        out_specs=[blk, pl.BlockSpec((h, 1), lambda i: (0, 0))],
        out_shape=[jax.ShapeDtypeStruct((h, s), F32), jax.ShapeDtypeStruct((h, 1), F32)],
        scratch_shapes=[pltpu.VMEM((h, 1), F32)], name="fgate_bwd", compiler_params=_params(("arbitrary",)),
    )(f_t, b_col, dc_a, dc_b)


_NT = (((1,), (1,)), ((), ()))


def _causal(shape, row0, col0, transposed):
    r = lax.broadcasted_iota(jnp.int32, shape, 0) + row0
    c = lax.broadcasted_iota(jnp.int32, shape, 1) + col0
    return (r <= c) if transposed else (c <= r)


_ATT_SCALE = B_HEAD_DIM ** -0.5
ATT_W = 128
_COL_CQ, _COL_CK, _COL_LSE, _COL_DELTA = 64, 67, 70, 64
_ATT_BLK = 1024
_ATT_SUB = 512


def _split3(x):
    h = x.astype(BF16).astype(F32)
    r = x - h
    m = r.astype(BF16).astype(F32)
    return h, m, (r - m).astype(BF16).astype(F32)


def _put_cols(base, lane, col0, parts):
    for t, pv in enumerate(parts):
        base = jnp.where(lane == col0 + t, pv, base)
    return base


def _attn_prep(q_raw, k_raw, v_raw, c_col, gq, gk):
    def fn(q, k, v, c, gqv, gkv):
        lane = lax.broadcasted_iota(jnp.int32, q.shape, 1)
        rq = lax.rsqrt(jnp.sum(q * q, axis=-1, keepdims=True) * (1.0 / B_HEAD_DIM) + EPS)
        rk = lax.rsqrt(jnp.sum(k * k, axis=-1, keepdims=True) * (1.0 / B_HEAD_DIM) + EPS)
        ch, cm, cl = _split3(c)
        one = jnp.ones_like(c)
        qq = _put_cols(q * rq * gqv * _ATT_SCALE, lane, _COL_CQ, (ch, cm, cl))
        qq = _put_cols(qq, lane, _COL_CK, (one, one, one))
        kk = _put_cols(k * rk * gkv, lane, _COL_CQ, (one, one, one))
        kk = _put_cols(kk, lane, _COL_CK, (-ch, -cm, -cl))
        kk = _put_cols(kk, lane, _COL_LSE, (one, one, one))
        vv = _put_cols(v, lane, _COL_DELTA, (one, one, one))
        return qq, kk, vv

    return _rw(fn, [q_raw, k_raw, v_raw, c_col], [gq, gk], [(ATT_W, BF16)] * 3, tr=2048, name="attn_prep")


def _attn_bwd_prep(do_raw, o, qp, lse):
    def fn(dov, ov, qv, lv):
        lane = lax.broadcasted_iota(jnp.int32, dov.shape, 1)
        dh, dm, dl = _split3(jnp.sum(dov * ov, axis=-1, keepdims=True))
        lh, lm, ll = _split3(lv)
        return (_put_cols(dov, lane, _COL_DELTA, (-dh, -dm, -dl)),
                _put_cols(qv.astype(F32), lane, _COL_LSE, (-lh, -lm, -ll)))

    return _rw(fn, [do_raw, o, qp, lse], [], [(ATT_W, BF16)] * 2, tr=2048, name="attn_bwd_prep")


def _attn_fwd(qp, kp, vp, *, blk=_ATT_BLK):
    h, s, d = qp.shape
    b = min(blk, s)
    n = s // b
    sub = min(_ATT_SUB, b)
    ns = b // sub

    def body(q_ref, k_ref, v_ref, o_ref, lse_ref):
        i = pl.program_id(1)
        qs = [q_ref[0, t * sub:(t + 1) * sub, :] for t in range(ns)]

        def tile(j, state, masked):
            off = pl.multiple_of(j * b, b)
            kk = k_ref[0, pl.ds(off, b), :]
            vv = v_ref[0, pl.ds(off, b), :]
            new = []
            for t in range(ns):
                m, l, acc = state[3 * t:3 * t + 3]
                sc = lax.dot_general(qs[t], kk, _NT, preferred_element_type=F32)
                if masked:
                    sc = jnp.where(_causal((sub, b), t * sub, 0, False), sc, NEG_INF)
                m_new = jnp.maximum(m, jnp.max(sc, axis=1, keepdims=True))
                alpha = jnp.exp(m - m_new)
                p = jnp.exp(sc - m_new)
                new += [m_new, alpha * l + jnp.sum(p, axis=1, keepdims=True),
                        alpha * acc + jnp.dot(p.astype(BF16), vv, preferred_element_type=F32)]
            return tuple(new)

        init = (jnp.full((sub, 1), NEG_INF, F32), jnp.zeros((sub, 1), F32), jnp.zeros((sub, d), F32)) * ns
        state = lax.fori_loop(0, i, lambda j, st: tile(j, st, False), init)
        state = tile(i, state, True)
        for t in range(ns):
            m, l, acc = state[3 * t:3 * t + 3]
            o_ref[0, t * sub:(t + 1) * sub, :] = acc / l
            lse_ref[0, t * sub:(t + 1) * sub, :] = m + jnp.log(l)

    blk_spec = pl.BlockSpec((1, b, d), lambda hh, i: (hh, i, 0))
    full = pl.BlockSpec((1, s, d), lambda hh, i: (hh, 0, 0))
    return pl.pallas_call(
        body, grid=(h, n), in_specs=[blk_spec, full, full],
        out_specs=[blk_spec, pl.BlockSpec((1, b, 1), lambda hh, i: (hh, i, 0))],
        out_shape=[jax.ShapeDtypeStruct((h, s, d), F32), jax.ShapeDtypeStruct((h, s, 1), F32)],
        name="attn_fwd", compiler_params=_params(("parallel", "arbitrary")),
    )(qp, kp, vp)


def _attn_bwd_dq(qpp, kp, vp, dop, *, blk=_ATT_BLK):
    h, s, d = qpp.shape
    b = min(blk, s)
    n = s // b
    sub = min(_ATT_SUB, b)
    ns = b // sub

    def body(q_ref, do_ref, k_ref, v_ref, dq_ref):
        i = pl.program_id(1)
        qs = [q_ref[0, t * sub:(t + 1) * sub, :] for t in range(ns)]
        dos = [do_ref[0, t * sub:(t + 1) * sub, :] for t in range(ns)]

        def tile(j, state, masked):
            off = pl.multiple_of(j * b, b)
            kk = k_ref[0, pl.ds(off, b), :]
            vv = v_ref[0, pl.ds(off, b), :]
            new = []
            for t in range(ns):
                p = jnp.exp(lax.dot_general(qs[t], kk, _NT, preferred_element_type=F32))
                if masked:
                    p = jnp.where(_causal((sub, b), t * sub, 0, False), p, 0.0)
                ds = p * lax.dot_general(dos[t], vv, _NT, preferred_element_type=F32)
                new.append(state[t] + jnp.dot(ds.astype(BF16), kk, preferred_element_type=F32))
            return tuple(new)

        state = lax.fori_loop(0, i, lambda j, st: tile(j, st, False), (jnp.zeros((sub, d), F32),) * ns)
        state = tile(i, state, True)
        for t in range(ns):
            dq_ref[0, t * sub:(t + 1) * sub, :] = state[t]

    blk_spec = pl.BlockSpec((1, b, d), lambda hh, i: (hh, i, 0))
    full = pl.BlockSpec((1, s, d), lambda hh, i: (hh, 0, 0))
    return pl.pallas_call(
        body, grid=(h, n), in_specs=[blk_spec, blk_spec, full, full], out_specs=blk_spec,
        out_shape=jax.ShapeDtypeStruct((h, s, d), F32), name="attn_bwd_dq",
        compiler_params=_params(("parallel", "arbitrary")),
    )(qpp, dop, kp, vp)


def _attn_bwd_dkv(qpp, kp, vp, dop, *, blk=_ATT_BLK):
    h, s, d = qpp.shape
    b = min(blk, s)
    n = s // b
    sub = min(_ATT_SUB, b)
    ns = b // sub

    def body(k_ref, v_ref, q_ref, do_ref, dk_ref, dv_ref):
        j = pl.program_id(1)
        ks = [k_ref[0, t * sub:(t + 1) * sub, :] for t in range(ns)]
        vs = [v_ref[0, t * sub:(t + 1) * sub, :] for t in range(ns)]

        def tile(i, state, masked):
            off = pl.multiple_of(i * b, b)
            qq = q_ref[0, pl.ds(off, b), :]
            dd = do_ref[0, pl.ds(off, b), :]
            new = []
            for t in range(ns):
                pt = jnp.exp(lax.dot_general(ks[t], qq, _NT, preferred_element_type=F32))
                if masked:
                    pt = jnp.where(_causal((sub, b), t * sub, 0, True), pt, 0.0)
                dst = pt * lax.dot_general(vs[t], dd, _NT, preferred_element_type=F32)
                new += [state[2 * t] + jnp.dot(dst.astype(BF16), qq, preferred_element_type=F32),
                        state[2 * t + 1] + jnp.dot(pt.astype(BF16), dd, preferred_element_type=F32)]
            return tuple(new)

        state = tile(j, (jnp.zeros((sub, d), F32),) * (2 * ns), True)
        state = lax.fori_loop(j + 1, n, lambda i, st: tile(i, st, False), state)
        for t in range(ns):
            dk_ref[0, t * sub:(t + 1) * sub, :] = state[2 * t]
            dv_ref[0, t * sub:(t + 1) * sub, :] = state[2 * t + 1]

    blk_spec = pl.BlockSpec((1, b, d), lambda hh, j: (hh, j, 0))
    full = pl.BlockSpec((1, s, d), lambda hh, j: (hh, 0, 0))
    return pl.pallas_call(
        body, grid=(h, n), in_specs=[blk_spec, blk_spec, full, full], out_specs=[blk_spec, blk_spec],
        out_shape=[jax.ShapeDtypeStruct((h, s, d), F32)] * 2, name="attn_bwd_dkv",
        compiler_params=_params(("parallel", "arbitrary")),
    )(kp, vp, qpp, dop)


def _heads_padded(a):
    s = a.shape[0]
    t = a.reshape(s, B_HEADS, B_HEAD_DIM).transpose(1, 0, 2)
    return jnp.pad(t, ((0, 0), (0, 0), (0, ATT_W - B_HEAD_DIM))).reshape(B_HEADS * s, ATT_W)


def _unheads(a):
    s = a.shape[1]
    return a.transpose(1, 0, 2).reshape(s, B_WIDTH)


def _pad_head(g):
    return jnp.pad(g, ((0, 0), (0, ATT_W - B_HEAD_DIM)))


def _even_fwd(x0, w, e, i):
    s = x0.shape[0]
    hs = B_HEADS * s
    hb = _rms_fwd(x0, w["norm_mix"][i:i + 1], name="mix_norm")
    z = _mm(hb, w["ev_w_in"][e], tb=True, name="ev_in", bn=896)
    b_s = w["ev_b_spatial"][e][:, :, None]
    v_gain = w["ev_v_norm"][e:e + 1]
    ya = _gmlp_fwd(z, w["ev_w_spatial"][e], b_s, v_gain)
    q_raw = _heads_padded(z[:, 2 * A_WIDTH:2 * A_WIDTH + B_WIDTH])
    k_raw = _heads_padded(z[:, 2 * A_WIDTH + B_WIDTH:2 * A_WIDTH + 2 * B_WIDTH])
    v_raw = _heads_padded(z[:, 2 * A_WIDTH + 2 * B_WIDTH:2 * A_WIDTH + 3 * B_WIDTH])
    f_t = z[:, IN_COLS - B_HEADS:IN_COLS].T
    c = _fgate_fwd(f_t, w["ev_b_fgate"][e][:, None])
    qp, kp, vp = _attn_prep(q_raw, k_raw, v_raw, c.reshape(hs, 1), _pad_head(w["ev_q_norm"][e:e + 1]),
                            _pad_head(w["ev_k_norm"][e:e + 1]))
    shp = (B_HEADS, s, ATT_W)
    o, lse = _attn_fwd(qp.reshape(shp), kp.reshape(shp), vp.reshape(shp))
    ymix = jnp.concatenate([ya, _unheads(o[:, :, :B_HEAD_DIM]).astype(BF16)], axis=1)
    x1 = _mm(ymix, w["ev_w_out"][e], add=x0, name="ev_out")
    return x1, (hb, z, ymix, q_raw, k_raw, qp, kp, vp, f_t, o, lse)


def _even_bwd(dx1, dx1b, x0, saved, w, e, i):
    hb, z, ymix, q_raw, k_raw, qp, kp, vp, f_t, o, lse = saved
    s = x0.shape[0]
    hs = B_HEADS * s
    hd = B_HEAD_DIM
    d_ymix = _mm(dx1b, w["ev_w_out"][e], tb=True, name="ev_out_dx")
    d_wout = _mm(ymix, dx1b, ta=True, name="ev_out_dw")
    do_raw = _heads_padded(d_ymix[:, A_WIDTH:])
    dop, qpp = _attn_bwd_prep(do_raw, o.reshape(hs, ATT_W), qp, lse.reshape(hs, 1))
    shp = (B_HEADS, s, ATT_W)
    qpp, dop, kp3, vp3 = qpp.reshape(shp), dop.reshape(shp), kp.reshape(shp), vp.reshape(shp)
    dqp = _attn_bwd_dq(qpp, kp3, vp3, dop)
    dkp, dvp = _attn_bwd_dkv(qpp, kp3, vp3, dop)
    d_ft, d_bf = _fgate_bwd(f_t, w["ev_b_fgate"][e][:, None], dqp[:, :, _COL_CQ], -dkp[:, :, _COL_CK])
    dq_raw, d_gq = _rms_bwd_call(q_raw[:, :hd], dqp.reshape(hs, ATT_W)[:, :hd], w["ev_q_norm"][e:e + 1], None,
                                 name="attn_qnorm_bwd", tr=2048, scale=_ATT_SCALE)
    dk_raw, d_gk = _rms_bwd_call(k_raw[:, :hd], dkp.reshape(hs, ATT_W)[:, :hd], w["ev_k_norm"][e:e + 1], None,
                                 name="attn_knorm_bwd", tr=2048)
    b_s = w["ev_b_spatial"][e][:, :, None]
    v_gain = w["ev_v_norm"][e:e + 1]
    dz_uv, d_ws, d_bs, d_vg = _gmlp_bwd(z, d_ymix, w["ev_w_spatial"][e], b_s, v_gain)
    shp64 = (B_HEADS, s, hd)
    dz = jnp.concatenate(
        [dz_uv, _unheads(dq_raw.reshape(shp64)).astype(BF16), _unheads(dk_raw.reshape(shp64)).astype(BF16),
         _unheads(dvp[:, :, :hd]).astype(BF16), d_ft.T.astype(BF16), jnp.zeros((s, IN_COLS_PAD - IN_COLS), BF16)],
        axis=1)
    d_h = _mm(dz, w["ev_w_in"][e], name="ev_in_dx")
    d_win_t = _mm(hb, dz, ta=True, name="ev_in_dw", bn=1344)[:, :IN_COLS].T
    dx0, d_nm = _rms_bwd_call(x0, d_h, w["norm_mix"][i:i + 1], dx1, name="mix_norm_bwd")
    grads = {"ev_w_in": d_win_t, "ev_w_out": d_wout, "ev_b_fgate": d_bf[:, 0], "ev_q_norm": d_gq[0],
             "ev_k_norm": d_gk[0], "ev_v_norm": d_vg[0], "ev_w_spatial": d_ws, "ev_b_spatial": d_bs[:, :, 0],
             "norm_mix": d_nm[0]}
    return dx0, grads


def _s5_disc(a_re, a_im, log_dt, b_re, b_im):
    dt = jnp.exp(log_dt)[:, None]
    lr, li = a_re, a_im
    mag = jnp.exp(lr * dt)
    ab_re, ab_im = mag * jnp.cos(li * dt), mag * jnp.sin(li * dt)
    den = lr * lr + li * li
    nr, ni = ab_re - 1.0, ab_im
    cr = (nr * lr + ni * li) / den
    ci = (ni * lr - nr * li) / den
    bb_re = cr[..., None] * b_re - ci[..., None] * b_im
    bb_im = cr[..., None] * b_im + ci[..., None] * b_re
    return ab_re, ab_im, bb_re, bb_im


_GPB = S5_GROUPS // S5_BLOCKS


def _blockdiag(t):
    a, b = t.shape[1:]
    eye = jnp.eye(_GPB, dtype=t.dtype)
    t = t.reshape(S5_BLOCKS, _GPB, a, 1, b) * eye[None, :, None, :, None]
    return t.reshape(S5_BLOCKS, _GPB * a, _GPB * b)


def _blockdiag_extract(m, a, b):
    eye = jnp.eye(_GPB, dtype=m.dtype)
    m = m.reshape(S5_BLOCKS, _GPB, a, _GPB, b) * eye[None, :, None, :, None]
    return jnp.sum(m, axis=3).reshape(S5_GROUPS, a, b)


_SCAN_ROWS = 8


def _cmul(a_r, a_i, b_r, b_i):
    return a_r * b_r - a_i * b_i, a_r * b_i + a_i * b_r


def _scan_tiles(yr_ref, yi_ref, a_r, a_i, c_r, c_i, *, rev, x_refs=None):
    tb, wl = yr_ref.shape
    ntile = tb // _SCAN_ROWS
    with_acc = x_refs is not None
    if rev:
        a_i = -a_i
    pw = [(a_r, a_i)]
    for _ in range(_SCAN_ROWS - 1):
        pw.append(_cmul(*pw[-1], a_r, a_i))
    sub = lax.broadcasted_iota(jnp.int32, (_SCAN_ROWS, wl), 0)
    dist = (_SCAN_ROWS - 1 - sub) if rev else sub
    zero = jnp.zeros((_SCAN_ROWS, wl), F32)
    steps = []
    for kk in (1, 2, 4):
        steps.append(((_SCAN_ROWS - kk) if rev else kk, jnp.where(dist >= kk, pw[kk - 1][0], zero),
                      jnp.where(dist >= kk, pw[kk - 1][1], zero)))
    e_r, e_i = zero, zero
    for d in range(_SCAN_ROWS):
        e_r = jnp.where(dist == d, pw[d][0], e_r)
        e_i = jnp.where(dist == d, pw[d][1], e_i)
    exit_row = 0 if rev else _SCAN_ROWS - 1

    def tile(q, carry):
        idx = (ntile - 1 - q) if rev else q
        rows = pl.ds(pl.multiple_of(idx * _SCAN_ROWS, _SCAN_ROWS), _SCAN_ROWS)
        y_r, y_i = yr_ref[rows, :], yi_ref[rows, :]
        for sh, k_r, k_i in steps:
            t_r, t_i = _cmul(k_r, k_i, pltpu.roll(y_r, sh, 0), pltpu.roll(y_i, sh, 0))
            y_r, y_i = y_r + t_r, y_i + t_i
        cb_r = jnp.broadcast_to(carry[0], (_SCAN_ROWS, wl))
        cb_i = jnp.broadcast_to(carry[1], (_SCAN_ROWS, wl))
        t_r, t_i = _cmul(e_r, e_i, cb_r, cb_i)
        y_r, y_i = y_r + t_r, y_i + t_i
        yr_ref[rows, :] = y_r
        yi_ref[rows, :] = y_i
        out = (y_r[exit_row:exit_row + 1], y_i[exit_row:exit_row + 1])
        if with_acc:
            n_r = jnp.where(sub == _SCAN_ROWS - 1, cb_r, pltpu.roll(y_r, _SCAN_ROWS - 1, 0))
            n_i = jnp.where(sub == _SCAN_ROWS - 1, cb_i, pltpu.roll(y_i, _SCAN_ROWS - 1, 0))
            s_r, s_i = x_refs[0][rows, :], x_refs[1][rows, :]
            out += (carry[2] + (n_r * s_r + n_i * s_i), carry[3] + (n_i * s_r - n_r * s_i))
        return out

    return lax.fori_loop(0, ntile, tile, (c_r, c_i) + ((zero, zero) if with_acc else ()), unroll=2)


_S5_WL = S5_N // S5_BLOCKS
_S5_CW = D_MODEL // S5_BLOCKS
_TN = (((0,), (0,)), ((), ()))


def _s5_fwd(hb, mats, *, tb=1024):
    s = hb.shape[0]
    tb = min(tb, s)

    def body(h_ref, wbr_ref, wbi_ref, wcr_ref, wci_ref, ar_ref, ai_ref, xr_ref, xi_ref, cp_ref, cr, ci):
        @pl.when(pl.program_id(1) == 0)
        def _():
            cr[...] = jnp.zeros_like(cr)
            ci[...] = jnp.zeros_like(ci)

        hv = h_ref[...]
        xr_ref[...] = jnp.dot(hv, wbr_ref[0], preferred_element_type=F32)
        xi_ref[...] = jnp.dot(hv, wbi_ref[0], preferred_element_type=F32)
        cr[...], ci[...] = _scan_tiles(xr_ref, xi_ref, ar_ref[...], ai_ref[...], cr[...], ci[...], rev=False)
        cp_ref[...] = (jnp.dot(xr_ref[...].astype(BF16), wcr_ref[0], preferred_element_type=F32)
                       + jnp.dot(xi_ref[...].astype(BF16), wci_ref[0], preferred_element_type=F32))

    wide = pl.BlockSpec((tb, _S5_WL), lambda c, t: (t, c))
    narrow = pl.BlockSpec((tb, _S5_CW), lambda c, t: (t, c))
    w_in = pl.BlockSpec((1, _S5_CW, _S5_WL), lambda c, t: (c, 0, 0))
    w_out = pl.BlockSpec((1, _S5_WL, _S5_CW), lambda c, t: (c, 0, 0))
    vec = pl.BlockSpec((1, _S5_WL), lambda c, t: (0, c))
    return pl.pallas_call(
        body, grid=(S5_BLOCKS, s // tb), in_specs=[narrow, w_in, w_in, w_out, w_out, vec, vec],
        out_specs=[wide, wide, narrow],
        out_shape=[jax.ShapeDtypeStruct((s, S5_N), F32)] * 2 + [jax.ShapeDtypeStruct((s, D_MODEL), F32)],
        scratch_shapes=[pltpu.VMEM((1, _S5_WL), F32)] * 2, name="s5_fwd",
        compiler_params=_params(("parallel", "arbitrary")),
    )(hb, mats["wb_re"], mats["wb_im"], mats["wc_re"], mats["wc_imn"], mats["ar"], mats["ai"])


def _s5_bwd(dyb, hb, xr, xi, dhd, mats, *, tb=1024):
    s = hb.shape[0]
    tb = min(tb, s)
    nt = s // tb

    def body(dy_ref, h_ref, xr_ref, xi_ref, dhd_ref, wctr_ref, wcti_ref, wbtr_ref, wbti_ref, ar_ref, ai_ref,
             dh_ref, dar_ref, dai_ref, dwbr_ref, dwbi_ref, dwcr_ref, dwci_ref, lr, li, cr, ci):
        @pl.when(pl.program_id(1) == 0)
        def _():
            for r in (cr, ci, dar_ref, dai_ref, dwbr_ref, dwbi_ref, dwcr_ref, dwci_ref):
                r[...] = jnp.zeros_like(r)

        dyv, hv = dy_ref[...], h_ref[...]
        lr[...] = jnp.dot(dyv, wctr_ref[0], preferred_element_type=F32)
        li[...] = jnp.dot(dyv, wcti_ref[0], preferred_element_type=F32)
        cr[...], ci[...], acc_r, acc_i = _scan_tiles(lr, li, ar_ref[...], ai_ref[...], cr[...], ci[...], rev=True,
                                                     x_refs=(xr_ref, xi_ref))
        dar_ref[...] += jnp.sum(acc_r, axis=0, keepdims=True)
        dai_ref[...] += jnp.sum(acc_i, axis=0, keepdims=True)
        lrb, lib = lr[...].astype(BF16), li[...].astype(BF16)
        dh_ref[...] = (jnp.dot(lrb, wbtr_ref[0], preferred_element_type=F32) + dhd_ref[...]
                       + jnp.dot(lib, wbti_ref[0], preferred_element_type=F32))
        dwbr_ref[0] += lax.dot_general(hv, lrb, _TN, preferred_element_type=F32)
        dwbi_ref[0] += lax.dot_general(hv, lib, _TN, preferred_element_type=F32)
        dwcr_ref[0] += lax.dot_general(dyv, xr_ref[...].astype(BF16), _TN, preferred_element_type=F32)
        dwci_ref[0] += lax.dot_general(dyv, xi_ref[...].astype(BF16), _TN, preferred_element_type=F32)

    wide = pl.BlockSpec((tb, _S5_WL), lambda c, t: (nt - 1 - t, c))
    narrow = pl.BlockSpec((tb, _S5_CW), lambda c, t: (nt - 1 - t, c))
    w_in = pl.BlockSpec((1, _S5_CW, _S5_WL), lambda c, t: (c, 0, 0))
    w_out = pl.BlockSpec((1, _S5_WL, _S5_CW), lambda c, t: (c, 0, 0))
    vec = pl.BlockSpec((1, _S5_WL), lambda c, t: (0, c))
    acc_shape = jax.ShapeDtypeStruct((S5_BLOCKS, _S5_CW, _S5_WL), F32)
    return pl.pallas_call(
        body, grid=(S5_BLOCKS, nt), in_specs=[narrow, narrow, wide, wide, narrow, w_in, w_in, w_out, w_out, vec, vec],
        out_specs=[narrow, vec, vec, w_in, w_in, w_in, w_in],
        out_shape=[jax.ShapeDtypeStruct((s, D_MODEL), F32)] + [jax.ShapeDtypeStruct((1, S5_N), F32)] * 2
        + [acc_shape] * 4,
        scratch_shapes=[pltpu.VMEM((tb, _S5_WL), F32)] * 2 + [pltpu.VMEM((1, _S5_WL), F32)] * 2, name="s5_bwd",
        compiler_params=_params(("parallel", "arbitrary")),
    )(dyb, hb, xr, xi, dhd, mats["wct_re"], mats["wct_imn"], mats["wbt_re"], mats["wbt_im"], mats["ar"], mats["ai"])


def _s5_mats(w, o):
    ab_re, ab_im, bb_re, bb_im = _s5_disc(w["od_a_re"][o], w["od_a_im"][o], w["od_log_dt"][o], w["od_b_re"][o],
                                          w["od_b_im"][o])
    c_re, c_im = w["od_c_re"][o], w["od_c_im"][o]
    tr = lambda t: t.transpose(0, 2, 1)
    bd = lambda t: _blockdiag(t).astype(BF16)
    return {
        "ar": ab_re.reshape(1, S5_N), "ai": ab_im.reshape(1, S5_N),
        "wb_re": bd(tr(bb_re)), "wb_im": bd(tr(bb_im)), "wc_re": bd(tr(c_re)), "wc_imn": bd(-tr(c_im)),
        "wct_re": bd(c_re), "wct_imn": bd(-c_im), "wbt_re": bd(bb_re), "wbt_im": bd(bb_im),
    }


def _odd_fwd(x0, w, o, i):
    mats = _s5_mats(w, o)
    nm = w["norm_mix"][i:i + 1]
    d_row = w["od_d"][o:o + 1]
    hb = _rms_fwd(x0, nm, name="mix_norm")
    xr, xi, cp = _s5_fwd(hb, mats)

    def fn(xv, cpv, gv, dv):
        y = cpv + dv * (xv * _rstd(xv) * gv)
        return y, _gelu(y)

    y, gy = _rw(fn, [x0, cp], [nm, d_row], [(D_MODEL, F32), (D_MODEL, BF16)], tr=512, name="s5_out")
    gg = _mm(gy, w["od_w_glu"][o], tb=True, name="od_glu")
    x1 = _rw(lambda xv, g: xv + g[:, :D_MODEL] * _sigmoid(g[:, D_MODEL:]), [x0, gg], [], [(D_MODEL, F32)], tr=512,
             name="od_glu_out")[0]
    return x1, (hb, xr, xi, y, gy, gg)


def _odd_bwd(dx1, x0, saved, w, o, i):
    hb, xr, xi, y, gy, gg = saved
    mats = _s5_mats(w, o)
    nm = w["norm_mix"][i:i + 1]
    d_row = w["od_d"][o:o + 1]

    def fn_glu(dv, g):
        ga, gb = g[:, :D_MODEL], g[:, D_MODEL:]
        sg = _sigmoid(gb)
        return jnp.concatenate([dv * sg, dv * ga * (sg * (1.0 - sg))], axis=1)

    dgg = _rw(fn_glu, [dx1, gg], [], [(2 * D_MODEL, BF16)], tr=512, name="od_glu_out_bwd")[0]
    d_wglu = _mm(gy, dgg, ta=True, name="od_glu_dw", bn=1024).T
    d_gy = _mm(dgg, w["od_w_glu"][o], name="od_glu_dx")

    def fn_y(dg, yv, xv, gv, dv):
        dy = dg * _gelu_grad(yv)
        h = xv * _rstd(xv) * gv
        return dy, dv * dy, jnp.sum(dy * h, axis=0, keepdims=True)

    dyb, dhd, d_d = _rw(fn_y, [d_gy, y, x0], [nm, d_row], [(D_MODEL, BF16), (D_MODEL, F32)], [(1, D_MODEL)], tr=512,
                        name="s5_out_bwd")
    dh, d_ar, d_ai, d_wb_re, d_wb_im, d_wc_re, d_wc_im = _s5_bwd(dyb, hb, xr, xi, dhd, mats)
    gc, gp = S5_GROUP_CH, S5_STATE
    d_c_re = _blockdiag_extract(d_wc_re, gc, gp)
    d_c_im = -_blockdiag_extract(d_wc_im, gc, gp)
    d_bb_re = _blockdiag_extract(d_wb_re, gc, gp).transpose(0, 2, 1)
    d_bb_im = _blockdiag_extract(d_wb_im, gc, gp).transpose(0, 2, 1)
    dx0, d_nm = _rms_bwd_call(x0, dh, nm, dx1, name="mix_norm_bwd")
    _, vjp = jax.vjp(_s5_disc, w["od_a_re"][o], w["od_a_im"][o], w["od_log_dt"][o], w["od_b_re"][o], w["od_b_im"][o])
    d_a_re, d_a_im, d_log_dt, d_b_re, d_b_im = vjp(
        (d_ar.reshape(S5_GROUPS, S5_STATE), d_ai.reshape(S5_GROUPS, S5_STATE), d_bb_re, d_bb_im))
    grads = {"od_a_re": d_a_re, "od_a_im": d_a_im, "od_log_dt": d_log_dt, "od_b_re": d_b_re, "od_b_im": d_b_im,
             "od_c_re": d_c_re, "od_c_im": d_c_im, "od_d": d_d[0], "od_w_glu": d_wglu, "norm_mix": d_nm[0]}
    return dx0, grads


_HBM = pl.BlockSpec(memory_space=pltpu.HBM)


def _mesh_pos():
    return lax.axis_index("x"), lax.axis_index("y"), lax.axis_index("c")


def _slot(px, py, pc):
    return 4 * px + 2 * py + pc


def _all_gather(x, *, name):
    def body(x_ref, out_ref, send_sems, recv_sems, local_sem):
        mx, my, mc = _mesh_pos()
        me, sibling = (mx, my, mc), (mx, my, 1 - mc)
        chips = [(1 - mx, my), (mx, 1 - my), (1 - mx, 1 - my)]

        def copy(k, block, to, src=None):
            dst = out_ref.at[_slot(*block)]
            return pltpu.make_async_remote_copy(
                src_ref=dst if src is None else src, dst_ref=dst, send_sem=send_sems.at[k], recv_sem=recv_sems.at[k],
                device_id=to, device_id_type=MESH)

        mine = pltpu.make_async_copy(x_ref, out_ref.at[_slot(*me)], local_sem)
        mine.start()
        first = [copy(0, me, sibling, src=x_ref)]
        first += [copy(1 + j, me, (*chip, mc), src=x_ref) for j, chip in enumerate(chips)]
        for cp in first:
            cp.start()
        passed = [copy(4 + j, (*chip, mc), sibling) for j, chip in enumerate(chips)]
        for j, chip in enumerate(chips):
            copy(1 + j, (*chip, mc), me).wait_recv()
            passed[j].start()
        copy(0, sibling, me).wait_recv()
        for j, chip in enumerate(chips):
            copy(4 + j, (*chip, 1 - mc), me).wait_recv()
        for cp in first + passed:
            cp.wait_send()
        mine.wait()

    return pl.pallas_call(
        body, out_shape=jax.ShapeDtypeStruct((N_DEV,) + x.shape, x.dtype), in_specs=[_HBM], out_specs=_HBM,
        scratch_shapes=[pltpu.SemaphoreType.DMA((7,)), pltpu.SemaphoreType.DMA((7,)), pltpu.SemaphoreType.DMA],
        name=name,
    )(x)


def _all_to_all(x, *, name):
    def body(x_ref, out_ref, send_sems, recv_sems, local_sem):
        mx, my, mc = _mesh_pos()
        my_slot = _slot(mx, my, mc)
        mine = pltpu.make_async_copy(x_ref.at[my_slot], out_ref.at[my_slot], local_sem)
        mine.start()
        copies = []
        for k in range(1, N_DEV):
            peer = (1 - mx if k & 4 else mx, 1 - my if k & 2 else my, 1 - mc if k & 1 else mc)
            copies.append(pltpu.make_async_remote_copy(
                src_ref=x_ref.at[_slot(*peer)], dst_ref=out_ref.at[my_slot], send_sem=send_sems.at[k - 1],
                recv_sem=recv_sems.at[k - 1], device_id=peer, device_id_type=MESH))
        for cp in copies:
            cp.start()
        for cp in copies:
            cp.wait()
        mine.wait()

    return pl.pallas_call(
        body, out_shape=jax.ShapeDtypeStruct(x.shape, x.dtype), in_specs=[_HBM], out_specs=_HBM,
        scratch_shapes=[pltpu.SemaphoreType.DMA((7,)), pltpu.SemaphoreType.DMA((7,)), pltpu.SemaphoreType.DMA],
        name=name,
    )(x)


_ADAMW_PARTS_BLOCK_BYTES = 8 * 1024 * 1024


def _adamw(parts, w, m, v, *, name):
    r, c = w.shape
    tr = r
    for cand in range(8, r + 1, 8):
        if r % cand == 0 and N_DEV * cand * c * 4 <= _ADAMW_PARTS_BLOCK_BYTES:
            tr = cand

    def body(p_ref, w_ref, m_ref, v_ref, g_ref, d_ref, nm_ref, nv_ref):
        g = p_ref[0]
        for k in range(1, N_DEV):
            g = g + p_ref[k]
        nm = ADAM_B1 * m_ref[...] + (1.0 - ADAM_B1) * g
        nv = ADAM_B2 * v_ref[...] + (1.0 - ADAM_B2) * (g * g)
        m_hat = nm / (1.0 - ADAM_B1 ** ADAM_STEP)
        v_hat = nv / (1.0 - ADAM_B2 ** ADAM_STEP)
        g_ref[...] = g
        d_ref[...] = -ADAM_LR * (m_hat / (jnp.sqrt(v_hat) + ADAM_EPS) + ADAM_WD * w_ref[...])
        nm_ref[...] = nm
        nv_ref[...] = nv

    blk = pl.BlockSpec((tr, c), lambda i: (i, 0))
    return pl.pallas_call(
        body, grid=(r // tr,), in_specs=[pl.BlockSpec((N_DEV, tr, c), lambda i: (0, i, 0)), blk, blk, blk],
        out_specs=[blk] * 4, out_shape=[jax.ShapeDtypeStruct((r, c), F32)] * 4, name=name,
        compiler_params=_params(("parallel",)),
    )(parts, w, m, v)


_FLAT_COLS = 1024
_FLAT_ROW_ALIGN = 128

_SHARD_AXIS = {
    "norm_mix": None, "norm_ffn": None, "ev_w_in": 2, "ev_b_fgate": None, "ev_q_norm": None, "ev_k_norm": None,
    "ev_v_norm": None, "ev_w_spatial": None, "ev_b_spatial": None, "ev_w_out": 1, "od_a_re": None, "od_a_im": None,
    "od_log_dt": None, "od_b_re": None, "od_b_im": None, "od_c_re": None, "od_c_im": None, "od_d": 1, "od_w_glu": 2,
    "ffn_w_up": 2, "ffn_conv_w": 2, "ffn_conv_b": None, "ffn_w_down": 1, "ple_w_proj": 2, "ple_w_gate": 1,
}
_WEIGHTS = list(_SHARD_AXIS)
_REPLICATED = [n for n in _WEIGHTS if _SHARD_AXIS[n] is None]
_COL_SHARDED = ["ev_w_in", "od_w_glu", "ffn_w_up", "ple_w_proj"]
_ROW_SHARDED = ["ev_w_out", "ffn_w_down", "ple_w_gate"]
_SMALL_SHARDED = ["od_d", "ffn_conv_w"]


def _pack(arrays, lead=()):
    nl = len(lead)
    flat = jnp.concatenate([a.reshape(lead + (-1,)) for a in arrays], axis=nl)
    n = flat.shape[nl]
    chunk = _FLAT_COLS * _FLAT_ROW_ALIGN
    total = -(-n // chunk) * chunk
    flat = jnp.pad(flat, [(0, 0)] * nl + [(0, total - n)])
    return flat.reshape(lead + (total // _FLAT_COLS, _FLAT_COLS))


def _unpack(buf, shapes, lead=()):
    nl = len(lead)
    flat = buf.reshape(lead + (-1,))
    out, off = [], 0
    for shp in shapes:
        n = math.prod(shp)
        out.append(lax.slice_in_dim(flat, off, off + n, axis=nl).reshape(lead + tuple(shp)))
        off += n
    return out


def _join_shards(g, axis):
    g = jnp.moveaxis(g, 0, axis)
    shp = g.shape
    return g.reshape(shp[:axis] + (shp[axis] * shp[axis + 1],) + shp[axis + 2:])


def _split_shards(full, axis):
    shp = full.shape
    g = full.reshape(shp[:axis] + (N_DEV, shp[axis] // N_DEV) + shp[axis + 1:])
    return jnp.moveaxis(g, axis, 0)


def _local_step(x, p, tgt, w):
    saved = []
    h = x
    for i in range(DEPTH):
        x0 = h
        if i % 2 == 0:
            x1, sv_mix = _even_fwd(x0, w, i // 2, i)
        else:
            x1, sv_mix = _odd_fwd(x0, w, i // 2, i)
        x2, sv_ffn = _ffn_fwd(x1, w, i)
        x3, sv_ple = _ple_fwd(x2, p[i], w, i)
        saved.append((x0, x1, x2, sv_mix, sv_ffn, sv_ple))
        h = x3
    loss, dh = _loss_and_grad(h, tgt)
    per_layer = {}
    for i in reversed(range(DEPTH)):
        x0, x1, x2, sv_mix, sv_ffn, sv_ple = saved[i]
        dx2, dx2b, g_ple = _ple_bwd(dh, x2, p[i], sv_ple, w, i)
        dx1, dx1b, g_ffn = _ffn_bwd(dx2, dx2b, x1, sv_ffn, w, i)
        if i % 2 == 0:
            dh, g_mix = _even_bwd(dx1, dx1b, x0, sv_mix, w, i // 2, i)
        else:
            dh, g_mix = _odd_bwd(dx1, x0, sv_mix, w, i // 2, i)
        for name, g in {**g_ple, **g_ffn, **g_mix}.items():
            per_layer.setdefault(name, []).append(g)
    grads = {name: gs[::-1] for name, gs in per_layer.items()}
    return loss, dh, grads


def kernel(x, p, norm_mix, norm_ffn, ev_w_in, ev_b_fgate, ev_q_norm, ev_k_norm, ev_v_norm, ev_w_spatial, ev_b_spatial, ev_w_out, od_a_re, od_a_im, od_log_dt, od_b_re, od_b_im, od_c_re, od_c_im, od_d, od_w_glu, ffn_w_up, ffn_conv_w, ffn_conv_b, ffn_w_down, ple_w_proj, ple_w_gate, loss_target, m_norm_mix, m_norm_ffn, m_ev_w_in, m_ev_b_fgate, m_ev_q_norm, m_ev_k_norm, m_ev_v_norm, m_ev_w_spatial, m_ev_b_spatial, m_ev_w_out, m_od_a_re, m_od_a_im, m_od_log_dt, m_od_b_re, m_od_b_im, m_od_c_re, m_od_c_im, m_od_d, m_od_w_glu, m_ffn_w_up, m_ffn_conv_w, m_ffn_conv_b, m_ffn_w_down, m_ple_w_proj, m_ple_w_gate, v_norm_mix, v_norm_ffn, v_ev_w_in, v_ev_b_fgate, v_ev_q_norm, v_ev_k_norm, v_ev_v_norm, v_ev_w_spatial, v_ev_b_spatial, v_ev_w_out, v_od_a_re, v_od_a_im, v_od_log_dt, v_od_b_re, v_od_b_im, v_od_c_re, v_od_c_im, v_od_d, v_od_w_glu, v_ffn_w_up, v_ffn_conv_w, v_ffn_conv_b, v_ffn_w_down, v_ple_w_proj, v_ple_w_gate):
    args = locals()
    wl = {n: args[n] for n in _WEIGHTS}
    ml = {n: args["m_" + n] for n in _WEIGHTS}
    vl = {n: args["v_" + n] for n in _WEIGHTS}

    kinds = ("g", "d", "m", "v")

    def local(n, layer, a):
        return a[n][layer].T if n in _COL_SHARDED else a[n][layer]

    w = {n: wl[n] for n in _REPLICATED}
    for n in _COL_SHARDED + _ROW_SHARDED:
        w[n] = []
        for layer in range(wl[n].shape[0]):
            g = _all_gather(local(n, layer, wl).astype(BF16), name="gather_" + n)
            full = g.reshape(N_DEV * g.shape[1], g.shape[2])
            if n == "ev_w_in":
                full = jnp.pad(full, ((0, IN_COLS_PAD - IN_COLS), (0, 0)))
            w[n].append(full)
    small = _all_gather(_pack([wl[n] for n in _SMALL_SHARDED]), name="gather_small_weights")
    for n, g in zip(_SMALL_SHARDED, _unpack(small, [wl[n].shape for n in _SMALL_SHARDED], lead=(N_DEV,))):
        w[n] = _join_shards(g, _SHARD_AXIS[n])

    loss, grad_x, grads = _local_step(x[0], p[:, 0], loss_target[0], w)
    loss = lax.psum(loss, ("x", "y", "c"))

    out = {kind: {} for kind in kinds}
    for n in _COL_SHARDED + _ROW_SHARDED:
        layers = []
        for layer, g in enumerate(grads[n]):
            parts = _all_to_all(g.reshape(N_DEV, g.shape[0] // N_DEV, g.shape[1]), name="scatter_" + n)
            res = _adamw(parts, local(n, layer, wl), local(n, layer, ml), local(n, layer, vl), name="adamw_" + n)
            layers.append([r.T if n in _COL_SHARDED else r for r in res])
        for k, kind in enumerate(kinds):
            out[kind][n] = jnp.stack([lay[k] for lay in layers])

    send = _pack([_split_shards(jnp.stack(grads[n]), _SHARD_AXIS[n]) for n in _SMALL_SHARDED], lead=(N_DEV,))
    parts = _all_to_all(send, name="scatter_small_grads")
    shapes = [wl[n].shape for n in _SMALL_SHARDED]
    res = _adamw(parts, _pack([wl[n] for n in _SMALL_SHARDED]), _pack([ml[n] for n in _SMALL_SHARDED]),
                 _pack([vl[n] for n in _SMALL_SHARDED]), name="adamw_small")
    for kind, buf in zip(kinds, res):
        out[kind].update(zip(_SMALL_SHARDED, _unpack(buf, shapes)))

    parts = _all_gather(_pack([jnp.stack(grads[n]) for n in _REPLICATED]), name="gather_replicated_grads")
    shapes = [wl[n].shape for n in _REPLICATED]
    res = _adamw(parts, _pack([wl[n] for n in _REPLICATED]), _pack([ml[n] for n in _REPLICATED]),
                 _pack([vl[n] for n in _REPLICATED]), name="adamw_replicated")
    for kind, buf in zip(kinds, res):
        out[kind].update(zip(_REPLICATED, _unpack(buf, shapes)))

    return (loss, grad_x[None], *[out["g"][n] for n in _WEIGHTS], *[out["d"][n] for n in _WEIGHTS],
            *[out["m"][n] for n in _WEIGHTS], *[out["v"][n] for n in _WEIGHTS])
```

```python
import functools
import math

import jax
import jax.numpy as jnp
from jax import lax
from jax.experimental import pallas as pl
from jax.experimental.pallas import tpu as pltpu

F32 = jnp.float32
BF16 = jnp.bfloat16

D_MODEL = 1024
DEPTH = 4
A_GROUPS = 4
A_CHUNK = 128
A_WIDTH = 512
B_HEADS = 8
B_HEAD_DIM = 64
B_WIDTH = 512
IN_COLS = 2 * A_WIDTH + 3 * B_WIDTH + B_HEADS
IN_COLS_PAD = 2688
S5_GROUP_CH = 16
S5_GROUPS = 64
S5_STATE = 64
S5_N = S5_GROUPS * S5_STATE
S5_BLOCKS = 8
D_FF = 2816
PLE_DIM = 256
EPS = 1e-6
NEG_INF = -1e30
N_DEV = 8

ADAM_LR = 0.001
ADAM_B1 = 0.9
ADAM_B2 = 0.999
ADAM_EPS = 1e-08
ADAM_WD = 0.01
ADAM_STEP = 10

VMEM_LIMIT_BYTES = 56 * 1024 * 1024
MESH = pl.DeviceIdType.MESH


def _params(sem, vmem=VMEM_LIMIT_BYTES):
    return pltpu.CompilerParams(dimension_semantics=sem, vmem_limit_bytes=vmem)


_GELU_K = math.sqrt(2.0 / math.pi)


def _gelu(x):
    return x * (0.5 * (1.0 + jnp.tanh(_GELU_K * (x + 0.044715 * (x * x * x)))))


def _gelu_grad(x):
    t = jnp.tanh(_GELU_K * (x + 0.044715 * (x * x * x)))
    return 0.5 * (1.0 + t) + 0.5 * x * (1.0 - t * t) * (_GELU_K * (1.0 + 3.0 * 0.044715 * (x * x)))


def _sigmoid(x):
    return 0.5 * jnp.tanh(0.5 * x) + 0.5


def _rstd(x):
    return lax.rsqrt(jnp.mean(x * x, axis=-1, keepdims=True) + EPS)


def _rms_bwd(x, dy, g):
    r = _rstd(x)
    xh = x * r
    dyg = dy if g is None else dy * g
    dx = r * (dyg - xh * jnp.mean(dyg * xh, axis=-1, keepdims=True))
    return dx, dy * xh


def _shift_down(blk, halo, k):
    tr = blk.shape[0]
    r = pltpu.roll(blk, k, 0)
    hr = pltpu.roll(halo, k, 0)
    first = jnp.where(lax.broadcasted_iota(jnp.int32, hr.shape, 0) < k, hr, r[0:8])
    return jnp.concatenate([first, r[8:tr]], axis=0)


def _shift_up(blk, halo, k):
    tr = blk.shape[0]
    r = pltpu.roll(blk, tr - k, 0)
    hr = pltpu.roll(halo, 8 - k, 0)
    last = jnp.where(lax.broadcasted_iota(jnp.int32, hr.shape, 0) >= 8 - k, hr, r[tr - 8:tr])
    return jnp.concatenate([r[0:tr - 8], last], axis=0)


def _rw(fn, rows, consts, outs, accs=(), *, tr, name, prev=(), nxt=(), widths=None):
    s = rows[0].shape[0]
    tr = min(tr, s)
    n = s // tr
    nr, nc, npv, nnx, no, na = len(rows), len(consts), len(prev), len(nxt), len(outs), len(accs)
    widths = widths or [None] * nr

    def body(*refs):
        ins, out_refs = refs[:nr + nc + npv + nnx], refs[nr + nc + npv + nnx:]
        i = pl.program_id(0)
        vals = [r[...] for r in ins[:nr + nc]]
        vals += [jnp.where(i == 0, 0.0, r[...]) for r in ins[nr + nc:nr + nc + npv]]
        vals += [jnp.where(i == n - 1, 0.0, r[...]) for r in ins[nr + nc + npv:]]
        res = fn(*vals)
        if not isinstance(res, (tuple, list)):
            res = (res,)
        for k in range(no):
            out_refs[k][...] = res[k].astype(out_refs[k].dtype)
        if na:
            @pl.when(i == 0)
            def _():
                for k in range(na):
                    out_refs[no + k][...] = jnp.zeros_like(out_refs[no + k])

            for k in range(na):
                out_refs[no + k][...] += res[no + k]

    in_specs = []
    for a, w in zip(rows, widths):
        if w is None:
            in_specs.append(pl.BlockSpec((tr, a.shape[1]), lambda i: (i, 0)))
        else:
            in_specs.append(pl.BlockSpec((tr, w[1]), functools.partial(lambda i, cb: (i, cb), cb=w[0])))
    for c in consts:
        in_specs.append(pl.BlockSpec(c.shape, functools.partial(lambda i, nd: (0,) * nd, nd=c.ndim)))
    t8 = tr // 8
    for a in prev:
        in_specs.append(pl.BlockSpec((8, a.shape[1]), lambda i: (jnp.maximum(i * t8 - 1, 0), 0)))
    for a in nxt:
        in_specs.append(pl.BlockSpec((8, a.shape[1]), lambda i: (jnp.minimum((i + 1) * t8, s // 8 - 1), 0)))
    out_shape = [jax.ShapeDtypeStruct((s, w), dt) for w, dt in outs]
    out_specs = [pl.BlockSpec((tr, w), lambda i: (i, 0)) for w, _ in outs]
    out_shape += [jax.ShapeDtypeStruct(a, F32) for a in accs]
    out_specs += [pl.BlockSpec(a, lambda i: (0, 0)) for a in accs]
    res = pl.pallas_call(
        body, grid=(n,), in_specs=in_specs, out_specs=out_specs, out_shape=out_shape, name=name,
        compiler_params=_params(("arbitrary",) if na else ("parallel",)),
    )(*rows, *consts, *prev, *nxt)
    return res


def _pick(n, cap):
    if n <= cap:
        return n
    best = None
    for d in range(128, cap + 1, 128):
        if n % d == 0:
            best = d
    assert best is not None, (n, cap)
    return best


def _mm(a, b, *, ta=False, tb=False, out_dtype=F32, add=None, epilogue=None, name, bm=1024, bn=512, bk=None):
    (k_dim, m) = a.shape if ta else a.shape[::-1]
    n = b.shape[0] if tb else b.shape[1]
    assert (b.shape[1] if tb else b.shape[0]) == k_dim
    if bk is None:
        bk = 1024 if ta else 2816
    bm, bn, bk = _pick(m, bm), _pick(n, bn), _pick(k_dim, bk)
    nk = k_dim // bk
    dims = (((0 if ta else 1,), (1 if tb else 0,)), ((), ()))
    assert add is None or epilogue is None
    epi_fn, extra, out_dtypes = epilogue or (None, [add] if add is not None else [], (out_dtype,))
    n_extra, n_out = len(extra), len(out_dtypes)

    def body(*refs):
        a_ref, b_ref = refs[0], refs[1]
        extra_refs = refs[2:2 + n_extra]
        out_refs = refs[2 + n_extra:2 + n_extra + n_out]
        p = lax.dot_general(a_ref[...].astype(BF16), b_ref[...].astype(BF16), dims, preferred_element_type=F32)

        def finish(acc):
            if epi_fn is not None:
                res = epi_fn(acc, *[r[...] for r in extra_refs])
            else:
                res = (acc + extra_refs[0][...],) if n_extra else (acc,)
            for o_ref, r in zip(out_refs, res):
                o_ref[...] = r.astype(o_ref.dtype)

        if nk == 1:
            finish(p)
        else:
            acc_ref = refs[2 + n_extra + n_out]
            kk = pl.program_id(2)

            @pl.when(kk == 0)
            def _():
                acc_ref[...] = p

            @pl.when(kk > 0)
            def _():
                acc_ref[...] += p

            @pl.when(kk == nk - 1)
            def _():
                finish(acc_ref[...])

    a_spec = pl.BlockSpec((bk, bm), lambda i, j, k: (k, i)) if ta else pl.BlockSpec((bm, bk), lambda i, j, k: (i, k))
    b_spec = pl.BlockSpec((bn, bk), lambda i, j, k: (j, k)) if tb else pl.BlockSpec((bk, bn), lambda i, j, k: (k, j))
    tile = pl.BlockSpec((bm, bn), lambda i, j, k: (i, j))
    res = pl.pallas_call(
        body, grid=(m // bm, n // bn, nk), in_specs=[a_spec, b_spec] + [tile] * n_extra,
        out_specs=[tile] * n_out, out_shape=[jax.ShapeDtypeStruct((m, n), dt) for dt in out_dtypes],
        scratch_shapes=[pltpu.VMEM((bm, bn), F32)] if nk > 1 else [],
        name=name, compiler_params=_params(("parallel", "parallel", "arbitrary")),
    )(a, b, *extra)
    return res if epilogue is not None else res[0]


def _rms_fwd(x, g, *, name):
    if g is None:
        return _rw(lambda xv: xv * _rstd(xv), [x], [], [(x.shape[1], BF16)], tr=512, name=name)[0]
    return _rw(lambda xv, gv: xv * _rstd(xv) * gv, [x], [g], [(x.shape[1], BF16)], tr=512, name=name)[0]


def _rms_bwd_call(x, dy, g, dx_add, *, name, tr=512, scale=None, bf16_copy=False):
    w = x.shape[1]
    has_g, has_add = g is not None, dx_add is not None

    def fn(*v):
        xv, dyv = v[0], v[1]
        if scale is not None:
            dyv = dyv * scale
        gv = v[2 + has_add] if has_g else None
        dx, dg = _rms_bwd(xv, dyv, gv)
        if has_add:
            dx = dx + v[2]
        return (dx,) + ((dx,) if bf16_copy else ()) + ((jnp.sum(dg, axis=0, keepdims=True),) if has_g else ())

    rows = [x, dy] + ([dx_add] if has_add else [])
    outs = [(w, F32)] + ([(w, BF16)] if bf16_copy else [])
    res = _rw(fn, rows, [g] if has_g else [], outs, [(1, w)] if has_g else [], tr=tr, name=name)
    return tuple(res) if has_g else tuple(res) + (None,)


def _conv_taps(hup, halo, cw):
    h1 = _shift_down(hup, halo, 1)
    h2 = _shift_down(hup, halo, 2)
    return h1, h2


def _ffn_fwd(x1, w, i):
    h2b = _rms_fwd(x1, w["norm_ffn"][i:i + 1], name="ffn_norm")
    hup = _mm(h2b, w["ffn_w_up"][i], tb=True, name="ffn_up", bn=1408)
    cw, cb = w["ffn_conv_w"][i], w["ffn_conv_b"][i:i + 1]

    def fn(hv, cwv, cbv, halo):
        h1, h2 = _conv_taps(hv, halo, cwv)
        hc = cbv + cwv[0:1] * h2
        hc = hc + cwv[1:2] * h1
        hc = hc + cwv[2:3] * hv
        g, up = hc[:, :D_FF], hc[:, D_FF:]
        return (g * _sigmoid(g)) * up

    a = _rw(fn, [hup], [cw, cb], [(D_FF, BF16)], tr=128, name="ffn_conv_gate", prev=[hup])[0]
    x2 = _mm(a, w["ffn_w_down"][i], add=x1, name="ffn_down")
    return x2, (h2b, hup, a)


def _ffn_bwd(dx2, dx2b, x1, saved, w, i):
    h2b, hup, a = saved
    cw, cb = w["ffn_conv_w"][i], w["ffn_conv_b"][i:i + 1]
    da = _mm(dx2b, w["ffn_w_down"][i], tb=True, name="ffn_down_dx", bn=1408)
    d_wdown = _mm(a, dx2b, ta=True, name="ffn_down_dw", bm=1408)

    def fn1(hv, dav, cwv, cbv, h_prev, h_next, da_next):
        tr = hv.shape[0]
        he = jnp.concatenate([hv, h_next], axis=0)
        dae = jnp.concatenate([dav, da_next], axis=0)
        h1, h2 = _conv_taps(he, h_prev, cwv)
        hc = cbv + cwv[0:1] * h2
        hc = hc + cwv[1:2] * h1
        hc = hc + cwv[2:3] * he
        g, up = hc[:, :D_FF], hc[:, D_FF:]
        sg = _sigmoid(g)
        d_up = dae * (g * sg)
        d_g = dae * up * (sg * (1.0 + g * (1.0 - sg)))
        d_hce = jnp.concatenate([d_g, d_up], axis=1)
        rows_e = tr + 8
        d_hup = (cwv[2:3] * d_hce[:tr] + cwv[1:2] * pltpu.roll(d_hce, rows_e - 1, 0)[:tr]
                 + cwv[0:1] * pltpu.roll(d_hce, rows_e - 2, 0)[:tr])
        d_hc = d_hce[:tr]
        col = lambda v: jnp.sum(v, axis=0, keepdims=True)
        return d_hup, col(d_hc), col(d_hc * h2[:tr]), col(d_hc * h1[:tr]), col(d_hc * hv)

    w2 = 2 * D_FF
    d_hup, d_cb, d_cw0, d_cw1, d_cw2 = _rw(fn1, [hup, da], [cw, cb], [(w2, BF16)], [(1, w2)] * 4, tr=128,
                                           name="ffn_conv_gate_bwd", prev=[hup], nxt=[hup, da])
    d_h2 = _mm(d_hup, w["ffn_w_up"][i], name="ffn_up_dx")
    d_wup_t = _mm(h2b, d_hup, ta=True, name="ffn_up_dw", bn=1408).T
    dx1, dx1b, d_norm = _rms_bwd_call(x1, d_h2, w["norm_ffn"][i:i + 1], dx2, name="ffn_norm_bwd", bf16_copy=True)
    grads = {"ffn_w_up": d_wup_t, "ffn_w_down": d_wdown, "ffn_conv_b": d_cb[0],
             "ffn_conv_w": jnp.concatenate([d_cw0, d_cw1, d_cw2], axis=0), "norm_ffn": d_norm[0]}
    return dx1, dx1b, grads


def _ple_fwd(x2, p_i, w, i):
    rn = _rms_fwd(x2, None, name="ple_norm")
    gl = _mm(rn, w["ple_w_gate"][i], name="ple_gate")
    x3 = _mm(p_i, w["ple_w_proj"][i], tb=True, name="ple_proj",
             epilogue=(lambda pp, xv, g: (xv + _sigmoid(g) * pp,), [x2, gl], (F32,)))[0]
    return x3, (rn, gl)


def _ple_bwd(dx3, x2, p_i, saved, w, i):
    rn, gl = saved

    def fn(pp, dv, g):
        sg = _sigmoid(g)
        return dv * sg, dv * pp * (sg * (1.0 - sg))

    d_pp, d_pre = _mm(p_i, w["ple_w_proj"][i], tb=True, name="ple_proj_bwd", epilogue=(fn, [dx3, gl], (BF16, BF16)))
    d_wproj_t = _mm(p_i, d_pp, ta=True, name="ple_proj_dw", bn=1024).T
    d_wgate = _mm(rn, d_pre, ta=True, name="ple_gate_dw")
    d_rn = _mm(d_pre, w["ple_w_gate"][i], tb=True, name="ple_gate_dx")
    dx2, dx2b, _ = _rms_bwd_call(x2, d_rn, None, dx3, name="ple_norm_bwd", bf16_copy=True)
    return dx2, dx2b, {"ple_w_proj": d_wproj_t, "ple_w_gate": d_wgate}


def _loss_and_grad(xf, tgt):
    def fn(xv, tv):
        d = xv - tv
        part = 0.5 * jnp.sum(jnp.mean(d * d, axis=-1, keepdims=True), axis=0, keepdims=True)
        return d * (1.0 / D_MODEL), part

    dx, loss = _rw(fn, [xf, tgt], [], [(D_MODEL, F32)], [(1, 1)], tr=512, name="loss_head")
    return loss[0, 0], dx


def _tril_mask():
    shp = (A_CHUNK, A_CHUNK)
    return lax.broadcasted_iota(jnp.int32, shp, 0) >= lax.broadcasted_iota(jnp.int32, shp, 1)


def _gmlp_fwd(z, w_s, b_s, v_gain, *, tr=512):
    s = z.shape[0]
    tr = min(tr, s)
    gw = A_WIDTH // A_GROUPS

    def body(z_ref, w_ref, b_ref, g_ref, y_ref):
        tril = _tril_mask()
        for g in range(A_GROUPS):
            wg = jnp.where(tril, w_ref[g], 0.0).astype(BF16)
            cols = slice(g * gw, (g + 1) * gw)
            vcols = slice(A_WIDTH + g * gw, A_WIDTH + (g + 1) * gw)
            for c in range(tr // A_CHUNK):
                rows = slice(c * A_CHUNK, (c + 1) * A_CHUNK)
                v = _gelu(z_ref[rows, vcols])
                vn = v * _rstd(v) * g_ref[:, cols]
                sv = jnp.dot(wg, vn.astype(BF16), preferred_element_type=F32) + b_ref[g]
                y_ref[rows, cols] = (_gelu(z_ref[rows, cols]) * sv).astype(BF16)

    return pl.pallas_call(
        body, grid=(s // tr,),
        in_specs=[pl.BlockSpec((tr, 2 * A_WIDTH), lambda i: (i, 0)),
                  pl.BlockSpec(w_s.shape, lambda i: (0, 0, 0)), pl.BlockSpec(b_s.shape, lambda i: (0, 0, 0)),
                  pl.BlockSpec(v_gain.shape, lambda i: (0, 0))],
        out_specs=pl.BlockSpec((tr, A_WIDTH), lambda i: (i, 0)),
        out_shape=jax.ShapeDtypeStruct((s, A_WIDTH), BF16), name="gmlp_fwd", compiler_params=_params(("parallel",)),
    )(z, w_s, b_s, v_gain)


def _gmlp_bwd(z, d_ymix, w_s, b_s, v_gain, *, tr=512):
    s = z.shape[0]
    tr = min(tr, s)
    gw = A_WIDTH // A_GROUPS

    def body(z_ref, dy_ref, w_ref, b_ref, g_ref, dz_ref, dw_ref, db_ref, dg_ref):
        @pl.when(pl.program_id(0) == 0)
        def _():
            dw_ref[...] = jnp.zeros_like(dw_ref)
            db_ref[...] = jnp.zeros_like(db_ref)
            dg_ref[...] = jnp.zeros_like(dg_ref)

        tril = _tril_mask()
        for g in range(A_GROUPS):
            wg = jnp.where(tril, w_ref[g], 0.0).astype(BF16)
            cols = slice(g * gw, (g + 1) * gw)
            vcols = slice(A_WIDTH + g * gw, A_WIDTH + (g + 1) * gw)
            gain = g_ref[:, cols]
            for c in range(tr // A_CHUNK):
                rows = slice(c * A_CHUNK, (c + 1) * A_CHUNK)
                va, ua = z_ref[rows, vcols], z_ref[rows, cols]
                v = _gelu(va)
                r = _rstd(v)
                vh = v * r
                vnb = (vh * gain).astype(BF16)
                sv = jnp.dot(wg, vnb, preferred_element_type=F32) + b_ref[g]
                dy = dy_ref[rows, cols]
                d_sv = dy * _gelu(ua)
                dz_ref[rows, cols] = (dy * sv * _gelu_grad(ua)).astype(BF16)
                d_svb = d_sv.astype(BF16)
                d_vn = lax.dot_general(wg, d_svb, (((0,), (0,)), ((), ())), preferred_element_type=F32)
                dwp = lax.dot_general(d_svb, vnb, (((1,), (1,)), ((), ())), preferred_element_type=F32)
                dw_ref[g] += jnp.where(tril, dwp, 0.0)
                db_ref[g] += jnp.sum(d_sv, axis=1, keepdims=True)
                dg_ref[:, cols] += jnp.sum(d_vn * vh, axis=0, keepdims=True)
                d_vh = d_vn * gain
                d_v = r * (d_vh - vh * jnp.mean(d_vh * vh, axis=-1, keepdims=True))
                dz_ref[rows, vcols] = (d_v * _gelu_grad(va)).astype(BF16)

    return pl.pallas_call(
        body, grid=(s // tr,),
        in_specs=[pl.BlockSpec((tr, 2 * A_WIDTH), lambda i: (i, 0)), pl.BlockSpec((tr, A_WIDTH), lambda i: (i, 0)),
                  pl.BlockSpec(w_s.shape, lambda i: (0, 0, 0)), pl.BlockSpec(b_s.shape, lambda i: (0, 0, 0)),
                  pl.BlockSpec(v_gain.shape, lambda i: (0, 0))],
        out_specs=[pl.BlockSpec((tr, 2 * A_WIDTH), lambda i: (i, 0)), pl.BlockSpec(w_s.shape, lambda i: (0, 0, 0)),
                   pl.BlockSpec(b_s.shape, lambda i: (0, 0, 0)), pl.BlockSpec(v_gain.shape, lambda i: (0, 0))],
        out_shape=[jax.ShapeDtypeStruct((s, 2 * A_WIDTH), BF16), jax.ShapeDtypeStruct(w_s.shape, F32),
                   jax.ShapeDtypeStruct(b_s.shape, F32), jax.ShapeDtypeStruct(v_gain.shape, F32)],
        name="gmlp_bwd", compiler_params=_params(("arbitrary",)),
    )(z, d_ymix, w_s, b_s, v_gain)


def _dot3(x, ub):
    x1 = x.astype(BF16)
    r1 = x - x1.astype(F32)
    x2 = r1.astype(BF16)
    x3 = (r1 - x2.astype(F32)).astype(BF16)
    d = lambda a: jnp.dot(a, ub, preferred_element_type=F32)
    return d(x1) + d(x2) + d(x3)


def _log_sigmoid(x):
    return jnp.minimum(x, 0.0) - jnp.log(1.0 + jnp.exp(-jnp.abs(x)))


def _fgate_fwd(f_t, b_col, *, tb=256):
    h, s = f_t.shape
    tb = min(tb, s)

    def body(f_ref, b_ref, c_ref, carry):
        @pl.when(pl.program_id(0) == 0)
        def _():
            carry[...] = jnp.zeros_like(carry)

        lf = _log_sigmoid(f_ref[...] + b_ref[...])
        shp = (tb, tb)
        upper = (lax.broadcasted_iota(jnp.int32, shp, 0) <= lax.broadcasted_iota(jnp.int32, shp, 1)).astype(BF16)
        c_ref[...] = _dot3(lf, upper) + carry[...]
        carry[...] += jnp.sum(lf, axis=1, keepdims=True)

    return pl.pallas_call(
        body, grid=(s // tb,),
        in_specs=[pl.BlockSpec((h, tb), lambda i: (0, i)), pl.BlockSpec((h, 1), lambda i: (0, 0))],
        out_specs=pl.BlockSpec((h, tb), lambda i: (0, i)), out_shape=jax.ShapeDtypeStruct((h, s), F32),
        scratch_shapes=[pltpu.VMEM((h, 1), F32)], name="fgate_fwd", compiler_params=_params(("arbitrary",)),
    )(f_t, b_col)


def _fgate_bwd(f_t, b_col, dc_a, dc_b, *, tb=256):
    h, s = f_t.shape
    tb = min(tb, s)
    n = s // tb

    def body(f_ref, b_ref, da_ref, db_ref, df_ref, dbias_ref, carry):
        @pl.when(pl.program_id(0) == 0)
        def _():
            carry[...] = jnp.zeros_like(carry)
            dbias_ref[...] = jnp.zeros_like(dbias_ref)

        dc = da_ref[...] + db_ref[...]
        shp = (tb, tb)
        lower = (lax.broadcasted_iota(jnp.int32, shp, 0) >= lax.broadcasted_iota(jnp.int32, shp, 1)).astype(BF16)
        d_lf = _dot3(dc, lower) + carry[...]
        carry[...] += jnp.sum(dc, axis=1, keepdims=True)
        df = d_lf * (1.0 - _sigmoid(f_ref[...] + b_ref[...]))
        df_ref[...] = df
        dbias_ref[...] += jnp.sum(df, axis=1, keepdims=True)

    blk = pl.BlockSpec((h, tb), lambda i: (0, n - 1 - i))
    return pl.pallas_call(
        body, grid=(n,), in_specs=[blk, pl.BlockSpec((h, 1), lambda i: (0, 0)), blk, blk],
        out_specs=[blk, pl.BlockSpec((h, 1), lambda i: (0, 0))],
        out_shape=[jax.ShapeDtypeStruct((h, s), F32), jax.ShapeDtypeStruct((h, 1), F32)],
        scratch_shapes=[pltpu.VMEM((h, 1), F32)], name="fgate_bwd", compiler_params=_params(("arbitrary",)),
    )(f_t, b_col, dc_a, dc_b)


_NT = (((1,), (1,)), ((), ()))


def _causal(shape, row0, col0, transposed):
    r = lax.broadcasted_iota(jnp.int32, shape, 0) + row0
    c = lax.broadcasted_iota(jnp.int32, shape, 1) + col0
    return (r <= c) if transposed else (c <= r)


_ATT_SCALE = B_HEAD_DIM ** -0.5
ATT_W = 128
_COL_CQ, _COL_CK, _COL_LSE, _COL_DELTA = 64, 67, 70, 64
_ATT_BLK = 1024
_ATT_SUB = 512


def _split3(x):
    h = x.astype(BF16).astype(F32)
    r = x - h
    m = r.astype(BF16).astype(F32)
    return h, m, (r - m).astype(BF16).astype(F32)


def _put_cols(base, lane, col0, parts):
    for t, pv in enumerate(parts):
        base = jnp.where(lane == col0 + t, pv, base)
    return base


def _attn_prep(q_raw, k_raw, v_raw, c_col, gq, gk):
    def fn(q, k, v, c, gqv, gkv):
        lane = lax.broadcasted_iota(jnp.int32, q.shape, 1)
        rq = lax.rsqrt(jnp.sum(q * q, axis=-1, keepdims=True) * (1.0 / B_HEAD_DIM) + EPS)
        rk = lax.rsqrt(jnp.sum(k * k, axis=-1, keepdims=True) * (1.0 / B_HEAD_DIM) + EPS)
        ch, cm, cl = _split3(c)
        one = jnp.ones_like(c)
        qq = _put_cols(q * rq * gqv * _ATT_SCALE, lane, _COL_CQ, (ch, cm, cl))
        qq = _put_cols(qq, lane, _COL_CK, (one, one, one))
        kk = _put_cols(k * rk * gkv, lane, _COL_CQ, (one, one, one))
        kk = _put_cols(kk, lane, _COL_CK, (-ch, -cm, -cl))
        kk = _put_cols(kk, lane, _COL_LSE, (one, one, one))
        vv = _put_cols(v, lane, _COL_DELTA, (one, one, one))
        return qq, kk, vv

    return _rw(fn, [q_raw, k_raw, v_raw, c_col], [gq, gk], [(ATT_W, BF16)] * 3, tr=2048, name="attn_prep")


def _attn_bwd_prep(do_raw, o, qp, lse):
    def fn(dov, ov, qv, lv):
        lane = lax.broadcasted_iota(jnp.int32, dov.shape, 1)
        dh, dm, dl = _split3(jnp.sum(dov * ov, axis=-1, keepdims=True))
        lh, lm, ll = _split3(lv)
        return (_put_cols(dov, lane, _COL_DELTA, (-dh, -dm, -dl)),
                _put_cols(qv.astype(F32), lane, _COL_LSE, (-lh, -lm, -ll)))

    return _rw(fn, [do_raw, o, qp, lse], [], [(ATT_W, BF16)] * 2, tr=2048, name="attn_bwd_prep")


def _attn_fwd(qp, kp, vp, *, blk=_ATT_BLK):
    h, s, d = qp.shape
    b = min(blk, s)
    n = s // b
    sub = min(_ATT_SUB, b)
    ns = b // sub

    def body(q_ref, k_ref, v_ref, o_ref, lse_ref):
        i = pl.program_id(1)
        qs = [q_ref[0, t * sub:(t + 1) * sub, :] for t in range(ns)]

        def tile(j, state, masked):
            off = pl.multiple_of(j * b, b)
            kk = k_ref[0, pl.ds(off, b), :]
            vv = v_ref[0, pl.ds(off, b), :]
            new = []
            for t in range(ns):
                m, l, acc = state[3 * t:3 * t + 3]
                nk = (t + 1) * sub if masked else b
                sc = lax.dot_general(qs[t], kk[:nk], _NT, preferred_element_type=F32)
                if masked:
                    sc = jnp.where(_causal((sub, nk), t * sub, 0, False), sc, NEG_INF)
                m_new = jnp.maximum(m, jnp.max(sc, axis=1, keepdims=True))
                alpha = jnp.exp(m - m_new)
                p = jnp.exp(sc - m_new)
                new += [m_new, alpha * l + jnp.sum(p, axis=1, keepdims=True),
                        alpha * acc + jnp.dot(p.astype(BF16), vv[:nk], preferred_element_type=F32)]
            return tuple(new)

        init = (jnp.full((sub, 1), NEG_INF, F32), jnp.zeros((sub, 1), F32), jnp.zeros((sub, d), F32)) * ns
        state = lax.fori_loop(0, i, lambda j, st: tile(j, st, False), init)
        state = tile(i, state, True)
        for t in range(ns):
            m, l, acc = state[3 * t:3 * t + 3]
            o_ref[0, t * sub:(t + 1) * sub, :] = acc / l
            lse_ref[0, t * sub:(t + 1) * sub, :] = m + jnp.log(l)

    blk_spec = pl.BlockSpec((1, b, d), lambda hh, i: (hh, i, 0))
    full = pl.BlockSpec((1, s, d), lambda hh, i: (hh, 0, 0))
    return pl.pallas_call(
        body, grid=(h, n), in_specs=[blk_spec, full, full],
        out_specs=[blk_spec, pl.BlockSpec((1, b, 1), lambda hh, i: (hh, i, 0))],
        out_shape=[jax.ShapeDtypeStruct((h, s, d), F32), jax.ShapeDtypeStruct((h, s, 1), F32)],
        name="attn_fwd", compiler_params=_params(("parallel", "arbitrary")),
    )(qp, kp, vp)


def _attn_bwd_dq(qpp, kp, vp, dop, *, blk=_ATT_BLK):
    h, s, d = qpp.shape
    b = min(blk, s)
    n = s // b
    sub = min(_ATT_SUB, b)
    ns = b // sub

    def body(q_ref, do_ref, k_ref, v_ref, dq_ref):
        i = pl.program_id(1)
        qs = [q_ref[0, t * sub:(t + 1) * sub, :] for t in range(ns)]
        dos = [do_ref[0, t * sub:(t + 1) * sub, :] for t in range(ns)]

        def tile(j, state, masked):
            off = pl.multiple_of(j * b, b)
            kk = k_ref[0, pl.ds(off, b), :]
            vv = v_ref[0, pl.ds(off, b), :]
            new = []
            for t in range(ns):
                nk = (t + 1) * sub if masked else b
                p = jnp.exp(lax.dot_general(qs[t], kk[:nk], _NT, preferred_element_type=F32))
                if masked:
                    p = jnp.where(_causal((sub, nk), t * sub, 0, False), p, 0.0)
                ds = p * lax.dot_general(dos[t], vv[:nk], _NT, preferred_element_type=F32)
                new.append(state[t] + jnp.dot(ds.astype(BF16), kk[:nk], preferred_element_type=F32))
            return tuple(new)

        state = lax.fori_loop(0, i, lambda j, st: tile(j, st, False), (jnp.zeros((sub, d), F32),) * ns)
        state = tile(i, state, True)
        for t in range(ns):
            dq_ref[0, t * sub:(t + 1) * sub, :] = state[t]

    blk_spec = pl.BlockSpec((1, b, d), lambda hh, i: (hh, i, 0))
    full = pl.BlockSpec((1, s, d), lambda hh, i: (hh, 0, 0))
    return pl.pallas_call(
        body, grid=(h, n), in_specs=[blk_spec, blk_spec, full, full], out_specs=blk_spec,
        out_shape=jax.ShapeDtypeStruct((h, s, d), F32), name="attn_bwd_dq",
        compiler_params=_params(("parallel", "arbitrary")),
    )(qpp, dop, kp, vp)


def _attn_bwd_dkv(qpp, kp, vp, dop, *, blk=_ATT_BLK):
    h, s, d = qpp.shape
    b = min(blk, s)
    n = s // b
    sub = min(_ATT_SUB, b)
    ns = b // sub

    def body(k_ref, v_ref, q_ref, do_ref, dk_ref, dv_ref):
        j = pl.program_id(1)
        ks = [k_ref[0, t * sub:(t + 1) * sub, :] for t in range(ns)]
        vs = [v_ref[0, t * sub:(t + 1) * sub, :] for t in range(ns)]

        def tile(i, state, masked):
            off = pl.multiple_of(i * b, b)
            qq = q_ref[0, pl.ds(off, b), :]
            dd = do_ref[0, pl.ds(off, b), :]
            new = []
            for t in range(ns):
                q0 = t * sub if masked else 0
                qt, dt = qq[q0:], dd[q0:]
                pt = jnp.exp(lax.dot_general(ks[t], qt, _NT, preferred_element_type=F32))
                if masked:
                    pt = jnp.where(_causal((sub, b - q0), q0, q0, True), pt, 0.0)
                dst = pt * lax.dot_general(vs[t], dt, _NT, preferred_element_type=F32)
                new += [state[2 * t] + jnp.dot(dst.astype(BF16), qt, preferred_element_type=F32),
                        state[2 * t + 1] + jnp.dot(pt.astype(BF16), dt, preferred_element_type=F32)]
            return tuple(new)

        state = tile(j, (jnp.zeros((sub, d), F32),) * (2 * ns), True)
        state = lax.fori_loop(j + 1, n, lambda i, st: tile(i, st, False), state)
        for t in range(ns):
            dk_ref[0, t * sub:(t + 1) * sub, :] = state[2 * t]
            dv_ref[0, t * sub:(t + 1) * sub, :] = state[2 * t + 1]

    blk_spec = pl.BlockSpec((1, b, d), lambda hh, j: (hh, j, 0))
    full = pl.BlockSpec((1, s, d), lambda hh, j: (hh, 0, 0))
    return pl.pallas_call(
        body, grid=(h, n), in_specs=[blk_spec, blk_spec, full, full], out_specs=[blk_spec, blk_spec],
        out_shape=[jax.ShapeDtypeStruct((h, s, d), F32)] * 2, name="attn_bwd_dkv",
        compiler_params=_params(("parallel", "arbitrary")),
    )(kp, vp, qpp, dop)


def _heads_padded(a):
    s = a.shape[0]
    t = a.reshape(s, B_HEADS, B_HEAD_DIM).transpose(1, 0, 2)
    return jnp.pad(t, ((0, 0), (0, 0), (0, ATT_W - B_HEAD_DIM))).reshape(B_HEADS * s, ATT_W)


def _unheads(a):
    s = a.shape[1]
    return a.transpose(1, 0, 2).reshape(s, B_WIDTH)


def _pad_head(g):
    return jnp.pad(g, ((0, 0), (0, ATT_W - B_HEAD_DIM)))


def _even_fwd(x0, w, e, i):
    s = x0.shape[0]
    hs = B_HEADS * s
    hb = _rms_fwd(x0, w["norm_mix"][i:i + 1], name="mix_norm")
    z = _mm(hb, w["ev_w_in"][e], tb=True, name="ev_in", bn=896)
    b_s = w["ev_b_spatial"][e][:, :, None]
    v_gain = w["ev_v_norm"][e:e + 1]
    ya = _gmlp_fwd(z, w["ev_w_spatial"][e], b_s, v_gain)
    q_raw = _heads_padded(z[:, 2 * A_WIDTH:2 * A_WIDTH + B_WIDTH])
    k_raw = _heads_padded(z[:, 2 * A_WIDTH + B_WIDTH:2 * A_WIDTH + 2 * B_WIDTH])
    v_raw = _heads_padded(z[:, 2 * A_WIDTH + 2 * B_WIDTH:2 * A_WIDTH + 3 * B_WIDTH])
    f_t = z[:, IN_COLS - B_HEADS:IN_COLS].T
    c = _fgate_fwd(f_t, w["ev_b_fgate"][e][:, None])
    qp, kp, vp = _attn_prep(q_raw, k_raw, v_raw, c.reshape(hs, 1), _pad_head(w["ev_q_norm"][e:e + 1]),
                            _pad_head(w["ev_k_norm"][e:e + 1]))
    shp = (B_HEADS, s, ATT_W)
    o, lse = _attn_fwd(qp.reshape(shp), kp.reshape(shp), vp.reshape(shp))
    ymix = jnp.concatenate([ya, _unheads(o[:, :, :B_HEAD_DIM]).astype(BF16)], axis=1)
    x1 = _mm(ymix, w["ev_w_out"][e], add=x0, name="ev_out")
    return x1, (hb, z, ymix, q_raw, k_raw, qp, kp, vp, f_t, o, lse)


def _even_bwd(dx1, dx1b, x0, saved, w, e, i):
    hb, z, ymix, q_raw, k_raw, qp, kp, vp, f_t, o, lse = saved
    s = x0.shape[0]
    hs = B_HEADS * s
    hd = B_HEAD_DIM
    d_ymix = _mm(dx1b, w["ev_w_out"][e], tb=True, name="ev_out_dx")
    d_wout = _mm(ymix, dx1b, ta=True, name="ev_out_dw")
    do_raw = _heads_padded(d_ymix[:, A_WIDTH:])
    dop, qpp = _attn_bwd_prep(do_raw, o.reshape(hs, ATT_W), qp, lse.reshape(hs, 1))
    shp = (B_HEADS, s, ATT_W)
    qpp, dop, kp3, vp3 = qpp.reshape(shp), dop.reshape(shp), kp.reshape(shp), vp.reshape(shp)
    dqp = _attn_bwd_dq(qpp, kp3, vp3, dop)
    dkp, dvp = _attn_bwd_dkv(qpp, kp3, vp3, dop)
    d_ft, d_bf = _fgate_bwd(f_t, w["ev_b_fgate"][e][:, None], dqp[:, :, _COL_CQ], -dkp[:, :, _COL_CK])
    dq_raw, d_gq = _rms_bwd_call(q_raw[:, :hd], dqp.reshape(hs, ATT_W)[:, :hd], w["ev_q_norm"][e:e + 1], None,
                                 name="attn_qnorm_bwd", tr=2048, scale=_ATT_SCALE)
    dk_raw, d_gk = _rms_bwd_call(k_raw[:, :hd], dkp.reshape(hs, ATT_W)[:, :hd], w["ev_k_norm"][e:e + 1], None,
                                 name="attn_knorm_bwd", tr=2048)
    b_s = w["ev_b_spatial"][e][:, :, None]
    v_gain = w["ev_v_norm"][e:e + 1]
    dz_uv, d_ws, d_bs, d_vg = _gmlp_bwd(z, d_ymix, w["ev_w_spatial"][e], b_s, v_gain)
    shp64 = (B_HEADS, s, hd)
    dz = jnp.concatenate(
        [dz_uv, _unheads(dq_raw.reshape(shp64)).astype(BF16), _unheads(dk_raw.reshape(shp64)).astype(BF16),
         _unheads(dvp[:, :, :hd]).astype(BF16), d_ft.T.astype(BF16), jnp.zeros((s, IN_COLS_PAD - IN_COLS), BF16)],
        axis=1)
    d_h = _mm(dz, w["ev_w_in"][e], name="ev_in_dx")
    d_win_t = _mm(hb, dz, ta=True, name="ev_in_dw", bn=1344)[:, :IN_COLS].T
    dx0, d_nm = _rms_bwd_call(x0, d_h, w["norm_mix"][i:i + 1], dx1, name="mix_norm_bwd")
    grads = {"ev_w_in": d_win_t, "ev_w_out": d_wout, "ev_b_fgate": d_bf[:, 0], "ev_q_norm": d_gq[0],
             "ev_k_norm": d_gk[0], "ev_v_norm": d_vg[0], "ev_w_spatial": d_ws, "ev_b_spatial": d_bs[:, :, 0],
             "norm_mix": d_nm[0]}
    return dx0, grads


def _s5_disc(a_re, a_im, log_dt, b_re, b_im):
    dt = jnp.exp(log_dt)[:, None]
    lr, li = a_re, a_im
    mag = jnp.exp(lr * dt)
    ab_re, ab_im = mag * jnp.cos(li * dt), mag * jnp.sin(li * dt)
    den = lr * lr + li * li
    nr, ni = ab_re - 1.0, ab_im
    cr = (nr * lr + ni * li) / den
    ci = (ni * lr - nr * li) / den
    bb_re = cr[..., None] * b_re - ci[..., None] * b_im
    bb_im = cr[..., None] * b_im + ci[..., None] * b_re
    return ab_re, ab_im, bb_re, bb_im


_GPB = S5_GROUPS // S5_BLOCKS


def _blockdiag(t):
    a, b = t.shape[1:]
    eye = jnp.eye(_GPB, dtype=t.dtype)
    t = t.reshape(S5_BLOCKS, _GPB, a, 1, b) * eye[None, :, None, :, None]
    return t.reshape(S5_BLOCKS, _GPB * a, _GPB * b)


def _blockdiag_extract(m, a, b):
    eye = jnp.eye(_GPB, dtype=m.dtype)
    m = m.reshape(S5_BLOCKS, _GPB, a, _GPB, b) * eye[None, :, None, :, None]
    return jnp.sum(m, axis=3).reshape(S5_GROUPS, a, b)


_SCAN_ROWS = 8


def _cmul(a_r, a_i, b_r, b_i):
    return a_r * b_r - a_i * b_i, a_r * b_i + a_i * b_r


def _scan_tiles(yr_ref, yi_ref, a_r, a_i, c_r, c_i, *, rev, x_refs=None):
    tb, wl = yr_ref.shape
    ntile = tb // _SCAN_ROWS
    with_acc = x_refs is not None
    if rev:
        a_i = -a_i
    pw = [(a_r, a_i)]
    for _ in range(_SCAN_ROWS - 1):
        pw.append(_cmul(*pw[-1], a_r, a_i))
    sub = lax.broadcasted_iota(jnp.int32, (_SCAN_ROWS, wl), 0)
    dist = (_SCAN_ROWS - 1 - sub) if rev else sub
    zero = jnp.zeros((_SCAN_ROWS, wl), F32)
    steps = []
    for kk in (1, 2, 4):
        steps.append(((_SCAN_ROWS - kk) if rev else kk, jnp.where(dist >= kk, pw[kk - 1][0], zero),
                      jnp.where(dist >= kk, pw[kk - 1][1], zero)))
    e_r, e_i = zero, zero
    for d in range(_SCAN_ROWS):
        e_r = jnp.where(dist == d, pw[d][0], e_r)
        e_i = jnp.where(dist == d, pw[d][1], e_i)
    exit_row = 0 if rev else _SCAN_ROWS - 1

    def tile(q, carry):
        idx = (ntile - 1 - q) if rev else q
        rows = pl.ds(pl.multiple_of(idx * _SCAN_ROWS, _SCAN_ROWS), _SCAN_ROWS)
        y_r, y_i = yr_ref[rows, :], yi_ref[rows, :]
        for sh, k_r, k_i in steps:
            t_r, t_i = _cmul(k_r, k_i, pltpu.roll(y_r, sh, 0), pltpu.roll(y_i, sh, 0))
            y_r, y_i = y_r + t_r, y_i + t_i
        cb_r = jnp.broadcast_to(carry[0], (_SCAN_ROWS, wl))
        cb_i = jnp.broadcast_to(carry[1], (_SCAN_ROWS, wl))
        t_r, t_i = _cmul(e_r, e_i, cb_r, cb_i)
        y_r, y_i = y_r + t_r, y_i + t_i
        yr_ref[rows, :] = y_r
        yi_ref[rows, :] = y_i
        out = (y_r[exit_row:exit_row + 1], y_i[exit_row:exit_row + 1])
        if with_acc:
            n_r = jnp.where(sub == _SCAN_ROWS - 1, cb_r, pltpu.roll(y_r, _SCAN_ROWS - 1, 0))
            n_i = jnp.where(sub == _SCAN_ROWS - 1, cb_i, pltpu.roll(y_i, _SCAN_ROWS - 1, 0))
            s_r, s_i = x_refs[0][rows, :], x_refs[1][rows, :]
            out += (carry[2] + (n_r * s_r + n_i * s_i), carry[3] + (n_i * s_r - n_r * s_i))
        return out

    return lax.fori_loop(0, ntile, tile, (c_r, c_i) + ((zero, zero) if with_acc else ()), unroll=2)


_S5_WL = S5_N // S5_BLOCKS
_S5_CW = D_MODEL // S5_BLOCKS
_TN = (((0,), (0,)), ((), ()))


def _s5_fwd(hb, mats, *, tb=1024):
    s = hb.shape[0]
    tb = min(tb, s)

    def body(h_ref, wbr_ref, wbi_ref, wcr_ref, wci_ref, ar_ref, ai_ref, xr_ref, xi_ref, cp_ref, cr, ci):
        @pl.when(pl.program_id(1) == 0)
        def _():
            cr[...] = jnp.zeros_like(cr)
            ci[...] = jnp.zeros_like(ci)

        hv = h_ref[...]
        xr_ref[...] = jnp.dot(hv, wbr_ref[0], preferred_element_type=F32)
        xi_ref[...] = jnp.dot(hv, wbi_ref[0], preferred_element_type=F32)
        cr[...], ci[...] = _scan_tiles(xr_ref, xi_ref, ar_ref[...], ai_ref[...], cr[...], ci[...], rev=False)
        cp_ref[...] = (jnp.dot(xr_ref[...].astype(BF16), wcr_ref[0], preferred_element_type=F32)
                       + jnp.dot(xi_ref[...].astype(BF16), wci_ref[0], preferred_element_type=F32))

    wide = pl.BlockSpec((tb, _S5_WL), lambda c, t: (t, c))
    narrow = pl.BlockSpec((tb, _S5_CW), lambda c, t: (t, c))
    w_in = pl.BlockSpec((1, _S5_CW, _S5_WL), lambda c, t: (c, 0, 0))
    w_out = pl.BlockSpec((1, _S5_WL, _S5_CW), lambda c, t: (c, 0, 0))
    vec = pl.BlockSpec((1, _S5_WL), lambda c, t: (0, c))
    return pl.pallas_call(
        body, grid=(S5_BLOCKS, s // tb), in_specs=[narrow, w_in, w_in, w_out, w_out, vec, vec],
        out_specs=[wide, wide, narrow],
        out_shape=[jax.ShapeDtypeStruct((s, S5_N), F32)] * 2 + [jax.ShapeDtypeStruct((s, D_MODEL), F32)],
        scratch_shapes=[pltpu.VMEM((1, _S5_WL), F32)] * 2, name="s5_fwd",
        compiler_params=_params(("parallel", "arbitrary")),
    )(hb, mats["wb_re"], mats["wb_im"], mats["wc_re"], mats["wc_imn"], mats["ar"], mats["ai"])


def _s5_bwd(dyb, hb, xr, xi, dhd, mats, *, tb=1024):
    s = hb.shape[0]
    tb = min(tb, s)
    nt = s // tb

    def body(dy_ref, h_ref, xr_ref, xi_ref, dhd_ref, wctr_ref, wcti_ref, wbtr_ref, wbti_ref, ar_ref, ai_ref,
             dh_ref, dar_ref, dai_ref, dwbr_ref, dwbi_ref, dwcr_ref, dwci_ref, lr, li, cr, ci):
        @pl.when(pl.program_id(1) == 0)
        def _():
            for r in (cr, ci, dar_ref, dai_ref, dwbr_ref, dwbi_ref, dwcr_ref, dwci_ref):
                r[...] = jnp.zeros_like(r)

        dyv, hv = dy_ref[...], h_ref[...]
        lr[...] = jnp.dot(dyv, wctr_ref[0], preferred_element_type=F32)
        li[...] = jnp.dot(dyv, wcti_ref[0], preferred_element_type=F32)
        cr[...], ci[...], acc_r, acc_i = _scan_tiles(lr, li, ar_ref[...], ai_ref[...], cr[...], ci[...], rev=True,
                                                     x_refs=(xr_ref, xi_ref))
        dar_ref[...] += jnp.sum(acc_r, axis=0, keepdims=True)
        dai_ref[...] += jnp.sum(acc_i, axis=0, keepdims=True)
        lrb, lib = lr[...].astype(BF16), li[...].astype(BF16)
        dh_ref[...] = (jnp.dot(lrb, wbtr_ref[0], preferred_element_type=F32) + dhd_ref[...]
                       + jnp.dot(lib, wbti_ref[0], preferred_element_type=F32))
        dwbr_ref[0] += lax.dot_general(hv, lrb, _TN, preferred_element_type=F32)
        dwbi_ref[0] += lax.dot_general(hv, lib, _TN, preferred_element_type=F32)
        dwcr_ref[0] += lax.dot_general(dyv, xr_ref[...].astype(BF16), _TN, preferred_element_type=F32)
        dwci_ref[0] += lax.dot_general(dyv, xi_ref[...].astype(BF16), _TN, preferred_element_type=F32)

    wide = pl.BlockSpec((tb, _S5_WL), lambda c, t: (nt - 1 - t, c))
    narrow = pl.BlockSpec((tb, _S5_CW), lambda c, t: (nt - 1 - t, c))
    w_in = pl.BlockSpec((1, _S5_CW, _S5_WL), lambda c, t: (c, 0, 0))
    w_out = pl.BlockSpec((1, _S5_WL, _S5_CW), lambda c, t: (c, 0, 0))
    vec = pl.BlockSpec((1, _S5_WL), lambda c, t: (0, c))
    acc_shape = jax.ShapeDtypeStruct((S5_BLOCKS, _S5_CW, _S5_WL), F32)
    return pl.pallas_call(
        body, grid=(S5_BLOCKS, nt), in_specs=[narrow, narrow, wide, wide, narrow, w_in, w_in, w_out, w_out, vec, vec],
        out_specs=[narrow, vec, vec, w_in, w_in, w_in, w_in],
        out_shape=[jax.ShapeDtypeStruct((s, D_MODEL), F32)] + [jax.ShapeDtypeStruct((1, S5_N), F32)] * 2
        + [acc_shape] * 4,
        scratch_shapes=[pltpu.VMEM((tb, _S5_WL), F32)] * 2 + [pltpu.VMEM((1, _S5_WL), F32)] * 2, name="s5_bwd",
        compiler_params=_params(("parallel", "arbitrary")),
    )(dyb, hb, xr, xi, dhd, mats["wct_re"], mats["wct_imn"], mats["wbt_re"], mats["wbt_im"], mats["ar"], mats["ai"])


def _s5_mats(w, o):
    ab_re, ab_im, bb_re, bb_im = _s5_disc(w["od_a_re"][o], w["od_a_im"][o], w["od_log_dt"][o], w["od_b_re"][o],
                                          w["od_b_im"][o])
    c_re, c_im = w["od_c_re"][o], w["od_c_im"][o]
    tr = lambda t: t.transpose(0, 2, 1)
    bd = lambda t: _blockdiag(t).astype(BF16)
    return {
        "ar": ab_re.reshape(1, S5_N), "ai": ab_im.reshape(1, S5_N),
        "wb_re": bd(tr(bb_re)), "wb_im": bd(tr(bb_im)), "wc_re": bd(tr(c_re)), "wc_imn": bd(-tr(c_im)),
        "wct_re": bd(c_re), "wct_imn": bd(-c_im), "wbt_re": bd(bb_re), "wbt_im": bd(bb_im),
    }


def _odd_fwd(x0, w, o, i):
    mats = _s5_mats(w, o)
    nm = w["norm_mix"][i:i + 1]
    d_row = w["od_d"][o:o + 1]
    hb = _rms_fwd(x0, nm, name="mix_norm")
    xr, xi, cp = _s5_fwd(hb, mats)

    def fn(xv, cpv, gv, dv):
        y = cpv + dv * (xv * _rstd(xv) * gv)
        return y, _gelu(y)

    y, gy = _rw(fn, [x0, cp], [nm, d_row], [(D_MODEL, F32), (D_MODEL, BF16)], tr=512, name="s5_out")
    gg = _mm(gy, w["od_w_glu"][o], tb=True, name="od_glu")
    x1 = _rw(lambda xv, g: xv + g[:, :D_MODEL] * _sigmoid(g[:, D_MODEL:]), [x0, gg], [], [(D_MODEL, F32)], tr=512,
             name="od_glu_out")[0]
    return x1, (hb, xr, xi, y, gy, gg)


def _odd_bwd(dx1, x0, saved, w, o, i):
    hb, xr, xi, y, gy, gg = saved
    mats = _s5_mats(w, o)
    nm = w["norm_mix"][i:i + 1]
    d_row = w["od_d"][o:o + 1]

    def fn_glu(dv, g):
        ga, gb = g[:, :D_MODEL], g[:, D_MODEL:]
        sg = _sigmoid(gb)
        return jnp.concatenate([dv * sg, dv * ga * (sg * (1.0 - sg))], axis=1)

    dgg = _rw(fn_glu, [dx1, gg], [], [(2 * D_MODEL, BF16)], tr=512, name="od_glu_out_bwd")[0]
    d_wglu = _mm(gy, dgg, ta=True, name="od_glu_dw", bn=1024).T
    d_gy = _mm(dgg, w["od_w_glu"][o], name="od_glu_dx")

    def fn_y(dg, yv, xv, gv, dv):
        dy = dg * _gelu_grad(yv)
        h = xv * _rstd(xv) * gv
        return dy, dv * dy, jnp.sum(dy * h, axis=0, keepdims=True)

    dyb, dhd, d_d = _rw(fn_y, [d_gy, y, x0], [nm, d_row], [(D_MODEL, BF16), (D_MODEL, F32)], [(1, D_MODEL)], tr=512,
                        name="s5_out_bwd")
    dh, d_ar, d_ai, d_wb_re, d_wb_im, d_wc_re, d_wc_im = _s5_bwd(dyb, hb, xr, xi, dhd, mats)
    gc, gp = S5_GROUP_CH, S5_STATE
    d_c_re = _blockdiag_extract(d_wc_re, gc, gp)
    d_c_im = -_blockdiag_extract(d_wc_im, gc, gp)
    d_bb_re = _blockdiag_extract(d_wb_re, gc, gp).transpose(0, 2, 1)
    d_bb_im = _blockdiag_extract(d_wb_im, gc, gp).transpose(0, 2, 1)
    dx0, d_nm = _rms_bwd_call(x0, dh, nm, dx1, name="mix_norm_bwd")
    _, vjp = jax.vjp(_s5_disc, w["od_a_re"][o], w["od_a_im"][o], w["od_log_dt"][o], w["od_b_re"][o], w["od_b_im"][o])
    d_a_re, d_a_im, d_log_dt, d_b_re, d_b_im = vjp(
        (d_ar.reshape(S5_GROUPS, S5_STATE), d_ai.reshape(S5_GROUPS, S5_STATE), d_bb_re, d_bb_im))
    grads = {"od_a_re": d_a_re, "od_a_im": d_a_im, "od_log_dt": d_log_dt, "od_b_re": d_b_re, "od_b_im": d_b_im,
             "od_c_re": d_c_re, "od_c_im": d_c_im, "od_d": d_d[0], "od_w_glu": d_wglu, "norm_mix": d_nm[0]}
    return dx0, grads


_HBM = pl.BlockSpec(memory_space=pltpu.HBM)


def _mesh_pos():
    return lax.axis_index("x"), lax.axis_index("y"), lax.axis_index("c")


def _slot(px, py, pc):
    return 4 * px + 2 * py + pc


def _all_gather(x, *, name):
    def body(x_ref, out_ref, send_sems, recv_sems, local_sem):
        mx, my, mc = _mesh_pos()
        me, sibling = (mx, my, mc), (mx, my, 1 - mc)
        chips = [(1 - mx, my), (mx, 1 - my), (1 - mx, 1 - my)]

        def copy(k, block, to, src=None):
            dst = out_ref.at[_slot(*block)]
            return pltpu.make_async_remote_copy(
                src_ref=dst if src is None else src, dst_ref=dst, send_sem=send_sems.at[k], recv_sem=recv_sems.at[k],
                device_id=to, device_id_type=MESH)

        mine = pltpu.make_async_copy(x_ref, out_ref.at[_slot(*me)], local_sem)
        mine.start()
        first = [copy(0, me, sibling, src=x_ref)]
        first += [copy(1 + j, me, (*chip, mc), src=x_ref) for j, chip in enumerate(chips)]
        for cp in first:
            cp.start()
        passed = [copy(4 + j, (*chip, mc), sibling) for j, chip in enumerate(chips)]
        for j, chip in enumerate(chips):
            copy(1 + j, (*chip, mc), me).wait_recv()
            passed[j].start()
        copy(0, sibling, me).wait_recv()
        for j, chip in enumerate(chips):
            copy(4 + j, (*chip, 1 - mc), me).wait_recv()
        for cp in first + passed:
            cp.wait_send()
        mine.wait()

    return pl.pallas_call(
        body, out_shape=jax.ShapeDtypeStruct((N_DEV,) + x.shape, x.dtype), in_specs=[_HBM], out_specs=_HBM,
        scratch_shapes=[pltpu.SemaphoreType.DMA((7,)), pltpu.SemaphoreType.DMA((7,)), pltpu.SemaphoreType.DMA],
        name=name,
    )(x)


def _all_to_all(x, *, name):
    def body(x_ref, out_ref, send_sems, recv_sems, local_sem):
        mx, my, mc = _mesh_pos()
        my_slot = _slot(mx, my, mc)
        mine = pltpu.make_async_copy(x_ref.at[my_slot], out_ref.at[my_slot], local_sem)
        mine.start()
        copies = []
        for k in range(1, N_DEV):
            peer = (1 - mx if k & 4 else mx, 1 - my if k & 2 else my, 1 - mc if k & 1 else mc)
            copies.append(pltpu.make_async_remote_copy(
                src_ref=x_ref.at[_slot(*peer)], dst_ref=out_ref.at[my_slot], send_sem=send_sems.at[k - 1],
                recv_sem=recv_sems.at[k - 1], device_id=peer, device_id_type=MESH))
        for cp in copies:
            cp.start()
        for cp in copies:
            cp.wait()
        mine.wait()

    return pl.pallas_call(
        body, out_shape=jax.ShapeDtypeStruct(x.shape, x.dtype), in_specs=[_HBM], out_specs=_HBM,
        scratch_shapes=[pltpu.SemaphoreType.DMA((7,)), pltpu.SemaphoreType.DMA((7,)), pltpu.SemaphoreType.DMA],
        name=name,
    )(x)


_ADAMW_PARTS_BLOCK_BYTES = 8 * 1024 * 1024


def _adamw(parts, w, m, v, *, name):
    r, c = w.shape
    tr = r
    for cand in range(8, r + 1, 8):
        if r % cand == 0 and N_DEV * cand * c * 4 <= _ADAMW_PARTS_BLOCK_BYTES:
            tr = cand

    def body(p_ref, w_ref, m_ref, v_ref, g_ref, d_ref, nm_ref, nv_ref):
        g = p_ref[0]
        for k in range(1, N_DEV):
            g = g + p_ref[k]
        nm = ADAM_B1 * m_ref[...] + (1.0 - ADAM_B1) * g
        nv = ADAM_B2 * v_ref[...] + (1.0 - ADAM_B2) * (g * g)
        m_hat = nm / (1.0 - ADAM_B1 ** ADAM_STEP)
        v_hat = nv / (1.0 - ADAM_B2 ** ADAM_STEP)
        g_ref[...] = g
        d_ref[...] = -ADAM_LR * (m_hat / (jnp.sqrt(v_hat) + ADAM_EPS) + ADAM_WD * w_ref[...])
        nm_ref[...] = nm
        nv_ref[...] = nv

    blk = pl.BlockSpec((tr, c), lambda i: (i, 0))
    return pl.pallas_call(
        body, grid=(r // tr,), in_specs=[pl.BlockSpec((N_DEV, tr, c), lambda i: (0, i, 0)), blk, blk, blk],
        out_specs=[blk] * 4, out_shape=[jax.ShapeDtypeStruct((r, c), F32)] * 4, name=name,
        compiler_params=_params(("parallel",)),
    )(parts, w, m, v)


_FLAT_COLS = 1024
_FLAT_ROW_ALIGN = 128

_SHARD_AXIS = {
    "norm_mix": None, "norm_ffn": None, "ev_w_in": 2, "ev_b_fgate": None, "ev_q_norm": None, "ev_k_norm": None,
    "ev_v_norm": None, "ev_w_spatial": None, "ev_b_spatial": None, "ev_w_out": 1, "od_a_re": None, "od_a_im": None,
    "od_log_dt": None, "od_b_re": None, "od_b_im": None, "od_c_re": None, "od_c_im": None, "od_d": 1, "od_w_glu": 2,
    "ffn_w_up": 2, "ffn_conv_w": 2, "ffn_conv_b": None, "ffn_w_down": 1, "ple_w_proj": 2, "ple_w_gate": 1,
}
_WEIGHTS = list(_SHARD_AXIS)
_REPLICATED = [n for n in _WEIGHTS if _SHARD_AXIS[n] is None]
_COL_SHARDED = ["ev_w_in", "od_w_glu", "ffn_w_up", "ple_w_proj"]
_ROW_SHARDED = ["ev_w_out", "ffn_w_down", "ple_w_gate"]
_SMALL_SHARDED = ["od_d", "ffn_conv_w"]


def _pack(arrays, lead=()):
    nl = len(lead)
    flat = jnp.concatenate([a.reshape(lead + (-1,)) for a in arrays], axis=nl)
    n = flat.shape[nl]
    chunk = _FLAT_COLS * _FLAT_ROW_ALIGN
    total = -(-n // chunk) * chunk
    flat = jnp.pad(flat, [(0, 0)] * nl + [(0, total - n)])
    return flat.reshape(lead + (total // _FLAT_COLS, _FLAT_COLS))


def _unpack(buf, shapes, lead=()):
    nl = len(lead)
    flat = buf.reshape(lead + (-1,))
    out, off = [], 0
    for shp in shapes:
        n = math.prod(shp)
        out.append(lax.slice_in_dim(flat, off, off + n, axis=nl).reshape(lead + tuple(shp)))
        off += n
    return out


def _join_shards(g, axis):
    g = jnp.moveaxis(g, 0, axis)
    shp = g.shape
    return g.reshape(shp[:axis] + (shp[axis] * shp[axis + 1],) + shp[axis + 2:])


def _split_shards(full, axis):
    shp = full.shape
    g = full.reshape(shp[:axis] + (N_DEV, shp[axis] // N_DEV) + shp[axis + 1:])
    return jnp.moveaxis(g, axis, 0)


def _local_step(x, p, tgt, w):
    saved = []
    h = x
    for i in range(DEPTH):
        x0 = h
        if i % 2 == 0:
            x1, sv_mix = _even_fwd(x0, w, i // 2, i)
        else:
            x1, sv_mix = _odd_fwd(x0, w, i // 2, i)
        x2, sv_ffn = _ffn_fwd(x1, w, i)
        x3, sv_ple = _ple_fwd(x2, p[i], w, i)
        saved.append((x0, x1, x2, sv_mix, sv_ffn, sv_ple))
        h = x3
    loss, dh = _loss_and_grad(h, tgt)
    per_layer = {}
    for i in reversed(range(DEPTH)):
        x0, x1, x2, sv_mix, sv_ffn, sv_ple = saved[i]
        dx2, dx2b, g_ple = _ple_bwd(dh, x2, p[i], sv_ple, w, i)
        dx1, dx1b, g_ffn = _ffn_bwd(dx2, dx2b, x1, sv_ffn, w, i)
        if i % 2 == 0:
            dh, g_mix = _even_bwd(dx1, dx1b, x0, sv_mix, w, i // 2, i)
        else:
            dh, g_mix = _odd_bwd(dx1, x0, sv_mix, w, i // 2, i)
        for name, g in {**g_ple, **g_ffn, **g_mix}.items():
            per_layer.setdefault(name, []).append(g)
    grads = {name: gs[::-1] for name, gs in per_layer.items()}
    return loss, dh, grads


def kernel(x, p, norm_mix, norm_ffn, ev_w_in, ev_b_fgate, ev_q_norm, ev_k_norm, ev_v_norm, ev_w_spatial, ev_b_spatial, ev_w_out, od_a_re, od_a_im, od_log_dt, od_b_re, od_b_im, od_c_re, od_c_im, od_d, od_w_glu, ffn_w_up, ffn_conv_w, ffn_conv_b, ffn_w_down, ple_w_proj, ple_w_gate, loss_target, m_norm_mix, m_norm_ffn, m_ev_w_in, m_ev_b_fgate, m_ev_q_norm, m_ev_k_norm, m_ev_v_norm, m_ev_w_spatial, m_ev_b_spatial, m_ev_w_out, m_od_a_re, m_od_a_im, m_od_log_dt, m_od_b_re, m_od_b_im, m_od_c_re, m_od_c_im, m_od_d, m_od_w_glu, m_ffn_w_up, m_ffn_conv_w, m_ffn_conv_b, m_ffn_w_down, m_ple_w_proj, m_ple_w_gate, v_norm_mix, v_norm_ffn, v_ev_w_in, v_ev_b_fgate, v_ev_q_norm, v_ev_k_norm, v_ev_v_norm, v_ev_w_spatial, v_ev_b_spatial, v_ev_w_out, v_od_a_re, v_od_a_im, v_od_log_dt, v_od_b_re, v_od_b_im, v_od_c_re, v_od_c_im, v_od_d, v_od_w_glu, v_ffn_w_up, v_ffn_conv_w, v_ffn_conv_b, v_ffn_w_down, v_ple_w_proj, v_ple_w_gate):
    args = locals()
    wl = {n: args[n] for n in _WEIGHTS}
    ml = {n: args["m_" + n] for n in _WEIGHTS}
    vl = {n: args["v_" + n] for n in _WEIGHTS}

    kinds = ("g", "d", "m", "v")

    def local(n, layer, a):
        return a[n][layer].T if n in _COL_SHARDED else a[n][layer]

    w = {n: wl[n] for n in _REPLICATED}
    for n in _COL_SHARDED + _ROW_SHARDED:
        w[n] = []
        for layer in range(wl[n].shape[0]):
            g = _all_gather(local(n, layer, wl).astype(BF16), name="gather_" + n)
            full = g.reshape(N_DEV * g.shape[1], g.shape[2])
            if n == "ev_w_in":
                full = jnp.pad(full, ((0, IN_COLS_PAD - IN_COLS), (0, 0)))
            w[n].append(full)
    small = _all_gather(_pack([wl[n] for n in _SMALL_SHARDED]), name="gather_small_weights")
    for n, g in zip(_SMALL_SHARDED, _unpack(small, [wl[n].shape for n in _SMALL_SHARDED], lead=(N_DEV,))):
        w[n] = _join_shards(g, _SHARD_AXIS[n])

    loss, grad_x, grads = _local_step(x[0], p[:, 0], loss_target[0], w)
    loss = lax.psum(loss, ("x", "y", "c"))

    out = {kind: {} for kind in kinds}
    for n in _COL_SHARDED + _ROW_SHARDED:
        layers = []
        for layer, g in enumerate(grads[n]):
            parts = _all_to_all(g.reshape(N_DEV, g.shape[0] // N_DEV, g.shape[1]), name="scatter_" + n)
            res = _adamw(parts, local(n, layer, wl), local(n, layer, ml), local(n, layer, vl), name="adamw_" + n)
            layers.append([r.T if n in _COL_SHARDED else r for r in res])
        for k, kind in enumerate(kinds):
            out[kind][n] = jnp.stack([lay[k] for lay in layers])

    send = _pack([_split_shards(jnp.stack(grads[n]), _SHARD_AXIS[n]) for n in _SMALL_SHARDED], lead=(N_DEV,))
    parts = _all_to_all(send, name="scatter_small_grads")
    shapes = [wl[n].shape for n in _SMALL_SHARDED]
    res = _adamw(parts, _pack([wl[n] for n in _SMALL_SHARDED]), _pack([ml[n] for n in _SMALL_SHARDED]),
                 _pack([vl[n] for n in _SMALL_SHARDED]), name="adamw_small")
    for kind, buf in zip(kinds, res):
        out[kind].update(zip(_SMALL_SHARDED, _unpack(buf, shapes)))

    parts = _all_gather(_pack([jnp.stack(grads[n]) for n in _REPLICATED]), name="gather_replicated_grads")
    shapes = [wl[n].shape for n in _REPLICATED]
    res = _adamw(parts, _pack([wl[n] for n in _REPLICATED]), _pack([ml[n] for n in _REPLICATED]),
                 _pack([vl[n] for n in _REPLICATED]), name="adamw_replicated")
    for kind, buf in zip(kinds, res):
        out[kind].update(zip(_REPLICATED, _unpack(buf, shapes)))

    return (loss, grad_x[None], *[out["g"][n] for n in _WEIGHTS], *[out["d"][n] for n in _WEIGHTS],
            *[out["m"][n] for n in _WEIGHTS], *[out["v"][n] for n in _WEIGHTS])
```

```python
import functools
import math

import jax
import jax.numpy as jnp
from jax import lax
from jax.experimental import pallas as pl
from jax.experimental.pallas import tpu as pltpu

F32 = jnp.float32
BF16 = jnp.bfloat16

D_MODEL = 1024
DEPTH = 4
A_GROUPS = 4
A_CHUNK = 128
A_WIDTH = 512
B_HEADS = 8
B_HEAD_DIM = 64
B_WIDTH = 512
IN_COLS = 2 * A_WIDTH + 3 * B_WIDTH + B_HEADS
IN_COLS_PAD = 2688
S5_GROUP_CH = 16
S5_GROUPS = 64
S5_STATE = 64
S5_N = S5_GROUPS * S5_STATE
S5_BLOCKS = 8
D_FF = 2816
PLE_DIM = 256
EPS = 1e-6
NEG_INF = -1e30
N_DEV = 8

ADAM_LR = 0.001
ADAM_B1 = 0.9
ADAM_B2 = 0.999
ADAM_EPS = 1e-08
ADAM_WD = 0.01
ADAM_STEP = 10

VMEM_LIMIT_BYTES = 56 * 1024 * 1024
MESH = pl.DeviceIdType.MESH


def _params(sem, vmem=VMEM_LIMIT_BYTES):
    return pltpu.CompilerParams(dimension_semantics=sem, vmem_limit_bytes=vmem)


_GELU_K = math.sqrt(2.0 / math.pi)


def _gelu(x):
    return x * (0.5 * (1.0 + jnp.tanh(_GELU_K * (x + 0.044715 * (x * x * x)))))


def _gelu_grad(x):
    t = jnp.tanh(_GELU_K * (x + 0.044715 * (x * x * x)))
    return 0.5 * (1.0 + t) + 0.5 * x * (1.0 - t * t) * (_GELU_K * (1.0 + 3.0 * 0.044715 * (x * x)))


def _sigmoid(x):
    return 0.5 * jnp.tanh(0.5 * x) + 0.5


def _rstd(x):
    return lax.rsqrt(jnp.mean(x * x, axis=-1, keepdims=True) + EPS)


def _rms_bwd(x, dy, g):
    r = _rstd(x)
    xh = x * r
    dyg = dy if g is None else dy * g
    dx = r * (dyg - xh * jnp.mean(dyg * xh, axis=-1, keepdims=True))
    return dx, dy * xh


def _shift_down(blk, halo, k):
    tr = blk.shape[0]
    r = pltpu.roll(blk, k, 0)
    hr = pltpu.roll(halo, k, 0)
    first = jnp.where(lax.broadcasted_iota(jnp.int32, hr.shape, 0) < k, hr, r[0:8])
    return jnp.concatenate([first, r[8:tr]], axis=0)


def _shift_up(blk, halo, k):
    tr = blk.shape[0]
    r = pltpu.roll(blk, tr - k, 0)
    hr = pltpu.roll(halo, 8 - k, 0)
    last = jnp.where(lax.broadcasted_iota(jnp.int32, hr.shape, 0) >= 8 - k, hr, r[tr - 8:tr])
    return jnp.concatenate([r[0:tr - 8], last], axis=0)


def _rw(fn, rows, consts, outs, accs=(), *, tr, name, prev=(), nxt=(), widths=None):
    s = rows[0].shape[0]
    tr = min(tr, s)
    n = s // tr
    nr, nc, npv, nnx, no, na = len(rows), len(consts), len(prev), len(nxt), len(outs), len(accs)
    widths = widths or [None] * nr

    def body(*refs):
        ins, out_refs = refs[:nr + nc + npv + nnx], refs[nr + nc + npv + nnx:]
        i = pl.program_id(0)
        vals = [r[...] for r in ins[:nr + nc]]
        vals += [jnp.where(i == 0, 0.0, r[...]) for r in ins[nr + nc:nr + nc + npv]]
        vals += [jnp.where(i == n - 1, 0.0, r[...]) for r in ins[nr + nc + npv:]]
        res = fn(*vals)
        if not isinstance(res, (tuple, list)):
            res = (res,)
        for k in range(no):
            out_refs[k][...] = res[k].astype(out_refs[k].dtype)
        if na:
            @pl.when(i == 0)
            def _():
                for k in range(na):
                    out_refs[no + k][...] = jnp.zeros_like(out_refs[no + k])

            for k in range(na):
                out_refs[no + k][...] += res[no + k]

    in_specs = []
    for a, w in zip(rows, widths):
        if w is None:
            in_specs.append(pl.BlockSpec((tr, a.shape[1]), lambda i: (i, 0)))
        else:
            in_specs.append(pl.BlockSpec((tr, w[1]), functools.partial(lambda i, cb: (i, cb), cb=w[0])))
    for c in consts:
        in_specs.append(pl.BlockSpec(c.shape, functools.partial(lambda i, nd: (0,) * nd, nd=c.ndim)))
    t8 = tr // 8
    for a in prev:
        in_specs.append(pl.BlockSpec((8, a.shape[1]), lambda i: (jnp.maximum(i * t8 - 1, 0), 0)))
    for a in nxt:
        in_specs.append(pl.BlockSpec((8, a.shape[1]), lambda i: (jnp.minimum((i + 1) * t8, s // 8 - 1), 0)))
    out_shape = [jax.ShapeDtypeStruct((s, w), dt) for w, dt in outs]
    out_specs = [pl.BlockSpec((tr, w), lambda i: (i, 0)) for w, _ in outs]
    out_shape += [jax.ShapeDtypeStruct(a, F32) for a in accs]
    out_specs += [pl.BlockSpec(a, lambda i: (0, 0)) for a in accs]
    res = pl.pallas_call(
        body, grid=(n,), in_specs=in_specs, out_specs=out_specs, out_shape=out_shape, name=name,
        compiler_params=_params(("arbitrary",) if na else ("parallel",)),
    )(*rows, *consts, *prev, *nxt)
    return res


def _pick(n, cap):
    if n <= cap:
        return n
    best = None
    for d in range(128, cap + 1, 128):
        if n % d == 0:
            best = d
    assert best is not None, (n, cap)
    return best


def _mm(a, b, *, ta=False, tb=False, out_dtype=F32, add=None, epilogue=None, name, bm=1024, bn=512, bk=None):
    (k_dim, m) = a.shape if ta else a.shape[::-1]
    n = b.shape[0] if tb else b.shape[1]
    assert (b.shape[1] if tb else b.shape[0]) == k_dim
    if bk is None:
        bk = 1024 if ta else 2816
    bm, bn, bk = _pick(m, bm), _pick(n, bn), _pick(k_dim, bk)
    nk = k_dim // bk
    dims = (((0 if ta else 1,), (1 if tb else 0,)), ((), ()))
    assert add is None or epilogue is None
    epi_fn, extra, out_dtypes = epilogue or (None, [add] if add is not None else [], (out_dtype,))
    n_extra, n_out = len(extra), len(out_dtypes)

    def body(*refs):
        a_ref, b_ref = refs[0], refs[1]
        extra_refs = refs[2:2 + n_extra]
        out_refs = refs[2 + n_extra:2 + n_extra + n_out]
        p = lax.dot_general(a_ref[...].astype(BF16), b_ref[...].astype(BF16), dims, preferred_element_type=F32)

        def finish(acc):
            if epi_fn is not None:
                res = epi_fn(acc, *[r[...] for r in extra_refs])
            else:
                res = (acc + extra_refs[0][...],) if n_extra else (acc,)
            for o_ref, r in zip(out_refs, res):
                o_ref[...] = r.astype(o_ref.dtype)

        if nk == 1:
            finish(p)
        else:
            acc_ref = refs[2 + n_extra + n_out]
            kk = pl.program_id(2)

            @pl.when(kk == 0)
            def _():
                acc_ref[...] = p

            @pl.when(kk > 0)
            def _():
                acc_ref[...] += p

            @pl.when(kk == nk - 1)
            def _():
                finish(acc_ref[...])

    a_spec = pl.BlockSpec((bk, bm), lambda i, j, k: (k, i)) if ta else pl.BlockSpec((bm, bk), lambda i, j, k: (i, k))
    b_spec = pl.BlockSpec((bn, bk), lambda i, j, k: (j, k)) if tb else pl.BlockSpec((bk, bn), lambda i, j, k: (k, j))
    tile = pl.BlockSpec((bm, bn), lambda i, j, k: (i, j))
    res = pl.pallas_call(
        body, grid=(m // bm, n // bn, nk), in_specs=[a_spec, b_spec] + [tile] * n_extra,
        out_specs=[tile] * n_out, out_shape=[jax.ShapeDtypeStruct((m, n), dt) for dt in out_dtypes],
        scratch_shapes=[pltpu.VMEM((bm, bn), F32)] if nk > 1 else [],
        name=name, compiler_params=_params(("parallel", "parallel", "arbitrary")),
    )(a, b, *extra)
    return res if epilogue is not None else res[0]


def _rms_fwd(x, g, *, name):
    if g is None:
        return _rw(lambda xv: xv * _rstd(xv), [x], [], [(x.shape[1], BF16)], tr=512, name=name)[0]
    return _rw(lambda xv, gv: xv * _rstd(xv) * gv, [x], [g], [(x.shape[1], BF16)], tr=512, name=name)[0]


def _rms_bwd_call(x, dy, g, dx_add, *, name, tr=512, scale=None, bf16_copy=False):
    w = x.shape[1]
    has_g, has_add = g is not None, dx_add is not None

    def fn(*v):
        xv, dyv = v[0], v[1]
        if scale is not None:
            dyv = dyv * scale
        gv = v[2 + has_add] if has_g else None
        dx, dg = _rms_bwd(xv, dyv, gv)
        if has_add:
            dx = dx + v[2]
        return (dx,) + ((dx,) if bf16_copy else ()) + ((jnp.sum(dg, axis=0, keepdims=True),) if has_g else ())

    rows = [x, dy] + ([dx_add] if has_add else [])
    outs = [(w, F32)] + ([(w, BF16)] if bf16_copy else [])
    res = _rw(fn, rows, [g] if has_g else [], outs, [(1, w)] if has_g else [], tr=tr, name=name)
    return tuple(res) if has_g else tuple(res) + (None,)


def _conv_taps(hup, halo, cw):
    h1 = _shift_down(hup, halo, 1)
    h2 = _shift_down(hup, halo, 2)
    return h1, h2


def _ffn_fwd(x1, w, i):
    h2b = _rms_fwd(x1, w["norm_ffn"][i:i + 1], name="ffn_norm")
    hup = _mm(h2b, w["ffn_w_up"][i], tb=True, name="ffn_up", bn=1408)
    cw, cb = w["ffn_conv_w"][i], w["ffn_conv_b"][i:i + 1]

    def fn(hv, cwv, cbv, halo):
        h1, h2 = _conv_taps(hv, halo, cwv)
        hc = cbv + cwv[0:1] * h2
        hc = hc + cwv[1:2] * h1
        hc = hc + cwv[2:3] * hv
        g, up = hc[:, :D_FF], hc[:, D_FF:]
        return (g * _sigmoid(g)) * up

    a = _rw(fn, [hup], [cw, cb], [(D_FF, BF16)], tr=128, name="ffn_conv_gate", prev=[hup])[0]
    x2 = _mm(a, w["ffn_w_down"][i], add=x1, name="ffn_down")
    return x2, (h2b, hup, a)


def _ffn_bwd(dx2, dx2b, x1, saved, w, i):
    h2b, hup, a = saved
    cw, cb = w["ffn_conv_w"][i], w["ffn_conv_b"][i:i + 1]
    da = _mm(dx2b, w["ffn_w_down"][i], tb=True, name="ffn_down_dx", bn=1408)
    d_wdown = _mm(a, dx2b, ta=True, name="ffn_down_dw", bm=1408)

    def fn1(hv, dav, cwv, cbv, h_prev, h_next, da_next):
        tr = hv.shape[0]
        he = jnp.concatenate([hv, h_next], axis=0)
        dae = jnp.concatenate([dav, da_next], axis=0)
        h1, h2 = _conv_taps(he, h_prev, cwv)
        hc = cbv + cwv[0:1] * h2
        hc = hc + cwv[1:2] * h1
        hc = hc + cwv[2:3] * he
        g, up = hc[:, :D_FF], hc[:, D_FF:]
        sg = _sigmoid(g)
        d_up = dae * (g * sg)
        d_g = dae * up * (sg * (1.0 + g * (1.0 - sg)))
        d_hce = jnp.concatenate([d_g, d_up], axis=1)
        rows_e = tr + 8
        d_hup = (cwv[2:3] * d_hce[:tr] + cwv[1:2] * pltpu.roll(d_hce, rows_e - 1, 0)[:tr]
                 + cwv[0:1] * pltpu.roll(d_hce, rows_e - 2, 0)[:tr])
        d_hc = d_hce[:tr]
        col = lambda v: jnp.sum(v, axis=0, keepdims=True)
        return d_hup, col(d_hc), col(d_hc * h2[:tr]), col(d_hc * h1[:tr]), col(d_hc * hv)

    w2 = 2 * D_FF
    d_hup, d_cb, d_cw0, d_cw1, d_cw2 = _rw(fn1, [hup, da], [cw, cb], [(w2, BF16)], [(1, w2)] * 4, tr=128,
                                           name="ffn_conv_gate_bwd", prev=[hup], nxt=[hup, da])
    d_h2 = _mm(d_hup, w["ffn_w_up"][i], name="ffn_up_dx")
    d_wup_t = _mm(h2b, d_hup, ta=True, name="ffn_up_dw", bn=1408).T
    dx1, dx1b, d_norm = _rms_bwd_call(x1, d_h2, w["norm_ffn"][i:i + 1], dx2, name="ffn_norm_bwd", bf16_copy=True)
    grads = {"ffn_w_up": d_wup_t, "ffn_w_down": d_wdown, "ffn_conv_b": d_cb[0],
             "ffn_conv_w": jnp.concatenate([d_cw0, d_cw1, d_cw2], axis=0), "norm_ffn": d_norm[0]}
    return dx1, dx1b, grads


def _ple_fwd(x2, p_i, w, i):
    rn = _rms_fwd(x2, None, name="ple_norm")
    gl = _mm(rn, w["ple_w_gate"][i], name="ple_gate")
    x3 = _mm(p_i, w["ple_w_proj"][i], tb=True, name="ple_proj",
             epilogue=(lambda pp, xv, g: (xv + _sigmoid(g) * pp,), [x2, gl], (F32,)))[0]
    return x3, (rn, gl)


def _ple_bwd(dx3, x2, p_i, saved, w, i):
    rn, gl = saved

    def fn(pp, dv, g):
        sg = _sigmoid(g)
        return dv * sg, dv * pp * (sg * (1.0 - sg))

    d_pp, d_pre = _mm(p_i, w["ple_w_proj"][i], tb=True, name="ple_proj_bwd", epilogue=(fn, [dx3, gl], (BF16, BF16)))
    d_wproj_t = _mm(p_i, d_pp, ta=True, name="ple_proj_dw", bn=1024).T
    d_wgate = _mm(rn, d_pre, ta=True, name="ple_gate_dw")
    d_rn = _mm(d_pre, w["ple_w_gate"][i], tb=True, name="ple_gate_dx")
    dx2, dx2b, _ = _rms_bwd_call(x2, d_rn, None, dx3, name="ple_norm_bwd", bf16_copy=True)
    return dx2, dx2b, {"ple_w_proj": d_wproj_t, "ple_w_gate": d_wgate}


def _loss_and_grad(xf, tgt):
    def fn(xv, tv):
        d = xv - tv
        part = 0.5 * jnp.sum(jnp.mean(d * d, axis=-1, keepdims=True), axis=0, keepdims=True)
        return d * (1.0 / D_MODEL), part

    dx, loss = _rw(fn, [xf, tgt], [], [(D_MODEL, F32)], [(1, 1)], tr=512, name="loss_head")
    return loss[0, 0], dx


def _tril_mask():
    shp = (A_CHUNK, A_CHUNK)
    return lax.broadcasted_iota(jnp.int32, shp, 0) >= lax.broadcasted_iota(jnp.int32, shp, 1)


def _gmlp_fwd(z, w_s, b_s, v_gain, *, tr=512):
    s = z.shape[0]
    tr = min(tr, s)
    gw = A_WIDTH // A_GROUPS

    def body(z_ref, w_ref, b_ref, g_ref, y_ref):
        tril = _tril_mask()
        for g in range(A_GROUPS):
            wg = jnp.where(tril, w_ref[g], 0.0).astype(BF16)
            cols = slice(g * gw, (g + 1) * gw)
            vcols = slice(A_WIDTH + g * gw, A_WIDTH + (g + 1) * gw)
            for c in range(tr // A_CHUNK):
                rows = slice(c * A_CHUNK, (c + 1) * A_CHUNK)
                v = _gelu(z_ref[rows, vcols])
                vn = v * _rstd(v) * g_ref[:, cols]
                sv = jnp.dot(wg, vn.astype(BF16), preferred_element_type=F32) + b_ref[g]
                y_ref[rows, cols] = (_gelu(z_ref[rows, cols]) * sv).astype(BF16)

    return pl.pallas_call(
        body, grid=(s // tr,),
        in_specs=[pl.BlockSpec((tr, 2 * A_WIDTH), lambda i: (i, 0)),
                  pl.BlockSpec(w_s.shape, lambda i: (0, 0, 0)), pl.BlockSpec(b_s.shape, lambda i: (0, 0, 0)),
                  pl.BlockSpec(v_gain.shape, lambda i: (0, 0))],
        out_specs=pl.BlockSpec((tr, A_WIDTH), lambda i: (i, 0)),
        out_shape=jax.ShapeDtypeStruct((s, A_WIDTH + B_WIDTH), BF16), name="gmlp_fwd",
        compiler_params=_params(("parallel",)),
    )(z, w_s, b_s, v_gain)


def _gmlp_bwd(z, d_ymix, w_s, b_s, v_gain, dz, *, tr=512):
    s = z.shape[0]
    tr = min(tr, s)
    gw = A_WIDTH // A_GROUPS

    def body(z_ref, dy_ref, w_ref, b_ref, g_ref, _, dz_ref, dw_ref, db_ref, dg_ref):
        @pl.when(pl.program_id(0) == 0)
        def _():
            dw_ref[...] = jnp.zeros_like(dw_ref)
            db_ref[...] = jnp.zeros_like(db_ref)
            dg_ref[...] = jnp.zeros_like(dg_ref)

        tril = _tril_mask()
        for g in range(A_GROUPS):
            wg = jnp.where(tril, w_ref[g], 0.0).astype(BF16)
            cols = slice(g * gw, (g + 1) * gw)
            vcols = slice(A_WIDTH + g * gw, A_WIDTH + (g + 1) * gw)
            gain = g_ref[:, cols]
            for c in range(tr // A_CHUNK):
                rows = slice(c * A_CHUNK, (c + 1) * A_CHUNK)
                va, ua = z_ref[rows, vcols], z_ref[rows, cols]
                v = _gelu(va)
                r = _rstd(v)
                vh = v * r
                vnb = (vh * gain).astype(BF16)
                sv = jnp.dot(wg, vnb, preferred_element_type=F32) + b_ref[g]
                dy = dy_ref[rows, cols]
                d_sv = dy * _gelu(ua)
                dz_ref[rows, cols] = (dy * sv * _gelu_grad(ua)).astype(BF16)
                d_svb = d_sv.astype(BF16)
                d_vn = lax.dot_general(wg, d_svb, (((0,), (0,)), ((), ())), preferred_element_type=F32)
                dwp = lax.dot_general(d_svb, vnb, (((1,), (1,)), ((), ())), preferred_element_type=F32)
                dw_ref[g] += jnp.where(tril, dwp, 0.0)
                db_ref[g] += jnp.sum(d_sv, axis=1, keepdims=True)
                dg_ref[:, cols] += jnp.sum(d_vn * vh, axis=0, keepdims=True)
                d_vh = d_vn * gain
                d_v = r * (d_vh - vh * jnp.mean(d_vh * vh, axis=-1, keepdims=True))
                dz_ref[rows, vcols] = (d_v * _gelu_grad(va)).astype(BF16)

    return pl.pallas_call(
        body, grid=(s // tr,),
        in_specs=[pl.BlockSpec((tr, 2 * A_WIDTH), lambda i: (i, 0)), pl.BlockSpec((tr, A_WIDTH), lambda i: (i, 0)),
                  pl.BlockSpec(w_s.shape, lambda i: (0, 0, 0)), pl.BlockSpec(b_s.shape, lambda i: (0, 0, 0)),
                  pl.BlockSpec(v_gain.shape, lambda i: (0, 0)), _ANY],
        out_specs=[pl.BlockSpec((tr, 2 * A_WIDTH), lambda i: (i, 0)), pl.BlockSpec(w_s.shape, lambda i: (0, 0, 0)),
                   pl.BlockSpec(b_s.shape, lambda i: (0, 0, 0)), pl.BlockSpec(v_gain.shape, lambda i: (0, 0))],
        out_shape=[jax.ShapeDtypeStruct(dz.shape, dz.dtype), jax.ShapeDtypeStruct(w_s.shape, F32),
                   jax.ShapeDtypeStruct(b_s.shape, F32), jax.ShapeDtypeStruct(v_gain.shape, F32)],
        input_output_aliases={5: 0}, name="gmlp_bwd", compiler_params=_params(("arbitrary",)),
    )(z, d_ymix, w_s, b_s, v_gain, dz)


def _dot3(x, ub):
    x1 = x.astype(BF16)
    r1 = x - x1.astype(F32)
    x2 = r1.astype(BF16)
    x3 = (r1 - x2.astype(F32)).astype(BF16)
    d = lambda a: jnp.dot(a, ub, preferred_element_type=F32)
    return d(x1) + d(x2) + d(x3)


def _log_sigmoid(x):
    return jnp.minimum(x, 0.0) - jnp.log(1.0 + jnp.exp(-jnp.abs(x)))


def _fgate_fwd(f_t, b_col, *, tb=256):
    h, s = f_t.shape
    tb = min(tb, s)

    def body(f_ref, b_ref, c_ref, carry):
        @pl.when(pl.program_id(0) == 0)
        def _():
            carry[...] = jnp.zeros_like(carry)

        lf = _log_sigmoid(f_ref[...] + b_ref[...])
        shp = (tb, tb)
        upper = (lax.broadcasted_iota(jnp.int32, shp, 0) <= lax.broadcasted_iota(jnp.int32, shp, 1)).astype(BF16)
        c_ref[...] = _dot3(lf, upper) + carry[...]
        carry[...] += jnp.sum(lf, axis=1, keepdims=True)

    return pl.pallas_call(
        body, grid=(s // tb,),
        in_specs=[pl.BlockSpec((h, tb), lambda i: (0, i)), pl.BlockSpec((h, 1), lambda i: (0, 0))],
        out_specs=pl.BlockSpec((h, tb), lambda i: (0, i)), out_shape=jax.ShapeDtypeStruct((h, s), F32),
        scratch_shapes=[pltpu.VMEM((h, 1), F32)], name="fgate_fwd", compiler_params=_params(("arbitrary",)),
    )(f_t, b_col)


def _fgate_bwd(f_t, b_col, dc_a, dc_b, *, tb=256):
    h, s = f_t.shape
    tb = min(tb, s)
    n = s // tb

    def body(f_ref, b_ref, da_ref, db_ref, df_ref, dbias_ref, carry):
        @pl.when(pl.program_id(0) == 0)
        def _():
            carry[...] = jnp.zeros_like(carry)
            dbias_ref[...] = jnp.zeros_like(dbias_ref)

        dc = da_ref[...] + db_ref[...]
        shp = (tb, tb)
        lower = (lax.broadcasted_iota(jnp.int32, shp, 0) >= lax.broadcasted_iota(jnp.int32, shp, 1)).astype(BF16)
        d_lf = _dot3(dc, lower) + carry[...]
        carry[...] += jnp.sum(dc, axis=1, keepdims=True)
        df = d_lf * (1.0 - _sigmoid(f_ref[...] + b_ref[...]))
        df_ref[...] = df
        dbias_ref[...] += jnp.sum(df, axis=1, keepdims=True)

    blk = pl.BlockSpec((h, tb), lambda i: (0, n - 1 - i))
    return pl.pallas_call(
        body, grid=(n,), in_specs=[blk, pl.BlockSpec((h, 1), lambda i: (0, 0)), blk, blk],
        out_specs=[blk, pl.BlockSpec((h, 1), lambda i: (0, 0))],
        out_shape=[jax.ShapeDtypeStruct((h, s), F32), jax.ShapeDtypeStruct((h, 1), F32)],
        scratch_shapes=[pltpu.VMEM((h, 1), F32)], name="fgate_bwd", compiler_params=_params(("arbitrary",)),
    )(f_t, b_col, dc_a, dc_b)


_NT = (((1,), (1,)), ((), ()))


def _causal(shape, row0, col0, transposed):
    r = lax.broadcasted_iota(jnp.int32, shape, 0) + row0
    c = lax.broadcasted_iota(jnp.int32, shape, 1) + col0
    return (r <= c) if transposed else (c <= r)


_ATT_SCALE = B_HEAD_DIM ** -0.5
ATT_W = 128
_COL_CQ, _COL_CK, _COL_LSE, _COL_DELTA = 64, 67, 70, 64
_ATT_BLK = 1024
_ATT_SUB = 512


def _split3(x):
    h = x.astype(BF16).astype(F32)
    r = x - h
    m = r.astype(BF16).astype(F32)
    return h, m, (r - m).astype(BF16).astype(F32)


def _put_cols(base, lane, col0, parts):
    for t, pv in enumerate(parts):
        base = jnp.where(lane == col0 + t, pv, base)
    return base


_HEAD_PAIRS = B_HEADS // 2
_Q_BLOCK, _K_BLOCK, _V_BLOCK = 8, 12, 16
_F_BLOCK = 20
_O_BLOCK = 4
_ANY = pl.BlockSpec(memory_space=pl.ANY)


def _head_half(ref, h):
    x = ref[...]
    x = jnp.where((h & 1) == 1, pltpu.roll(x, B_HEAD_DIM, 1), x)
    return jnp.where(lax.broadcasted_iota(jnp.int32, x.shape, 1) < B_HEAD_DIM, x, 0.0)


def _pair(a, b):
    lane = lax.broadcasted_iota(jnp.int32, a.shape, 1)
    return jnp.where(lane < B_HEAD_DIM, a, pltpu.roll(b, B_HEAD_DIM, 1))


def _attn_prep(z, c_col, gq, gk, *, tr=1024):
    s = z.shape[0]
    tr = min(tr, s)
    n = s // tr

    def body(q_ref, k_ref, v_ref, c_ref, gq_ref, gk_ref, qo_ref, ko_ref, vo_ref):
        h = pl.program_id(0)
        q, k, v = _head_half(q_ref, h), _head_half(k_ref, h), _head_half(v_ref, h)
        lane = lax.broadcasted_iota(jnp.int32, q.shape, 1)
        rq = lax.rsqrt(jnp.sum(q * q, axis=-1, keepdims=True) * (1.0 / B_HEAD_DIM) + EPS)
        rk = lax.rsqrt(jnp.sum(k * k, axis=-1, keepdims=True) * (1.0 / B_HEAD_DIM) + EPS)
        c = c_ref[...]
        ch, cm, cl = _split3(c)
        one = jnp.ones_like(c)
        qq = _put_cols(q * rq * gq_ref[...] * _ATT_SCALE, lane, _COL_CQ, (ch, cm, cl))
        qq = _put_cols(qq, lane, _COL_CK, (one, one, one))
        kk = _put_cols(k * rk * gk_ref[...], lane, _COL_CQ, (one, one, one))
        kk = _put_cols(kk, lane, _COL_CK, (-ch, -cm, -cl))
        kk = _put_cols(kk, lane, _COL_LSE, (one, one, one))
        qo_ref[...] = qq.astype(BF16)
        ko_ref[...] = kk.astype(BF16)
        vo_ref[...] = _put_cols(v, lane, _COL_DELTA, (one, one, one)).astype(BF16)

    def zcol(base):
        return pl.BlockSpec((tr, ATT_W), lambda h, i: (i, base + (h >> 1)))

    rows = pl.BlockSpec((tr, ATT_W), lambda h, i: (h * n + i, 0))
    gain = pl.BlockSpec((1, ATT_W), lambda h, i: (0, 0))
    return pl.pallas_call(
        body, grid=(B_HEADS, n),
        in_specs=[zcol(_Q_BLOCK), zcol(_K_BLOCK), zcol(_V_BLOCK), pl.BlockSpec((tr, 1), lambda h, i: (h * n + i, 0)),
                  gain, gain],
        out_specs=[rows] * 3, out_shape=[jax.ShapeDtypeStruct((B_HEADS * s, ATT_W), BF16)] * 3, name="attn_prep",
        compiler_params=_params(("parallel", "parallel")),
    )(z, z, z, c_col, gq, gk)


def _attn_bwd_prep(d_ymix, o, qp, lse, *, tr=1024):
    s = d_ymix.shape[0]
    tr = min(tr, s)
    n = s // tr

    def body(do_ref, o_ref, q_ref, l_ref, dop_ref, qpp_ref):
        dov = _head_half(do_ref, pl.program_id(0))
        lane = lax.broadcasted_iota(jnp.int32, dov.shape, 1)
        dh, dm, dl = _split3(jnp.sum(dov * o_ref[...], axis=-1, keepdims=True))
        lh, lm, ll = _split3(l_ref[...])
        dop_ref[...] = _put_cols(dov, lane, _COL_DELTA, (-dh, -dm, -dl)).astype(BF16)
        qpp_ref[...] = _put_cols(q_ref[...].astype(F32), lane, _COL_LSE, (-lh, -lm, -ll)).astype(BF16)

    rows = pl.BlockSpec((tr, ATT_W), lambda h, i: (h * n + i, 0))
    return pl.pallas_call(
        body, grid=(B_HEADS, n),
        in_specs=[pl.BlockSpec((tr, ATT_W), lambda h, i: (i, _O_BLOCK + (h >> 1))), rows, rows,
                  pl.BlockSpec((tr, 1), lambda h, i: (h * n + i, 0))],
        out_specs=[rows] * 2, out_shape=[jax.ShapeDtypeStruct((B_HEADS * s, ATT_W), BF16)] * 2,
        name="attn_bwd_prep", compiler_params=_params(("parallel", "parallel")),
    )(d_ymix, o, qp, lse)


def _attn_merge_out(o, ymix, *, tr=1024):
    s = o.shape[1]
    tr = min(tr, s)

    def body(a_ref, b_ref, buf_ref, out_ref):
        out_ref[...] = _pair(a_ref[0], b_ref[0]).astype(BF16)

    return pl.pallas_call(
        body, grid=(_HEAD_PAIRS, s // tr),
        in_specs=[pl.BlockSpec((1, tr, ATT_W), lambda j, i: (2 * j, i, 0)),
                  pl.BlockSpec((1, tr, ATT_W), lambda j, i: (2 * j + 1, i, 0)), _ANY],
        out_specs=pl.BlockSpec((tr, ATT_W), lambda j, i: (i, _O_BLOCK + j)),
        out_shape=jax.ShapeDtypeStruct(ymix.shape, ymix.dtype), input_output_aliases={2: 0}, name="attn_merge_out",
        compiler_params=_params(("parallel", "parallel")),
    )(o, o, ymix)


def _pair_bwd(d_heads, dz, out_block, *, z=None, in_block=None, gain=None, scale=None, name, tr=1024):
    s = d_heads.shape[1]
    tr = min(tr, s)
    norm = gain is not None

    def body(*refs):
        a_ref, b_ref = refs[0], refs[1]
        d = _pair(a_ref[0], b_ref[0])
        if norm:
            x_ref, g_ref, _, out_ref, dg_ref = refs[2:]

            @pl.when((pl.program_id(0) == 0) & (pl.program_id(1) == 0))
            def _():
                dg_ref[...] = jnp.zeros_like(dg_ref)

            x = x_ref[...]
            lo = lax.broadcasted_iota(jnp.int32, x.shape, 1) < B_HEAD_DIM

            def half_mean(v):
                s_lo = jnp.sum(jnp.where(lo, v, 0.0), axis=-1, keepdims=True)
                s_hi = jnp.sum(jnp.where(lo, 0.0, v), axis=-1, keepdims=True)
                return jnp.where(lo, s_lo, s_hi) * (1.0 / B_HEAD_DIM)

            r = lax.rsqrt(half_mean(x * x) + EPS)
            xh = x * r
            dy = d * scale if scale is not None else d
            dyg = dy * g_ref[...]
            d = r * (dyg - xh * half_mean(dyg * xh))
            dg_ref[...] += jnp.sum(dy * xh, axis=0, keepdims=True)
        else:
            out_ref = refs[3]
        out_ref[...] = d.astype(BF16)

    heads = [pl.BlockSpec((1, tr, ATT_W), lambda j, i: (2 * j, i, 0)),
             pl.BlockSpec((1, tr, ATT_W), lambda j, i: (2 * j + 1, i, 0))]
    out_spec = pl.BlockSpec((tr, ATT_W), lambda j, i: (i, out_block + j))
    dz_shape = jax.ShapeDtypeStruct(dz.shape, dz.dtype)
    if norm:
        res = pl.pallas_call(
            body, grid=(_HEAD_PAIRS, s // tr),
            in_specs=heads + [pl.BlockSpec((tr, ATT_W), lambda j, i: (i, in_block + j)),
                              pl.BlockSpec((1, ATT_W), lambda j, i: (0, 0)), _ANY],
            out_specs=[out_spec, pl.BlockSpec((1, ATT_W), lambda j, i: (0, 0))],
            out_shape=[dz_shape, jax.ShapeDtypeStruct((1, ATT_W), F32)], input_output_aliases={4: 0}, name=name,
            compiler_params=_params(("arbitrary", "arbitrary")),
        )(d_heads, d_heads, z, gain, dz)
        return res[0], res[1]
    return pl.pallas_call(
        body, grid=(_HEAD_PAIRS, s // tr), in_specs=heads + [_ANY], out_specs=out_spec, out_shape=dz_shape,
        input_output_aliases={2: 0}, name=name, compiler_params=_params(("parallel", "parallel")),
    )(d_heads, d_heads, dz), None


def _attn_fwd(qp, kp, vp, *, blk=_ATT_BLK):
    h, s, d = qp.shape
    b = min(blk, s)
    n = s // b
    sub = min(_ATT_SUB, b)
    ns = b // sub

    def body(q_ref, k_ref, v_ref, o_ref, lse_ref):
        i = pl.program_id(1)
        qs = [q_ref[0, t * sub:(t + 1) * sub, :] for t in range(ns)]

        def tile(j, state, masked):
            off = pl.multiple_of(j * b, b)
            kk = k_ref[0, pl.ds(off, b), :]
            vv = v_ref[0, pl.ds(off, b), :]
            new = []
            for t in range(ns):
                m, l, acc = state[3 * t:3 * t + 3]
                nk = (t + 1) * sub if masked else b
                sc = lax.dot_general(qs[t], kk[:nk], _NT, preferred_element_type=F32)
                if masked:
                    sc = jnp.where(_causal((sub, nk), t * sub, 0, False), sc, NEG_INF)
                m_new = jnp.maximum(m, jnp.max(sc, axis=1, keepdims=True))
                alpha = jnp.exp(m - m_new)
                p = jnp.exp(sc - m_new)
                new += [m_new, alpha * l + jnp.sum(p, axis=1, keepdims=True),
                        alpha * acc + jnp.dot(p.astype(BF16), vv[:nk], preferred_element_type=F32)]
            return tuple(new)

        init = (jnp.full((sub, 1), NEG_INF, F32), jnp.zeros((sub, 1), F32), jnp.zeros((sub, d), F32)) * ns
        state = lax.fori_loop(0, i, lambda j, st: tile(j, st, False), init)
        state = tile(i, state, True)
        for t in range(ns):
            m, l, acc = state[3 * t:3 * t + 3]
            o_ref[0, t * sub:(t + 1) * sub, :] = acc / l
            lse_ref[0, t * sub:(t + 1) * sub, :] = m + jnp.log(l)

    blk_spec = pl.BlockSpec((1, b, d), lambda hh, i: (hh, i, 0))
    full = pl.BlockSpec((1, s, d), lambda hh, i: (hh, 0, 0))
    return pl.pallas_call(
        body, grid=(h, n), in_specs=[blk_spec, full, full],
        out_specs=[blk_spec, pl.BlockSpec((1, b, 1), lambda hh, i: (hh, i, 0))],
        out_shape=[jax.ShapeDtypeStruct((h, s, d), F32), jax.ShapeDtypeStruct((h, s, 1), F32)],
        name="attn_fwd", compiler_params=_params(("parallel", "arbitrary")),
    )(qp, kp, vp)


def _attn_bwd_dq(qpp, kp, vp, dop, *, blk=_ATT_BLK):
    h, s, d = qpp.shape
    b = min(blk, s)
    n = s // b
    sub = min(_ATT_SUB, b)
    ns = b // sub

    def body(q_ref, do_ref, k_ref, v_ref, dq_ref):
        i = pl.program_id(1)
        qs = [q_ref[0, t * sub:(t + 1) * sub, :] for t in range(ns)]
        dos = [do_ref[0, t * sub:(t + 1) * sub, :] for t in range(ns)]

        def tile(j, state, masked):
            off = pl.multiple_of(j * b, b)
            kk = k_ref[0, pl.ds(off, b), :]
            vv = v_ref[0, pl.ds(off, b), :]
            new = []
            for t in range(ns):
                nk = (t + 1) * sub if masked else b
                p = jnp.exp(lax.dot_general(qs[t], kk[:nk], _NT, preferred_element_type=F32))
                if masked:
                    p = jnp.where(_causal((sub, nk), t * sub, 0, False), p, 0.0)
                ds = p * lax.dot_general(dos[t], vv[:nk], _NT, preferred_element_type=F32)
                new.append(state[t] + jnp.dot(ds.astype(BF16), kk[:nk], preferred_element_type=F32))
            return tuple(new)

        state = lax.fori_loop(0, i, lambda j, st: tile(j, st, False), (jnp.zeros((sub, d), F32),) * ns)
        state = tile(i, state, True)
        for t in range(ns):
            dq_ref[0, t * sub:(t + 1) * sub, :] = state[t]

    blk_spec = pl.BlockSpec((1, b, d), lambda hh, i: (hh, i, 0))
    full = pl.BlockSpec((1, s, d), lambda hh, i: (hh, 0, 0))
    return pl.pallas_call(
        body, grid=(h, n), in_specs=[blk_spec, blk_spec, full, full], out_specs=blk_spec,
        out_shape=jax.ShapeDtypeStruct((h, s, d), F32), name="attn_bwd_dq",
        compiler_params=_params(("parallel", "arbitrary")),
    )(qpp, dop, kp, vp)


def _attn_bwd_dkv(qpp, kp, vp, dop, *, blk=_ATT_BLK):
    h, s, d = qpp.shape
    b = min(blk, s)
    n = s // b
    sub = min(_ATT_SUB, b)
    ns = b // sub

    def body(k_ref, v_ref, q_ref, do_ref, dk_ref, dv_ref):
        j = pl.program_id(1)
        ks = [k_ref[0, t * sub:(t + 1) * sub, :] for t in range(ns)]
        vs = [v_ref[0, t * sub:(t + 1) * sub, :] for t in range(ns)]

        def tile(i, state, masked):
            off = pl.multiple_of(i * b, b)
            qq = q_ref[0, pl.ds(off, b), :]
            dd = do_ref[0, pl.ds(off, b), :]
            new = []
            for t in range(ns):
                q0 = t * sub if masked else 0
                qt, dt = qq[q0:], dd[q0:]
                pt = jnp.exp(lax.dot_general(ks[t], qt, _NT, preferred_element_type=F32))
                if masked:
                    pt = jnp.where(_causal((sub, b - q0), q0, q0, True), pt, 0.0)
                dst = pt * lax.dot_general(vs[t], dt, _NT, preferred_element_type=F32)
                new += [state[2 * t] + jnp.dot(dst.astype(BF16), qt, preferred_element_type=F32),
                        state[2 * t + 1] + jnp.dot(pt.astype(BF16), dt, preferred_element_type=F32)]
            return tuple(new)

        state = tile(j, (jnp.zeros((sub, d), F32),) * (2 * ns), True)
        state = lax.fori_loop(j + 1, n, lambda i, st: tile(i, st, False), state)
        for t in range(ns):
            dk_ref[0, t * sub:(t + 1) * sub, :] = state[2 * t]
            dv_ref[0, t * sub:(t + 1) * sub, :] = state[2 * t + 1]

    blk_spec = pl.BlockSpec((1, b, d), lambda hh, j: (hh, j, 0))
    full = pl.BlockSpec((1, s, d), lambda hh, j: (hh, 0, 0))
    return pl.pallas_call(
        body, grid=(h, n), in_specs=[blk_spec, blk_spec, full, full], out_specs=[blk_spec, blk_spec],
        out_shape=[jax.ShapeDtypeStruct((h, s, d), F32)] * 2, name="attn_bwd_dkv",
        compiler_params=_params(("parallel", "arbitrary")),
    )(kp, vp, qpp, dop)


def _pad_head(g):
    return jnp.pad(g, ((0, 0), (0, ATT_W - B_HEAD_DIM)))


def _even_fwd(x0, w, e, i):
    s = x0.shape[0]
    hs = B_HEADS * s
    hb = _rms_fwd(x0, w["norm_mix"][i:i + 1], name="mix_norm")
    z = _mm(hb, w["ev_w_in"][e], tb=True, name="ev_in", bn=896)
    b_s = w["ev_b_spatial"][e][:, :, None]
    v_gain = w["ev_v_norm"][e:e + 1]
    ymix = _gmlp_fwd(z, w["ev_w_spatial"][e], b_s, v_gain)
    f_t = z[:, IN_COLS - B_HEADS:IN_COLS].T
    c = _fgate_fwd(f_t, w["ev_b_fgate"][e][:, None])
    qp, kp, vp = _attn_prep(z, c.reshape(hs, 1), _pad_head(w["ev_q_norm"][e:e + 1]),
                            _pad_head(w["ev_k_norm"][e:e + 1]))
    shp = (B_HEADS, s, ATT_W)
    o, lse = _attn_fwd(qp.reshape(shp), kp.reshape(shp), vp.reshape(shp), blk=2 * _ATT_BLK)
    ymix = _attn_merge_out(o, ymix)
    x1 = _mm(ymix, w["ev_w_out"][e], add=x0, name="ev_out")
    return x1, (hb, z, ymix, qp, kp, vp, f_t, o, lse)


def _even_bwd(dx1, dx1b, x0, saved, w, e, i):
    hb, z, ymix, qp, kp, vp, f_t, o, lse = saved
    s = x0.shape[0]
    hs = B_HEADS * s
    d_ymix = _mm(dx1b, w["ev_w_out"][e], tb=True, name="ev_out_dx")
    d_wout = _mm(ymix, dx1b, ta=True, name="ev_out_dw")
    dop, qpp = _attn_bwd_prep(d_ymix, o.reshape(hs, ATT_W), qp, lse.reshape(hs, 1))
    shp = (B_HEADS, s, ATT_W)
    qpp, dop, kp3, vp3 = qpp.reshape(shp), dop.reshape(shp), kp.reshape(shp), vp.reshape(shp)
    dqp = _attn_bwd_dq(qpp, kp3, vp3, dop)
    dkp, dvp = _attn_bwd_dkv(qpp, kp3, vp3, dop)
    d_ft, d_bf = _fgate_bwd(f_t, w["ev_b_fgate"][e][:, None], dqp[:, :, _COL_CQ], -dkp[:, :, _COL_CK])
    dz = jnp.pad(d_ft.T.astype(BF16), ((0, 0), (_F_BLOCK * ATT_W, IN_COLS_PAD - IN_COLS)))
    b_s = w["ev_b_spatial"][e][:, :, None]
    v_gain = w["ev_v_norm"][e:e + 1]
    dz, d_ws, d_bs, d_vg = _gmlp_bwd(z, d_ymix, w["ev_w_spatial"][e], b_s, v_gain, dz)
    twice = lambda g: jnp.concatenate([g, g], axis=1)
    dz, d_gq = _pair_bwd(dqp, dz, _Q_BLOCK, z=z, in_block=_Q_BLOCK, gain=twice(w["ev_q_norm"][e:e + 1]),
                         scale=_ATT_SCALE, name="attn_qnorm_bwd")
    dz, d_gk = _pair_bwd(dkp, dz, _K_BLOCK, z=z, in_block=_K_BLOCK, gain=twice(w["ev_k_norm"][e:e + 1]),
                         name="attn_knorm_bwd")
    dz, _ = _pair_bwd(dvp, dz, _V_BLOCK, name="attn_dv_out")
    d_h = _mm(dz, w["ev_w_in"][e], name="ev_in_dx")
    d_win_t = _mm(hb, dz, ta=True, name="ev_in_dw", bn=1344)[:, :IN_COLS].T
    dx0, d_nm = _rms_bwd_call(x0, d_h, w["norm_mix"][i:i + 1], dx1, name="mix_norm_bwd")
    hd = B_HEAD_DIM
    grads = {"ev_w_in": d_win_t, "ev_w_out": d_wout, "ev_b_fgate": d_bf[:, 0],
             "ev_q_norm": d_gq[0, :hd] + d_gq[0, hd:], "ev_k_norm": d_gk[0, :hd] + d_gk[0, hd:], "ev_v_norm": d_vg[0],
             "ev_w_spatial": d_ws, "ev_b_spatial": d_bs[:, :, 0], "norm_mix": d_nm[0]}
    return dx0, grads


def _s5_disc(a_re, a_im, log_dt, b_re, b_im):
    dt = jnp.exp(log_dt)[:, None]
    lr, li = a_re, a_im
    mag = jnp.exp(lr * dt)
    ab_re, ab_im = mag * jnp.cos(li * dt), mag * jnp.sin(li * dt)
    den = lr * lr + li * li
    nr, ni = ab_re - 1.0, ab_im
    cr = (nr * lr + ni * li) / den
    ci = (ni * lr - nr * li) / den
    bb_re = cr[..., None] * b_re - ci[..., None] * b_im
    bb_im = cr[..., None] * b_im + ci[..., None] * b_re
    return ab_re, ab_im, bb_re, bb_im


_GPB = S5_GROUPS // S5_BLOCKS


def _blockdiag(t):
    a, b = t.shape[1:]
    eye = jnp.eye(_GPB, dtype=t.dtype)
    t = t.reshape(S5_BLOCKS, _GPB, a, 1, b) * eye[None, :, None, :, None]
    return t.reshape(S5_BLOCKS, _GPB * a, _GPB * b)


def _blockdiag_extract(m, a, b):
    eye = jnp.eye(_GPB, dtype=m.dtype)
    m = m.reshape(S5_BLOCKS, _GPB, a, _GPB, b) * eye[None, :, None, :, None]
    return jnp.sum(m, axis=3).reshape(S5_GROUPS, a, b)


_SCAN_ROWS = 8


def _cmul(a_r, a_i, b_r, b_i):
    return a_r * b_r - a_i * b_i, a_r * b_i + a_i * b_r


def _scan_tiles(yr_ref, yi_ref, a_r, a_i, c_r, c_i, *, rev, x_refs=None):
    tb, wl = yr_ref.shape
    ntile = tb // _SCAN_ROWS
    with_acc = x_refs is not None
    if rev:
        a_i = -a_i
    pw = [(a_r, a_i)]
    for _ in range(_SCAN_ROWS - 1):
        pw.append(_cmul(*pw[-1], a_r, a_i))
    sub = lax.broadcasted_iota(jnp.int32, (_SCAN_ROWS, wl), 0)
    dist = (_SCAN_ROWS - 1 - sub) if rev else sub
    zero = jnp.zeros((_SCAN_ROWS, wl), F32)
    steps = []
    for kk in (1, 2, 4):
        steps.append(((_SCAN_ROWS - kk) if rev else kk, jnp.where(dist >= kk, pw[kk - 1][0], zero),
                      jnp.where(dist >= kk, pw[kk - 1][1], zero)))
    e_r, e_i = zero, zero
    for d in range(_SCAN_ROWS):
        e_r = jnp.where(dist == d, pw[d][0], e_r)
        e_i = jnp.where(dist == d, pw[d][1], e_i)
    exit_row = 0 if rev else _SCAN_ROWS - 1

    def tile(q, carry):
        idx = (ntile - 1 - q) if rev else q
        rows = pl.ds(pl.multiple_of(idx * _SCAN_ROWS, _SCAN_ROWS), _SCAN_ROWS)
        y_r, y_i = yr_ref[rows, :], yi_ref[rows, :]
        for sh, k_r, k_i in steps:
            t_r, t_i = _cmul(k_r, k_i, pltpu.roll(y_r, sh, 0), pltpu.roll(y_i, sh, 0))
            y_r, y_i = y_r + t_r, y_i + t_i
        cb_r = jnp.broadcast_to(carry[0], (_SCAN_ROWS, wl))
        cb_i = jnp.broadcast_to(carry[1], (_SCAN_ROWS, wl))
        t_r, t_i = _cmul(e_r, e_i, cb_r, cb_i)
        y_r, y_i = y_r + t_r, y_i + t_i
        yr_ref[rows, :] = y_r
        yi_ref[rows, :] = y_i
        out = (y_r[exit_row:exit_row + 1], y_i[exit_row:exit_row + 1])
        if with_acc:
            n_r = jnp.where(sub == _SCAN_ROWS - 1, cb_r, pltpu.roll(y_r, _SCAN_ROWS - 1, 0))
            n_i = jnp.where(sub == _SCAN_ROWS - 1, cb_i, pltpu.roll(y_i, _SCAN_ROWS - 1, 0))
            s_r, s_i = x_refs[0][rows, :], x_refs[1][rows, :]
            out += (carry[2] + (n_r * s_r + n_i * s_i), carry[3] + (n_i * s_r - n_r * s_i))
        return out

    return lax.fori_loop(0, ntile, tile, (c_r, c_i) + ((zero, zero) if with_acc else ()), unroll=2)


_S5_WL = S5_N // S5_BLOCKS
_S5_CW = D_MODEL // S5_BLOCKS
_TN = (((0,), (0,)), ((), ()))


def _s5_fwd(hb, mats, *, tb=1024):
    s = hb.shape[0]
    tb = min(tb, s)

    def body(h_ref, wbr_ref, wbi_ref, wcr_ref, wci_ref, ar_ref, ai_ref, xr_ref, xi_ref, cp_ref, cr, ci):
        @pl.when(pl.program_id(1) == 0)
        def _():
            cr[...] = jnp.zeros_like(cr)
            ci[...] = jnp.zeros_like(ci)

        hv = h_ref[...]
        xr_ref[...] = jnp.dot(hv, wbr_ref[0], preferred_element_type=F32)
        xi_ref[...] = jnp.dot(hv, wbi_ref[0], preferred_element_type=F32)
        cr[...], ci[...] = _scan_tiles(xr_ref, xi_ref, ar_ref[...], ai_ref[...], cr[...], ci[...], rev=False)
        cp_ref[...] = (jnp.dot(xr_ref[...].astype(BF16), wcr_ref[0], preferred_element_type=F32)
                       + jnp.dot(xi_ref[...].astype(BF16), wci_ref[0], preferred_element_type=F32))

    wide = pl.BlockSpec((tb, _S5_WL), lambda c, t: (t, c))
    narrow = pl.BlockSpec((tb, _S5_CW), lambda c, t: (t, c))
    w_in = pl.BlockSpec((1, _S5_CW, _S5_WL), lambda c, t: (c, 0, 0))
    w_out = pl.BlockSpec((1, _S5_WL, _S5_CW), lambda c, t: (c, 0, 0))
    vec = pl.BlockSpec((1, _S5_WL), lambda c, t: (0, c))
    return pl.pallas_call(
        body, grid=(S5_BLOCKS, s // tb), in_specs=[narrow, w_in, w_in, w_out, w_out, vec, vec],
        out_specs=[wide, wide, narrow],
        out_shape=[jax.ShapeDtypeStruct((s, S5_N), F32)] * 2 + [jax.ShapeDtypeStruct((s, D_MODEL), F32)],
        scratch_shapes=[pltpu.VMEM((1, _S5_WL), F32)] * 2, name="s5_fwd",
        compiler_params=_params(("parallel", "arbitrary")),
    )(hb, mats["wb_re"], mats["wb_im"], mats["wc_re"], mats["wc_imn"], mats["ar"], mats["ai"])


def _s5_bwd(dyb, hb, xr, xi, dhd, mats, *, tb=1024):
    s = hb.shape[0]
    tb = min(tb, s)
    nt = s // tb

    def body(dy_ref, h_ref, xr_ref, xi_ref, dhd_ref, wctr_ref, wcti_ref, wbtr_ref, wbti_ref, ar_ref, ai_ref,
             dh_ref, dar_ref, dai_ref, dwbr_ref, dwbi_ref, dwcr_ref, dwci_ref, lr, li, cr, ci):
        @pl.when(pl.program_id(1) == 0)
        def _():
            for r in (cr, ci, dar_ref, dai_ref, dwbr_ref, dwbi_ref, dwcr_ref, dwci_ref):
                r[...] = jnp.zeros_like(r)

        dyv, hv = dy_ref[...], h_ref[...]
        lr[...] = jnp.dot(dyv, wctr_ref[0], preferred_element_type=F32)
        li[...] = jnp.dot(dyv, wcti_ref[0], preferred_element_type=F32)
        cr[...], ci[...], acc_r, acc_i = _scan_tiles(lr, li, ar_ref[...], ai_ref[...], cr[...], ci[...], rev=True,
                                                     x_refs=(xr_ref, xi_ref))
        dar_ref[...] += jnp.sum(acc_r, axis=0, keepdims=True)
        dai_ref[...] += jnp.sum(acc_i, axis=0, keepdims=True)
        lrb, lib = lr[...].astype(BF16), li[...].astype(BF16)
        dh_ref[...] = (jnp.dot(lrb, wbtr_ref[0], preferred_element_type=F32) + dhd_ref[...]
                       + jnp.dot(lib, wbti_ref[0], preferred_element_type=F32))
        dwbr_ref[0] += lax.dot_general(hv, lrb, _TN, preferred_element_type=F32)
        dwbi_ref[0] += lax.dot_general(hv, lib, _TN, preferred_element_type=F32)
        dwcr_ref[0] += lax.dot_general(dyv, xr_ref[...].astype(BF16), _TN, preferred_element_type=F32)
        dwci_ref[0] += lax.dot_general(dyv, xi_ref[...].astype(BF16), _TN, preferred_element_type=F32)

    wide = pl.BlockSpec((tb, _S5_WL), lambda c, t: (nt - 1 - t, c))
    narrow = pl.BlockSpec((tb, _S5_CW), lambda c, t: (nt - 1 - t, c))
    w_in = pl.BlockSpec((1, _S5_CW, _S5_WL), lambda c, t: (c, 0, 0))
    w_out = pl.BlockSpec((1, _S5_WL, _S5_CW), lambda c, t: (c, 0, 0))
    vec = pl.BlockSpec((1, _S5_WL), lambda c, t: (0, c))
    acc_shape = jax.ShapeDtypeStruct((S5_BLOCKS, _S5_CW, _S5_WL), F32)
    return pl.pallas_call(
        body, grid=(S5_BLOCKS, nt), in_specs=[narrow, narrow, wide, wide, narrow, w_in, w_in, w_out, w_out, vec, vec],
        out_specs=[narrow, vec, vec, w_in, w_in, w_in, w_in],
        out_shape=[jax.ShapeDtypeStruct((s, D_MODEL), F32)] + [jax.ShapeDtypeStruct((1, S5_N), F32)] * 2
        + [acc_shape] * 4,
        scratch_shapes=[pltpu.VMEM((tb, _S5_WL), F32)] * 2 + [pltpu.VMEM((1, _S5_WL), F32)] * 2, name="s5_bwd",
        compiler_params=_params(("parallel", "arbitrary")),
    )(dyb, hb, xr, xi, dhd, mats["wct_re"], mats["wct_imn"], mats["wbt_re"], mats["wbt_im"], mats["ar"], mats["ai"])


def _s5_mats(w, o):
    ab_re, ab_im, bb_re, bb_im = _s5_disc(w["od_a_re"][o], w["od_a_im"][o], w["od_log_dt"][o], w["od_b_re"][o],
                                          w["od_b_im"][o])
    c_re, c_im = w["od_c_re"][o], w["od_c_im"][o]
    tr = lambda t: t.transpose(0, 2, 1)
    bd = lambda t: _blockdiag(t).astype(BF16)
    return {
        "ar": ab_re.reshape(1, S5_N), "ai": ab_im.reshape(1, S5_N),
        "wb_re": bd(tr(bb_re)), "wb_im": bd(tr(bb_im)), "wc_re": bd(tr(c_re)), "wc_imn": bd(-tr(c_im)),
        "wct_re": bd(c_re), "wct_imn": bd(-c_im), "wbt_re": bd(bb_re), "wbt_im": bd(bb_im),
    }


def _odd_fwd(x0, w, o, i):
    mats = _s5_mats(w, o)
    nm = w["norm_mix"][i:i + 1]
    d_row = w["od_d"][o:o + 1]
    hb = _rms_fwd(x0, nm, name="mix_norm")
    xr, xi, cp = _s5_fwd(hb, mats)

    def fn(xv, cpv, gv, dv):
        y = cpv + dv * (xv * _rstd(xv) * gv)
        return y, _gelu(y)

    y, gy = _rw(fn, [x0, cp], [nm, d_row], [(D_MODEL, F32), (D_MODEL, BF16)], tr=512, name="s5_out")
    gg = _mm(gy, w["od_w_glu"][o], tb=True, name="od_glu")
    x1 = _rw(lambda xv, g: xv + g[:, :D_MODEL] * _sigmoid(g[:, D_MODEL:]), [x0, gg], [], [(D_MODEL, F32)], tr=512,
             name="od_glu_out")[0]
    return x1, (hb, xr, xi, y, gy, gg)


def _odd_bwd(dx1, x0, saved, w, o, i):
    hb, xr, xi, y, gy, gg = saved
    mats = _s5_mats(w, o)
    nm = w["norm_mix"][i:i + 1]
    d_row = w["od_d"][o:o + 1]

    def fn_glu(dv, g):
        ga, gb = g[:, :D_MODEL], g[:, D_MODEL:]
        sg = _sigmoid(gb)
        return jnp.concatenate([dv * sg, dv * ga * (sg * (1.0 - sg))], axis=1)

    dgg = _rw(fn_glu, [dx1, gg], [], [(2 * D_MODEL, BF16)], tr=512, name="od_glu_out_bwd")[0]
    d_wglu = _mm(gy, dgg, ta=True, name="od_glu_dw", bn=1024).T
    d_gy = _mm(dgg, w["od_w_glu"][o], name="od_glu_dx")

    def fn_y(dg, yv, xv, gv, dv):
        dy = dg * _gelu_grad(yv)
        h = xv * _rstd(xv) * gv
        return dy, dv * dy, jnp.sum(dy * h, axis=0, keepdims=True)

    dyb, dhd, d_d = _rw(fn_y, [d_gy, y, x0], [nm, d_row], [(D_MODEL, BF16), (D_MODEL, F32)], [(1, D_MODEL)], tr=512,
                        name="s5_out_bwd")
    dh, d_ar, d_ai, d_wb_re, d_wb_im, d_wc_re, d_wc_im = _s5_bwd(dyb, hb, xr, xi, dhd, mats)
    gc, gp = S5_GROUP_CH, S5_STATE
    d_c_re = _blockdiag_extract(d_wc_re, gc, gp)
    d_c_im = -_blockdiag_extract(d_wc_im, gc, gp)
    d_bb_re = _blockdiag_extract(d_wb_re, gc, gp).transpose(0, 2, 1)
    d_bb_im = _blockdiag_extract(d_wb_im, gc, gp).transpose(0, 2, 1)
    dx0, d_nm = _rms_bwd_call(x0, dh, nm, dx1, name="mix_norm_bwd")
    _, vjp = jax.vjp(_s5_disc, w["od_a_re"][o], w["od_a_im"][o], w["od_log_dt"][o], w["od_b_re"][o], w["od_b_im"][o])
    d_a_re, d_a_im, d_log_dt, d_b_re, d_b_im = vjp(
        (d_ar.reshape(S5_GROUPS, S5_STATE), d_ai.reshape(S5_GROUPS, S5_STATE), d_bb_re, d_bb_im))
    grads = {"od_a_re": d_a_re, "od_a_im": d_a_im, "od_log_dt": d_log_dt, "od_b_re": d_b_re, "od_b_im": d_b_im,
             "od_c_re": d_c_re, "od_c_im": d_c_im, "od_d": d_d[0], "od_w_glu": d_wglu, "norm_mix": d_nm[0]}
    return dx0, grads


_HBM = pl.BlockSpec(memory_space=pltpu.HBM)


def _mesh_pos():
    return lax.axis_index("x"), lax.axis_index("y"), lax.axis_index("c")


def _slot(px, py, pc):
    return 4 * px + 2 * py + pc


def _all_gather(x, *, name):
    def body(x_ref, out_ref, send_sems, recv_sems, local_sem):
        mx, my, mc = _mesh_pos()
        me, sibling = (mx, my, mc), (mx, my, 1 - mc)
        chips = [(1 - mx, my), (mx, 1 - my), (1 - mx, 1 - my)]

        def copy(k, block, to, src=None):
            dst = out_ref.at[_slot(*block)]
            return pltpu.make_async_remote_copy(
                src_ref=dst if src is None else src, dst_ref=dst, send_sem=send_sems.at[k], recv_sem=recv_sems.at[k],
                device_id=to, device_id_type=MESH)

        mine = pltpu.make_async_copy(x_ref, out_ref.at[_slot(*me)], local_sem)
        mine.start()
        first = [copy(0, me, sibling, src=x_ref)]
        first += [copy(1 + j, me, (*chip, mc), src=x_ref) for j, chip in enumerate(chips)]
        for cp in first:
            cp.start()
        passed = [copy(4 + j, (*chip, mc), sibling) for j, chip in enumerate(chips)]
        for j, chip in enumerate(chips):
            copy(1 + j, (*chip, mc), me).wait_recv()
            passed[j].start()
        copy(0, sibling, me).wait_recv()
        for j, chip in enumerate(chips):
            copy(4 + j, (*chip, 1 - mc), me).wait_recv()
        for cp in first + passed:
            cp.wait_send()
        mine.wait()

    return pl.pallas_call(
        body, out_shape=jax.ShapeDtypeStruct((N_DEV,) + x.shape, x.dtype), in_specs=[_HBM], out_specs=_HBM,
        scratch_shapes=[pltpu.SemaphoreType.DMA((7,)), pltpu.SemaphoreType.DMA((7,)), pltpu.SemaphoreType.DMA],
        name=name,
    )(x)


def _all_to_all(x, *, name):
    def body(x_ref, out_ref, send_sems, recv_sems, local_sem):
        mx, my, mc = _mesh_pos()
        my_slot = _slot(mx, my, mc)
        mine = pltpu.make_async_copy(x_ref.at[my_slot], out_ref.at[my_slot], local_sem)
        mine.start()
        copies = []
        for k in range(1, N_DEV):
            peer = (1 - mx if k & 4 else mx, 1 - my if k & 2 else my, 1 - mc if k & 1 else mc)
            copies.append(pltpu.make_async_remote_copy(
                src_ref=x_ref.at[_slot(*peer)], dst_ref=out_ref.at[my_slot], send_sem=send_sems.at[k - 1],
                recv_sem=recv_sems.at[k - 1], device_id=peer, device_id_type=MESH))
        for cp in copies:
            cp.start()
        for cp in copies:
            cp.wait()
        mine.wait()

    return pl.pallas_call(
        body, out_shape=jax.ShapeDtypeStruct(x.shape, x.dtype), in_specs=[_HBM], out_specs=_HBM,
        scratch_shapes=[pltpu.SemaphoreType.DMA((7,)), pltpu.SemaphoreType.DMA((7,)), pltpu.SemaphoreType.DMA],
        name=name,
    )(x)


_ADAMW_PARTS_BLOCK_BYTES = 8 * 1024 * 1024


def _adamw(parts, w, m, v, *, name):
    r, c = w.shape
    tr = r
    for cand in range(8, r + 1, 8):
        if r % cand == 0 and N_DEV * cand * c * 4 <= _ADAMW_PARTS_BLOCK_BYTES:
            tr = cand

    def body(p_ref, w_ref, m_ref, v_ref, g_ref, d_ref, nm_ref, nv_ref):
        g = p_ref[0]
        for k in range(1, N_DEV):
            g = g + p_ref[k]
        nm = ADAM_B1 * m_ref[...] + (1.0 - ADAM_B1) * g
        nv = ADAM_B2 * v_ref[...] + (1.0 - ADAM_B2) * (g * g)
        m_hat = nm / (1.0 - ADAM_B1 ** ADAM_STEP)
        v_hat = nv / (1.0 - ADAM_B2 ** ADAM_STEP)
        g_ref[...] = g
        d_ref[...] = -ADAM_LR * (m_hat / (jnp.sqrt(v_hat) + ADAM_EPS) + ADAM_WD * w_ref[...])
        nm_ref[...] = nm
        nv_ref[...] = nv

    blk = pl.BlockSpec((tr, c), lambda i: (i, 0))
    return pl.pallas_call(
        body, grid=(r // tr,), in_specs=[pl.BlockSpec((N_DEV, tr, c), lambda i: (0, i, 0)), blk, blk, blk],
        out_specs=[blk] * 4, out_shape=[jax.ShapeDtypeStruct((r, c), F32)] * 4, name=name,
        compiler_params=_params(("parallel",)),
    )(parts, w, m, v)


_FLAT_COLS = 1024
_FLAT_ROW_ALIGN = 128

_SHARD_AXIS = {
    "norm_mix": None, "norm_ffn": None, "ev_w_in": 2, "ev_b_fgate": None, "ev_q_norm": None, "ev_k_norm": None,
    "ev_v_norm": None, "ev_w_spatial": None, "ev_b_spatial": None, "ev_w_out": 1, "od_a_re": None, "od_a_im": None,
    "od_log_dt": None, "od_b_re": None, "od_b_im": None, "od_c_re": None, "od_c_im": None, "od_d": 1, "od_w_glu": 2,
    "ffn_w_up": 2, "ffn_conv_w": 2, "ffn_conv_b": None, "ffn_w_down": 1, "ple_w_proj": 2, "ple_w_gate": 1,
}
_WEIGHTS = list(_SHARD_AXIS)
_REPLICATED = [n for n in _WEIGHTS if _SHARD_AXIS[n] is None]
_COL_SHARDED = ["ev_w_in", "od_w_glu", "ffn_w_up", "ple_w_proj"]
_ROW_SHARDED = ["ev_w_out", "ffn_w_down", "ple_w_gate"]
_SMALL_SHARDED = ["od_d", "ffn_conv_w"]


def _pack(arrays, lead=()):
    nl = len(lead)
    flat = jnp.concatenate([a.reshape(lead + (-1,)) for a in arrays], axis=nl)
    n = flat.shape[nl]
    chunk = _FLAT_COLS * _FLAT_ROW_ALIGN
    total = -(-n // chunk) * chunk
    flat = jnp.pad(flat, [(0, 0)] * nl + [(0, total - n)])
    return flat.reshape(lead + (total // _FLAT_COLS, _FLAT_COLS))


def _unpack(buf, shapes, lead=()):
    nl = len(lead)
    flat = buf.reshape(lead + (-1,))
    out, off = [], 0
    for shp in shapes:
        n = math.prod(shp)
        out.append(lax.slice_in_dim(flat, off, off + n, axis=nl).reshape(lead + tuple(shp)))
        off += n
    return out


def _join_shards(g, axis):
    g = jnp.moveaxis(g, 0, axis)
    shp = g.shape
    return g.reshape(shp[:axis] + (shp[axis] * shp[axis + 1],) + shp[axis + 2:])


def _split_shards(full, axis):
    shp = full.shape
    g = full.reshape(shp[:axis] + (N_DEV, shp[axis] // N_DEV) + shp[axis + 1:])
    return jnp.moveaxis(g, axis, 0)


def _local_step(x, p, tgt, w):
    saved = []
    h = x
    for i in range(DEPTH):
        x0 = h
        if i % 2 == 0:
            x1, sv_mix = _even_fwd(x0, w, i // 2, i)
        else:
            x1, sv_mix = _odd_fwd(x0, w, i // 2, i)
        x2, sv_ffn = _ffn_fwd(x1, w, i)
        x3, sv_ple = _ple_fwd(x2, p[i], w, i)
        saved.append((x0, x1, x2, sv_mix, sv_ffn, sv_ple))
        h = x3
    loss, dh = _loss_and_grad(h, tgt)
    per_layer = {}
    for i in reversed(range(DEPTH)):
        x0, x1, x2, sv_mix, sv_ffn, sv_ple = saved[i]
        dx2, dx2b, g_ple = _ple_bwd(dh, x2, p[i], sv_ple, w, i)
        dx1, dx1b, g_ffn = _ffn_bwd(dx2, dx2b, x1, sv_ffn, w, i)
        if i % 2 == 0:
            dh, g_mix = _even_bwd(dx1, dx1b, x0, sv_mix, w, i // 2, i)
        else:
            dh, g_mix = _odd_bwd(dx1, x0, sv_mix, w, i // 2, i)
        for name, g in {**g_ple, **g_ffn, **g_mix}.items():
            per_layer.setdefault(name, []).append(g)
    grads = {name: gs[::-1] for name, gs in per_layer.items()}
    return loss, dh, grads


def kernel(x, p, norm_mix, norm_ffn, ev_w_in, ev_b_fgate, ev_q_norm, ev_k_norm, ev_v_norm, ev_w_spatial, ev_b_spatial, ev_w_out, od_a_re, od_a_im, od_log_dt, od_b_re, od_b_im, od_c_re, od_c_im, od_d, od_w_glu, ffn_w_up, ffn_conv_w, ffn_conv_b, ffn_w_down, ple_w_proj, ple_w_gate, loss_target, m_norm_mix, m_norm_ffn, m_ev_w_in, m_ev_b_fgate, m_ev_q_norm, m_ev_k_norm, m_ev_v_norm, m_ev_w_spatial, m_ev_b_spatial, m_ev_w_out, m_od_a_re, m_od_a_im, m_od_log_dt, m_od_b_re, m_od_b_im, m_od_c_re, m_od_c_im, m_od_d, m_od_w_glu, m_ffn_w_up, m_ffn_conv_w, m_ffn_conv_b, m_ffn_w_down, m_ple_w_proj, m_ple_w_gate, v_norm_mix, v_norm_ffn, v_ev_w_in, v_ev_b_fgate, v_ev_q_norm, v_ev_k_norm, v_ev_v_norm, v_ev_w_spatial, v_ev_b_spatial, v_ev_w_out, v_od_a_re, v_od_a_im, v_od_log_dt, v_od_b_re, v_od_b_im, v_od_c_re, v_od_c_im, v_od_d, v_od_w_glu, v_ffn_w_up, v_ffn_conv_w, v_ffn_conv_b, v_ffn_w_down, v_ple_w_proj, v_ple_w_gate):
    args = locals()
    wl = {n: args[n] for n in _WEIGHTS}
    ml = {n: args["m_" + n] for n in _WEIGHTS}
    vl = {n: args["v_" + n] for n in _WEIGHTS}

    kinds = ("g", "d", "m", "v")

    def local(n, layer, a):
        return a[n][layer].T if n in _COL_SHARDED else a[n][layer]

    w = {n: wl[n] for n in _REPLICATED}
    for n in _COL_SHARDED + _ROW_SHARDED:
        w[n] = []
        for layer in range(wl[n].shape[0]):
            g = _all_gather(local(n, layer, wl).astype(BF16), name="gather_" + n)
            full = g.reshape(N_DEV * g.shape[1], g.shape[2])
            if n == "ev_w_in":
                full = jnp.pad(full, ((0, IN_COLS_PAD - IN_COLS), (0, 0)))
            w[n].append(full)
    small = _all_gather(_pack([wl[n] for n in _SMALL_SHARDED]), name="gather_small_weights")
    for n, g in zip(_SMALL_SHARDED, _unpack(small, [wl[n].shape for n in _SMALL_SHARDED], lead=(N_DEV,))):
        w[n] = _join_shards(g, _SHARD_AXIS[n])

    loss, grad_x, grads = _local_step(x[0], p[:, 0], loss_target[0], w)
    loss = lax.psum(loss, ("x", "y", "c"))

    out = {kind: {} for kind in kinds}
    for n in _COL_SHARDED + _ROW_SHARDED:
        layers = []
        for layer, g in enumerate(grads[n]):
            parts = _all_to_all(g.reshape(N_DEV, g.shape[0] // N_DEV, g.shape[1]), name="scatter_" + n)
            res = _adamw(parts, local(n, layer, wl), local(n, layer, ml), local(n, layer, vl), name="adamw_" + n)
            layers.append([r.T if n in _COL_SHARDED else r for r in res])
        for k, kind in enumerate(kinds):
            out[kind][n] = jnp.stack([lay[k] for lay in layers])

    send = _pack([_split_shards(jnp.stack(grads[n]), _SHARD_AXIS[n]) for n in _SMALL_SHARDED], lead=(N_DEV,))
    parts = _all_to_all(send, name="scatter_small_grads")
    shapes = [wl[n].shape for n in _SMALL_SHARDED]
    res = _adamw(parts, _pack([wl[n] for n in _SMALL_SHARDED]), _pack([ml[n] for n in _SMALL_SHARDED]),
                 _pack([vl[n] for n in _SMALL_SHARDED]), name="adamw_small")
    for kind, buf in zip(kinds, res):
        out[kind].update(zip(_SMALL_SHARDED, _unpack(buf, shapes)))

    parts = _all_gather(_pack([jnp.stack(grads[n]) for n in _REPLICATED]), name="gather_replicated_grads")
    shapes = [wl[n].shape for n in _REPLICATED]
    res = _adamw(parts, _pack([wl[n] for n in _REPLICATED]), _pack([ml[n] for n in _REPLICATED]),
                 _pack([vl[n] for n in _REPLICATED]), name="adamw_replicated")
    for kind, buf in zip(kinds, res):
        out[kind].update(zip(_REPLICATED, _unpack(buf, shapes)))

    return (loss, grad_x[None], *[out["g"][n] for n in _WEIGHTS], *[out["d"][n] for n in _WEIGHTS],
            *[out["m"][n] for n in _WEIGHTS], *[out["v"][n] for n in _WEIGHTS])
```

```python
import functools
import math

import jax
import jax.numpy as jnp
from jax import lax
from jax.experimental import pallas as pl
from jax.experimental.pallas import tpu as pltpu

F32 = jnp.float32
BF16 = jnp.bfloat16

D_MODEL = 1024
DEPTH = 4
A_GROUPS = 4
A_CHUNK = 128
A_WIDTH = 512
B_HEADS = 8
B_HEAD_DIM = 64
B_WIDTH = 512
IN_COLS = 2 * A_WIDTH + 3 * B_WIDTH + B_HEADS
IN_COLS_PAD = 2688
S5_GROUP_CH = 16
S5_GROUPS = 64
S5_STATE = 64
S5_N = S5_GROUPS * S5_STATE
S5_BLOCKS = 8
D_FF = 2816
PLE_DIM = 256
EPS = 1e-6
NEG_INF = -1e30
N_DEV = 8

ADAM_LR = 0.001
ADAM_B1 = 0.9
ADAM_B2 = 0.999
ADAM_EPS = 1e-08
ADAM_WD = 0.01
ADAM_STEP = 10

VMEM_LIMIT_BYTES = 56 * 1024 * 1024
MESH = pl.DeviceIdType.MESH


def _params(sem, vmem=VMEM_LIMIT_BYTES):
    return pltpu.CompilerParams(dimension_semantics=sem, vmem_limit_bytes=vmem)


_GELU_K = math.sqrt(2.0 / math.pi)


def _gelu(x):
    return x * (0.5 * (1.0 + jnp.tanh(_GELU_K * (x + 0.044715 * (x * x * x)))))


def _gelu_grad(x):
    t = jnp.tanh(_GELU_K * (x + 0.044715 * (x * x * x)))
    return 0.5 * (1.0 + t) + 0.5 * x * (1.0 - t * t) * (_GELU_K * (1.0 + 3.0 * 0.044715 * (x * x)))


def _sigmoid(x):
    return 0.5 * jnp.tanh(0.5 * x) + 0.5


def _rstd(x):
    return lax.rsqrt(jnp.mean(x * x, axis=-1, keepdims=True) + EPS)


def _rms_bwd(x, dy, g):
    r = _rstd(x)
    xh = x * r
    dyg = dy if g is None else dy * g
    dx = r * (dyg - xh * jnp.mean(dyg * xh, axis=-1, keepdims=True))
    return dx, dy * xh


def _shift_down(blk, halo, k):
    tr = blk.shape[0]
    r = pltpu.roll(blk, k, 0)
    hr = pltpu.roll(halo, k, 0)
    first = jnp.where(lax.broadcasted_iota(jnp.int32, hr.shape, 0) < k, hr, r[0:8])
    return jnp.concatenate([first, r[8:tr]], axis=0)


def _shift_up(blk, halo, k):
    tr = blk.shape[0]
    r = pltpu.roll(blk, tr - k, 0)
    hr = pltpu.roll(halo, 8 - k, 0)
    last = jnp.where(lax.broadcasted_iota(jnp.int32, hr.shape, 0) >= 8 - k, hr, r[tr - 8:tr])
    return jnp.concatenate([r[0:tr - 8], last], axis=0)


def _rw(fn, rows, consts, outs, accs=(), *, tr, name, prev=(), nxt=(), widths=None):
    s = rows[0].shape[0]
    tr = min(tr, s)
    n = s // tr
    nr, nc, npv, nnx, no, na = len(rows), len(consts), len(prev), len(nxt), len(outs), len(accs)
    widths = widths or [None] * nr

    def body(*refs):
        ins, out_refs = refs[:nr + nc + npv + nnx], refs[nr + nc + npv + nnx:]
        i = pl.program_id(0)
        vals = [r[...] for r in ins[:nr + nc]]
        vals += [jnp.where(i == 0, 0.0, r[...]) for r in ins[nr + nc:nr + nc + npv]]
        vals += [jnp.where(i == n - 1, 0.0, r[...]) for r in ins[nr + nc + npv:]]
        res = fn(*vals)
        if not isinstance(res, (tuple, list)):
            res = (res,)
        for k in range(no):
            out_refs[k][...] = res[k].astype(out_refs[k].dtype)
        if na:
            @pl.when(i == 0)
            def _():
                for k in range(na):
                    out_refs[no + k][...] = jnp.zeros_like(out_refs[no + k])

            for k in range(na):
                out_refs[no + k][...] += res[no + k]

    in_specs = []
    for a, w in zip(rows, widths):
        if w is None:
            in_specs.append(pl.BlockSpec((tr, a.shape[1]), lambda i: (i, 0)))
        else:
            in_specs.append(pl.BlockSpec((tr, w[1]), functools.partial(lambda i, cb: (i, cb), cb=w[0])))
    for c in consts:
        in_specs.append(pl.BlockSpec(c.shape, functools.partial(lambda i, nd: (0,) * nd, nd=c.ndim)))
    t8 = tr // 8
    for a in prev:
        in_specs.append(pl.BlockSpec((8, a.shape[1]), lambda i: (jnp.maximum(i * t8 - 1, 0), 0)))
    for a in nxt:
        in_specs.append(pl.BlockSpec((8, a.shape[1]), lambda i: (jnp.minimum((i + 1) * t8, s // 8 - 1), 0)))
    out_shape = [jax.ShapeDtypeStruct((s, w), dt) for w, dt in outs]
    out_specs = [pl.BlockSpec((tr, w), lambda i: (i, 0)) for w, _ in outs]
    out_shape += [jax.ShapeDtypeStruct(a, F32) for a in accs]
    out_specs += [pl.BlockSpec(a, lambda i: (0, 0)) for a in accs]
    res = pl.pallas_call(
        body, grid=(n,), in_specs=in_specs, out_specs=out_specs, out_shape=out_shape, name=name,
        compiler_params=_params(("arbitrary",) if na else ("parallel",)),
    )(*rows, *consts, *prev, *nxt)
    return res


def _pick(n, cap):
    if n <= cap:
        return n
    best = None
    for d in range(128, cap + 1, 128):
        if n % d == 0:
            best = d
    assert best is not None, (n, cap)
    return best


def _mm(a, b, *, ta=False, tb=False, out_dtype=F32, add=None, epilogue=None, name, bm=1024, bn=512, bk=None):
    (k_dim, m) = a.shape if ta else a.shape[::-1]
    n = b.shape[0] if tb else b.shape[1]
    assert (b.shape[1] if tb else b.shape[0]) == k_dim
    if bk is None:
        bk = 1024 if ta else 2816
    bm, bn, bk = _pick(m, bm), _pick(n, bn), _pick(k_dim, bk)
    nk = k_dim // bk
    dims = (((0 if ta else 1,), (1 if tb else 0,)), ((), ()))
    assert add is None or epilogue is None
    epi_fn, extra, out_dtypes = epilogue or (None, [add] if add is not None else [], (out_dtype,))
    n_extra, n_out = len(extra), len(out_dtypes)

    def body(*refs):
        a_ref, b_ref = refs[0], refs[1]
        extra_refs = refs[2:2 + n_extra]
        out_refs = refs[2 + n_extra:2 + n_extra + n_out]
        p = lax.dot_general(a_ref[...].astype(BF16), b_ref[...].astype(BF16), dims, preferred_element_type=F32)

        def finish(acc):
            if epi_fn is not None:
                res = epi_fn(acc, *[r[...] for r in extra_refs])
            else:
                res = (acc + extra_refs[0][...],) if n_extra else (acc,)
            for o_ref, r in zip(out_refs, res):
                o_ref[...] = r.astype(o_ref.dtype)

        if nk == 1:
            finish(p)
        else:
            acc_ref = refs[2 + n_extra + n_out]
            kk = pl.program_id(2)

            @pl.when(kk == 0)
            def _():
                acc_ref[...] = p

            @pl.when(kk > 0)
            def _():
                acc_ref[...] += p

            @pl.when(kk == nk - 1)
            def _():
                finish(acc_ref[...])

    a_spec = pl.BlockSpec((bk, bm), lambda i, j, k: (k, i)) if ta else pl.BlockSpec((bm, bk), lambda i, j, k: (i, k))
    b_spec = pl.BlockSpec((bn, bk), lambda i, j, k: (j, k)) if tb else pl.BlockSpec((bk, bn), lambda i, j, k: (k, j))
    tile = pl.BlockSpec((bm, bn), lambda i, j, k: (i, j))
    res = pl.pallas_call(
        body, grid=(m // bm, n // bn, nk), in_specs=[a_spec, b_spec] + [tile] * n_extra,
        out_specs=[tile] * n_out, out_shape=[jax.ShapeDtypeStruct((m, n), dt) for dt in out_dtypes],
        scratch_shapes=[pltpu.VMEM((bm, bn), F32)] if nk > 1 else [],
        name=name, compiler_params=_params(("parallel", "parallel", "arbitrary")),
    )(a, b, *extra)
    return res if epilogue is not None else res[0]


def _rms_fwd(x, g, *, name):
    if g is None:
        return _rw(lambda xv: xv * _rstd(xv), [x], [], [(x.shape[1], BF16)], tr=512, name=name)[0]
    return _rw(lambda xv, gv: xv * _rstd(xv) * gv, [x], [g], [(x.shape[1], BF16)], tr=512, name=name)[0]


def _rms_bwd_call(x, dy, g, dx_add, *, name, tr=512, scale=None, bf16_copy=False):
    w = x.shape[1]
    has_g, has_add = g is not None, dx_add is not None

    def fn(*v):
        xv, dyv = v[0], v[1]
        if scale is not None:
            dyv = dyv * scale
        gv = v[2 + has_add] if has_g else None
        dx, dg = _rms_bwd(xv, dyv, gv)
        if has_add:
            dx = dx + v[2]
        return (dx,) + ((dx,) if bf16_copy else ()) + ((jnp.sum(dg, axis=0, keepdims=True),) if has_g else ())

    rows = [x, dy] + ([dx_add] if has_add else [])
    outs = [(w, F32)] + ([(w, BF16)] if bf16_copy else [])
    res = _rw(fn, rows, [g] if has_g else [], outs, [(1, w)] if has_g else [], tr=tr, name=name)
    return tuple(res) if has_g else tuple(res) + (None,)


def _conv_taps(hup, halo, cw):
    h1 = _shift_down(hup, halo, 1)
    h2 = _shift_down(hup, halo, 2)
    return h1, h2


def _ffn_fwd(x1, w, i):
    h2b = _rms_fwd(x1, w["norm_ffn"][i:i + 1], name="ffn_norm")
    hup = _mm(h2b, w["ffn_w_up"][i], tb=True, name="ffn_up", bn=1408)
    cw, cb = w["ffn_conv_w"][i], w["ffn_conv_b"][i:i + 1]

    def fn(hv, cwv, cbv, halo):
        h1, h2 = _conv_taps(hv, halo, cwv)
        hc = cbv + cwv[0:1] * h2
        hc = hc + cwv[1:2] * h1
        hc = hc + cwv[2:3] * hv
        g, up = hc[:, :D_FF], hc[:, D_FF:]
        return (g * _sigmoid(g)) * up

    a = _rw(fn, [hup], [cw, cb], [(D_FF, BF16)], tr=128, name="ffn_conv_gate", prev=[hup])[0]
    x2 = _mm(a, w["ffn_w_down"][i], add=x1, name="ffn_down")
    return x2, (h2b, hup, a)


def _ffn_bwd(dx2, dx2b, x1, saved, w, i):
    h2b, hup, a = saved
    cw, cb = w["ffn_conv_w"][i], w["ffn_conv_b"][i:i + 1]
    da = _mm(dx2b, w["ffn_w_down"][i], tb=True, name="ffn_down_dx", bn=1408)
    d_wdown = _mm(a, dx2b, ta=True, name="ffn_down_dw", bm=1408)

    def fn1(hv, dav, cwv, cbv, h_prev, h_next, da_next):
        tr = hv.shape[0]
        he = jnp.concatenate([hv, h_next], axis=0)
        dae = jnp.concatenate([dav, da_next], axis=0)
        h1, h2 = _conv_taps(he, h_prev, cwv)
        hc = cbv + cwv[0:1] * h2
        hc = hc + cwv[1:2] * h1
        hc = hc + cwv[2:3] * he
        g, up = hc[:, :D_FF], hc[:, D_FF:]
        sg = _sigmoid(g)
        d_up = dae * (g * sg)
        d_g = dae * up * (sg * (1.0 + g * (1.0 - sg)))
        d_hce = jnp.concatenate([d_g, d_up], axis=1)
        rows_e = tr + 8
        d_hup = (cwv[2:3] * d_hce[:tr] + cwv[1:2] * pltpu.roll(d_hce, rows_e - 1, 0)[:tr]
                 + cwv[0:1] * pltpu.roll(d_hce, rows_e - 2, 0)[:tr])
        d_hc = d_hce[:tr]
        col = lambda v: jnp.sum(v, axis=0, keepdims=True)
        return d_hup, col(d_hc), col(d_hc * h2[:tr]), col(d_hc * h1[:tr]), col(d_hc * hv)

    w2 = 2 * D_FF
    d_hup, d_cb, d_cw0, d_cw1, d_cw2 = _rw(fn1, [hup, da], [cw, cb], [(w2, BF16)], [(1, w2)] * 4, tr=128,
                                           name="ffn_conv_gate_bwd", prev=[hup], nxt=[hup, da])
    d_h2 = _mm(d_hup, w["ffn_w_up"][i], name="ffn_up_dx")
    d_wup_t = _mm(h2b, d_hup, ta=True, name="ffn_up_dw", bn=1408).T
    dx1, dx1b, d_norm = _rms_bwd_call(x1, d_h2, w["norm_ffn"][i:i + 1], dx2, name="ffn_norm_bwd", bf16_copy=True)
    grads = {"ffn_w_up": d_wup_t, "ffn_w_down": d_wdown, "ffn_conv_b": d_cb[0],
             "ffn_conv_w": jnp.concatenate([d_cw0, d_cw1, d_cw2], axis=0), "norm_ffn": d_norm[0]}
    return dx1, dx1b, grads


def _ple_fwd(x2, p_i, w, i):
    rn = _rms_fwd(x2, None, name="ple_norm")
    gl = _mm(rn, w["ple_w_gate"][i], name="ple_gate")
    x3 = _mm(p_i, w["ple_w_proj"][i], tb=True, name="ple_proj",
             epilogue=(lambda pp, xv, g: (xv + _sigmoid(g) * pp,), [x2, gl], (F32,)))[0]
    return x3, (rn, gl)


def _ple_bwd(dx3, x2, p_i, saved, w, i):
    rn, gl = saved

    def fn(pp, dv, g):
        sg = _sigmoid(g)
        return dv * sg, dv * pp * (sg * (1.0 - sg))

    d_pp, d_pre = _mm(p_i, w["ple_w_proj"][i], tb=True, name="ple_proj_bwd", epilogue=(fn, [dx3, gl], (BF16, BF16)))
    d_wproj_t = _mm(p_i, d_pp, ta=True, name="ple_proj_dw", bn=1024).T
    d_wgate = _mm(rn, d_pre, ta=True, name="ple_gate_dw")
    d_rn = _mm(d_pre, w["ple_w_gate"][i], tb=True, name="ple_gate_dx")
    dx2, dx2b, _ = _rms_bwd_call(x2, d_rn, None, dx3, name="ple_norm_bwd", bf16_copy=True)
    return dx2, dx2b, {"ple_w_proj": d_wproj_t, "ple_w_gate": d_wgate}


def _loss_and_grad(xf, tgt):
    def fn(xv, tv):
        d = xv - tv
        part = 0.5 * jnp.sum(jnp.mean(d * d, axis=-1, keepdims=True), axis=0, keepdims=True)
        return d * (1.0 / D_MODEL), part

    dx, loss = _rw(fn, [xf, tgt], [], [(D_MODEL, F32)], [(1, 1)], tr=512, name="loss_head")
    return loss[0, 0], dx


def _tril_mask():
    shp = (A_CHUNK, A_CHUNK)
    return lax.broadcasted_iota(jnp.int32, shp, 0) >= lax.broadcasted_iota(jnp.int32, shp, 1)


def _gmlp_fwd(z, w_s, b_s, v_gain, *, tr=512):
    s = z.shape[0]
    tr = min(tr, s)
    gw = A_WIDTH // A_GROUPS

    def body(z_ref, w_ref, b_ref, g_ref, y_ref):
        tril = _tril_mask()
        for g in range(A_GROUPS):
            wg = jnp.where(tril, w_ref[g], 0.0).astype(BF16)
            cols = slice(g * gw, (g + 1) * gw)
            vcols = slice(A_WIDTH + g * gw, A_WIDTH + (g + 1) * gw)
            for c in range(tr // A_CHUNK):
                rows = slice(c * A_CHUNK, (c + 1) * A_CHUNK)
                v = _gelu(z_ref[rows, vcols])
                vn = v * _rstd(v) * g_ref[:, cols]
                sv = jnp.dot(wg, vn.astype(BF16), preferred_element_type=F32) + b_ref[g]
                y_ref[rows, cols] = (_gelu(z_ref[rows, cols]) * sv).astype(BF16)

    return pl.pallas_call(
        body, grid=(s // tr,),
        in_specs=[pl.BlockSpec((tr, 2 * A_WIDTH), lambda i: (i, 0)),
                  pl.BlockSpec(w_s.shape, lambda i: (0, 0, 0)), pl.BlockSpec(b_s.shape, lambda i: (0, 0, 0)),
                  pl.BlockSpec(v_gain.shape, lambda i: (0, 0))],
        out_specs=pl.BlockSpec((tr, A_WIDTH), lambda i: (i, 0)),
        out_shape=jax.ShapeDtypeStruct((s, A_WIDTH + B_WIDTH), BF16), name="gmlp_fwd",
        compiler_params=_params(("parallel",)),
    )(z, w_s, b_s, v_gain)


def _gmlp_bwd(z, d_ymix, w_s, b_s, v_gain, dz, *, tr=512):
    s = z.shape[0]
    tr = min(tr, s)
    gw = A_WIDTH // A_GROUPS

    def body(z_ref, dy_ref, w_ref, b_ref, g_ref, _, dz_ref, dw_ref, db_ref, dg_ref):
        @pl.when(pl.program_id(0) == 0)
        def _():
            dw_ref[...] = jnp.zeros_like(dw_ref)
            db_ref[...] = jnp.zeros_like(db_ref)
            dg_ref[...] = jnp.zeros_like(dg_ref)

        tril = _tril_mask()
        for g in range(A_GROUPS):
            wg = jnp.where(tril, w_ref[g], 0.0).astype(BF16)
            cols = slice(g * gw, (g + 1) * gw)
            vcols = slice(A_WIDTH + g * gw, A_WIDTH + (g + 1) * gw)
            gain = g_ref[:, cols]
            for c in range(tr // A_CHUNK):
                rows = slice(c * A_CHUNK, (c + 1) * A_CHUNK)
                va, ua = z_ref[rows, vcols], z_ref[rows, cols]
                v = _gelu(va)
                r = _rstd(v)
                vh = v * r
                vnb = (vh * gain).astype(BF16)
                sv = jnp.dot(wg, vnb, preferred_element_type=F32) + b_ref[g]
                dy = dy_ref[rows, cols]
                d_sv = dy * _gelu(ua)
                dz_ref[rows, cols] = (dy * sv * _gelu_grad(ua)).astype(BF16)
                d_svb = d_sv.astype(BF16)
                d_vn = lax.dot_general(wg, d_svb, (((0,), (0,)), ((), ())), preferred_element_type=F32)
                dwp = lax.dot_general(d_svb, vnb, (((1,), (1,)), ((), ())), preferred_element_type=F32)
                dw_ref[g] += jnp.where(tril, dwp, 0.0)
                db_ref[g] += jnp.sum(d_sv, axis=1, keepdims=True)
                dg_ref[:, cols] += jnp.sum(d_vn * vh, axis=0, keepdims=True)
                d_vh = d_vn * gain
                d_v = r * (d_vh - vh * jnp.mean(d_vh * vh, axis=-1, keepdims=True))
                dz_ref[rows, vcols] = (d_v * _gelu_grad(va)).astype(BF16)

    return pl.pallas_call(
        body, grid=(s // tr,),
        in_specs=[pl.BlockSpec((tr, 2 * A_WIDTH), lambda i: (i, 0)), pl.BlockSpec((tr, A_WIDTH), lambda i: (i, 0)),
                  pl.BlockSpec(w_s.shape, lambda i: (0, 0, 0)), pl.BlockSpec(b_s.shape, lambda i: (0, 0, 0)),
                  pl.BlockSpec(v_gain.shape, lambda i: (0, 0)), _ANY],
        out_specs=[pl.BlockSpec((tr, 2 * A_WIDTH), lambda i: (i, 0)), pl.BlockSpec(w_s.shape, lambda i: (0, 0, 0)),
                   pl.BlockSpec(b_s.shape, lambda i: (0, 0, 0)), pl.BlockSpec(v_gain.shape, lambda i: (0, 0))],
        out_shape=[jax.ShapeDtypeStruct(dz.shape, dz.dtype), jax.ShapeDtypeStruct(w_s.shape, F32),
                   jax.ShapeDtypeStruct(b_s.shape, F32), jax.ShapeDtypeStruct(v_gain.shape, F32)],
        input_output_aliases={5: 0}, name="gmlp_bwd", compiler_params=_params(("arbitrary",)),
    )(z, d_ymix, w_s, b_s, v_gain, dz)


def _dot3(x, ub):
    x1 = x.astype(BF16)
    r1 = x - x1.astype(F32)
    x2 = r1.astype(BF16)
    x3 = (r1 - x2.astype(F32)).astype(BF16)
    d = lambda a: jnp.dot(a, ub, preferred_element_type=F32)
    return d(x1) + d(x2) + d(x3)


def _log_sigmoid(x):
    return jnp.minimum(x, 0.0) - jnp.log(1.0 + jnp.exp(-jnp.abs(x)))


def _fgate_fwd(f_t, b_col, *, tb=256):
    h, s = f_t.shape
    tb = min(tb, s)

    def body(f_ref, b_ref, c_ref, carry):
        @pl.when(pl.program_id(0) == 0)
        def _():
            carry[...] = jnp.zeros_like(carry)

        lf = _log_sigmoid(f_ref[...] + b_ref[...])
        shp = (tb, tb)
        upper = (lax.broadcasted_iota(jnp.int32, shp, 0) <= lax.broadcasted_iota(jnp.int32, shp, 1)).astype(BF16)
        c_ref[...] = _dot3(lf, upper) + carry[...]
        carry[...] += jnp.sum(lf, axis=1, keepdims=True)

    return pl.pallas_call(
        body, grid=(s // tb,),
        in_specs=[pl.BlockSpec((h, tb), lambda i: (0, i)), pl.BlockSpec((h, 1), lambda i: (0, 0))],
        out_specs=pl.BlockSpec((h, tb), lambda i: (0, i)), out_shape=jax.ShapeDtypeStruct((h, s), F32),
        scratch_shapes=[pltpu.VMEM((h, 1), F32)], name="fgate_fwd", compiler_params=_params(("arbitrary",)),
    )(f_t, b_col)


def _fgate_bwd(f_t, b_col, dc_a, dc_b, *, tb=256):
    h, s = f_t.shape
    tb = min(tb, s)
    n = s // tb

    def body(f_ref, b_ref, da_ref, db_ref, df_ref, dbias_ref, carry):
        @pl.when(pl.program_id(0) == 0)
        def _():
            carry[...] = jnp.zeros_like(carry)
            dbias_ref[...] = jnp.zeros_like(dbias_ref)

        dc = da_ref[...] + db_ref[...]
        shp = (tb, tb)
        lower = (lax.broadcasted_iota(jnp.int32, shp, 0) >= lax.broadcasted_iota(jnp.int32, shp, 1)).astype(BF16)
        d_lf = _dot3(dc, lower) + carry[...]
        carry[...] += jnp.sum(dc, axis=1, keepdims=True)
        df = d_lf * (1.0 - _sigmoid(f_ref[...] + b_ref[...]))
        df_ref[...] = df
        dbias_ref[...] += jnp.sum(df, axis=1, keepdims=True)

    blk = pl.BlockSpec((h, tb), lambda i: (0, n - 1 - i))
    return pl.pallas_call(
        body, grid=(n,), in_specs=[blk, pl.BlockSpec((h, 1), lambda i: (0, 0)), blk, blk],
        out_specs=[blk, pl.BlockSpec((h, 1), lambda i: (0, 0))],
        out_shape=[jax.ShapeDtypeStruct((h, s), F32), jax.ShapeDtypeStruct((h, 1), F32)],
        scratch_shapes=[pltpu.VMEM((h, 1), F32)], name="fgate_bwd", compiler_params=_params(("arbitrary",)),
    )(f_t, b_col, dc_a, dc_b)


_NT = (((1,), (1,)), ((), ()))


def _causal(shape, row0, col0, transposed):
    r = lax.broadcasted_iota(jnp.int32, shape, 0) + row0
    c = lax.broadcasted_iota(jnp.int32, shape, 1) + col0
    return (r <= c) if transposed else (c <= r)


_ATT_SCALE = B_HEAD_DIM ** -0.5
ATT_W = 128
_COL_CQ, _COL_CK, _COL_LSE, _COL_DELTA = 64, 67, 70, 64
_ATT_BLK = 2048
_ATT_SUB = 512


def _split3(x):
    h = x.astype(BF16).astype(F32)
    r = x - h
    m = r.astype(BF16).astype(F32)
    return h, m, (r - m).astype(BF16).astype(F32)


def _put_cols(base, lane, col0, parts):
    for t, pv in enumerate(parts):
        base = jnp.where(lane == col0 + t, pv, base)
    return base


_HEAD_PAIRS = B_HEADS // 2
_Q_BLOCK, _K_BLOCK, _V_BLOCK = 8, 12, 16
_F_BLOCK = 20
_O_BLOCK = 4
_ANY = pl.BlockSpec(memory_space=pl.ANY)


def _head_half(ref, h):
    x = ref[...]
    x = jnp.where((h & 1) == 1, pltpu.roll(x, B_HEAD_DIM, 1), x)
    return jnp.where(lax.broadcasted_iota(jnp.int32, x.shape, 1) < B_HEAD_DIM, x, 0.0)


def _pair(a, b):
    lane = lax.broadcasted_iota(jnp.int32, a.shape, 1)
    return jnp.where(lane < B_HEAD_DIM, a, pltpu.roll(b, B_HEAD_DIM, 1))


def _attn_prep(z, c_col, gq, gk, *, tr=1024):
    s = z.shape[0]
    tr = min(tr, s)
    n = s // tr

    def body(q_ref, k_ref, v_ref, c_ref, gq_ref, gk_ref, qo_ref, ko_ref, vo_ref):
        h = pl.program_id(0)
        q, k, v = _head_half(q_ref, h), _head_half(k_ref, h), _head_half(v_ref, h)
        lane = lax.broadcasted_iota(jnp.int32, q.shape, 1)
        rq = lax.rsqrt(jnp.sum(q * q, axis=-1, keepdims=True) * (1.0 / B_HEAD_DIM) + EPS)
        rk = lax.rsqrt(jnp.sum(k * k, axis=-1, keepdims=True) * (1.0 / B_HEAD_DIM) + EPS)
        c = c_ref[...]
        ch, cm, cl = _split3(c)
        one = jnp.ones_like(c)
        qq = _put_cols(q * rq * gq_ref[...] * _ATT_SCALE, lane, _COL_CQ, (ch, cm, cl))
        qq = _put_cols(qq, lane, _COL_CK, (one, one, one))
        kk = _put_cols(k * rk * gk_ref[...], lane, _COL_CQ, (one, one, one))
        kk = _put_cols(kk, lane, _COL_CK, (-ch, -cm, -cl))
        kk = _put_cols(kk, lane, _COL_LSE, (one, one, one))
        qo_ref[...] = qq.astype(BF16)
        ko_ref[...] = kk.astype(BF16)
        vo_ref[...] = _put_cols(v, lane, _COL_DELTA, (one, one, one)).astype(BF16)

    def zcol(base):
        return pl.BlockSpec((tr, ATT_W), lambda h, i: (i, base + (h >> 1)))

    rows = pl.BlockSpec((tr, ATT_W), lambda h, i: (h * n + i, 0))
    gain = pl.BlockSpec((1, ATT_W), lambda h, i: (0, 0))
    return pl.pallas_call(
        body, grid=(B_HEADS, n),
        in_specs=[zcol(_Q_BLOCK), zcol(_K_BLOCK), zcol(_V_BLOCK), pl.BlockSpec((tr, 1), lambda h, i: (h * n + i, 0)),
                  gain, gain],
        out_specs=[rows] * 3, out_shape=[jax.ShapeDtypeStruct((B_HEADS * s, ATT_W), BF16)] * 3, name="attn_prep",
        compiler_params=_params(("parallel", "parallel")),
    )(z, z, z, c_col, gq, gk)


def _attn_bwd_prep(d_ymix, o, qp, lse, *, tr=1024):
    s = d_ymix.shape[0]
    tr = min(tr, s)
    n = s // tr

    def body(do_ref, o_ref, q_ref, l_ref, dop_ref, qpp_ref):
        dov = _head_half(do_ref, pl.program_id(0))
        lane = lax.broadcasted_iota(jnp.int32, dov.shape, 1)
        dh, dm, dl = _split3(jnp.sum(dov * o_ref[...], axis=-1, keepdims=True))
        lh, lm, ll = _split3(l_ref[...])
        dop_ref[...] = _put_cols(dov, lane, _COL_DELTA, (-dh, -dm, -dl)).astype(BF16)
        qpp_ref[...] = _put_cols(q_ref[...].astype(F32), lane, _COL_LSE, (-lh, -lm, -ll)).astype(BF16)

    rows = pl.BlockSpec((tr, ATT_W), lambda h, i: (h * n + i, 0))
    return pl.pallas_call(
        body, grid=(B_HEADS, n),
        in_specs=[pl.BlockSpec((tr, ATT_W), lambda h, i: (i, _O_BLOCK + (h >> 1))), rows, rows,
                  pl.BlockSpec((tr, 1), lambda h, i: (h * n + i, 0))],
        out_specs=[rows] * 2, out_shape=[jax.ShapeDtypeStruct((B_HEADS * s, ATT_W), BF16)] * 2,
        name="attn_bwd_prep", compiler_params=_params(("parallel", "parallel")),
    )(d_ymix, o, qp, lse)


def _attn_merge_out(o, ymix, *, tr=1024):
    s = o.shape[1]
    tr = min(tr, s)

    def body(a_ref, b_ref, buf_ref, out_ref):
        out_ref[...] = _pair(a_ref[0], b_ref[0]).astype(BF16)

    return pl.pallas_call(
        body, grid=(_HEAD_PAIRS, s // tr),
        in_specs=[pl.BlockSpec((1, tr, ATT_W), lambda j, i: (2 * j, i, 0)),
                  pl.BlockSpec((1, tr, ATT_W), lambda j, i: (2 * j + 1, i, 0)), _ANY],
        out_specs=pl.BlockSpec((tr, ATT_W), lambda j, i: (i, _O_BLOCK + j)),
        out_shape=jax.ShapeDtypeStruct(ymix.shape, ymix.dtype), input_output_aliases={2: 0}, name="attn_merge_out",
        compiler_params=_params(("parallel", "parallel")),
    )(o, o, ymix)


def _pair_bwd(d_heads, dz, out_block, *, z=None, in_block=None, gain=None, scale=None, name, tr=1024):
    s = d_heads.shape[1]
    tr = min(tr, s)
    norm = gain is not None

    def body(*refs):
        a_ref, b_ref = refs[0], refs[1]
        d = _pair(a_ref[0], b_ref[0])
        if norm:
            x_ref, g_ref, _, out_ref, dg_ref = refs[2:]

            @pl.when((pl.program_id(0) == 0) & (pl.program_id(1) == 0))
            def _():
                dg_ref[...] = jnp.zeros_like(dg_ref)

            x = x_ref[...]
            lo = lax.broadcasted_iota(jnp.int32, x.shape, 1) < B_HEAD_DIM

            def half_mean(v):
                s_lo = jnp.sum(jnp.where(lo, v, 0.0), axis=-1, keepdims=True)
                s_hi = jnp.sum(jnp.where(lo, 0.0, v), axis=-1, keepdims=True)
                return jnp.where(lo, s_lo, s_hi) * (1.0 / B_HEAD_DIM)

            r = lax.rsqrt(half_mean(x * x) + EPS)
            xh = x * r
            dy = d * scale if scale is not None else d
            dyg = dy * g_ref[...]
            d = r * (dyg - xh * half_mean(dyg * xh))
            dg_ref[...] += jnp.sum(dy * xh, axis=0, keepdims=True)
        else:
            out_ref = refs[3]
        out_ref[...] = d.astype(BF16)

    heads = [pl.BlockSpec((1, tr, ATT_W), lambda j, i: (2 * j, i, 0)),
             pl.BlockSpec((1, tr, ATT_W), lambda j, i: (2 * j + 1, i, 0))]
    out_spec = pl.BlockSpec((tr, ATT_W), lambda j, i: (i, out_block + j))
    dz_shape = jax.ShapeDtypeStruct(dz.shape, dz.dtype)
    if norm:
        res = pl.pallas_call(
            body, grid=(_HEAD_PAIRS, s // tr),
            in_specs=heads + [pl.BlockSpec((tr, ATT_W), lambda j, i: (i, in_block + j)),
                              pl.BlockSpec((1, ATT_W), lambda j, i: (0, 0)), _ANY],
            out_specs=[out_spec, pl.BlockSpec((1, ATT_W), lambda j, i: (0, 0))],
            out_shape=[dz_shape, jax.ShapeDtypeStruct((1, ATT_W), F32)], input_output_aliases={4: 0}, name=name,
            compiler_params=_params(("arbitrary", "arbitrary")),
        )(d_heads, d_heads, z, gain, dz)
        return res[0], res[1]
    return pl.pallas_call(
        body, grid=(_HEAD_PAIRS, s // tr), in_specs=heads + [_ANY], out_specs=out_spec, out_shape=dz_shape,
        input_output_aliases={2: 0}, name=name, compiler_params=_params(("parallel", "parallel")),
    )(d_heads, d_heads, dz), None


def _attn_fwd(qp, kp, vp, *, blk=_ATT_BLK):
    h, s, d = qp.shape
    b = min(blk, s)
    n = s // b
    sub = min(_ATT_SUB, b)
    ns = b // sub

    def body(q_ref, k_ref, v_ref, o_ref, lse_ref):
        i = pl.program_id(1)
        qs = [q_ref[0, t * sub:(t + 1) * sub, :] for t in range(ns)]

        def tile(j, state, masked):
            off = pl.multiple_of(j * b, b)
            kk = k_ref[0, pl.ds(off, b), :]
            vv = v_ref[0, pl.ds(off, b), :]
            new = []
            for t in range(ns):
                m, l, acc = state[3 * t:3 * t + 3]
                nk = (t + 1) * sub if masked else b
                sc = lax.dot_general(qs[t], kk[:nk], _NT, preferred_element_type=F32)
                if masked:
                    sc = jnp.where(_causal((sub, nk), t * sub, 0, False), sc, NEG_INF)
                m_new = jnp.maximum(m, jnp.max(sc, axis=1, keepdims=True))
                alpha = jnp.exp(m - m_new)
                p = jnp.exp(sc - m_new)
                new += [m_new, alpha * l + jnp.sum(p, axis=1, keepdims=True),
                        alpha * acc + jnp.dot(p.astype(BF16), vv[:nk], preferred_element_type=F32)]
            return tuple(new)

        init = (jnp.full((sub, 1), NEG_INF, F32), jnp.zeros((sub, 1), F32), jnp.zeros((sub, d), F32)) * ns
        state = lax.fori_loop(0, i, lambda j, st: tile(j, st, False), init)
        state = tile(i, state, True)
        for t in range(ns):
            m, l, acc = state[3 * t:3 * t + 3]
            o_ref[0, t * sub:(t + 1) * sub, :] = acc / l
            lse_ref[0, t * sub:(t + 1) * sub, :] = m + jnp.log(l)

    blk_spec = pl.BlockSpec((1, b, d), lambda hh, i: (hh, i, 0))
    full = pl.BlockSpec((1, s, d), lambda hh, i: (hh, 0, 0))
    return pl.pallas_call(
        body, grid=(h, n), in_specs=[blk_spec, full, full],
        out_specs=[blk_spec, pl.BlockSpec((1, b, 1), lambda hh, i: (hh, i, 0))],
        out_shape=[jax.ShapeDtypeStruct((h, s, d), F32), jax.ShapeDtypeStruct((h, s, 1), F32)],
        name="attn_fwd", compiler_params=_params(("parallel", "arbitrary")),
    )(qp, kp, vp)


def _attn_bwd_dq(qpp, kp, vp, dop, *, blk=_ATT_BLK):
    h, s, d = qpp.shape
    b = min(blk, s)
    n = s // b
    sub = min(_ATT_SUB, b)
    ns = b // sub

    def body(q_ref, do_ref, k_ref, v_ref, dq_ref):
        i = pl.program_id(1)
        qs = [q_ref[0, t * sub:(t + 1) * sub, :] for t in range(ns)]
        dos = [do_ref[0, t * sub:(t + 1) * sub, :] for t in range(ns)]

        def tile(j, state, masked):
            off = pl.multiple_of(j * b, b)
            kk = k_ref[0, pl.ds(off, b), :]
            vv = v_ref[0, pl.ds(off, b), :]
            new = []
            for t in range(ns):
                nk = (t + 1) * sub if masked else b
                p = jnp.exp(lax.dot_general(qs[t], kk[:nk], _NT, preferred_element_type=F32))
                if masked:
                    p = jnp.where(_causal((sub, nk), t * sub, 0, False), p, 0.0)
                ds = p * lax.dot_general(dos[t], vv[:nk], _NT, preferred_element_type=F32)
                new.append(state[t] + jnp.dot(ds.astype(BF16), kk[:nk], preferred_element_type=F32))
            return tuple(new)

        state = lax.fori_loop(0, i, lambda j, st: tile(j, st, False), (jnp.zeros((sub, d), F32),) * ns)
        state = tile(i, state, True)
        for t in range(ns):
            dq_ref[0, t * sub:(t + 1) * sub, :] = state[t]

    blk_spec = pl.BlockSpec((1, b, d), lambda hh, i: (hh, i, 0))
    full = pl.BlockSpec((1, s, d), lambda hh, i: (hh, 0, 0))
    return pl.pallas_call(
        body, grid=(h, n), in_specs=[blk_spec, blk_spec, full, full], out_specs=blk_spec,
        out_shape=jax.ShapeDtypeStruct((h, s, d), F32), name="attn_bwd_dq",
        compiler_params=_params(("parallel", "arbitrary")),
    )(qpp, dop, kp, vp)


def _attn_bwd_dkv(qpp, kp, vp, dop, *, blk=_ATT_BLK):
    h, s, d = qpp.shape
    b = min(blk, s)
    n = s // b
    sub = min(_ATT_SUB, b)
    ns = b // sub

    def body(k_ref, v_ref, q_ref, do_ref, dk_ref, dv_ref):
        j = pl.program_id(1)
        ks = [k_ref[0, t * sub:(t + 1) * sub, :] for t in range(ns)]
        vs = [v_ref[0, t * sub:(t + 1) * sub, :] for t in range(ns)]

        def tile(i, state, masked):
            off = pl.multiple_of(i * b, b)
            qq = q_ref[0, pl.ds(off, b), :]
            dd = do_ref[0, pl.ds(off, b), :]
            new = []
            for t in range(ns):
                q0 = t * sub if masked else 0
                qt, dt = qq[q0:], dd[q0:]
                pt = jnp.exp(lax.dot_general(ks[t], qt, _NT, preferred_element_type=F32))
                if masked:
                    pt = jnp.where(_causal((sub, b - q0), q0, q0, True), pt, 0.0)
                dst = pt * lax.dot_general(vs[t], dt, _NT, preferred_element_type=F32)
                new += [state[2 * t] + jnp.dot(dst.astype(BF16), qt, preferred_element_type=F32),
                        state[2 * t + 1] + jnp.dot(pt.astype(BF16), dt, preferred_element_type=F32)]
            return tuple(new)

        state = tile(j, (jnp.zeros((sub, d), F32),) * (2 * ns), True)
        state = lax.fori_loop(j + 1, n, lambda i, st: tile(i, st, False), state)
        for t in range(ns):
            dk_ref[0, t * sub:(t + 1) * sub, :] = state[2 * t]
            dv_ref[0, t * sub:(t + 1) * sub, :] = state[2 * t + 1]

    blk_spec = pl.BlockSpec((1, b, d), lambda hh, j: (hh, j, 0))
    full = pl.BlockSpec((1, s, d), lambda hh, j: (hh, 0, 0))
    return pl.pallas_call(
        body, grid=(h, n), in_specs=[blk_spec, blk_spec, full, full], out_specs=[blk_spec, blk_spec],
        out_shape=[jax.ShapeDtypeStruct((h, s, d), F32)] * 2, name="attn_bwd_dkv",
        compiler_params=_params(("parallel", "arbitrary")),
    )(kp, vp, qpp, dop)


def _pad_head(g):
    return jnp.pad(g, ((0, 0), (0, ATT_W - B_HEAD_DIM)))


def _even_fwd(x0, w, e, i):
    s = x0.shape[0]
    hs = B_HEADS * s
    hb = _rms_fwd(x0, w["norm_mix"][i:i + 1], name="mix_norm")
    z = _mm(hb, w["ev_w_in"][e], tb=True, name="ev_in", bn=896)
    b_s = w["ev_b_spatial"][e][:, :, None]
    v_gain = w["ev_v_norm"][e:e + 1]
    ymix = _gmlp_fwd(z, w["ev_w_spatial"][e], b_s, v_gain)
    f_t = z[:, IN_COLS - B_HEADS:IN_COLS].T
    c = _fgate_fwd(f_t, w["ev_b_fgate"][e][:, None])
    qp, kp, vp = _attn_prep(z, c.reshape(hs, 1), _pad_head(w["ev_q_norm"][e:e + 1]),
                            _pad_head(w["ev_k_norm"][e:e + 1]))
    shp = (B_HEADS, s, ATT_W)
    o, lse = _attn_fwd(qp.reshape(shp), kp.reshape(shp), vp.reshape(shp))
    ymix = _attn_merge_out(o, ymix)
    x1 = _mm(ymix, w["ev_w_out"][e], add=x0, name="ev_out")
    return x1, (hb, z, ymix, qp, kp, vp, f_t, o, lse)


def _even_bwd(dx1, dx1b, x0, saved, w, e, i):
    hb, z, ymix, qp, kp, vp, f_t, o, lse = saved
    s = x0.shape[0]
    hs = B_HEADS * s
    d_ymix = _mm(dx1b, w["ev_w_out"][e], tb=True, name="ev_out_dx")
    d_wout = _mm(ymix, dx1b, ta=True, name="ev_out_dw")
    dop, qpp = _attn_bwd_prep(d_ymix, o.reshape(hs, ATT_W), qp, lse.reshape(hs, 1))
    shp = (B_HEADS, s, ATT_W)
    qpp, dop, kp3, vp3 = qpp.reshape(shp), dop.reshape(shp), kp.reshape(shp), vp.reshape(shp)
    dqp = _attn_bwd_dq(qpp, kp3, vp3, dop)
    dkp, dvp = _attn_bwd_dkv(qpp, kp3, vp3, dop)
    d_ft, d_bf = _fgate_bwd(f_t, w["ev_b_fgate"][e][:, None], dqp[:, :, _COL_CQ], -dkp[:, :, _COL_CK])
    dz = jnp.pad(d_ft.T.astype(BF16), ((0, 0), (_F_BLOCK * ATT_W, IN_COLS_PAD - IN_COLS)))
    b_s = w["ev_b_spatial"][e][:, :, None]
    v_gain = w["ev_v_norm"][e:e + 1]
    dz, d_ws, d_bs, d_vg = _gmlp_bwd(z, d_ymix, w["ev_w_spatial"][e], b_s, v_gain, dz)
    twice = lambda g: jnp.concatenate([g, g], axis=1)
    dz, d_gq = _pair_bwd(dqp, dz, _Q_BLOCK, z=z, in_block=_Q_BLOCK, gain=twice(w["ev_q_norm"][e:e + 1]),
                         scale=_ATT_SCALE, name="attn_qnorm_bwd")
    dz, d_gk = _pair_bwd(dkp, dz, _K_BLOCK, z=z, in_block=_K_BLOCK, gain=twice(w["ev_k_norm"][e:e + 1]),
                         name="attn_knorm_bwd")
    dz, _ = _pair_bwd(dvp, dz, _V_BLOCK, name="attn_dv_out")
    d_h = _mm(dz, w["ev_w_in"][e], name="ev_in_dx")
    d_win_t = _mm(hb, dz, ta=True, name="ev_in_dw", bn=1344)[:, :IN_COLS].T
    dx0, d_nm = _rms_bwd_call(x0, d_h, w["norm_mix"][i:i + 1], dx1, name="mix_norm_bwd")
    hd = B_HEAD_DIM
    grads = {"ev_w_in": d_win_t, "ev_w_out": d_wout, "ev_b_fgate": d_bf[:, 0],
             "ev_q_norm": d_gq[0, :hd] + d_gq[0, hd:], "ev_k_norm": d_gk[0, :hd] + d_gk[0, hd:], "ev_v_norm": d_vg[0],
             "ev_w_spatial": d_ws, "ev_b_spatial": d_bs[:, :, 0], "norm_mix": d_nm[0]}
    return dx0, grads


def _s5_disc(a_re, a_im, log_dt, b_re, b_im):
    dt = jnp.exp(log_dt)[:, None]
    lr, li = a_re, a_im
    mag = jnp.exp(lr * dt)
    ab_re, ab_im = mag * jnp.cos(li * dt), mag * jnp.sin(li * dt)
    den = lr * lr + li * li
    nr, ni = ab_re - 1.0, ab_im
    cr = (nr * lr + ni * li) / den
    ci = (ni * lr - nr * li) / den
    bb_re = cr[..., None] * b_re - ci[..., None] * b_im
    bb_im = cr[..., None] * b_im + ci[..., None] * b_re
    return ab_re, ab_im, bb_re, bb_im


_GPB = S5_GROUPS // S5_BLOCKS


def _blockdiag(t):
    a, b = t.shape[1:]
    eye = jnp.eye(_GPB, dtype=t.dtype)
    t = t.reshape(S5_BLOCKS, _GPB, a, 1, b) * eye[None, :, None, :, None]
    return t.reshape(S5_BLOCKS, _GPB * a, _GPB * b)


def _blockdiag_extract(m, a, b):
    eye = jnp.eye(_GPB, dtype=m.dtype)
    m = m.reshape(S5_BLOCKS, _GPB, a, _GPB, b) * eye[None, :, None, :, None]
    return jnp.sum(m, axis=3).reshape(S5_GROUPS, a, b)


_SCAN_ROWS = 8


def _cmul(a_r, a_i, b_r, b_i):
    return a_r * b_r - a_i * b_i, a_r * b_i + a_i * b_r


def _scan_tiles(yr_ref, yi_ref, a_r, a_i, c_r, c_i, *, rev, x_refs=None):
    tb, wl = yr_ref.shape
    ntile = tb // _SCAN_ROWS
    with_acc = x_refs is not None
    if rev:
        a_i = -a_i
    pw = [(a_r, a_i)]
    for _ in range(_SCAN_ROWS - 1):
        pw.append(_cmul(*pw[-1], a_r, a_i))
    sub = lax.broadcasted_iota(jnp.int32, (_SCAN_ROWS, wl), 0)
    dist = (_SCAN_ROWS - 1 - sub) if rev else sub
    zero = jnp.zeros((_SCAN_ROWS, wl), F32)
    steps = []
    for kk in (1, 2, 4):
        steps.append(((_SCAN_ROWS - kk) if rev else kk, jnp.where(dist >= kk, pw[kk - 1][0], zero),
                      jnp.where(dist >= kk, pw[kk - 1][1], zero)))
    e_r, e_i = zero, zero
    for d in range(_SCAN_ROWS):
        e_r = jnp.where(dist == d, pw[d][0], e_r)
        e_i = jnp.where(dist == d, pw[d][1], e_i)
    exit_row = 0 if rev else _SCAN_ROWS - 1

    def tile(q, carry):
        idx = (ntile - 1 - q) if rev else q
        rows = pl.ds(pl.multiple_of(idx * _SCAN_ROWS, _SCAN_ROWS), _SCAN_ROWS)
        y_r, y_i = yr_ref[rows, :], yi_ref[rows, :]
        for sh, k_r, k_i in steps:
            t_r, t_i = _cmul(k_r, k_i, pltpu.roll(y_r, sh, 0), pltpu.roll(y_i, sh, 0))
            y_r, y_i = y_r + t_r, y_i + t_i
        cb_r = jnp.broadcast_to(carry[0], (_SCAN_ROWS, wl))
        cb_i = jnp.broadcast_to(carry[1], (_SCAN_ROWS, wl))
        t_r, t_i = _cmul(e_r, e_i, cb_r, cb_i)
        y_r, y_i = y_r + t_r, y_i + t_i
        yr_ref[rows, :] = y_r
        yi_ref[rows, :] = y_i
        out = (y_r[exit_row:exit_row + 1], y_i[exit_row:exit_row + 1])
        if with_acc:
            n_r = jnp.where(sub == _SCAN_ROWS - 1, cb_r, pltpu.roll(y_r, _SCAN_ROWS - 1, 0))
            n_i = jnp.where(sub == _SCAN_ROWS - 1, cb_i, pltpu.roll(y_i, _SCAN_ROWS - 1, 0))
            s_r, s_i = x_refs[0][rows, :], x_refs[1][rows, :]
            out += (carry[2] + (n_r * s_r + n_i * s_i), carry[3] + (n_i * s_r - n_r * s_i))
        return out

    return lax.fori_loop(0, ntile, tile, (c_r, c_i) + ((zero, zero) if with_acc else ()), unroll=2)


_S5_WL = S5_N // S5_BLOCKS
_S5_CW = D_MODEL // S5_BLOCKS
_TN = (((0,), (0,)), ((), ()))


def _s5_fwd(hb, mats, *, tb=1024):
    s = hb.shape[0]
    tb = min(tb, s)

    def body(h_ref, wbr_ref, wbi_ref, wcr_ref, wci_ref, ar_ref, ai_ref, xr_ref, xi_ref, cp_ref, cr, ci):
        @pl.when(pl.program_id(1) == 0)
        def _():
            cr[...] = jnp.zeros_like(cr)
            ci[...] = jnp.zeros_like(ci)

        hv = h_ref[...]
        xr_ref[...] = jnp.dot(hv, wbr_ref[0], preferred_element_type=F32)
        xi_ref[...] = jnp.dot(hv, wbi_ref[0], preferred_element_type=F32)
        cr[...], ci[...] = _scan_tiles(xr_ref, xi_ref, ar_ref[...], ai_ref[...], cr[...], ci[...], rev=False)
        cp_ref[...] = (jnp.dot(xr_ref[...].astype(BF16), wcr_ref[0], preferred_element_type=F32)
                       + jnp.dot(xi_ref[...].astype(BF16), wci_ref[0], preferred_element_type=F32))

    wide = pl.BlockSpec((tb, _S5_WL), lambda c, t: (t, c))
    narrow = pl.BlockSpec((tb, _S5_CW), lambda c, t: (t, c))
    w_in = pl.BlockSpec((1, _S5_CW, _S5_WL), lambda c, t: (c, 0, 0))
    w_out = pl.BlockSpec((1, _S5_WL, _S5_CW), lambda c, t: (c, 0, 0))
    vec = pl.BlockSpec((1, _S5_WL), lambda c, t: (0, c))
    return pl.pallas_call(
        body, grid=(S5_BLOCKS, s // tb), in_specs=[narrow, w_in, w_in, w_out, w_out, vec, vec],
        out_specs=[wide, wide, narrow],
        out_shape=[jax.ShapeDtypeStruct((s, S5_N), F32)] * 2 + [jax.ShapeDtypeStruct((s, D_MODEL), F32)],
        scratch_shapes=[pltpu.VMEM((1, _S5_WL), F32)] * 2, name="s5_fwd",
        compiler_params=_params(("parallel", "arbitrary")),
    )(hb, mats["wb_re"], mats["wb_im"], mats["wc_re"], mats["wc_imn"], mats["ar"], mats["ai"])


def _s5_bwd(dyb, hb, xr, xi, dhd, mats, *, tb=1024):
    s = hb.shape[0]
    tb = min(tb, s)
    nt = s // tb

    def body(dy_ref, h_ref, xr_ref, xi_ref, dhd_ref, wctr_ref, wcti_ref, wbtr_ref, wbti_ref, ar_ref, ai_ref,
             dh_ref, dar_ref, dai_ref, dwbr_ref, dwbi_ref, dwcr_ref, dwci_ref, lr, li, cr, ci):
        @pl.when(pl.program_id(1) == 0)
        def _():
            for r in (cr, ci, dar_ref, dai_ref, dwbr_ref, dwbi_ref, dwcr_ref, dwci_ref):
                r[...] = jnp.zeros_like(r)

        dyv, hv = dy_ref[...], h_ref[...]
        lr[...] = jnp.dot(dyv, wctr_ref[0], preferred_element_type=F32)
        li[...] = jnp.dot(dyv, wcti_ref[0], preferred_element_type=F32)
        cr[...], ci[...], acc_r, acc_i = _scan_tiles(lr, li, ar_ref[...], ai_ref[...], cr[...], ci[...], rev=True,
                                                     x_refs=(xr_ref, xi_ref))
        dar_ref[...] += jnp.sum(acc_r, axis=0, keepdims=True)
        dai_ref[...] += jnp.sum(acc_i, axis=0, keepdims=True)
        lrb, lib = lr[...].astype(BF16), li[...].astype(BF16)
        dh_ref[...] = (jnp.dot(lrb, wbtr_ref[0], preferred_element_type=F32) + dhd_ref[...]
                       + jnp.dot(lib, wbti_ref[0], preferred_element_type=F32))
        dwbr_ref[0] += lax.dot_general(hv, lrb, _TN, preferred_element_type=F32)
        dwbi_ref[0] += lax.dot_general(hv, lib, _TN, preferred_element_type=F32)
        dwcr_ref[0] += lax.dot_general(dyv, xr_ref[...].astype(BF16), _TN, preferred_element_type=F32)
        dwci_ref[0] += lax.dot_general(dyv, xi_ref[...].astype(BF16), _TN, preferred_element_type=F32)

    wide = pl.BlockSpec((tb, _S5_WL), lambda c, t: (nt - 1 - t, c))
    narrow = pl.BlockSpec((tb, _S5_CW), lambda c, t: (nt - 1 - t, c))
    w_in = pl.BlockSpec((1, _S5_CW, _S5_WL), lambda c, t: (c, 0, 0))
    w_out = pl.BlockSpec((1, _S5_WL, _S5_CW), lambda c, t: (c, 0, 0))
    vec = pl.BlockSpec((1, _S5_WL), lambda c, t: (0, c))
    acc_shape = jax.ShapeDtypeStruct((S5_BLOCKS, _S5_CW, _S5_WL), F32)
    return pl.pallas_call(
        body, grid=(S5_BLOCKS, nt), in_specs=[narrow, narrow, wide, wide, narrow, w_in, w_in, w_out, w_out, vec, vec],
        out_specs=[narrow, vec, vec, w_in, w_in, w_in, w_in],
        out_shape=[jax.ShapeDtypeStruct((s, D_MODEL), F32)] + [jax.ShapeDtypeStruct((1, S5_N), F32)] * 2
        + [acc_shape] * 4,
        scratch_shapes=[pltpu.VMEM((tb, _S5_WL), F32)] * 2 + [pltpu.VMEM((1, _S5_WL), F32)] * 2, name="s5_bwd",
        compiler_params=_params(("parallel", "arbitrary")),
    )(dyb, hb, xr, xi, dhd, mats["wct_re"], mats["wct_imn"], mats["wbt_re"], mats["wbt_im"], mats["ar"], mats["ai"])


def _s5_mats(w, o):
    ab_re, ab_im, bb_re, bb_im = _s5_disc(w["od_a_re"][o], w["od_a_im"][o], w["od_log_dt"][o], w["od_b_re"][o],
                                          w["od_b_im"][o])
    c_re, c_im = w["od_c_re"][o], w["od_c_im"][o]
    tr = lambda t: t.transpose(0, 2, 1)
    bd = lambda t: _blockdiag(t).astype(BF16)
    return {
        "ar": ab_re.reshape(1, S5_N), "ai": ab_im.reshape(1, S5_N),
        "wb_re": bd(tr(bb_re)), "wb_im": bd(tr(bb_im)), "wc_re": bd(tr(c_re)), "wc_imn": bd(-tr(c_im)),
        "wct_re": bd(c_re), "wct_imn": bd(-c_im), "wbt_re": bd(bb_re), "wbt_im": bd(bb_im),
    }


def _odd_fwd(x0, w, o, i):
    mats = _s5_mats(w, o)
    nm = w["norm_mix"][i:i + 1]
    d_row = w["od_d"][o:o + 1]
    hb = _rms_fwd(x0, nm, name="mix_norm")
    xr, xi, cp = _s5_fwd(hb, mats)

    def fn(xv, cpv, gv, dv):
        y = cpv + dv * (xv * _rstd(xv) * gv)
        return y, _gelu(y)

    y, gy = _rw(fn, [x0, cp], [nm, d_row], [(D_MODEL, F32), (D_MODEL, BF16)], tr=512, name="s5_out")
    gg = _mm(gy, w["od_w_glu"][o], tb=True, name="od_glu")
    x1 = _rw(lambda xv, g: xv + g[:, :D_MODEL] * _sigmoid(g[:, D_MODEL:]), [x0, gg], [], [(D_MODEL, F32)], tr=512,
             name="od_glu_out")[0]
    return x1, (hb, xr, xi, y, gy, gg)


def _odd_bwd(dx1, x0, saved, w, o, i):
    hb, xr, xi, y, gy, gg = saved
    mats = _s5_mats(w, o)
    nm = w["norm_mix"][i:i + 1]
    d_row = w["od_d"][o:o + 1]

    def fn_glu(dv, g):
        ga, gb = g[:, :D_MODEL], g[:, D_MODEL:]
        sg = _sigmoid(gb)
        return jnp.concatenate([dv * sg, dv * ga * (sg * (1.0 - sg))], axis=1)

    dgg = _rw(fn_glu, [dx1, gg], [], [(2 * D_MODEL, BF16)], tr=512, name="od_glu_out_bwd")[0]
    d_wglu = _mm(gy, dgg, ta=True, name="od_glu_dw", bn=1024).T
    d_gy = _mm(dgg, w["od_w_glu"][o], name="od_glu_dx")

    def fn_y(dg, yv, xv, gv, dv):
        dy = dg * _gelu_grad(yv)
        h = xv * _rstd(xv) * gv
        return dy, dv * dy, jnp.sum(dy * h, axis=0, keepdims=True)

    dyb, dhd, d_d = _rw(fn_y, [d_gy, y, x0], [nm, d_row], [(D_MODEL, BF16), (D_MODEL, F32)], [(1, D_MODEL)], tr=512,
                        name="s5_out_bwd")
    dh, d_ar, d_ai, d_wb_re, d_wb_im, d_wc_re, d_wc_im = _s5_bwd(dyb, hb, xr, xi, dhd, mats)
    gc, gp = S5_GROUP_CH, S5_STATE
    d_c_re = _blockdiag_extract(d_wc_re, gc, gp)
    d_c_im = -_blockdiag_extract(d_wc_im, gc, gp)
    d_bb_re = _blockdiag_extract(d_wb_re, gc, gp).transpose(0, 2, 1)
    d_bb_im = _blockdiag_extract(d_wb_im, gc, gp).transpose(0, 2, 1)
    dx0, d_nm = _rms_bwd_call(x0, dh, nm, dx1, name="mix_norm_bwd")
    _, vjp = jax.vjp(_s5_disc, w["od_a_re"][o], w["od_a_im"][o], w["od_log_dt"][o], w["od_b_re"][o], w["od_b_im"][o])
    d_a_re, d_a_im, d_log_dt, d_b_re, d_b_im = vjp(
        (d_ar.reshape(S5_GROUPS, S5_STATE), d_ai.reshape(S5_GROUPS, S5_STATE), d_bb_re, d_bb_im))
    grads = {"od_a_re": d_a_re, "od_a_im": d_a_im, "od_log_dt": d_log_dt, "od_b_re": d_b_re, "od_b_im": d_b_im,
             "od_c_re": d_c_re, "od_c_im": d_c_im, "od_d": d_d[0], "od_w_glu": d_wglu, "norm_mix": d_nm[0]}
    return dx0, grads


_HBM = pl.BlockSpec(memory_space=pltpu.HBM)


def _mesh_pos():
    return lax.axis_index("x"), lax.axis_index("y"), lax.axis_index("c")


def _slot(px, py, pc):
    return 4 * px + 2 * py + pc


def _all_gather(x, *, name):
    def body(x_ref, out_ref, send_sems, recv_sems, local_sem):
        mx, my, mc = _mesh_pos()
        me, sibling = (mx, my, mc), (mx, my, 1 - mc)
        chips = [(1 - mx, my), (mx, 1 - my), (1 - mx, 1 - my)]

        def copy(k, block, to, src=None):
            dst = out_ref.at[_slot(*block)]
            return pltpu.make_async_remote_copy(
                src_ref=dst if src is None else src, dst_ref=dst, send_sem=send_sems.at[k], recv_sem=recv_sems.at[k],
                device_id=to, device_id_type=MESH)

        mine = pltpu.make_async_copy(x_ref, out_ref.at[_slot(*me)], local_sem)
        mine.start()
        first = [copy(0, me, sibling, src=x_ref)]
        first += [copy(1 + j, me, (*chip, mc), src=x_ref) for j, chip in enumerate(chips)]
        for cp in first:
            cp.start()
        passed = [copy(4 + j, (*chip, mc), sibling) for j, chip in enumerate(chips)]
        for j, chip in enumerate(chips):
            copy(1 + j, (*chip, mc), me).wait_recv()
            passed[j].start()
        copy(0, sibling, me).wait_recv()
        for j, chip in enumerate(chips):
            copy(4 + j, (*chip, 1 - mc), me).wait_recv()
        for cp in first + passed:
            cp.wait_send()
        mine.wait()

    return pl.pallas_call(
        body, out_shape=jax.ShapeDtypeStruct((N_DEV,) + x.shape, x.dtype), in_specs=[_HBM], out_specs=_HBM,
        scratch_shapes=[pltpu.SemaphoreType.DMA((7,)), pltpu.SemaphoreType.DMA((7,)), pltpu.SemaphoreType.DMA],
        name=name,
    )(x)


def _all_to_all(x, *, name):
    def body(x_ref, out_ref, send_sems, recv_sems, local_sem):
        mx, my, mc = _mesh_pos()
        my_slot = _slot(mx, my, mc)
        mine = pltpu.make_async_copy(x_ref.at[my_slot], out_ref.at[my_slot], local_sem)
        mine.start()
        copies = []
        for k in range(1, N_DEV):
            peer = (1 - mx if k & 4 else mx, 1 - my if k & 2 else my, 1 - mc if k & 1 else mc)
            copies.append(pltpu.make_async_remote_copy(
                src_ref=x_ref.at[_slot(*peer)], dst_ref=out_ref.at[my_slot], send_sem=send_sems.at[k - 1],
                recv_sem=recv_sems.at[k - 1], device_id=peer, device_id_type=MESH))
        for cp in copies:
            cp.start()
        for cp in copies:
            cp.wait()
        mine.wait()

    return pl.pallas_call(
        body, out_shape=jax.ShapeDtypeStruct(x.shape, x.dtype), in_specs=[_HBM], out_specs=_HBM,
        scratch_shapes=[pltpu.SemaphoreType.DMA((7,)), pltpu.SemaphoreType.DMA((7,)), pltpu.SemaphoreType.DMA],
        name=name,
    )(x)


_SEMS = pltpu.SemaphoreType.DMA((N_DEV - 1,))
_SEM_SPEC = pl.BlockSpec(memory_space=pltpu.SEMAPHORE)
_SPLIT_COPY = pltpu.CompilerParams(has_side_effects=pltpu.SideEffectType.DATAFLOW_SIDE_EFFECTING)


def _scatter_copies(x_refs, land_refs, send_sems, recv_sems):
    mx, my, mc = _mesh_pos()
    my_slot = _slot(mx, my, mc)
    copies = []
    for a, (x_ref, land_ref) in enumerate(zip(x_refs, land_refs)):
        for k in range(1, N_DEV):
            peer = (1 - mx if k & 4 else mx, 1 - my if k & 2 else my, 1 - mc if k & 1 else mc)
            copies.append(pltpu.make_async_remote_copy(
                src_ref=x_ref.at[_slot(*peer)], dst_ref=land_ref.at[my_slot], send_sem=send_sems[a].at[k - 1],
                recv_sem=recv_sems[a].at[k - 1], device_id=peer, device_id_type=MESH))
    return copies


def _scatter_start(xs, carry, *, name):
    n = len(xs)

    def body(*refs):
        outs = refs[2 * n + 1:]
        for cp in _scatter_copies(refs[:n], refs[n:2 * n], outs[:n], outs[n:2 * n]):
            cp.start()

    hbm = lambda a: pltpu.HBM(a.shape, a.dtype)
    lands = [pltpu.with_memory_space_constraint(lax.empty(x.shape, x.dtype), pltpu.HBM) for x in xs]
    ins = [pltpu.with_memory_space_constraint(a, pltpu.HBM) for a in list(xs) + [carry]]
    res = pl.pallas_call(
        body, name=name, in_specs=[_HBM] * (2 * n + 1), out_specs=[_SEM_SPEC] * (2 * n) + [_HBM] * (2 * n + 1),
        out_shape=[_SEMS] * (2 * n) + [hbm(x) for x in xs] * 2 + [hbm(carry)],
        input_output_aliases={**{a: 2 * n + a for a in range(2 * n)}, 2 * n: 4 * n}, compiler_params=_SPLIT_COPY,
    )(*ins[:n], *lands, ins[n])
    return (res[:n], res[n:2 * n], res[2 * n:3 * n], res[3 * n:4 * n]), res[4 * n]


def _scatter_wait(handles, after, *, name):
    send_sems, recv_sems, xs, lands = handles
    n = len(xs)

    def body(*refs):
        for cp in _scatter_copies(refs[:n], refs[n:2 * n], refs[2 * n:3 * n], refs[3 * n:4 * n]):
            cp.wait_send()
            cp.wait_recv()

    hbm = lambda a: pltpu.HBM(a.shape, a.dtype)
    res = pl.pallas_call(
        body, name=name, in_specs=[_HBM] * (2 * n) + [_SEM_SPEC] * (2 * n) + [_ANY], out_specs=[_HBM] * (2 * n),
        out_shape=[hbm(a) for a in list(xs) + list(lands)], input_output_aliases={a: a for a in range(2 * n)},
        compiler_params=_SPLIT_COPY,
    )(*xs, *lands, *send_sems, *recv_sems, after)
    return res[:n], res[n:]


_ADAMW_PARTS_BLOCK_BYTES = 8 * 1024 * 1024


def _adamw(parts, w, m, v, *, name, own=None):
    r, c = w.shape
    tr = r
    for cand in range(8, r + 1, 8):
        if r % cand == 0 and N_DEV * cand * c * 4 <= _ADAMW_PARTS_BLOCK_BYTES:
            tr = cand
    has_own = own is not None

    def body(*refs):
        p_ref, w_ref, m_ref, v_ref = refs[0], refs[1 + has_own], refs[2 + has_own], refs[3 + has_own]
        g_ref, d_ref, nm_ref, nv_ref = refs[4 + has_own:]
        if has_own:
            me = _slot(*_mesh_pos())
            part = lambda k: jnp.where(me == k, refs[1][...], p_ref[k])
        else:
            part = lambda k: p_ref[k]
        g = part(0)
        for k in range(1, N_DEV):
            g = g + part(k)
        nm = ADAM_B1 * m_ref[...] + (1.0 - ADAM_B1) * g
        nv = ADAM_B2 * v_ref[...] + (1.0 - ADAM_B2) * (g * g)
        m_hat = nm / (1.0 - ADAM_B1 ** ADAM_STEP)
        v_hat = nv / (1.0 - ADAM_B2 ** ADAM_STEP)
        g_ref[...] = g
        d_ref[...] = -ADAM_LR * (m_hat / (jnp.sqrt(v_hat) + ADAM_EPS) + ADAM_WD * w_ref[...])
        nm_ref[...] = nm
        nv_ref[...] = nv

    blk = pl.BlockSpec((tr, c), lambda i: (i, 0))
    return pl.pallas_call(
        body, grid=(r // tr,),
        in_specs=[pl.BlockSpec((N_DEV, tr, c), lambda i: (0, i, 0))] + [blk] * (3 + has_own),
        out_specs=[blk] * 4, out_shape=[jax.ShapeDtypeStruct((r, c), F32)] * 4, name=name,
        compiler_params=_params(("parallel",)),
    )(parts, *([own] if has_own else []), w, m, v)


_FLAT_COLS = 1024
_FLAT_ROW_ALIGN = 128

_SHARD_AXIS = {
    "norm_mix": None, "norm_ffn": None, "ev_w_in": 2, "ev_b_fgate": None, "ev_q_norm": None, "ev_k_norm": None,
    "ev_v_norm": None, "ev_w_spatial": None, "ev_b_spatial": None, "ev_w_out": 1, "od_a_re": None, "od_a_im": None,
    "od_log_dt": None, "od_b_re": None, "od_b_im": None, "od_c_re": None, "od_c_im": None, "od_d": 1, "od_w_glu": 2,
    "ffn_w_up": 2, "ffn_conv_w": 2, "ffn_conv_b": None, "ffn_w_down": 1, "ple_w_proj": 2, "ple_w_gate": 1,
}
_WEIGHTS = list(_SHARD_AXIS)
_REPLICATED = [n for n in _WEIGHTS if _SHARD_AXIS[n] is None]
_COL_SHARDED = ["ev_w_in", "od_w_glu", "ffn_w_up", "ple_w_proj"]
_ROW_SHARDED = ["ev_w_out", "ffn_w_down", "ple_w_gate"]
_SMALL_SHARDED = ["od_d", "ffn_conv_w"]


def _pack(arrays, lead=()):
    nl = len(lead)
    flat = jnp.concatenate([a.reshape(lead + (-1,)) for a in arrays], axis=nl)
    n = flat.shape[nl]
    chunk = _FLAT_COLS * _FLAT_ROW_ALIGN
    total = -(-n // chunk) * chunk
    flat = jnp.pad(flat, [(0, 0)] * nl + [(0, total - n)])
    return flat.reshape(lead + (total // _FLAT_COLS, _FLAT_COLS))


def _unpack(buf, shapes, lead=()):
    nl = len(lead)
    flat = buf.reshape(lead + (-1,))
    out, off = [], 0
    for shp in shapes:
        n = math.prod(shp)
        out.append(lax.slice_in_dim(flat, off, off + n, axis=nl).reshape(lead + tuple(shp)))
        off += n
    return out


def _join_shards(g, axis):
    g = jnp.moveaxis(g, 0, axis)
    shp = g.shape
    return g.reshape(shp[:axis] + (shp[axis] * shp[axis + 1],) + shp[axis + 2:])


def _split_shards(full, axis):
    shp = full.shape
    g = full.reshape(shp[:axis] + (N_DEV, shp[axis] // N_DEV) + shp[axis + 1:])
    return jnp.moveaxis(g, axis, 0)


def _local_step(x, p, tgt, w, on_layer_done):
    saved = []
    h = x
    for i in range(DEPTH):
        x0 = h
        if i % 2 == 0:
            x1, sv_mix = _even_fwd(x0, w, i // 2, i)
        else:
            x1, sv_mix = _odd_fwd(x0, w, i // 2, i)
        x2, sv_ffn = _ffn_fwd(x1, w, i)
        x3, sv_ple = _ple_fwd(x2, p[i], w, i)
        saved.append((x0, x1, x2, sv_mix, sv_ffn, sv_ple))
        h = x3
    loss, dh = _loss_and_grad(h, tgt)
    per_layer = {}
    for i in reversed(range(DEPTH)):
        x0, x1, x2, sv_mix, sv_ffn, sv_ple = saved[i]
        dx2, dx2b, g_ple = _ple_bwd(dh, x2, p[i], sv_ple, w, i)
        dx1, dx1b, g_ffn = _ffn_bwd(dx2, dx2b, x1, sv_ffn, w, i)
        if i % 2 == 0:
            dh, g_mix = _even_bwd(dx1, dx1b, x0, sv_mix, w, i // 2, i)
        else:
            dh, g_mix = _odd_bwd(dx1, x0, sv_mix, w, i // 2, i)
        layer_grads = {**g_ple, **g_ffn, **g_mix}
        dh = on_layer_done(i, layer_grads, dh)
        for name, g in layer_grads.items():
            per_layer.setdefault(name, []).append(g)
    grads = {name: gs[::-1] for name, gs in per_layer.items()}
    return loss, dh, grads


def kernel(x, p, norm_mix, norm_ffn, ev_w_in, ev_b_fgate, ev_q_norm, ev_k_norm, ev_v_norm, ev_w_spatial, ev_b_spatial, ev_w_out, od_a_re, od_a_im, od_log_dt, od_b_re, od_b_im, od_c_re, od_c_im, od_d, od_w_glu, ffn_w_up, ffn_conv_w, ffn_conv_b, ffn_w_down, ple_w_proj, ple_w_gate, loss_target, m_norm_mix, m_norm_ffn, m_ev_w_in, m_ev_b_fgate, m_ev_q_norm, m_ev_k_norm, m_ev_v_norm, m_ev_w_spatial, m_ev_b_spatial, m_ev_w_out, m_od_a_re, m_od_a_im, m_od_log_dt, m_od_b_re, m_od_b_im, m_od_c_re, m_od_c_im, m_od_d, m_od_w_glu, m_ffn_w_up, m_ffn_conv_w, m_ffn_conv_b, m_ffn_w_down, m_ple_w_proj, m_ple_w_gate, v_norm_mix, v_norm_ffn, v_ev_w_in, v_ev_b_fgate, v_ev_q_norm, v_ev_k_norm, v_ev_v_norm, v_ev_w_spatial, v_ev_b_spatial, v_ev_w_out, v_od_a_re, v_od_a_im, v_od_log_dt, v_od_b_re, v_od_b_im, v_od_c_re, v_od_c_im, v_od_d, v_od_w_glu, v_ffn_w_up, v_ffn_conv_w, v_ffn_conv_b, v_ffn_w_down, v_ple_w_proj, v_ple_w_gate):
    args = locals()
    wl = {n: args[n] for n in _WEIGHTS}
    ml = {n: args["m_" + n] for n in _WEIGHTS}
    vl = {n: args["v_" + n] for n in _WEIGHTS}

    kinds = ("g", "d", "m", "v")

    def local(n, layer, a):
        return a[n][layer].T if n in _COL_SHARDED else a[n][layer]

    w = {n: wl[n] for n in _REPLICATED}
    for n in _COL_SHARDED + _ROW_SHARDED:
        w[n] = []
        for layer in range(wl[n].shape[0]):
            g = _all_gather(local(n, layer, wl).astype(BF16), name="gather_" + n)
            full = g.reshape(N_DEV * g.shape[1], g.shape[2])
            if n == "ev_w_in":
                full = jnp.pad(full, ((0, IN_COLS_PAD - IN_COLS), (0, 0)))
            w[n].append(full)
    small = _all_gather(_pack([wl[n] for n in _SMALL_SHARDED]), name="gather_small_weights")
    for n, g in zip(_SMALL_SHARDED, _unpack(small, [wl[n].shape for n in _SMALL_SHARDED], lead=(N_DEV,))):
        w[n] = _join_shards(g, _SHARD_AXIS[n])

    in_flight = {}

    def start_exchange(i, layer_grads, dh):
        names = [n for n in _COL_SHARDED + _ROW_SHARDED if n in layer_grads]
        xs = [layer_grads[n].reshape(N_DEV, layer_grads[n].shape[0] // N_DEV, layer_grads[n].shape[1]) for n in names]
        handles, dh = _scatter_start(xs, dh, name="scatter_start_layer%d" % i)
        in_flight[i] = (names, handles)
        return dh

    loss, grad_x, grads = _local_step(x[0], p[:, 0], loss_target[0], w, start_exchange)
    loss = lax.psum(loss, ("x", "y", "c"))

    me = _slot(*_mesh_pos())
    out = {kind: {} for kind in kinds}
    done = {}
    for i in sorted(in_flight, reverse=True):
        names, handles = in_flight[i]
        sent, landed = _scatter_wait(handles, grad_x, name="scatter_wait_layer%d" % i)
        for n, x_sent, parts in zip(names, sent, landed):
            layer = i if wl[n].shape[0] == DEPTH else i // 2
            own = lax.dynamic_index_in_dim(x_sent, me, axis=0, keepdims=False)
            res = _adamw(parts, local(n, layer, wl), local(n, layer, ml), local(n, layer, vl), name="adamw_" + n, own=own)
            done.setdefault(n, {})[layer] = [r.T if n in _COL_SHARDED else r for r in res]
    for n, layers in done.items():
        for k, kind in enumerate(kinds):
            out[kind][n] = jnp.stack([layers[layer][k] for layer in sorted(layers)])

    send = _pack([_split_shards(jnp.stack(grads[n]), _SHARD_AXIS[n]) for n in _SMALL_SHARDED], lead=(N_DEV,))
    parts = _all_to_all(send, name="scatter_small_grads")
    shapes = [wl[n].shape for n in _SMALL_SHARDED]
    res = _adamw(parts, _pack([wl[n] for n in _SMALL_SHARDED]), _pack([ml[n] for n in _SMALL_SHARDED]),
                 _pack([vl[n] for n in _SMALL_SHARDED]), name="adamw_small")
    for kind, buf in zip(kinds, res):
        out[kind].update(zip(_SMALL_SHARDED, _unpack(buf, shapes)))

    parts = _all_gather(_pack([jnp.stack(grads[n]) for n in _REPLICATED]), name="gather_replicated_grads")
    shapes = [wl[n].shape for n in _REPLICATED]
    res = _adamw(parts, _pack([wl[n] for n in _REPLICATED]), _pack([ml[n] for n in _REPLICATED]),
                 _pack([vl[n] for n in _REPLICATED]), name="adamw_replicated")
    for kind, buf in zip(kinds, res):
        out[kind].update(zip(_REPLICATED, _unpack(buf, shapes)))

    return (loss, grad_x[None], *[out["g"][n] for n in _WEIGHTS], *[out["d"][n] for n in _WEIGHTS],
            *[out["m"][n] for n in _WEIGHTS], *[out["v"][n] for n in _WEIGHTS])
```

```python
import functools
import math

import jax
import jax.numpy as jnp
from jax import lax
from jax.experimental import pallas as pl
from jax.experimental.pallas import tpu as pltpu

F32 = jnp.float32
BF16 = jnp.bfloat16

D_MODEL = 1024
DEPTH = 4
A_GROUPS = 4
A_CHUNK = 128
A_WIDTH = 512
B_HEADS = 8
B_HEAD_DIM = 64
B_WIDTH = 512
IN_COLS = 2 * A_WIDTH + 3 * B_WIDTH + B_HEADS
IN_COLS_PAD = 2688
S5_GROUP_CH = 16
S5_GROUPS = 64
S5_STATE = 64
S5_N = S5_GROUPS * S5_STATE
S5_BLOCKS = 8
D_FF = 2816
PLE_DIM = 256
EPS = 1e-6
NEG_INF = -1e30
N_DEV = 8

ADAM_LR = 0.001
ADAM_B1 = 0.9
ADAM_B2 = 0.999
ADAM_EPS = 1e-08
ADAM_WD = 0.01
ADAM_STEP = 10

VMEM_LIMIT_BYTES = 56 * 1024 * 1024
MESH = pl.DeviceIdType.MESH


def _params(sem, vmem=VMEM_LIMIT_BYTES):
    return pltpu.CompilerParams(dimension_semantics=sem, vmem_limit_bytes=vmem)


_GELU_K = math.sqrt(2.0 / math.pi)


def _gelu(x):
    return x * (0.5 * (1.0 + jnp.tanh(_GELU_K * (x + 0.044715 * (x * x * x)))))


def _gelu_grad(x):
    t = jnp.tanh(_GELU_K * (x + 0.044715 * (x * x * x)))
    return 0.5 * (1.0 + t) + 0.5 * x * (1.0 - t * t) * (_GELU_K * (1.0 + 3.0 * 0.044715 * (x * x)))


def _sigmoid(x):
    return 0.5 * jnp.tanh(0.5 * x) + 0.5


def _rstd(x):
    return lax.rsqrt(jnp.mean(x * x, axis=-1, keepdims=True) + EPS)


def _rms_bwd(x, dy, g):
    r = _rstd(x)
    xh = x * r
    dyg = dy if g is None else dy * g
    dx = r * (dyg - xh * jnp.mean(dyg * xh, axis=-1, keepdims=True))
    return dx, dy * xh


def _shift_down(blk, halo, k):
    tr = blk.shape[0]
    r = pltpu.roll(blk, k, 0)
    hr = pltpu.roll(halo, k, 0)
    first = jnp.where(lax.broadcasted_iota(jnp.int32, hr.shape, 0) < k, hr, r[0:8])
    return jnp.concatenate([first, r[8:tr]], axis=0)


def _shift_up(blk, halo, k):
    tr = blk.shape[0]
    r = pltpu.roll(blk, tr - k, 0)
    hr = pltpu.roll(halo, 8 - k, 0)
    last = jnp.where(lax.broadcasted_iota(jnp.int32, hr.shape, 0) >= 8 - k, hr, r[tr - 8:tr])
    return jnp.concatenate([r[0:tr - 8], last], axis=0)


def _rw(fn, rows, consts, outs, accs=(), *, tr, name, prev=(), nxt=(), widths=None):
    s = rows[0].shape[0]
    tr = min(tr, s)
    n = s // tr
    nr, nc, npv, nnx, no, na = len(rows), len(consts), len(prev), len(nxt), len(outs), len(accs)
    widths = widths or [None] * nr

    def body(*refs):
        ins, out_refs = refs[:nr + nc + npv + nnx], refs[nr + nc + npv + nnx:]
        i = pl.program_id(0)
        vals = [r[...] for r in ins[:nr + nc]]
        vals += [jnp.where(i == 0, 0.0, r[...]) for r in ins[nr + nc:nr + nc + npv]]
        vals += [jnp.where(i == n - 1, 0.0, r[...]) for r in ins[nr + nc + npv:]]
        res = fn(*vals)
        if not isinstance(res, (tuple, list)):
            res = (res,)
        for k in range(no):
            out_refs[k][...] = res[k].astype(out_refs[k].dtype)
        if na:
            @pl.when(i == 0)
            def _():
                for k in range(na):
                    out_refs[no + k][...] = jnp.zeros_like(out_refs[no + k])

            for k in range(na):
                out_refs[no + k][...] += res[no + k]

    in_specs = []
    for a, w in zip(rows, widths):
        if w is None:
            in_specs.append(pl.BlockSpec((tr, a.shape[1]), lambda i: (i, 0)))
        else:
            in_specs.append(pl.BlockSpec((tr, w[1]), functools.partial(lambda i, cb: (i, cb), cb=w[0])))
    for c in consts:
        in_specs.append(pl.BlockSpec(c.shape, functools.partial(lambda i, nd: (0,) * nd, nd=c.ndim)))
    t8 = tr // 8
    for a in prev:
        in_specs.append(pl.BlockSpec((8, a.shape[1]), lambda i: (jnp.maximum(i * t8 - 1, 0), 0)))
    for a in nxt:
        in_specs.append(pl.BlockSpec((8, a.shape[1]), lambda i: (jnp.minimum((i + 1) * t8, s // 8 - 1), 0)))
    out_shape = [jax.ShapeDtypeStruct((s, w), dt) for w, dt in outs]
    out_specs = [pl.BlockSpec((tr, w), lambda i: (i, 0)) for w, _ in outs]
    out_shape += [jax.ShapeDtypeStruct(a, F32) for a in accs]
    out_specs += [pl.BlockSpec(a, lambda i: (0, 0)) for a in accs]
    res = pl.pallas_call(
        body, grid=(n,), in_specs=in_specs, out_specs=out_specs, out_shape=out_shape, name=name,
        compiler_params=_params(("arbitrary",) if na else ("parallel",)),
    )(*rows, *consts, *prev, *nxt)
    return res


def _pick(n, cap):
    if n <= cap:
        return n
    best = None
    for d in range(128, cap + 1, 128):
        if n % d == 0:
            best = d
    assert best is not None, (n, cap)
    return best


def _mm(a, b, *, ta=False, tb=False, out_dtype=F32, add=None, epilogue=None, name, bm=1024, bn=512, bk=None):
    (k_dim, m) = a.shape if ta else a.shape[::-1]
    n = b.shape[0] if tb else b.shape[1]
    assert (b.shape[1] if tb else b.shape[0]) == k_dim
    if bk is None:
        bk = 1024 if ta else 2816
    bm, bn, bk = _pick(m, bm), _pick(n, bn), _pick(k_dim, bk)
    nk = k_dim // bk
    dims = (((0 if ta else 1,), (1 if tb else 0,)), ((), ()))
    assert add is None or epilogue is None
    epi_fn, extra, out_dtypes = epilogue or (None, [add] if add is not None else [], (out_dtype,))
    n_extra, n_out = len(extra), len(out_dtypes)

    def body(*refs):
        a_ref, b_ref = refs[0], refs[1]
        extra_refs = refs[2:2 + n_extra]
        out_refs = refs[2 + n_extra:2 + n_extra + n_out]
        p = lax.dot_general(a_ref[...].astype(BF16), b_ref[...].astype(BF16), dims, preferred_element_type=F32)

        def finish(acc):
            if epi_fn is not None:
                res = epi_fn(acc, *[r[...] for r in extra_refs])
            else:
                res = (acc + extra_refs[0][...],) if n_extra else (acc,)
            for o_ref, r in zip(out_refs, res):
                o_ref[...] = r.astype(o_ref.dtype)

        if nk == 1:
            finish(p)
        else:
            acc_ref = refs[2 + n_extra + n_out]
            kk = pl.program_id(2)

            @pl.when(kk == 0)
            def _():
                acc_ref[...] = p

            @pl.when(kk > 0)
            def _():
                acc_ref[...] += p

            @pl.when(kk == nk - 1)
            def _():
                finish(acc_ref[...])

    a_spec = pl.BlockSpec((bk, bm), lambda i, j, k: (k, i)) if ta else pl.BlockSpec((bm, bk), lambda i, j, k: (i, k))
    b_spec = pl.BlockSpec((bn, bk), lambda i, j, k: (j, k)) if tb else pl.BlockSpec((bk, bn), lambda i, j, k: (k, j))
    tile = pl.BlockSpec((bm, bn), lambda i, j, k: (i, j))
    res = pl.pallas_call(
        body, grid=(m // bm, n // bn, nk), in_specs=[a_spec, b_spec] + [tile] * n_extra,
        out_specs=[tile] * n_out, out_shape=[jax.ShapeDtypeStruct((m, n), dt) for dt in out_dtypes],
        scratch_shapes=[pltpu.VMEM((bm, bn), F32)] if nk > 1 else [],
        name=name, compiler_params=_params(("parallel", "parallel", "arbitrary")),
    )(a, b, *extra)
    return res if epilogue is not None else res[0]


def _rms_fwd(x, g, *, name):
    if g is None:
        return _rw(lambda xv: xv * _rstd(xv), [x], [], [(x.shape[1], BF16)], tr=512, name=name)[0]
    return _rw(lambda xv, gv: xv * _rstd(xv) * gv, [x], [g], [(x.shape[1], BF16)], tr=512, name=name)[0]


def _rms_bwd_call(x, dy, g, dx_add, *, name, tr=512, scale=None, bf16_copy=False):
    w = x.shape[1]
    has_g, has_add = g is not None, dx_add is not None

    def fn(*v):
        xv, dyv = v[0], v[1]
        if scale is not None:
            dyv = dyv * scale
        gv = v[2 + has_add] if has_g else None
        dx, dg = _rms_bwd(xv, dyv, gv)
        if has_add:
            dx = dx + v[2]
        return (dx,) + ((dx,) if bf16_copy else ()) + ((jnp.sum(dg, axis=0, keepdims=True),) if has_g else ())

    rows = [x, dy] + ([dx_add] if has_add else [])
    outs = [(w, F32)] + ([(w, BF16)] if bf16_copy else [])
    res = _rw(fn, rows, [g] if has_g else [], outs, [(1, w)] if has_g else [], tr=tr, name=name)
    return tuple(res) if has_g else tuple(res) + (None,)


def _conv_taps(hup, halo, cw):
    h1 = _shift_down(hup, halo, 1)
    h2 = _shift_down(hup, halo, 2)
    return h1, h2


def _ffn_fwd(x1, w, i):
    h2b = _rms_fwd(x1, w["norm_ffn"][i:i + 1], name="ffn_norm")
    hup = _mm(h2b, w["ffn_w_up"][i], tb=True, name="ffn_up", bn=1408)
    cw, cb = w["ffn_conv_w"][i], w["ffn_conv_b"][i:i + 1]

    def fn(hv, cwv, cbv, halo):
        h1, h2 = _conv_taps(hv, halo, cwv)
        hc = cbv + cwv[0:1] * h2
        hc = hc + cwv[1:2] * h1
        hc = hc + cwv[2:3] * hv
        g, up = hc[:, :D_FF], hc[:, D_FF:]
        return (g * _sigmoid(g)) * up

    a = _rw(fn, [hup], [cw, cb], [(D_FF, BF16)], tr=128, name="ffn_conv_gate", prev=[hup])[0]
    x2 = _mm(a, w["ffn_w_down"][i], add=x1, name="ffn_down")
    return x2, (h2b, hup, a)


def _ffn_bwd(dx2, dx2b, x1, saved, w, i):
    h2b, hup, a = saved
    cw, cb = w["ffn_conv_w"][i], w["ffn_conv_b"][i:i + 1]
    da = _mm(dx2b, w["ffn_w_down"][i], tb=True, name="ffn_down_dx", bn=1408)
    d_wdown = _mm(a, dx2b, ta=True, name="ffn_down_dw", bm=1408)

    def fn1(hv, dav, cwv, cbv, h_prev, h_next, da_next):
        tr = hv.shape[0]
        he = jnp.concatenate([hv, h_next], axis=0)
        dae = jnp.concatenate([dav, da_next], axis=0)
        h1, h2 = _conv_taps(he, h_prev, cwv)
        hc = cbv + cwv[0:1] * h2
        hc = hc + cwv[1:2] * h1
        hc = hc + cwv[2:3] * he
        g, up = hc[:, :D_FF], hc[:, D_FF:]
        sg = _sigmoid(g)
        d_up = dae * (g * sg)
        d_g = dae * up * (sg * (1.0 + g * (1.0 - sg)))
        d_hce = jnp.concatenate([d_g, d_up], axis=1)
        rows_e = tr + 8
        d_hup = (cwv[2:3] * d_hce[:tr] + cwv[1:2] * pltpu.roll(d_hce, rows_e - 1, 0)[:tr]
                 + cwv[0:1] * pltpu.roll(d_hce, rows_e - 2, 0)[:tr])
        d_hc = d_hce[:tr]
        col = lambda v: jnp.sum(v, axis=0, keepdims=True)
        return d_hup, col(d_hc), col(d_hc * h2[:tr]), col(d_hc * h1[:tr]), col(d_hc * hv)

    w2 = 2 * D_FF
    d_hup, d_cb, d_cw0, d_cw1, d_cw2 = _rw(fn1, [hup, da], [cw, cb], [(w2, BF16)], [(1, w2)] * 4, tr=128,
                                           name="ffn_conv_gate_bwd", prev=[hup], nxt=[hup, da])
    d_h2 = _mm(d_hup, w["ffn_w_up"][i], name="ffn_up_dx")
    d_wup_t = _mm(h2b, d_hup, ta=True, name="ffn_up_dw", bn=1408).T
    dx1, dx1b, d_norm = _rms_bwd_call(x1, d_h2, w["norm_ffn"][i:i + 1], dx2, name="ffn_norm_bwd", bf16_copy=True)
    grads = {"ffn_w_up": d_wup_t, "ffn_w_down": d_wdown, "ffn_conv_b": d_cb[0],
             "ffn_conv_w": jnp.concatenate([d_cw0, d_cw1, d_cw2], axis=0), "norm_ffn": d_norm[0]}
    return dx1, dx1b, grads


def _ple_fwd(x2, p_i, w, i):
    rn = _rms_fwd(x2, None, name="ple_norm")
    gl = _mm(rn, w["ple_w_gate"][i], name="ple_gate")
    x3 = _mm(p_i, w["ple_w_proj"][i], tb=True, name="ple_proj",
             epilogue=(lambda pp, xv, g: (xv + _sigmoid(g) * pp,), [x2, gl], (F32,)))[0]
    return x3, (rn, gl)


def _ple_bwd(dx3, x2, p_i, saved, w, i):
    rn, gl = saved

    def fn(pp, dv, g):
        sg = _sigmoid(g)
        return dv * sg, dv * pp * (sg * (1.0 - sg))

    d_pp, d_pre = _mm(p_i, w["ple_w_proj"][i], tb=True, name="ple_proj_bwd", epilogue=(fn, [dx3, gl], (BF16, BF16)))
    d_wproj_t = _mm(p_i, d_pp, ta=True, name="ple_proj_dw", bn=1024).T
    d_wgate = _mm(rn, d_pre, ta=True, name="ple_gate_dw")
    d_rn = _mm(d_pre, w["ple_w_gate"][i], tb=True, name="ple_gate_dx")
    dx2, dx2b, _ = _rms_bwd_call(x2, d_rn, None, dx3, name="ple_norm_bwd", bf16_copy=True)
    return dx2, dx2b, {"ple_w_proj": d_wproj_t, "ple_w_gate": d_wgate}


def _loss_and_grad(xf, tgt):
    def fn(xv, tv):
        d = xv - tv
        part = 0.5 * jnp.sum(jnp.mean(d * d, axis=-1, keepdims=True), axis=0, keepdims=True)
        return d * (1.0 / D_MODEL), part

    dx, loss = _rw(fn, [xf, tgt], [], [(D_MODEL, F32)], [(1, 1)], tr=512, name="loss_head")
    return loss[0, 0], dx


def _tril_mask():
    shp = (A_CHUNK, A_CHUNK)
    return lax.broadcasted_iota(jnp.int32, shp, 0) >= lax.broadcasted_iota(jnp.int32, shp, 1)


def _gmlp_fwd(z, w_s, b_s, v_gain, *, tr=512):
    s = z.shape[0]
    tr = min(tr, s)
    gw = A_WIDTH // A_GROUPS

    def body(z_ref, w_ref, b_ref, g_ref, y_ref):
        tril = _tril_mask()
        for g in range(A_GROUPS):
            wg = jnp.where(tril, w_ref[g], 0.0).astype(BF16)
            cols = slice(g * gw, (g + 1) * gw)
            vcols = slice(A_WIDTH + g * gw, A_WIDTH + (g + 1) * gw)
            for c in range(tr // A_CHUNK):
                rows = slice(c * A_CHUNK, (c + 1) * A_CHUNK)
                v = _gelu(z_ref[rows, vcols])
                vn = v * _rstd(v) * g_ref[:, cols]
                sv = jnp.dot(wg, vn.astype(BF16), preferred_element_type=F32) + b_ref[g]
                y_ref[rows, cols] = (_gelu(z_ref[rows, cols]) * sv).astype(BF16)

    return pl.pallas_call(
        body, grid=(s // tr,),
        in_specs=[pl.BlockSpec((tr, 2 * A_WIDTH), lambda i: (i, 0)),
                  pl.BlockSpec(w_s.shape, lambda i: (0, 0, 0)), pl.BlockSpec(b_s.shape, lambda i: (0, 0, 0)),
                  pl.BlockSpec(v_gain.shape, lambda i: (0, 0))],
        out_specs=pl.BlockSpec((tr, A_WIDTH), lambda i: (i, 0)),
        out_shape=jax.ShapeDtypeStruct((s, A_WIDTH + B_WIDTH), BF16), name="gmlp_fwd",
        compiler_params=_params(("parallel",)),
    )(z, w_s, b_s, v_gain)


def _gmlp_bwd(z, d_ymix, w_s, b_s, v_gain, dz, *, tr=512):
    s = z.shape[0]
    tr = min(tr, s)
    gw = A_WIDTH // A_GROUPS

    def body(z_ref, dy_ref, w_ref, b_ref, g_ref, _, dz_ref, dw_ref, db_ref, dg_ref):
        @pl.when(pl.program_id(0) == 0)
        def _():
            dw_ref[...] = jnp.zeros_like(dw_ref)
            db_ref[...] = jnp.zeros_like(db_ref)
            dg_ref[...] = jnp.zeros_like(dg_ref)

        tril = _tril_mask()
        for g in range(A_GROUPS):
            wg = jnp.where(tril, w_ref[g], 0.0).astype(BF16)
            cols = slice(g * gw, (g + 1) * gw)
            vcols = slice(A_WIDTH + g * gw, A_WIDTH + (g + 1) * gw)
            gain = g_ref[:, cols]
            for c in range(tr // A_CHUNK):
                rows = slice(c * A_CHUNK, (c + 1) * A_CHUNK)
                va, ua = z_ref[rows, vcols], z_ref[rows, cols]
                v = _gelu(va)
                r = _rstd(v)
                vh = v * r
                vnb = (vh * gain).astype(BF16)
                sv = jnp.dot(wg, vnb, preferred_element_type=F32) + b_ref[g]
                dy = dy_ref[rows, cols]
                d_sv = dy * _gelu(ua)
                dz_ref[rows, cols] = (dy * sv * _gelu_grad(ua)).astype(BF16)
                d_svb = d_sv.astype(BF16)
                d_vn = lax.dot_general(wg, d_svb, (((0,), (0,)), ((), ())), preferred_element_type=F32)
                dwp = lax.dot_general(d_svb, vnb, (((1,), (1,)), ((), ())), preferred_element_type=F32)
                dw_ref[g] += jnp.where(tril, dwp, 0.0)
                db_ref[g] += jnp.sum(d_sv, axis=1, keepdims=True)
                dg_ref[:, cols] += jnp.sum(d_vn * vh, axis=0, keepdims=True)
                d_vh = d_vn * gain
                d_v = r * (d_vh - vh * jnp.mean(d_vh * vh, axis=-1, keepdims=True))
                dz_ref[rows, vcols] = (d_v * _gelu_grad(va)).astype(BF16)

    return pl.pallas_call(
        body, grid=(s // tr,),
        in_specs=[pl.BlockSpec((tr, 2 * A_WIDTH), lambda i: (i, 0)), pl.BlockSpec((tr, A_WIDTH), lambda i: (i, 0)),
                  pl.BlockSpec(w_s.shape, lambda i: (0, 0, 0)), pl.BlockSpec(b_s.shape, lambda i: (0, 0, 0)),
                  pl.BlockSpec(v_gain.shape, lambda i: (0, 0)), _ANY],
        out_specs=[pl.BlockSpec((tr, 2 * A_WIDTH), lambda i: (i, 0)), pl.BlockSpec(w_s.shape, lambda i: (0, 0, 0)),
                   pl.BlockSpec(b_s.shape, lambda i: (0, 0, 0)), pl.BlockSpec(v_gain.shape, lambda i: (0, 0))],
        out_shape=[jax.ShapeDtypeStruct(dz.shape, dz.dtype), jax.ShapeDtypeStruct(w_s.shape, F32),
                   jax.ShapeDtypeStruct(b_s.shape, F32), jax.ShapeDtypeStruct(v_gain.shape, F32)],
        input_output_aliases={5: 0}, name="gmlp_bwd", compiler_params=_params(("arbitrary",)),
    )(z, d_ymix, w_s, b_s, v_gain, dz)


def _dot3(x, ub):
    x1 = x.astype(BF16)
    r1 = x - x1.astype(F32)
    x2 = r1.astype(BF16)
    x3 = (r1 - x2.astype(F32)).astype(BF16)
    d = lambda a: jnp.dot(a, ub, preferred_element_type=F32)
    return d(x1) + d(x2) + d(x3)


def _log_sigmoid(x):
    return jnp.minimum(x, 0.0) - jnp.log(1.0 + jnp.exp(-jnp.abs(x)))


def _fgate_fwd(f_t, b_col, *, tb=256):
    h, s = f_t.shape
    tb = min(tb, s)

    def body(f_ref, b_ref, c_ref, carry):
        @pl.when(pl.program_id(0) == 0)
        def _():
            carry[...] = jnp.zeros_like(carry)

        lf = _log_sigmoid(f_ref[...] + b_ref[...])
        shp = (tb, tb)
        upper = (lax.broadcasted_iota(jnp.int32, shp, 0) <= lax.broadcasted_iota(jnp.int32, shp, 1)).astype(BF16)
        c_ref[...] = _dot3(lf, upper) + carry[...]
        carry[...] += jnp.sum(lf, axis=1, keepdims=True)

    return pl.pallas_call(
        body, grid=(s // tb,),
        in_specs=[pl.BlockSpec((h, tb), lambda i: (0, i)), pl.BlockSpec((h, 1), lambda i: (0, 0))],
        out_specs=pl.BlockSpec((h, tb), lambda i: (0, i)), out_shape=jax.ShapeDtypeStruct((h, s), F32),
        scratch_shapes=[pltpu.VMEM((h, 1), F32)], name="fgate_fwd", compiler_params=_params(("arbitrary",)),
    )(f_t, b_col)


def _fgate_bwd(f_t, b_col, dc_a, dc_b, *, tb=256):
    h, s = f_t.shape
    tb = min(tb, s)
    n = s // tb

    def body(f_ref, b_ref, da_ref, db_ref, df_ref, dbias_ref, carry):
        @pl.when(pl.program_id(0) == 0)
        def _():
            carry[...] = jnp.zeros_like(carry)
            dbias_ref[...] = jnp.zeros_like(dbias_ref)

        dc = da_ref[...] + db_ref[...]
        shp = (tb, tb)
        lower = (lax.broadcasted_iota(jnp.int32, shp, 0) >= lax.broadcasted_iota(jnp.int32, shp, 1)).astype(BF16)
        d_lf = _dot3(dc, lower) + carry[...]
        carry[...] += jnp.sum(dc, axis=1, keepdims=True)
        df = d_lf * (1.0 - _sigmoid(f_ref[...] + b_ref[...]))
        df_ref[...] = df
        dbias_ref[...] += jnp.sum(df, axis=1, keepdims=True)

    blk = pl.BlockSpec((h, tb), lambda i: (0, n - 1 - i))
    return pl.pallas_call(
        body, grid=(n,), in_specs=[blk, pl.BlockSpec((h, 1), lambda i: (0, 0)), blk, blk],
        out_specs=[blk, pl.BlockSpec((h, 1), lambda i: (0, 0))],
        out_shape=[jax.ShapeDtypeStruct((h, s), F32), jax.ShapeDtypeStruct((h, 1), F32)],
        scratch_shapes=[pltpu.VMEM((h, 1), F32)], name="fgate_bwd", compiler_params=_params(("arbitrary",)),
    )(f_t, b_col, dc_a, dc_b)


_NT = (((1,), (1,)), ((), ()))


def _causal(shape, row0, col0, transposed):
    r = lax.broadcasted_iota(jnp.int32, shape, 0) + row0
    c = lax.broadcasted_iota(jnp.int32, shape, 1) + col0
    return (r <= c) if transposed else (c <= r)


_ATT_SCALE = B_HEAD_DIM ** -0.5
ATT_W = 128
_COL_CQ, _COL_CK, _COL_LSE, _COL_DELTA = 64, 67, 70, 64
_ATT_BLK = 2048
_ATT_SUB = 512


def _split3(x):
    h = x.astype(BF16).astype(F32)
    r = x - h
    m = r.astype(BF16).astype(F32)
    return h, m, (r - m).astype(BF16).astype(F32)


def _put_cols(base, lane, col0, parts):
    for t, pv in enumerate(parts):
        base = jnp.where(lane == col0 + t, pv, base)
    return base


_HEAD_PAIRS = B_HEADS // 2
_Q_BLOCK, _K_BLOCK, _V_BLOCK = 8, 12, 16
_F_BLOCK = 20
_O_BLOCK = 4
_ANY = pl.BlockSpec(memory_space=pl.ANY)


def _head_half(ref, h):
    x = ref[...]
    x = jnp.where((h & 1) == 1, pltpu.roll(x, B_HEAD_DIM, 1), x)
    return jnp.where(lax.broadcasted_iota(jnp.int32, x.shape, 1) < B_HEAD_DIM, x, 0.0)


def _pair(a, b):
    lane = lax.broadcasted_iota(jnp.int32, a.shape, 1)
    return jnp.where(lane < B_HEAD_DIM, a, pltpu.roll(b, B_HEAD_DIM, 1))


def _attn_prep(z, c_col, gq, gk, *, tr=1024):
    s = z.shape[0]
    tr = min(tr, s)
    n = s // tr

    def body(q_ref, k_ref, v_ref, c_ref, gq_ref, gk_ref, qo_ref, ko_ref, vo_ref):
        h = pl.program_id(0)
        q, k, v = _head_half(q_ref, h), _head_half(k_ref, h), _head_half(v_ref, h)
        lane = lax.broadcasted_iota(jnp.int32, q.shape, 1)
        rq = lax.rsqrt(jnp.sum(q * q, axis=-1, keepdims=True) * (1.0 / B_HEAD_DIM) + EPS)
        rk = lax.rsqrt(jnp.sum(k * k, axis=-1, keepdims=True) * (1.0 / B_HEAD_DIM) + EPS)
        c = c_ref[...]
        ch, cm, cl = _split3(c)
        one = jnp.ones_like(c)
        qq = _put_cols(q * rq * gq_ref[...] * _ATT_SCALE, lane, _COL_CQ, (ch, cm, cl))
        qq = _put_cols(qq, lane, _COL_CK, (one, one, one))
        kk = _put_cols(k * rk * gk_ref[...], lane, _COL_CQ, (one, one, one))
        kk = _put_cols(kk, lane, _COL_CK, (-ch, -cm, -cl))
        kk = _put_cols(kk, lane, _COL_LSE, (one, one, one))
        qo_ref[...] = qq.astype(BF16)
        ko_ref[...] = kk.astype(BF16)
        vo_ref[...] = _put_cols(v, lane, _COL_DELTA, (one, one, one)).astype(BF16)

    def zcol(base):
        return pl.BlockSpec((tr, ATT_W), lambda h, i: (i, base + (h >> 1)))

    rows = pl.BlockSpec((tr, ATT_W), lambda h, i: (h * n + i, 0))
    gain = pl.BlockSpec((1, ATT_W), lambda h, i: (0, 0))
    return pl.pallas_call(
        body, grid=(B_HEADS, n),
        in_specs=[zcol(_Q_BLOCK), zcol(_K_BLOCK), zcol(_V_BLOCK), pl.BlockSpec((tr, 1), lambda h, i: (h * n + i, 0)),
                  gain, gain],
        out_specs=[rows] * 3, out_shape=[jax.ShapeDtypeStruct((B_HEADS * s, ATT_W), BF16)] * 3, name="attn_prep",
        compiler_params=_params(("parallel", "parallel")),
    )(z, z, z, c_col, gq, gk)


def _attn_bwd_prep(d_ymix, o, qp, lse, *, tr=1024):
    s = d_ymix.shape[0]
    tr = min(tr, s)
    n = s // tr

    def body(do_ref, o_ref, q_ref, l_ref, dop_ref, qpp_ref):
        dov = _head_half(do_ref, pl.program_id(0))
        lane = lax.broadcasted_iota(jnp.int32, dov.shape, 1)
        dh, dm, dl = _split3(jnp.sum(dov * o_ref[...], axis=-1, keepdims=True))
        lh, lm, ll = _split3(l_ref[...])
        dop_ref[...] = _put_cols(dov, lane, _COL_DELTA, (-dh, -dm, -dl)).astype(BF16)
        qpp_ref[...] = _put_cols(q_ref[...].astype(F32), lane, _COL_LSE, (-lh, -lm, -ll)).astype(BF16)

    rows = pl.BlockSpec((tr, ATT_W), lambda h, i: (h * n + i, 0))
    return pl.pallas_call(
        body, grid=(B_HEADS, n),
        in_specs=[pl.BlockSpec((tr, ATT_W), lambda h, i: (i, _O_BLOCK + (h >> 1))), rows, rows,
                  pl.BlockSpec((tr, 1), lambda h, i: (h * n + i, 0))],
        out_specs=[rows] * 2, out_shape=[jax.ShapeDtypeStruct((B_HEADS * s, ATT_W), BF16)] * 2,
        name="attn_bwd_prep", compiler_params=_params(("parallel", "parallel")),
    )(d_ymix, o, qp, lse)


def _attn_merge_out(o, ymix, *, tr=1024):
    s = o.shape[1]
    tr = min(tr, s)

    def body(a_ref, b_ref, buf_ref, out_ref):
        out_ref[...] = _pair(a_ref[0], b_ref[0]).astype(BF16)

    return pl.pallas_call(
        body, grid=(_HEAD_PAIRS, s // tr),
        in_specs=[pl.BlockSpec((1, tr, ATT_W), lambda j, i: (2 * j, i, 0)),
                  pl.BlockSpec((1, tr, ATT_W), lambda j, i: (2 * j + 1, i, 0)), _ANY],
        out_specs=pl.BlockSpec((tr, ATT_W), lambda j, i: (i, _O_BLOCK + j)),
        out_shape=jax.ShapeDtypeStruct(ymix.shape, ymix.dtype), input_output_aliases={2: 0}, name="attn_merge_out",
        compiler_params=_params(("parallel", "parallel")),
    )(o, o, ymix)


def _pair_bwd(d_heads, dz, out_block, *, z=None, in_block=None, gain=None, scale=None, name, tr=1024):
    s = d_heads.shape[1]
    tr = min(tr, s)
    norm = gain is not None

    def body(*refs):
        a_ref, b_ref = refs[0], refs[1]
        d = _pair(a_ref[0], b_ref[0])
        if norm:
            x_ref, g_ref, _, out_ref, dg_ref = refs[2:]

            @pl.when((pl.program_id(0) == 0) & (pl.program_id(1) == 0))
            def _():
                dg_ref[...] = jnp.zeros_like(dg_ref)

            x = x_ref[...]
            lo = lax.broadcasted_iota(jnp.int32, x.shape, 1) < B_HEAD_DIM

            def half_mean(v):
                s_lo = jnp.sum(jnp.where(lo, v, 0.0), axis=-1, keepdims=True)
                s_hi = jnp.sum(jnp.where(lo, 0.0, v), axis=-1, keepdims=True)
                return jnp.where(lo, s_lo, s_hi) * (1.0 / B_HEAD_DIM)

            r = lax.rsqrt(half_mean(x * x) + EPS)
            xh = x * r
            dy = d * scale if scale is not None else d
            dyg = dy * g_ref[...]
            d = r * (dyg - xh * half_mean(dyg * xh))
            dg_ref[...] += jnp.sum(dy * xh, axis=0, keepdims=True)
        else:
            out_ref = refs[3]
        out_ref[...] = d.astype(BF16)

    heads = [pl.BlockSpec((1, tr, ATT_W), lambda j, i: (2 * j, i, 0)),
             pl.BlockSpec((1, tr, ATT_W), lambda j, i: (2 * j + 1, i, 0))]
    out_spec = pl.BlockSpec((tr, ATT_W), lambda j, i: (i, out_block + j))
    dz_shape = jax.ShapeDtypeStruct(dz.shape, dz.dtype)
    if norm:
        res = pl.pallas_call(
            body, grid=(_HEAD_PAIRS, s // tr),
            in_specs=heads + [pl.BlockSpec((tr, ATT_W), lambda j, i: (i, in_block + j)),
                              pl.BlockSpec((1, ATT_W), lambda j, i: (0, 0)), _ANY],
            out_specs=[out_spec, pl.BlockSpec((1, ATT_W), lambda j, i: (0, 0))],
            out_shape=[dz_shape, jax.ShapeDtypeStruct((1, ATT_W), F32)], input_output_aliases={4: 0}, name=name,
            compiler_params=_params(("arbitrary", "arbitrary")),
        )(d_heads, d_heads, z, gain, dz)
        return res[0], res[1]
    return pl.pallas_call(
        body, grid=(_HEAD_PAIRS, s // tr), in_specs=heads + [_ANY], out_specs=out_spec, out_shape=dz_shape,
        input_output_aliases={2: 0}, name=name, compiler_params=_params(("parallel", "parallel")),
    )(d_heads, d_heads, dz), None


def _attn_fwd(qp, kp, vp, *, blk=_ATT_BLK):
    h, s, d = qp.shape
    b = min(blk, s)
    n = s // b
    sub = min(_ATT_SUB, b)
    ns = b // sub

    def body(q_ref, k_ref, v_ref, o_ref, lse_ref):
        i = pl.program_id(1)
        qs = [q_ref[0, t * sub:(t + 1) * sub, :] for t in range(ns)]

        def tile(j, state, masked):
            off = pl.multiple_of(j * b, b)
            kk = k_ref[0, pl.ds(off, b), :]
            vv = v_ref[0, pl.ds(off, b), :]
            new = []
            for t in range(ns):
                m, l, acc = state[3 * t:3 * t + 3]
                nk = (t + 1) * sub if masked else b
                sc = lax.dot_general(qs[t], kk[:nk], _NT, preferred_element_type=F32)
                if masked:
                    sc = jnp.where(_causal((sub, nk), t * sub, 0, False), sc, NEG_INF)
                m_new = jnp.maximum(m, jnp.max(sc, axis=1, keepdims=True))
                alpha = jnp.exp(m - m_new)
                p = jnp.exp(sc - m_new)
                new += [m_new, alpha * l + jnp.sum(p, axis=1, keepdims=True),
                        alpha * acc + jnp.dot(p.astype(BF16), vv[:nk], preferred_element_type=F32)]
            return tuple(new)

        init = (jnp.full((sub, 1), NEG_INF, F32), jnp.zeros((sub, 1), F32), jnp.zeros((sub, d), F32)) * ns
        state = lax.fori_loop(0, i, lambda j, st: tile(j, st, False), init)
        state = tile(i, state, True)
        for t in range(ns):
            m, l, acc = state[3 * t:3 * t + 3]
            o_ref[0, t * sub:(t + 1) * sub, :] = acc / l
            lse_ref[0, t * sub:(t + 1) * sub, :] = m + jnp.log(l)

    blk_spec = pl.BlockSpec((1, b, d), lambda hh, i: (hh, i, 0))
    full = pl.BlockSpec((1, s, d), lambda hh, i: (hh, 0, 0))
    return pl.pallas_call(
        body, grid=(h, n), in_specs=[blk_spec, full, full],
        out_specs=[blk_spec, pl.BlockSpec((1, b, 1), lambda hh, i: (hh, i, 0))],
        out_shape=[jax.ShapeDtypeStruct((h, s, d), F32), jax.ShapeDtypeStruct((h, s, 1), F32)],
        name="attn_fwd", compiler_params=_params(("parallel", "arbitrary")),
    )(qp, kp, vp)


def _attn_bwd_dq(qpp, kp, vp, dop, *, blk=_ATT_BLK):
    h, s, d = qpp.shape
    b = min(blk, s)
    n = s // b
    sub = min(_ATT_SUB, b)
    ns = b // sub

    def body(q_ref, do_ref, k_ref, v_ref, dq_ref):
        i = pl.program_id(1)
        qs = [q_ref[0, t * sub:(t + 1) * sub, :] for t in range(ns)]
        dos = [do_ref[0, t * sub:(t + 1) * sub, :] for t in range(ns)]

        def tile(j, state, masked):
            off = pl.multiple_of(j * b, b)
            kk = k_ref[0, pl.ds(off, b), :]
            vv = v_ref[0, pl.ds(off, b), :]
            new = []
            for t in range(ns):
                nk = (t + 1) * sub if masked else b
                p = jnp.exp(lax.dot_general(qs[t], kk[:nk], _NT, preferred_element_type=F32))
                if masked:
                    p = jnp.where(_causal((sub, nk), t * sub, 0, False), p, 0.0)
                ds = p * lax.dot_general(dos[t], vv[:nk], _NT, preferred_element_type=F32)
                new.append(state[t] + jnp.dot(ds.astype(BF16), kk[:nk], preferred_element_type=F32))
            return tuple(new)

        state = lax.fori_loop(0, i, lambda j, st: tile(j, st, False), (jnp.zeros((sub, d), F32),) * ns)
        state = tile(i, state, True)
        for t in range(ns):
            dq_ref[0, t * sub:(t + 1) * sub, :] = state[t]

    blk_spec = pl.BlockSpec((1, b, d), lambda hh, i: (hh, i, 0))
    full = pl.BlockSpec((1, s, d), lambda hh, i: (hh, 0, 0))
    return pl.pallas_call(
        body, grid=(h, n), in_specs=[blk_spec, blk_spec, full, full], out_specs=blk_spec,
        out_shape=jax.ShapeDtypeStruct((h, s, d), F32), name="attn_bwd_dq",
        compiler_params=_params(("parallel", "arbitrary")),
    )(qpp, dop, kp, vp)


def _attn_bwd_dkv(qpp, kp, vp, dop, *, blk=_ATT_BLK):
    h, s, d = qpp.shape
    b = min(blk, s)
    n = s // b
    sub = min(_ATT_SUB, b)
    ns = b // sub

    def body(k_ref, v_ref, q_ref, do_ref, dk_ref, dv_ref):
        j = pl.program_id(1)
        ks = [k_ref[0, t * sub:(t + 1) * sub, :] for t in range(ns)]
        vs = [v_ref[0, t * sub:(t + 1) * sub, :] for t in range(ns)]

        def tile(i, state, masked):
            off = pl.multiple_of(i * b, b)
            qq = q_ref[0, pl.ds(off, b), :]
            dd = do_ref[0, pl.ds(off, b), :]
            new = []
            for t in range(ns):
                q0 = t * sub if masked else 0
                qt, dt = qq[q0:], dd[q0:]
                pt = jnp.exp(lax.dot_general(ks[t], qt, _NT, preferred_element_type=F32))
                if masked:
                    pt = jnp.where(_causal((sub, b - q0), q0, q0, True), pt, 0.0)
                dst = pt * lax.dot_general(vs[t], dt, _NT, preferred_element_type=F32)
                new += [state[2 * t] + jnp.dot(dst.astype(BF16), qt, preferred_element_type=F32),
                        state[2 * t + 1] + jnp.dot(pt.astype(BF16), dt, preferred_element_type=F32)]
            return tuple(new)

        state = tile(j, (jnp.zeros((sub, d), F32),) * (2 * ns), True)
        state = lax.fori_loop(j + 1, n, lambda i, st: tile(i, st, False), state)
        for t in range(ns):
            dk_ref[0, t * sub:(t + 1) * sub, :] = state[2 * t]
            dv_ref[0, t * sub:(t + 1) * sub, :] = state[2 * t + 1]

    blk_spec = pl.BlockSpec((1, b, d), lambda hh, j: (hh, j, 0))
    full = pl.BlockSpec((1, s, d), lambda hh, j: (hh, 0, 0))
    return pl.pallas_call(
        body, grid=(h, n), in_specs=[blk_spec, blk_spec, full, full], out_specs=[blk_spec, blk_spec],
        out_shape=[jax.ShapeDtypeStruct((h, s, d), F32)] * 2, name="attn_bwd_dkv",
        compiler_params=_params(("parallel", "arbitrary")),
    )(kp, vp, qpp, dop)


def _pad_head(g):
    return jnp.pad(g, ((0, 0), (0, ATT_W - B_HEAD_DIM)))


def _even_fwd(x0, w, e, i):
    s = x0.shape[0]
    hs = B_HEADS * s
    hb = _rms_fwd(x0, w["norm_mix"][i:i + 1], name="mix_norm")
    z = _mm(hb, w["ev_w_in"][e], tb=True, name="ev_in", bn=896)
    b_s = w["ev_b_spatial"][e][:, :, None]
    v_gain = w["ev_v_norm"][e:e + 1]
    ymix = _gmlp_fwd(z, w["ev_w_spatial"][e], b_s, v_gain)
    f_t = z[:, IN_COLS - B_HEADS:IN_COLS].T
    c = _fgate_fwd(f_t, w["ev_b_fgate"][e][:, None])
    qp, kp, vp = _attn_prep(z, c.reshape(hs, 1), _pad_head(w["ev_q_norm"][e:e + 1]),
                            _pad_head(w["ev_k_norm"][e:e + 1]))
    shp = (B_HEADS, s, ATT_W)
    o, lse = _attn_fwd(qp.reshape(shp), kp.reshape(shp), vp.reshape(shp))
    ymix = _attn_merge_out(o, ymix)
    x1 = _mm(ymix, w["ev_w_out"][e], add=x0, name="ev_out")
    return x1, (hb, z, ymix, qp, kp, vp, f_t, o, lse)


def _even_bwd(dx1, dx1b, x0, saved, w, e, i):
    hb, z, ymix, qp, kp, vp, f_t, o, lse = saved
    s = x0.shape[0]
    hs = B_HEADS * s
    d_ymix = _mm(dx1b, w["ev_w_out"][e], tb=True, name="ev_out_dx")
    d_wout = _mm(ymix, dx1b, ta=True, name="ev_out_dw")
    dop, qpp = _attn_bwd_prep(d_ymix, o.reshape(hs, ATT_W), qp, lse.reshape(hs, 1))
    shp = (B_HEADS, s, ATT_W)
    qpp, dop, kp3, vp3 = qpp.reshape(shp), dop.reshape(shp), kp.reshape(shp), vp.reshape(shp)
    dqp = _attn_bwd_dq(qpp, kp3, vp3, dop)
    dkp, dvp = _attn_bwd_dkv(qpp, kp3, vp3, dop)
    d_ft, d_bf = _fgate_bwd(f_t, w["ev_b_fgate"][e][:, None], dqp[:, :, _COL_CQ], -dkp[:, :, _COL_CK])
    dz = jnp.pad(d_ft.T.astype(BF16), ((0, 0), (_F_BLOCK * ATT_W, IN_COLS_PAD - IN_COLS)))
    b_s = w["ev_b_spatial"][e][:, :, None]
    v_gain = w["ev_v_norm"][e:e + 1]
    dz, d_ws, d_bs, d_vg = _gmlp_bwd(z, d_ymix, w["ev_w_spatial"][e], b_s, v_gain, dz)
    twice = lambda g: jnp.concatenate([g, g], axis=1)
    dz, d_gq = _pair_bwd(dqp, dz, _Q_BLOCK, z=z, in_block=_Q_BLOCK, gain=twice(w["ev_q_norm"][e:e + 1]),
                         scale=_ATT_SCALE, name="attn_qnorm_bwd")
    dz, d_gk = _pair_bwd(dkp, dz, _K_BLOCK, z=z, in_block=_K_BLOCK, gain=twice(w["ev_k_norm"][e:e + 1]),
                         name="attn_knorm_bwd")
    dz, _ = _pair_bwd(dvp, dz, _V_BLOCK, name="attn_dv_out")
    d_h = _mm(dz, w["ev_w_in"][e], name="ev_in_dx")
    d_win_t = _mm(hb, dz, ta=True, name="ev_in_dw", bn=1344)[:, :IN_COLS].T
    dx0, d_nm = _rms_bwd_call(x0, d_h, w["norm_mix"][i:i + 1], dx1, name="mix_norm_bwd")
    hd = B_HEAD_DIM
    grads = {"ev_w_in": d_win_t, "ev_w_out": d_wout, "ev_b_fgate": d_bf[:, 0],
             "ev_q_norm": d_gq[0, :hd] + d_gq[0, hd:], "ev_k_norm": d_gk[0, :hd] + d_gk[0, hd:], "ev_v_norm": d_vg[0],
             "ev_w_spatial": d_ws, "ev_b_spatial": d_bs[:, :, 0], "norm_mix": d_nm[0]}
    return dx0, grads


def _s5_disc(a_re, a_im, log_dt, b_re, b_im):
    dt = jnp.exp(log_dt)[:, None]
    lr, li = a_re, a_im
    mag = jnp.exp(lr * dt)
    ab_re, ab_im = mag * jnp.cos(li * dt), mag * jnp.sin(li * dt)
    den = lr * lr + li * li
    nr, ni = ab_re - 1.0, ab_im
    cr = (nr * lr + ni * li) / den
    ci = (ni * lr - nr * li) / den
    bb_re = cr[..., None] * b_re - ci[..., None] * b_im
    bb_im = cr[..., None] * b_im + ci[..., None] * b_re
    return ab_re, ab_im, bb_re, bb_im


_GPB = S5_GROUPS // S5_BLOCKS


def _blockdiag(t):
    a, b = t.shape[1:]
    eye = jnp.eye(_GPB, dtype=t.dtype)
    t = t.reshape(S5_BLOCKS, _GPB, a, 1, b) * eye[None, :, None, :, None]
    return t.reshape(S5_BLOCKS, _GPB * a, _GPB * b)


def _blockdiag_extract(m, a, b):
    eye = jnp.eye(_GPB, dtype=m.dtype)
    m = m.reshape(S5_BLOCKS, _GPB, a, _GPB, b) * eye[None, :, None, :, None]
    return jnp.sum(m, axis=3).reshape(S5_GROUPS, a, b)


_SCAN_ROWS = 8


def _cmul(a_r, a_i, b_r, b_i):
    return a_r * b_r - a_i * b_i, a_r * b_i + a_i * b_r


def _scan_tiles(yr_ref, yi_ref, a_r, a_i, c_r, c_i, *, rev, x_refs=None):
    tb, wl = yr_ref.shape
    ntile = tb // _SCAN_ROWS
    with_acc = x_refs is not None
    if rev:
        a_i = -a_i
    pw = [(a_r, a_i)]
    for _ in range(_SCAN_ROWS - 1):
        pw.append(_cmul(*pw[-1], a_r, a_i))
    sub = lax.broadcasted_iota(jnp.int32, (_SCAN_ROWS, wl), 0)
    dist = (_SCAN_ROWS - 1 - sub) if rev else sub
    zero = jnp.zeros((_SCAN_ROWS, wl), F32)
    steps = []
    for kk in (1, 2, 4):
        steps.append(((_SCAN_ROWS - kk) if rev else kk, jnp.where(dist >= kk, pw[kk - 1][0], zero),
                      jnp.where(dist >= kk, pw[kk - 1][1], zero)))
    e_r, e_i = zero, zero
    for d in range(_SCAN_ROWS):
        e_r = jnp.where(dist == d, pw[d][0], e_r)
        e_i = jnp.where(dist == d, pw[d][1], e_i)
    exit_row = 0 if rev else _SCAN_ROWS - 1

    def tile(q, carry):
        idx = (ntile - 1 - q) if rev else q
        rows = pl.ds(pl.multiple_of(idx * _SCAN_ROWS, _SCAN_ROWS), _SCAN_ROWS)
        y_r, y_i = yr_ref[rows, :], yi_ref[rows, :]
        for sh, k_r, k_i in steps:
            t_r, t_i = _cmul(k_r, k_i, pltpu.roll(y_r, sh, 0), pltpu.roll(y_i, sh, 0))
            y_r, y_i = y_r + t_r, y_i + t_i
        cb_r = jnp.broadcast_to(carry[0], (_SCAN_ROWS, wl))
        cb_i = jnp.broadcast_to(carry[1], (_SCAN_ROWS, wl))
        t_r, t_i = _cmul(e_r, e_i, cb_r, cb_i)
        y_r, y_i = y_r + t_r, y_i + t_i
        yr_ref[rows, :] = y_r
        yi_ref[rows, :] = y_i
        out = (y_r[exit_row:exit_row + 1], y_i[exit_row:exit_row + 1])
        if with_acc:
            n_r = jnp.where(sub == _SCAN_ROWS - 1, cb_r, pltpu.roll(y_r, _SCAN_ROWS - 1, 0))
            n_i = jnp.where(sub == _SCAN_ROWS - 1, cb_i, pltpu.roll(y_i, _SCAN_ROWS - 1, 0))
            s_r, s_i = x_refs[0][rows, :], x_refs[1][rows, :]
            out += (carry[2] + (n_r * s_r + n_i * s_i), carry[3] + (n_i * s_r - n_r * s_i))
        return out

    return lax.fori_loop(0, ntile, tile, (c_r, c_i) + ((zero, zero) if with_acc else ()), unroll=2)


_S5_WL = S5_N // S5_BLOCKS
_S5_CW = D_MODEL // S5_BLOCKS
_TN = (((0,), (0,)), ((), ()))


def _s5_fwd(hb, mats, *, tb=1024):
    s = hb.shape[0]
    tb = min(tb, s)

    def body(h_ref, wbr_ref, wbi_ref, wcr_ref, wci_ref, ar_ref, ai_ref, xr_ref, xi_ref, cp_ref, cr, ci):
        @pl.when(pl.program_id(1) == 0)
        def _():
            cr[...] = jnp.zeros_like(cr)
            ci[...] = jnp.zeros_like(ci)

        hv = h_ref[...]
        xr_ref[...] = jnp.dot(hv, wbr_ref[0], preferred_element_type=F32)
        xi_ref[...] = jnp.dot(hv, wbi_ref[0], preferred_element_type=F32)
        cr[...], ci[...] = _scan_tiles(xr_ref, xi_ref, ar_ref[...], ai_ref[...], cr[...], ci[...], rev=False)
        cp_ref[...] = (jnp.dot(xr_ref[...].astype(BF16), wcr_ref[0], preferred_element_type=F32)
                       + jnp.dot(xi_ref[...].astype(BF16), wci_ref[0], preferred_element_type=F32))

    wide = pl.BlockSpec((tb, _S5_WL), lambda c, t: (t, c))
    narrow = pl.BlockSpec((tb, _S5_CW), lambda c, t: (t, c))
    w_in = pl.BlockSpec((1, _S5_CW, _S5_WL), lambda c, t: (c, 0, 0))
    w_out = pl.BlockSpec((1, _S5_WL, _S5_CW), lambda c, t: (c, 0, 0))
    vec = pl.BlockSpec((1, _S5_WL), lambda c, t: (0, c))
    return pl.pallas_call(
        body, grid=(S5_BLOCKS, s // tb), in_specs=[narrow, w_in, w_in, w_out, w_out, vec, vec],
        out_specs=[wide, wide, narrow],
        out_shape=[jax.ShapeDtypeStruct((s, S5_N), F32)] * 2 + [jax.ShapeDtypeStruct((s, D_MODEL), F32)],
        scratch_shapes=[pltpu.VMEM((1, _S5_WL), F32)] * 2, name="s5_fwd",
        compiler_params=_params(("parallel", "arbitrary")),
    )(hb, mats["wb_re"], mats["wb_im"], mats["wc_re"], mats["wc_imn"], mats["ar"], mats["ai"])


def _s5_bwd(dyb, hb, xr, xi, dhd, mats, *, tb=1024):
    s = hb.shape[0]
    tb = min(tb, s)
    nt = s // tb

    def body(dy_ref, h_ref, xr_ref, xi_ref, dhd_ref, wctr_ref, wcti_ref, wbtr_ref, wbti_ref, ar_ref, ai_ref,
             dh_ref, dar_ref, dai_ref, dwbr_ref, dwbi_ref, dwcr_ref, dwci_ref, lr, li, cr, ci):
        @pl.when(pl.program_id(1) == 0)
        def _():
            for r in (cr, ci, dar_ref, dai_ref, dwbr_ref, dwbi_ref, dwcr_ref, dwci_ref):
                r[...] = jnp.zeros_like(r)

        dyv, hv = dy_ref[...], h_ref[...]
        lr[...] = jnp.dot(dyv, wctr_ref[0], preferred_element_type=F32)
        li[...] = jnp.dot(dyv, wcti_ref[0], preferred_element_type=F32)
        cr[...], ci[...], acc_r, acc_i = _scan_tiles(lr, li, ar_ref[...], ai_ref[...], cr[...], ci[...], rev=True,
                                                     x_refs=(xr_ref, xi_ref))
        dar_ref[...] += jnp.sum(acc_r, axis=0, keepdims=True)
        dai_ref[...] += jnp.sum(acc_i, axis=0, keepdims=True)
        lrb, lib = lr[...].astype(BF16), li[...].astype(BF16)
        dh_ref[...] = (jnp.dot(lrb, wbtr_ref[0], preferred_element_type=F32) + dhd_ref[...]
                       + jnp.dot(lib, wbti_ref[0], preferred_element_type=F32))
        dwbr_ref[0] += lax.dot_general(hv, lrb, _TN, preferred_element_type=F32)
        dwbi_ref[0] += lax.dot_general(hv, lib, _TN, preferred_element_type=F32)
        dwcr_ref[0] += lax.dot_general(dyv, xr_ref[...].astype(BF16), _TN, preferred_element_type=F32)
        dwci_ref[0] += lax.dot_general(dyv, xi_ref[...].astype(BF16), _TN, preferred_element_type=F32)

    wide = pl.BlockSpec((tb, _S5_WL), lambda c, t: (nt - 1 - t, c))
    narrow = pl.BlockSpec((tb, _S5_CW), lambda c, t: (nt - 1 - t, c))
    w_in = pl.BlockSpec((1, _S5_CW, _S5_WL), lambda c, t: (c, 0, 0))
    w_out = pl.BlockSpec((1, _S5_WL, _S5_CW), lambda c, t: (c, 0, 0))
    vec = pl.BlockSpec((1, _S5_WL), lambda c, t: (0, c))
    acc_shape = jax.ShapeDtypeStruct((S5_BLOCKS, _S5_CW, _S5_WL), F32)
    return pl.pallas_call(
        body, grid=(S5_BLOCKS, nt), in_specs=[narrow, narrow, wide, wide, narrow, w_in, w_in, w_out, w_out, vec, vec],
        out_specs=[narrow, vec, vec, w_in, w_in, w_in, w_in],
        out_shape=[jax.ShapeDtypeStruct((s, D_MODEL), F32)] + [jax.ShapeDtypeStruct((1, S5_N), F32)] * 2
        + [acc_shape] * 4,
        scratch_shapes=[pltpu.VMEM((tb, _S5_WL), F32)] * 2 + [pltpu.VMEM((1, _S5_WL), F32)] * 2, name="s5_bwd",
        compiler_params=_params(("parallel", "arbitrary")),
    )(dyb, hb, xr, xi, dhd, mats["wct_re"], mats["wct_imn"], mats["wbt_re"], mats["wbt_im"], mats["ar"], mats["ai"])


def _s5_mats(w, o):
    ab_re, ab_im, bb_re, bb_im = _s5_disc(w["od_a_re"][o], w["od_a_im"][o], w["od_log_dt"][o], w["od_b_re"][o],
                                          w["od_b_im"][o])
    c_re, c_im = w["od_c_re"][o], w["od_c_im"][o]
    tr = lambda t: t.transpose(0, 2, 1)
    bd = lambda t: _blockdiag(t).astype(BF16)
    return {
        "ar": ab_re.reshape(1, S5_N), "ai": ab_im.reshape(1, S5_N),
        "wb_re": bd(tr(bb_re)), "wb_im": bd(tr(bb_im)), "wc_re": bd(tr(c_re)), "wc_imn": bd(-tr(c_im)),
        "wct_re": bd(c_re), "wct_imn": bd(-c_im), "wbt_re": bd(bb_re), "wbt_im": bd(bb_im),
    }


def _odd_fwd(x0, w, o, i):
    mats = _s5_mats(w, o)
    nm = w["norm_mix"][i:i + 1]
    d_row = w["od_d"][o:o + 1]
    hb = _rms_fwd(x0, nm, name="mix_norm")
    xr, xi, cp = _s5_fwd(hb, mats)

    def fn(xv, cpv, gv, dv):
        y = cpv + dv * (xv * _rstd(xv) * gv)
        return y, _gelu(y)

    y, gy = _rw(fn, [x0, cp], [nm, d_row], [(D_MODEL, F32), (D_MODEL, BF16)], tr=512, name="s5_out")
    gg = _mm(gy, w["od_w_glu"][o], tb=True, name="od_glu")
    x1 = _rw(lambda xv, g: xv + g[:, :D_MODEL] * _sigmoid(g[:, D_MODEL:]), [x0, gg], [], [(D_MODEL, F32)], tr=512,
             name="od_glu_out")[0]
    return x1, (hb, xr, xi, y, gy, gg)


def _odd_bwd(dx1, x0, saved, w, o, i):
    hb, xr, xi, y, gy, gg = saved
    mats = _s5_mats(w, o)
    nm = w["norm_mix"][i:i + 1]
    d_row = w["od_d"][o:o + 1]

    def fn_glu(dv, g):
        ga, gb = g[:, :D_MODEL], g[:, D_MODEL:]
        sg = _sigmoid(gb)
        return jnp.concatenate([dv * sg, dv * ga * (sg * (1.0 - sg))], axis=1)

    dgg = _rw(fn_glu, [dx1, gg], [], [(2 * D_MODEL, BF16)], tr=512, name="od_glu_out_bwd")[0]
    d_wglu = _mm(gy, dgg, ta=True, name="od_glu_dw", bn=1024).T
    d_gy = _mm(dgg, w["od_w_glu"][o], name="od_glu_dx")

    def fn_y(dg, yv, xv, gv, dv):
        dy = dg * _gelu_grad(yv)
        h = xv * _rstd(xv) * gv
        return dy, dv * dy, jnp.sum(dy * h, axis=0, keepdims=True)

    dyb, dhd, d_d = _rw(fn_y, [d_gy, y, x0], [nm, d_row], [(D_MODEL, BF16), (D_MODEL, F32)], [(1, D_MODEL)], tr=512,
                        name="s5_out_bwd")
    dh, d_ar, d_ai, d_wb_re, d_wb_im, d_wc_re, d_wc_im = _s5_bwd(dyb, hb, xr, xi, dhd, mats)
    gc, gp = S5_GROUP_CH, S5_STATE
    d_c_re = _blockdiag_extract(d_wc_re, gc, gp)
    d_c_im = -_blockdiag_extract(d_wc_im, gc, gp)
    d_bb_re = _blockdiag_extract(d_wb_re, gc, gp).transpose(0, 2, 1)
    d_bb_im = _blockdiag_extract(d_wb_im, gc, gp).transpose(0, 2, 1)
    dx0, d_nm = _rms_bwd_call(x0, dh, nm, dx1, name="mix_norm_bwd")
    _, vjp = jax.vjp(_s5_disc, w["od_a_re"][o], w["od_a_im"][o], w["od_log_dt"][o], w["od_b_re"][o], w["od_b_im"][o])
    d_a_re, d_a_im, d_log_dt, d_b_re, d_b_im = vjp(
        (d_ar.reshape(S5_GROUPS, S5_STATE), d_ai.reshape(S5_GROUPS, S5_STATE), d_bb_re, d_bb_im))
    grads = {"od_a_re": d_a_re, "od_a_im": d_a_im, "od_log_dt": d_log_dt, "od_b_re": d_b_re, "od_b_im": d_b_im,
             "od_c_re": d_c_re, "od_c_im": d_c_im, "od_d": d_d[0], "od_w_glu": d_wglu, "norm_mix": d_nm[0]}
    return dx0, grads


_HBM = pl.BlockSpec(memory_space=pltpu.HBM)


def _mesh_pos():
    return lax.axis_index("x"), lax.axis_index("y"), lax.axis_index("c")


def _slot(px, py, pc):
    return 4 * px + 2 * py + pc


def _all_gather(x, *, name):
    def body(x_ref, out_ref, send_sems, recv_sems, local_sem):
        mx, my, mc = _mesh_pos()
        me, sibling = (mx, my, mc), (mx, my, 1 - mc)
        chips = [(1 - mx, my), (mx, 1 - my), (1 - mx, 1 - my)]

        def copy(k, block, to, src=None):
            dst = out_ref.at[_slot(*block)]
            return pltpu.make_async_remote_copy(
                src_ref=dst if src is None else src, dst_ref=dst, send_sem=send_sems.at[k], recv_sem=recv_sems.at[k],
                device_id=to, device_id_type=MESH)

        mine = pltpu.make_async_copy(x_ref, out_ref.at[_slot(*me)], local_sem)
        mine.start()
        first = [copy(0, me, sibling, src=x_ref)]
        first += [copy(1 + j, me, (*chip, mc), src=x_ref) for j, chip in enumerate(chips)]
        for cp in first:
            cp.start()
        passed = [copy(4 + j, (*chip, mc), sibling) for j, chip in enumerate(chips)]
        for j, chip in enumerate(chips):
            copy(1 + j, (*chip, mc), me).wait_recv()
            passed[j].start()
        copy(0, sibling, me).wait_recv()
        for j, chip in enumerate(chips):
            copy(4 + j, (*chip, 1 - mc), me).wait_recv()
        for cp in first + passed:
            cp.wait_send()
        mine.wait()

    return pl.pallas_call(
        body, out_shape=jax.ShapeDtypeStruct((N_DEV,) + x.shape, x.dtype), in_specs=[_HBM], out_specs=_HBM,
        scratch_shapes=[pltpu.SemaphoreType.DMA((7,)), pltpu.SemaphoreType.DMA((7,)), pltpu.SemaphoreType.DMA],
        name=name,
    )(x)


def _all_to_all(x, *, name):
    def body(x_ref, out_ref, send_sems, recv_sems, local_sem):
        mx, my, mc = _mesh_pos()
        my_slot = _slot(mx, my, mc)
        mine = pltpu.make_async_copy(x_ref.at[my_slot], out_ref.at[my_slot], local_sem)
        mine.start()
        copies = []
        for k in range(1, N_DEV):
            peer = (1 - mx if k & 4 else mx, 1 - my if k & 2 else my, 1 - mc if k & 1 else mc)
            copies.append(pltpu.make_async_remote_copy(
                src_ref=x_ref.at[_slot(*peer)], dst_ref=out_ref.at[my_slot], send_sem=send_sems.at[k - 1],
                recv_sem=recv_sems.at[k - 1], device_id=peer, device_id_type=MESH))
        for cp in copies:
            cp.start()
        for cp in copies:
            cp.wait()
        mine.wait()

    return pl.pallas_call(
        body, out_shape=jax.ShapeDtypeStruct(x.shape, x.dtype), in_specs=[_HBM], out_specs=_HBM,
        scratch_shapes=[pltpu.SemaphoreType.DMA((7,)), pltpu.SemaphoreType.DMA((7,)), pltpu.SemaphoreType.DMA],
        name=name,
    )(x)


_SEMS = pltpu.SemaphoreType.DMA((N_DEV - 1,))
_SEM_SPEC = pl.BlockSpec(memory_space=pltpu.SEMAPHORE)
_SPLIT_COPY = pltpu.CompilerParams(has_side_effects=pltpu.SideEffectType.DATAFLOW_SIDE_EFFECTING)


def _exchange_copies(x_refs, land_refs, send_sems, recv_sems, gather):
    mx, my, mc = _mesh_pos()
    my_slot = _slot(mx, my, mc)
    copies = []
    for a, (x_ref, land_ref) in enumerate(zip(x_refs, land_refs)):
        for k in range(1, N_DEV):
            peer = (1 - mx if k & 4 else mx, 1 - my if k & 2 else my, 1 - mc if k & 1 else mc)
            copies.append(pltpu.make_async_remote_copy(
                src_ref=x_ref if gather else x_ref.at[_slot(*peer)], dst_ref=land_ref.at[my_slot],
                send_sem=send_sems[a].at[k - 1], recv_sem=recv_sems[a].at[k - 1], device_id=peer, device_id_type=MESH))
    return copies


def _exchange_start(xs, carry, *, gather, name):
    n = len(xs)

    def body(*refs):
        outs = refs[2 * n + 1:]
        for cp in _exchange_copies(refs[:n], refs[n:2 * n], outs[:n], outs[n:2 * n], gather):
            cp.start()

    hbm = lambda a: pltpu.HBM(a.shape, a.dtype)
    lands = [pltpu.with_memory_space_constraint(lax.empty(((N_DEV,) if gather else ()) + x.shape, x.dtype), pltpu.HBM)
             for x in xs]
    ins = [pltpu.with_memory_space_constraint(a, pltpu.HBM) for a in list(xs) + [carry]]
    res = pl.pallas_call(
        body, name=name, in_specs=[_HBM] * (2 * n + 1), out_specs=[_SEM_SPEC] * (2 * n) + [_HBM] * (2 * n + 1),
        out_shape=[_SEMS] * (2 * n) + [hbm(a) for a in list(xs) + lands] + [hbm(carry)],
        input_output_aliases={**{a: 2 * n + a for a in range(2 * n)}, 2 * n: 4 * n}, compiler_params=_SPLIT_COPY,
    )(*ins[:n], *lands, ins[n])
    return (res[:n], res[n:2 * n], res[2 * n:3 * n], res[3 * n:4 * n]), res[4 * n]


def _exchange_wait(handles, after, *, gather, name):
    send_sems, recv_sems, xs, lands = handles
    n = len(xs)

    def body(*refs):
        for cp in _exchange_copies(refs[:n], refs[n:2 * n], refs[2 * n:3 * n], refs[3 * n:4 * n], gather):
            cp.wait_send()
            cp.wait_recv()

    hbm = lambda a: pltpu.HBM(a.shape, a.dtype)
    res = pl.pallas_call(
        body, name=name, in_specs=[_HBM] * (2 * n) + [_SEM_SPEC] * (2 * n) + [_ANY], out_specs=[_HBM] * (2 * n),
        out_shape=[hbm(a) for a in list(xs) + list(lands)], input_output_aliases={a: a for a in range(2 * n)},
        compiler_params=_SPLIT_COPY,
    )(*xs, *lands, *send_sems, *recv_sems, after)
    return res[:n], res[n:]


_ADAMW_PARTS_BLOCK_BYTES = 8 * 1024 * 1024


def _adamw(parts, w, m, v, *, name, own=None):
    r, c = w.shape
    tr = r
    for cand in range(8, r + 1, 8):
        if r % cand == 0 and N_DEV * cand * c * 4 <= _ADAMW_PARTS_BLOCK_BYTES:
            tr = cand
    has_own = own is not None

    def body(*refs):
        p_ref, w_ref, m_ref, v_ref = refs[0], refs[1 + has_own], refs[2 + has_own], refs[3 + has_own]
        g_ref, d_ref, nm_ref, nv_ref = refs[4 + has_own:]
        if has_own:
            me = _slot(*_mesh_pos())
            part = lambda k: jnp.where(me == k, refs[1][...], p_ref[k])
        else:
            part = lambda k: p_ref[k]
        g = part(0)
        for k in range(1, N_DEV):
            g = g + part(k)
        nm = ADAM_B1 * m_ref[...] + (1.0 - ADAM_B1) * g
        nv = ADAM_B2 * v_ref[...] + (1.0 - ADAM_B2) * (g * g)
        m_hat = nm / (1.0 - ADAM_B1 ** ADAM_STEP)
        v_hat = nv / (1.0 - ADAM_B2 ** ADAM_STEP)
        g_ref[...] = g
        d_ref[...] = -ADAM_LR * (m_hat / (jnp.sqrt(v_hat) + ADAM_EPS) + ADAM_WD * w_ref[...])
        nm_ref[...] = nm
        nv_ref[...] = nv

    blk = pl.BlockSpec((tr, c), lambda i: (i, 0))
    return pl.pallas_call(
        body, grid=(r // tr,),
        in_specs=[pl.BlockSpec((N_DEV, tr, c), lambda i: (0, i, 0))] + [blk] * (3 + has_own),
        out_specs=[blk] * 4, out_shape=[jax.ShapeDtypeStruct((r, c), F32)] * 4, name=name,
        compiler_params=_params(("parallel",)),
    )(parts, *([own] if has_own else []), w, m, v)


_FLAT_COLS = 1024
_FLAT_ROW_ALIGN = 128

_SHARD_AXIS = {
    "norm_mix": None, "norm_ffn": None, "ev_w_in": 2, "ev_b_fgate": None, "ev_q_norm": None, "ev_k_norm": None,
    "ev_v_norm": None, "ev_w_spatial": None, "ev_b_spatial": None, "ev_w_out": 1, "od_a_re": None, "od_a_im": None,
    "od_log_dt": None, "od_b_re": None, "od_b_im": None, "od_c_re": None, "od_c_im": None, "od_d": 1, "od_w_glu": 2,
    "ffn_w_up": 2, "ffn_conv_w": 2, "ffn_conv_b": None, "ffn_w_down": 1, "ple_w_proj": 2, "ple_w_gate": 1,
}
_WEIGHTS = list(_SHARD_AXIS)
_REPLICATED = [n for n in _WEIGHTS if _SHARD_AXIS[n] is None]
_COL_SHARDED = ["ev_w_in", "od_w_glu", "ffn_w_up", "ple_w_proj"]
_ROW_SHARDED = ["ev_w_out", "ffn_w_down", "ple_w_gate"]
_SMALL_SHARDED = ["od_d", "ffn_conv_w"]


def _pack(arrays, lead=()):
    nl = len(lead)
    flat = jnp.concatenate([a.reshape(lead + (-1,)) for a in arrays], axis=nl)
    n = flat.shape[nl]
    chunk = _FLAT_COLS * _FLAT_ROW_ALIGN
    total = -(-n // chunk) * chunk
    flat = jnp.pad(flat, [(0, 0)] * nl + [(0, total - n)])
    return flat.reshape(lead + (total // _FLAT_COLS, _FLAT_COLS))


def _unpack(buf, shapes, lead=()):
    nl = len(lead)
    flat = buf.reshape(lead + (-1,))
    out, off = [], 0
    for shp in shapes:
        n = math.prod(shp)
        out.append(lax.slice_in_dim(flat, off, off + n, axis=nl).reshape(lead + tuple(shp)))
        off += n
    return out


def _join_shards(g, axis):
    g = jnp.moveaxis(g, 0, axis)
    shp = g.shape
    return g.reshape(shp[:axis] + (shp[axis] * shp[axis + 1],) + shp[axis + 2:])


def _split_shards(full, axis):
    shp = full.shape
    g = full.reshape(shp[:axis] + (N_DEV, shp[axis] // N_DEV) + shp[axis + 1:])
    return jnp.moveaxis(g, axis, 0)


def _local_step(x, p, tgt, w, before_layer, on_grads):
    saved = []
    h = x
    for i in range(DEPTH):
        x0 = before_layer(i, h)
        if i % 2 == 0:
            x1, sv_mix = _even_fwd(x0, w, i // 2, i)
        else:
            x1, sv_mix = _odd_fwd(x0, w, i // 2, i)
        x2, sv_ffn = _ffn_fwd(x1, w, i)
        x3, sv_ple = _ple_fwd(x2, p[i], w, i)
        saved.append((x0, x1, x2, sv_mix, sv_ffn, sv_ple))
        h = x3
    loss, dh = _loss_and_grad(h, tgt)
    per_layer = {}
    for i in reversed(range(DEPTH)):
        x0, x1, x2, sv_mix, sv_ffn, sv_ple = saved[i]
        dx2, dx2b, g_ple = _ple_bwd(dh, x2, p[i], sv_ple, w, i)
        dx1, dx1b, g_ffn = _ffn_bwd(dx2, dx2b, x1, sv_ffn, w, i)
        dx1 = on_grads(i, 0, {**g_ple, **g_ffn}, dx1)
        if i % 2 == 0:
            dh, g_mix = _even_bwd(dx1, dx1b, x0, sv_mix, w, i // 2, i)
        else:
            dh, g_mix = _odd_bwd(dx1, x0, sv_mix, w, i // 2, i)
        dh = on_grads(i, 1, g_mix, dh)
        layer_grads = {**g_ple, **g_ffn, **g_mix}
        for name, g in layer_grads.items():
            per_layer.setdefault(name, []).append(g)
    grads = {name: gs[::-1] for name, gs in per_layer.items()}
    return loss, dh, grads


def kernel(x, p, norm_mix, norm_ffn, ev_w_in, ev_b_fgate, ev_q_norm, ev_k_norm, ev_v_norm, ev_w_spatial, ev_b_spatial, ev_w_out, od_a_re, od_a_im, od_log_dt, od_b_re, od_b_im, od_c_re, od_c_im, od_d, od_w_glu, ffn_w_up, ffn_conv_w, ffn_conv_b, ffn_w_down, ple_w_proj, ple_w_gate, loss_target, m_norm_mix, m_norm_ffn, m_ev_w_in, m_ev_b_fgate, m_ev_q_norm, m_ev_k_norm, m_ev_v_norm, m_ev_w_spatial, m_ev_b_spatial, m_ev_w_out, m_od_a_re, m_od_a_im, m_od_log_dt, m_od_b_re, m_od_b_im, m_od_c_re, m_od_c_im, m_od_d, m_od_w_glu, m_ffn_w_up, m_ffn_conv_w, m_ffn_conv_b, m_ffn_w_down, m_ple_w_proj, m_ple_w_gate, v_norm_mix, v_norm_ffn, v_ev_w_in, v_ev_b_fgate, v_ev_q_norm, v_ev_k_norm, v_ev_v_norm, v_ev_w_spatial, v_ev_b_spatial, v_ev_w_out, v_od_a_re, v_od_a_im, v_od_log_dt, v_od_b_re, v_od_b_im, v_od_c_re, v_od_c_im, v_od_d, v_od_w_glu, v_ffn_w_up, v_ffn_conv_w, v_ffn_conv_b, v_ffn_w_down, v_ple_w_proj, v_ple_w_gate):
    args = locals()
    wl = {n: args[n] for n in _WEIGHTS}
    ml = {n: args["m_" + n] for n in _WEIGHTS}
    vl = {n: args["v_" + n] for n in _WEIGHTS}

    kinds = ("g", "d", "m", "v")

    def local(n, layer, a):
        return a[n][layer].T if n in _COL_SHARDED else a[n][layer]

    me = _slot(*_mesh_pos())
    w = {n: wl[n] for n in _REPLICATED}
    small = _all_gather(_pack([wl[n] for n in _SMALL_SHARDED]), name="gather_small_weights")
    for n, g in zip(_SMALL_SHARDED, _unpack(small, [wl[n].shape for n in _SMALL_SHARDED], lead=(N_DEV,))):
        w[n] = _join_shards(g, _SHARD_AXIS[n])

    def layer_of(n, i):
        return i if wl[n].shape[0] == DEPTH else i // 2

    def weights_of(i):
        mixer = ["ev_w_in", "ev_w_out"] if i % 2 == 0 else ["od_w_glu"]
        return mixer + ["ffn_w_up", "ffn_w_down", "ple_w_proj", "ple_w_gate"]

    gathers = {}
    for n in _COL_SHARDED + _ROW_SHARDED:
        w[n] = {}

    def start_gather(i, carry):
        blocks = [local(n, layer_of(n, i), wl).astype(BF16) for n in weights_of(i)]
        gathers[i], carry = _exchange_start(blocks, carry, gather=True, name="gather_start_layer%d" % i)
        return carry

    def finish_gather(i, h):
        sent, landed = _exchange_wait(gathers[i], h, gather=True, name="gather_wait_layer%d" % i)
        for n, block, land in zip(weights_of(i), sent, landed):
            full = lax.dynamic_update_slice(land, block[None], (me, 0, 0)).reshape(N_DEV * block.shape[0], block.shape[1])
            if n == "ev_w_in":
                full = jnp.pad(full, ((0, IN_COLS_PAD - IN_COLS), (0, 0)))
            w[n][layer_of(n, i)] = full
        if i == 0:
            for later in range(1, DEPTH):
                h = start_gather(later, h)
        return h

    x_in = start_gather(0, x[0])

    in_flight = {}

    def start_scatter(i, part, part_grads, dh):
        names = [n for n in _COL_SHARDED + _ROW_SHARDED if n in part_grads]
        xs = [part_grads[n].reshape(N_DEV, part_grads[n].shape[0] // N_DEV, part_grads[n].shape[1]) for n in names]
        handles, dh = _exchange_start(xs, dh, gather=False, name="scatter_start_layer%d_part%d" % (i, part))
        in_flight[i, part] = (names, handles)
        return dh

    loss, grad_x, grads = _local_step(x_in, p[:, 0], loss_target[0], w, finish_gather, start_scatter)
    loss = lax.psum(loss, ("x", "y", "c"))

    out = {kind: {} for kind in kinds}
    done = {}
    for i, part in sorted(in_flight, reverse=True):
        names, handles = in_flight[i, part]
        sent, landed = _exchange_wait(handles, grad_x, gather=False, name="scatter_wait_layer%d_part%d" % (i, part))
        for n, x_sent, parts in zip(names, sent, landed):
            layer = layer_of(n, i)
            own = lax.dynamic_index_in_dim(x_sent, me, axis=0, keepdims=False)
            res = _adamw(parts, local(n, layer, wl), local(n, layer, ml), local(n, layer, vl), name="adamw_" + n, own=own)
            done.setdefault(n, {})[layer] = [r.T if n in _COL_SHARDED else r for r in res]
    for n, layers in done.items():
        for k, kind in enumerate(kinds):
            out[kind][n] = jnp.stack([layers[layer][k] for layer in sorted(layers)])

    send = _pack([_split_shards(jnp.stack(grads[n]), _SHARD_AXIS[n]) for n in _SMALL_SHARDED], lead=(N_DEV,))
    parts = _all_to_all(send, name="scatter_small_grads")
    shapes = [wl[n].shape for n in _SMALL_SHARDED]
    res = _adamw(parts, _pack([wl[n] for n in _SMALL_SHARDED]), _pack([ml[n] for n in _SMALL_SHARDED]),
                 _pack([vl[n] for n in _SMALL_SHARDED]), name="adamw_small")
    for kind, buf in zip(kinds, res):
        out[kind].update(zip(_SMALL_SHARDED, _unpack(buf, shapes)))

    parts = _all_gather(_pack([jnp.stack(grads[n]) for n in _REPLICATED]), name="gather_replicated_grads")
    shapes = [wl[n].shape for n in _REPLICATED]
    res = _adamw(parts, _pack([wl[n] for n in _REPLICATED]), _pack([ml[n] for n in _REPLICATED]),
                 _pack([vl[n] for n in _REPLICATED]), name="adamw_replicated")
    for kind, buf in zip(kinds, res):
        out[kind].update(zip(_REPLICATED, _unpack(buf, shapes)))

    return (loss, grad_x[None], *[out["g"][n] for n in _WEIGHTS], *[out["d"][n] for n in _WEIGHTS],
            *[out["m"][n] for n in _WEIGHTS], *[out["v"][n] for n in _WEIGHTS])
```

```python
import functools
import math

import jax
import jax.numpy as jnp
from jax import lax
from jax.experimental import pallas as pl
from jax.experimental.pallas import tpu as pltpu

F32 = jnp.float32
BF16 = jnp.bfloat16

D_MODEL = 1024
DEPTH = 4
A_GROUPS = 4
A_CHUNK = 128
A_WIDTH = 512
B_HEADS = 8
B_HEAD_DIM = 64
B_WIDTH = 512
IN_COLS = 2 * A_WIDTH + 3 * B_WIDTH + B_HEADS
IN_COLS_PAD = 2688
S5_GROUP_CH = 16
S5_GROUPS = 64
S5_STATE = 64
S5_N = S5_GROUPS * S5_STATE
S5_BLOCKS = 8
D_FF = 2816
PLE_DIM = 256
EPS = 1e-6
NEG_INF = -1e30
N_DEV = 8

ADAM_LR = 0.001
ADAM_B1 = 0.9
ADAM_B2 = 0.999
ADAM_EPS = 1e-08
ADAM_WD = 0.01
ADAM_STEP = 10

VMEM_LIMIT_BYTES = 56 * 1024 * 1024
MESH = pl.DeviceIdType.MESH


def _params(sem, vmem=VMEM_LIMIT_BYTES):
    return pltpu.CompilerParams(dimension_semantics=sem, vmem_limit_bytes=vmem)


_GELU_K = math.sqrt(2.0 / math.pi)


def _gelu(x):
    return x * (0.5 * (1.0 + jnp.tanh(_GELU_K * (x + 0.044715 * (x * x * x)))))


def _gelu_grad(x):
    t = jnp.tanh(_GELU_K * (x + 0.044715 * (x * x * x)))
    return 0.5 * (1.0 + t) + 0.5 * x * (1.0 - t * t) * (_GELU_K * (1.0 + 3.0 * 0.044715 * (x * x)))


def _sigmoid(x):
    return 0.5 * jnp.tanh(0.5 * x) + 0.5


def _rstd(x):
    return lax.rsqrt(jnp.mean(x * x, axis=-1, keepdims=True) + EPS)


def _rms_bwd(x, dy, g):
    r = _rstd(x)
    xh = x * r
    dyg = dy if g is None else dy * g
    dx = r * (dyg - xh * jnp.mean(dyg * xh, axis=-1, keepdims=True))
    return dx, dy * xh


def _shift_down(blk, halo, k):
    tr = blk.shape[0]
    r = pltpu.roll(blk, k, 0)
    hr = pltpu.roll(halo, k, 0)
    first = jnp.where(lax.broadcasted_iota(jnp.int32, hr.shape, 0) < k, hr, r[0:8])
    return jnp.concatenate([first, r[8:tr]], axis=0)


def _shift_up(blk, halo, k):
    tr = blk.shape[0]
    r = pltpu.roll(blk, tr - k, 0)
    hr = pltpu.roll(halo, 8 - k, 0)
    last = jnp.where(lax.broadcasted_iota(jnp.int32, hr.shape, 0) >= 8 - k, hr, r[tr - 8:tr])
    return jnp.concatenate([r[0:tr - 8], last], axis=0)


def _rw(fn, rows, consts, outs, accs=(), *, tr, name, prev=(), nxt=(), widths=None):
    s = rows[0].shape[0]
    tr = min(tr, s)
    n = s // tr
    nr, nc, npv, nnx, no, na = len(rows), len(consts), len(prev), len(nxt), len(outs), len(accs)
    widths = widths or [None] * nr

    def body(*refs):
        ins, out_refs = refs[:nr + nc + npv + nnx], refs[nr + nc + npv + nnx:]
        i = pl.program_id(0)
        vals = [r[...] for r in ins[:nr + nc]]
        vals += [jnp.where(i == 0, 0.0, r[...]) for r in ins[nr + nc:nr + nc + npv]]
        vals += [jnp.where(i == n - 1, 0.0, r[...]) for r in ins[nr + nc + npv:]]
        res = fn(*vals)
        if not isinstance(res, (tuple, list)):
            res = (res,)
        for k in range(no):
            out_refs[k][...] = res[k].astype(out_refs[k].dtype)
        if na:
            @pl.when(i == 0)
            def _():
                for k in range(na):
                    out_refs[no + k][...] = jnp.zeros_like(out_refs[no + k])

            for k in range(na):
                out_refs[no + k][...] += res[no + k]

    in_specs = []
    for a, w in zip(rows, widths):
        if w is None:
            in_specs.append(pl.BlockSpec((tr, a.shape[1]), lambda i: (i, 0)))
        else:
            in_specs.append(pl.BlockSpec((tr, w[1]), functools.partial(lambda i, cb: (i, cb), cb=w[0])))
    for c in consts:
        in_specs.append(pl.BlockSpec(c.shape, functools.partial(lambda i, nd: (0,) * nd, nd=c.ndim)))
    t8 = tr // 8
    for a in prev:
        in_specs.append(pl.BlockSpec((8, a.shape[1]), lambda i: (jnp.maximum(i * t8 - 1, 0), 0)))
    for a in nxt:
        in_specs.append(pl.BlockSpec((8, a.shape[1]), lambda i: (jnp.minimum((i + 1) * t8, s // 8 - 1), 0)))
    out_shape = [jax.ShapeDtypeStruct((s, w), dt) for w, dt in outs]
    out_specs = [pl.BlockSpec((tr, w), lambda i: (i, 0)) for w, _ in outs]
    out_shape += [jax.ShapeDtypeStruct(a, F32) for a in accs]
    out_specs += [pl.BlockSpec(a, lambda i: (0, 0)) for a in accs]
    res = pl.pallas_call(
        body, grid=(n,), in_specs=in_specs, out_specs=out_specs, out_shape=out_shape, name=name,
        compiler_params=_params(("arbitrary",) if na else ("parallel",)),
    )(*rows, *consts, *prev, *nxt)
    return res


def _pick(n, cap):
    if n <= cap:
        return n
    best = None
    for d in range(128, cap + 1, 128):
        if n % d == 0:
            best = d
    assert best is not None, (n, cap)
    return best


def _mm(a, b, *, ta=False, tb=False, out_dtype=F32, add=None, epilogue=None, name, bm=1024, bn=512, bk=None):
    (k_dim, m) = a.shape if ta else a.shape[::-1]
    n = b.shape[0] if tb else b.shape[1]
    assert (b.shape[1] if tb else b.shape[0]) == k_dim
    if bk is None:
        bk = 1024 if ta else 2816
    bm, bn, bk = _pick(m, bm), _pick(n, bn), _pick(k_dim, bk)
    nk = k_dim // bk
    dims = (((0 if ta else 1,), (1 if tb else 0,)), ((), ()))
    assert add is None or epilogue is None
    epi_fn, extra, out_dtypes = epilogue or (None, [add] if add is not None else [], (out_dtype,))
    n_extra, n_out = len(extra), len(out_dtypes)

    def body(*refs):
        a_ref, b_ref = refs[0], refs[1]
        extra_refs = refs[2:2 + n_extra]
        out_refs = refs[2 + n_extra:2 + n_extra + n_out]
        p = lax.dot_general(a_ref[...].astype(BF16), b_ref[...].astype(BF16), dims, preferred_element_type=F32)

        def finish(acc):
            if epi_fn is not None:
                res = epi_fn(acc, *[r[...] for r in extra_refs])
            else:
                res = (acc + extra_refs[0][...],) if n_extra else (acc,)
            for o_ref, r in zip(out_refs, res):
                o_ref[...] = r.astype(o_ref.dtype)

        if nk == 1:
            finish(p)
        else:
            acc_ref = refs[2 + n_extra + n_out]
            kk = pl.program_id(2)

            @pl.when(kk == 0)
            def _():
                acc_ref[...] = p

            @pl.when(kk > 0)
            def _():
                acc_ref[...] += p

            @pl.when(kk == nk - 1)
            def _():
                finish(acc_ref[...])

    a_spec = pl.BlockSpec((bk, bm), lambda i, j, k: (k, i)) if ta else pl.BlockSpec((bm, bk), lambda i, j, k: (i, k))
    b_spec = pl.BlockSpec((bn, bk), lambda i, j, k: (j, k)) if tb else pl.BlockSpec((bk, bn), lambda i, j, k: (k, j))
    tile = pl.BlockSpec((bm, bn), lambda i, j, k: (i, j))
    res = pl.pallas_call(
        body, grid=(m // bm, n // bn, nk), in_specs=[a_spec, b_spec] + [tile] * n_extra,
        out_specs=[tile] * n_out, out_shape=[jax.ShapeDtypeStruct((m, n), dt) for dt in out_dtypes],
        scratch_shapes=[pltpu.VMEM((bm, bn), F32)] if nk > 1 else [],
        name=name, compiler_params=_params(("parallel", "parallel", "arbitrary")),
    )(a, b, *extra)
    return res if epilogue is not None else res[0]


def _rms_fwd(x, g, *, name):
    if g is None:
        return _rw(lambda xv: xv * _rstd(xv), [x], [], [(x.shape[1], BF16)], tr=512, name=name)[0]
    return _rw(lambda xv, gv: xv * _rstd(xv) * gv, [x], [g], [(x.shape[1], BF16)], tr=512, name=name)[0]


def _rms_bwd_call(x, dy, g, dx_add, *, name, tr=512, scale=None, bf16_copy=False):
    w = x.shape[1]
    has_g, has_add = g is not None, dx_add is not None

    def fn(*v):
        xv, dyv = v[0], v[1]
        if scale is not None:
            dyv = dyv * scale
        gv = v[2 + has_add] if has_g else None
        dx, dg = _rms_bwd(xv, dyv, gv)
        if has_add:
            dx = dx + v[2]
        return (dx,) + ((dx,) if bf16_copy else ()) + ((jnp.sum(dg, axis=0, keepdims=True),) if has_g else ())

    rows = [x, dy] + ([dx_add] if has_add else [])
    outs = [(w, F32)] + ([(w, BF16)] if bf16_copy else [])
    res = _rw(fn, rows, [g] if has_g else [], outs, [(1, w)] if has_g else [], tr=tr, name=name)
    return tuple(res) if has_g else tuple(res) + (None,)


def _conv_taps(hup, halo, cw):
    h1 = _shift_down(hup, halo, 1)
    h2 = _shift_down(hup, halo, 2)
    return h1, h2


def _ffn_fwd(x1, w, i):
    h2b = _rms_fwd(x1, w["norm_ffn"][i:i + 1], name="ffn_norm")
    hup = _mm(h2b, w["ffn_w_up"][i], tb=True, name="ffn_up", bn=1408)
    cw, cb = w["ffn_conv_w"][i], w["ffn_conv_b"][i:i + 1]

    def fn(hv, cwv, cbv, halo):
        h1, h2 = _conv_taps(hv, halo, cwv)
        hc = cbv + cwv[0:1] * h2
        hc = hc + cwv[1:2] * h1
        hc = hc + cwv[2:3] * hv
        g, up = hc[:, :D_FF], hc[:, D_FF:]
        return (g * _sigmoid(g)) * up

    a = _rw(fn, [hup], [cw, cb], [(D_FF, BF16)], tr=128, name="ffn_conv_gate", prev=[hup])[0]
    x2 = _mm(a, w["ffn_w_down"][i], add=x1, name="ffn_down")
    return x2, (h2b, hup, a)


def _ffn_bwd(dx2, dx2b, x1, saved, w, i):
    h2b, hup, a = saved
    cw, cb = w["ffn_conv_w"][i], w["ffn_conv_b"][i:i + 1]
    da = _mm(dx2b, w["ffn_w_down"][i], tb=True, name="ffn_down_dx", bn=1408)
    d_wdown = _mm(a, dx2b, ta=True, name="ffn_down_dw", bm=1408)

    def fn1(hv, dav, cwv, cbv, h_prev, h_next, da_next):
        tr = hv.shape[0]
        he = jnp.concatenate([hv, h_next], axis=0)
        dae = jnp.concatenate([dav, da_next], axis=0)
        h1, h2 = _conv_taps(he, h_prev, cwv)
        hc = cbv + cwv[0:1] * h2
        hc = hc + cwv[1:2] * h1
        hc = hc + cwv[2:3] * he
        g, up = hc[:, :D_FF], hc[:, D_FF:]
        sg = _sigmoid(g)
        d_up = dae * (g * sg)
        d_g = dae * up * (sg * (1.0 + g * (1.0 - sg)))
        d_hce = jnp.concatenate([d_g, d_up], axis=1)
        rows_e = tr + 8
        d_hup = (cwv[2:3] * d_hce[:tr] + cwv[1:2] * pltpu.roll(d_hce, rows_e - 1, 0)[:tr]
                 + cwv[0:1] * pltpu.roll(d_hce, rows_e - 2, 0)[:tr])
        d_hc = d_hce[:tr]
        col = lambda v: jnp.sum(v, axis=0, keepdims=True)
        return d_hup, col(d_hc), col(d_hc * h2[:tr]), col(d_hc * h1[:tr]), col(d_hc * hv)

    w2 = 2 * D_FF
    d_hup, d_cb, d_cw0, d_cw1, d_cw2 = _rw(fn1, [hup, da], [cw, cb], [(w2, BF16)], [(1, w2)] * 4, tr=128,
                                           name="ffn_conv_gate_bwd", prev=[hup], nxt=[hup, da])
    d_h2 = _mm(d_hup, w["ffn_w_up"][i], name="ffn_up_dx")
    d_wup_t = _mm(h2b, d_hup, ta=True, name="ffn_up_dw", bn=1408).T
    dx1, dx1b, d_norm = _rms_bwd_call(x1, d_h2, w["norm_ffn"][i:i + 1], dx2, name="ffn_norm_bwd", bf16_copy=True)
    grads = {"ffn_w_up": d_wup_t, "ffn_w_down": d_wdown, "ffn_conv_b": d_cb[0],
             "ffn_conv_w": jnp.concatenate([d_cw0, d_cw1, d_cw2], axis=0), "norm_ffn": d_norm[0]}
    return dx1, dx1b, grads


def _ple_fwd(x2, p_i, w, i):
    rn = _rms_fwd(x2, None, name="ple_norm")
    gl = _mm(rn, w["ple_w_gate"][i], name="ple_gate")
    x3 = _mm(p_i, w["ple_w_proj"][i], tb=True, name="ple_proj",
             epilogue=(lambda pp, xv, g: (xv + _sigmoid(g) * pp,), [x2, gl], (F32,)))[0]
    return x3, (rn, gl)


def _ple_bwd(dx3, x2, p_i, saved, w, i):
    rn, gl = saved

    def fn(pp, dv, g):
        sg = _sigmoid(g)
        return dv * sg, dv * pp * (sg * (1.0 - sg))

    d_pp, d_pre = _mm(p_i, w["ple_w_proj"][i], tb=True, name="ple_proj_bwd", epilogue=(fn, [dx3, gl], (BF16, BF16)))
    d_wproj_t = _mm(p_i, d_pp, ta=True, name="ple_proj_dw", bn=1024).T
    d_wgate = _mm(rn, d_pre, ta=True, name="ple_gate_dw")
    d_rn = _mm(d_pre, w["ple_w_gate"][i], tb=True, name="ple_gate_dx")
    dx2, dx2b, _ = _rms_bwd_call(x2, d_rn, None, dx3, name="ple_norm_bwd", bf16_copy=True)
    return dx2, dx2b, {"ple_w_proj": d_wproj_t, "ple_w_gate": d_wgate}


def _loss_and_grad(xf, tgt):
    def fn(xv, tv):
        d = xv - tv
        part = 0.5 * jnp.sum(jnp.mean(d * d, axis=-1, keepdims=True), axis=0, keepdims=True)
        return d * (1.0 / D_MODEL), part

    dx, loss = _rw(fn, [xf, tgt], [], [(D_MODEL, F32)], [(1, 1)], tr=512, name="loss_head")
    return loss[0, 0], dx


def _tril_mask():
    shp = (A_CHUNK, A_CHUNK)
    return lax.broadcasted_iota(jnp.int32, shp, 0) >= lax.broadcasted_iota(jnp.int32, shp, 1)


def _gmlp_fwd(z, w_s, b_s, v_gain, *, tr=512):
    s = z.shape[0]
    tr = min(tr, s)
    gw = A_WIDTH // A_GROUPS

    def body(z_ref, w_ref, b_ref, g_ref, y_ref):
        tril = _tril_mask()
        for g in range(A_GROUPS):
            wg = jnp.where(tril, w_ref[g], 0.0).astype(BF16)
            cols = slice(g * gw, (g + 1) * gw)
            vcols = slice(A_WIDTH + g * gw, A_WIDTH + (g + 1) * gw)
            for c in range(tr // A_CHUNK):
                rows = slice(c * A_CHUNK, (c + 1) * A_CHUNK)
                v = _gelu(z_ref[rows, vcols])
                vn = v * _rstd(v) * g_ref[:, cols]
                sv = jnp.dot(wg, vn.astype(BF16), preferred_element_type=F32) + b_ref[g]
                y_ref[rows, cols] = (_gelu(z_ref[rows, cols]) * sv).astype(BF16)

    return pl.pallas_call(
        body, grid=(s // tr,),
        in_specs=[pl.BlockSpec((tr, 2 * A_WIDTH), lambda i: (i, 0)),
                  pl.BlockSpec(w_s.shape, lambda i: (0, 0, 0)), pl.BlockSpec(b_s.shape, lambda i: (0, 0, 0)),
                  pl.BlockSpec(v_gain.shape, lambda i: (0, 0))],
        out_specs=pl.BlockSpec((tr, A_WIDTH), lambda i: (i, 0)),
        out_shape=jax.ShapeDtypeStruct((s, A_WIDTH + B_WIDTH), BF16), name="gmlp_fwd",
        compiler_params=_params(("parallel",)),
    )(z, w_s, b_s, v_gain)


def _gmlp_bwd(z, d_ymix, w_s, b_s, v_gain, dz, *, tr=512):
    s = z.shape[0]
    tr = min(tr, s)
    gw = A_WIDTH // A_GROUPS

    def body(z_ref, dy_ref, w_ref, b_ref, g_ref, _, dz_ref, dw_ref, db_ref, dg_ref):
        @pl.when(pl.program_id(0) == 0)
        def _():
            dw_ref[...] = jnp.zeros_like(dw_ref)
            db_ref[...] = jnp.zeros_like(db_ref)
            dg_ref[...] = jnp.zeros_like(dg_ref)

        tril = _tril_mask()
        for g in range(A_GROUPS):
            wg = jnp.where(tril, w_ref[g], 0.0).astype(BF16)
            cols = slice(g * gw, (g + 1) * gw)
            vcols = slice(A_WIDTH + g * gw, A_WIDTH + (g + 1) * gw)
            gain = g_ref[:, cols]
            for c in range(tr // A_CHUNK):
                rows = slice(c * A_CHUNK, (c + 1) * A_CHUNK)
                va, ua = z_ref[rows, vcols], z_ref[rows, cols]
                v = _gelu(va)
                r = _rstd(v)
                vh = v * r
                vnb = (vh * gain).astype(BF16)
                sv = jnp.dot(wg, vnb, preferred_element_type=F32) + b_ref[g]
                dy = dy_ref[rows, cols]
                d_sv = dy * _gelu(ua)
                dz_ref[rows, cols] = (dy * sv * _gelu_grad(ua)).astype(BF16)
                d_svb = d_sv.astype(BF16)
                d_vn = lax.dot_general(wg, d_svb, (((0,), (0,)), ((), ())), preferred_element_type=F32)
                dwp = lax.dot_general(d_svb, vnb, (((1,), (1,)), ((), ())), preferred_element_type=F32)
                dw_ref[g] += jnp.where(tril, dwp, 0.0)
                db_ref[g] += jnp.sum(d_sv, axis=1, keepdims=True)
                dg_ref[:, cols] += jnp.sum(d_vn * vh, axis=0, keepdims=True)
                d_vh = d_vn * gain
                d_v = r * (d_vh - vh * jnp.mean(d_vh * vh, axis=-1, keepdims=True))
                dz_ref[rows, vcols] = (d_v * _gelu_grad(va)).astype(BF16)

    return pl.pallas_call(
        body, grid=(s // tr,),
        in_specs=[pl.BlockSpec((tr, 2 * A_WIDTH), lambda i: (i, 0)), pl.BlockSpec((tr, A_WIDTH), lambda i: (i, 0)),
                  pl.BlockSpec(w_s.shape, lambda i: (0, 0, 0)), pl.BlockSpec(b_s.shape, lambda i: (0, 0, 0)),
                  pl.BlockSpec(v_gain.shape, lambda i: (0, 0)), _ANY],
        out_specs=[pl.BlockSpec((tr, 2 * A_WIDTH), lambda i: (i, 0)), pl.BlockSpec(w_s.shape, lambda i: (0, 0, 0)),
                   pl.BlockSpec(b_s.shape, lambda i: (0, 0, 0)), pl.BlockSpec(v_gain.shape, lambda i: (0, 0))],
        out_shape=[jax.ShapeDtypeStruct(dz.shape, dz.dtype), jax.ShapeDtypeStruct(w_s.shape, F32),
                   jax.ShapeDtypeStruct(b_s.shape, F32), jax.ShapeDtypeStruct(v_gain.shape, F32)],
        input_output_aliases={5: 0}, name="gmlp_bwd", compiler_params=_params(("arbitrary",)),
    )(z, d_ymix, w_s, b_s, v_gain, dz)


def _dot3(x, ub):
    x1 = x.astype(BF16)
    r1 = x - x1.astype(F32)
    x2 = r1.astype(BF16)
    x3 = (r1 - x2.astype(F32)).astype(BF16)
    d = lambda a: jnp.dot(a, ub, preferred_element_type=F32)
    return d(x1) + d(x2) + d(x3)


def _log_sigmoid(x):
    return jnp.minimum(x, 0.0) - jnp.log(1.0 + jnp.exp(-jnp.abs(x)))


def _fgate_fwd(f_t, b_col, *, tb=256):
    h, s = f_t.shape
    tb = min(tb, s)

    def body(f_ref, b_ref, c_ref, carry):
        @pl.when(pl.program_id(0) == 0)
        def _():
            carry[...] = jnp.zeros_like(carry)

        lf = _log_sigmoid(f_ref[...] + b_ref[...])
        shp = (tb, tb)
        upper = (lax.broadcasted_iota(jnp.int32, shp, 0) <= lax.broadcasted_iota(jnp.int32, shp, 1)).astype(BF16)
        c_ref[...] = _dot3(lf, upper) + carry[...]
        carry[...] += jnp.sum(lf, axis=1, keepdims=True)

    return pl.pallas_call(
        body, grid=(s // tb,),
        in_specs=[pl.BlockSpec((h, tb), lambda i: (0, i)), pl.BlockSpec((h, 1), lambda i: (0, 0))],
        out_specs=pl.BlockSpec((h, tb), lambda i: (0, i)), out_shape=jax.ShapeDtypeStruct((h, s), F32),
        scratch_shapes=[pltpu.VMEM((h, 1), F32)], name="fgate_fwd", compiler_params=_params(("arbitrary",)),
    )(f_t, b_col)


def _fgate_bwd(f_t, b_col, dc_a, dc_b, *, tb=256):
    h, s = f_t.shape
    tb = min(tb, s)
    n = s // tb

    def body(f_ref, b_ref, da_ref, db_ref, df_ref, dbias_ref, carry):
        @pl.when(pl.program_id(0) == 0)
        def _():
            carry[...] = jnp.zeros_like(carry)
            dbias_ref[...] = jnp.zeros_like(dbias_ref)

        dc = da_ref[...] + db_ref[...]
        shp = (tb, tb)
        lower = (lax.broadcasted_iota(jnp.int32, shp, 0) >= lax.broadcasted_iota(jnp.int32, shp, 1)).astype(BF16)
        d_lf = _dot3(dc, lower) + carry[...]
        carry[...] += jnp.sum(dc, axis=1, keepdims=True)
        df = d_lf * (1.0 - _sigmoid(f_ref[...] + b_ref[...]))
        df_ref[...] = df
        dbias_ref[...] += jnp.sum(df, axis=1, keepdims=True)

    blk = pl.BlockSpec((h, tb), lambda i: (0, n - 1 - i))
    return pl.pallas_call(
        body, grid=(n,), in_specs=[blk, pl.BlockSpec((h, 1), lambda i: (0, 0)), blk, blk],
        out_specs=[blk, pl.BlockSpec((h, 1), lambda i: (0, 0))],
        out_shape=[jax.ShapeDtypeStruct((h, s), F32), jax.ShapeDtypeStruct((h, 1), F32)],
        scratch_shapes=[pltpu.VMEM((h, 1), F32)], name="fgate_bwd", compiler_params=_params(("arbitrary",)),
    )(f_t, b_col, dc_a, dc_b)


_NT = (((1,), (1,)), ((), ()))


def _causal(shape, row0, col0, transposed):
    r = lax.broadcasted_iota(jnp.int32, shape, 0) + row0
    c = lax.broadcasted_iota(jnp.int32, shape, 1) + col0
    return (r <= c) if transposed else (c <= r)


_ATT_SCALE = B_HEAD_DIM ** -0.5
ATT_W = 128
_COL_CQ, _COL_CK, _COL_LSE, _COL_DELTA = 64, 67, 70, 64
_ATT_BLK = 2048
_ATT_SUB = 512


def _split3(x):
    h = x.astype(BF16).astype(F32)
    r = x - h
    m = r.astype(BF16).astype(F32)
    return h, m, (r - m).astype(BF16).astype(F32)


def _put_cols(base, lane, col0, parts):
    for t, pv in enumerate(parts):
        base = jnp.where(lane == col0 + t, pv, base)
    return base


_HEAD_PAIRS = B_HEADS // 2
_Q_BLOCK, _K_BLOCK, _V_BLOCK = 8, 12, 16
_F_BLOCK = 20
_O_BLOCK = 4
_ANY = pl.BlockSpec(memory_space=pl.ANY)


def _head_half(ref, h):
    x = ref[...]
    x = jnp.where((h & 1) == 1, pltpu.roll(x, B_HEAD_DIM, 1), x)
    return jnp.where(lax.broadcasted_iota(jnp.int32, x.shape, 1) < B_HEAD_DIM, x, 0.0)


def _pair(a, b):
    lane = lax.broadcasted_iota(jnp.int32, a.shape, 1)
    return jnp.where(lane < B_HEAD_DIM, a, pltpu.roll(b, B_HEAD_DIM, 1))


def _attn_prep(z, c_col, gq, gk, *, tr=1024):
    s = z.shape[0]
    tr = min(tr, s)
    n = s // tr

    def body(q_ref, k_ref, v_ref, c_ref, gq_ref, gk_ref, qo_ref, ko_ref, vo_ref):
        h = pl.program_id(0)
        q, k, v = _head_half(q_ref, h), _head_half(k_ref, h), _head_half(v_ref, h)
        lane = lax.broadcasted_iota(jnp.int32, q.shape, 1)
        rq = lax.rsqrt(jnp.sum(q * q, axis=-1, keepdims=True) * (1.0 / B_HEAD_DIM) + EPS)
        rk = lax.rsqrt(jnp.sum(k * k, axis=-1, keepdims=True) * (1.0 / B_HEAD_DIM) + EPS)
        c = c_ref[...]
        ch, cm, cl = _split3(c)
        one = jnp.ones_like(c)
        qq = _put_cols(q * rq * gq_ref[...] * _ATT_SCALE, lane, _COL_CQ, (ch, cm, cl))
        qq = _put_cols(qq, lane, _COL_CK, (one, one, one))
        kk = _put_cols(k * rk * gk_ref[...], lane, _COL_CQ, (one, one, one))
        kk = _put_cols(kk, lane, _COL_CK, (-ch, -cm, -cl))
        kk = _put_cols(kk, lane, _COL_LSE, (one, one, one))
        qo_ref[...] = qq.astype(BF16)
        ko_ref[...] = kk.astype(BF16)
        vo_ref[...] = _put_cols(v, lane, _COL_DELTA, (one, one, one)).astype(BF16)

    def zcol(base):
        return pl.BlockSpec((tr, ATT_W), lambda h, i: (i, base + (h >> 1)))

    rows = pl.BlockSpec((tr, ATT_W), lambda h, i: (h * n + i, 0))
    gain = pl.BlockSpec((1, ATT_W), lambda h, i: (0, 0))
    return pl.pallas_call(
        body, grid=(B_HEADS, n),
        in_specs=[zcol(_Q_BLOCK), zcol(_K_BLOCK), zcol(_V_BLOCK), pl.BlockSpec((tr, 1), lambda h, i: (h * n + i, 0)),
                  gain, gain],
        out_specs=[rows] * 3, out_shape=[jax.ShapeDtypeStruct((B_HEADS * s, ATT_W), BF16)] * 3, name="attn_prep",
        compiler_params=_params(("parallel", "parallel")),
    )(z, z, z, c_col, gq, gk)


def _attn_bwd_prep(d_ymix, o, qp, lse, *, tr=1024):
    s = d_ymix.shape[0]
    tr = min(tr, s)
    n = s // tr

    def body(do_ref, o_ref, q_ref, l_ref, dop_ref, qpp_ref):
        dov = _head_half(do_ref, pl.program_id(0))
        lane = lax.broadcasted_iota(jnp.int32, dov.shape, 1)
        dh, dm, dl = _split3(jnp.sum(dov * o_ref[...], axis=-1, keepdims=True))
        lh, lm, ll = _split3(l_ref[...])
        dop_ref[...] = _put_cols(dov, lane, _COL_DELTA, (-dh, -dm, -dl)).astype(BF16)
        qpp_ref[...] = _put_cols(q_ref[...].astype(F32), lane, _COL_LSE, (-lh, -lm, -ll)).astype(BF16)

    rows = pl.BlockSpec((tr, ATT_W), lambda h, i: (h * n + i, 0))
    return pl.pallas_call(
        body, grid=(B_HEADS, n),
        in_specs=[pl.BlockSpec((tr, ATT_W), lambda h, i: (i, _O_BLOCK + (h >> 1))), rows, rows,
                  pl.BlockSpec((tr, 1), lambda h, i: (h * n + i, 0))],
        out_specs=[rows] * 2, out_shape=[jax.ShapeDtypeStruct((B_HEADS * s, ATT_W), BF16)] * 2,
        name="attn_bwd_prep", compiler_params=_params(("parallel", "parallel")),
    )(d_ymix, o, qp, lse)


def _attn_merge_out(o, ymix, *, tr=1024):
    s = o.shape[1]
    tr = min(tr, s)

    def body(a_ref, b_ref, buf_ref, out_ref):
        out_ref[...] = _pair(a_ref[0], b_ref[0]).astype(BF16)

    return pl.pallas_call(
        body, grid=(_HEAD_PAIRS, s // tr),
        in_specs=[pl.BlockSpec((1, tr, ATT_W), lambda j, i: (2 * j, i, 0)),
                  pl.BlockSpec((1, tr, ATT_W), lambda j, i: (2 * j + 1, i, 0)), _ANY],
        out_specs=pl.BlockSpec((tr, ATT_W), lambda j, i: (i, _O_BLOCK + j)),
        out_shape=jax.ShapeDtypeStruct(ymix.shape, ymix.dtype), input_output_aliases={2: 0}, name="attn_merge_out",
        compiler_params=_params(("parallel", "parallel")),
    )(o, o, ymix)


def _pair_bwd(d_heads, dz, out_block, *, z=None, in_block=None, gain=None, scale=None, name, tr=1024):
    s = d_heads.shape[1]
    tr = min(tr, s)
    norm = gain is not None

    def body(*refs):
        a_ref, b_ref = refs[0], refs[1]
        d = _pair(a_ref[0], b_ref[0])
        if norm:
            x_ref, g_ref, _, out_ref, dg_ref = refs[2:]

            @pl.when((pl.program_id(0) == 0) & (pl.program_id(1) == 0))
            def _():
                dg_ref[...] = jnp.zeros_like(dg_ref)

            x = x_ref[...]
            lo = lax.broadcasted_iota(jnp.int32, x.shape, 1) < B_HEAD_DIM

            def half_mean(v):
                s_lo = jnp.sum(jnp.where(lo, v, 0.0), axis=-1, keepdims=True)
                s_hi = jnp.sum(jnp.where(lo, 0.0, v), axis=-1, keepdims=True)
                return jnp.where(lo, s_lo, s_hi) * (1.0 / B_HEAD_DIM)

            r = lax.rsqrt(half_mean(x * x) + EPS)
            xh = x * r
            dy = d * scale if scale is not None else d
            dyg = dy * g_ref[...]
            d = r * (dyg - xh * half_mean(dyg * xh))
            dg_ref[...] += jnp.sum(dy * xh, axis=0, keepdims=True)
        else:
            out_ref = refs[3]
        out_ref[...] = d.astype(BF16)

    heads = [pl.BlockSpec((1, tr, ATT_W), lambda j, i: (2 * j, i, 0)),
             pl.BlockSpec((1, tr, ATT_W), lambda j, i: (2 * j + 1, i, 0))]
    out_spec = pl.BlockSpec((tr, ATT_W), lambda j, i: (i, out_block + j))
    dz_shape = jax.ShapeDtypeStruct(dz.shape, dz.dtype)
    if norm:
        res = pl.pallas_call(
            body, grid=(_HEAD_PAIRS, s // tr),
            in_specs=heads + [pl.BlockSpec((tr, ATT_W), lambda j, i: (i, in_block + j)),
                              pl.BlockSpec((1, ATT_W), lambda j, i: (0, 0)), _ANY],
            out_specs=[out_spec, pl.BlockSpec((1, ATT_W), lambda j, i: (0, 0))],
            out_shape=[dz_shape, jax.ShapeDtypeStruct((1, ATT_W), F32)], input_output_aliases={4: 0}, name=name,
            compiler_params=_params(("arbitrary", "arbitrary")),
        )(d_heads, d_heads, z, gain, dz)
        return res[0], res[1]
    return pl.pallas_call(
        body, grid=(_HEAD_PAIRS, s // tr), in_specs=heads + [_ANY], out_specs=out_spec, out_shape=dz_shape,
        input_output_aliases={2: 0}, name=name, compiler_params=_params(("parallel", "parallel")),
    )(d_heads, d_heads, dz), None


def _attn_fwd(qp, kp, vp, *, blk=_ATT_BLK):
    h, s, d = qp.shape
    b = min(blk, s)
    n = s // b
    sub = min(_ATT_SUB, b)
    ns = b // sub

    def body(q_ref, k_ref, v_ref, o_ref, lse_ref):
        i = pl.program_id(1)
        qs = [q_ref[0, t * sub:(t + 1) * sub, :] for t in range(ns)]

        def tile(j, state, masked):
            off = pl.multiple_of(j * b, b)
            kk = k_ref[0, pl.ds(off, b), :]
            vv = v_ref[0, pl.ds(off, b), :]
            new = []
            for t in range(ns):
                m, l, acc = state[3 * t:3 * t + 3]
                nk = (t + 1) * sub if masked else b
                sc = lax.dot_general(qs[t], kk[:nk], _NT, preferred_element_type=F32)
                if masked:
                    sc = jnp.where(_causal((sub, nk), t * sub, 0, False), sc, NEG_INF)
                m_new = jnp.maximum(m, jnp.max(sc, axis=1, keepdims=True))
                alpha = jnp.exp(m - m_new)
                p = jnp.exp(sc - m_new)
                new += [m_new, alpha * l + jnp.sum(p, axis=1, keepdims=True),
                        alpha * acc + jnp.dot(p.astype(BF16), vv[:nk], preferred_element_type=F32)]
            return tuple(new)

        init = (jnp.full((sub, 1), NEG_INF, F32), jnp.zeros((sub, 1), F32), jnp.zeros((sub, d), F32)) * ns
        state = lax.fori_loop(0, i, lambda j, st: tile(j, st, False), init)
        state = tile(i, state, True)
        for t in range(ns):
            m, l, acc = state[3 * t:3 * t + 3]
            o_ref[0, t * sub:(t + 1) * sub, :] = acc / l
            lse_ref[0, t * sub:(t + 1) * sub, :] = m + jnp.log(l)

    blk_spec = pl.BlockSpec((1, b, d), lambda hh, i: (hh, i, 0))
    full = pl.BlockSpec((1, s, d), lambda hh, i: (hh, 0, 0))
    return pl.pallas_call(
        body, grid=(h, n), in_specs=[blk_spec, full, full],
        out_specs=[blk_spec, pl.BlockSpec((1, b, 1), lambda hh, i: (hh, i, 0))],
        out_shape=[jax.ShapeDtypeStruct((h, s, d), F32), jax.ShapeDtypeStruct((h, s, 1), F32)],
        name="attn_fwd", compiler_params=_params(("parallel", "arbitrary")),
    )(qp, kp, vp)


_ATT_BWD_BLK = 1024


def _attn_bwd(qpp, kp, vp, dop, *, blk=_ATT_BWD_BLK):
    h, s, d = qpp.shape
    b = min(blk, s)
    n = s // b
    sub = min(_ATT_SUB, b)
    ns = b // sub

    def body(k_ref, v_ref, q_ref, do_ref, dq_ref, dk_ref, dv_ref):
        j = pl.program_id(1)

        @pl.when(j == 0)
        def _():
            dq_ref[...] = jnp.zeros_like(dq_ref)

        ks = [k_ref[0, t * sub:(t + 1) * sub, :] for t in range(ns)]
        vs = [v_ref[0, t * sub:(t + 1) * sub, :] for t in range(ns)]

        def tile(i, state, masked):
            off = pl.multiple_of(i * b, b)
            qq = q_ref[0, pl.ds(off, b), :]
            dd = do_ref[0, pl.ds(off, b), :]
            new = []
            dq = None
            for t in range(ns):
                q0 = t * sub if masked else 0
                qt, dt = qq[q0:], dd[q0:]
                pt = jnp.exp(lax.dot_general(ks[t], qt, _NT, preferred_element_type=F32))
                if masked:
                    pt = jnp.where(_causal((sub, b - q0), q0, q0, True), pt, 0.0)
                dst = (pt * lax.dot_general(vs[t], dt, _NT, preferred_element_type=F32)).astype(BF16)
                new += [state[2 * t] + jnp.dot(dst, qt, preferred_element_type=F32),
                        state[2 * t + 1] + jnp.dot(pt.astype(BF16), dt, preferred_element_type=F32)]
                part = lax.dot_general(dst, ks[t], _TN, preferred_element_type=F32)
                if q0:
                    part = jnp.concatenate([jnp.zeros((q0, d), F32), part], axis=0)
                dq = part if dq is None else dq + part
            dq_ref[0, pl.ds(off, b), :] += dq
            return tuple(new)

        state = tile(j, (jnp.zeros((sub, d), F32),) * (2 * ns), True)
        state = lax.fori_loop(j + 1, n, lambda i, st: tile(i, st, False), state)
        for t in range(ns):
            dk_ref[0, t * sub:(t + 1) * sub, :] = state[2 * t]
            dv_ref[0, t * sub:(t + 1) * sub, :] = state[2 * t + 1]

    blk_spec = pl.BlockSpec((1, b, d), lambda hh, j: (hh, j, 0))
    full = pl.BlockSpec((1, s, d), lambda hh, j: (hh, 0, 0))
    return pl.pallas_call(
        body, grid=(h, n), in_specs=[blk_spec, blk_spec, full, full], out_specs=[full, blk_spec, blk_spec],
        out_shape=[jax.ShapeDtypeStruct((h, s, d), F32)] * 3, name="attn_bwd",
        compiler_params=_params(("parallel", "arbitrary")),
    )(kp, vp, qpp, dop)


def _pad_head(g):
    return jnp.pad(g, ((0, 0), (0, ATT_W - B_HEAD_DIM)))


def _even_fwd(x0, w, e, i):
    s = x0.shape[0]
    hs = B_HEADS * s
    hb = _rms_fwd(x0, w["norm_mix"][i:i + 1], name="mix_norm")
    z = _mm(hb, w["ev_w_in"][e], tb=True, name="ev_in", bn=896)
    b_s = w["ev_b_spatial"][e][:, :, None]
    v_gain = w["ev_v_norm"][e:e + 1]
    ymix = _gmlp_fwd(z, w["ev_w_spatial"][e], b_s, v_gain)
    f_t = z[:, IN_COLS - B_HEADS:IN_COLS].T
    c = _fgate_fwd(f_t, w["ev_b_fgate"][e][:, None])
    qp, kp, vp = _attn_prep(z, c.reshape(hs, 1), _pad_head(w["ev_q_norm"][e:e + 1]),
                            _pad_head(w["ev_k_norm"][e:e + 1]))
    shp = (B_HEADS, s, ATT_W)
    o, lse = _attn_fwd(qp.reshape(shp), kp.reshape(shp), vp.reshape(shp))
    ymix = _attn_merge_out(o, ymix)
    x1 = _mm(ymix, w["ev_w_out"][e], add=x0, name="ev_out")
    return x1, (hb, z, ymix, qp, kp, vp, f_t, o, lse)


def _even_bwd(dx1, dx1b, x0, saved, w, e, i):
    hb, z, ymix, qp, kp, vp, f_t, o, lse = saved
    s = x0.shape[0]
    hs = B_HEADS * s
    d_ymix = _mm(dx1b, w["ev_w_out"][e], tb=True, name="ev_out_dx")
    d_wout = _mm(ymix, dx1b, ta=True, name="ev_out_dw")
    dop, qpp = _attn_bwd_prep(d_ymix, o.reshape(hs, ATT_W), qp, lse.reshape(hs, 1))
    shp = (B_HEADS, s, ATT_W)
    qpp, dop, kp3, vp3 = qpp.reshape(shp), dop.reshape(shp), kp.reshape(shp), vp.reshape(shp)
    dqp, dkp, dvp = _attn_bwd(qpp, kp3, vp3, dop)
    d_ft, d_bf = _fgate_bwd(f_t, w["ev_b_fgate"][e][:, None], dqp[:, :, _COL_CQ], -dkp[:, :, _COL_CK])
    dz = jnp.pad(d_ft.T.astype(BF16), ((0, 0), (_F_BLOCK * ATT_W, IN_COLS_PAD - IN_COLS)))
    b_s = w["ev_b_spatial"][e][:, :, None]
    v_gain = w["ev_v_norm"][e:e + 1]
    dz, d_ws, d_bs, d_vg = _gmlp_bwd(z, d_ymix, w["ev_w_spatial"][e], b_s, v_gain, dz)
    twice = lambda g: jnp.concatenate([g, g], axis=1)
    dz, d_gq = _pair_bwd(dqp, dz, _Q_BLOCK, z=z, in_block=_Q_BLOCK, gain=twice(w["ev_q_norm"][e:e + 1]),
                         scale=_ATT_SCALE, name="attn_qnorm_bwd")
    dz, d_gk = _pair_bwd(dkp, dz, _K_BLOCK, z=z, in_block=_K_BLOCK, gain=twice(w["ev_k_norm"][e:e + 1]),
                         name="attn_knorm_bwd")
    dz, _ = _pair_bwd(dvp, dz, _V_BLOCK, name="attn_dv_out")
    d_h = _mm(dz, w["ev_w_in"][e], name="ev_in_dx")
    d_win_t = _mm(hb, dz, ta=True, name="ev_in_dw", bn=1344)[:, :IN_COLS].T
    dx0, d_nm = _rms_bwd_call(x0, d_h, w["norm_mix"][i:i + 1], dx1, name="mix_norm_bwd")
    hd = B_HEAD_DIM
    grads = {"ev_w_in": d_win_t, "ev_w_out": d_wout, "ev_b_fgate": d_bf[:, 0],
             "ev_q_norm": d_gq[0, :hd] + d_gq[0, hd:], "ev_k_norm": d_gk[0, :hd] + d_gk[0, hd:], "ev_v_norm": d_vg[0],
             "ev_w_spatial": d_ws, "ev_b_spatial": d_bs[:, :, 0], "norm_mix": d_nm[0]}
    return dx0, grads


def _s5_disc(a_re, a_im, log_dt, b_re, b_im):
    dt = jnp.exp(log_dt)[:, None]
    lr, li = a_re, a_im
    mag = jnp.exp(lr * dt)
    ab_re, ab_im = mag * jnp.cos(li * dt), mag * jnp.sin(li * dt)
    den = lr * lr + li * li
    nr, ni = ab_re - 1.0, ab_im
    cr = (nr * lr + ni * li) / den
    ci = (ni * lr - nr * li) / den
    bb_re = cr[..., None] * b_re - ci[..., None] * b_im
    bb_im = cr[..., None] * b_im + ci[..., None] * b_re
    return ab_re, ab_im, bb_re, bb_im


_GPB = S5_GROUPS // S5_BLOCKS


def _blockdiag(t):
    a, b = t.shape[1:]
    eye = jnp.eye(_GPB, dtype=t.dtype)
    t = t.reshape(S5_BLOCKS, _GPB, a, 1, b) * eye[None, :, None, :, None]
    return t.reshape(S5_BLOCKS, _GPB * a, _GPB * b)


def _blockdiag_extract(m, a, b):
    eye = jnp.eye(_GPB, dtype=m.dtype)
    m = m.reshape(S5_BLOCKS, _GPB, a, _GPB, b) * eye[None, :, None, :, None]
    return jnp.sum(m, axis=3).reshape(S5_GROUPS, a, b)


_SCAN_ROWS = 8


def _cmul(a_r, a_i, b_r, b_i):
    return a_r * b_r - a_i * b_i, a_r * b_i + a_i * b_r


def _scan_tiles(yr_ref, yi_ref, a_r, a_i, c_r, c_i, *, rev, x_refs=None):
    tb, wl = yr_ref.shape
    ntile = tb // _SCAN_ROWS
    with_acc = x_refs is not None
    if rev:
        a_i = -a_i
    pw = [(a_r, a_i)]
    for _ in range(_SCAN_ROWS - 1):
        pw.append(_cmul(*pw[-1], a_r, a_i))
    sub = lax.broadcasted_iota(jnp.int32, (_SCAN_ROWS, wl), 0)
    dist = (_SCAN_ROWS - 1 - sub) if rev else sub
    zero = jnp.zeros((_SCAN_ROWS, wl), F32)
    steps = []
    for kk in (1, 2, 4):
        steps.append(((_SCAN_ROWS - kk) if rev else kk, jnp.where(dist >= kk, pw[kk - 1][0], zero),
                      jnp.where(dist >= kk, pw[kk - 1][1], zero)))
    e_r, e_i = zero, zero
    for d in range(_SCAN_ROWS):
        e_r = jnp.where(dist == d, pw[d][0], e_r)
        e_i = jnp.where(dist == d, pw[d][1], e_i)
    exit_row = 0 if rev else _SCAN_ROWS - 1

    def tile(q, carry):
        idx = (ntile - 1 - q) if rev else q
        rows = pl.ds(pl.multiple_of(idx * _SCAN_ROWS, _SCAN_ROWS), _SCAN_ROWS)
        y_r, y_i = yr_ref[rows, :], yi_ref[rows, :]
        for sh, k_r, k_i in steps:
            t_r, t_i = _cmul(k_r, k_i, pltpu.roll(y_r, sh, 0), pltpu.roll(y_i, sh, 0))
            y_r, y_i = y_r + t_r, y_i + t_i
        cb_r = jnp.broadcast_to(carry[0], (_SCAN_ROWS, wl))
        cb_i = jnp.broadcast_to(carry[1], (_SCAN_ROWS, wl))
        t_r, t_i = _cmul(e_r, e_i, cb_r, cb_i)
        y_r, y_i = y_r + t_r, y_i + t_i
        yr_ref[rows, :] = y_r
        yi_ref[rows, :] = y_i
        out = (y_r[exit_row:exit_row + 1], y_i[exit_row:exit_row + 1])
        if with_acc:
            n_r = jnp.where(sub == _SCAN_ROWS - 1, cb_r, pltpu.roll(y_r, _SCAN_ROWS - 1, 0))
            n_i = jnp.where(sub == _SCAN_ROWS - 1, cb_i, pltpu.roll(y_i, _SCAN_ROWS - 1, 0))
            s_r, s_i = x_refs[0][rows, :], x_refs[1][rows, :]
            out += (carry[2] + (n_r * s_r + n_i * s_i), carry[3] + (n_i * s_r - n_r * s_i))
        return out

    return lax.fori_loop(0, ntile, tile, (c_r, c_i) + ((zero, zero) if with_acc else ()), unroll=2)


_S5_WL = S5_N // S5_BLOCKS
_S5_CW = D_MODEL // S5_BLOCKS
_TN = (((0,), (0,)), ((), ()))


def _s5_fwd(hb, mats, *, tb=1024):
    s = hb.shape[0]
    tb = min(tb, s)

    def body(h_ref, wbr_ref, wbi_ref, wcr_ref, wci_ref, ar_ref, ai_ref, xr_ref, xi_ref, cp_ref, cr, ci):
        @pl.when(pl.program_id(1) == 0)
        def _():
            cr[...] = jnp.zeros_like(cr)
            ci[...] = jnp.zeros_like(ci)

        hv = h_ref[...]
        xr_ref[...] = jnp.dot(hv, wbr_ref[0], preferred_element_type=F32)
        xi_ref[...] = jnp.dot(hv, wbi_ref[0], preferred_element_type=F32)
        cr[...], ci[...] = _scan_tiles(xr_ref, xi_ref, ar_ref[...], ai_ref[...], cr[...], ci[...], rev=False)
        cp_ref[...] = (jnp.dot(xr_ref[...].astype(BF16), wcr_ref[0], preferred_element_type=F32)
                       + jnp.dot(xi_ref[...].astype(BF16), wci_ref[0], preferred_element_type=F32))

    wide = pl.BlockSpec((tb, _S5_WL), lambda c, t: (t, c))
    narrow = pl.BlockSpec((tb, _S5_CW), lambda c, t: (t, c))
    w_in = pl.BlockSpec((1, _S5_CW, _S5_WL), lambda c, t: (c, 0, 0))
    w_out = pl.BlockSpec((1, _S5_WL, _S5_CW), lambda c, t: (c, 0, 0))
    vec = pl.BlockSpec((1, _S5_WL), lambda c, t: (0, c))
    return pl.pallas_call(
        body, grid=(S5_BLOCKS, s // tb), in_specs=[narrow, w_in, w_in, w_out, w_out, vec, vec],
        out_specs=[wide, wide, narrow],
        out_shape=[jax.ShapeDtypeStruct((s, S5_N), F32)] * 2 + [jax.ShapeDtypeStruct((s, D_MODEL), F32)],
        scratch_shapes=[pltpu.VMEM((1, _S5_WL), F32)] * 2, name="s5_fwd",
        compiler_params=_params(("parallel", "arbitrary")),
    )(hb, mats["wb_re"], mats["wb_im"], mats["wc_re"], mats["wc_imn"], mats["ar"], mats["ai"])


def _s5_bwd(dyb, hb, xr, xi, dhd, mats, *, tb=1024):
    s = hb.shape[0]
    tb = min(tb, s)
    nt = s // tb

    def body(dy_ref, h_ref, xr_ref, xi_ref, dhd_ref, wctr_ref, wcti_ref, wbtr_ref, wbti_ref, ar_ref, ai_ref,
             dh_ref, dar_ref, dai_ref, dwbr_ref, dwbi_ref, dwcr_ref, dwci_ref, lr, li, cr, ci):
        @pl.when(pl.program_id(1) == 0)
        def _():
            for r in (cr, ci, dar_ref, dai_ref, dwbr_ref, dwbi_ref, dwcr_ref, dwci_ref):
                r[...] = jnp.zeros_like(r)

        dyv, hv = dy_ref[...], h_ref[...]
        lr[...] = jnp.dot(dyv, wctr_ref[0], preferred_element_type=F32)
        li[...] = jnp.dot(dyv, wcti_ref[0], preferred_element_type=F32)
        cr[...], ci[...], acc_r, acc_i = _scan_tiles(lr, li, ar_ref[...], ai_ref[...], cr[...], ci[...], rev=True,
                                                     x_refs=(xr_ref, xi_ref))
        dar_ref[...] += jnp.sum(acc_r, axis=0, keepdims=True)
        dai_ref[...] += jnp.sum(acc_i, axis=0, keepdims=True)
        lrb, lib = lr[...].astype(BF16), li[...].astype(BF16)
        dh_ref[...] = (jnp.dot(lrb, wbtr_ref[0], preferred_element_type=F32) + dhd_ref[...]
                       + jnp.dot(lib, wbti_ref[0], preferred_element_type=F32))
        dwbr_ref[0] += lax.dot_general(hv, lrb, _TN, preferred_element_type=F32)
        dwbi_ref[0] += lax.dot_general(hv, lib, _TN, preferred_element_type=F32)
        dwcr_ref[0] += lax.dot_general(dyv, xr_ref[...].astype(BF16), _TN, preferred_element_type=F32)
        dwci_ref[0] += lax.dot_general(dyv, xi_ref[...].astype(BF16), _TN, preferred_element_type=F32)

    wide = pl.BlockSpec((tb, _S5_WL), lambda c, t: (nt - 1 - t, c))
    narrow = pl.BlockSpec((tb, _S5_CW), lambda c, t: (nt - 1 - t, c))
    w_in = pl.BlockSpec((1, _S5_CW, _S5_WL), lambda c, t: (c, 0, 0))
    w_out = pl.BlockSpec((1, _S5_WL, _S5_CW), lambda c, t: (c, 0, 0))
    vec = pl.BlockSpec((1, _S5_WL), lambda c, t: (0, c))
    acc_shape = jax.ShapeDtypeStruct((S5_BLOCKS, _S5_CW, _S5_WL), F32)
    return pl.pallas_call(
        body, grid=(S5_BLOCKS, nt), in_specs=[narrow, narrow, wide, wide, narrow, w_in, w_in, w_out, w_out, vec, vec],
        out_specs=[narrow, vec, vec, w_in, w_in, w_in, w_in],
        out_shape=[jax.ShapeDtypeStruct((s, D_MODEL), F32)] + [jax.ShapeDtypeStruct((1, S5_N), F32)] * 2
        + [acc_shape] * 4,
        scratch_shapes=[pltpu.VMEM((tb, _S5_WL), F32)] * 2 + [pltpu.VMEM((1, _S5_WL), F32)] * 2, name="s5_bwd",
        compiler_params=_params(("parallel", "arbitrary")),
    )(dyb, hb, xr, xi, dhd, mats["wct_re"], mats["wct_imn"], mats["wbt_re"], mats["wbt_im"], mats["ar"], mats["ai"])


def _s5_mats(w, o):
    ab_re, ab_im, bb_re, bb_im = _s5_disc(w["od_a_re"][o], w["od_a_im"][o], w["od_log_dt"][o], w["od_b_re"][o],
                                          w["od_b_im"][o])
    c_re, c_im = w["od_c_re"][o], w["od_c_im"][o]
    tr = lambda t: t.transpose(0, 2, 1)
    bd = lambda t: _blockdiag(t).astype(BF16)
    return {
        "ar": ab_re.reshape(1, S5_N), "ai": ab_im.reshape(1, S5_N),
        "wb_re": bd(tr(bb_re)), "wb_im": bd(tr(bb_im)), "wc_re": bd(tr(c_re)), "wc_imn": bd(-tr(c_im)),
        "wct_re": bd(c_re), "wct_imn": bd(-c_im), "wbt_re": bd(bb_re), "wbt_im": bd(bb_im),
    }


def _odd_fwd(x0, w, o, i):
    mats = _s5_mats(w, o)
    nm = w["norm_mix"][i:i + 1]
    d_row = w["od_d"][o:o + 1]
    hb = _rms_fwd(x0, nm, name="mix_norm")
    xr, xi, cp = _s5_fwd(hb, mats)

    def fn(xv, cpv, gv, dv):
        y = cpv + dv * (xv * _rstd(xv) * gv)
        return y, _gelu(y)

    y, gy = _rw(fn, [x0, cp], [nm, d_row], [(D_MODEL, F32), (D_MODEL, BF16)], tr=512, name="s5_out")
    gg = _mm(gy, w["od_w_glu"][o], tb=True, name="od_glu")
    x1 = _rw(lambda xv, g: xv + g[:, :D_MODEL] * _sigmoid(g[:, D_MODEL:]), [x0, gg], [], [(D_MODEL, F32)], tr=512,
             name="od_glu_out")[0]
    return x1, (hb, xr, xi, y, gy, gg)


def _odd_bwd(dx1, x0, saved, w, o, i):
    hb, xr, xi, y, gy, gg = saved
    mats = _s5_mats(w, o)
    nm = w["norm_mix"][i:i + 1]
    d_row = w["od_d"][o:o + 1]

    def fn_glu(dv, g):
        ga, gb = g[:, :D_MODEL], g[:, D_MODEL:]
        sg = _sigmoid(gb)
        return jnp.concatenate([dv * sg, dv * ga * (sg * (1.0 - sg))], axis=1)

    dgg = _rw(fn_glu, [dx1, gg], [], [(2 * D_MODEL, BF16)], tr=512, name="od_glu_out_bwd")[0]
    d_wglu = _mm(gy, dgg, ta=True, name="od_glu_dw", bn=1024).T
    d_gy = _mm(dgg, w["od_w_glu"][o], name="od_glu_dx")

    def fn_y(dg, yv, xv, gv, dv):
        dy = dg * _gelu_grad(yv)
        h = xv * _rstd(xv) * gv
        return dy, dv * dy, jnp.sum(dy * h, axis=0, keepdims=True)

    dyb, dhd, d_d = _rw(fn_y, [d_gy, y, x0], [nm, d_row], [(D_MODEL, BF16), (D_MODEL, F32)], [(1, D_MODEL)], tr=512,
                        name="s5_out_bwd")
    dh, d_ar, d_ai, d_wb_re, d_wb_im, d_wc_re, d_wc_im = _s5_bwd(dyb, hb, xr, xi, dhd, mats)
    gc, gp = S5_GROUP_CH, S5_STATE
    d_c_re = _blockdiag_extract(d_wc_re, gc, gp)
    d_c_im = -_blockdiag_extract(d_wc_im, gc, gp)
    d_bb_re = _blockdiag_extract(d_wb_re, gc, gp).transpose(0, 2, 1)
    d_bb_im = _blockdiag_extract(d_wb_im, gc, gp).transpose(0, 2, 1)
    dx0, d_nm = _rms_bwd_call(x0, dh, nm, dx1, name="mix_norm_bwd")
    _, vjp = jax.vjp(_s5_disc, w["od_a_re"][o], w["od_a_im"][o], w["od_log_dt"][o], w["od_b_re"][o], w["od_b_im"][o])
    d_a_re, d_a_im, d_log_dt, d_b_re, d_b_im = vjp(
        (d_ar.reshape(S5_GROUPS, S5_STATE), d_ai.reshape(S5_GROUPS, S5_STATE), d_bb_re, d_bb_im))
    grads = {"od_a_re": d_a_re, "od_a_im": d_a_im, "od_log_dt": d_log_dt, "od_b_re": d_b_re, "od_b_im": d_b_im,
             "od_c_re": d_c_re, "od_c_im": d_c_im, "od_d": d_d[0], "od_w_glu": d_wglu, "norm_mix": d_nm[0]}
    return dx0, grads


_HBM = pl.BlockSpec(memory_space=pltpu.HBM)


def _mesh_pos():
    return lax.axis_index("x"), lax.axis_index("y"), lax.axis_index("c")


def _slot(px, py, pc):
    return 4 * px + 2 * py + pc


def _all_gather(x, *, name):
    def body(x_ref, out_ref, send_sems, recv_sems, local_sem):
        mx, my, mc = _mesh_pos()
        me, sibling = (mx, my, mc), (mx, my, 1 - mc)
        chips = [(1 - mx, my), (mx, 1 - my), (1 - mx, 1 - my)]

        def copy(k, block, to, src=None):
            dst = out_ref.at[_slot(*block)]
            return pltpu.make_async_remote_copy(
                src_ref=dst if src is None else src, dst_ref=dst, send_sem=send_sems.at[k], recv_sem=recv_sems.at[k],
                device_id=to, device_id_type=MESH)

        mine = pltpu.make_async_copy(x_ref, out_ref.at[_slot(*me)], local_sem)
        mine.start()
        first = [copy(0, me, sibling, src=x_ref)]
        first += [copy(1 + j, me, (*chip, mc), src=x_ref) for j, chip in enumerate(chips)]
        for cp in first:
            cp.start()
        passed = [copy(4 + j, (*chip, mc), sibling) for j, chip in enumerate(chips)]
        for j, chip in enumerate(chips):
            copy(1 + j, (*chip, mc), me).wait_recv()
            passed[j].start()
        copy(0, sibling, me).wait_recv()
        for j, chip in enumerate(chips):
            copy(4 + j, (*chip, 1 - mc), me).wait_recv()
        for cp in first + passed:
            cp.wait_send()
        mine.wait()

    return pl.pallas_call(
        body, out_shape=jax.ShapeDtypeStruct((N_DEV,) + x.shape, x.dtype), in_specs=[_HBM], out_specs=_HBM,
        scratch_shapes=[pltpu.SemaphoreType.DMA((7,)), pltpu.SemaphoreType.DMA((7,)), pltpu.SemaphoreType.DMA],
        name=name,
    )(x)


def _all_to_all(x, *, name):
    def body(x_ref, out_ref, send_sems, recv_sems, local_sem):
        mx, my, mc = _mesh_pos()
        my_slot = _slot(mx, my, mc)
        mine = pltpu.make_async_copy(x_ref.at[my_slot], out_ref.at[my_slot], local_sem)
        mine.start()
        copies = []
        for k in range(1, N_DEV):
            peer = (1 - mx if k & 4 else mx, 1 - my if k & 2 else my, 1 - mc if k & 1 else mc)
            copies.append(pltpu.make_async_remote_copy(
                src_ref=x_ref.at[_slot(*peer)], dst_ref=out_ref.at[my_slot], send_sem=send_sems.at[k - 1],
                recv_sem=recv_sems.at[k - 1], device_id=peer, device_id_type=MESH))
        for cp in copies:
            cp.start()
        for cp in copies:
            cp.wait()
        mine.wait()

    return pl.pallas_call(
        body, out_shape=jax.ShapeDtypeStruct(x.shape, x.dtype), in_specs=[_HBM], out_specs=_HBM,
        scratch_shapes=[pltpu.SemaphoreType.DMA((7,)), pltpu.SemaphoreType.DMA((7,)), pltpu.SemaphoreType.DMA],
        name=name,
    )(x)


_SEMS = pltpu.SemaphoreType.DMA((N_DEV - 1,))
_SEM_SPEC = pl.BlockSpec(memory_space=pltpu.SEMAPHORE)
_SPLIT_COPY = pltpu.CompilerParams(has_side_effects=pltpu.SideEffectType.DATAFLOW_SIDE_EFFECTING)


def _exchange_copies(x_refs, land_refs, send_sems, recv_sems, gather):
    mx, my, mc = _mesh_pos()
    my_slot = _slot(mx, my, mc)
    copies = []
    for a, (x_ref, land_ref) in enumerate(zip(x_refs, land_refs)):
        for k in range(1, N_DEV):
            peer = (1 - mx if k & 4 else mx, 1 - my if k & 2 else my, 1 - mc if k & 1 else mc)
            copies.append(pltpu.make_async_remote_copy(
                src_ref=x_ref if gather else x_ref.at[_slot(*peer)], dst_ref=land_ref.at[my_slot],
                send_sem=send_sems[a].at[k - 1], recv_sem=recv_sems[a].at[k - 1], device_id=peer, device_id_type=MESH))
    return copies


def _exchange_start(xs, carry, *, gather, name):
    n = len(xs)

    def body(*refs):
        outs = refs[2 * n + 1:]
        for cp in _exchange_copies(refs[:n], refs[n:2 * n], outs[:n], outs[n:2 * n], gather):
            cp.start()

    hbm = lambda a: pltpu.HBM(a.shape, a.dtype)
    lands = [pltpu.with_memory_space_constraint(lax.empty(((N_DEV,) if gather else ()) + x.shape, x.dtype), pltpu.HBM)
             for x in xs]
    ins = [pltpu.with_memory_space_constraint(a, pltpu.HBM) for a in list(xs) + [carry]]
    res = pl.pallas_call(
        body, name=name, in_specs=[_HBM] * (2 * n + 1), out_specs=[_SEM_SPEC] * (2 * n) + [_HBM] * (2 * n + 1),
        out_shape=[_SEMS] * (2 * n) + [hbm(a) for a in list(xs) + lands] + [hbm(carry)],
        input_output_aliases={**{a: 2 * n + a for a in range(2 * n)}, 2 * n: 4 * n}, compiler_params=_SPLIT_COPY,
    )(*ins[:n], *lands, ins[n])
    return (res[:n], res[n:2 * n], res[2 * n:3 * n], res[3 * n:4 * n]), res[4 * n]


def _exchange_wait(handles, after, *, gather, name):
    send_sems, recv_sems, xs, lands = handles
    n = len(xs)

    def body(*refs):
        for cp in _exchange_copies(refs[:n], refs[n:2 * n], refs[2 * n:3 * n], refs[3 * n:4 * n], gather):
            cp.wait_send()
            cp.wait_recv()

    hbm = lambda a: pltpu.HBM(a.shape, a.dtype)
    res = pl.pallas_call(
        body, name=name, in_specs=[_HBM] * (2 * n) + [_SEM_SPEC] * (2 * n) + [_ANY], out_specs=[_HBM] * (2 * n),
        out_shape=[hbm(a) for a in list(xs) + list(lands)], input_output_aliases={a: a for a in range(2 * n)},
        compiler_params=_SPLIT_COPY,
    )(*xs, *lands, *send_sems, *recv_sems, after)
    return res[:n], res[n:]


_ADAMW_PARTS_BLOCK_BYTES = 8 * 1024 * 1024


def _adamw(parts, w, m, v, *, name, own=None):
    r, c = w.shape
    tr = r
    for cand in range(8, r + 1, 8):
        if r % cand == 0 and N_DEV * cand * c * 4 <= _ADAMW_PARTS_BLOCK_BYTES:
            tr = cand
    has_own = own is not None

    def body(*refs):
        p_ref, w_ref, m_ref, v_ref = refs[0], refs[1 + has_own], refs[2 + has_own], refs[3 + has_own]
        g_ref, d_ref, nm_ref, nv_ref = refs[4 + has_own:]
        if has_own:
            me = _slot(*_mesh_pos())
            part = lambda k: jnp.where(me == k, refs[1][...], p_ref[k])
        else:
            part = lambda k: p_ref[k]
        g = part(0)
        for k in range(1, N_DEV):
            g = g + part(k)
        nm = ADAM_B1 * m_ref[...] + (1.0 - ADAM_B1) * g
        nv = ADAM_B2 * v_ref[...] + (1.0 - ADAM_B2) * (g * g)
        m_hat = nm / (1.0 - ADAM_B1 ** ADAM_STEP)
        v_hat = nv / (1.0 - ADAM_B2 ** ADAM_STEP)
        g_ref[...] = g
        d_ref[...] = -ADAM_LR * (m_hat / (jnp.sqrt(v_hat) + ADAM_EPS) + ADAM_WD * w_ref[...])
        nm_ref[...] = nm
        nv_ref[...] = nv

    blk = pl.BlockSpec((tr, c), lambda i: (i, 0))
    return pl.pallas_call(
        body, grid=(r // tr,),
        in_specs=[pl.BlockSpec((N_DEV, tr, c), lambda i: (0, i, 0))] + [blk] * (3 + has_own),
        out_specs=[blk] * 4, out_shape=[jax.ShapeDtypeStruct((r, c), F32)] * 4, name=name,
        compiler_params=_params(("parallel",)),
    )(parts, *([own] if has_own else []), w, m, v)


_FLAT_COLS = 1024
_FLAT_ROW_ALIGN = 128

_SHARD_AXIS = {
    "norm_mix": None, "norm_ffn": None, "ev_w_in": 2, "ev_b_fgate": None, "ev_q_norm": None, "ev_k_norm": None,
    "ev_v_norm": None, "ev_w_spatial": None, "ev_b_spatial": None, "ev_w_out": 1, "od_a_re": None, "od_a_im": None,
    "od_log_dt": None, "od_b_re": None, "od_b_im": None, "od_c_re": None, "od_c_im": None, "od_d": 1, "od_w_glu": 2,
    "ffn_w_up": 2, "ffn_conv_w": 2, "ffn_conv_b": None, "ffn_w_down": 1, "ple_w_proj": 2, "ple_w_gate": 1,
}
_WEIGHTS = list(_SHARD_AXIS)
_REPLICATED = [n for n in _WEIGHTS if _SHARD_AXIS[n] is None]
_COL_SHARDED = ["ev_w_in", "od_w_glu", "ffn_w_up", "ple_w_proj"]
_ROW_SHARDED = ["ev_w_out", "ffn_w_down", "ple_w_gate"]
_SMALL_SHARDED = ["od_d", "ffn_conv_w"]


def _pack(arrays, lead=()):
    nl = len(lead)
    flat = jnp.concatenate([a.reshape(lead + (-1,)) for a in arrays], axis=nl)
    n = flat.shape[nl]
    chunk = _FLAT_COLS * _FLAT_ROW_ALIGN
    total = -(-n // chunk) * chunk
    flat = jnp.pad(flat, [(0, 0)] * nl + [(0, total - n)])
    return flat.reshape(lead + (total // _FLAT_COLS, _FLAT_COLS))


def _unpack(buf, shapes, lead=()):
    nl = len(lead)
    flat = buf.reshape(lead + (-1,))
    out, off = [], 0
    for shp in shapes:
        n = math.prod(shp)
        out.append(lax.slice_in_dim(flat, off, off + n, axis=nl).reshape(lead + tuple(shp)))
        off += n
    return out


def _join_shards(g, axis):
    g = jnp.moveaxis(g, 0, axis)
    shp = g.shape
    return g.reshape(shp[:axis] + (shp[axis] * shp[axis + 1],) + shp[axis + 2:])


def _split_shards(full, axis):
    shp = full.shape
    g = full.reshape(shp[:axis] + (N_DEV, shp[axis] // N_DEV) + shp[axis + 1:])
    return jnp.moveaxis(g, axis, 0)


def _local_step(x, p, tgt, w, before_layer, on_grads):
    saved = []
    h = x
    for i in range(DEPTH):
        x0 = before_layer(i, h)
        if i % 2 == 0:
            x1, sv_mix = _even_fwd(x0, w, i // 2, i)
        else:
            x1, sv_mix = _odd_fwd(x0, w, i // 2, i)
        x2, sv_ffn = _ffn_fwd(x1, w, i)
        x3, sv_ple = _ple_fwd(x2, p[i], w, i)
        saved.append((x0, x1, x2, sv_mix, sv_ffn, sv_ple))
        h = x3
    loss, dh = _loss_and_grad(h, tgt)
    per_layer = {}
    for i in reversed(range(DEPTH)):
        x0, x1, x2, sv_mix, sv_ffn, sv_ple = saved[i]
        dx2, dx2b, g_ple = _ple_bwd(dh, x2, p[i], sv_ple, w, i)
        dx1, dx1b, g_ffn = _ffn_bwd(dx2, dx2b, x1, sv_ffn, w, i)
        dx1 = on_grads(i, 0, {**g_ple, **g_ffn}, dx1)
        if i % 2 == 0:
            dh, g_mix = _even_bwd(dx1, dx1b, x0, sv_mix, w, i // 2, i)
        else:
            dh, g_mix = _odd_bwd(dx1, x0, sv_mix, w, i // 2, i)
        dh = on_grads(i, 1, g_mix, dh)
        layer_grads = {**g_ple, **g_ffn, **g_mix}
        for name, g in layer_grads.items():
            per_layer.setdefault(name, []).append(g)
    grads = {name: gs[::-1] for name, gs in per_layer.items()}
    return loss, dh, grads


def kernel(x, p, norm_mix, norm_ffn, ev_w_in, ev_b_fgate, ev_q_norm, ev_k_norm, ev_v_norm, ev_w_spatial, ev_b_spatial, ev_w_out, od_a_re, od_a_im, od_log_dt, od_b_re, od_b_im, od_c_re, od_c_im, od_d, od_w_glu, ffn_w_up, ffn_conv_w, ffn_conv_b, ffn_w_down, ple_w_proj, ple_w_gate, loss_target, m_norm_mix, m_norm_ffn, m_ev_w_in, m_ev_b_fgate, m_ev_q_norm, m_ev_k_norm, m_ev_v_norm, m_ev_w_spatial, m_ev_b_spatial, m_ev_w_out, m_od_a_re, m_od_a_im, m_od_log_dt, m_od_b_re, m_od_b_im, m_od_c_re, m_od_c_im, m_od_d, m_od_w_glu, m_ffn_w_up, m_ffn_conv_w, m_ffn_conv_b, m_ffn_w_down, m_ple_w_proj, m_ple_w_gate, v_norm_mix, v_norm_ffn, v_ev_w_in, v_ev_b_fgate, v_ev_q_norm, v_ev_k_norm, v_ev_v_norm, v_ev_w_spatial, v_ev_b_spatial, v_ev_w_out, v_od_a_re, v_od_a_im, v_od_log_dt, v_od_b_re, v_od_b_im, v_od_c_re, v_od_c_im, v_od_d, v_od_w_glu, v_ffn_w_up, v_ffn_conv_w, v_ffn_conv_b, v_ffn_w_down, v_ple_w_proj, v_ple_w_gate):
    args = locals()
    wl = {n: args[n] for n in _WEIGHTS}
    ml = {n: args["m_" + n] for n in _WEIGHTS}
    vl = {n: args["v_" + n] for n in _WEIGHTS}

    kinds = ("g", "d", "m", "v")

    def local(n, layer, a):
        return a[n][layer].T if n in _COL_SHARDED else a[n][layer]

    me = _slot(*_mesh_pos())
    w = {n: wl[n] for n in _REPLICATED}
    small = _all_gather(_pack([wl[n] for n in _SMALL_SHARDED]), name="gather_small_weights")
    for n, g in zip(_SMALL_SHARDED, _unpack(small, [wl[n].shape for n in _SMALL_SHARDED], lead=(N_DEV,))):
        w[n] = _join_shards(g, _SHARD_AXIS[n])

    def layer_of(n, i):
        return i if wl[n].shape[0] == DEPTH else i // 2

    def weights_of(i):
        mixer = ["ev_w_in", "ev_w_out"] if i % 2 == 0 else ["od_w_glu"]
        return mixer + ["ffn_w_up", "ffn_w_down", "ple_w_proj", "ple_w_gate"]

    gathers = {}
    for n in _COL_SHARDED + _ROW_SHARDED:
        w[n] = {}

    def start_gather(i, carry):
        blocks = [local(n, layer_of(n, i), wl).astype(BF16) for n in weights_of(i)]
        gathers[i], carry = _exchange_start(blocks, carry, gather=True, name="gather_start_layer%d" % i)
        return carry

    def finish_gather(i, h):
        sent, landed = _exchange_wait(gathers[i], h, gather=True, name="gather_wait_layer%d" % i)
        for n, block, land in zip(weights_of(i), sent, landed):
            full = lax.dynamic_update_slice(land, block[None], (me, 0, 0)).reshape(N_DEV * block.shape[0], block.shape[1])
            if n == "ev_w_in":
                full = jnp.pad(full, ((0, IN_COLS_PAD - IN_COLS), (0, 0)))
            w[n][layer_of(n, i)] = full
        if i == 0:
            for later in range(1, DEPTH):
                h = start_gather(later, h)
        return h

    x_in = start_gather(0, x[0])

    in_flight = {}

    def start_scatter(i, part, part_grads, dh):
        names = [n for n in _COL_SHARDED + _ROW_SHARDED if n in part_grads]
        xs = [part_grads[n].reshape(N_DEV, part_grads[n].shape[0] // N_DEV, part_grads[n].shape[1]) for n in names]
        handles, dh = _exchange_start(xs, dh, gather=False, name="scatter_start_layer%d_part%d" % (i, part))
        in_flight[i, part] = (names, handles)
        return dh

    loss, grad_x, grads = _local_step(x_in, p[:, 0], loss_target[0], w, finish_gather, start_scatter)
    loss = lax.psum(loss, ("x", "y", "c"))

    out = {kind: {} for kind in kinds}
    done = {}
    for i, part in sorted(in_flight, reverse=True):
        names, handles = in_flight[i, part]
        sent, landed = _exchange_wait(handles, grad_x, gather=False, name="scatter_wait_layer%d_part%d" % (i, part))
        for n, x_sent, parts in zip(names, sent, landed):
            layer = layer_of(n, i)
            own = lax.dynamic_index_in_dim(x_sent, me, axis=0, keepdims=False)
            res = _adamw(parts, local(n, layer, wl), local(n, layer, ml), local(n, layer, vl), name="adamw_" + n, own=own)
            done.setdefault(n, {})[layer] = [r.T if n in _COL_SHARDED else r for r in res]
    for n, layers in done.items():
        for k, kind in enumerate(kinds):
            out[kind][n] = jnp.stack([layers[layer][k] for layer in sorted(layers)])

    send = _pack([_split_shards(jnp.stack(grads[n]), _SHARD_AXIS[n]) for n in _SMALL_SHARDED], lead=(N_DEV,))
    parts = _all_to_all(send, name="scatter_small_grads")
    shapes = [wl[n].shape for n in _SMALL_SHARDED]
    res = _adamw(parts, _pack([wl[n] for n in _SMALL_SHARDED]), _pack([ml[n] for n in _SMALL_SHARDED]),
                 _pack([vl[n] for n in _SMALL_SHARDED]), name="adamw_small")
    for kind, buf in zip(kinds, res):
        out[kind].update(zip(_SMALL_SHARDED, _unpack(buf, shapes)))

    parts = _all_gather(_pack([jnp.stack(grads[n]) for n in _REPLICATED]), name="gather_replicated_grads")
    shapes = [wl[n].shape for n in _REPLICATED]
    res = _adamw(parts, _pack([wl[n] for n in _REPLICATED]), _pack([ml[n] for n in _REPLICATED]),
                 _pack([vl[n] for n in _REPLICATED]), name="adamw_replicated")
    for kind, buf in zip(kinds, res):
        out[kind].update(zip(_REPLICATED, _unpack(buf, shapes)))

    return (loss, grad_x[None], *[out["g"][n] for n in _WEIGHTS], *[out["d"][n] for n in _WEIGHTS],
            *[out["m"][n] for n in _WEIGHTS], *[out["v"][n] for n in _WEIGHTS])
```

```python
import functools
import math

import jax
import jax.numpy as jnp
from jax import lax
from jax.experimental import pallas as pl
from jax.experimental.pallas import tpu as pltpu

F32 = jnp.float32
BF16 = jnp.bfloat16

D_MODEL = 1024
DEPTH = 4
A_GROUPS = 4
A_CHUNK = 128
A_WIDTH = 512
B_HEADS = 8
B_HEAD_DIM = 64
B_WIDTH = 512
IN_COLS = 2 * A_WIDTH + 3 * B_WIDTH + B_HEADS
IN_COLS_PAD = 2688
S5_GROUP_CH = 16
S5_GROUPS = 64
S5_STATE = 64
S5_N = S5_GROUPS * S5_STATE
S5_BLOCKS = 8
D_FF = 2816
PLE_DIM = 256
EPS = 1e-6
NEG_INF = -1e30
N_DEV = 8

ADAM_LR = 0.001
ADAM_B1 = 0.9
ADAM_B2 = 0.999
ADAM_EPS = 1e-08
ADAM_WD = 0.01
ADAM_STEP = 10

VMEM_LIMIT_BYTES = 56 * 1024 * 1024
MESH = pl.DeviceIdType.MESH


def _params(sem, vmem=VMEM_LIMIT_BYTES):
    return pltpu.CompilerParams(dimension_semantics=sem, vmem_limit_bytes=vmem)


_GELU_K = math.sqrt(2.0 / math.pi)


def _gelu(x):
    return x * (0.5 * (1.0 + jnp.tanh(_GELU_K * (x + 0.044715 * (x * x * x)))))


def _gelu_grad(x):
    t = jnp.tanh(_GELU_K * (x + 0.044715 * (x * x * x)))
    return 0.5 * (1.0 + t) + 0.5 * x * (1.0 - t * t) * (_GELU_K * (1.0 + 3.0 * 0.044715 * (x * x)))


def _sigmoid(x):
    return 0.5 * jnp.tanh(0.5 * x) + 0.5


def _rstd(x):
    return lax.rsqrt(jnp.mean(x * x, axis=-1, keepdims=True) + EPS)


def _rms_bwd(x, dy, g):
    r = _rstd(x)
    xh = x * r
    dyg = dy if g is None else dy * g
    dx = r * (dyg - xh * jnp.mean(dyg * xh, axis=-1, keepdims=True))
    return dx, dy * xh


def _shift_down(blk, halo, k):
    tr = blk.shape[0]
    r = pltpu.roll(blk, k, 0)
    hr = pltpu.roll(halo, k, 0)
    first = jnp.where(lax.broadcasted_iota(jnp.int32, hr.shape, 0) < k, hr, r[0:8])
    return jnp.concatenate([first, r[8:tr]], axis=0)


def _shift_up(blk, halo, k):
    tr = blk.shape[0]
    r = pltpu.roll(blk, tr - k, 0)
    hr = pltpu.roll(halo, 8 - k, 0)
    last = jnp.where(lax.broadcasted_iota(jnp.int32, hr.shape, 0) >= 8 - k, hr, r[tr - 8:tr])
    return jnp.concatenate([r[0:tr - 8], last], axis=0)


def _rw(fn, rows, consts, outs, accs=(), *, tr, name, prev=(), nxt=(), widths=None):
    s = rows[0].shape[0]
    tr = min(tr, s)
    n = s // tr
    nr, nc, npv, nnx, no, na = len(rows), len(consts), len(prev), len(nxt), len(outs), len(accs)
    widths = widths or [None] * nr

    def body(*refs):
        ins, out_refs = refs[:nr + nc + npv + nnx], refs[nr + nc + npv + nnx:]
        i = pl.program_id(0)
        vals = [r[...] for r in ins[:nr + nc]]
        vals += [jnp.where(i == 0, 0.0, r[...]) for r in ins[nr + nc:nr + nc + npv]]
        vals += [jnp.where(i == n - 1, 0.0, r[...]) for r in ins[nr + nc + npv:]]
        res = fn(*vals)
        if not isinstance(res, (tuple, list)):
            res = (res,)
        for k in range(no):
            out_refs[k][...] = res[k].astype(out_refs[k].dtype)
        if na:
            @pl.when(i == 0)
            def _():
                for k in range(na):
                    out_refs[no + k][...] = jnp.zeros_like(out_refs[no + k])

            for k in range(na):
                out_refs[no + k][...] += res[no + k]

    in_specs = []
    for a, w in zip(rows, widths):
        if w is None:
            in_specs.append(pl.BlockSpec((tr, a.shape[1]), lambda i: (i, 0)))
        else:
            in_specs.append(pl.BlockSpec((tr, w[1]), functools.partial(lambda i, cb: (i, cb), cb=w[0])))
    for c in consts:
        in_specs.append(pl.BlockSpec(c.shape, functools.partial(lambda i, nd: (0,) * nd, nd=c.ndim)))
    t8 = tr // 8
    for a in prev:
        in_specs.append(pl.BlockSpec((8, a.shape[1]), lambda i: (jnp.maximum(i * t8 - 1, 0), 0)))
    for a in nxt:
        in_specs.append(pl.BlockSpec((8, a.shape[1]), lambda i: (jnp.minimum((i + 1) * t8, s // 8 - 1), 0)))
    out_shape = [jax.ShapeDtypeStruct((s, w), dt) for w, dt in outs]
    out_specs = [pl.BlockSpec((tr, w), lambda i: (i, 0)) for w, _ in outs]
    out_shape += [jax.ShapeDtypeStruct(a, F32) for a in accs]
    out_specs += [pl.BlockSpec(a, lambda i: (0, 0)) for a in accs]
    res = pl.pallas_call(
        body, grid=(n,), in_specs=in_specs, out_specs=out_specs, out_shape=out_shape, name=name,
        compiler_params=_params(("arbitrary",) if na else ("parallel",)),
    )(*rows, *consts, *prev, *nxt)
    return res


def _pick(n, cap):
    if n <= cap:
        return n
    best = None
    for d in range(128, cap + 1, 128):
        if n % d == 0:
            best = d
    assert best is not None, (n, cap)
    return best


def _mm(a, b, *, ta=False, tb=False, out_dtype=F32, add=None, epilogue=None, name, bm=1024, bn=512, bk=None):
    (k_dim, m) = a.shape if ta else a.shape[::-1]
    n = b.shape[0] if tb else b.shape[1]
    assert (b.shape[1] if tb else b.shape[0]) == k_dim
    if bk is None:
        bk = 1024 if ta else 2816
    bm, bn, bk = _pick(m, bm), _pick(n, bn), _pick(k_dim, bk)
    nk = k_dim // bk
    dims = (((0 if ta else 1,), (1 if tb else 0,)), ((), ()))
    assert add is None or epilogue is None
    epi_fn, extra, out_dtypes, vecs, n_sums = (tuple(epilogue or (None, [add] if add is not None else [], (out_dtype,)))
                                               + ((), 0))[:5]
    n_extra, n_vec, n_out = len(extra), len(vecs), len(out_dtypes)

    def body(*refs):
        a_ref, b_ref = refs[0], refs[1]
        extra_refs = refs[2:2 + n_extra + n_vec]
        out_refs = refs[2 + n_extra + n_vec:2 + n_extra + n_vec + n_out + n_sums]
        p = lax.dot_general(a_ref[...].astype(BF16), b_ref[...].astype(BF16), dims, preferred_element_type=F32)

        def finish(acc):
            if epi_fn is not None:
                res = epi_fn(acc, *[r[...] for r in extra_refs])
            else:
                res = (acc + extra_refs[0][...],) if n_extra else (acc,)
            for o_ref, r in zip(out_refs, res):
                o_ref[...] = r.reshape(o_ref.shape).astype(o_ref.dtype)

        if nk == 1:
            finish(p)
        else:
            acc_ref = refs[2 + n_extra + n_vec + n_out + n_sums]
            kk = pl.program_id(2)

            @pl.when(kk == 0)
            def _():
                acc_ref[...] = p

            @pl.when(kk > 0)
            def _():
                acc_ref[...] += p

            @pl.when(kk == nk - 1)
            def _():
                finish(acc_ref[...])

    a_spec = pl.BlockSpec((bk, bm), lambda i, j, k: (k, i)) if ta else pl.BlockSpec((bm, bk), lambda i, j, k: (i, k))
    b_spec = pl.BlockSpec((bn, bk), lambda i, j, k: (j, k)) if tb else pl.BlockSpec((bk, bn), lambda i, j, k: (k, j))
    tile = pl.BlockSpec((bm, bn), lambda i, j, k: (i, j))
    vec = pl.BlockSpec((1, bn), lambda i, j, k: (0, j))
    sums = pl.BlockSpec((1, 1, bn), lambda i, j, k: (i, 0, j))
    res = pl.pallas_call(
        body, grid=(m // bm, n // bn, nk), in_specs=[a_spec, b_spec] + [tile] * n_extra + [vec] * n_vec,
        out_specs=[tile] * n_out + [sums] * n_sums,
        out_shape=[jax.ShapeDtypeStruct((m, n), dt) for dt in out_dtypes]
        + [jax.ShapeDtypeStruct((m // bm, 1, n), F32)] * n_sums,
        scratch_shapes=[pltpu.VMEM((bm, bn), F32)] if nk > 1 else [],
        name=name, compiler_params=_params(("parallel", "parallel", "arbitrary")),
    )(a, b, *extra, *vecs)
    return res if epilogue is not None else res[0]


def _mm_rms_bwd(a, b, x, g, dx_add, *, tb=False, name, bf16_copy=False):
    n = x.shape[1]
    has_g = g is not None

    def fn(acc, xv, addv, *gv):
        dx, dg = _rms_bwd(xv, acc, gv[0] if has_g else None)
        dx = dx + addv
        return (dx,) + ((dx,) if bf16_copy else ()) + ((jnp.sum(dg, axis=0, keepdims=True),) if has_g else ())

    outs = (F32,) + ((BF16,) if bf16_copy else ())
    res = _mm(a, b, tb=tb, name=name, bm=512, bn=n, epilogue=(fn, [x, dx_add], outs, [g] if has_g else [], int(has_g)))
    if has_g:
        return tuple(res[:-1]) + (jnp.sum(res[-1], axis=0),)
    return tuple(res) + (None,)


def _rms_fwd(x, g, *, name):
    if g is None:
        return _rw(lambda xv: xv * _rstd(xv), [x], [], [(x.shape[1], BF16)], tr=512, name=name)[0]
    return _rw(lambda xv, gv: xv * _rstd(xv) * gv, [x], [g], [(x.shape[1], BF16)], tr=512, name=name)[0]


def _rms_bwd_call(x, dy, g, dx_add, *, name, tr=512, scale=None, bf16_copy=False):
    w = x.shape[1]
    has_g, has_add = g is not None, dx_add is not None

    def fn(*v):
        xv, dyv = v[0], v[1]
        if scale is not None:
            dyv = dyv * scale
        gv = v[2 + has_add] if has_g else None
        dx, dg = _rms_bwd(xv, dyv, gv)
        if has_add:
            dx = dx + v[2]
        return (dx,) + ((dx,) if bf16_copy else ()) + ((jnp.sum(dg, axis=0, keepdims=True),) if has_g else ())

    rows = [x, dy] + ([dx_add] if has_add else [])
    outs = [(w, F32)] + ([(w, BF16)] if bf16_copy else [])
    res = _rw(fn, rows, [g] if has_g else [], outs, [(1, w)] if has_g else [], tr=tr, name=name)
    return tuple(res) if has_g else tuple(res) + (None,)


def _conv_taps(hup, halo, cw):
    h1 = _shift_down(hup, halo, 1)
    h2 = _shift_down(hup, halo, 2)
    return h1, h2


def _ffn_fwd(x1, w, i):
    h2b = _rms_fwd(x1, w["norm_ffn"][i:i + 1], name="ffn_norm")
    hup = _mm(h2b, w["ffn_w_up"][i], tb=True, name="ffn_up", bn=1408)
    cw, cb = w["ffn_conv_w"][i], w["ffn_conv_b"][i:i + 1]

    def fn(hv, cwv, cbv, halo):
        h1, h2 = _conv_taps(hv, halo, cwv)
        hc = cbv + cwv[0:1] * h2
        hc = hc + cwv[1:2] * h1
        hc = hc + cwv[2:3] * hv
        g, up = hc[:, :D_FF], hc[:, D_FF:]
        return (g * _sigmoid(g)) * up

    a = _rw(fn, [hup], [cw, cb], [(D_FF, BF16)], tr=128, name="ffn_conv_gate", prev=[hup])[0]
    x2 = _mm(a, w["ffn_w_down"][i], add=x1, name="ffn_down")
    return x2, (h2b, hup, a)


def _ffn_bwd(dx2, dx2b, x1, saved, w, i):
    h2b, hup, a = saved
    cw, cb = w["ffn_conv_w"][i], w["ffn_conv_b"][i:i + 1]
    da = _mm(dx2b, w["ffn_w_down"][i], tb=True, name="ffn_down_dx", bn=1408)
    d_wdown = _mm(a, dx2b, ta=True, name="ffn_down_dw", bm=1408)

    def fn1(hv, dav, cwv, cbv, h_prev, h_next, da_next):
        tr = hv.shape[0]
        he = jnp.concatenate([hv, h_next], axis=0)
        dae = jnp.concatenate([dav, da_next], axis=0)
        h1, h2 = _conv_taps(he, h_prev, cwv)
        hc = cbv + cwv[0:1] * h2
        hc = hc + cwv[1:2] * h1
        hc = hc + cwv[2:3] * he
        g, up = hc[:, :D_FF], hc[:, D_FF:]
        sg = _sigmoid(g)
        d_up = dae * (g * sg)
        d_g = dae * up * (sg * (1.0 + g * (1.0 - sg)))
        d_hce = jnp.concatenate([d_g, d_up], axis=1)
        rows_e = tr + 8
        d_hup = (cwv[2:3] * d_hce[:tr] + cwv[1:2] * pltpu.roll(d_hce, rows_e - 1, 0)[:tr]
                 + cwv[0:1] * pltpu.roll(d_hce, rows_e - 2, 0)[:tr])
        d_hc = d_hce[:tr]
        col = lambda v: jnp.sum(v, axis=0, keepdims=True)
        return d_hup, col(d_hc), col(d_hc * h2[:tr]), col(d_hc * h1[:tr]), col(d_hc * hv)

    w2 = 2 * D_FF
    d_hup, d_cb, d_cw0, d_cw1, d_cw2 = _rw(fn1, [hup, da], [cw, cb], [(w2, BF16)], [(1, w2)] * 4, tr=128,
                                           name="ffn_conv_gate_bwd", prev=[hup], nxt=[hup, da])
    dx1, dx1b, d_norm = _mm_rms_bwd(d_hup, w["ffn_w_up"][i], x1, w["norm_ffn"][i:i + 1], dx2, name="ffn_up_dx",
                                    bf16_copy=True)
    d_wup_t = _mm(h2b, d_hup, ta=True, name="ffn_up_dw", bn=1408).T
    grads = {"ffn_w_up": d_wup_t, "ffn_w_down": d_wdown, "ffn_conv_b": d_cb[0],
             "ffn_conv_w": jnp.concatenate([d_cw0, d_cw1, d_cw2], axis=0), "norm_ffn": d_norm[0]}
    return dx1, dx1b, grads


def _ple_fwd(x2, p_i, w, i):
    rn = _rms_fwd(x2, None, name="ple_norm")
    gl = _mm(rn, w["ple_w_gate"][i], name="ple_gate")
    x3 = _mm(p_i, w["ple_w_proj"][i], tb=True, name="ple_proj",
             epilogue=(lambda pp, xv, g: (xv + _sigmoid(g) * pp,), [x2, gl], (F32,)))[0]
    return x3, (rn, gl)


def _ple_bwd(dx3, x2, p_i, saved, w, i):
    rn, gl = saved

    def fn(pp, dv, g):
        sg = _sigmoid(g)
        return dv * sg, dv * pp * (sg * (1.0 - sg))

    d_pp, d_pre = _mm(p_i, w["ple_w_proj"][i], tb=True, name="ple_proj_bwd", epilogue=(fn, [dx3, gl], (BF16, BF16)))
    d_wproj_t = _mm(p_i, d_pp, ta=True, name="ple_proj_dw", bn=1024).T
    d_wgate = _mm(rn, d_pre, ta=True, name="ple_gate_dw")
    dx2, dx2b, _ = _mm_rms_bwd(d_pre, w["ple_w_gate"][i], x2, None, dx3, tb=True, name="ple_gate_dx", bf16_copy=True)
    return dx2, dx2b, {"ple_w_proj": d_wproj_t, "ple_w_gate": d_wgate}


def _loss_and_grad(xf, tgt):
    def fn(xv, tv):
        d = xv - tv
        part = 0.5 * jnp.sum(jnp.mean(d * d, axis=-1, keepdims=True), axis=0, keepdims=True)
        return d * (1.0 / D_MODEL), part

    dx, loss = _rw(fn, [xf, tgt], [], [(D_MODEL, F32)], [(1, 1)], tr=512, name="loss_head")
    return loss[0, 0], dx


def _tril_mask():
    shp = (A_CHUNK, A_CHUNK)
    return lax.broadcasted_iota(jnp.int32, shp, 0) >= lax.broadcasted_iota(jnp.int32, shp, 1)


def _gmlp_fwd(z, w_s, b_s, v_gain, *, tr=512):
    s = z.shape[0]
    tr = min(tr, s)
    gw = A_WIDTH // A_GROUPS

    def body(z_ref, w_ref, b_ref, g_ref, y_ref):
        tril = _tril_mask()
        for g in range(A_GROUPS):
            wg = jnp.where(tril, w_ref[g], 0.0).astype(BF16)
            cols = slice(g * gw, (g + 1) * gw)
            vcols = slice(A_WIDTH + g * gw, A_WIDTH + (g + 1) * gw)
            for c in range(tr // A_CHUNK):
                rows = slice(c * A_CHUNK, (c + 1) * A_CHUNK)
                v = _gelu(z_ref[rows, vcols])
                vn = v * _rstd(v) * g_ref[:, cols]
                sv = jnp.dot(wg, vn.astype(BF16), preferred_element_type=F32) + b_ref[g]
                y_ref[rows, cols] = (_gelu(z_ref[rows, cols]) * sv).astype(BF16)

    return pl.pallas_call(
        body, grid=(s // tr,),
        in_specs=[pl.BlockSpec((tr, 2 * A_WIDTH), lambda i: (i, 0)),
                  pl.BlockSpec(w_s.shape, lambda i: (0, 0, 0)), pl.BlockSpec(b_s.shape, lambda i: (0, 0, 0)),
                  pl.BlockSpec(v_gain.shape, lambda i: (0, 0))],
        out_specs=pl.BlockSpec((tr, A_WIDTH), lambda i: (i, 0)),
        out_shape=jax.ShapeDtypeStruct((s, A_WIDTH + B_WIDTH), BF16), name="gmlp_fwd",
        compiler_params=_params(("parallel",)),
    )(z, w_s, b_s, v_gain)


def _gmlp_bwd(z, d_ymix, w_s, b_s, v_gain, dz, *, tr=512):
    s = z.shape[0]
    tr = min(tr, s)
    gw = A_WIDTH // A_GROUPS

    def body(z_ref, dy_ref, w_ref, b_ref, g_ref, _, dz_ref, dw_ref, db_ref, dg_ref):
        @pl.when(pl.program_id(0) == 0)
        def _():
            dw_ref[...] = jnp.zeros_like(dw_ref)
            db_ref[...] = jnp.zeros_like(db_ref)
            dg_ref[...] = jnp.zeros_like(dg_ref)

        tril = _tril_mask()
        for g in range(A_GROUPS):
            wg = jnp.where(tril, w_ref[g], 0.0).astype(BF16)
            cols = slice(g * gw, (g + 1) * gw)
            vcols = slice(A_WIDTH + g * gw, A_WIDTH + (g + 1) * gw)
            gain = g_ref[:, cols]
            for c in range(tr // A_CHUNK):
                rows = slice(c * A_CHUNK, (c + 1) * A_CHUNK)
                va, ua = z_ref[rows, vcols], z_ref[rows, cols]
                v = _gelu(va)
                r = _rstd(v)
                vh = v * r
                vnb = (vh * gain).astype(BF16)
                sv = jnp.dot(wg, vnb, preferred_element_type=F32) + b_ref[g]
                dy = dy_ref[rows, cols]
                d_sv = dy * _gelu(ua)
                dz_ref[rows, cols] = (dy * sv * _gelu_grad(ua)).astype(BF16)
                d_svb = d_sv.astype(BF16)
                d_vn = lax.dot_general(wg, d_svb, (((0,), (0,)), ((), ())), preferred_element_type=F32)
                dwp = lax.dot_general(d_svb, vnb, (((1,), (1,)), ((), ())), preferred_element_type=F32)
                dw_ref[g] += jnp.where(tril, dwp, 0.0)
                db_ref[g] += jnp.sum(d_sv, axis=1, keepdims=True)
                dg_ref[:, cols] += jnp.sum(d_vn * vh, axis=0, keepdims=True)
                d_vh = d_vn * gain
                d_v = r * (d_vh - vh * jnp.mean(d_vh * vh, axis=-1, keepdims=True))
                dz_ref[rows, vcols] = (d_v * _gelu_grad(va)).astype(BF16)

    return pl.pallas_call(
        body, grid=(s // tr,),
        in_specs=[pl.BlockSpec((tr, 2 * A_WIDTH), lambda i: (i, 0)), pl.BlockSpec((tr, A_WIDTH), lambda i: (i, 0)),
                  pl.BlockSpec(w_s.shape, lambda i: (0, 0, 0)), pl.BlockSpec(b_s.shape, lambda i: (0, 0, 0)),
                  pl.BlockSpec(v_gain.shape, lambda i: (0, 0)), _ANY],
        out_specs=[pl.BlockSpec((tr, 2 * A_WIDTH), lambda i: (i, 0)), pl.BlockSpec(w_s.shape, lambda i: (0, 0, 0)),
                   pl.BlockSpec(b_s.shape, lambda i: (0, 0, 0)), pl.BlockSpec(v_gain.shape, lambda i: (0, 0))],
        out_shape=[jax.ShapeDtypeStruct(dz.shape, dz.dtype), jax.ShapeDtypeStruct(w_s.shape, F32),
                   jax.ShapeDtypeStruct(b_s.shape, F32), jax.ShapeDtypeStruct(v_gain.shape, F32)],
        input_output_aliases={5: 0}, name="gmlp_bwd", compiler_params=_params(("arbitrary",)),
    )(z, d_ymix, w_s, b_s, v_gain, dz)


def _dot3(x, ub):
    x1 = x.astype(BF16)
    r1 = x - x1.astype(F32)
    x2 = r1.astype(BF16)
    x3 = (r1 - x2.astype(F32)).astype(BF16)
    d = lambda a: jnp.dot(a, ub, preferred_element_type=F32)
    return d(x1) + d(x2) + d(x3)


def _log_sigmoid(x):
    return jnp.minimum(x, 0.0) - jnp.log(1.0 + jnp.exp(-jnp.abs(x)))


def _fgate_fwd(f_t, b_col, *, tb=256):
    h, s = f_t.shape
    tb = min(tb, s)

    def body(f_ref, b_ref, c_ref, carry):
        @pl.when(pl.program_id(0) == 0)
        def _():
            carry[...] = jnp.zeros_like(carry)

        lf = _log_sigmoid(f_ref[...] + b_ref[...])
        shp = (tb, tb)
        upper = (lax.broadcasted_iota(jnp.int32, shp, 0) <= lax.broadcasted_iota(jnp.int32, shp, 1)).astype(BF16)
        c_ref[...] = _dot3(lf, upper) + carry[...]
        carry[...] += jnp.sum(lf, axis=1, keepdims=True)

    return pl.pallas_call(
        body, grid=(s // tb,),
        in_specs=[pl.BlockSpec((h, tb), lambda i: (0, i)), pl.BlockSpec((h, 1), lambda i: (0, 0))],
        out_specs=pl.BlockSpec((h, tb), lambda i: (0, i)), out_shape=jax.ShapeDtypeStruct((h, s), F32),
        scratch_shapes=[pltpu.VMEM((h, 1), F32)], name="fgate_fwd", compiler_params=_params(("arbitrary",)),
    )(f_t, b_col)


def _fgate_bwd(f_t, b_col, dc_a, dc_b, *, tb=256):
    h, s = f_t.shape
    tb = min(tb, s)
    n = s // tb

    def body(f_ref, b_ref, da_ref, db_ref, df_ref, dbias_ref, carry):
        @pl.when(pl.program_id(0) == 0)
        def _():
            carry[...] = jnp.zeros_like(carry)
            dbias_ref[...] = jnp.zeros_like(dbias_ref)

        dc = da_ref[...] + db_ref[...]
        shp = (tb, tb)
        lower = (lax.broadcasted_iota(jnp.int32, shp, 0) >= lax.broadcasted_iota(jnp.int32, shp, 1)).astype(BF16)
        d_lf = _dot3(dc, lower) + carry[...]
        carry[...] += jnp.sum(dc, axis=1, keepdims=True)
        df = d_lf * (1.0 - _sigmoid(f_ref[...] + b_ref[...]))
        df_ref[...] = df
        dbias_ref[...] += jnp.sum(df, axis=1, keepdims=True)

    blk = pl.BlockSpec((h, tb), lambda i: (0, n - 1 - i))
    return pl.pallas_call(
        body, grid=(n,), in_specs=[blk, pl.BlockSpec((h, 1), lambda i: (0, 0)), blk, blk],
        out_specs=[blk, pl.BlockSpec((h, 1), lambda i: (0, 0))],
        out_shape=[jax.ShapeDtypeStruct((h, s), F32), jax.ShapeDtypeStruct((h, 1), F32)],
        scratch_shapes=[pltpu.VMEM((h, 1), F32)], name="fgate_bwd", compiler_params=_params(("arbitrary",)),
    )(f_t, b_col, dc_a, dc_b)


_NT = (((1,), (1,)), ((), ()))


def _causal(shape, row0, col0, transposed):
    r = lax.broadcasted_iota(jnp.int32, shape, 0) + row0
    c = lax.broadcasted_iota(jnp.int32, shape, 1) + col0
    return (r <= c) if transposed else (c <= r)


_ATT_SCALE = B_HEAD_DIM ** -0.5
ATT_W = 128
_COL_CQ, _COL_CK, _COL_LSE, _COL_DELTA = 64, 67, 70, 64
_ATT_BLK = 2048
_ATT_SUB = 512


def _split3(x):
    h = x.astype(BF16).astype(F32)
    r = x - h
    m = r.astype(BF16).astype(F32)
    return h, m, (r - m).astype(BF16).astype(F32)


def _put_cols(base, lane, col0, parts):
    for t, pv in enumerate(parts):
        base = jnp.where(lane == col0 + t, pv, base)
    return base


_HEAD_PAIRS = B_HEADS // 2
_Q_BLOCK, _K_BLOCK, _V_BLOCK = 8, 12, 16
_F_BLOCK = 20
_O_BLOCK = 4
_ANY = pl.BlockSpec(memory_space=pl.ANY)


def _head_half(ref, h):
    x = ref[...]
    x = jnp.where((h & 1) == 1, pltpu.roll(x, B_HEAD_DIM, 1), x)
    return jnp.where(lax.broadcasted_iota(jnp.int32, x.shape, 1) < B_HEAD_DIM, x, 0.0)


def _pair(a, b):
    lane = lax.broadcasted_iota(jnp.int32, a.shape, 1)
    return jnp.where(lane < B_HEAD_DIM, a, pltpu.roll(b, B_HEAD_DIM, 1))


def _attn_prep(z, c_col, gq, gk, *, tr=1024):
    s = z.shape[0]
    tr = min(tr, s)
    n = s // tr

    def body(q_ref, k_ref, v_ref, c_ref, gq_ref, gk_ref, qo_ref, ko_ref, vo_ref):
        h = pl.program_id(0)
        q, k, v = _head_half(q_ref, h), _head_half(k_ref, h), _head_half(v_ref, h)
        lane = lax.broadcasted_iota(jnp.int32, q.shape, 1)
        rq = lax.rsqrt(jnp.sum(q * q, axis=-1, keepdims=True) * (1.0 / B_HEAD_DIM) + EPS)
        rk = lax.rsqrt(jnp.sum(k * k, axis=-1, keepdims=True) * (1.0 / B_HEAD_DIM) + EPS)
        c = c_ref[...]
        ch, cm, cl = _split3(c)
        one = jnp.ones_like(c)
        qq = _put_cols(q * rq * gq_ref[...] * _ATT_SCALE, lane, _COL_CQ, (ch, cm, cl))
        qq = _put_cols(qq, lane, _COL_CK, (one, one, one))
        kk = _put_cols(k * rk * gk_ref[...], lane, _COL_CQ, (one, one, one))
        kk = _put_cols(kk, lane, _COL_CK, (-ch, -cm, -cl))
        kk = _put_cols(kk, lane, _COL_LSE, (one, one, one))
        qo_ref[...] = qq.astype(BF16)
        ko_ref[...] = kk.astype(BF16)
        vo_ref[...] = _put_cols(v, lane, _COL_DELTA, (one, one, one)).astype(BF16)

    def zcol(base):
        return pl.BlockSpec((tr, ATT_W), lambda h, i: (i, base + (h >> 1)))

    rows = pl.BlockSpec((tr, ATT_W), lambda h, i: (h * n + i, 0))
    gain = pl.BlockSpec((1, ATT_W), lambda h, i: (0, 0))
    return pl.pallas_call(
        body, grid=(B_HEADS, n),
        in_specs=[zcol(_Q_BLOCK), zcol(_K_BLOCK), zcol(_V_BLOCK), pl.BlockSpec((tr, 1), lambda h, i: (h * n + i, 0)),
                  gain, gain],
        out_specs=[rows] * 3, out_shape=[jax.ShapeDtypeStruct((B_HEADS * s, ATT_W), BF16)] * 3, name="attn_prep",
        compiler_params=_params(("parallel", "parallel")),
    )(z, z, z, c_col, gq, gk)


def _attn_bwd_prep(d_ymix, o, qp, lse, *, tr=1024):
    s = d_ymix.shape[0]
    tr = min(tr, s)
    n = s // tr

    def body(do_ref, o_ref, q_ref, l_ref, dop_ref, qpp_ref):
        dov = _head_half(do_ref, pl.program_id(0))
        lane = lax.broadcasted_iota(jnp.int32, dov.shape, 1)
        dh, dm, dl = _split3(jnp.sum(dov * o_ref[...], axis=-1, keepdims=True))
        lh, lm, ll = _split3(l_ref[...])
        dop_ref[...] = _put_cols(dov, lane, _COL_DELTA, (-dh, -dm, -dl)).astype(BF16)
        qpp_ref[...] = _put_cols(q_ref[...].astype(F32), lane, _COL_LSE, (-lh, -lm, -ll)).astype(BF16)

    rows = pl.BlockSpec((tr, ATT_W), lambda h, i: (h * n + i, 0))
    return pl.pallas_call(
        body, grid=(B_HEADS, n),
        in_specs=[pl.BlockSpec((tr, ATT_W), lambda h, i: (i, _O_BLOCK + (h >> 1))), rows, rows,
                  pl.BlockSpec((tr, 1), lambda h, i: (h * n + i, 0))],
        out_specs=[rows] * 2, out_shape=[jax.ShapeDtypeStruct((B_HEADS * s, ATT_W), BF16)] * 2,
        name="attn_bwd_prep", compiler_params=_params(("parallel", "parallel")),
    )(d_ymix, o, qp, lse)


def _attn_merge_out(o, ymix, *, tr=1024):
    s = o.shape[1]
    tr = min(tr, s)

    def body(a_ref, b_ref, buf_ref, out_ref):
        out_ref[...] = _pair(a_ref[0], b_ref[0]).astype(BF16)

    return pl.pallas_call(
        body, grid=(_HEAD_PAIRS, s // tr),
        in_specs=[pl.BlockSpec((1, tr, ATT_W), lambda j, i: (2 * j, i, 0)),
                  pl.BlockSpec((1, tr, ATT_W), lambda j, i: (2 * j + 1, i, 0)), _ANY],
        out_specs=pl.BlockSpec((tr, ATT_W), lambda j, i: (i, _O_BLOCK + j)),
        out_shape=jax.ShapeDtypeStruct(ymix.shape, ymix.dtype), input_output_aliases={2: 0}, name="attn_merge_out",
        compiler_params=_params(("parallel", "parallel")),
    )(o, o, ymix)


def _pair_bwd(d_heads, dz, out_block, *, z=None, in_block=None, gain=None, scale=None, name, tr=1024):
    s = d_heads.shape[1]
    tr = min(tr, s)
    norm = gain is not None

    def body(*refs):
        a_ref, b_ref = refs[0], refs[1]
        d = _pair(a_ref[0], b_ref[0])
        if norm:
            x_ref, g_ref, _, out_ref, dg_ref = refs[2:]

            @pl.when((pl.program_id(0) == 0) & (pl.program_id(1) == 0))
            def _():
                dg_ref[...] = jnp.zeros_like(dg_ref)

            x = x_ref[...]
            lo = lax.broadcasted_iota(jnp.int32, x.shape, 1) < B_HEAD_DIM

            def half_mean(v):
                s_lo = jnp.sum(jnp.where(lo, v, 0.0), axis=-1, keepdims=True)
                s_hi = jnp.sum(jnp.where(lo, 0.0, v), axis=-1, keepdims=True)
                return jnp.where(lo, s_lo, s_hi) * (1.0 / B_HEAD_DIM)

            r = lax.rsqrt(half_mean(x * x) + EPS)
            xh = x * r
            dy = d * scale if scale is not None else d
            dyg = dy * g_ref[...]
            d = r * (dyg - xh * half_mean(dyg * xh))
            dg_ref[...] += jnp.sum(dy * xh, axis=0, keepdims=True)
        else:
            out_ref = refs[3]
        out_ref[...] = d.astype(BF16)

    heads = [pl.BlockSpec((1, tr, ATT_W), lambda j, i: (2 * j, i, 0)),
             pl.BlockSpec((1, tr, ATT_W), lambda j, i: (2 * j + 1, i, 0))]
    out_spec = pl.BlockSpec((tr, ATT_W), lambda j, i: (i, out_block + j))
    dz_shape = jax.ShapeDtypeStruct(dz.shape, dz.dtype)
    if norm:
        res = pl.pallas_call(
            body, grid=(_HEAD_PAIRS, s // tr),
            in_specs=heads + [pl.BlockSpec((tr, ATT_W), lambda j, i: (i, in_block + j)),
                              pl.BlockSpec((1, ATT_W), lambda j, i: (0, 0)), _ANY],
            out_specs=[out_spec, pl.BlockSpec((1, ATT_W), lambda j, i: (0, 0))],
            out_shape=[dz_shape, jax.ShapeDtypeStruct((1, ATT_W), F32)], input_output_aliases={4: 0}, name=name,
            compiler_params=_params(("arbitrary", "arbitrary")),
        )(d_heads, d_heads, z, gain, dz)
        return res[0], res[1]
    return pl.pallas_call(
        body, grid=(_HEAD_PAIRS, s // tr), in_specs=heads + [_ANY], out_specs=out_spec, out_shape=dz_shape,
        input_output_aliases={2: 0}, name=name, compiler_params=_params(("parallel", "parallel")),
    )(d_heads, d_heads, dz), None


def _attn_fwd(qp, kp, vp, *, blk=_ATT_BLK):
    h, s, d = qp.shape
    b = min(blk, s)
    n = s // b
    sub = min(_ATT_SUB, b)
    ns = b // sub

    def body(q_ref, k_ref, v_ref, o_ref, lse_ref):
        i = pl.program_id(1)
        qs = [q_ref[0, t * sub:(t + 1) * sub, :] for t in range(ns)]

        def tile(j, state, masked):
            off = pl.multiple_of(j * b, b)
            kk = k_ref[0, pl.ds(off, b), :]
            vv = v_ref[0, pl.ds(off, b), :]
            new = []
            for t in range(ns):
                m, l, acc = state[3 * t:3 * t + 3]
                nk = (t + 1) * sub if masked else b
                sc = lax.dot_general(qs[t], kk[:nk], _NT, preferred_element_type=F32)
                if masked:
                    sc = jnp.where(_causal((sub, nk), t * sub, 0, False), sc, NEG_INF)
                m_new = jnp.maximum(m, jnp.max(sc, axis=1, keepdims=True))
                alpha = jnp.exp(m - m_new)
                p = jnp.exp(sc - m_new)
                new += [m_new, alpha * l + jnp.sum(p, axis=1, keepdims=True),
                        alpha * acc + jnp.dot(p.astype(BF16), vv[:nk], preferred_element_type=F32)]
            return tuple(new)

        init = (jnp.full((sub, 1), NEG_INF, F32), jnp.zeros((sub, 1), F32), jnp.zeros((sub, d), F32)) * ns
        state = lax.fori_loop(0, i, lambda j, st: tile(j, st, False), init)
        state = tile(i, state, True)
        for t in range(ns):
            m, l, acc = state[3 * t:3 * t + 3]
            o_ref[0, t * sub:(t + 1) * sub, :] = acc / l
            lse_ref[0, t * sub:(t + 1) * sub, :] = m + jnp.log(l)

    blk_spec = pl.BlockSpec((1, b, d), lambda hh, i: (hh, i, 0))
    full = pl.BlockSpec((1, s, d), lambda hh, i: (hh, 0, 0))
    return pl.pallas_call(
        body, grid=(h, n), in_specs=[blk_spec, full, full],
        out_specs=[blk_spec, pl.BlockSpec((1, b, 1), lambda hh, i: (hh, i, 0))],
        out_shape=[jax.ShapeDtypeStruct((h, s, d), F32), jax.ShapeDtypeStruct((h, s, 1), F32)],
        name="attn_fwd", compiler_params=_params(("parallel", "arbitrary")),
    )(qp, kp, vp)


_ATT_BWD_BLK = 1024


def _attn_bwd(qpp, kp, vp, dop, *, blk=_ATT_BWD_BLK):
    h, s, d = qpp.shape
    b = min(blk, s)
    n = s // b
    sub = min(_ATT_SUB, b)
    ns = b // sub

    def body(k_ref, v_ref, q_ref, do_ref, dq_ref, dk_ref, dv_ref):
        j = pl.program_id(1)

        @pl.when(j == 0)
        def _():
            dq_ref[...] = jnp.zeros_like(dq_ref)

        ks = [k_ref[0, t * sub:(t + 1) * sub, :] for t in range(ns)]
        vs = [v_ref[0, t * sub:(t + 1) * sub, :] for t in range(ns)]

        def tile(i, state, masked):
            off = pl.multiple_of(i * b, b)
            qq = q_ref[0, pl.ds(off, b), :]
            dd = do_ref[0, pl.ds(off, b), :]
            new = []
            dq = None
            for t in range(ns):
                q0 = t * sub if masked else 0
                qt, dt = qq[q0:], dd[q0:]
                pt = jnp.exp(lax.dot_general(ks[t], qt, _NT, preferred_element_type=F32))
                if masked:
                    pt = jnp.where(_causal((sub, b - q0), q0, q0, True), pt, 0.0)
                dst = (pt * lax.dot_general(vs[t], dt, _NT, preferred_element_type=F32)).astype(BF16)
                new += [state[2 * t] + jnp.dot(dst, qt, preferred_element_type=F32),
                        state[2 * t + 1] + jnp.dot(pt.astype(BF16), dt, preferred_element_type=F32)]
                part = lax.dot_general(dst, ks[t], _TN, preferred_element_type=F32)
                if q0:
                    part = jnp.concatenate([jnp.zeros((q0, d), F32), part], axis=0)
                dq = part if dq is None else dq + part
            dq_ref[0, pl.ds(off, b), :] += dq
            return tuple(new)

        state = tile(j, (jnp.zeros((sub, d), F32),) * (2 * ns), True)
        state = lax.fori_loop(j + 1, n, lambda i, st: tile(i, st, False), state)
        for t in range(ns):
            dk_ref[0, t * sub:(t + 1) * sub, :] = state[2 * t]
            dv_ref[0, t * sub:(t + 1) * sub, :] = state[2 * t + 1]

    blk_spec = pl.BlockSpec((1, b, d), lambda hh, j: (hh, j, 0))
    full = pl.BlockSpec((1, s, d), lambda hh, j: (hh, 0, 0))
    return pl.pallas_call(
        body, grid=(h, n), in_specs=[blk_spec, blk_spec, full, full], out_specs=[full, blk_spec, blk_spec],
        out_shape=[jax.ShapeDtypeStruct((h, s, d), F32)] * 3, name="attn_bwd",
        compiler_params=_params(("parallel", "arbitrary")),
    )(kp, vp, qpp, dop)


def _pad_head(g):
    return jnp.pad(g, ((0, 0), (0, ATT_W - B_HEAD_DIM)))


def _even_fwd(x0, w, e, i):
    s = x0.shape[0]
    hs = B_HEADS * s
    hb = _rms_fwd(x0, w["norm_mix"][i:i + 1], name="mix_norm")
    z = _mm(hb, w["ev_w_in"][e], tb=True, name="ev_in", bn=896)
    b_s = w["ev_b_spatial"][e][:, :, None]
    v_gain = w["ev_v_norm"][e:e + 1]
    ymix = _gmlp_fwd(z, w["ev_w_spatial"][e], b_s, v_gain)
    f_t = z[:, IN_COLS - B_HEADS:IN_COLS].T
    c = _fgate_fwd(f_t, w["ev_b_fgate"][e][:, None])
    qp, kp, vp = _attn_prep(z, c.reshape(hs, 1), _pad_head(w["ev_q_norm"][e:e + 1]),
                            _pad_head(w["ev_k_norm"][e:e + 1]))
    shp = (B_HEADS, s, ATT_W)
    o, lse = _attn_fwd(qp.reshape(shp), kp.reshape(shp), vp.reshape(shp))
    ymix = _attn_merge_out(o, ymix)
    x1 = _mm(ymix, w["ev_w_out"][e], add=x0, name="ev_out")
    return x1, (hb, z, ymix, qp, kp, vp, f_t, o, lse)


def _even_bwd(dx1, dx1b, x0, saved, w, e, i):
    hb, z, ymix, qp, kp, vp, f_t, o, lse = saved
    s = x0.shape[0]
    hs = B_HEADS * s
    d_ymix = _mm(dx1b, w["ev_w_out"][e], tb=True, name="ev_out_dx")
    d_wout = _mm(ymix, dx1b, ta=True, name="ev_out_dw")
    dop, qpp = _attn_bwd_prep(d_ymix, o.reshape(hs, ATT_W), qp, lse.reshape(hs, 1))
    shp = (B_HEADS, s, ATT_W)
    qpp, dop, kp3, vp3 = qpp.reshape(shp), dop.reshape(shp), kp.reshape(shp), vp.reshape(shp)
    dqp, dkp, dvp = _attn_bwd(qpp, kp3, vp3, dop)
    d_ft, d_bf = _fgate_bwd(f_t, w["ev_b_fgate"][e][:, None], dqp[:, :, _COL_CQ], -dkp[:, :, _COL_CK])
    dz = jnp.pad(d_ft.T.astype(BF16), ((0, 0), (_F_BLOCK * ATT_W, IN_COLS_PAD - IN_COLS)))
    b_s = w["ev_b_spatial"][e][:, :, None]
    v_gain = w["ev_v_norm"][e:e + 1]
    dz, d_ws, d_bs, d_vg = _gmlp_bwd(z, d_ymix, w["ev_w_spatial"][e], b_s, v_gain, dz)
    twice = lambda g: jnp.concatenate([g, g], axis=1)
    dz, d_gq = _pair_bwd(dqp, dz, _Q_BLOCK, z=z, in_block=_Q_BLOCK, gain=twice(w["ev_q_norm"][e:e + 1]),
                         scale=_ATT_SCALE, name="attn_qnorm_bwd")
    dz, d_gk = _pair_bwd(dkp, dz, _K_BLOCK, z=z, in_block=_K_BLOCK, gain=twice(w["ev_k_norm"][e:e + 1]),
                         name="attn_knorm_bwd")
    dz, _ = _pair_bwd(dvp, dz, _V_BLOCK, name="attn_dv_out")
    dx0, d_nm = _mm_rms_bwd(dz, w["ev_w_in"][e], x0, w["norm_mix"][i:i + 1], dx1, name="ev_in_dx")
    d_win_t = _mm(hb, dz, ta=True, name="ev_in_dw", bn=1344)[:, :IN_COLS].T
    hd = B_HEAD_DIM
    grads = {"ev_w_in": d_win_t, "ev_w_out": d_wout, "ev_b_fgate": d_bf[:, 0],
             "ev_q_norm": d_gq[0, :hd] + d_gq[0, hd:], "ev_k_norm": d_gk[0, :hd] + d_gk[0, hd:], "ev_v_norm": d_vg[0],
             "ev_w_spatial": d_ws, "ev_b_spatial": d_bs[:, :, 0], "norm_mix": d_nm[0]}
    return dx0, grads


def _s5_disc(a_re, a_im, log_dt, b_re, b_im):
    dt = jnp.exp(log_dt)[:, None]
    lr, li = a_re, a_im
    mag = jnp.exp(lr * dt)
    ab_re, ab_im = mag * jnp.cos(li * dt), mag * jnp.sin(li * dt)
    den = lr * lr + li * li
    nr, ni = ab_re - 1.0, ab_im
    cr = (nr * lr + ni * li) / den
    ci = (ni * lr - nr * li) / den
    bb_re = cr[..., None] * b_re - ci[..., None] * b_im
    bb_im = cr[..., None] * b_im + ci[..., None] * b_re
    return ab_re, ab_im, bb_re, bb_im


_GPB = S5_GROUPS // S5_BLOCKS


def _blockdiag(t):
    a, b = t.shape[1:]
    eye = jnp.eye(_GPB, dtype=t.dtype)
    t = t.reshape(S5_BLOCKS, _GPB, a, 1, b) * eye[None, :, None, :, None]
    return t.reshape(S5_BLOCKS, _GPB * a, _GPB * b)


def _blockdiag_extract(m, a, b):
    eye = jnp.eye(_GPB, dtype=m.dtype)
    m = m.reshape(S5_BLOCKS, _GPB, a, _GPB, b) * eye[None, :, None, :, None]
    return jnp.sum(m, axis=3).reshape(S5_GROUPS, a, b)


_SCAN_ROWS = 8


def _cmul(a_r, a_i, b_r, b_i):
    return a_r * b_r - a_i * b_i, a_r * b_i + a_i * b_r


def _scan_tiles(yr_ref, yi_ref, a_r, a_i, c_r, c_i, *, rev, x_refs=None):
    tb, wl = yr_ref.shape
    ntile = tb // _SCAN_ROWS
    with_acc = x_refs is not None
    if rev:
        a_i = -a_i
    pw = [(a_r, a_i)]
    for _ in range(_SCAN_ROWS - 1):
        pw.append(_cmul(*pw[-1], a_r, a_i))
    sub = lax.broadcasted_iota(jnp.int32, (_SCAN_ROWS, wl), 0)
    dist = (_SCAN_ROWS - 1 - sub) if rev else sub
    zero = jnp.zeros((_SCAN_ROWS, wl), F32)
    steps = []
    for kk in (1, 2, 4):
        steps.append(((_SCAN_ROWS - kk) if rev else kk, jnp.where(dist >= kk, pw[kk - 1][0], zero),
                      jnp.where(dist >= kk, pw[kk - 1][1], zero)))
    e_r, e_i = zero, zero
    for d in range(_SCAN_ROWS):
        e_r = jnp.where(dist == d, pw[d][0], e_r)
        e_i = jnp.where(dist == d, pw[d][1], e_i)
    exit_row = 0 if rev else _SCAN_ROWS - 1

    def tile(q, carry):
        idx = (ntile - 1 - q) if rev else q
        rows = pl.ds(pl.multiple_of(idx * _SCAN_ROWS, _SCAN_ROWS), _SCAN_ROWS)
        y_r, y_i = yr_ref[rows, :], yi_ref[rows, :]
        for sh, k_r, k_i in steps:
            t_r, t_i = _cmul(k_r, k_i, pltpu.roll(y_r, sh, 0), pltpu.roll(y_i, sh, 0))
            y_r, y_i = y_r + t_r, y_i + t_i
        cb_r = jnp.broadcast_to(carry[0], (_SCAN_ROWS, wl))
        cb_i = jnp.broadcast_to(carry[1], (_SCAN_ROWS, wl))
        t_r, t_i = _cmul(e_r, e_i, cb_r, cb_i)
        y_r, y_i = y_r + t_r, y_i + t_i
        yr_ref[rows, :] = y_r
        yi_ref[rows, :] = y_i
        out = (y_r[exit_row:exit_row + 1], y_i[exit_row:exit_row + 1])
        if with_acc:
            n_r = jnp.where(sub == _SCAN_ROWS - 1, cb_r, pltpu.roll(y_r, _SCAN_ROWS - 1, 0))
            n_i = jnp.where(sub == _SCAN_ROWS - 1, cb_i, pltpu.roll(y_i, _SCAN_ROWS - 1, 0))
            s_r, s_i = x_refs[0][rows, :], x_refs[1][rows, :]
            out += (carry[2] + (n_r * s_r + n_i * s_i), carry[3] + (n_i * s_r - n_r * s_i))
        return out

    return lax.fori_loop(0, ntile, tile, (c_r, c_i) + ((zero, zero) if with_acc else ()), unroll=2)


_S5_WL = S5_N // S5_BLOCKS
_S5_CW = D_MODEL // S5_BLOCKS
_TN = (((0,), (0,)), ((), ()))


def _s5_fwd(hb, mats, *, tb=1024):
    s = hb.shape[0]
    tb = min(tb, s)

    def body(h_ref, wbr_ref, wbi_ref, wcr_ref, wci_ref, ar_ref, ai_ref, xr_ref, xi_ref, cp_ref, cr, ci):
        @pl.when(pl.program_id(1) == 0)
        def _():
            cr[...] = jnp.zeros_like(cr)
            ci[...] = jnp.zeros_like(ci)

        hv = h_ref[...]
        xr_ref[...] = jnp.dot(hv, wbr_ref[0], preferred_element_type=F32)
        xi_ref[...] = jnp.dot(hv, wbi_ref[0], preferred_element_type=F32)
        cr[...], ci[...] = _scan_tiles(xr_ref, xi_ref, ar_ref[...], ai_ref[...], cr[...], ci[...], rev=False)
        cp_ref[...] = (jnp.dot(xr_ref[...].astype(BF16), wcr_ref[0], preferred_element_type=F32)
                       + jnp.dot(xi_ref[...].astype(BF16), wci_ref[0], preferred_element_type=F32))

    wide = pl.BlockSpec((tb, _S5_WL), lambda c, t: (t, c))
    narrow = pl.BlockSpec((tb, _S5_CW), lambda c, t: (t, c))
    w_in = pl.BlockSpec((1, _S5_CW, _S5_WL), lambda c, t: (c, 0, 0))
    w_out = pl.BlockSpec((1, _S5_WL, _S5_CW), lambda c, t: (c, 0, 0))
    vec = pl.BlockSpec((1, _S5_WL), lambda c, t: (0, c))
    return pl.pallas_call(
        body, grid=(S5_BLOCKS, s // tb), in_specs=[narrow, w_in, w_in, w_out, w_out, vec, vec],
        out_specs=[wide, wide, narrow],
        out_shape=[jax.ShapeDtypeStruct((s, S5_N), F32)] * 2 + [jax.ShapeDtypeStruct((s, D_MODEL), F32)],
        scratch_shapes=[pltpu.VMEM((1, _S5_WL), F32)] * 2, name="s5_fwd",
        compiler_params=_params(("parallel", "arbitrary")),
    )(hb, mats["wb_re"], mats["wb_im"], mats["wc_re"], mats["wc_imn"], mats["ar"], mats["ai"])


def _s5_bwd(dyb, hb, xr, xi, dhd, mats, *, tb=1024):
    s = hb.shape[0]
    tb = min(tb, s)
    nt = s // tb

    def body(dy_ref, h_ref, xr_ref, xi_ref, dhd_ref, wctr_ref, wcti_ref, wbtr_ref, wbti_ref, ar_ref, ai_ref,
             dh_ref, dar_ref, dai_ref, dwbr_ref, dwbi_ref, dwcr_ref, dwci_ref, lr, li, cr, ci):
        @pl.when(pl.program_id(1) == 0)
        def _():
            for r in (cr, ci, dar_ref, dai_ref, dwbr_ref, dwbi_ref, dwcr_ref, dwci_ref):
                r[...] = jnp.zeros_like(r)

        dyv, hv = dy_ref[...], h_ref[...]
        lr[...] = jnp.dot(dyv, wctr_ref[0], preferred_element_type=F32)
        li[...] = jnp.dot(dyv, wcti_ref[0], preferred_element_type=F32)
        cr[...], ci[...], acc_r, acc_i = _scan_tiles(lr, li, ar_ref[...], ai_ref[...], cr[...], ci[...], rev=True,
                                                     x_refs=(xr_ref, xi_ref))
        dar_ref[...] += jnp.sum(acc_r, axis=0, keepdims=True)
        dai_ref[...] += jnp.sum(acc_i, axis=0, keepdims=True)
        lrb, lib = lr[...].astype(BF16), li[...].astype(BF16)
        dh_ref[...] = (jnp.dot(lrb, wbtr_ref[0], preferred_element_type=F32) + dhd_ref[...]
                       + jnp.dot(lib, wbti_ref[0], preferred_element_type=F32))
        dwbr_ref[0] += lax.dot_general(hv, lrb, _TN, preferred_element_type=F32)
        dwbi_ref[0] += lax.dot_general(hv, lib, _TN, preferred_element_type=F32)
        dwcr_ref[0] += lax.dot_general(dyv, xr_ref[...].astype(BF16), _TN, preferred_element_type=F32)
        dwci_ref[0] += lax.dot_general(dyv, xi_ref[...].astype(BF16), _TN, preferred_element_type=F32)

    wide = pl.BlockSpec((tb, _S5_WL), lambda c, t: (nt - 1 - t, c))
    narrow = pl.BlockSpec((tb, _S5_CW), lambda c, t: (nt - 1 - t, c))
    w_in = pl.BlockSpec((1, _S5_CW, _S5_WL), lambda c, t: (c, 0, 0))
    w_out = pl.BlockSpec((1, _S5_WL, _S5_CW), lambda c, t: (c, 0, 0))
    vec = pl.BlockSpec((1, _S5_WL), lambda c, t: (0, c))
    acc_shape = jax.ShapeDtypeStruct((S5_BLOCKS, _S5_CW, _S5_WL), F32)
    return pl.pallas_call(
        body, grid=(S5_BLOCKS, nt), in_specs=[narrow, narrow, wide, wide, narrow, w_in, w_in, w_out, w_out, vec, vec],
        out_specs=[narrow, vec, vec, w_in, w_in, w_in, w_in],
        out_shape=[jax.ShapeDtypeStruct((s, D_MODEL), F32)] + [jax.ShapeDtypeStruct((1, S5_N), F32)] * 2
        + [acc_shape] * 4,
        scratch_shapes=[pltpu.VMEM((tb, _S5_WL), F32)] * 2 + [pltpu.VMEM((1, _S5_WL), F32)] * 2, name="s5_bwd",
        compiler_params=_params(("parallel", "arbitrary")),
    )(dyb, hb, xr, xi, dhd, mats["wct_re"], mats["wct_imn"], mats["wbt_re"], mats["wbt_im"], mats["ar"], mats["ai"])


def _s5_mats(w, o):
    ab_re, ab_im, bb_re, bb_im = _s5_disc(w["od_a_re"][o], w["od_a_im"][o], w["od_log_dt"][o], w["od_b_re"][o],
                                          w["od_b_im"][o])
    c_re, c_im = w["od_c_re"][o], w["od_c_im"][o]
    tr = lambda t: t.transpose(0, 2, 1)
    bd = lambda t: _blockdiag(t).astype(BF16)
    return {
        "ar": ab_re.reshape(1, S5_N), "ai": ab_im.reshape(1, S5_N),
        "wb_re": bd(tr(bb_re)), "wb_im": bd(tr(bb_im)), "wc_re": bd(tr(c_re)), "wc_imn": bd(-tr(c_im)),
        "wct_re": bd(c_re), "wct_imn": bd(-c_im), "wbt_re": bd(bb_re), "wbt_im": bd(bb_im),
    }


def _odd_fwd(x0, w, o, i):
    mats = _s5_mats(w, o)
    nm = w["norm_mix"][i:i + 1]
    d_row = w["od_d"][o:o + 1]
    hb = _rms_fwd(x0, nm, name="mix_norm")
    xr, xi, cp = _s5_fwd(hb, mats)

    def fn(xv, cpv, gv, dv):
        y = cpv + dv * (xv * _rstd(xv) * gv)
        return y, _gelu(y)

    y, gy = _rw(fn, [x0, cp], [nm, d_row], [(D_MODEL, F32), (D_MODEL, BF16)], tr=512, name="s5_out")
    gg = _mm(gy, w["od_w_glu"][o], tb=True, name="od_glu")
    x1 = _rw(lambda xv, g: xv + g[:, :D_MODEL] * _sigmoid(g[:, D_MODEL:]), [x0, gg], [], [(D_MODEL, F32)], tr=512,
             name="od_glu_out")[0]
    return x1, (hb, xr, xi, y, gy, gg)


def _odd_bwd(dx1, x0, saved, w, o, i):
    hb, xr, xi, y, gy, gg = saved
    mats = _s5_mats(w, o)
    nm = w["norm_mix"][i:i + 1]
    d_row = w["od_d"][o:o + 1]

    def fn_glu(dv, g):
        ga, gb = g[:, :D_MODEL], g[:, D_MODEL:]
        sg = _sigmoid(gb)
        return jnp.concatenate([dv * sg, dv * ga * (sg * (1.0 - sg))], axis=1)

    dgg = _rw(fn_glu, [dx1, gg], [], [(2 * D_MODEL, BF16)], tr=512, name="od_glu_out_bwd")[0]
    d_wglu = _mm(gy, dgg, ta=True, name="od_glu_dw", bn=1024).T
    d_gy = _mm(dgg, w["od_w_glu"][o], name="od_glu_dx")

    def fn_y(dg, yv, xv, gv, dv):
        dy = dg * _gelu_grad(yv)
        h = xv * _rstd(xv) * gv
        return dy, dv * dy, jnp.sum(dy * h, axis=0, keepdims=True)

    dyb, dhd, d_d = _rw(fn_y, [d_gy, y, x0], [nm, d_row], [(D_MODEL, BF16), (D_MODEL, F32)], [(1, D_MODEL)], tr=512,
                        name="s5_out_bwd")
    dh, d_ar, d_ai, d_wb_re, d_wb_im, d_wc_re, d_wc_im = _s5_bwd(dyb, hb, xr, xi, dhd, mats)
    gc, gp = S5_GROUP_CH, S5_STATE
    d_c_re = _blockdiag_extract(d_wc_re, gc, gp)
    d_c_im = -_blockdiag_extract(d_wc_im, gc, gp)
    d_bb_re = _blockdiag_extract(d_wb_re, gc, gp).transpose(0, 2, 1)
    d_bb_im = _blockdiag_extract(d_wb_im, gc, gp).transpose(0, 2, 1)
    dx0, d_nm = _rms_bwd_call(x0, dh, nm, dx1, name="mix_norm_bwd")
    _, vjp = jax.vjp(_s5_disc, w["od_a_re"][o], w["od_a_im"][o], w["od_log_dt"][o], w["od_b_re"][o], w["od_b_im"][o])
    d_a_re, d_a_im, d_log_dt, d_b_re, d_b_im = vjp(
        (d_ar.reshape(S5_GROUPS, S5_STATE), d_ai.reshape(S5_GROUPS, S5_STATE), d_bb_re, d_bb_im))
    grads = {"od_a_re": d_a_re, "od_a_im": d_a_im, "od_log_dt": d_log_dt, "od_b_re": d_b_re, "od_b_im": d_b_im,
             "od_c_re": d_c_re, "od_c_im": d_c_im, "od_d": d_d[0], "od_w_glu": d_wglu, "norm_mix": d_nm[0]}
    return dx0, grads


_HBM = pl.BlockSpec(memory_space=pltpu.HBM)


def _mesh_pos():
    return lax.axis_index("x"), lax.axis_index("y"), lax.axis_index("c")


def _slot(px, py, pc):
    return 4 * px + 2 * py + pc


def _all_gather(x, *, name):
    def body(x_ref, out_ref, send_sems, recv_sems, local_sem):
        mx, my, mc = _mesh_pos()
        me, sibling = (mx, my, mc), (mx, my, 1 - mc)
        chips = [(1 - mx, my), (mx, 1 - my), (1 - mx, 1 - my)]

        def copy(k, block, to, src=None):
            dst = out_ref.at[_slot(*block)]
            return pltpu.make_async_remote_copy(
                src_ref=dst if src is None else src, dst_ref=dst, send_sem=send_sems.at[k], recv_sem=recv_sems.at[k],
                device_id=to, device_id_type=MESH)

        mine = pltpu.make_async_copy(x_ref, out_ref.at[_slot(*me)], local_sem)
        mine.start()
        first = [copy(0, me, sibling, src=x_ref)]
        first += [copy(1 + j, me, (*chip, mc), src=x_ref) for j, chip in enumerate(chips)]
        for cp in first:
            cp.start()
        passed = [copy(4 + j, (*chip, mc), sibling) for j, chip in enumerate(chips)]
        for j, chip in enumerate(chips):
            copy(1 + j, (*chip, mc), me).wait_recv()
            passed[j].start()
        copy(0, sibling, me).wait_recv()
        for j, chip in enumerate(chips):
            copy(4 + j, (*chip, 1 - mc), me).wait_recv()
        for cp in first + passed:
            cp.wait_send()
        mine.wait()

    return pl.pallas_call(
        body, out_shape=jax.ShapeDtypeStruct((N_DEV,) + x.shape, x.dtype), in_specs=[_HBM], out_specs=_HBM,
        scratch_shapes=[pltpu.SemaphoreType.DMA((7,)), pltpu.SemaphoreType.DMA((7,)), pltpu.SemaphoreType.DMA],
        name=name,
    )(x)


_SEMS = pltpu.SemaphoreType.DMA((N_DEV - 1,))
_SEM_SPEC = pl.BlockSpec(memory_space=pltpu.SEMAPHORE)
_SPLIT_COPY = pltpu.CompilerParams(has_side_effects=pltpu.SideEffectType.DATAFLOW_SIDE_EFFECTING)


def _exchange_copies(x_refs, land_refs, send_sems, recv_sems, gather):
    mx, my, mc = _mesh_pos()
    my_slot = _slot(mx, my, mc)
    copies = []
    for a, (x_ref, land_ref) in enumerate(zip(x_refs, land_refs)):
        for k in range(1, N_DEV):
            peer = (1 - mx if k & 4 else mx, 1 - my if k & 2 else my, 1 - mc if k & 1 else mc)
            copies.append(pltpu.make_async_remote_copy(
                src_ref=x_ref if gather else x_ref.at[_slot(*peer)], dst_ref=land_ref.at[my_slot],
                send_sem=send_sems[a].at[k - 1], recv_sem=recv_sems[a].at[k - 1], device_id=peer, device_id_type=MESH))
    return copies


def _exchange_start(xs, carry, *, gather, name):
    n = len(xs)

    def body(*refs):
        outs = refs[2 * n + 1:]
        for cp in _exchange_copies(refs[:n], refs[n:2 * n], outs[:n], outs[n:2 * n], gather):
            cp.start()

    hbm = lambda a: pltpu.HBM(a.shape, a.dtype)
    lands = [pltpu.with_memory_space_constraint(lax.empty(((N_DEV,) if gather else ()) + x.shape, x.dtype), pltpu.HBM)
             for x in xs]
    ins = [pltpu.with_memory_space_constraint(a, pltpu.HBM) for a in list(xs) + [carry]]
    res = pl.pallas_call(
        body, name=name, in_specs=[_HBM] * (2 * n + 1), out_specs=[_SEM_SPEC] * (2 * n) + [_HBM] * (2 * n + 1),
        out_shape=[_SEMS] * (2 * n) + [hbm(a) for a in list(xs) + lands] + [hbm(carry)],
        input_output_aliases={**{a: 2 * n + a for a in range(2 * n)}, 2 * n: 4 * n}, compiler_params=_SPLIT_COPY,
    )(*ins[:n], *lands, ins[n])
    return (res[:n], res[n:2 * n], res[2 * n:3 * n], res[3 * n:4 * n]), res[4 * n]


def _exchange_wait(handles, after, *, gather, name):
    send_sems, recv_sems, xs, lands = handles
    n = len(xs)

    def body(*refs):
        for cp in _exchange_copies(refs[:n], refs[n:2 * n], refs[2 * n:3 * n], refs[3 * n:4 * n], gather):
            cp.wait_send()
            cp.wait_recv()

    hbm = lambda a: pltpu.HBM(a.shape, a.dtype)
    res = pl.pallas_call(
        body, name=name, in_specs=[_HBM] * (2 * n) + [_SEM_SPEC] * (2 * n) + [_ANY], out_specs=[_HBM] * (2 * n),
        out_shape=[hbm(a) for a in list(xs) + list(lands)], input_output_aliases={a: a for a in range(2 * n)},
        compiler_params=_SPLIT_COPY,
    )(*xs, *lands, *send_sems, *recv_sems, after)
    return res[:n], res[n:]


_ADAMW_PARTS_BLOCK_BYTES = 8 * 1024 * 1024


def _adamw(parts, w, m, v, *, name, own=None):
    r, c = w.shape
    tr = r
    for cand in range(8, r + 1, 8):
        if r % cand == 0 and N_DEV * cand * c * 4 <= _ADAMW_PARTS_BLOCK_BYTES:
            tr = cand
    has_own = own is not None

    def body(*refs):
        p_ref, w_ref, m_ref, v_ref = refs[0], refs[1 + has_own], refs[2 + has_own], refs[3 + has_own]
        g_ref, d_ref, nm_ref, nv_ref = refs[4 + has_own:]
        if has_own:
            me = _slot(*_mesh_pos())
            part = lambda k: jnp.where(me == k, refs[1][...], p_ref[k])
        else:
            part = lambda k: p_ref[k]
        g = part(0)
        for k in range(1, N_DEV):
            g = g + part(k)
        nm = ADAM_B1 * m_ref[...] + (1.0 - ADAM_B1) * g
        nv = ADAM_B2 * v_ref[...] + (1.0 - ADAM_B2) * (g * g)
        m_hat = nm / (1.0 - ADAM_B1 ** ADAM_STEP)
        v_hat = nv / (1.0 - ADAM_B2 ** ADAM_STEP)
        g_ref[...] = g
        d_ref[...] = -ADAM_LR * (m_hat / (jnp.sqrt(v_hat) + ADAM_EPS) + ADAM_WD * w_ref[...])
        nm_ref[...] = nm
        nv_ref[...] = nv

    blk = pl.BlockSpec((tr, c), lambda i: (i, 0))
    return pl.pallas_call(
        body, grid=(r // tr,),
        in_specs=[pl.BlockSpec((N_DEV, tr, c), lambda i: (0, i, 0))] + [blk] * (3 + has_own),
        out_specs=[blk] * 4, out_shape=[jax.ShapeDtypeStruct((r, c), F32)] * 4, name=name,
        compiler_params=_params(("parallel",)),
    )(parts, *([own] if has_own else []), w, m, v)


_FLAT_COLS = 1024
_FLAT_ROW_ALIGN = 128

_SHARD_AXIS = {
    "norm_mix": None, "norm_ffn": None, "ev_w_in": 2, "ev_b_fgate": None, "ev_q_norm": None, "ev_k_norm": None,
    "ev_v_norm": None, "ev_w_spatial": None, "ev_b_spatial": None, "ev_w_out": 1, "od_a_re": None, "od_a_im": None,
    "od_log_dt": None, "od_b_re": None, "od_b_im": None, "od_c_re": None, "od_c_im": None, "od_d": 1, "od_w_glu": 2,
    "ffn_w_up": 2, "ffn_conv_w": 2, "ffn_conv_b": None, "ffn_w_down": 1, "ple_w_proj": 2, "ple_w_gate": 1,
}
_WEIGHTS = list(_SHARD_AXIS)
_REPLICATED = [n for n in _WEIGHTS if _SHARD_AXIS[n] is None]
_COL_SHARDED = ["ev_w_in", "od_w_glu", "ffn_w_up", "ple_w_proj"]
_ROW_SHARDED = ["ev_w_out", "ffn_w_down", "ple_w_gate"]
_SMALL_SHARDED = ["od_d", "ffn_conv_w"]


def _pack(arrays, lead=()):
    nl = len(lead)
    flat = jnp.concatenate([a.reshape(lead + (-1,)) for a in arrays], axis=nl)
    n = flat.shape[nl]
    chunk = _FLAT_COLS * _FLAT_ROW_ALIGN
    total = -(-n // chunk) * chunk
    flat = jnp.pad(flat, [(0, 0)] * nl + [(0, total - n)])
    return flat.reshape(lead + (total // _FLAT_COLS, _FLAT_COLS))


def _unpack(buf, shapes, lead=()):
    nl = len(lead)
    flat = buf.reshape(lead + (-1,))
    out, off = [], 0
    for shp in shapes:
        n = math.prod(shp)
        out.append(lax.slice_in_dim(flat, off, off + n, axis=nl).reshape(lead + tuple(shp)))
        off += n
    return out


def _join_shards(g, axis):
    g = jnp.moveaxis(g, 0, axis)
    shp = g.shape
    return g.reshape(shp[:axis] + (shp[axis] * shp[axis + 1],) + shp[axis + 2:])


def _split_shards(full, axis):
    shp = full.shape
    g = full.reshape(shp[:axis] + (N_DEV, shp[axis] // N_DEV) + shp[axis + 1:])
    return jnp.moveaxis(g, axis, 0)


def _local_step(x, p, tgt, w, before_layer, on_grads):
    saved = []
    h = x
    for i in range(DEPTH):
        x0 = before_layer(i, h)
        if i % 2 == 0:
            x1, sv_mix = _even_fwd(x0, w, i // 2, i)
        else:
            x1, sv_mix = _odd_fwd(x0, w, i // 2, i)
        x2, sv_ffn = _ffn_fwd(x1, w, i)
        x3, sv_ple = _ple_fwd(x2, p[i], w, i)
        saved.append((x0, x1, x2, sv_mix, sv_ffn, sv_ple))
        h = x3
    loss, dh = _loss_and_grad(h, tgt)
    per_layer = {}
    for i in reversed(range(DEPTH)):
        x0, x1, x2, sv_mix, sv_ffn, sv_ple = saved[i]
        dx2, dx2b, g_ple = _ple_bwd(dh, x2, p[i], sv_ple, w, i)
        dx1, dx1b, g_ffn = _ffn_bwd(dx2, dx2b, x1, sv_ffn, w, i)
        dx1 = on_grads(i, 0, {**g_ple, **g_ffn}, dx1)
        if i % 2 == 0:
            dh, g_mix = _even_bwd(dx1, dx1b, x0, sv_mix, w, i // 2, i)
        else:
            dh, g_mix = _odd_bwd(dx1, x0, sv_mix, w, i // 2, i)
        dh = on_grads(i, 1, g_mix, dh)
        layer_grads = {**g_ple, **g_ffn, **g_mix}
        for name, g in layer_grads.items():
            per_layer.setdefault(name, []).append(g)
    grads = {name: gs[::-1] for name, gs in per_layer.items()}
    return loss, dh, grads


def kernel(x, p, norm_mix, norm_ffn, ev_w_in, ev_b_fgate, ev_q_norm, ev_k_norm, ev_v_norm, ev_w_spatial, ev_b_spatial, ev_w_out, od_a_re, od_a_im, od_log_dt, od_b_re, od_b_im, od_c_re, od_c_im, od_d, od_w_glu, ffn_w_up, ffn_conv_w, ffn_conv_b, ffn_w_down, ple_w_proj, ple_w_gate, loss_target, m_norm_mix, m_norm_ffn, m_ev_w_in, m_ev_b_fgate, m_ev_q_norm, m_ev_k_norm, m_ev_v_norm, m_ev_w_spatial, m_ev_b_spatial, m_ev_w_out, m_od_a_re, m_od_a_im, m_od_log_dt, m_od_b_re, m_od_b_im, m_od_c_re, m_od_c_im, m_od_d, m_od_w_glu, m_ffn_w_up, m_ffn_conv_w, m_ffn_conv_b, m_ffn_w_down, m_ple_w_proj, m_ple_w_gate, v_norm_mix, v_norm_ffn, v_ev_w_in, v_ev_b_fgate, v_ev_q_norm, v_ev_k_norm, v_ev_v_norm, v_ev_w_spatial, v_ev_b_spatial, v_ev_w_out, v_od_a_re, v_od_a_im, v_od_log_dt, v_od_b_re, v_od_b_im, v_od_c_re, v_od_c_im, v_od_d, v_od_w_glu, v_ffn_w_up, v_ffn_conv_w, v_ffn_conv_b, v_ffn_w_down, v_ple_w_proj, v_ple_w_gate):
    args = locals()
    wl = {n: args[n] for n in _WEIGHTS}
    ml = {n: args["m_" + n] for n in _WEIGHTS}
    vl = {n: args["v_" + n] for n in _WEIGHTS}

    kinds = ("g", "d", "m", "v")

    def local(n, layer, a):
        return a[n][layer].T if n in _COL_SHARDED else a[n][layer]

    me = _slot(*_mesh_pos())
    w = {n: wl[n] for n in _REPLICATED}
    small = _all_gather(_pack([wl[n] for n in _SMALL_SHARDED]), name="gather_small_weights")
    for n, g in zip(_SMALL_SHARDED, _unpack(small, [wl[n].shape for n in _SMALL_SHARDED], lead=(N_DEV,))):
        w[n] = _join_shards(g, _SHARD_AXIS[n])

    def layer_of(n, i):
        return i if wl[n].shape[0] == DEPTH else i // 2

    def weights_of(i):
        mixer = ["ev_w_in", "ev_w_out"] if i % 2 == 0 else ["od_w_glu"]
        return mixer + ["ffn_w_up", "ffn_w_down", "ple_w_proj", "ple_w_gate"]

    gathers = {}
    for n in _COL_SHARDED + _ROW_SHARDED:
        w[n] = {}

    def start_gather(i, carry):
        blocks = [local(n, layer_of(n, i), wl).astype(BF16) for n in weights_of(i)]
        gathers[i], carry = _exchange_start(blocks, carry, gather=True, name="gather_start_layer%d" % i)
        return carry

    def finish_gather(i, h):
        sent, landed = _exchange_wait(gathers[i], h, gather=True, name="gather_wait_layer%d" % i)
        for n, block, land in zip(weights_of(i), sent, landed):
            full = lax.dynamic_update_slice(land, block[None], (me, 0, 0)).reshape(N_DEV * block.shape[0], block.shape[1])
            if n == "ev_w_in":
                full = jnp.pad(full, ((0, IN_COLS_PAD - IN_COLS), (0, 0)))
            w[n][layer_of(n, i)] = full
        if i == 0:
            for later in range(1, DEPTH):
                h = start_gather(later, h)
        return h

    x_in = start_gather(0, x[0])

    in_flight = {}

    def start_scatter(i, part, part_grads, dh):
        names = [n for n in _COL_SHARDED + _ROW_SHARDED if n in part_grads]
        xs = [part_grads[n].reshape(N_DEV, part_grads[n].shape[0] // N_DEV, part_grads[n].shape[1]) for n in names]
        handles, dh = _exchange_start(xs, dh, gather=False, name="scatter_start_layer%d_part%d" % (i, part))
        in_flight[i, part] = (names, handles)
        return dh

    loss, grad_x, grads = _local_step(x_in, p[:, 0], loss_target[0], w, finish_gather, start_scatter)
    loss = lax.psum(loss, ("x", "y", "c"))

    rep_send = _pack([jnp.stack(grads[n]) for n in _REPLICATED])
    small_send = _pack([_split_shards(jnp.stack(grads[n]), _SHARD_AXIS[n]) for n in _SMALL_SHARDED], lead=(N_DEV,))
    rep_handles, grad_x = _exchange_start([rep_send], grad_x, gather=True, name="gather_start_replicated_grads")
    small_handles, grad_x = _exchange_start([small_send], grad_x, gather=False, name="scatter_start_small_grads")

    out = {kind: {} for kind in kinds}
    done = {}
    for i, part in sorted(in_flight, reverse=True):
        names, handles = in_flight[i, part]
        sent, landed = _exchange_wait(handles, grad_x, gather=False, name="scatter_wait_layer%d_part%d" % (i, part))
        for n, x_sent, parts in zip(names, sent, landed):
            layer = layer_of(n, i)
            own = lax.dynamic_index_in_dim(x_sent, me, axis=0, keepdims=False)
            res = _adamw(parts, local(n, layer, wl), local(n, layer, ml), local(n, layer, vl), name="adamw_" + n, own=own)
            done.setdefault(n, {})[layer] = [r.T if n in _COL_SHARDED else r for r in res]
    for n, layers in done.items():
        for k, kind in enumerate(kinds):
            out[kind][n] = jnp.stack([layers[layer][k] for layer in sorted(layers)])

    last = res[0]
    (small_sent,), (parts,) = _exchange_wait(small_handles, last, gather=False, name="scatter_wait_small_grads")
    shapes = [wl[n].shape for n in _SMALL_SHARDED]
    res = _adamw(parts, _pack([wl[n] for n in _SMALL_SHARDED]), _pack([ml[n] for n in _SMALL_SHARDED]),
                 _pack([vl[n] for n in _SMALL_SHARDED]), name="adamw_small",
                 own=lax.dynamic_index_in_dim(small_sent, me, axis=0, keepdims=False))
    for kind, buf in zip(kinds, res):
        out[kind].update(zip(_SMALL_SHARDED, _unpack(buf, shapes)))

    (rep_sent,), (parts,) = _exchange_wait(rep_handles, last, gather=True, name="gather_wait_replicated_grads")
    shapes = [wl[n].shape for n in _REPLICATED]
    res = _adamw(parts, _pack([wl[n] for n in _REPLICATED]), _pack([ml[n] for n in _REPLICATED]),
                 _pack([vl[n] for n in _REPLICATED]), name="adamw_replicated", own=rep_sent)
    for kind, buf in zip(kinds, res):
        out[kind].update(zip(_REPLICATED, _unpack(buf, shapes)))

    return (loss, grad_x[None], *[out["g"][n] for n in _WEIGHTS], *[out["d"][n] for n in _WEIGHTS],
            *[out["m"][n] for n in _WEIGHTS], *[out["v"][n] for n in _WEIGHTS])
```

```python
import functools
import math

import jax
import jax.numpy as jnp
from jax import lax
from jax.experimental import pallas as pl
from jax.experimental.pallas import tpu as pltpu

F32 = jnp.float32
BF16 = jnp.bfloat16

D_MODEL = 1024
DEPTH = 4
A_GROUPS = 4
A_CHUNK = 128
A_WIDTH = 512
B_HEADS = 8
B_HEAD_DIM = 64
B_WIDTH = 512
IN_COLS = 2 * A_WIDTH + 3 * B_WIDTH + B_HEADS
IN_COLS_PAD = 2688
S5_GROUP_CH = 16
S5_GROUPS = 64
S5_STATE = 64
S5_N = S5_GROUPS * S5_STATE
S5_BLOCKS = 8
D_FF = 2816
PLE_DIM = 256
EPS = 1e-6
NEG_INF = -1e30
N_DEV = 8

ADAM_LR = 0.001
ADAM_B1 = 0.9
ADAM_B2 = 0.999
ADAM_EPS = 1e-08
ADAM_WD = 0.01
ADAM_STEP = 10

VMEM_LIMIT_BYTES = 56 * 1024 * 1024
MESH = pl.DeviceIdType.MESH


def _params(sem, vmem=VMEM_LIMIT_BYTES):
    return pltpu.CompilerParams(dimension_semantics=sem, vmem_limit_bytes=vmem)


_GELU_K = math.sqrt(2.0 / math.pi)


def _gelu(x):
    return x * (0.5 * (1.0 + jnp.tanh(_GELU_K * (x + 0.044715 * (x * x * x)))))


def _gelu_grad(x):
    t = jnp.tanh(_GELU_K * (x + 0.044715 * (x * x * x)))
    return 0.5 * (1.0 + t) + 0.5 * x * (1.0 - t * t) * (_GELU_K * (1.0 + 3.0 * 0.044715 * (x * x)))


def _sigmoid(x):
    return 0.5 * jnp.tanh(0.5 * x) + 0.5


def _rstd(x):
    return lax.rsqrt(jnp.mean(x * x, axis=-1, keepdims=True) + EPS)


def _rms_bwd(x, dy, g):
    r = _rstd(x)
    xh = x * r
    dyg = dy if g is None else dy * g
    dx = r * (dyg - xh * jnp.mean(dyg * xh, axis=-1, keepdims=True))
    return dx, dy * xh


def _shift_down(blk, halo, k):
    tr = blk.shape[0]
    r = pltpu.roll(blk, k, 0)
    hr = pltpu.roll(halo, k, 0)
    first = jnp.where(lax.broadcasted_iota(jnp.int32, hr.shape, 0) < k, hr, r[0:8])
    return jnp.concatenate([first, r[8:tr]], axis=0)


def _rw(fn, rows, consts, outs, accs=(), *, tr, name, prev=(), nxt=(), widths=None):
    s = rows[0].shape[0]
    tr = min(tr, s)
    n = s // tr
    nr, nc, npv, nnx, no, na = len(rows), len(consts), len(prev), len(nxt), len(outs), len(accs)
    widths = widths or [None] * nr

    def body(*refs):
        ins, out_refs = refs[:nr + nc + npv + nnx], refs[nr + nc + npv + nnx:]
        i = pl.program_id(0)
        vals = [r[...] for r in ins[:nr + nc]]
        vals += [jnp.where(i == 0, 0.0, r[...]) for r in ins[nr + nc:nr + nc + npv]]
        vals += [jnp.where(i == n - 1, 0.0, r[...]) for r in ins[nr + nc + npv:]]
        res = fn(*vals)
        if not isinstance(res, (tuple, list)):
            res = (res,)
        for k in range(no):
            out_refs[k][...] = res[k].astype(out_refs[k].dtype)
        if na:
            @pl.when(i == 0)
            def _():
                for k in range(na):
                    out_refs[no + k][...] = jnp.zeros_like(out_refs[no + k])

            for k in range(na):
                out_refs[no + k][...] += res[no + k]

    in_specs = []
    for a, w in zip(rows, widths):
        if w is None:
            in_specs.append(pl.BlockSpec((tr, a.shape[1]), lambda i: (i, 0)))
        else:
            in_specs.append(pl.BlockSpec((tr, w[1]), functools.partial(lambda i, cb: (i, cb), cb=w[0])))
    for c in consts:
        in_specs.append(pl.BlockSpec(c.shape, functools.partial(lambda i, nd: (0,) * nd, nd=c.ndim)))
    t8 = tr // 8
    for a in prev:
        in_specs.append(pl.BlockSpec((8, a.shape[1]), lambda i: (jnp.maximum(i * t8 - 1, 0), 0)))
    for a in nxt:
        in_specs.append(pl.BlockSpec((8, a.shape[1]), lambda i: (jnp.minimum((i + 1) * t8, s // 8 - 1), 0)))
    out_shape = [jax.ShapeDtypeStruct((s, w), dt) for w, dt in outs]
    out_specs = [pl.BlockSpec((tr, w), lambda i: (i, 0)) for w, _ in outs]
    out_shape += [jax.ShapeDtypeStruct(a, F32) for a in accs]
    out_specs += [pl.BlockSpec(a, lambda i: (0, 0)) for a in accs]
    res = pl.pallas_call(
        body, grid=(n,), in_specs=in_specs, out_specs=out_specs, out_shape=out_shape, name=name,
        compiler_params=_params(("arbitrary",) if na else ("parallel",)),
    )(*rows, *consts, *prev, *nxt)
    return res


def _pick(n, cap):
    if n <= cap:
        return n
    best = None
    for d in range(128, cap + 1, 128):
        if n % d == 0:
            best = d
    assert best is not None, (n, cap)
    return best


def _mm(a, b, *, ta=False, tb=False, out_dtype=F32, add=None, epilogue=None, name, bm=1024, bn=512, bk=None):
    (k_dim, m) = a.shape if ta else a.shape[::-1]
    n = b.shape[0] if tb else b.shape[1]
    assert (b.shape[1] if tb else b.shape[0]) == k_dim
    if bk is None:
        bk = 1024 if ta else 2816
    bm, bn, bk = _pick(m, bm), _pick(n, bn), _pick(k_dim, bk)
    nk = k_dim // bk
    dims = (((0 if ta else 1,), (1 if tb else 0,)), ((), ()))
    assert add is None or epilogue is None
    epi_fn, extra, out_dtypes, vecs, n_sums = (tuple(epilogue or (None, [add] if add is not None else [], (out_dtype,)))
                                               + ((), 0))[:5]
    n_extra, n_vec, n_out = len(extra), len(vecs), len(out_dtypes)

    def body(*refs):
        a_ref, b_ref = refs[0], refs[1]
        extra_refs = refs[2:2 + n_extra + n_vec]
        out_refs = refs[2 + n_extra + n_vec:2 + n_extra + n_vec + n_out + n_sums]
        p = lax.dot_general(a_ref[...].astype(BF16), b_ref[...].astype(BF16), dims, preferred_element_type=F32)

        def finish(acc):
            if epi_fn is not None:
                res = epi_fn(acc, *[r[...] for r in extra_refs])
            else:
                res = (acc + extra_refs[0][...],) if n_extra else (acc,)
            for o_ref, r in zip(out_refs, res):
                o_ref[...] = r.reshape(o_ref.shape).astype(o_ref.dtype)

        if nk == 1:
            finish(p)
        else:
            acc_ref = refs[2 + n_extra + n_vec + n_out + n_sums]
            kk = pl.program_id(2)

            @pl.when(kk == 0)
            def _():
                acc_ref[...] = p

            @pl.when(kk > 0)
            def _():
                acc_ref[...] += p

            @pl.when(kk == nk - 1)
            def _():
                finish(acc_ref[...])

    a_spec = pl.BlockSpec((bk, bm), lambda i, j, k: (k, i)) if ta else pl.BlockSpec((bm, bk), lambda i, j, k: (i, k))
    b_spec = pl.BlockSpec((bn, bk), lambda i, j, k: (j, k)) if tb else pl.BlockSpec((bk, bn), lambda i, j, k: (k, j))
    tile = pl.BlockSpec((bm, bn), lambda i, j, k: (i, j))
    vec = pl.BlockSpec((1, bn), lambda i, j, k: (0, j))
    sums = pl.BlockSpec((1, 1, bn), lambda i, j, k: (i, 0, j))
    res = pl.pallas_call(
        body, grid=(m // bm, n // bn, nk), in_specs=[a_spec, b_spec] + [tile] * n_extra + [vec] * n_vec,
        out_specs=[tile] * n_out + [sums] * n_sums,
        out_shape=[jax.ShapeDtypeStruct((m, n), dt) for dt in out_dtypes]
        + [jax.ShapeDtypeStruct((m // bm, 1, n), F32)] * n_sums,
        scratch_shapes=[pltpu.VMEM((bm, bn), F32)] if nk > 1 else [],
        name=name, compiler_params=_params(("parallel", "parallel", "arbitrary")),
    )(a, b, *extra, *vecs)
    return res if epilogue is not None else res[0]


def _mm_rms_bwd(a, b, x, g, dx_add, *, tb=False, name, bf16_copy=False):
    n = x.shape[1]
    has_g = g is not None

    def fn(acc, xv, addv, *gv):
        dx, dg = _rms_bwd(xv, acc, gv[0] if has_g else None)
        dx = dx + addv
        return (dx,) + ((dx,) if bf16_copy else ()) + ((jnp.sum(dg, axis=0, keepdims=True),) if has_g else ())

    outs = (F32,) + ((BF16,) if bf16_copy else ())
    res = _mm(a, b, tb=tb, name=name, bm=512, bn=n, epilogue=(fn, [x, dx_add], outs, [g] if has_g else [], int(has_g)))
    if has_g:
        return tuple(res[:-1]) + (jnp.sum(res[-1], axis=0),)
    return tuple(res) + (None,)


def _rms_fwd(x, g, *, name):
    if g is None:
        return _rw(lambda xv: xv * _rstd(xv), [x], [], [(x.shape[1], BF16)], tr=512, name=name)[0]
    return _rw(lambda xv, gv: xv * _rstd(xv) * gv, [x], [g], [(x.shape[1], BF16)], tr=512, name=name)[0]


def _rms_bwd_call(x, dy, g, dx_add, *, name, tr=512):
    w = x.shape[1]

    def fn(xv, dyv, addv, gv):
        dx, dg = _rms_bwd(xv, dyv, gv)
        return dx + addv, jnp.sum(dg, axis=0, keepdims=True)

    return _rw(fn, [x, dy, dx_add], [g], [(w, F32)], [(1, w)], tr=tr, name=name)


def _conv_taps(hup, halo):
    return _shift_down(hup, halo, 1), _shift_down(hup, halo, 2)


def _ffn_fwd(x1, w, i):
    h2b = _rms_fwd(x1, w["norm_ffn"][i:i + 1], name="ffn_norm")
    hup = _mm(h2b, w["ffn_w_up"][i], tb=True, name="ffn_up", bn=1408)
    cw, cb = w["ffn_conv_w"][i], w["ffn_conv_b"][i:i + 1]

    def fn(hv, cwv, cbv, halo):
        h1, h2 = _conv_taps(hv, halo)
        hc = cbv + cwv[0:1] * h2
        hc = hc + cwv[1:2] * h1
        hc = hc + cwv[2:3] * hv
        g, up = hc[:, :D_FF], hc[:, D_FF:]
        return (g * _sigmoid(g)) * up

    a = _rw(fn, [hup], [cw, cb], [(D_FF, BF16)], tr=128, name="ffn_conv_gate", prev=[hup])[0]
    x2 = _mm(a, w["ffn_w_down"][i], add=x1, name="ffn_down")
    return x2, (h2b, hup, a)


def _ffn_bwd(dx2, dx2b, x1, saved, w, i):
    h2b, hup, a = saved
    cw, cb = w["ffn_conv_w"][i], w["ffn_conv_b"][i:i + 1]
    da = _mm(dx2b, w["ffn_w_down"][i], tb=True, name="ffn_down_dx", bn=1408)
    d_wdown = _mm(a, dx2b, ta=True, name="ffn_down_dw", bm=1408)

    def fn1(hv, dav, cwv, cbv, h_prev, h_next, da_next):
        tr = hv.shape[0]
        he = jnp.concatenate([hv, h_next], axis=0)
        dae = jnp.concatenate([dav, da_next], axis=0)
        h1, h2 = _conv_taps(he, h_prev)
        hc = cbv + cwv[0:1] * h2
        hc = hc + cwv[1:2] * h1
        hc = hc + cwv[2:3] * he
        g, up = hc[:, :D_FF], hc[:, D_FF:]
        sg = _sigmoid(g)
        d_up = dae * (g * sg)
        d_g = dae * up * (sg * (1.0 + g * (1.0 - sg)))
        d_hce = jnp.concatenate([d_g, d_up], axis=1)
        rows_e = tr + 8
        d_hup = (cwv[2:3] * d_hce[:tr] + cwv[1:2] * pltpu.roll(d_hce, rows_e - 1, 0)[:tr]
                 + cwv[0:1] * pltpu.roll(d_hce, rows_e - 2, 0)[:tr])
        d_hc = d_hce[:tr]
        col = lambda v: jnp.sum(v, axis=0, keepdims=True)
        return d_hup, col(d_hc), col(d_hc * h2[:tr]), col(d_hc * h1[:tr]), col(d_hc * hv)

    w2 = 2 * D_FF
    d_hup, d_cb, d_cw0, d_cw1, d_cw2 = _rw(fn1, [hup, da], [cw, cb], [(w2, BF16)], [(1, w2)] * 4, tr=128,
                                           name="ffn_conv_gate_bwd", prev=[hup], nxt=[hup, da])
    dx1, dx1b, d_norm = _mm_rms_bwd(d_hup, w["ffn_w_up"][i], x1, w["norm_ffn"][i:i + 1], dx2, name="ffn_up_dx",
                                    bf16_copy=True)
    d_wup_t = _mm(h2b, d_hup, ta=True, name="ffn_up_dw", bn=1408).T
    grads = {"ffn_w_up": d_wup_t, "ffn_w_down": d_wdown, "ffn_conv_b": d_cb[0],
             "ffn_conv_w": jnp.concatenate([d_cw0, d_cw1, d_cw2], axis=0), "norm_ffn": d_norm[0]}
    return dx1, dx1b, grads


def _ple_fwd(x2, p_i, w, i):
    rn = _rms_fwd(x2, None, name="ple_norm")
    gl = _mm(rn, w["ple_w_gate"][i], name="ple_gate")
    x3 = _mm(p_i, w["ple_w_proj"][i], tb=True, name="ple_proj",
             epilogue=(lambda pp, xv, g: (xv + _sigmoid(g) * pp,), [x2, gl], (F32,)))[0]
    return x3, (rn, gl)


def _ple_bwd(dx3, x2, p_i, saved, w, i):
    rn, gl = saved

    def fn(pp, dv, g):
        sg = _sigmoid(g)
        return dv * sg, dv * pp * (sg * (1.0 - sg))

    d_pp, d_pre = _mm(p_i, w["ple_w_proj"][i], tb=True, name="ple_proj_bwd", epilogue=(fn, [dx3, gl], (BF16, BF16)))
    d_wproj_t = _mm(p_i, d_pp, ta=True, name="ple_proj_dw", bn=1024).T
    d_wgate = _mm(rn, d_pre, ta=True, name="ple_gate_dw")
    dx2, dx2b, _ = _mm_rms_bwd(d_pre, w["ple_w_gate"][i], x2, None, dx3, tb=True, name="ple_gate_dx", bf16_copy=True)
    return dx2, dx2b, {"ple_w_proj": d_wproj_t, "ple_w_gate": d_wgate}


def _loss_and_grad(xf, tgt):
    def fn(xv, tv):
        d = xv - tv
        part = 0.5 * jnp.sum(jnp.mean(d * d, axis=-1, keepdims=True), axis=0, keepdims=True)
        return d * (1.0 / D_MODEL), part

    dx, loss = _rw(fn, [xf, tgt], [], [(D_MODEL, F32)], [(1, 1)], tr=512, name="loss_head")
    return loss[0, 0], dx


def _tril_mask():
    shp = (A_CHUNK, A_CHUNK)
    return lax.broadcasted_iota(jnp.int32, shp, 0) >= lax.broadcasted_iota(jnp.int32, shp, 1)


def _gmlp_fwd(z, w_s, b_s, v_gain, *, tr=512):
    s = z.shape[0]
    tr = min(tr, s)
    gw = A_WIDTH // A_GROUPS

    def body(z_ref, w_ref, b_ref, g_ref, y_ref):
        tril = _tril_mask()
        for g in range(A_GROUPS):
            wg = jnp.where(tril, w_ref[g], 0.0).astype(BF16)
            cols = slice(g * gw, (g + 1) * gw)
            vcols = slice(A_WIDTH + g * gw, A_WIDTH + (g + 1) * gw)
            for c in range(tr // A_CHUNK):
                rows = slice(c * A_CHUNK, (c + 1) * A_CHUNK)
                v = _gelu(z_ref[rows, vcols])
                vn = v * _rstd(v) * g_ref[:, cols]
                sv = jnp.dot(wg, vn.astype(BF16), preferred_element_type=F32) + b_ref[g]
                y_ref[rows, cols] = (_gelu(z_ref[rows, cols]) * sv).astype(BF16)

    return pl.pallas_call(
        body, grid=(s // tr,),
        in_specs=[pl.BlockSpec((tr, 2 * A_WIDTH), lambda i: (i, 0)),
                  pl.BlockSpec(w_s.shape, lambda i: (0, 0, 0)), pl.BlockSpec(b_s.shape, lambda i: (0, 0, 0)),
                  pl.BlockSpec(v_gain.shape, lambda i: (0, 0))],
        out_specs=pl.BlockSpec((tr, A_WIDTH), lambda i: (i, 0)),
        out_shape=jax.ShapeDtypeStruct((s, A_WIDTH + B_WIDTH), BF16), name="gmlp_fwd",
        compiler_params=_params(("parallel",)),
    )(z, w_s, b_s, v_gain)


def _gmlp_bwd(z, d_ymix, w_s, b_s, v_gain, dz, *, tr=512):
    s = z.shape[0]
    tr = min(tr, s)
    gw = A_WIDTH // A_GROUPS

    def body(z_ref, dy_ref, w_ref, b_ref, g_ref, _, dz_ref, dw_ref, db_ref, dg_ref):
        @pl.when(pl.program_id(0) == 0)
        def _():
            dw_ref[...] = jnp.zeros_like(dw_ref)
            db_ref[...] = jnp.zeros_like(db_ref)
            dg_ref[...] = jnp.zeros_like(dg_ref)

        tril = _tril_mask()
        for g in range(A_GROUPS):
            wg = jnp.where(tril, w_ref[g], 0.0).astype(BF16)
            cols = slice(g * gw, (g + 1) * gw)
            vcols = slice(A_WIDTH + g * gw, A_WIDTH + (g + 1) * gw)
            gain = g_ref[:, cols]
            for c in range(tr // A_CHUNK):
                rows = slice(c * A_CHUNK, (c + 1) * A_CHUNK)
                va, ua = z_ref[rows, vcols], z_ref[rows, cols]
                v = _gelu(va)
                r = _rstd(v)
                vh = v * r
                vnb = (vh * gain).astype(BF16)
                sv = jnp.dot(wg, vnb, preferred_element_type=F32) + b_ref[g]
                dy = dy_ref[rows, cols]
                d_sv = dy * _gelu(ua)
                dz_ref[rows, cols] = (dy * sv * _gelu_grad(ua)).astype(BF16)
                d_svb = d_sv.astype(BF16)
                d_vn = lax.dot_general(wg, d_svb, (((0,), (0,)), ((), ())), preferred_element_type=F32)
                dwp = lax.dot_general(d_svb, vnb, (((1,), (1,)), ((), ())), preferred_element_type=F32)
                dw_ref[g] += jnp.where(tril, dwp, 0.0)
                db_ref[g] += jnp.sum(d_sv, axis=1, keepdims=True)
                dg_ref[:, cols] += jnp.sum(d_vn * vh, axis=0, keepdims=True)
                d_vh = d_vn * gain
                d_v = r * (d_vh - vh * jnp.mean(d_vh * vh, axis=-1, keepdims=True))
                dz_ref[rows, vcols] = (d_v * _gelu_grad(va)).astype(BF16)

    return pl.pallas_call(
        body, grid=(s // tr,),
        in_specs=[pl.BlockSpec((tr, 2 * A_WIDTH), lambda i: (i, 0)), pl.BlockSpec((tr, A_WIDTH), lambda i: (i, 0)),
                  pl.BlockSpec(w_s.shape, lambda i: (0, 0, 0)), pl.BlockSpec(b_s.shape, lambda i: (0, 0, 0)),
                  pl.BlockSpec(v_gain.shape, lambda i: (0, 0)), _ANY],
        out_specs=[pl.BlockSpec((tr, 2 * A_WIDTH), lambda i: (i, 0)), pl.BlockSpec(w_s.shape, lambda i: (0, 0, 0)),
                   pl.BlockSpec(b_s.shape, lambda i: (0, 0, 0)), pl.BlockSpec(v_gain.shape, lambda i: (0, 0))],
        out_shape=[jax.ShapeDtypeStruct(dz.shape, dz.dtype), jax.ShapeDtypeStruct(w_s.shape, F32),
                   jax.ShapeDtypeStruct(b_s.shape, F32), jax.ShapeDtypeStruct(v_gain.shape, F32)],
        input_output_aliases={5: 0}, name="gmlp_bwd", compiler_params=_params(("arbitrary",)),
    )(z, d_ymix, w_s, b_s, v_gain, dz)


def _dot3(x, ub):
    x1 = x.astype(BF16)
    r1 = x - x1.astype(F32)
    x2 = r1.astype(BF16)
    x3 = (r1 - x2.astype(F32)).astype(BF16)
    d = lambda a: jnp.dot(a, ub, preferred_element_type=F32)
    return d(x1) + d(x2) + d(x3)


def _log_sigmoid(x):
    return jnp.minimum(x, 0.0) - jnp.log(1.0 + jnp.exp(-jnp.abs(x)))


def _fgate_fwd(f_t, b_col, *, tb=256):
    h, s = f_t.shape
    tb = min(tb, s)

    def body(f_ref, b_ref, c_ref, carry):
        @pl.when(pl.program_id(0) == 0)
        def _():
            carry[...] = jnp.zeros_like(carry)

        lf = _log_sigmoid(f_ref[...] + b_ref[...])
        shp = (tb, tb)
        upper = (lax.broadcasted_iota(jnp.int32, shp, 0) <= lax.broadcasted_iota(jnp.int32, shp, 1)).astype(BF16)
        c_ref[...] = _dot3(lf, upper) + carry[...]
        carry[...] += jnp.sum(lf, axis=1, keepdims=True)

    return pl.pallas_call(
        body, grid=(s // tb,),
        in_specs=[pl.BlockSpec((h, tb), lambda i: (0, i)), pl.BlockSpec((h, 1), lambda i: (0, 0))],
        out_specs=pl.BlockSpec((h, tb), lambda i: (0, i)), out_shape=jax.ShapeDtypeStruct((h, s), F32),
        scratch_shapes=[pltpu.VMEM((h, 1), F32)], name="fgate_fwd", compiler_params=_params(("arbitrary",)),
    )(f_t, b_col)


def _fgate_bwd(f_t, b_col, dc_a, dc_b, *, tb=256):
    h, s = f_t.shape
    tb = min(tb, s)
    n = s // tb

    def body(f_ref, b_ref, da_ref, db_ref, df_ref, dbias_ref, carry):
        @pl.when(pl.program_id(0) == 0)
        def _():
            carry[...] = jnp.zeros_like(carry)
            dbias_ref[...] = jnp.zeros_like(dbias_ref)

        dc = da_ref[...] + db_ref[...]
        shp = (tb, tb)
        lower = (lax.broadcasted_iota(jnp.int32, shp, 0) >= lax.broadcasted_iota(jnp.int32, shp, 1)).astype(BF16)
        d_lf = _dot3(dc, lower) + carry[...]
        carry[...] += jnp.sum(dc, axis=1, keepdims=True)
        df = d_lf * (1.0 - _sigmoid(f_ref[...] + b_ref[...]))
        df_ref[...] = df
        dbias_ref[...] += jnp.sum(df, axis=1, keepdims=True)

    blk = pl.BlockSpec((h, tb), lambda i: (0, n - 1 - i))
    return pl.pallas_call(
        body, grid=(n,), in_specs=[blk, pl.BlockSpec((h, 1), lambda i: (0, 0)), blk, blk],
        out_specs=[blk, pl.BlockSpec((h, 1), lambda i: (0, 0))],
        out_shape=[jax.ShapeDtypeStruct((h, s), F32), jax.ShapeDtypeStruct((h, 1), F32)],
        scratch_shapes=[pltpu.VMEM((h, 1), F32)], name="fgate_bwd", compiler_params=_params(("arbitrary",)),
    )(f_t, b_col, dc_a, dc_b)


_NT = (((1,), (1,)), ((), ()))


def _causal(shape, row0, col0, transposed):
    r = lax.broadcasted_iota(jnp.int32, shape, 0) + row0
    c = lax.broadcasted_iota(jnp.int32, shape, 1) + col0
    return (r <= c) if transposed else (c <= r)


_ATT_SCALE = B_HEAD_DIM ** -0.5
ATT_W = 128
_COL_CQ, _COL_CK, _COL_LSE, _COL_DELTA = 64, 67, 70, 64
_ATT_BLK = 2048
_ATT_SUB = 512


def _split3(x):
    h = x.astype(BF16).astype(F32)
    r = x - h
    m = r.astype(BF16).astype(F32)
    return h, m, (r - m).astype(BF16).astype(F32)


def _put_cols(base, lane, col0, parts):
    for t, pv in enumerate(parts):
        base = jnp.where(lane == col0 + t, pv, base)
    return base


_HEAD_PAIRS = B_HEADS // 2
_Q_BLOCK, _K_BLOCK, _V_BLOCK = 8, 12, 16
_F_BLOCK = 20
_O_BLOCK = 4
_ANY = pl.BlockSpec(memory_space=pl.ANY)


def _head_half(ref, h):
    x = ref[...]
    x = jnp.where((h & 1) == 1, pltpu.roll(x, B_HEAD_DIM, 1), x)
    return jnp.where(lax.broadcasted_iota(jnp.int32, x.shape, 1) < B_HEAD_DIM, x, 0.0)


def _pair(a, b):
    lane = lax.broadcasted_iota(jnp.int32, a.shape, 1)
    return jnp.where(lane < B_HEAD_DIM, a, pltpu.roll(b, B_HEAD_DIM, 1))


def _attn_prep(z, c_col, gq, gk, *, tr=1024):
    s = z.shape[0]
    tr = min(tr, s)
    n = s // tr

    def body(q_ref, k_ref, v_ref, c_ref, gq_ref, gk_ref, qo_ref, ko_ref, vo_ref):
        h = pl.program_id(0)
        q, k, v = _head_half(q_ref, h), _head_half(k_ref, h), _head_half(v_ref, h)
        lane = lax.broadcasted_iota(jnp.int32, q.shape, 1)
        rq = lax.rsqrt(jnp.sum(q * q, axis=-1, keepdims=True) * (1.0 / B_HEAD_DIM) + EPS)
        rk = lax.rsqrt(jnp.sum(k * k, axis=-1, keepdims=True) * (1.0 / B_HEAD_DIM) + EPS)
        c = c_ref[...]
        ch, cm, cl = _split3(c)
        one = jnp.ones_like(c)
        qq = _put_cols(q * rq * gq_ref[...] * _ATT_SCALE, lane, _COL_CQ, (ch, cm, cl))
        qq = _put_cols(qq, lane, _COL_CK, (one, one, one))
        kk = _put_cols(k * rk * gk_ref[...], lane, _COL_CQ, (one, one, one))
        kk = _put_cols(kk, lane, _COL_CK, (-ch, -cm, -cl))
        kk = _put_cols(kk, lane, _COL_LSE, (one, one, one))
        qo_ref[...] = qq.astype(BF16)
        ko_ref[...] = kk.astype(BF16)
        vo_ref[...] = _put_cols(v, lane, _COL_DELTA, (one, one, one)).astype(BF16)

    def zcol(base):
        return pl.BlockSpec((tr, ATT_W), lambda h, i: (i, base + (h >> 1)))

    rows = pl.BlockSpec((tr, ATT_W), lambda h, i: (h * n + i, 0))
    gain = pl.BlockSpec((1, ATT_W), lambda h, i: (0, 0))
    return pl.pallas_call(
        body, grid=(B_HEADS, n),
        in_specs=[zcol(_Q_BLOCK), zcol(_K_BLOCK), zcol(_V_BLOCK), pl.BlockSpec((tr, 1), lambda h, i: (h * n + i, 0)),
                  gain, gain],
        out_specs=[rows] * 3, out_shape=[jax.ShapeDtypeStruct((B_HEADS * s, ATT_W), BF16)] * 3, name="attn_prep",
        compiler_params=_params(("parallel", "parallel")),
    )(z, z, z, c_col, gq, gk)


def _attn_bwd_prep(d_ymix, o, qp, lse, *, tr=1024):
    s = d_ymix.shape[0]
    tr = min(tr, s)
    n = s // tr

    def body(do_ref, o_ref, q_ref, l_ref, dop_ref, qpp_ref):
        dov = _head_half(do_ref, pl.program_id(0))
        lane = lax.broadcasted_iota(jnp.int32, dov.shape, 1)
        dh, dm, dl = _split3(jnp.sum(dov * o_ref[...], axis=-1, keepdims=True))
        lh, lm, ll = _split3(l_ref[...])
        dop_ref[...] = _put_cols(dov, lane, _COL_DELTA, (-dh, -dm, -dl)).astype(BF16)
        qpp_ref[...] = _put_cols(q_ref[...].astype(F32), lane, _COL_LSE, (-lh, -lm, -ll)).astype(BF16)

    rows = pl.BlockSpec((tr, ATT_W), lambda h, i: (h * n + i, 0))
    return pl.pallas_call(
        body, grid=(B_HEADS, n),
        in_specs=[pl.BlockSpec((tr, ATT_W), lambda h, i: (i, _O_BLOCK + (h >> 1))), rows, rows,
                  pl.BlockSpec((tr, 1), lambda h, i: (h * n + i, 0))],
        out_specs=[rows] * 2, out_shape=[jax.ShapeDtypeStruct((B_HEADS * s, ATT_W), BF16)] * 2,
        name="attn_bwd_prep", compiler_params=_params(("parallel", "parallel")),
    )(d_ymix, o, qp, lse)


def _attn_merge_out(o, ymix, *, tr=1024):
    s = o.shape[1]
    tr = min(tr, s)

    def body(a_ref, b_ref, buf_ref, out_ref):
        out_ref[...] = _pair(a_ref[0], b_ref[0]).astype(BF16)

    return pl.pallas_call(
        body, grid=(_HEAD_PAIRS, s // tr),
        in_specs=[pl.BlockSpec((1, tr, ATT_W), lambda j, i: (2 * j, i, 0)),
                  pl.BlockSpec((1, tr, ATT_W), lambda j, i: (2 * j + 1, i, 0)), _ANY],
        out_specs=pl.BlockSpec((tr, ATT_W), lambda j, i: (i, _O_BLOCK + j)),
        out_shape=jax.ShapeDtypeStruct(ymix.shape, ymix.dtype), input_output_aliases={2: 0}, name="attn_merge_out",
        compiler_params=_params(("parallel", "parallel")),
    )(o, o, ymix)


def _pair_bwd(d_heads, dz, out_block, *, z=None, in_block=None, gain=None, scale=None, name, tr=1024):
    s = d_heads.shape[1]
    tr = min(tr, s)
    norm = gain is not None

    def body(*refs):
        a_ref, b_ref = refs[0], refs[1]
        d = _pair(a_ref[0], b_ref[0])
        if norm:
            x_ref, g_ref, _, out_ref, dg_ref = refs[2:]

            @pl.when((pl.program_id(0) == 0) & (pl.program_id(1) == 0))
            def _():
                dg_ref[...] = jnp.zeros_like(dg_ref)

            x = x_ref[...]
            lo = lax.broadcasted_iota(jnp.int32, x.shape, 1) < B_HEAD_DIM

            def half_mean(v):
                s_lo = jnp.sum(jnp.where(lo, v, 0.0), axis=-1, keepdims=True)
                s_hi = jnp.sum(jnp.where(lo, 0.0, v), axis=-1, keepdims=True)
                return jnp.where(lo, s_lo, s_hi) * (1.0 / B_HEAD_DIM)

            r = lax.rsqrt(half_mean(x * x) + EPS)
            xh = x * r
            dy = d * scale if scale is not None else d
            dyg = dy * g_ref[...]
            d = r * (dyg - xh * half_mean(dyg * xh))
            dg_ref[...] += jnp.sum(dy * xh, axis=0, keepdims=True)
        else:
            out_ref = refs[3]
        out_ref[...] = d.astype(BF16)

    heads = [pl.BlockSpec((1, tr, ATT_W), lambda j, i: (2 * j, i, 0)),
             pl.BlockSpec((1, tr, ATT_W), lambda j, i: (2 * j + 1, i, 0))]
    out_spec = pl.BlockSpec((tr, ATT_W), lambda j, i: (i, out_block + j))
    dz_shape = jax.ShapeDtypeStruct(dz.shape, dz.dtype)
    if norm:
        res = pl.pallas_call(
            body, grid=(_HEAD_PAIRS, s // tr),
            in_specs=heads + [pl.BlockSpec((tr, ATT_W), lambda j, i: (i, in_block + j)),
                              pl.BlockSpec((1, ATT_W), lambda j, i: (0, 0)), _ANY],
            out_specs=[out_spec, pl.BlockSpec((1, ATT_W), lambda j, i: (0, 0))],
            out_shape=[dz_shape, jax.ShapeDtypeStruct((1, ATT_W), F32)], input_output_aliases={4: 0}, name=name,
            compiler_params=_params(("arbitrary", "arbitrary")),
        )(d_heads, d_heads, z, gain, dz)
        return res[0], res[1]
    return pl.pallas_call(
        body, grid=(_HEAD_PAIRS, s // tr), in_specs=heads + [_ANY], out_specs=out_spec, out_shape=dz_shape,
        input_output_aliases={2: 0}, name=name, compiler_params=_params(("parallel", "parallel")),
    )(d_heads, d_heads, dz), None


def _attn_fwd(qp, kp, vp, *, blk=_ATT_BLK):
    h, s, d = qp.shape
    b = min(blk, s)
    n = s // b
    sub = min(_ATT_SUB, b)
    ns = b // sub

    def body(q_ref, k_ref, v_ref, o_ref, lse_ref):
        i = pl.program_id(1)
        qs = [q_ref[0, t * sub:(t + 1) * sub, :] for t in range(ns)]

        def tile(j, state, masked):
            off = pl.multiple_of(j * b, b)
            kk = k_ref[0, pl.ds(off, b), :]
            vv = v_ref[0, pl.ds(off, b), :]
            new = []
            for t in range(ns):
                m, l, acc = state[3 * t:3 * t + 3]
                nk = (t + 1) * sub if masked else b
                sc = lax.dot_general(qs[t], kk[:nk], _NT, preferred_element_type=F32)
                if masked:
                    sc = jnp.where(_causal((sub, nk), t * sub, 0, False), sc, NEG_INF)
                m_new = jnp.maximum(m, jnp.max(sc, axis=1, keepdims=True))
                alpha = jnp.exp(m - m_new)
                p = jnp.exp(sc - m_new)
                new += [m_new, alpha * l + jnp.sum(p, axis=1, keepdims=True),
                        alpha * acc + jnp.dot(p.astype(BF16), vv[:nk], preferred_element_type=F32)]
            return tuple(new)

        init = (jnp.full((sub, 1), NEG_INF, F32), jnp.zeros((sub, 1), F32), jnp.zeros((sub, d), F32)) * ns
        state = lax.fori_loop(0, i, lambda j, st: tile(j, st, False), init)
        state = tile(i, state, True)
        for t in range(ns):
            m, l, acc = state[3 * t:3 * t + 3]
            o_ref[0, t * sub:(t + 1) * sub, :] = acc / l
            lse_ref[0, t * sub:(t + 1) * sub, :] = m + jnp.log(l)

    blk_spec = pl.BlockSpec((1, b, d), lambda hh, i: (hh, i, 0))
    full = pl.BlockSpec((1, s, d), lambda hh, i: (hh, 0, 0))
    return pl.pallas_call(
        body, grid=(h, n), in_specs=[blk_spec, full, full],
        out_specs=[blk_spec, pl.BlockSpec((1, b, 1), lambda hh, i: (hh, i, 0))],
        out_shape=[jax.ShapeDtypeStruct((h, s, d), F32), jax.ShapeDtypeStruct((h, s, 1), F32)],
        name="attn_fwd", compiler_params=_params(("parallel", "arbitrary")),
    )(qp, kp, vp)


_ATT_BWD_BLK = 1024


def _attn_bwd(qpp, kp, vp, dop, *, blk=_ATT_BWD_BLK):
    h, s, d = qpp.shape
    b = min(blk, s)
    n = s // b
    sub = min(_ATT_SUB, b)
    ns = b // sub

    def body(k_ref, v_ref, q_ref, do_ref, dq_ref, dk_ref, dv_ref):
        j = pl.program_id(1)

        @pl.when(j == 0)
        def _():
            dq_ref[...] = jnp.zeros_like(dq_ref)

        ks = [k_ref[0, t * sub:(t + 1) * sub, :] for t in range(ns)]
        vs = [v_ref[0, t * sub:(t + 1) * sub, :] for t in range(ns)]

        def tile(i, state, masked):
            off = pl.multiple_of(i * b, b)
            qq = q_ref[0, pl.ds(off, b), :]
            dd = do_ref[0, pl.ds(off, b), :]
            new = []
            dq = None
            for t in range(ns):
                q0 = t * sub if masked else 0
                qt, dt = qq[q0:], dd[q0:]
                pt = jnp.exp(lax.dot_general(ks[t], qt, _NT, preferred_element_type=F32))
                if masked:
                    pt = jnp.where(_causal((sub, b - q0), q0, q0, True), pt, 0.0)
                dst = (pt * lax.dot_general(vs[t], dt, _NT, preferred_element_type=F32)).astype(BF16)
                new += [state[2 * t] + jnp.dot(dst, qt, preferred_element_type=F32),
                        state[2 * t + 1] + jnp.dot(pt.astype(BF16), dt, preferred_element_type=F32)]
                part = lax.dot_general(dst, ks[t], _TN, preferred_element_type=F32)
                if q0:
                    part = jnp.concatenate([jnp.zeros((q0, d), F32), part], axis=0)
                dq = part if dq is None else dq + part
            dq_ref[0, pl.ds(off, b), :] += dq
            return tuple(new)

        state = tile(j, (jnp.zeros((sub, d), F32),) * (2 * ns), True)
        state = lax.fori_loop(j + 1, n, lambda i, st: tile(i, st, False), state)
        for t in range(ns):
            dk_ref[0, t * sub:(t + 1) * sub, :] = state[2 * t]
            dv_ref[0, t * sub:(t + 1) * sub, :] = state[2 * t + 1]

    blk_spec = pl.BlockSpec((1, b, d), lambda hh, j: (hh, j, 0))
    full = pl.BlockSpec((1, s, d), lambda hh, j: (hh, 0, 0))
    return pl.pallas_call(
        body, grid=(h, n), in_specs=[blk_spec, blk_spec, full, full], out_specs=[full, blk_spec, blk_spec],
        out_shape=[jax.ShapeDtypeStruct((h, s, d), F32)] * 3, name="attn_bwd",
        compiler_params=_params(("parallel", "arbitrary")),
    )(kp, vp, qpp, dop)


def _pad_head(g):
    return jnp.pad(g, ((0, 0), (0, ATT_W - B_HEAD_DIM)))


def _even_fwd(x0, w, e, i):
    s = x0.shape[0]
    hs = B_HEADS * s
    hb = _rms_fwd(x0, w["norm_mix"][i:i + 1], name="mix_norm")
    z = _mm(hb, w["ev_w_in"][e], tb=True, name="ev_in", bn=896)
    b_s = w["ev_b_spatial"][e][:, :, None]
    v_gain = w["ev_v_norm"][e:e + 1]
    ymix = _gmlp_fwd(z, w["ev_w_spatial"][e], b_s, v_gain)
    f_t = z[:, IN_COLS - B_HEADS:IN_COLS].T
    c = _fgate_fwd(f_t, w["ev_b_fgate"][e][:, None])
    qp, kp, vp = _attn_prep(z, c.reshape(hs, 1), _pad_head(w["ev_q_norm"][e:e + 1]),
                            _pad_head(w["ev_k_norm"][e:e + 1]))
    shp = (B_HEADS, s, ATT_W)
    o, lse = _attn_fwd(qp.reshape(shp), kp.reshape(shp), vp.reshape(shp))
    ymix = _attn_merge_out(o, ymix)
    x1 = _mm(ymix, w["ev_w_out"][e], add=x0, name="ev_out")
    return x1, (hb, z, ymix, qp, kp, vp, f_t, o, lse)


def _even_bwd(dx1, dx1b, x0, saved, w, e, i):
    hb, z, ymix, qp, kp, vp, f_t, o, lse = saved
    s = x0.shape[0]
    hs = B_HEADS * s
    d_ymix = _mm(dx1b, w["ev_w_out"][e], tb=True, name="ev_out_dx")
    d_wout = _mm(ymix, dx1b, ta=True, name="ev_out_dw")
    dop, qpp = _attn_bwd_prep(d_ymix, o.reshape(hs, ATT_W), qp, lse.reshape(hs, 1))
    shp = (B_HEADS, s, ATT_W)
    qpp, dop, kp3, vp3 = qpp.reshape(shp), dop.reshape(shp), kp.reshape(shp), vp.reshape(shp)
    dqp, dkp, dvp = _attn_bwd(qpp, kp3, vp3, dop)
    d_ft, d_bf = _fgate_bwd(f_t, w["ev_b_fgate"][e][:, None], dqp[:, :, _COL_CQ], -dkp[:, :, _COL_CK])
    dz = jnp.pad(d_ft.T.astype(BF16), ((0, 0), (_F_BLOCK * ATT_W, IN_COLS_PAD - IN_COLS)))
    b_s = w["ev_b_spatial"][e][:, :, None]
    v_gain = w["ev_v_norm"][e:e + 1]
    dz, d_ws, d_bs, d_vg = _gmlp_bwd(z, d_ymix, w["ev_w_spatial"][e], b_s, v_gain, dz)
    twice = lambda g: jnp.concatenate([g, g], axis=1)
    dz, d_gq = _pair_bwd(dqp, dz, _Q_BLOCK, z=z, in_block=_Q_BLOCK, gain=twice(w["ev_q_norm"][e:e + 1]),
                         scale=_ATT_SCALE, name="attn_qnorm_bwd")
    dz, d_gk = _pair_bwd(dkp, dz, _K_BLOCK, z=z, in_block=_K_BLOCK, gain=twice(w["ev_k_norm"][e:e + 1]),
                         name="attn_knorm_bwd")
    dz, _ = _pair_bwd(dvp, dz, _V_BLOCK, name="attn_dv_out")
    dx0, d_nm = _mm_rms_bwd(dz, w["ev_w_in"][e], x0, w["norm_mix"][i:i + 1], dx1, name="ev_in_dx")
    d_win_t = _mm(hb, dz, ta=True, name="ev_in_dw", bn=1344)[:, :IN_COLS].T
    hd = B_HEAD_DIM
    grads = {"ev_w_in": d_win_t, "ev_w_out": d_wout, "ev_b_fgate": d_bf[:, 0],
             "ev_q_norm": d_gq[0, :hd] + d_gq[0, hd:], "ev_k_norm": d_gk[0, :hd] + d_gk[0, hd:], "ev_v_norm": d_vg[0],
             "ev_w_spatial": d_ws, "ev_b_spatial": d_bs[:, :, 0], "norm_mix": d_nm[0]}
    return dx0, grads


def _s5_disc(a_re, a_im, log_dt, b_re, b_im):
    dt = jnp.exp(log_dt)[:, None]
    lr, li = a_re, a_im
    mag = jnp.exp(lr * dt)
    ab_re, ab_im = mag * jnp.cos(li * dt), mag * jnp.sin(li * dt)
    den = lr * lr + li * li
    nr, ni = ab_re - 1.0, ab_im
    cr = (nr * lr + ni * li) / den
    ci = (ni * lr - nr * li) / den
    bb_re = cr[..., None] * b_re - ci[..., None] * b_im
    bb_im = cr[..., None] * b_im + ci[..., None] * b_re
    return ab_re, ab_im, bb_re, bb_im


_GPB = S5_GROUPS // S5_BLOCKS


def _blockdiag(t):
    a, b = t.shape[1:]
    eye = jnp.eye(_GPB, dtype=t.dtype)
    t = t.reshape(S5_BLOCKS, _GPB, a, 1, b) * eye[None, :, None, :, None]
    return t.reshape(S5_BLOCKS, _GPB * a, _GPB * b)


def _blockdiag_extract(m, a, b):
    eye = jnp.eye(_GPB, dtype=m.dtype)
    m = m.reshape(S5_BLOCKS, _GPB, a, _GPB, b) * eye[None, :, None, :, None]
    return jnp.sum(m, axis=3).reshape(S5_GROUPS, a, b)


_SCAN_ROWS = 8


def _cmul(a_r, a_i, b_r, b_i):
    return a_r * b_r - a_i * b_i, a_r * b_i + a_i * b_r


def _scan_tiles(yr_ref, yi_ref, a_r, a_i, c_r, c_i, *, rev, x_refs=None):
    tb, wl = yr_ref.shape
    ntile = tb // _SCAN_ROWS
    with_acc = x_refs is not None
    if rev:
        a_i = -a_i
    pw = [(a_r, a_i)]
    for _ in range(_SCAN_ROWS - 1):
        pw.append(_cmul(*pw[-1], a_r, a_i))
    sub = lax.broadcasted_iota(jnp.int32, (_SCAN_ROWS, wl), 0)
    dist = (_SCAN_ROWS - 1 - sub) if rev else sub
    zero = jnp.zeros((_SCAN_ROWS, wl), F32)
    steps = []
    for kk in (1, 2, 4):
        steps.append(((_SCAN_ROWS - kk) if rev else kk, jnp.where(dist >= kk, pw[kk - 1][0], zero),
                      jnp.where(dist >= kk, pw[kk - 1][1], zero)))
    e_r, e_i = zero, zero
    for d in range(_SCAN_ROWS):
        e_r = jnp.where(dist == d, pw[d][0], e_r)
        e_i = jnp.where(dist == d, pw[d][1], e_i)
    exit_row = 0 if rev else _SCAN_ROWS - 1

    def tile(q, carry):
        idx = (ntile - 1 - q) if rev else q
        rows = pl.ds(pl.multiple_of(idx * _SCAN_ROWS, _SCAN_ROWS), _SCAN_ROWS)
        y_r, y_i = yr_ref[rows, :], yi_ref[rows, :]
        for sh, k_r, k_i in steps:
            t_r, t_i = _cmul(k_r, k_i, pltpu.roll(y_r, sh, 0), pltpu.roll(y_i, sh, 0))
            y_r, y_i = y_r + t_r, y_i + t_i
        cb_r = jnp.broadcast_to(carry[0], (_SCAN_ROWS, wl))
        cb_i = jnp.broadcast_to(carry[1], (_SCAN_ROWS, wl))
        t_r, t_i = _cmul(e_r, e_i, cb_r, cb_i)
        y_r, y_i = y_r + t_r, y_i + t_i
        yr_ref[rows, :] = y_r
        yi_ref[rows, :] = y_i
        out = (y_r[exit_row:exit_row + 1], y_i[exit_row:exit_row + 1])
        if with_acc:
            n_r = jnp.where(sub == _SCAN_ROWS - 1, cb_r, pltpu.roll(y_r, _SCAN_ROWS - 1, 0))
            n_i = jnp.where(sub == _SCAN_ROWS - 1, cb_i, pltpu.roll(y_i, _SCAN_ROWS - 1, 0))
            s_r, s_i = x_refs[0][rows, :], x_refs[1][rows, :]
            out += (carry[2] + (n_r * s_r + n_i * s_i), carry[3] + (n_i * s_r - n_r * s_i))
        return out

    return lax.fori_loop(0, ntile, tile, (c_r, c_i) + ((zero, zero) if with_acc else ()), unroll=2)


_S5_WL = S5_N // S5_BLOCKS
_S5_CW = D_MODEL // S5_BLOCKS
_TN = (((0,), (0,)), ((), ()))


def _s5_fwd(hb, mats, *, tb=1024):
    s = hb.shape[0]
    tb = min(tb, s)

    def body(h_ref, wbr_ref, wbi_ref, wcr_ref, wci_ref, ar_ref, ai_ref, xr_ref, xi_ref, cp_ref, cr, ci):
        @pl.when(pl.program_id(1) == 0)
        def _():
            cr[...] = jnp.zeros_like(cr)
            ci[...] = jnp.zeros_like(ci)

        hv = h_ref[...]
        xr_ref[...] = jnp.dot(hv, wbr_ref[0], preferred_element_type=F32)
        xi_ref[...] = jnp.dot(hv, wbi_ref[0], preferred_element_type=F32)
        cr[...], ci[...] = _scan_tiles(xr_ref, xi_ref, ar_ref[...], ai_ref[...], cr[...], ci[...], rev=False)
        cp_ref[...] = (jnp.dot(xr_ref[...].astype(BF16), wcr_ref[0], preferred_element_type=F32)
                       + jnp.dot(xi_ref[...].astype(BF16), wci_ref[0], preferred_element_type=F32))

    wide = pl.BlockSpec((tb, _S5_WL), lambda c, t: (t, c))
    narrow = pl.BlockSpec((tb, _S5_CW), lambda c, t: (t, c))
    w_in = pl.BlockSpec((1, _S5_CW, _S5_WL), lambda c, t: (c, 0, 0))
    w_out = pl.BlockSpec((1, _S5_WL, _S5_CW), lambda c, t: (c, 0, 0))
    vec = pl.BlockSpec((1, _S5_WL), lambda c, t: (0, c))
    return pl.pallas_call(
        body, grid=(S5_BLOCKS, s // tb), in_specs=[narrow, w_in, w_in, w_out, w_out, vec, vec],
        out_specs=[wide, wide, narrow],
        out_shape=[jax.ShapeDtypeStruct((s, S5_N), F32)] * 2 + [jax.ShapeDtypeStruct((s, D_MODEL), F32)],
        scratch_shapes=[pltpu.VMEM((1, _S5_WL), F32)] * 2, name="s5_fwd",
        compiler_params=_params(("parallel", "arbitrary")),
    )(hb, mats["wb_re"], mats["wb_im"], mats["wc_re"], mats["wc_imn"], mats["ar"], mats["ai"])


def _s5_bwd(dyb, hb, xr, xi, dhd, mats, *, tb=1024):
    s = hb.shape[0]
    tb = min(tb, s)
    nt = s // tb

    def body(dy_ref, h_ref, xr_ref, xi_ref, dhd_ref, wctr_ref, wcti_ref, wbtr_ref, wbti_ref, ar_ref, ai_ref,
             dh_ref, dar_ref, dai_ref, dwbr_ref, dwbi_ref, dwcr_ref, dwci_ref, lr, li, cr, ci):
        @pl.when(pl.program_id(1) == 0)
        def _():
            for r in (cr, ci, dar_ref, dai_ref, dwbr_ref, dwbi_ref, dwcr_ref, dwci_ref):
                r[...] = jnp.zeros_like(r)

        dyv, hv = dy_ref[...], h_ref[...]
        lr[...] = jnp.dot(dyv, wctr_ref[0], preferred_element_type=F32)
        li[...] = jnp.dot(dyv, wcti_ref[0], preferred_element_type=F32)
        cr[...], ci[...], acc_r, acc_i = _scan_tiles(lr, li, ar_ref[...], ai_ref[...], cr[...], ci[...], rev=True,
                                                     x_refs=(xr_ref, xi_ref))
        dar_ref[...] += jnp.sum(acc_r, axis=0, keepdims=True)
        dai_ref[...] += jnp.sum(acc_i, axis=0, keepdims=True)
        lrb, lib = lr[...].astype(BF16), li[...].astype(BF16)
        dh_ref[...] = (jnp.dot(lrb, wbtr_ref[0], preferred_element_type=F32) + dhd_ref[...]
                       + jnp.dot(lib, wbti_ref[0], preferred_element_type=F32))
        dwbr_ref[0] += lax.dot_general(hv, lrb, _TN, preferred_element_type=F32)
        dwbi_ref[0] += lax.dot_general(hv, lib, _TN, preferred_element_type=F32)
        dwcr_ref[0] += lax.dot_general(dyv, xr_ref[...].astype(BF16), _TN, preferred_element_type=F32)
        dwci_ref[0] += lax.dot_general(dyv, xi_ref[...].astype(BF16), _TN, preferred_element_type=F32)

    wide = pl.BlockSpec((tb, _S5_WL), lambda c, t: (nt - 1 - t, c))
    narrow = pl.BlockSpec((tb, _S5_CW), lambda c, t: (nt - 1 - t, c))
    w_in = pl.BlockSpec((1, _S5_CW, _S5_WL), lambda c, t: (c, 0, 0))
    w_out = pl.BlockSpec((1, _S5_WL, _S5_CW), lambda c, t: (c, 0, 0))
    vec = pl.BlockSpec((1, _S5_WL), lambda c, t: (0, c))
    acc_shape = jax.ShapeDtypeStruct((S5_BLOCKS, _S5_CW, _S5_WL), F32)
    return pl.pallas_call(
        body, grid=(S5_BLOCKS, nt), in_specs=[narrow, narrow, wide, wide, narrow, w_in, w_in, w_out, w_out, vec, vec],
        out_specs=[narrow, vec, vec, w_in, w_in, w_in, w_in],
        out_shape=[jax.ShapeDtypeStruct((s, D_MODEL), F32)] + [jax.ShapeDtypeStruct((1, S5_N), F32)] * 2
        + [acc_shape] * 4,
        scratch_shapes=[pltpu.VMEM((tb, _S5_WL), F32)] * 2 + [pltpu.VMEM((1, _S5_WL), F32)] * 2, name="s5_bwd",
        compiler_params=_params(("parallel", "arbitrary")),
    )(dyb, hb, xr, xi, dhd, mats["wct_re"], mats["wct_imn"], mats["wbt_re"], mats["wbt_im"], mats["ar"], mats["ai"])


def _s5_mats(w, o):
    ab_re, ab_im, bb_re, bb_im = _s5_disc(w["od_a_re"][o], w["od_a_im"][o], w["od_log_dt"][o], w["od_b_re"][o],
                                          w["od_b_im"][o])
    c_re, c_im = w["od_c_re"][o], w["od_c_im"][o]
    tr = lambda t: t.transpose(0, 2, 1)
    bd = lambda t: _blockdiag(t).astype(BF16)
    return {
        "ar": ab_re.reshape(1, S5_N), "ai": ab_im.reshape(1, S5_N),
        "wb_re": bd(tr(bb_re)), "wb_im": bd(tr(bb_im)), "wc_re": bd(tr(c_re)), "wc_imn": bd(-tr(c_im)),
        "wct_re": bd(c_re), "wct_imn": bd(-c_im), "wbt_re": bd(bb_re), "wbt_im": bd(bb_im),
    }


def _odd_fwd(x0, w, o, i):
    mats = _s5_mats(w, o)
    nm = w["norm_mix"][i:i + 1]
    d_row = w["od_d"][o:o + 1]
    hb = _rms_fwd(x0, nm, name="mix_norm")
    xr, xi, cp = _s5_fwd(hb, mats)

    def fn(xv, cpv, gv, dv):
        y = cpv + dv * (xv * _rstd(xv) * gv)
        return y, _gelu(y)

    y, gy = _rw(fn, [x0, cp], [nm, d_row], [(D_MODEL, F32), (D_MODEL, BF16)], tr=512, name="s5_out")
    gg = _mm(gy, w["od_w_glu"][o], tb=True, name="od_glu")
    x1 = _rw(lambda xv, g: xv + g[:, :D_MODEL] * _sigmoid(g[:, D_MODEL:]), [x0, gg], [], [(D_MODEL, F32)], tr=512,
             name="od_glu_out")[0]
    return x1, (hb, xr, xi, y, gy, gg)


def _odd_bwd(dx1, x0, saved, w, o, i):
    hb, xr, xi, y, gy, gg = saved
    mats = _s5_mats(w, o)
    nm = w["norm_mix"][i:i + 1]
    d_row = w["od_d"][o:o + 1]

    def fn_glu(dv, g):
        ga, gb = g[:, :D_MODEL], g[:, D_MODEL:]
        sg = _sigmoid(gb)
        return jnp.concatenate([dv * sg, dv * ga * (sg * (1.0 - sg))], axis=1)

    dgg = _rw(fn_glu, [dx1, gg], [], [(2 * D_MODEL, BF16)], tr=512, name="od_glu_out_bwd")[0]
    d_wglu = _mm(gy, dgg, ta=True, name="od_glu_dw", bn=1024).T
    d_gy = _mm(dgg, w["od_w_glu"][o], name="od_glu_dx")

    def fn_y(dg, yv, xv, gv, dv):
        dy = dg * _gelu_grad(yv)
        h = xv * _rstd(xv) * gv
        return dy, dv * dy, jnp.sum(dy * h, axis=0, keepdims=True)

    dyb, dhd, d_d = _rw(fn_y, [d_gy, y, x0], [nm, d_row], [(D_MODEL, BF16), (D_MODEL, F32)], [(1, D_MODEL)], tr=512,
                        name="s5_out_bwd")
    dh, d_ar, d_ai, d_wb_re, d_wb_im, d_wc_re, d_wc_im = _s5_bwd(dyb, hb, xr, xi, dhd, mats)
    gc, gp = S5_GROUP_CH, S5_STATE
    d_c_re = _blockdiag_extract(d_wc_re, gc, gp)
    d_c_im = -_blockdiag_extract(d_wc_im, gc, gp)
    d_bb_re = _blockdiag_extract(d_wb_re, gc, gp).transpose(0, 2, 1)
    d_bb_im = _blockdiag_extract(d_wb_im, gc, gp).transpose(0, 2, 1)
    dx0, d_nm = _rms_bwd_call(x0, dh, nm, dx1, name="mix_norm_bwd")
    _, vjp = jax.vjp(_s5_disc, w["od_a_re"][o], w["od_a_im"][o], w["od_log_dt"][o], w["od_b_re"][o], w["od_b_im"][o])
    d_a_re, d_a_im, d_log_dt, d_b_re, d_b_im = vjp(
        (d_ar.reshape(S5_GROUPS, S5_STATE), d_ai.reshape(S5_GROUPS, S5_STATE), d_bb_re, d_bb_im))
    grads = {"od_a_re": d_a_re, "od_a_im": d_a_im, "od_log_dt": d_log_dt, "od_b_re": d_b_re, "od_b_im": d_b_im,
             "od_c_re": d_c_re, "od_c_im": d_c_im, "od_d": d_d[0], "od_w_glu": d_wglu, "norm_mix": d_nm[0]}
    return dx0, grads


_HBM = pl.BlockSpec(memory_space=pltpu.HBM)


def _mesh_pos():
    return lax.axis_index("x"), lax.axis_index("y"), lax.axis_index("c")


def _slot(px, py, pc):
    return 4 * px + 2 * py + pc


def _all_gather(x, *, name):
    def body(x_ref, out_ref, send_sems, recv_sems, local_sem):
        mx, my, mc = _mesh_pos()
        me, sibling = (mx, my, mc), (mx, my, 1 - mc)
        chips = [(1 - mx, my), (mx, 1 - my), (1 - mx, 1 - my)]

        def copy(k, block, to, src=None):
            dst = out_ref.at[_slot(*block)]
            return pltpu.make_async_remote_copy(
                src_ref=dst if src is None else src, dst_ref=dst, send_sem=send_sems.at[k], recv_sem=recv_sems.at[k],
                device_id=to, device_id_type=MESH)

        mine = pltpu.make_async_copy(x_ref, out_ref.at[_slot(*me)], local_sem)
        mine.start()
        first = [copy(0, me, sibling, src=x_ref)]
        first += [copy(1 + j, me, (*chip, mc), src=x_ref) for j, chip in enumerate(chips)]
        for cp in first:
            cp.start()
        passed = [copy(4 + j, (*chip, mc), sibling) for j, chip in enumerate(chips)]
        for j, chip in enumerate(chips):
            copy(1 + j, (*chip, mc), me).wait_recv()
            passed[j].start()
        copy(0, sibling, me).wait_recv()
        for j, chip in enumerate(chips):
            copy(4 + j, (*chip, 1 - mc), me).wait_recv()
        for cp in first + passed:
            cp.wait_send()
        mine.wait()

    return pl.pallas_call(
        body, out_shape=jax.ShapeDtypeStruct((N_DEV,) + x.shape, x.dtype), in_specs=[_HBM], out_specs=_HBM,
        scratch_shapes=[pltpu.SemaphoreType.DMA((7,)), pltpu.SemaphoreType.DMA((7,)), pltpu.SemaphoreType.DMA],
        name=name,
    )(x)


_SEMS = pltpu.SemaphoreType.DMA((N_DEV - 1,))
_SEM_SPEC = pl.BlockSpec(memory_space=pltpu.SEMAPHORE)
_SPLIT_COPY = pltpu.CompilerParams(has_side_effects=pltpu.SideEffectType.DATAFLOW_SIDE_EFFECTING)


def _exchange_copies(x_refs, land_refs, send_sems, recv_sems, gather):
    mx, my, mc = _mesh_pos()
    my_slot = _slot(mx, my, mc)
    copies = []
    for a, (x_ref, land_ref) in enumerate(zip(x_refs, land_refs)):
        for k in range(1, N_DEV):
            peer = (1 - mx if k & 4 else mx, 1 - my if k & 2 else my, 1 - mc if k & 1 else mc)
            copies.append(pltpu.make_async_remote_copy(
                src_ref=x_ref if gather else x_ref.at[_slot(*peer)], dst_ref=land_ref.at[my_slot],
                send_sem=send_sems[a].at[k - 1], recv_sem=recv_sems[a].at[k - 1], device_id=peer, device_id_type=MESH))
    return copies


def _exchange_start(xs, carry, *, gather, name):
    n = len(xs)

    def body(*refs):
        outs = refs[2 * n + 1:]
        for cp in _exchange_copies(refs[:n], refs[n:2 * n], outs[:n], outs[n:2 * n], gather):
            cp.start()

    hbm = lambda a: pltpu.HBM(a.shape, a.dtype)
    lands = [pltpu.with_memory_space_constraint(lax.empty(((N_DEV,) if gather else ()) + x.shape, x.dtype), pltpu.HBM)
             for x in xs]
    ins = [pltpu.with_memory_space_constraint(a, pltpu.HBM) for a in list(xs) + [carry]]
    res = pl.pallas_call(
        body, name=name, in_specs=[_HBM] * (2 * n + 1), out_specs=[_SEM_SPEC] * (2 * n) + [_HBM] * (2 * n + 1),
        out_shape=[_SEMS] * (2 * n) + [hbm(a) for a in list(xs) + lands] + [hbm(carry)],
        input_output_aliases={**{a: 2 * n + a for a in range(2 * n)}, 2 * n: 4 * n}, compiler_params=_SPLIT_COPY,
    )(*ins[:n], *lands, ins[n])
    return (res[:n], res[n:2 * n], res[2 * n:3 * n], res[3 * n:4 * n]), res[4 * n]


def _exchange_wait(handles, after, *, gather, name):
    send_sems, recv_sems, xs, lands = handles
    n = len(xs)

    def body(*refs):
        for cp in _exchange_copies(refs[:n], refs[n:2 * n], refs[2 * n:3 * n], refs[3 * n:4 * n], gather):
            cp.wait_send()
            cp.wait_recv()

    hbm = lambda a: pltpu.HBM(a.shape, a.dtype)
    res = pl.pallas_call(
        body, name=name, in_specs=[_HBM] * (2 * n) + [_SEM_SPEC] * (2 * n) + [_ANY], out_specs=[_HBM] * (2 * n),
        out_shape=[hbm(a) for a in list(xs) + list(lands)], input_output_aliases={a: a for a in range(2 * n)},
        compiler_params=_SPLIT_COPY,
    )(*xs, *lands, *send_sems, *recv_sems, after)
    return res[:n], res[n:]


_ADAMW_PARTS_BLOCK_BYTES = 8 * 1024 * 1024


def _adamw(parts, w, m, v, *, name, own=None):
    r, c = w.shape
    tr = r
    for cand in range(8, r + 1, 8):
        if r % cand == 0 and N_DEV * cand * c * 4 <= _ADAMW_PARTS_BLOCK_BYTES:
            tr = cand
    has_own = own is not None

    def body(*refs):
        p_ref, w_ref, m_ref, v_ref = refs[0], refs[1 + has_own], refs[2 + has_own], refs[3 + has_own]
        g_ref, d_ref, nm_ref, nv_ref = refs[4 + has_own:]
        if has_own:
            me = _slot(*_mesh_pos())
            part = lambda k: jnp.where(me == k, refs[1][...], p_ref[k])
        else:
            part = lambda k: p_ref[k]
        g = part(0)
        for k in range(1, N_DEV):
            g = g + part(k)
        nm = ADAM_B1 * m_ref[...] + (1.0 - ADAM_B1) * g
        nv = ADAM_B2 * v_ref[...] + (1.0 - ADAM_B2) * (g * g)
        m_hat = nm / (1.0 - ADAM_B1 ** ADAM_STEP)
        v_hat = nv / (1.0 - ADAM_B2 ** ADAM_STEP)
        g_ref[...] = g
        d_ref[...] = -ADAM_LR * (m_hat / (jnp.sqrt(v_hat) + ADAM_EPS) + ADAM_WD * w_ref[...])
        nm_ref[...] = nm
        nv_ref[...] = nv

    blk = pl.BlockSpec((tr, c), lambda i: (i, 0))
    return pl.pallas_call(
        body, grid=(r // tr,),
        in_specs=[pl.BlockSpec((N_DEV, tr, c), lambda i: (0, i, 0))] + [blk] * (3 + has_own),
        out_specs=[blk] * 4, out_shape=[jax.ShapeDtypeStruct((r, c), F32)] * 4, name=name,
        compiler_params=_params(("parallel",)),
    )(parts, *([own] if has_own else []), w, m, v)


_FLAT_COLS = 1024
_FLAT_ROW_ALIGN = 128

_SHARD_AXIS = {
    "norm_mix": None, "norm_ffn": None, "ev_w_in": 2, "ev_b_fgate": None, "ev_q_norm": None, "ev_k_norm": None,
    "ev_v_norm": None, "ev_w_spatial": None, "ev_b_spatial": None, "ev_w_out": 1, "od_a_re": None, "od_a_im": None,
    "od_log_dt": None, "od_b_re": None, "od_b_im": None, "od_c_re": None, "od_c_im": None, "od_d": 1, "od_w_glu": 2,
    "ffn_w_up": 2, "ffn_conv_w": 2, "ffn_conv_b": None, "ffn_w_down": 1, "ple_w_proj": 2, "ple_w_gate": 1,
}
_WEIGHTS = list(_SHARD_AXIS)
_REPLICATED = [n for n in _WEIGHTS if _SHARD_AXIS[n] is None]
_COL_SHARDED = ["ev_w_in", "od_w_glu", "ffn_w_up", "ple_w_proj"]
_ROW_SHARDED = ["ev_w_out", "ffn_w_down", "ple_w_gate"]
_SMALL_SHARDED = ["od_d", "ffn_conv_w"]


def _pack(arrays, lead=()):
    nl = len(lead)
    flat = jnp.concatenate([a.reshape(lead + (-1,)) for a in arrays], axis=nl)
    n = flat.shape[nl]
    chunk = _FLAT_COLS * _FLAT_ROW_ALIGN
    total = -(-n // chunk) * chunk
    flat = jnp.pad(flat, [(0, 0)] * nl + [(0, total - n)])
    return flat.reshape(lead + (total // _FLAT_COLS, _FLAT_COLS))


def _unpack(buf, shapes, lead=()):
    nl = len(lead)
    flat = buf.reshape(lead + (-1,))
    out, off = [], 0
    for shp in shapes:
        n = math.prod(shp)
        out.append(lax.slice_in_dim(flat, off, off + n, axis=nl).reshape(lead + tuple(shp)))
        off += n
    return out


def _join_shards(g, axis):
    g = jnp.moveaxis(g, 0, axis)
    shp = g.shape
    return g.reshape(shp[:axis] + (shp[axis] * shp[axis + 1],) + shp[axis + 2:])


def _split_shards(full, axis):
    shp = full.shape
    g = full.reshape(shp[:axis] + (N_DEV, shp[axis] // N_DEV) + shp[axis + 1:])
    return jnp.moveaxis(g, axis, 0)


def _local_step(x, p, tgt, w, before_part, on_grads):
    saved = []
    h = x
    for i in range(DEPTH):
        x0 = before_part(i, 0, h)
        if i % 2 == 0:
            x1, sv_mix = _even_fwd(x0, w, i // 2, i)
        else:
            x1, sv_mix = _odd_fwd(x0, w, i // 2, i)
        x1 = before_part(i, 1, x1)
        x2, sv_ffn = _ffn_fwd(x1, w, i)
        x3, sv_ple = _ple_fwd(x2, p[i], w, i)
        saved.append((x0, x1, x2, sv_mix, sv_ffn, sv_ple))
        h = x3
    loss, dh = _loss_and_grad(h, tgt)
    per_layer = {}
    for i in reversed(range(DEPTH)):
        x0, x1, x2, sv_mix, sv_ffn, sv_ple = saved[i]
        dx2, dx2b, g_ple = _ple_bwd(dh, x2, p[i], sv_ple, w, i)
        dx1, dx1b, g_ffn = _ffn_bwd(dx2, dx2b, x1, sv_ffn, w, i)
        dx1 = on_grads(i, 0, {**g_ple, **g_ffn}, dx1)
        if i % 2 == 0:
            dh, g_mix = _even_bwd(dx1, dx1b, x0, sv_mix, w, i // 2, i)
        else:
            dh, g_mix = _odd_bwd(dx1, x0, sv_mix, w, i // 2, i)
        dh = on_grads(i, 1, g_mix, dh)
        layer_grads = {**g_ple, **g_ffn, **g_mix}
        for name, g in layer_grads.items():
            per_layer.setdefault(name, []).append(g)
    grads = {name: gs[::-1] for name, gs in per_layer.items()}
    return loss, dh, grads


def kernel(x, p, norm_mix, norm_ffn, ev_w_in, ev_b_fgate, ev_q_norm, ev_k_norm, ev_v_norm, ev_w_spatial, ev_b_spatial, ev_w_out, od_a_re, od_a_im, od_log_dt, od_b_re, od_b_im, od_c_re, od_c_im, od_d, od_w_glu, ffn_w_up, ffn_conv_w, ffn_conv_b, ffn_w_down, ple_w_proj, ple_w_gate, loss_target, m_norm_mix, m_norm_ffn, m_ev_w_in, m_ev_b_fgate, m_ev_q_norm, m_ev_k_norm, m_ev_v_norm, m_ev_w_spatial, m_ev_b_spatial, m_ev_w_out, m_od_a_re, m_od_a_im, m_od_log_dt, m_od_b_re, m_od_b_im, m_od_c_re, m_od_c_im, m_od_d, m_od_w_glu, m_ffn_w_up, m_ffn_conv_w, m_ffn_conv_b, m_ffn_w_down, m_ple_w_proj, m_ple_w_gate, v_norm_mix, v_norm_ffn, v_ev_w_in, v_ev_b_fgate, v_ev_q_norm, v_ev_k_norm, v_ev_v_norm, v_ev_w_spatial, v_ev_b_spatial, v_ev_w_out, v_od_a_re, v_od_a_im, v_od_log_dt, v_od_b_re, v_od_b_im, v_od_c_re, v_od_c_im, v_od_d, v_od_w_glu, v_ffn_w_up, v_ffn_conv_w, v_ffn_conv_b, v_ffn_w_down, v_ple_w_proj, v_ple_w_gate):
    args = locals()
    wl = {n: args[n] for n in _WEIGHTS}
    ml = {n: args["m_" + n] for n in _WEIGHTS}
    vl = {n: args["v_" + n] for n in _WEIGHTS}

    kinds = ("g", "d", "m", "v")

    def local(n, layer, a):
        return a[n][layer].T if n in _COL_SHARDED else a[n][layer]

    me = _slot(*_mesh_pos())
    w = {n: wl[n] for n in _REPLICATED}
    small = _all_gather(_pack([wl[n] for n in _SMALL_SHARDED]), name="gather_small_weights")
    for n, g in zip(_SMALL_SHARDED, _unpack(small, [wl[n].shape for n in _SMALL_SHARDED], lead=(N_DEV,))):
        w[n] = _join_shards(g, _SHARD_AXIS[n])

    def layer_of(n, i):
        return i if wl[n].shape[0] == DEPTH else i // 2

    def weights_of(i, part):
        if part == 0:
            return ["ev_w_in", "ev_w_out"] if i % 2 == 0 else ["od_w_glu"]
        return ["ffn_w_up", "ffn_w_down", "ple_w_proj", "ple_w_gate"]

    gathers = {}
    for n in _COL_SHARDED + _ROW_SHARDED:
        w[n] = {}

    def start_gather(i, carry):
        for part in (0, 1):
            blocks = [local(n, layer_of(n, i), wl).astype(BF16) for n in weights_of(i, part)]
            gathers[i, part], carry = _exchange_start(blocks, carry, gather=True,
                                                      name="gather_start_layer%d_part%d" % (i, part))
        return carry

    def finish_gather(i, part, h):
        sent, landed = _exchange_wait(gathers[i, part], h, gather=True, name="gather_wait_layer%d_part%d" % (i, part))
        for n, block, land in zip(weights_of(i, part), sent, landed):
            full = lax.dynamic_update_slice(land, block[None], (me, 0, 0)).reshape(N_DEV * block.shape[0], block.shape[1])
            if n == "ev_w_in":
                full = jnp.pad(full, ((0, IN_COLS_PAD - IN_COLS), (0, 0)))
            w[n][layer_of(n, i)] = full
        if (i, part) == (0, 0):
            for later in range(1, DEPTH):
                h = start_gather(later, h)
        return h

    x_in = start_gather(0, x[0])

    in_flight = {}

    def start_scatter(i, part, part_grads, dh):
        names = [n for n in _COL_SHARDED + _ROW_SHARDED if n in part_grads]
        xs = [part_grads[n].reshape(N_DEV, part_grads[n].shape[0] // N_DEV, part_grads[n].shape[1]) for n in names]
        handles, dh = _exchange_start(xs, dh, gather=False, name="scatter_start_layer%d_part%d" % (i, part))
        in_flight[i, part] = (names, handles)
        return dh

    loss, grad_x, grads = _local_step(x_in, p[:, 0], loss_target[0], w, finish_gather, start_scatter)
    loss = lax.psum(loss, ("x", "y", "c"))

    rep_send = _pack([jnp.stack(grads[n]) for n in _REPLICATED])
    small_send = _pack([_split_shards(jnp.stack(grads[n]), _SHARD_AXIS[n]) for n in _SMALL_SHARDED], lead=(N_DEV,))
    rep_handles, grad_x = _exchange_start([rep_send], grad_x, gather=True, name="gather_start_replicated_grads")
    small_handles, grad_x = _exchange_start([small_send], grad_x, gather=False, name="scatter_start_small_grads")

    out = {kind: {} for kind in kinds}
    done = {}
    res = [grad_x]
    for i, part in sorted(in_flight, key=lambda ip: (-ip[0], ip[1])):
        names, handles = in_flight[i, part]
        sent, landed = _exchange_wait(handles, res[0], gather=False, name="scatter_wait_layer%d_part%d" % (i, part))
        for n, x_sent, parts in zip(names, sent, landed):
            layer = layer_of(n, i)
            own = lax.dynamic_index_in_dim(x_sent, me, axis=0, keepdims=False)
            res = _adamw(parts, local(n, layer, wl), local(n, layer, ml), local(n, layer, vl), name="adamw_" + n, own=own)
            done.setdefault(n, {})[layer] = [r.T if n in _COL_SHARDED else r for r in res]
    for n, layers in done.items():
        for k, kind in enumerate(kinds):
            out[kind][n] = jnp.stack([layers[layer][k] for layer in sorted(layers)])

    last = res[0]
    (small_sent,), (parts,) = _exchange_wait(small_handles, last, gather=False, name="scatter_wait_small_grads")
    shapes = [wl[n].shape for n in _SMALL_SHARDED]
    res = _adamw(parts, _pack([wl[n] for n in _SMALL_SHARDED]), _pack([ml[n] for n in _SMALL_SHARDED]),
                 _pack([vl[n] for n in _SMALL_SHARDED]), name="adamw_small",
                 own=lax.dynamic_index_in_dim(small_sent, me, axis=0, keepdims=False))
    for kind, buf in zip(kinds, res):
        out[kind].update(zip(_SMALL_SHARDED, _unpack(buf, shapes)))

    (rep_sent,), (parts,) = _exchange_wait(rep_handles, last, gather=True, name="gather_wait_replicated_grads")
    shapes = [wl[n].shape for n in _REPLICATED]
    res = _adamw(parts, _pack([wl[n] for n in _REPLICATED]), _pack([ml[n] for n in _REPLICATED]),
                 _pack([vl[n] for n in _REPLICATED]), name="adamw_replicated", own=rep_sent)
    for kind, buf in zip(kinds, res):
        out[kind].update(zip(_REPLICATED, _unpack(buf, shapes)))

    return (loss, grad_x[None], *[out["g"][n] for n in _WEIGHTS], *[out["d"][n] for n in _WEIGHTS],
            *[out["m"][n] for n in _WEIGHTS], *[out["v"][n] for n in _WEIGHTS])
```

```python
import functools
import math

import jax
import jax.numpy as jnp
from jax import lax
from jax.experimental import pallas as pl
from jax.experimental.pallas import tpu as pltpu

F32 = jnp.float32
BF16 = jnp.bfloat16

D_MODEL = 1024
DEPTH = 4
A_GROUPS = 4
A_CHUNK = 128
A_WIDTH = 512
B_HEADS = 8
B_HEAD_DIM = 64
B_WIDTH = 512
IN_COLS = 2 * A_WIDTH + 3 * B_WIDTH + B_HEADS
IN_COLS_PAD = 2688
S5_GROUP_CH = 16
S5_GROUPS = 64
S5_STATE = 64
S5_N = S5_GROUPS * S5_STATE
S5_BLOCKS = 8
D_FF = 2816
PLE_DIM = 256
EPS = 1e-6
NEG_INF = -1e30
N_DEV = 8

ADAM_LR = 0.001
ADAM_B1 = 0.9
ADAM_B2 = 0.999
ADAM_EPS = 1e-08
ADAM_WD = 0.01
ADAM_STEP = 10

VMEM_LIMIT_BYTES = 56 * 1024 * 1024
MESH = pl.DeviceIdType.MESH


def _params(sem, vmem=VMEM_LIMIT_BYTES):
    return pltpu.CompilerParams(dimension_semantics=sem, vmem_limit_bytes=vmem)


_GELU_K = math.sqrt(2.0 / math.pi)


def _gelu(x):
    return x * (0.5 * (1.0 + jnp.tanh(_GELU_K * (x + 0.044715 * (x * x * x)))))


def _gelu_grad(x):
    t = jnp.tanh(_GELU_K * (x + 0.044715 * (x * x * x)))
    return 0.5 * (1.0 + t) + 0.5 * x * (1.0 - t * t) * (_GELU_K * (1.0 + 3.0 * 0.044715 * (x * x)))


def _sigmoid(x):
    return 0.5 * jnp.tanh(0.5 * x) + 0.5


def _rstd(x):
    return lax.rsqrt(jnp.mean(x * x, axis=-1, keepdims=True) + EPS)


def _rms_bwd(x, dy, g):
    r = _rstd(x)
    xh = x * r
    dyg = dy if g is None else dy * g
    dx = r * (dyg - xh * jnp.mean(dyg * xh, axis=-1, keepdims=True))
    return dx, dy * xh


def _shift_down(blk, halo, k):
    tr = blk.shape[0]
    r = pltpu.roll(blk, k, 0)
    hr = pltpu.roll(halo, k, 0)
    first = jnp.where(lax.broadcasted_iota(jnp.int32, hr.shape, 0) < k, hr, r[0:8])
    return jnp.concatenate([first, r[8:tr]], axis=0)


def _rw(fn, rows, consts, outs, accs=(), *, tr, name, prev=(), nxt=(), widths=None):
    s = rows[0].shape[0]
    tr = min(tr, s)
    n = s // tr
    nr, nc, npv, nnx, no, na = len(rows), len(consts), len(prev), len(nxt), len(outs), len(accs)
    widths = widths or [None] * nr

    def body(*refs):
        ins, out_refs = refs[:nr + nc + npv + nnx], refs[nr + nc + npv + nnx:]
        i = pl.program_id(0)
        vals = [r[...] for r in ins[:nr + nc]]
        vals += [jnp.where(i == 0, 0.0, r[...]) for r in ins[nr + nc:nr + nc + npv]]
        vals += [jnp.where(i == n - 1, 0.0, r[...]) for r in ins[nr + nc + npv:]]
        res = fn(*vals)
        if not isinstance(res, (tuple, list)):
            res = (res,)
        for k in range(no):
            out_refs[k][...] = res[k].astype(out_refs[k].dtype)
        if na:
            @pl.when(i == 0)
            def _():
                for k in range(na):
                    out_refs[no + k][...] = jnp.zeros_like(out_refs[no + k])

            for k in range(na):
                out_refs[no + k][...] += res[no + k]

    in_specs = []
    for a, w in zip(rows, widths):
        if w is None:
            in_specs.append(pl.BlockSpec((tr, a.shape[1]), lambda i: (i, 0)))
        else:
            in_specs.append(pl.BlockSpec((tr, w[1]), functools.partial(lambda i, cb: (i, cb), cb=w[0])))
    for c in consts:
        in_specs.append(pl.BlockSpec(c.shape, functools.partial(lambda i, nd: (0,) * nd, nd=c.ndim)))
    t8 = tr // 8
    for a in prev:
        in_specs.append(pl.BlockSpec((8, a.shape[1]), lambda i: (jnp.maximum(i * t8 - 1, 0), 0)))
    for a in nxt:
        in_specs.append(pl.BlockSpec((8, a.shape[1]), lambda i: (jnp.minimum((i + 1) * t8, s // 8 - 1), 0)))
    out_shape = [jax.ShapeDtypeStruct((s, w), dt) for w, dt in outs]
    out_specs = [pl.BlockSpec((tr, w), lambda i: (i, 0)) for w, _ in outs]
    out_shape += [jax.ShapeDtypeStruct(a, F32) for a in accs]
    out_specs += [pl.BlockSpec(a, lambda i: (0, 0)) for a in accs]
    res = pl.pallas_call(
        body, grid=(n,), in_specs=in_specs, out_specs=out_specs, out_shape=out_shape, name=name,
        compiler_params=_params(("arbitrary",) if na else ("parallel",)),
    )(*rows, *consts, *prev, *nxt)
    return res


def _pick(n, cap):
    if n <= cap:
        return n
    best = None
    for d in range(128, cap + 1, 128):
        if n % d == 0:
            best = d
    assert best is not None, (n, cap)
    return best


def _mm(a, b, *, ta=False, tb=False, out_dtype=F32, add=None, epilogue=None, name, bm=1024, bn=512, bk=None):
    (k_dim, m) = a.shape if ta else a.shape[::-1]
    n = b.shape[0] if tb else b.shape[1]
    assert (b.shape[1] if tb else b.shape[0]) == k_dim
    if bk is None:
        bk = 1024 if ta else 2816
    bm, bn, bk = _pick(m, bm), _pick(n, bn), _pick(k_dim, bk)
    nk = k_dim // bk
    dims = (((0 if ta else 1,), (1 if tb else 0,)), ((), ()))
    assert add is None or epilogue is None
    epi_fn, extra, out_dtypes, vecs, n_sums = (tuple(epilogue or (None, [add] if add is not None else [], (out_dtype,)))
                                               + ((), 0))[:5]
    n_extra, n_vec, n_out = len(extra), len(vecs), len(out_dtypes)

    def body(*refs):
        a_ref, b_ref = refs[0], refs[1]
        extra_refs = refs[2:2 + n_extra + n_vec]
        out_refs = refs[2 + n_extra + n_vec:2 + n_extra + n_vec + n_out + n_sums]
        p = lax.dot_general(a_ref[...].astype(BF16), b_ref[...].astype(BF16), dims, preferred_element_type=F32)

        def finish(acc):
            if epi_fn is not None:
                res = epi_fn(acc, *[r[...] for r in extra_refs])
            else:
                res = (acc + extra_refs[0][...],) if n_extra else (acc,)
            for o_ref, r in zip(out_refs, res):
                o_ref[...] = r.reshape(o_ref.shape).astype(o_ref.dtype)

        if nk == 1:
            finish(p)
        else:
            acc_ref = refs[2 + n_extra + n_vec + n_out + n_sums]
            kk = pl.program_id(2)

            @pl.when(kk == 0)
            def _():
                acc_ref[...] = p

            @pl.when(kk > 0)
            def _():
                acc_ref[...] += p

            @pl.when(kk == nk - 1)
            def _():
                finish(acc_ref[...])

    a_spec = pl.BlockSpec((bk, bm), lambda i, j, k: (k, i)) if ta else pl.BlockSpec((bm, bk), lambda i, j, k: (i, k))
    b_spec = pl.BlockSpec((bn, bk), lambda i, j, k: (j, k)) if tb else pl.BlockSpec((bk, bn), lambda i, j, k: (k, j))
    tile = pl.BlockSpec((bm, bn), lambda i, j, k: (i, j))
    vec = pl.BlockSpec((1, bn), lambda i, j, k: (0, j))
    sums = pl.BlockSpec((1, 1, bn), lambda i, j, k: (i, 0, j))
    res = pl.pallas_call(
        body, grid=(m // bm, n // bn, nk), in_specs=[a_spec, b_spec] + [tile] * n_extra + [vec] * n_vec,
        out_specs=[tile] * n_out + [sums] * n_sums,
        out_shape=[jax.ShapeDtypeStruct((m, n), dt) for dt in out_dtypes]
        + [jax.ShapeDtypeStruct((m // bm, 1, n), F32)] * n_sums,
        scratch_shapes=[pltpu.VMEM((bm, bn), F32)] if nk > 1 else [],
        name=name, compiler_params=_params(("parallel", "parallel", "arbitrary")),
    )(a, b, *extra, *vecs)
    return res if epilogue is not None else res[0]


def _mm_rms_bwd(a, b, x, g, dx_add, *, tb=False, name, bf16_copy=False):
    n = x.shape[1]
    has_g = g is not None

    def fn(acc, xv, addv, *gv):
        dx, dg = _rms_bwd(xv, acc, gv[0] if has_g else None)
        dx = dx + addv
        return (dx,) + ((dx,) if bf16_copy else ()) + ((jnp.sum(dg, axis=0, keepdims=True),) if has_g else ())

    outs = (F32,) + ((BF16,) if bf16_copy else ())
    res = _mm(a, b, tb=tb, name=name, bm=512, bn=n, epilogue=(fn, [x, dx_add], outs, [g] if has_g else [], int(has_g)))
    if has_g:
        return tuple(res[:-1]) + (jnp.sum(res[-1], axis=0),)
    return tuple(res) + (None,)


def _rms_fwd(x, g, *, name):
    if g is None:
        return _rw(lambda xv: xv * _rstd(xv), [x], [], [(x.shape[1], BF16)], tr=512, name=name)[0]
    return _rw(lambda xv, gv: xv * _rstd(xv) * gv, [x], [g], [(x.shape[1], BF16)], tr=512, name=name)[0]


def _rms_bwd_call(x, dy, g, dx_add, *, name, tr=512):
    w = x.shape[1]

    def fn(xv, dyv, addv, gv):
        dx, dg = _rms_bwd(xv, dyv, gv)
        return dx + addv, jnp.sum(dg, axis=0, keepdims=True)

    return _rw(fn, [x, dy, dx_add], [g], [(w, F32)], [(1, w)], tr=tr, name=name)


def _conv_taps(hup, halo):
    return _shift_down(hup, halo, 1), _shift_down(hup, halo, 2)


def _ffn_fwd(x1, w, i):
    h2b = _rms_fwd(x1, w["norm_ffn"][i:i + 1], name="ffn_norm")
    hup = _mm(h2b, w["ffn_w_up"][i], tb=True, name="ffn_up", bn=1408)
    cw, cb = w["ffn_conv_w"][i], w["ffn_conv_b"][i:i + 1]

    def fn(hv, cwv, cbv, halo):
        h1, h2 = _conv_taps(hv, halo)
        hc = cbv + cwv[0:1] * h2
        hc = hc + cwv[1:2] * h1
        hc = hc + cwv[2:3] * hv
        g, up = hc[:, :D_FF], hc[:, D_FF:]
        return (g * _sigmoid(g)) * up

    a = _rw(fn, [hup], [cw, cb], [(D_FF, BF16)], tr=128, name="ffn_conv_gate", prev=[hup])[0]
    x2 = _mm(a, w["ffn_w_down"][i], add=x1, name="ffn_down")
    return x2, (h2b, hup, a)


def _ffn_bwd(dx2, dx2b, x1, saved, w, i):
    h2b, hup, a = saved
    cw, cb = w["ffn_conv_w"][i], w["ffn_conv_b"][i:i + 1]
    da = _mm(dx2b, w["ffn_w_down"][i], tb=True, name="ffn_down_dx", bn=1408)
    d_wdown = _mm(a, dx2b, ta=True, name="ffn_down_dw", bm=1408)

    def fn1(hv, dav, cwv, cbv, h_prev, h_next, da_next):
        tr = hv.shape[0]
        he = jnp.concatenate([hv, h_next], axis=0)
        dae = jnp.concatenate([dav, da_next], axis=0)
        h1, h2 = _conv_taps(he, h_prev)
        hc = cbv + cwv[0:1] * h2
        hc = hc + cwv[1:2] * h1
        hc = hc + cwv[2:3] * he
        g, up = hc[:, :D_FF], hc[:, D_FF:]
        sg = _sigmoid(g)
        d_up = dae * (g * sg)
        d_g = dae * up * (sg * (1.0 + g * (1.0 - sg)))
        d_hce = jnp.concatenate([d_g, d_up], axis=1)
        rows_e = tr + 8
        d_hup = (cwv[2:3] * d_hce[:tr] + cwv[1:2] * pltpu.roll(d_hce, rows_e - 1, 0)[:tr]
                 + cwv[0:1] * pltpu.roll(d_hce, rows_e - 2, 0)[:tr])
        d_hc = d_hce[:tr]
        col = lambda v: jnp.sum(v, axis=0, keepdims=True)
        return d_hup, col(d_hc), col(d_hc * h2[:tr]), col(d_hc * h1[:tr]), col(d_hc * hv)

    w2 = 2 * D_FF
    d_hup, d_cb, d_cw0, d_cw1, d_cw2 = _rw(fn1, [hup, da], [cw, cb], [(w2, BF16)], [(1, w2)] * 4, tr=128,
                                           name="ffn_conv_gate_bwd", prev=[hup], nxt=[hup, da])
    dx1, dx1b, d_norm = _mm_rms_bwd(d_hup, w["ffn_w_up"][i], x1, w["norm_ffn"][i:i + 1], dx2, name="ffn_up_dx",
                                    bf16_copy=True)
    d_wup_t = _mm(h2b, d_hup, ta=True, name="ffn_up_dw", bn=1408).T
    grads = {"ffn_w_up": d_wup_t, "ffn_w_down": d_wdown, "ffn_conv_b": d_cb[0],
             "ffn_conv_w": jnp.concatenate([d_cw0, d_cw1, d_cw2], axis=0), "norm_ffn": d_norm[0]}
    return dx1, dx1b, grads


def _ple_fwd(x2, p_i, w, i):
    rn = _rms_fwd(x2, None, name="ple_norm")
    gl = _mm(rn, w["ple_w_gate"][i], name="ple_gate")
    x3 = _mm(p_i, w["ple_w_proj"][i], tb=True, name="ple_proj",
             epilogue=(lambda pp, xv, g: (xv + _sigmoid(g) * pp,), [x2, gl], (F32,)))[0]
    return x3, (rn, gl)


def _ple_bwd(dx3, x2, p_i, saved, w, i):
    rn, gl = saved

    def fn(pp, dv, g):
        sg = _sigmoid(g)
        return dv * sg, dv * pp * (sg * (1.0 - sg))

    d_pp, d_pre = _mm(p_i, w["ple_w_proj"][i], tb=True, name="ple_proj_bwd", epilogue=(fn, [dx3, gl], (BF16, BF16)))
    d_wproj_t = _mm(p_i, d_pp, ta=True, name="ple_proj_dw", bn=1024).T
    d_wgate = _mm(rn, d_pre, ta=True, name="ple_gate_dw")
    dx2, dx2b, _ = _mm_rms_bwd(d_pre, w["ple_w_gate"][i], x2, None, dx3, tb=True, name="ple_gate_dx", bf16_copy=True)
    return dx2, dx2b, {"ple_w_proj": d_wproj_t, "ple_w_gate": d_wgate}


def _loss_and_grad(xf, tgt):
    def fn(xv, tv):
        d = xv - tv
        part = 0.5 * jnp.sum(jnp.mean(d * d, axis=-1, keepdims=True), axis=0, keepdims=True)
        return d * (1.0 / D_MODEL), part

    dx, loss = _rw(fn, [xf, tgt], [], [(D_MODEL, F32)], [(1, 1)], tr=512, name="loss_head")
    return loss[0, 0], dx


def _tril_mask():
    shp = (A_CHUNK, A_CHUNK)
    return lax.broadcasted_iota(jnp.int32, shp, 0) >= lax.broadcasted_iota(jnp.int32, shp, 1)


def _gmlp_fwd(z, w_s, b_s, v_gain, *, tr=512):
    s = z.shape[0]
    tr = min(tr, s)
    gw = A_WIDTH // A_GROUPS

    def body(z_ref, w_ref, b_ref, g_ref, y_ref):
        tril = _tril_mask()
        for g in range(A_GROUPS):
            wg = jnp.where(tril, w_ref[g], 0.0).astype(BF16)
            cols = slice(g * gw, (g + 1) * gw)
            vcols = slice(A_WIDTH + g * gw, A_WIDTH + (g + 1) * gw)
            for c in range(tr // A_CHUNK):
                rows = slice(c * A_CHUNK, (c + 1) * A_CHUNK)
                v = _gelu(z_ref[rows, vcols])
                vn = v * _rstd(v) * g_ref[:, cols]
                sv = jnp.dot(wg, vn.astype(BF16), preferred_element_type=F32) + b_ref[g]
                y_ref[rows, cols] = (_gelu(z_ref[rows, cols]) * sv).astype(BF16)

    return pl.pallas_call(
        body, grid=(s // tr,),
        in_specs=[pl.BlockSpec((tr, 2 * A_WIDTH), lambda i: (i, 0)),
                  pl.BlockSpec(w_s.shape, lambda i: (0, 0, 0)), pl.BlockSpec(b_s.shape, lambda i: (0, 0, 0)),
                  pl.BlockSpec(v_gain.shape, lambda i: (0, 0))],
        out_specs=pl.BlockSpec((tr, A_WIDTH), lambda i: (i, 0)),
        out_shape=jax.ShapeDtypeStruct((s, A_WIDTH + B_WIDTH), BF16), name="gmlp_fwd",
        compiler_params=_params(("parallel",)),
    )(z, w_s, b_s, v_gain)


def _gmlp_bwd(z, d_ymix, w_s, b_s, v_gain, dz, *, tr=512):
    s = z.shape[0]
    tr = min(tr, s)
    gw = A_WIDTH // A_GROUPS

    def body(z_ref, dy_ref, w_ref, b_ref, g_ref, _, dz_ref, dw_ref, db_ref, dg_ref):
        @pl.when(pl.program_id(0) == 0)
        def _():
            dw_ref[...] = jnp.zeros_like(dw_ref)
            db_ref[...] = jnp.zeros_like(db_ref)
            dg_ref[...] = jnp.zeros_like(dg_ref)

        tril = _tril_mask()
        for g in range(A_GROUPS):
            wg = jnp.where(tril, w_ref[g], 0.0).astype(BF16)
            cols = slice(g * gw, (g + 1) * gw)
            vcols = slice(A_WIDTH + g * gw, A_WIDTH + (g + 1) * gw)
            gain = g_ref[:, cols]
            for c in range(tr // A_CHUNK):
                rows = slice(c * A_CHUNK, (c + 1) * A_CHUNK)
                va, ua = z_ref[rows, vcols], z_ref[rows, cols]
                v = _gelu(va)
                r = _rstd(v)
                vh = v * r
                vnb = (vh * gain).astype(BF16)
                sv = jnp.dot(wg, vnb, preferred_element_type=F32) + b_ref[g]
                dy = dy_ref[rows, cols]
                d_sv = dy * _gelu(ua)
                dz_ref[rows, cols] = (dy * sv * _gelu_grad(ua)).astype(BF16)
                d_svb = d_sv.astype(BF16)
                d_vn = lax.dot_general(wg, d_svb, (((0,), (0,)), ((), ())), preferred_element_type=F32)
                dwp = lax.dot_general(d_svb, vnb, (((1,), (1,)), ((), ())), preferred_element_type=F32)
                dw_ref[g] += jnp.where(tril, dwp, 0.0)
                db_ref[g] += jnp.sum(d_sv, axis=1, keepdims=True)
                dg_ref[:, cols] += jnp.sum(d_vn * vh, axis=0, keepdims=True)
                d_vh = d_vn * gain
                d_v = r * (d_vh - vh * jnp.mean(d_vh * vh, axis=-1, keepdims=True))
                dz_ref[rows, vcols] = (d_v * _gelu_grad(va)).astype(BF16)

    return pl.pallas_call(
        body, grid=(s // tr,),
        in_specs=[pl.BlockSpec((tr, 2 * A_WIDTH), lambda i: (i, 0)), pl.BlockSpec((tr, A_WIDTH), lambda i: (i, 0)),
                  pl.BlockSpec(w_s.shape, lambda i: (0, 0, 0)), pl.BlockSpec(b_s.shape, lambda i: (0, 0, 0)),
                  pl.BlockSpec(v_gain.shape, lambda i: (0, 0)), _ANY],
        out_specs=[pl.BlockSpec((tr, 2 * A_WIDTH), lambda i: (i, 0)), pl.BlockSpec(w_s.shape, lambda i: (0, 0, 0)),
                   pl.BlockSpec(b_s.shape, lambda i: (0, 0, 0)), pl.BlockSpec(v_gain.shape, lambda i: (0, 0))],
        out_shape=[jax.ShapeDtypeStruct(dz.shape, dz.dtype), jax.ShapeDtypeStruct(w_s.shape, F32),
                   jax.ShapeDtypeStruct(b_s.shape, F32), jax.ShapeDtypeStruct(v_gain.shape, F32)],
        input_output_aliases={5: 0}, name="gmlp_bwd", compiler_params=_params(("arbitrary",)),
    )(z, d_ymix, w_s, b_s, v_gain, dz)


def _dot3(x, ub):
    x1 = x.astype(BF16)
    r1 = x - x1.astype(F32)
    x2 = r1.astype(BF16)
    x3 = (r1 - x2.astype(F32)).astype(BF16)
    d = lambda a: jnp.dot(a, ub, preferred_element_type=F32)
    return d(x1) + d(x2) + d(x3)


def _log_sigmoid(x):
    return jnp.minimum(x, 0.0) - jnp.log(1.0 + jnp.exp(-jnp.abs(x)))


def _fgate_fwd(f_t, b_col, *, tb=256):
    h, s = f_t.shape
    tb = min(tb, s)

    def body(f_ref, b_ref, c_ref, carry):
        @pl.when(pl.program_id(0) == 0)
        def _():
            carry[...] = jnp.zeros_like(carry)

        lf = _log_sigmoid(f_ref[...] + b_ref[...])
        shp = (tb, tb)
        upper = (lax.broadcasted_iota(jnp.int32, shp, 0) <= lax.broadcasted_iota(jnp.int32, shp, 1)).astype(BF16)
        c_ref[...] = _dot3(lf, upper) + carry[...]
        carry[...] += jnp.sum(lf, axis=1, keepdims=True)

    return pl.pallas_call(
        body, grid=(s // tb,),
        in_specs=[pl.BlockSpec((h, tb), lambda i: (0, i)), pl.BlockSpec((h, 1), lambda i: (0, 0))],
        out_specs=pl.BlockSpec((h, tb), lambda i: (0, i)), out_shape=jax.ShapeDtypeStruct((h, s), F32),
        scratch_shapes=[pltpu.VMEM((h, 1), F32)], name="fgate_fwd", compiler_params=_params(("arbitrary",)),
    )(f_t, b_col)


def _fgate_bwd(f_t, b_col, dc_a, dc_b, *, tb=256):
    h, s = f_t.shape
    tb = min(tb, s)
    n = s // tb

    def body(f_ref, b_ref, da_ref, db_ref, df_ref, dbias_ref, carry):
        @pl.when(pl.program_id(0) == 0)
        def _():
            carry[...] = jnp.zeros_like(carry)
            dbias_ref[...] = jnp.zeros_like(dbias_ref)

        dc = da_ref[...] + db_ref[...]
        shp = (tb, tb)
        lower = (lax.broadcasted_iota(jnp.int32, shp, 0) >= lax.broadcasted_iota(jnp.int32, shp, 1)).astype(BF16)
        d_lf = _dot3(dc, lower) + carry[...]
        carry[...] += jnp.sum(dc, axis=1, keepdims=True)
        df = d_lf * (1.0 - _sigmoid(f_ref[...] + b_ref[...]))
        df_ref[...] = df
        dbias_ref[...] += jnp.sum(df, axis=1, keepdims=True)

    blk = pl.BlockSpec((h, tb), lambda i: (0, n - 1 - i))
    return pl.pallas_call(
        body, grid=(n,), in_specs=[blk, pl.BlockSpec((h, 1), lambda i: (0, 0)), blk, blk],
        out_specs=[blk, pl.BlockSpec((h, 1), lambda i: (0, 0))],
        out_shape=[jax.ShapeDtypeStruct((h, s), F32), jax.ShapeDtypeStruct((h, 1), F32)],
        scratch_shapes=[pltpu.VMEM((h, 1), F32)], name="fgate_bwd", compiler_params=_params(("arbitrary",)),
    )(f_t, b_col, dc_a, dc_b)


_NT = (((1,), (1,)), ((), ()))


def _causal(shape, row0, col0, transposed):
    r = lax.broadcasted_iota(jnp.int32, shape, 0) + row0
    c = lax.broadcasted_iota(jnp.int32, shape, 1) + col0
    return (r <= c) if transposed else (c <= r)


_ATT_SCALE = B_HEAD_DIM ** -0.5
ATT_W = 128
_COL_CQ, _COL_CK, _COL_LSE, _COL_DELTA = 64, 67, 70, 64
_ATT_BLK = 2048
_ATT_SUB = 512


def _split3(x):
    h = x.astype(BF16).astype(F32)
    r = x - h
    m = r.astype(BF16).astype(F32)
    return h, m, (r - m).astype(BF16).astype(F32)


def _put_cols(base, lane, col0, parts):
    for t, pv in enumerate(parts):
        base = jnp.where(lane == col0 + t, pv, base)
    return base


_HEAD_PAIRS = B_HEADS // 2
_Q_BLOCK, _K_BLOCK, _V_BLOCK = 8, 12, 16
_F_BLOCK = 20
_O_BLOCK = 4
_ANY = pl.BlockSpec(memory_space=pl.ANY)


def _head_half(ref, h):
    x = ref[...]
    x = jnp.where((h & 1) == 1, pltpu.roll(x, B_HEAD_DIM, 1), x)
    return jnp.where(lax.broadcasted_iota(jnp.int32, x.shape, 1) < B_HEAD_DIM, x, 0.0)


def _pair(a, b):
    lane = lax.broadcasted_iota(jnp.int32, a.shape, 1)
    return jnp.where(lane < B_HEAD_DIM, a, pltpu.roll(b, B_HEAD_DIM, 1))


def _attn_prep(z, c_col, gq, gk, *, tr=1024):
    s = z.shape[0]
    tr = min(tr, s)
    n = s // tr

    def body(q_ref, k_ref, v_ref, c_ref, gq_ref, gk_ref, qo_ref, ko_ref, vo_ref):
        h = pl.program_id(0)
        q, k, v = _head_half(q_ref, h), _head_half(k_ref, h), _head_half(v_ref, h)
        lane = lax.broadcasted_iota(jnp.int32, q.shape, 1)
        rq = lax.rsqrt(jnp.sum(q * q, axis=-1, keepdims=True) * (1.0 / B_HEAD_DIM) + EPS)
        rk = lax.rsqrt(jnp.sum(k * k, axis=-1, keepdims=True) * (1.0 / B_HEAD_DIM) + EPS)
        c = c_ref[...]
        ch, cm, cl = _split3(c)
        one = jnp.ones_like(c)
        qq = _put_cols(q * rq * gq_ref[...] * _ATT_SCALE, lane, _COL_CQ, (ch, cm, cl))
        qq = _put_cols(qq, lane, _COL_CK, (one, one, one))
        kk = _put_cols(k * rk * gk_ref[...], lane, _COL_CQ, (one, one, one))
        kk = _put_cols(kk, lane, _COL_CK, (-ch, -cm, -cl))
        kk = _put_cols(kk, lane, _COL_LSE, (one, one, one))
        qo_ref[...] = qq.astype(BF16)
        ko_ref[...] = kk.astype(BF16)
        vo_ref[...] = _put_cols(v, lane, _COL_DELTA, (one, one, one)).astype(BF16)

    def zcol(base):
        return pl.BlockSpec((tr, ATT_W), lambda h, i: (i, base + (h >> 1)))

    rows = pl.BlockSpec((tr, ATT_W), lambda h, i: (h * n + i, 0))
    gain = pl.BlockSpec((1, ATT_W), lambda h, i: (0, 0))
    return pl.pallas_call(
        body, grid=(B_HEADS, n),
        in_specs=[zcol(_Q_BLOCK), zcol(_K_BLOCK), zcol(_V_BLOCK), pl.BlockSpec((tr, 1), lambda h, i: (h * n + i, 0)),
                  gain, gain],
        out_specs=[rows] * 3, out_shape=[jax.ShapeDtypeStruct((B_HEADS * s, ATT_W), BF16)] * 3, name="attn_prep",
        compiler_params=_params(("parallel", "parallel")),
    )(z, z, z, c_col, gq, gk)


def _attn_bwd_prep(d_ymix, o, qp, lse, *, tr=1024):
    s = d_ymix.shape[0]
    tr = min(tr, s)
    n = s // tr

    def body(do_ref, o_ref, q_ref, l_ref, dop_ref, qpp_ref):
        dov = _head_half(do_ref, pl.program_id(0))
        lane = lax.broadcasted_iota(jnp.int32, dov.shape, 1)
        dh, dm, dl = _split3(jnp.sum(dov * o_ref[...], axis=-1, keepdims=True))
        lh, lm, ll = _split3(l_ref[...])
        dop_ref[...] = _put_cols(dov, lane, _COL_DELTA, (-dh, -dm, -dl)).astype(BF16)
        qpp_ref[...] = _put_cols(q_ref[...].astype(F32), lane, _COL_LSE, (-lh, -lm, -ll)).astype(BF16)

    rows = pl.BlockSpec((tr, ATT_W), lambda h, i: (h * n + i, 0))
    return pl.pallas_call(
        body, grid=(B_HEADS, n),
        in_specs=[pl.BlockSpec((tr, ATT_W), lambda h, i: (i, _O_BLOCK + (h >> 1))), rows, rows,
                  pl.BlockSpec((tr, 1), lambda h, i: (h * n + i, 0))],
        out_specs=[rows] * 2, out_shape=[jax.ShapeDtypeStruct((B_HEADS * s, ATT_W), BF16)] * 2,
        name="attn_bwd_prep", compiler_params=_params(("parallel", "parallel")),
    )(d_ymix, o, qp, lse)


def _attn_merge_out(o, ymix, *, tr=1024):
    s = o.shape[1]
    tr = min(tr, s)

    def body(a_ref, b_ref, buf_ref, out_ref):
        out_ref[...] = _pair(a_ref[0], b_ref[0]).astype(BF16)

    return pl.pallas_call(
        body, grid=(_HEAD_PAIRS, s // tr),
        in_specs=[pl.BlockSpec((1, tr, ATT_W), lambda j, i: (2 * j, i, 0)),
                  pl.BlockSpec((1, tr, ATT_W), lambda j, i: (2 * j + 1, i, 0)), _ANY],
        out_specs=pl.BlockSpec((tr, ATT_W), lambda j, i: (i, _O_BLOCK + j)),
        out_shape=jax.ShapeDtypeStruct(ymix.shape, ymix.dtype), input_output_aliases={2: 0}, name="attn_merge_out",
        compiler_params=_params(("parallel", "parallel")),
    )(o, o, ymix)


def _pair_bwd(d_heads, dz, out_block, *, z=None, in_block=None, gain=None, scale=None, name, tr=1024):
    s = d_heads.shape[1]
    tr = min(tr, s)
    norm = gain is not None

    def body(*refs):
        a_ref, b_ref = refs[0], refs[1]
        d = _pair(a_ref[0], b_ref[0])
        if norm:
            x_ref, g_ref, _, out_ref, dg_ref = refs[2:]

            @pl.when((pl.program_id(0) == 0) & (pl.program_id(1) == 0))
            def _():
                dg_ref[...] = jnp.zeros_like(dg_ref)

            x = x_ref[...]
            lo = lax.broadcasted_iota(jnp.int32, x.shape, 1) < B_HEAD_DIM

            def half_mean(v):
                s_lo = jnp.sum(jnp.where(lo, v, 0.0), axis=-1, keepdims=True)
                s_hi = jnp.sum(jnp.where(lo, 0.0, v), axis=-1, keepdims=True)
                return jnp.where(lo, s_lo, s_hi) * (1.0 / B_HEAD_DIM)

            r = lax.rsqrt(half_mean(x * x) + EPS)
            xh = x * r
            dy = d * scale if scale is not None else d
            dyg = dy * g_ref[...]
            d = r * (dyg - xh * half_mean(dyg * xh))
            dg_ref[...] += jnp.sum(dy * xh, axis=0, keepdims=True)
        else:
            out_ref = refs[3]
        out_ref[...] = d.astype(BF16)

    heads = [pl.BlockSpec((1, tr, ATT_W), lambda j, i: (2 * j, i, 0)),
             pl.BlockSpec((1, tr, ATT_W), lambda j, i: (2 * j + 1, i, 0))]
    out_spec = pl.BlockSpec((tr, ATT_W), lambda j, i: (i, out_block + j))
    dz_shape = jax.ShapeDtypeStruct(dz.shape, dz.dtype)
    if norm:
        res = pl.pallas_call(
            body, grid=(_HEAD_PAIRS, s // tr),
            in_specs=heads + [pl.BlockSpec((tr, ATT_W), lambda j, i: (i, in_block + j)),
                              pl.BlockSpec((1, ATT_W), lambda j, i: (0, 0)), _ANY],
            out_specs=[out_spec, pl.BlockSpec((1, ATT_W), lambda j, i: (0, 0))],
            out_shape=[dz_shape, jax.ShapeDtypeStruct((1, ATT_W), F32)], input_output_aliases={4: 0}, name=name,
            compiler_params=_params(("arbitrary", "arbitrary")),
        )(d_heads, d_heads, z, gain, dz)
        return res[0], res[1]
    return pl.pallas_call(
        body, grid=(_HEAD_PAIRS, s // tr), in_specs=heads + [_ANY], out_specs=out_spec, out_shape=dz_shape,
        input_output_aliases={2: 0}, name=name, compiler_params=_params(("parallel", "parallel")),
    )(d_heads, d_heads, dz), None


def _attn_fwd(qp, kp, vp, *, blk=_ATT_BLK):
    h, s, d = qp.shape
    b = min(blk, s)
    n = s // b
    sub = min(_ATT_SUB, b)
    ns = b // sub

    def body(q_ref, k_ref, v_ref, o_ref, lse_ref):
        i = pl.program_id(1)
        qs = [q_ref[0, t * sub:(t + 1) * sub, :] for t in range(ns)]

        def tile(j, state, masked):
            off = pl.multiple_of(j * b, b)
            kk = k_ref[0, pl.ds(off, b), :]
            vv = v_ref[0, pl.ds(off, b), :]
            new = []
            for t in range(ns):
                m, l, acc = state[3 * t:3 * t + 3]
                nk = (t + 1) * sub if masked else b
                sc = lax.dot_general(qs[t], kk[:nk], _NT, preferred_element_type=F32)
                if masked:
                    sc = jnp.where(_causal((sub, nk), t * sub, 0, False), sc, NEG_INF)
                m_new = jnp.maximum(m, jnp.max(sc, axis=1, keepdims=True))
                alpha = jnp.exp(m - m_new)
                p = jnp.exp(sc - m_new)
                new += [m_new, alpha * l + jnp.sum(p, axis=1, keepdims=True),
                        alpha * acc + jnp.dot(p.astype(BF16), vv[:nk], preferred_element_type=F32)]
            return tuple(new)

        init = (jnp.full((sub, 1), NEG_INF, F32), jnp.zeros((sub, 1), F32), jnp.zeros((sub, d), F32)) * ns
        state = lax.fori_loop(0, i, lambda j, st: tile(j, st, False), init)
        state = tile(i, state, True)
        for t in range(ns):
            m, l, acc = state[3 * t:3 * t + 3]
            o_ref[0, t * sub:(t + 1) * sub, :] = acc / l
            lse_ref[0, t * sub:(t + 1) * sub, :] = m + jnp.log(l)

    blk_spec = pl.BlockSpec((1, b, d), lambda hh, i: (hh, i, 0))
    full = pl.BlockSpec((1, s, d), lambda hh, i: (hh, 0, 0))
    return pl.pallas_call(
        body, grid=(h, n), in_specs=[blk_spec, full, full],
        out_specs=[blk_spec, pl.BlockSpec((1, b, 1), lambda hh, i: (hh, i, 0))],
        out_shape=[jax.ShapeDtypeStruct((h, s, d), F32), jax.ShapeDtypeStruct((h, s, 1), F32)],
        name="attn_fwd", compiler_params=_params(("parallel", "arbitrary")),
    )(qp, kp, vp)


_ATT_BWD_BLK = 1024


def _attn_bwd(qpp, kp, vp, dop, *, blk=_ATT_BWD_BLK):
    h, s, d = qpp.shape
    b = min(blk, s)
    n = s // b
    sub = min(_ATT_SUB, b)
    ns = b // sub

    def body(k_ref, v_ref, q_ref, do_ref, dq_ref, dk_ref, dv_ref):
        j = pl.program_id(1)

        @pl.when(j == 0)
        def _():
            dq_ref[...] = jnp.zeros_like(dq_ref)

        ks = [k_ref[0, t * sub:(t + 1) * sub, :] for t in range(ns)]
        vs = [v_ref[0, t * sub:(t + 1) * sub, :] for t in range(ns)]

        def tile(i, state, masked):
            off = pl.multiple_of(i * b, b)
            qq = q_ref[0, pl.ds(off, b), :]
            dd = do_ref[0, pl.ds(off, b), :]
            new = []
            dq = None
            for t in range(ns):
                q0 = t * sub if masked else 0
                qt, dt = qq[q0:], dd[q0:]
                pt = jnp.exp(lax.dot_general(ks[t], qt, _NT, preferred_element_type=F32))
                if masked:
                    pt = jnp.where(_causal((sub, b - q0), q0, q0, True), pt, 0.0)
                dst = (pt * lax.dot_general(vs[t], dt, _NT, preferred_element_type=F32)).astype(BF16)
                new += [state[2 * t] + jnp.dot(dst, qt, preferred_element_type=F32),
                        state[2 * t + 1] + jnp.dot(pt.astype(BF16), dt, preferred_element_type=F32)]
                part = lax.dot_general(dst, ks[t], _TN, preferred_element_type=F32)
                if q0:
                    part = jnp.concatenate([jnp.zeros((q0, d), F32), part], axis=0)
                dq = part if dq is None else dq + part
            dq_ref[0, pl.ds(off, b), :] += dq
            return tuple(new)

        state = tile(j, (jnp.zeros((sub, d), F32),) * (2 * ns), True)
        state = lax.fori_loop(j + 1, n, lambda i, st: tile(i, st, False), state)
        for t in range(ns):
            dk_ref[0, t * sub:(t + 1) * sub, :] = state[2 * t]
            dv_ref[0, t * sub:(t + 1) * sub, :] = state[2 * t + 1]

    blk_spec = pl.BlockSpec((1, b, d), lambda hh, j: (hh, j, 0))
    full = pl.BlockSpec((1, s, d), lambda hh, j: (hh, 0, 0))
    return pl.pallas_call(
        body, grid=(h, n), in_specs=[blk_spec, blk_spec, full, full], out_specs=[full, blk_spec, blk_spec],
        out_shape=[jax.ShapeDtypeStruct((h, s, d), F32)] * 3, name="attn_bwd",
        compiler_params=_params(("parallel", "arbitrary")),
    )(kp, vp, qpp, dop)


def _pad_head(g):
    return jnp.pad(g, ((0, 0), (0, ATT_W - B_HEAD_DIM)))


def _even_fwd(x0, w, e, i):
    s = x0.shape[0]
    hs = B_HEADS * s
    hb = _rms_fwd(x0, w["norm_mix"][i:i + 1], name="mix_norm")
    z = _mm(hb, w["ev_w_in"][e], tb=True, name="ev_in", bn=896)
    b_s = w["ev_b_spatial"][e][:, :, None]
    v_gain = w["ev_v_norm"][e:e + 1]
    ymix = _gmlp_fwd(z, w["ev_w_spatial"][e], b_s, v_gain)
    f_t = z[:, IN_COLS - B_HEADS:IN_COLS].T
    c = _fgate_fwd(f_t, w["ev_b_fgate"][e][:, None])
    qp, kp, vp = _attn_prep(z, c.reshape(hs, 1), _pad_head(w["ev_q_norm"][e:e + 1]),
                            _pad_head(w["ev_k_norm"][e:e + 1]))
    shp = (B_HEADS, s, ATT_W)
    o, lse = _attn_fwd(qp.reshape(shp), kp.reshape(shp), vp.reshape(shp))
    ymix = _attn_merge_out(o, ymix)
    x1 = _mm(ymix, w["ev_w_out"][e], add=x0, name="ev_out")
    return x1, (hb, z, ymix, qp, kp, vp, f_t, o, lse)


def _even_bwd(dx1, dx1b, x0, saved, w, e, i):
    hb, z, ymix, qp, kp, vp, f_t, o, lse = saved
    s = x0.shape[0]
    hs = B_HEADS * s
    d_ymix = _mm(dx1b, w["ev_w_out"][e], tb=True, name="ev_out_dx")
    d_wout = _mm(ymix, dx1b, ta=True, name="ev_out_dw")
    dop, qpp = _attn_bwd_prep(d_ymix, o.reshape(hs, ATT_W), qp, lse.reshape(hs, 1))
    shp = (B_HEADS, s, ATT_W)
    qpp, dop, kp3, vp3 = qpp.reshape(shp), dop.reshape(shp), kp.reshape(shp), vp.reshape(shp)
    dqp, dkp, dvp = _attn_bwd(qpp, kp3, vp3, dop)
    d_ft, d_bf = _fgate_bwd(f_t, w["ev_b_fgate"][e][:, None], dqp[:, :, _COL_CQ], -dkp[:, :, _COL_CK])
    dz = jnp.pad(d_ft.T.astype(BF16), ((0, 0), (_F_BLOCK * ATT_W, IN_COLS_PAD - IN_COLS)))
    b_s = w["ev_b_spatial"][e][:, :, None]
    v_gain = w["ev_v_norm"][e:e + 1]
    dz, d_ws, d_bs, d_vg = _gmlp_bwd(z, d_ymix, w["ev_w_spatial"][e], b_s, v_gain, dz)
    twice = lambda g: jnp.concatenate([g, g], axis=1)
    dz, d_gq = _pair_bwd(dqp, dz, _Q_BLOCK, z=z, in_block=_Q_BLOCK, gain=twice(w["ev_q_norm"][e:e + 1]),
                         scale=_ATT_SCALE, name="attn_qnorm_bwd")
    dz, d_gk = _pair_bwd(dkp, dz, _K_BLOCK, z=z, in_block=_K_BLOCK, gain=twice(w["ev_k_norm"][e:e + 1]),
                         name="attn_knorm_bwd")
    dz, _ = _pair_bwd(dvp, dz, _V_BLOCK, name="attn_dv_out")
    dx0, d_nm = _mm_rms_bwd(dz, w["ev_w_in"][e], x0, w["norm_mix"][i:i + 1], dx1, name="ev_in_dx")
    d_win_t = _mm(hb, dz, ta=True, name="ev_in_dw", bn=1344)[:, :IN_COLS].T
    hd = B_HEAD_DIM
    grads = {"ev_w_in": d_win_t, "ev_w_out": d_wout, "ev_b_fgate": d_bf[:, 0],
             "ev_q_norm": d_gq[0, :hd] + d_gq[0, hd:], "ev_k_norm": d_gk[0, :hd] + d_gk[0, hd:], "ev_v_norm": d_vg[0],
             "ev_w_spatial": d_ws, "ev_b_spatial": d_bs[:, :, 0], "norm_mix": d_nm[0]}
    return dx0, grads


def _s5_disc(a_re, a_im, log_dt, b_re, b_im):
    dt = jnp.exp(log_dt)[:, None]
    lr, li = a_re, a_im
    mag = jnp.exp(lr * dt)
    ab_re, ab_im = mag * jnp.cos(li * dt), mag * jnp.sin(li * dt)
    den = lr * lr + li * li
    nr, ni = ab_re - 1.0, ab_im
    cr = (nr * lr + ni * li) / den
    ci = (ni * lr - nr * li) / den
    bb_re = cr[..., None] * b_re - ci[..., None] * b_im
    bb_im = cr[..., None] * b_im + ci[..., None] * b_re
    return ab_re, ab_im, bb_re, bb_im


_GPB = S5_GROUPS // S5_BLOCKS


def _blockdiag(t):
    a, b = t.shape[1:]
    eye = jnp.eye(_GPB, dtype=t.dtype)
    t = t.reshape(S5_BLOCKS, _GPB, a, 1, b) * eye[None, :, None, :, None]
    return t.reshape(S5_BLOCKS, _GPB * a, _GPB * b)


def _blockdiag_extract(m, a, b):
    eye = jnp.eye(_GPB, dtype=m.dtype)
    m = m.reshape(S5_BLOCKS, _GPB, a, _GPB, b) * eye[None, :, None, :, None]
    return jnp.sum(m, axis=3).reshape(S5_GROUPS, a, b)


_SCAN_ROWS = 8


def _cmul(a_r, a_i, b_r, b_i):
    return a_r * b_r - a_i * b_i, a_r * b_i + a_i * b_r


def _scan_tiles(yr_ref, yi_ref, a_r, a_i, c_r, c_i, *, rev, x_refs=None):
    tb, wl = yr_ref.shape
    ntile = tb // _SCAN_ROWS
    with_acc = x_refs is not None
    if rev:
        a_i = -a_i
    pw = [(a_r, a_i)]
    for _ in range(_SCAN_ROWS - 1):
        pw.append(_cmul(*pw[-1], a_r, a_i))
    sub = lax.broadcasted_iota(jnp.int32, (_SCAN_ROWS, wl), 0)
    dist = (_SCAN_ROWS - 1 - sub) if rev else sub
    zero = jnp.zeros((_SCAN_ROWS, wl), F32)
    steps = []
    for kk in (1, 2, 4):
        steps.append(((_SCAN_ROWS - kk) if rev else kk, jnp.where(dist >= kk, pw[kk - 1][0], zero),
                      jnp.where(dist >= kk, pw[kk - 1][1], zero)))
    e_r, e_i = zero, zero
    for d in range(_SCAN_ROWS):
        e_r = jnp.where(dist == d, pw[d][0], e_r)
        e_i = jnp.where(dist == d, pw[d][1], e_i)
    exit_row = 0 if rev else _SCAN_ROWS - 1

    def tile(q, carry):
        idx = (ntile - 1 - q) if rev else q
        rows = pl.ds(pl.multiple_of(idx * _SCAN_ROWS, _SCAN_ROWS), _SCAN_ROWS)
        y_r, y_i = yr_ref[rows, :], yi_ref[rows, :]
        for sh, k_r, k_i in steps:
            t_r, t_i = _cmul(k_r, k_i, pltpu.roll(y_r, sh, 0), pltpu.roll(y_i, sh, 0))
            y_r, y_i = y_r + t_r, y_i + t_i
        cb_r = jnp.broadcast_to(carry[0], (_SCAN_ROWS, wl))
        cb_i = jnp.broadcast_to(carry[1], (_SCAN_ROWS, wl))
        t_r, t_i = _cmul(e_r, e_i, cb_r, cb_i)
        y_r, y_i = y_r + t_r, y_i + t_i
        yr_ref[rows, :] = y_r
        yi_ref[rows, :] = y_i
        out = (y_r[exit_row:exit_row + 1], y_i[exit_row:exit_row + 1])
        if with_acc:
            n_r = jnp.where(sub == _SCAN_ROWS - 1, cb_r, pltpu.roll(y_r, _SCAN_ROWS - 1, 0))
            n_i = jnp.where(sub == _SCAN_ROWS - 1, cb_i, pltpu.roll(y_i, _SCAN_ROWS - 1, 0))
            s_r, s_i = x_refs[0][rows, :], x_refs[1][rows, :]
            out += (carry[2] + (n_r * s_r + n_i * s_i), carry[3] + (n_i * s_r - n_r * s_i))
        return out

    return lax.fori_loop(0, ntile, tile, (c_r, c_i) + ((zero, zero) if with_acc else ()), unroll=2)


_S5_WL = S5_N // S5_BLOCKS
_S5_CW = D_MODEL // S5_BLOCKS
_TN = (((0,), (0,)), ((), ()))


def _s5_fwd(hb, mats, *, tb=1024):
    s = hb.shape[0]
    tb = min(tb, s)

    def body(h_ref, wbr_ref, wbi_ref, wcr_ref, wci_ref, ar_ref, ai_ref, xr_ref, xi_ref, cp_ref, cr, ci):
        @pl.when(pl.program_id(1) == 0)
        def _():
            cr[...] = jnp.zeros_like(cr)
            ci[...] = jnp.zeros_like(ci)

        hv = h_ref[...]
        xr_ref[...] = jnp.dot(hv, wbr_ref[0], preferred_element_type=F32)
        xi_ref[...] = jnp.dot(hv, wbi_ref[0], preferred_element_type=F32)
        cr[...], ci[...] = _scan_tiles(xr_ref, xi_ref, ar_ref[...], ai_ref[...], cr[...], ci[...], rev=False)
        cp_ref[...] = (jnp.dot(xr_ref[...].astype(BF16), wcr_ref[0], preferred_element_type=F32)
                       + jnp.dot(xi_ref[...].astype(BF16), wci_ref[0], preferred_element_type=F32))

    wide = pl.BlockSpec((tb, _S5_WL), lambda c, t: (t, c))
    narrow = pl.BlockSpec((tb, _S5_CW), lambda c, t: (t, c))
    w_in = pl.BlockSpec((1, _S5_CW, _S5_WL), lambda c, t: (c, 0, 0))
    w_out = pl.BlockSpec((1, _S5_WL, _S5_CW), lambda c, t: (c, 0, 0))
    vec = pl.BlockSpec((1, _S5_WL), lambda c, t: (0, c))
    return pl.pallas_call(
        body, grid=(S5_BLOCKS, s // tb), in_specs=[narrow, w_in, w_in, w_out, w_out, vec, vec],
        out_specs=[wide, wide, narrow],
        out_shape=[jax.ShapeDtypeStruct((s, S5_N), F32)] * 2 + [jax.ShapeDtypeStruct((s, D_MODEL), F32)],
        scratch_shapes=[pltpu.VMEM((1, _S5_WL), F32)] * 2, name="s5_fwd",
        compiler_params=_params(("parallel", "arbitrary")),
    )(hb, mats["wb_re"], mats["wb_im"], mats["wc_re"], mats["wc_imn"], mats["ar"], mats["ai"])


def _s5_bwd(dyb, hb, xr, xi, dhd, mats, *, tb=1024):
    s = hb.shape[0]
    tb = min(tb, s)
    nt = s // tb

    def body(dy_ref, h_ref, xr_ref, xi_ref, dhd_ref, wctr_ref, wcti_ref, wbtr_ref, wbti_ref, ar_ref, ai_ref,
             dh_ref, dar_ref, dai_ref, dwbr_ref, dwbi_ref, dwcr_ref, dwci_ref, lr, li, cr, ci):
        @pl.when(pl.program_id(1) == 0)
        def _():
            for r in (cr, ci, dar_ref, dai_ref, dwbr_ref, dwbi_ref, dwcr_ref, dwci_ref):
                r[...] = jnp.zeros_like(r)

        dyv, hv = dy_ref[...], h_ref[...]
        lr[...] = jnp.dot(dyv, wctr_ref[0], preferred_element_type=F32)
        li[...] = jnp.dot(dyv, wcti_ref[0], preferred_element_type=F32)
        cr[...], ci[...], acc_r, acc_i = _scan_tiles(lr, li, ar_ref[...], ai_ref[...], cr[...], ci[...], rev=True,
                                                     x_refs=(xr_ref, xi_ref))
        dar_ref[...] += jnp.sum(acc_r, axis=0, keepdims=True)
        dai_ref[...] += jnp.sum(acc_i, axis=0, keepdims=True)
        lrb, lib = lr[...].astype(BF16), li[...].astype(BF16)
        dh_ref[...] = (jnp.dot(lrb, wbtr_ref[0], preferred_element_type=F32) + dhd_ref[...]
                       + jnp.dot(lib, wbti_ref[0], preferred_element_type=F32))
        dwbr_ref[0] += lax.dot_general(hv, lrb, _TN, preferred_element_type=F32)
        dwbi_ref[0] += lax.dot_general(hv, lib, _TN, preferred_element_type=F32)
        dwcr_ref[0] += lax.dot_general(dyv, xr_ref[...].astype(BF16), _TN, preferred_element_type=F32)
        dwci_ref[0] += lax.dot_general(dyv, xi_ref[...].astype(BF16), _TN, preferred_element_type=F32)

    wide = pl.BlockSpec((tb, _S5_WL), lambda c, t: (nt - 1 - t, c))
    narrow = pl.BlockSpec((tb, _S5_CW), lambda c, t: (nt - 1 - t, c))
    w_in = pl.BlockSpec((1, _S5_CW, _S5_WL), lambda c, t: (c, 0, 0))
    w_out = pl.BlockSpec((1, _S5_WL, _S5_CW), lambda c, t: (c, 0, 0))
    vec = pl.BlockSpec((1, _S5_WL), lambda c, t: (0, c))
    acc_shape = jax.ShapeDtypeStruct((S5_BLOCKS, _S5_CW, _S5_WL), F32)
    return pl.pallas_call(
        body, grid=(S5_BLOCKS, nt), in_specs=[narrow, narrow, wide, wide, narrow, w_in, w_in, w_out, w_out, vec, vec],
        out_specs=[narrow, vec, vec, w_in, w_in, w_in, w_in],
        out_shape=[jax.ShapeDtypeStruct((s, D_MODEL), F32)] + [jax.ShapeDtypeStruct((1, S5_N), F32)] * 2
        + [acc_shape] * 4,
        scratch_shapes=[pltpu.VMEM((tb, _S5_WL), F32)] * 2 + [pltpu.VMEM((1, _S5_WL), F32)] * 2, name="s5_bwd",
        compiler_params=_params(("parallel", "arbitrary")),
    )(dyb, hb, xr, xi, dhd, mats["wct_re"], mats["wct_imn"], mats["wbt_re"], mats["wbt_im"], mats["ar"], mats["ai"])


def _s5_mats(w, o):
    ab_re, ab_im, bb_re, bb_im = _s5_disc(w["od_a_re"][o], w["od_a_im"][o], w["od_log_dt"][o], w["od_b_re"][o],
                                          w["od_b_im"][o])
    c_re, c_im = w["od_c_re"][o], w["od_c_im"][o]
    tr = lambda t: t.transpose(0, 2, 1)
    bd = lambda t: _blockdiag(t).astype(BF16)
    return {
        "ar": ab_re.reshape(1, S5_N), "ai": ab_im.reshape(1, S5_N),
        "wb_re": bd(tr(bb_re)), "wb_im": bd(tr(bb_im)), "wc_re": bd(tr(c_re)), "wc_imn": bd(-tr(c_im)),
        "wct_re": bd(c_re), "wct_imn": bd(-c_im), "wbt_re": bd(bb_re), "wbt_im": bd(bb_im),
    }


def _odd_fwd(x0, w, o, i):
    mats = _s5_mats(w, o)
    nm = w["norm_mix"][i:i + 1]
    d_row = w["od_d"][o:o + 1]
    hb = _rms_fwd(x0, nm, name="mix_norm")
    xr, xi, cp = _s5_fwd(hb, mats)

    def fn(xv, cpv, gv, dv):
        y = cpv + dv * (xv * _rstd(xv) * gv)
        return y, _gelu(y)

    y, gy = _rw(fn, [x0, cp], [nm, d_row], [(D_MODEL, F32), (D_MODEL, BF16)], tr=512, name="s5_out")
    gg = _mm(gy, w["od_w_glu"][o], tb=True, name="od_glu")
    x1 = _rw(lambda xv, g: xv + g[:, :D_MODEL] * _sigmoid(g[:, D_MODEL:]), [x0, gg], [], [(D_MODEL, F32)], tr=512,
             name="od_glu_out")[0]
    return x1, (hb, xr, xi, y, gy, gg)


def _odd_bwd(dx1, x0, saved, w, o, i):
    hb, xr, xi, y, gy, gg = saved
    mats = _s5_mats(w, o)
    nm = w["norm_mix"][i:i + 1]
    d_row = w["od_d"][o:o + 1]

    def fn_glu(dv, g):
        ga, gb = g[:, :D_MODEL], g[:, D_MODEL:]
        sg = _sigmoid(gb)
        return jnp.concatenate([dv * sg, dv * ga * (sg * (1.0 - sg))], axis=1)

    dgg = _rw(fn_glu, [dx1, gg], [], [(2 * D_MODEL, BF16)], tr=512, name="od_glu_out_bwd")[0]
    d_wglu = _mm(gy, dgg, ta=True, name="od_glu_dw", bn=1024).T
    d_gy = _mm(dgg, w["od_w_glu"][o], name="od_glu_dx")

    def fn_y(dg, yv, xv, gv, dv):
        dy = dg * _gelu_grad(yv)
        h = xv * _rstd(xv) * gv
        return dy, dv * dy, jnp.sum(dy * h, axis=0, keepdims=True)

    dyb, dhd, d_d = _rw(fn_y, [d_gy, y, x0], [nm, d_row], [(D_MODEL, BF16), (D_MODEL, F32)], [(1, D_MODEL)], tr=512,
                        name="s5_out_bwd")
    dh, d_ar, d_ai, d_wb_re, d_wb_im, d_wc_re, d_wc_im = _s5_bwd(dyb, hb, xr, xi, dhd, mats)
    gc, gp = S5_GROUP_CH, S5_STATE
    d_c_re = _blockdiag_extract(d_wc_re, gc, gp)
    d_c_im = -_blockdiag_extract(d_wc_im, gc, gp)
    d_bb_re = _blockdiag_extract(d_wb_re, gc, gp).transpose(0, 2, 1)
    d_bb_im = _blockdiag_extract(d_wb_im, gc, gp).transpose(0, 2, 1)
    dx0, d_nm = _rms_bwd_call(x0, dh, nm, dx1, name="mix_norm_bwd")
    _, vjp = jax.vjp(_s5_disc, w["od_a_re"][o], w["od_a_im"][o], w["od_log_dt"][o], w["od_b_re"][o], w["od_b_im"][o])
    d_a_re, d_a_im, d_log_dt, d_b_re, d_b_im = vjp(
        (d_ar.reshape(S5_GROUPS, S5_STATE), d_ai.reshape(S5_GROUPS, S5_STATE), d_bb_re, d_bb_im))
    grads = {"od_a_re": d_a_re, "od_a_im": d_a_im, "od_log_dt": d_log_dt, "od_b_re": d_b_re, "od_b_im": d_b_im,
             "od_c_re": d_c_re, "od_c_im": d_c_im, "od_d": d_d[0], "od_w_glu": d_wglu, "norm_mix": d_nm[0]}
    return dx0, grads


_HBM = pl.BlockSpec(memory_space=pltpu.HBM)


def _mesh_pos():
    return lax.axis_index("x"), lax.axis_index("y"), lax.axis_index("c")


def _slot(px, py, pc):
    return 4 * px + 2 * py + pc


def _all_gather(x, *, name):
    def body(x_ref, out_ref, send_sems, recv_sems, local_sem):
        mx, my, mc = _mesh_pos()
        me, sibling = (mx, my, mc), (mx, my, 1 - mc)
        chips = [(1 - mx, my), (mx, 1 - my), (1 - mx, 1 - my)]

        def copy(k, block, to, src=None):
            dst = out_ref.at[_slot(*block)]
            return pltpu.make_async_remote_copy(
                src_ref=dst if src is None else src, dst_ref=dst, send_sem=send_sems.at[k], recv_sem=recv_sems.at[k],
                device_id=to, device_id_type=MESH)

        mine = pltpu.make_async_copy(x_ref, out_ref.at[_slot(*me)], local_sem)
        mine.start()
        first = [copy(0, me, sibling, src=x_ref)]
        first += [copy(1 + j, me, (*chip, mc), src=x_ref) for j, chip in enumerate(chips)]
        for cp in first:
            cp.start()
        passed = [copy(4 + j, (*chip, mc), sibling) for j, chip in enumerate(chips)]
        for j, chip in enumerate(chips):
            copy(1 + j, (*chip, mc), me).wait_recv()
            passed[j].start()
        copy(0, sibling, me).wait_recv()
        for j, chip in enumerate(chips):
            copy(4 + j, (*chip, 1 - mc), me).wait_recv()
        for cp in first + passed:
            cp.wait_send()
        mine.wait()

    return pl.pallas_call(
        body, out_shape=jax.ShapeDtypeStruct((N_DEV,) + x.shape, x.dtype), in_specs=[_HBM], out_specs=_HBM,
        scratch_shapes=[pltpu.SemaphoreType.DMA((7,)), pltpu.SemaphoreType.DMA((7,)), pltpu.SemaphoreType.DMA],
        name=name,
    )(x)


_SEMS = pltpu.SemaphoreType.DMA((N_DEV - 1,))
_SEM_SPEC = pl.BlockSpec(memory_space=pltpu.SEMAPHORE)
_SPLIT_COPY = pltpu.CompilerParams(has_side_effects=pltpu.SideEffectType.DATAFLOW_SIDE_EFFECTING)


def _exchange_copies(x_refs, land_refs, send_sems, recv_sems, gather):
    mx, my, mc = _mesh_pos()
    my_slot = _slot(mx, my, mc)
    copies = []
    for a, (x_ref, land_ref) in enumerate(zip(x_refs, land_refs)):
        for k in range(1, N_DEV):
            peer = (1 - mx if k & 4 else mx, 1 - my if k & 2 else my, 1 - mc if k & 1 else mc)
            copies.append(pltpu.make_async_remote_copy(
                src_ref=x_ref if gather else x_ref.at[_slot(*peer)], dst_ref=land_ref.at[my_slot],
                send_sem=send_sems[a].at[k - 1], recv_sem=recv_sems[a].at[k - 1], device_id=peer, device_id_type=MESH))
    return copies


def _exchange_start(xs, carry, *, gather, name):
    n = len(xs)

    def body(*refs):
        outs = refs[2 * n + 1:]
        for cp in _exchange_copies(refs[:n], refs[n:2 * n], outs[:n], outs[n:2 * n], gather):
            cp.start()

    hbm = lambda a: pltpu.HBM(a.shape, a.dtype)
    lands = [pltpu.with_memory_space_constraint(lax.empty(((N_DEV,) if gather else ()) + x.shape, x.dtype), pltpu.HBM)
             for x in xs]
    ins = [pltpu.with_memory_space_constraint(a, pltpu.HBM) for a in list(xs) + [carry]]
    res = pl.pallas_call(
        body, name=name, in_specs=[_HBM] * (2 * n + 1), out_specs=[_SEM_SPEC] * (2 * n) + [_HBM] * (2 * n + 1),
        out_shape=[_SEMS] * (2 * n) + [hbm(a) for a in list(xs) + lands] + [hbm(carry)],
        input_output_aliases={**{a: 2 * n + a for a in range(2 * n)}, 2 * n: 4 * n}, compiler_params=_SPLIT_COPY,
    )(*ins[:n], *lands, ins[n])
    return (res[:n], res[n:2 * n], res[2 * n:3 * n], res[3 * n:4 * n]), res[4 * n]


def _exchange_wait(handles, after, *, gather, name):
    send_sems, recv_sems, xs, lands = handles
    n = len(xs)

    def body(*refs):
        for cp in _exchange_copies(refs[:n], refs[n:2 * n], refs[2 * n:3 * n], refs[3 * n:4 * n], gather):
            cp.wait_send()
            cp.wait_recv()

    hbm = lambda a: pltpu.HBM(a.shape, a.dtype)
    res = pl.pallas_call(
        body, name=name, in_specs=[_HBM] * (2 * n) + [_SEM_SPEC] * (2 * n) + [_ANY], out_specs=[_HBM] * (2 * n),
        out_shape=[hbm(a) for a in list(xs) + list(lands)], input_output_aliases={a: a for a in range(2 * n)},
        compiler_params=_SPLIT_COPY,
    )(*xs, *lands, *send_sems, *recv_sems, after)
    return res[:n], res[n:]


_ADAMW_PARTS_BLOCK_BYTES = 8 * 1024 * 1024


def _adamw(parts, w, m, v, *, name, own=None):
    r, c = w.shape
    tr = r
    for cand in range(8, r + 1, 8):
        if r % cand == 0 and N_DEV * cand * c * 4 <= _ADAMW_PARTS_BLOCK_BYTES:
            tr = cand
    has_own = own is not None

    def body(*refs):
        p_ref, w_ref, m_ref, v_ref = refs[0], refs[1 + has_own], refs[2 + has_own], refs[3 + has_own]
        g_ref, d_ref, nm_ref, nv_ref = refs[4 + has_own:]
        if has_own:
            me = _slot(*_mesh_pos())
            part = lambda k: jnp.where(me == k, refs[1][...], p_ref[k])
        else:
            part = lambda k: p_ref[k]
        g = part(0)
        for k in range(1, N_DEV):
            g = g + part(k)
        nm = ADAM_B1 * m_ref[...] + (1.0 - ADAM_B1) * g
        nv = ADAM_B2 * v_ref[...] + (1.0 - ADAM_B2) * (g * g)
        m_hat = nm / (1.0 - ADAM_B1 ** ADAM_STEP)
        v_hat = nv / (1.0 - ADAM_B2 ** ADAM_STEP)
        g_ref[...] = g
        d_ref[...] = -ADAM_LR * (m_hat / (jnp.sqrt(v_hat) + ADAM_EPS) + ADAM_WD * w_ref[...])
        nm_ref[...] = nm
        nv_ref[...] = nv

    blk = pl.BlockSpec((tr, c), lambda i: (i, 0))
    return pl.pallas_call(
        body, grid=(r // tr,),
        in_specs=[pl.BlockSpec((N_DEV, tr, c), lambda i: (0, i, 0))] + [blk] * (3 + has_own),
        out_specs=[blk] * 4, out_shape=[jax.ShapeDtypeStruct((r, c), F32)] * 4, name=name,
        compiler_params=_params(("parallel",)),
    )(parts, *([own] if has_own else []), w, m, v)


_FLAT_COLS = 1024
_FLAT_ROW_ALIGN = 128

_SHARD_AXIS = {
    "norm_mix": None, "norm_ffn": None, "ev_w_in": 2, "ev_b_fgate": None, "ev_q_norm": None, "ev_k_norm": None,
    "ev_v_norm": None, "ev_w_spatial": None, "ev_b_spatial": None, "ev_w_out": 1, "od_a_re": None, "od_a_im": None,
    "od_log_dt": None, "od_b_re": None, "od_b_im": None, "od_c_re": None, "od_c_im": None, "od_d": 1, "od_w_glu": 2,
    "ffn_w_up": 2, "ffn_conv_w": 2, "ffn_conv_b": None, "ffn_w_down": 1, "ple_w_proj": 2, "ple_w_gate": 1,
}
_WEIGHTS = list(_SHARD_AXIS)
_REPLICATED = [n for n in _WEIGHTS if _SHARD_AXIS[n] is None]
_COL_SHARDED = ["ev_w_in", "od_w_glu", "ffn_w_up", "ple_w_proj"]
_ROW_SHARDED = ["ev_w_out", "ffn_w_down", "ple_w_gate"]
_SMALL_SHARDED = ["od_d", "ffn_conv_w"]


def _pack(arrays, lead=()):
    nl = len(lead)
    flat = jnp.concatenate([a.reshape(lead + (-1,)) for a in arrays], axis=nl)
    n = flat.shape[nl]
    chunk = _FLAT_COLS * _FLAT_ROW_ALIGN
    total = -(-n // chunk) * chunk
    flat = jnp.pad(flat, [(0, 0)] * nl + [(0, total - n)])
    return flat.reshape(lead + (total // _FLAT_COLS, _FLAT_COLS))


def _unpack(buf, shapes, lead=()):
    nl = len(lead)
    flat = buf.reshape(lead + (-1,))
    out, off = [], 0
    for shp in shapes:
        n = math.prod(shp)
        out.append(lax.slice_in_dim(flat, off, off + n, axis=nl).reshape(lead + tuple(shp)))
        off += n
    return out


def _join_shards(g, axis):
    g = jnp.moveaxis(g, 0, axis)
    shp = g.shape
    return g.reshape(shp[:axis] + (shp[axis] * shp[axis + 1],) + shp[axis + 2:])


def _split_shards(full, axis):
    shp = full.shape
    g = full.reshape(shp[:axis] + (N_DEV, shp[axis] // N_DEV) + shp[axis + 1:])
    return jnp.moveaxis(g, axis, 0)


def _local_step(x, p, tgt, w, before_part, on_grads):
    saved = []
    h = x
    for i in range(DEPTH):
        x0 = before_part(i, 0, h)
        if i % 2 == 0:
            x1, sv_mix = _even_fwd(x0, w, i // 2, i)
        else:
            x1, sv_mix = _odd_fwd(x0, w, i // 2, i)
        x1 = before_part(i, 1, x1)
        x2, sv_ffn = _ffn_fwd(x1, w, i)
        x3, sv_ple = _ple_fwd(x2, p[i], w, i)
        saved.append((x0, x1, x2, sv_mix, sv_ffn, sv_ple))
        h = x3
    loss, dh = _loss_and_grad(h, tgt)
    per_layer = {}
    for i in reversed(range(DEPTH)):
        x0, x1, x2, sv_mix, sv_ffn, sv_ple = saved[i]
        dx2, dx2b, g_ple = _ple_bwd(dh, x2, p[i], sv_ple, w, i)
        dx1, dx1b, g_ffn = _ffn_bwd(dx2, dx2b, x1, sv_ffn, w, i)
        dx1 = on_grads(i, 0, {**g_ple, **g_ffn}, dx1)
        if i % 2 == 0:
            dh, g_mix = _even_bwd(dx1, dx1b, x0, sv_mix, w, i // 2, i)
        else:
            dh, g_mix = _odd_bwd(dx1, x0, sv_mix, w, i // 2, i)
        dh = on_grads(i, 1, g_mix, dh)
        layer_grads = {**g_ple, **g_ffn, **g_mix}
        for name, g in layer_grads.items():
            per_layer.setdefault(name, []).append(g)
    grads = {name: gs[::-1] for name, gs in per_layer.items()}
    return loss, dh, grads


def kernel(x, p, norm_mix, norm_ffn, ev_w_in, ev_b_fgate, ev_q_norm, ev_k_norm, ev_v_norm, ev_w_spatial, ev_b_spatial, ev_w_out, od_a_re, od_a_im, od_log_dt, od_b_re, od_b_im, od_c_re, od_c_im, od_d, od_w_glu, ffn_w_up, ffn_conv_w, ffn_conv_b, ffn_w_down, ple_w_proj, ple_w_gate, loss_target, m_norm_mix, m_norm_ffn, m_ev_w_in, m_ev_b_fgate, m_ev_q_norm, m_ev_k_norm, m_ev_v_norm, m_ev_w_spatial, m_ev_b_spatial, m_ev_w_out, m_od_a_re, m_od_a_im, m_od_log_dt, m_od_b_re, m_od_b_im, m_od_c_re, m_od_c_im, m_od_d, m_od_w_glu, m_ffn_w_up, m_ffn_conv_w, m_ffn_conv_b, m_ffn_w_down, m_ple_w_proj, m_ple_w_gate, v_norm_mix, v_norm_ffn, v_ev_w_in, v_ev_b_fgate, v_ev_q_norm, v_ev_k_norm, v_ev_v_norm, v_ev_w_spatial, v_ev_b_spatial, v_ev_w_out, v_od_a_re, v_od_a_im, v_od_log_dt, v_od_b_re, v_od_b_im, v_od_c_re, v_od_c_im, v_od_d, v_od_w_glu, v_ffn_w_up, v_ffn_conv_w, v_ffn_conv_b, v_ffn_w_down, v_ple_w_proj, v_ple_w_gate):
    args = locals()
    wl = {n: args[n] for n in _WEIGHTS}
    ml = {n: args["m_" + n] for n in _WEIGHTS}
    vl = {n: args["v_" + n] for n in _WEIGHTS}

    kinds = ("g", "d", "m", "v")

    def local(n, layer, a):
        return a[n][layer].T if n in _COL_SHARDED else a[n][layer]

    me = _slot(*_mesh_pos())
    w = {n: wl[n] for n in _REPLICATED}
    small = _all_gather(_pack([wl[n] for n in _SMALL_SHARDED]), name="gather_small_weights")
    for n, g in zip(_SMALL_SHARDED, _unpack(small, [wl[n].shape for n in _SMALL_SHARDED], lead=(N_DEV,))):
        w[n] = _join_shards(g, _SHARD_AXIS[n])

    def layer_of(n, i):
        return i if wl[n].shape[0] == DEPTH else i // 2

    def weights_of(i, part):
        if part == 0:
            return ["ev_w_in", "ev_w_out"] if i % 2 == 0 else ["od_w_glu"]
        return ["ffn_w_up", "ffn_w_down", "ple_w_proj", "ple_w_gate"]

    gathers = {}
    for n in _COL_SHARDED + _ROW_SHARDED:
        w[n] = {}

    def start_gather(i, carry):
        for part in (0, 1):
            blocks = [local(n, layer_of(n, i), wl).astype(BF16) for n in weights_of(i, part)]
            gathers[i, part], carry = _exchange_start(blocks, carry, gather=True,
                                                      name="gather_start_layer%d_part%d" % (i, part))
        return carry

    def finish_gather(i, part, h):
        sent, landed = _exchange_wait(gathers[i, part], h, gather=True, name="gather_wait_layer%d_part%d" % (i, part))
        for n, block, land in zip(weights_of(i, part), sent, landed):
            full = lax.dynamic_update_slice(land, block[None], (me, 0, 0)).reshape(N_DEV * block.shape[0], block.shape[1])
            if n == "ev_w_in":
                full = jnp.pad(full, ((0, IN_COLS_PAD - IN_COLS), (0, 0)))
            w[n][layer_of(n, i)] = full
        if (i, part) == (0, 0):
            for later in range(1, DEPTH):
                h = start_gather(later, h)
        return h

    x_in = start_gather(0, x[0])

    in_flight = {}

    def start_scatter(i, part, part_grads, dh):
        names = [n for n in _COL_SHARDED + _ROW_SHARDED if n in part_grads]
        xs = [part_grads[n].reshape(N_DEV, part_grads[n].shape[0] // N_DEV, part_grads[n].shape[1]) for n in names]
        handles, dh = _exchange_start(xs, dh, gather=False, name="scatter_start_layer%d_part%d" % (i, part))
        in_flight[i, part] = (names, handles)
        for n in rep_early:
            if n in part_grads:
                s5_grads.setdefault(n, {})[i // 2] = part_grads[n]
        if (i, part) == (1, 1):
            send = _pack([jnp.stack([s5_grads[n][o] for o in sorted(s5_grads[n])]) for n in rep_early])
            s5_handles[0], dh = _exchange_start([send], dh, gather=True, name="gather_start_s5_grads")
        return dh

    rep_early = [n for n in _REPLICATED if n.startswith("od_")]
    rep_late = [n for n in _REPLICATED if not n.startswith("od_")]
    s5_grads, s5_handles = {}, [None]
    loss, grad_x, grads = _local_step(x_in, p[:, 0], loss_target[0], w, finish_gather, start_scatter)
    loss = lax.psum(loss, ("x", "y", "c"))

    rep_send = _pack([jnp.stack(grads[n]) for n in rep_late])
    small_send = _pack([_split_shards(jnp.stack(grads[n]), _SHARD_AXIS[n]) for n in _SMALL_SHARDED], lead=(N_DEV,))
    rep_handles, grad_x = _exchange_start([rep_send], grad_x, gather=True, name="gather_start_replicated_grads")
    small_handles, grad_x = _exchange_start([small_send], grad_x, gather=False, name="scatter_start_small_grads")

    out = {kind: {} for kind in kinds}
    done = {}
    res = [grad_x]
    for i, part in sorted(in_flight, key=lambda ip: (-ip[0], ip[1])):
        names, handles = in_flight[i, part]
        sent, landed = _exchange_wait(handles, res[0], gather=False, name="scatter_wait_layer%d_part%d" % (i, part))
        for n, x_sent, parts in zip(names, sent, landed):
            layer = layer_of(n, i)
            own = lax.dynamic_index_in_dim(x_sent, me, axis=0, keepdims=False)
            res = _adamw(parts, local(n, layer, wl), local(n, layer, ml), local(n, layer, vl), name="adamw_" + n, own=own)
            done.setdefault(n, {})[layer] = [r.T if n in _COL_SHARDED else r for r in res]
    for n, layers in done.items():
        for k, kind in enumerate(kinds):
            out[kind][n] = jnp.stack([layers[layer][k] for layer in sorted(layers)])

    last = res[0]
    (small_sent,), (parts,) = _exchange_wait(small_handles, last, gather=False, name="scatter_wait_small_grads")
    shapes = [wl[n].shape for n in _SMALL_SHARDED]
    res = _adamw(parts, _pack([wl[n] for n in _SMALL_SHARDED]), _pack([ml[n] for n in _SMALL_SHARDED]),
                 _pack([vl[n] for n in _SMALL_SHARDED]), name="adamw_small",
                 own=lax.dynamic_index_in_dim(small_sent, me, axis=0, keepdims=False))
    for kind, buf in zip(kinds, res):
        out[kind].update(zip(_SMALL_SHARDED, _unpack(buf, shapes)))

    for group, handles, tag in ((rep_early, s5_handles[0], "s5"), (rep_late, rep_handles, "replicated")):
        (rep_sent,), (parts,) = _exchange_wait(handles, last, gather=True, name="gather_wait_%s_grads" % tag)
        shapes = [wl[n].shape for n in group]
        res = _adamw(parts, _pack([wl[n] for n in group]), _pack([ml[n] for n in group]),
                     _pack([vl[n] for n in group]), name="adamw_" + tag, own=rep_sent)
        for kind, buf in zip(kinds, res):
            out[kind].update(zip(group, _unpack(buf, shapes)))

    return (loss, grad_x[None], *[out["g"][n] for n in _WEIGHTS], *[out["d"][n] for n in _WEIGHTS],
            *[out["m"][n] for n in _WEIGHTS], *[out["v"][n] for n in _WEIGHTS])
```

```python
import functools
import math

import jax
import jax.numpy as jnp
from jax import lax
from jax.experimental import pallas as pl
from jax.experimental.pallas import tpu as pltpu

F32 = jnp.float32
BF16 = jnp.bfloat16

D_MODEL = 1024
DEPTH = 4
A_GROUPS = 4
A_CHUNK = 128
A_WIDTH = 512
B_HEADS = 8
B_HEAD_DIM = 64
B_WIDTH = 512
IN_COLS = 2 * A_WIDTH + 3 * B_WIDTH + B_HEADS
IN_COLS_PAD = 2688
S5_GROUP_CH = 16
S5_GROUPS = 64
S5_STATE = 64
S5_N = S5_GROUPS * S5_STATE
S5_BLOCKS = 8
D_FF = 2816
PLE_DIM = 256
EPS = 1e-6
NEG_INF = -1e30
N_DEV = 8

ADAM_LR = 0.001
ADAM_B1 = 0.9
ADAM_B2 = 0.999
ADAM_EPS = 1e-08
ADAM_WD = 0.01
ADAM_STEP = 10

VMEM_LIMIT_BYTES = 56 * 1024 * 1024
MESH = pl.DeviceIdType.MESH


def _params(sem, vmem=VMEM_LIMIT_BYTES):
    return pltpu.CompilerParams(dimension_semantics=sem, vmem_limit_bytes=vmem)


_GELU_K = math.sqrt(2.0 / math.pi)


def _gelu(x):
    return x * (0.5 * (1.0 + jnp.tanh(_GELU_K * (x + 0.044715 * (x * x * x)))))


def _gelu_grad(x):
    t = jnp.tanh(_GELU_K * (x + 0.044715 * (x * x * x)))
    return 0.5 * (1.0 + t) + 0.5 * x * (1.0 - t * t) * (_GELU_K * (1.0 + 3.0 * 0.044715 * (x * x)))


def _sigmoid(x):
    return 0.5 * jnp.tanh(0.5 * x) + 0.5


def _rstd(x):
    return lax.rsqrt(jnp.mean(x * x, axis=-1, keepdims=True) + EPS)


def _rms_bwd(x, dy, g):
    r = _rstd(x)
    xh = x * r
    dyg = dy if g is None else dy * g
    dx = r * (dyg - xh * jnp.mean(dyg * xh, axis=-1, keepdims=True))
    return dx, dy * xh


def _shift_down(blk, halo, k):
    tr = blk.shape[0]
    r = pltpu.roll(blk, k, 0)
    hr = pltpu.roll(halo, k, 0)
    first = jnp.where(lax.broadcasted_iota(jnp.int32, hr.shape, 0) < k, hr, r[0:8])
    return jnp.concatenate([first, r[8:tr]], axis=0)


def _rw(fn, rows, consts, outs, accs=(), *, tr, name, prev=(), nxt=(), widths=None):
    s = rows[0].shape[0]
    tr = min(tr, s)
    n = s // tr
    nr, nc, npv, nnx, no, na = len(rows), len(consts), len(prev), len(nxt), len(outs), len(accs)
    widths = widths or [None] * nr

    def body(*refs):
        ins, out_refs = refs[:nr + nc + npv + nnx], refs[nr + nc + npv + nnx:]
        i = pl.program_id(0)
        vals = [r[...] for r in ins[:nr + nc]]
        vals += [jnp.where(i == 0, 0.0, r[...]) for r in ins[nr + nc:nr + nc + npv]]
        vals += [jnp.where(i == n - 1, 0.0, r[...]) for r in ins[nr + nc + npv:]]
        res = fn(*vals)
        if not isinstance(res, (tuple, list)):
            res = (res,)
        for k in range(no):
            out_refs[k][...] = res[k].astype(out_refs[k].dtype)
        if na:
            @pl.when(i == 0)
            def _():
                for k in range(na):
                    out_refs[no + k][...] = jnp.zeros_like(out_refs[no + k])

            for k in range(na):
                out_refs[no + k][...] += res[no + k]

    in_specs = []
    for a, w in zip(rows, widths):
        if w is None:
            in_specs.append(pl.BlockSpec((tr, a.shape[1]), lambda i: (i, 0)))
        else:
            in_specs.append(pl.BlockSpec((tr, w[1]), functools.partial(lambda i, cb: (i, cb), cb=w[0])))
    for c in consts:
        in_specs.append(pl.BlockSpec(c.shape, functools.partial(lambda i, nd: (0,) * nd, nd=c.ndim)))
    t8 = tr // 8
    for a in prev:
        in_specs.append(pl.BlockSpec((8, a.shape[1]), lambda i: (jnp.maximum(i * t8 - 1, 0), 0)))
    for a in nxt:
        in_specs.append(pl.BlockSpec((8, a.shape[1]), lambda i: (jnp.minimum((i + 1) * t8, s // 8 - 1), 0)))
    out_shape = [jax.ShapeDtypeStruct((s, w), dt) for w, dt in outs]
    out_specs = [pl.BlockSpec((tr, w), lambda i: (i, 0)) for w, _ in outs]
    out_shape += [jax.ShapeDtypeStruct(a, F32) for a in accs]
    out_specs += [pl.BlockSpec(a, lambda i: (0, 0)) for a in accs]
    res = pl.pallas_call(
        body, grid=(n,), in_specs=in_specs, out_specs=out_specs, out_shape=out_shape, name=name,
        compiler_params=_params(("arbitrary",) if na else ("parallel",)),
    )(*rows, *consts, *prev, *nxt)
    return res


def _pick(n, cap):
    if n <= cap:
        return n
    best = None
    for d in range(128, cap + 1, 128):
        if n % d == 0:
            best = d
    assert best is not None, (n, cap)
    return best


def _mm(a, b, *, ta=False, tb=False, out_dtype=F32, add=None, epilogue=None, name, bm=1024, bn=512, bk=None):
    (k_dim, m) = a.shape if ta else a.shape[::-1]
    n = b.shape[0] if tb else b.shape[1]
    assert (b.shape[1] if tb else b.shape[0]) == k_dim
    if bk is None:
        bk = 1024 if ta else 2816
    bm, bn, bk = _pick(m, bm), _pick(n, bn), _pick(k_dim, bk)
    nk = k_dim // bk
    dims = (((0 if ta else 1,), (1 if tb else 0,)), ((), ()))
    assert add is None or epilogue is None
    epi_fn, extra, out_dtypes, vecs, n_sums = (tuple(epilogue or (None, [add] if add is not None else [], (out_dtype,)))
                                               + ((), 0))[:5]
    n_extra, n_vec, n_out = len(extra), len(vecs), len(out_dtypes)

    def body(*refs):
        a_ref, b_ref = refs[0], refs[1]
        extra_refs = refs[2:2 + n_extra + n_vec]
        out_refs = refs[2 + n_extra + n_vec:2 + n_extra + n_vec + n_out + n_sums]
        p = lax.dot_general(a_ref[...].astype(BF16), b_ref[...].astype(BF16), dims, preferred_element_type=F32)

        def finish(acc):
            if epi_fn is not None:
                res = epi_fn(acc, *[r[...] for r in extra_refs])
            else:
                res = (acc + extra_refs[0][...],) if n_extra else (acc,)
            for o_ref, r in zip(out_refs, res):
                o_ref[...] = r.reshape(o_ref.shape).astype(o_ref.dtype)

        if nk == 1:
            finish(p)
        else:
            acc_ref = refs[2 + n_extra + n_vec + n_out + n_sums]
            kk = pl.program_id(2)

            @pl.when(kk == 0)
            def _():
                acc_ref[...] = p

            @pl.when(kk > 0)
            def _():
                acc_ref[...] += p

            @pl.when(kk == nk - 1)
            def _():
                finish(acc_ref[...])

    a_spec = pl.BlockSpec((bk, bm), lambda i, j, k: (k, i)) if ta else pl.BlockSpec((bm, bk), lambda i, j, k: (i, k))
    b_spec = pl.BlockSpec((bn, bk), lambda i, j, k: (j, k)) if tb else pl.BlockSpec((bk, bn), lambda i, j, k: (k, j))
    tile = pl.BlockSpec((bm, bn), lambda i, j, k: (i, j))
    vec = pl.BlockSpec((1, bn), lambda i, j, k: (0, j))
    sums = pl.BlockSpec((1, 1, bn), lambda i, j, k: (i, 0, j))
    res = pl.pallas_call(
        body, grid=(m // bm, n // bn, nk), in_specs=[a_spec, b_spec] + [tile] * n_extra + [vec] * n_vec,
        out_specs=[tile] * n_out + [sums] * n_sums,
        out_shape=[jax.ShapeDtypeStruct((m, n), dt) for dt in out_dtypes]
        + [jax.ShapeDtypeStruct((m // bm, 1, n), F32)] * n_sums,
        scratch_shapes=[pltpu.VMEM((bm, bn), F32)] if nk > 1 else [],
        name=name, compiler_params=_params(("parallel", "parallel", "arbitrary")),
    )(a, b, *extra, *vecs)
    return res if epilogue is not None else res[0]


def _mm_rms_bwd(a, b, x, g, dx_add, *, tb=False, name, bf16_copy=False):
    n = x.shape[1]
    has_g = g is not None

    def fn(acc, xv, addv, *gv):
        dx, dg = _rms_bwd(xv, acc, gv[0] if has_g else None)
        dx = dx + addv
        return (dx,) + ((dx,) if bf16_copy else ()) + ((jnp.sum(dg, axis=0, keepdims=True),) if has_g else ())

    outs = (F32,) + ((BF16,) if bf16_copy else ())
    res = _mm(a, b, tb=tb, name=name, bm=512, bn=n, epilogue=(fn, [x, dx_add], outs, [g] if has_g else [], int(has_g)))
    if has_g:
        return tuple(res[:-1]) + (jnp.sum(res[-1], axis=0),)
    return tuple(res) + (None,)


def _rms_fwd(x, g, *, name):
    if g is None:
        return _rw(lambda xv: xv * _rstd(xv), [x], [], [(x.shape[1], BF16)], tr=512, name=name)[0]
    return _rw(lambda xv, gv: xv * _rstd(xv) * gv, [x], [g], [(x.shape[1], BF16)], tr=512, name=name)[0]


def _rms_bwd_call(x, dy, g, dx_add, *, name, tr=512):
    w = x.shape[1]

    def fn(xv, dyv, addv, gv):
        dx, dg = _rms_bwd(xv, dyv, gv)
        return dx + addv, jnp.sum(dg, axis=0, keepdims=True)

    return _rw(fn, [x, dy, dx_add], [g], [(w, F32)], [(1, w)], tr=tr, name=name)


def _conv_taps(hup, halo):
    return _shift_down(hup, halo, 1), _shift_down(hup, halo, 2)


def _ffn_fwd(x1, w, i):
    h2b = _rms_fwd(x1, w["norm_ffn"][i:i + 1], name="ffn_norm")
    hup = _mm(h2b, w["ffn_w_up"][i], tb=True, name="ffn_up", bn=1408)
    cw, cb = w["ffn_conv_w"][i], w["ffn_conv_b"][i:i + 1]

    def fn(hv, cwv, cbv, halo):
        h1, h2 = _conv_taps(hv, halo)
        hc = cbv + cwv[0:1] * h2
        hc = hc + cwv[1:2] * h1
        hc = hc + cwv[2:3] * hv
        g, up = hc[:, :D_FF], hc[:, D_FF:]
        return (g * _sigmoid(g)) * up

    a = _rw(fn, [hup], [cw, cb], [(D_FF, BF16)], tr=128, name="ffn_conv_gate", prev=[hup])[0]
    x2 = _mm(a, w["ffn_w_down"][i], add=x1, name="ffn_down")
    return x2, (h2b, hup, a)


def _ffn_bwd(dx2, dx2b, x1, saved, w, i):
    h2b, hup, a = saved
    cw, cb = w["ffn_conv_w"][i], w["ffn_conv_b"][i:i + 1]
    da = _mm(dx2b, w["ffn_w_down"][i], tb=True, name="ffn_down_dx", bn=1408)
    d_wdown = _mm(a, dx2b, ta=True, name="ffn_down_dw", bm=1408)

    def fn1(hv, dav, cwv, cbv, h_prev, h_next, da_next):
        tr = hv.shape[0]
        he = jnp.concatenate([hv, h_next], axis=0)
        dae = jnp.concatenate([dav, da_next], axis=0)
        h1, h2 = _conv_taps(he, h_prev)
        hc = cbv + cwv[0:1] * h2
        hc = hc + cwv[1:2] * h1
        hc = hc + cwv[2:3] * he
        g, up = hc[:, :D_FF], hc[:, D_FF:]
        sg = _sigmoid(g)
        d_up = dae * (g * sg)
        d_g = dae * up * (sg * (1.0 + g * (1.0 - sg)))
        d_hce = jnp.concatenate([d_g, d_up], axis=1)
        rows_e = tr + 8
        d_hup = (cwv[2:3] * d_hce[:tr] + cwv[1:2] * pltpu.roll(d_hce, rows_e - 1, 0)[:tr]
                 + cwv[0:1] * pltpu.roll(d_hce, rows_e - 2, 0)[:tr])
        d_hc = d_hce[:tr]
        col = lambda v: jnp.sum(v, axis=0, keepdims=True)
        return d_hup, col(d_hc), col(d_hc * h2[:tr]), col(d_hc * h1[:tr]), col(d_hc * hv)

    w2 = 2 * D_FF
    d_hup, d_cb, d_cw0, d_cw1, d_cw2 = _rw(fn1, [hup, da], [cw, cb], [(w2, BF16)], [(1, w2)] * 4, tr=128,
                                           name="ffn_conv_gate_bwd", prev=[hup], nxt=[hup, da])
    dx1, dx1b, d_norm = _mm_rms_bwd(d_hup, w["ffn_w_up"][i], x1, w["norm_ffn"][i:i + 1], dx2, name="ffn_up_dx",
                                    bf16_copy=True)
    d_wup_t = _mm(h2b, d_hup, ta=True, name="ffn_up_dw", bn=1408).T
    grads = {"ffn_w_up": d_wup_t, "ffn_w_down": d_wdown, "ffn_conv_b": d_cb[0],
             "ffn_conv_w": jnp.concatenate([d_cw0, d_cw1, d_cw2], axis=0), "norm_ffn": d_norm[0]}
    return dx1, dx1b, grads


def _ple_fwd(x2, p_i, w, i):
    rn = _rms_fwd(x2, None, name="ple_norm")
    gl = _mm(rn, w["ple_w_gate"][i], name="ple_gate")
    x3 = _mm(p_i, w["ple_w_proj"][i], tb=True, name="ple_proj",
             epilogue=(lambda pp, xv, g: (xv + _sigmoid(g) * pp,), [x2, gl], (F32,)))[0]
    return x3, (rn, gl)


def _ple_bwd(dx3, x2, p_i, saved, w, i):
    rn, gl = saved

    def fn(pp, dv, g):
        sg = _sigmoid(g)
        return dv * sg, dv * pp * (sg * (1.0 - sg))

    d_pp, d_pre = _mm(p_i, w["ple_w_proj"][i], tb=True, name="ple_proj_bwd", epilogue=(fn, [dx3, gl], (BF16, BF16)))
    d_wproj_t = _mm(p_i, d_pp, ta=True, name="ple_proj_dw", bn=1024).T
    d_wgate = _mm(rn, d_pre, ta=True, name="ple_gate_dw")
    dx2, dx2b, _ = _mm_rms_bwd(d_pre, w["ple_w_gate"][i], x2, None, dx3, tb=True, name="ple_gate_dx", bf16_copy=True)
    return dx2, dx2b, {"ple_w_proj": d_wproj_t, "ple_w_gate": d_wgate}


def _loss_and_grad(xf, tgt):
    def fn(xv, tv):
        d = xv - tv
        part = 0.5 * jnp.sum(jnp.mean(d * d, axis=-1, keepdims=True), axis=0, keepdims=True)
        return d * (1.0 / D_MODEL), part

    dx, loss = _rw(fn, [xf, tgt], [], [(D_MODEL, F32)], [(1, 1)], tr=512, name="loss_head")
    return loss[0, 0], dx


def _tril_mask():
    shp = (A_CHUNK, A_CHUNK)
    return lax.broadcasted_iota(jnp.int32, shp, 0) >= lax.broadcasted_iota(jnp.int32, shp, 1)


def _gmlp_fwd(z, w_s, b_s, v_gain, *, tr=512):
    s = z.shape[0]
    tr = min(tr, s)
    gw = A_WIDTH // A_GROUPS

    def body(z_ref, w_ref, b_ref, g_ref, y_ref):
        tril = _tril_mask()
        for g in range(A_GROUPS):
            wg = jnp.where(tril, w_ref[g], 0.0).astype(BF16)
            cols = slice(g * gw, (g + 1) * gw)
            vcols = slice(A_WIDTH + g * gw, A_WIDTH + (g + 1) * gw)
            for c in range(tr // A_CHUNK):
                rows = slice(c * A_CHUNK, (c + 1) * A_CHUNK)
                v = _gelu(z_ref[rows, vcols])
                vn = v * _rstd(v) * g_ref[:, cols]
                sv = jnp.dot(wg, vn.astype(BF16), preferred_element_type=F32) + b_ref[g]
                y_ref[rows, cols] = (_gelu(z_ref[rows, cols]) * sv).astype(BF16)

    return pl.pallas_call(
        body, grid=(s // tr,),
        in_specs=[pl.BlockSpec((tr, 2 * A_WIDTH), lambda i: (i, 0)),
                  pl.BlockSpec(w_s.shape, lambda i: (0, 0, 0)), pl.BlockSpec(b_s.shape, lambda i: (0, 0, 0)),
                  pl.BlockSpec(v_gain.shape, lambda i: (0, 0))],
        out_specs=pl.BlockSpec((tr, A_WIDTH), lambda i: (i, 0)),
        out_shape=jax.ShapeDtypeStruct((s, A_WIDTH + B_WIDTH), BF16), name="gmlp_fwd",
        compiler_params=_params(("parallel",)),
    )(z, w_s, b_s, v_gain)


def _gmlp_bwd(z, d_ymix, w_s, b_s, v_gain, dz, *, tr=512):
    s = z.shape[0]
    tr = min(tr, s)
    gw = A_WIDTH // A_GROUPS

    def body(z_ref, dy_ref, w_ref, b_ref, g_ref, _, dz_ref, dw_ref, db_ref, dg_ref):
        @pl.when(pl.program_id(0) == 0)
        def _():
            dw_ref[...] = jnp.zeros_like(dw_ref)
            db_ref[...] = jnp.zeros_like(db_ref)
            dg_ref[...] = jnp.zeros_like(dg_ref)

        tril = _tril_mask()
        for g in range(A_GROUPS):
            wg = jnp.where(tril, w_ref[g], 0.0).astype(BF16)
            cols = slice(g * gw, (g + 1) * gw)
            vcols = slice(A_WIDTH + g * gw, A_WIDTH + (g + 1) * gw)
            gain = g_ref[:, cols]
            for c in range(tr // A_CHUNK):
                rows = slice(c * A_CHUNK, (c + 1) * A_CHUNK)
                va, ua = z_ref[rows, vcols], z_ref[rows, cols]
                v = _gelu(va)
                r = _rstd(v)
                vh = v * r
                vnb = (vh * gain).astype(BF16)
                sv = jnp.dot(wg, vnb, preferred_element_type=F32) + b_ref[g]
                dy = dy_ref[rows, cols]
                d_sv = dy * _gelu(ua)
                dz_ref[rows, cols] = (dy * sv * _gelu_grad(ua)).astype(BF16)
                d_svb = d_sv.astype(BF16)
                d_vn = lax.dot_general(wg, d_svb, (((0,), (0,)), ((), ())), preferred_element_type=F32)
                dwp = lax.dot_general(d_svb, vnb, (((1,), (1,)), ((), ())), preferred_element_type=F32)
                dw_ref[g] += jnp.where(tril, dwp, 0.0)
                db_ref[g] += jnp.sum(d_sv, axis=1, keepdims=True)
                dg_ref[:, cols] += jnp.sum(d_vn * vh, axis=0, keepdims=True)
                d_vh = d_vn * gain
                d_v = r * (d_vh - vh * jnp.mean(d_vh * vh, axis=-1, keepdims=True))
                dz_ref[rows, vcols] = (d_v * _gelu_grad(va)).astype(BF16)

    return pl.pallas_call(
        body, grid=(s // tr,),
        in_specs=[pl.BlockSpec((tr, 2 * A_WIDTH), lambda i: (i, 0)), pl.BlockSpec((tr, A_WIDTH), lambda i: (i, 0)),
                  pl.BlockSpec(w_s.shape, lambda i: (0, 0, 0)), pl.BlockSpec(b_s.shape, lambda i: (0, 0, 0)),
                  pl.BlockSpec(v_gain.shape, lambda i: (0, 0)), _ANY],
        out_specs=[pl.BlockSpec((tr, 2 * A_WIDTH), lambda i: (i, 0)), pl.BlockSpec(w_s.shape, lambda i: (0, 0, 0)),
                   pl.BlockSpec(b_s.shape, lambda i: (0, 0, 0)), pl.BlockSpec(v_gain.shape, lambda i: (0, 0))],
        out_shape=[jax.ShapeDtypeStruct(dz.shape, dz.dtype), jax.ShapeDtypeStruct(w_s.shape, F32),
                   jax.ShapeDtypeStruct(b_s.shape, F32), jax.ShapeDtypeStruct(v_gain.shape, F32)],
        input_output_aliases={5: 0}, name="gmlp_bwd", compiler_params=_params(("arbitrary",)),
    )(z, d_ymix, w_s, b_s, v_gain, dz)


def _dot3(x, ub):
    x1 = x.astype(BF16)
    r1 = x - x1.astype(F32)
    x2 = r1.astype(BF16)
    x3 = (r1 - x2.astype(F32)).astype(BF16)
    d = lambda a: jnp.dot(a, ub, preferred_element_type=F32)
    return d(x1) + d(x2) + d(x3)


def _log_sigmoid(x):
    return jnp.minimum(x, 0.0) - jnp.log(1.0 + jnp.exp(-jnp.abs(x)))


def _fgate_fwd(f_t, b_col, *, tb=256):
    h, s = f_t.shape
    tb = min(tb, s)

    def body(f_ref, b_ref, c_ref, carry):
        @pl.when(pl.program_id(0) == 0)
        def _():
            carry[...] = jnp.zeros_like(carry)

        lf = _log_sigmoid(f_ref[...] + b_ref[...])
        shp = (tb, tb)
        upper = (lax.broadcasted_iota(jnp.int32, shp, 0) <= lax.broadcasted_iota(jnp.int32, shp, 1)).astype(BF16)
        c_ref[...] = _dot3(lf, upper) + carry[...]
        carry[...] += jnp.sum(lf, axis=1, keepdims=True)

    return pl.pallas_call(
        body, grid=(s // tb,),
        in_specs=[pl.BlockSpec((h, tb), lambda i: (0, i)), pl.BlockSpec((h, 1), lambda i: (0, 0))],
        out_specs=pl.BlockSpec((h, tb), lambda i: (0, i)), out_shape=jax.ShapeDtypeStruct((h, s), F32),
        scratch_shapes=[pltpu.VMEM((h, 1), F32)], name="fgate_fwd", compiler_params=_params(("arbitrary",)),
    )(f_t, b_col)


def _fgate_bwd(f_t, b_col, dc_a, dc_b, *, tb=256):
    h, s = f_t.shape
    tb = min(tb, s)
    n = s // tb

    def body(f_ref, b_ref, da_ref, db_ref, df_ref, dbias_ref, carry):
        @pl.when(pl.program_id(0) == 0)
        def _():
            carry[...] = jnp.zeros_like(carry)
            dbias_ref[...] = jnp.zeros_like(dbias_ref)

        dc = da_ref[...] + db_ref[...]
        shp = (tb, tb)
        lower = (lax.broadcasted_iota(jnp.int32, shp, 0) >= lax.broadcasted_iota(jnp.int32, shp, 1)).astype(BF16)
        d_lf = _dot3(dc, lower) + carry[...]
        carry[...] += jnp.sum(dc, axis=1, keepdims=True)
        df = d_lf * (1.0 - _sigmoid(f_ref[...] + b_ref[...]))
        df_ref[...] = df
        dbias_ref[...] += jnp.sum(df, axis=1, keepdims=True)

    blk = pl.BlockSpec((h, tb), lambda i: (0, n - 1 - i))
    return pl.pallas_call(
        body, grid=(n,), in_specs=[blk, pl.BlockSpec((h, 1), lambda i: (0, 0)), blk, blk],
        out_specs=[blk, pl.BlockSpec((h, 1), lambda i: (0, 0))],
        out_shape=[jax.ShapeDtypeStruct((h, s), F32), jax.ShapeDtypeStruct((h, 1), F32)],
        scratch_shapes=[pltpu.VMEM((h, 1), F32)], name="fgate_bwd", compiler_params=_params(("arbitrary",)),
    )(f_t, b_col, dc_a, dc_b)


_NT = (((1,), (1,)), ((), ()))


def _causal(shape, row0, col0, transposed):
    r = lax.broadcasted_iota(jnp.int32, shape, 0) + row0
    c = lax.broadcasted_iota(jnp.int32, shape, 1) + col0
    return (r <= c) if transposed else (c <= r)


_ATT_SCALE = B_HEAD_DIM ** -0.5
ATT_W = 128
_COL_CQ, _COL_CK, _COL_LSE, _COL_DELTA = 64, 67, 70, 64
_ATT_BLK = 2048
_ATT_SUB = 512


def _split3(x):
    h = x.astype(BF16).astype(F32)
    r = x - h
    m = r.astype(BF16).astype(F32)
    return h, m, (r - m).astype(BF16).astype(F32)


def _put_cols(base, lane, col0, parts):
    for t, pv in enumerate(parts):
        base = jnp.where(lane == col0 + t, pv, base)
    return base


_HEAD_PAIRS = B_HEADS // 2
_Q_BLOCK, _K_BLOCK, _V_BLOCK = 8, 12, 16
_F_BLOCK = 20
_O_BLOCK = 4
_ANY = pl.BlockSpec(memory_space=pl.ANY)


def _head_half(ref, h):
    x = ref[...]
    x = jnp.where((h & 1) == 1, pltpu.roll(x, B_HEAD_DIM, 1), x)
    return jnp.where(lax.broadcasted_iota(jnp.int32, x.shape, 1) < B_HEAD_DIM, x, 0.0)


def _pair(a, b):
    lane = lax.broadcasted_iota(jnp.int32, a.shape, 1)
    return jnp.where(lane < B_HEAD_DIM, a, pltpu.roll(b, B_HEAD_DIM, 1))


def _attn_prep(z, c_col, gq, gk, *, tr=1024):
    s = z.shape[0]
    tr = min(tr, s)
    n = s // tr

    def body(q_ref, k_ref, v_ref, c_ref, gq_ref, gk_ref, qo_ref, ko_ref, vo_ref):
        h = pl.program_id(0)
        q, k, v = _head_half(q_ref, h), _head_half(k_ref, h), _head_half(v_ref, h)
        lane = lax.broadcasted_iota(jnp.int32, q.shape, 1)
        rq = lax.rsqrt(jnp.sum(q * q, axis=-1, keepdims=True) * (1.0 / B_HEAD_DIM) + EPS)
        rk = lax.rsqrt(jnp.sum(k * k, axis=-1, keepdims=True) * (1.0 / B_HEAD_DIM) + EPS)
        c = c_ref[...]
        ch, cm, cl = _split3(c)
        one = jnp.ones_like(c)
        qq = _put_cols(q * rq * gq_ref[...] * _ATT_SCALE, lane, _COL_CQ, (ch, cm, cl))
        qq = _put_cols(qq, lane, _COL_CK, (one, one, one))
        kk = _put_cols(k * rk * gk_ref[...], lane, _COL_CQ, (one, one, one))
        kk = _put_cols(kk, lane, _COL_CK, (-ch, -cm, -cl))
        kk = _put_cols(kk, lane, _COL_LSE, (one, one, one))
        qo_ref[...] = qq.astype(BF16)
        ko_ref[...] = kk.astype(BF16)
        vo_ref[...] = _put_cols(v, lane, _COL_DELTA, (one, one, one)).astype(BF16)

    def zcol(base):
        return pl.BlockSpec((tr, ATT_W), lambda h, i: (i, base + (h >> 1)))

    rows = pl.BlockSpec((tr, ATT_W), lambda h, i: (h * n + i, 0))
    gain = pl.BlockSpec((1, ATT_W), lambda h, i: (0, 0))
    return pl.pallas_call(
        body, grid=(B_HEADS, n),
        in_specs=[zcol(_Q_BLOCK), zcol(_K_BLOCK), zcol(_V_BLOCK), pl.BlockSpec((tr, 1), lambda h, i: (h * n + i, 0)),
                  gain, gain],
        out_specs=[rows] * 3, out_shape=[jax.ShapeDtypeStruct((B_HEADS * s, ATT_W), BF16)] * 3, name="attn_prep",
        compiler_params=_params(("parallel", "parallel")),
    )(z, z, z, c_col, gq, gk)


def _attn_bwd_prep(d_ymix, o, qp, lse, *, tr=1024):
    s = d_ymix.shape[0]
    tr = min(tr, s)
    n = s // tr

    def body(do_ref, o_ref, q_ref, l_ref, dop_ref, qpp_ref):
        dov = _head_half(do_ref, pl.program_id(0))
        lane = lax.broadcasted_iota(jnp.int32, dov.shape, 1)
        dh, dm, dl = _split3(jnp.sum(dov * o_ref[...], axis=-1, keepdims=True))
        lh, lm, ll = _split3(l_ref[...])
        dop_ref[...] = _put_cols(dov, lane, _COL_DELTA, (-dh, -dm, -dl)).astype(BF16)
        qpp_ref[...] = _put_cols(q_ref[...].astype(F32), lane, _COL_LSE, (-lh, -lm, -ll)).astype(BF16)

    rows = pl.BlockSpec((tr, ATT_W), lambda h, i: (h * n + i, 0))
    return pl.pallas_call(
        body, grid=(B_HEADS, n),
        in_specs=[pl.BlockSpec((tr, ATT_W), lambda h, i: (i, _O_BLOCK + (h >> 1))), rows, rows,
                  pl.BlockSpec((tr, 1), lambda h, i: (h * n + i, 0))],
        out_specs=[rows] * 2, out_shape=[jax.ShapeDtypeStruct((B_HEADS * s, ATT_W), BF16)] * 2,
        name="attn_bwd_prep", compiler_params=_params(("parallel", "parallel")),
    )(d_ymix, o, qp, lse)


def _attn_merge_out(o, ymix, *, tr=1024):
    s = o.shape[1]
    tr = min(tr, s)

    def body(a_ref, b_ref, buf_ref, out_ref):
        out_ref[...] = _pair(a_ref[0], b_ref[0]).astype(BF16)

    return pl.pallas_call(
        body, grid=(_HEAD_PAIRS, s // tr),
        in_specs=[pl.BlockSpec((1, tr, ATT_W), lambda j, i: (2 * j, i, 0)),
                  pl.BlockSpec((1, tr, ATT_W), lambda j, i: (2 * j + 1, i, 0)), _ANY],
        out_specs=pl.BlockSpec((tr, ATT_W), lambda j, i: (i, _O_BLOCK + j)),
        out_shape=jax.ShapeDtypeStruct(ymix.shape, ymix.dtype), input_output_aliases={2: 0}, name="attn_merge_out",
        compiler_params=_params(("parallel", "parallel")),
    )(o, o, ymix)


def _pair_bwd(d_heads, dz, out_block, *, z=None, in_block=None, gain=None, scale=None, name, tr=1024):
    s = d_heads.shape[1]
    tr = min(tr, s)
    norm = gain is not None

    def body(*refs):
        a_ref, b_ref = refs[0], refs[1]
        d = _pair(a_ref[0], b_ref[0])
        if norm:
            x_ref, g_ref, _, out_ref, dg_ref = refs[2:]

            @pl.when((pl.program_id(0) == 0) & (pl.program_id(1) == 0))
            def _():
                dg_ref[...] = jnp.zeros_like(dg_ref)

            x = x_ref[...]
            lo = lax.broadcasted_iota(jnp.int32, x.shape, 1) < B_HEAD_DIM

            def half_mean(v):
                s_lo = jnp.sum(jnp.where(lo, v, 0.0), axis=-1, keepdims=True)
                s_hi = jnp.sum(jnp.where(lo, 0.0, v), axis=-1, keepdims=True)
                return jnp.where(lo, s_lo, s_hi) * (1.0 / B_HEAD_DIM)

            r = lax.rsqrt(half_mean(x * x) + EPS)
            xh = x * r
            dy = d * scale if scale is not None else d
            dyg = dy * g_ref[...]
            d = r * (dyg - xh * half_mean(dyg * xh))
            dg_ref[...] += jnp.sum(dy * xh, axis=0, keepdims=True)
        else:
            out_ref = refs[3]
        out_ref[...] = d.astype(BF16)

    heads = [pl.BlockSpec((1, tr, ATT_W), lambda j, i: (2 * j, i, 0)),
             pl.BlockSpec((1, tr, ATT_W), lambda j, i: (2 * j + 1, i, 0))]
    out_spec = pl.BlockSpec((tr, ATT_W), lambda j, i: (i, out_block + j))
    dz_shape = jax.ShapeDtypeStruct(dz.shape, dz.dtype)
    if norm:
        res = pl.pallas_call(
            body, grid=(_HEAD_PAIRS, s // tr),
            in_specs=heads + [pl.BlockSpec((tr, ATT_W), lambda j, i: (i, in_block + j)),
                              pl.BlockSpec((1, ATT_W), lambda j, i: (0, 0)), _ANY],
            out_specs=[out_spec, pl.BlockSpec((1, ATT_W), lambda j, i: (0, 0))],
            out_shape=[dz_shape, jax.ShapeDtypeStruct((1, ATT_W), F32)], input_output_aliases={4: 0}, name=name,
            compiler_params=_params(("arbitrary", "arbitrary")),
        )(d_heads, d_heads, z, gain, dz)
        return res[0], res[1]
    return pl.pallas_call(
        body, grid=(_HEAD_PAIRS, s // tr), in_specs=heads + [_ANY], out_specs=out_spec, out_shape=dz_shape,
        input_output_aliases={2: 0}, name=name, compiler_params=_params(("parallel", "parallel")),
    )(d_heads, d_heads, dz), None


def _attn_fwd(qp, kp, vp, *, blk=_ATT_BLK):
    h, s, d = qp.shape
    b = min(blk, s)
    n = s // b
    sub = min(_ATT_SUB, b)
    ns = b // sub

    def body(q_ref, k_ref, v_ref, o_ref, lse_ref):
        i = pl.program_id(1)
        qs = [q_ref[0, t * sub:(t + 1) * sub, :] for t in range(ns)]

        def tile(j, state, masked):
            off = pl.multiple_of(j * b, b)
            kk = k_ref[0, pl.ds(off, b), :]
            vv = v_ref[0, pl.ds(off, b), :]
            new = []
            for t in range(ns):
                m, l, acc = state[3 * t:3 * t + 3]
                nk = (t + 1) * sub if masked else b
                sc = lax.dot_general(qs[t], kk[:nk], _NT, preferred_element_type=F32)
                if masked:
                    sc = jnp.where(_causal((sub, nk), t * sub, 0, False), sc, NEG_INF)
                m_new = jnp.maximum(m, jnp.max(sc, axis=1, keepdims=True))
                alpha = jnp.exp(m - m_new)
                p = jnp.exp(sc - m_new)
                new += [m_new, alpha * l + jnp.sum(p, axis=1, keepdims=True),
                        alpha * acc + jnp.dot(p.astype(BF16), vv[:nk], preferred_element_type=F32)]
            return tuple(new)

        init = (jnp.full((sub, 1), NEG_INF, F32), jnp.zeros((sub, 1), F32), jnp.zeros((sub, d), F32)) * ns
        state = lax.fori_loop(0, i, lambda j, st: tile(j, st, False), init)
        state = tile(i, state, True)
        for t in range(ns):
            m, l, acc = state[3 * t:3 * t + 3]
            o_ref[0, t * sub:(t + 1) * sub, :] = acc / l
            lse_ref[0, t * sub:(t + 1) * sub, :] = m + jnp.log(l)

    blk_spec = pl.BlockSpec((1, b, d), lambda hh, i: (hh, i, 0))
    full = pl.BlockSpec((1, s, d), lambda hh, i: (hh, 0, 0))
    return pl.pallas_call(
        body, grid=(h, n), in_specs=[blk_spec, full, full],
        out_specs=[blk_spec, pl.BlockSpec((1, b, 1), lambda hh, i: (hh, i, 0))],
        out_shape=[jax.ShapeDtypeStruct((h, s, d), F32), jax.ShapeDtypeStruct((h, s, 1), F32)],
        name="attn_fwd", compiler_params=_params(("parallel", "arbitrary")),
    )(qp, kp, vp)


_ATT_BWD_BLK = 1024


def _attn_bwd(qpp, kp, vp, dop, *, blk=_ATT_BWD_BLK):
    h, s, d = qpp.shape
    b = min(blk, s)
    n = s // b
    sub = min(_ATT_SUB, b)
    ns = b // sub

    def body(k_ref, v_ref, q_ref, do_ref, dq_ref, dk_ref, dv_ref):
        j = pl.program_id(1)

        @pl.when(j == 0)
        def _():
            dq_ref[...] = jnp.zeros_like(dq_ref)

        ks = [k_ref[0, t * sub:(t + 1) * sub, :] for t in range(ns)]
        vs = [v_ref[0, t * sub:(t + 1) * sub, :] for t in range(ns)]

        def tile(i, state, masked):
            off = pl.multiple_of(i * b, b)
            qq = q_ref[0, pl.ds(off, b), :]
            dd = do_ref[0, pl.ds(off, b), :]
            new = []
            dq = None
            for t in range(ns):
                q0 = t * sub if masked else 0
                qt, dt = qq[q0:], dd[q0:]
                pt = jnp.exp(lax.dot_general(ks[t], qt, _NT, preferred_element_type=F32))
                if masked:
                    pt = jnp.where(_causal((sub, b - q0), q0, q0, True), pt, 0.0)
                dst = (pt * lax.dot_general(vs[t], dt, _NT, preferred_element_type=F32)).astype(BF16)
                new += [state[2 * t] + jnp.dot(dst, qt, preferred_element_type=F32),
                        state[2 * t + 1] + jnp.dot(pt.astype(BF16), dt, preferred_element_type=F32)]
                part = lax.dot_general(dst, ks[t], _TN, preferred_element_type=F32)
                if q0:
                    part = jnp.concatenate([jnp.zeros((q0, d), F32), part], axis=0)
                dq = part if dq is None else dq + part
            dq_ref[0, pl.ds(off, b), :] += dq
            return tuple(new)

        state = tile(j, (jnp.zeros((sub, d), F32),) * (2 * ns), True)
        state = lax.fori_loop(j + 1, n, lambda i, st: tile(i, st, False), state)
        for t in range(ns):
            dk_ref[0, t * sub:(t + 1) * sub, :] = state[2 * t]
            dv_ref[0, t * sub:(t + 1) * sub, :] = state[2 * t + 1]

    blk_spec = pl.BlockSpec((1, b, d), lambda hh, j: (hh, j, 0))
    full = pl.BlockSpec((1, s, d), lambda hh, j: (hh, 0, 0))
    return pl.pallas_call(
        body, grid=(h, n), in_specs=[blk_spec, blk_spec, full, full], out_specs=[full, blk_spec, blk_spec],
        out_shape=[jax.ShapeDtypeStruct((h, s, d), F32)] * 3, name="attn_bwd",
        compiler_params=_params(("parallel", "arbitrary")),
    )(kp, vp, qpp, dop)


def _pad_head(g):
    return jnp.pad(g, ((0, 0), (0, ATT_W - B_HEAD_DIM)))


def _even_fwd(x0, w, e, i):
    s = x0.shape[0]
    hs = B_HEADS * s
    hb = _rms_fwd(x0, w["norm_mix"][i:i + 1], name="mix_norm")
    z = _mm(hb, w["ev_w_in"][e], tb=True, name="ev_in", bn=896)
    b_s = w["ev_b_spatial"][e][:, :, None]
    v_gain = w["ev_v_norm"][e:e + 1]
    ymix = _gmlp_fwd(z, w["ev_w_spatial"][e], b_s, v_gain)
    f_t = z[:, IN_COLS - B_HEADS:IN_COLS].T
    c = _fgate_fwd(f_t, w["ev_b_fgate"][e][:, None])
    qp, kp, vp = _attn_prep(z, c.reshape(hs, 1), _pad_head(w["ev_q_norm"][e:e + 1]),
                            _pad_head(w["ev_k_norm"][e:e + 1]))
    shp = (B_HEADS, s, ATT_W)
    o, lse = _attn_fwd(qp.reshape(shp), kp.reshape(shp), vp.reshape(shp))
    ymix = _attn_merge_out(o, ymix)
    x1 = _mm(ymix, w["ev_w_out"][e], add=x0, name="ev_out")
    return x1, (hb, z, ymix, qp, kp, vp, f_t, o, lse)


def _even_bwd(dx1, dx1b, x0, saved, w, e, i):
    hb, z, ymix, qp, kp, vp, f_t, o, lse = saved
    s = x0.shape[0]
    hs = B_HEADS * s
    d_ymix = _mm(dx1b, w["ev_w_out"][e], tb=True, name="ev_out_dx")
    d_wout = _mm(ymix, dx1b, ta=True, name="ev_out_dw")
    dop, qpp = _attn_bwd_prep(d_ymix, o.reshape(hs, ATT_W), qp, lse.reshape(hs, 1))
    shp = (B_HEADS, s, ATT_W)
    qpp, dop, kp3, vp3 = qpp.reshape(shp), dop.reshape(shp), kp.reshape(shp), vp.reshape(shp)
    dqp, dkp, dvp = _attn_bwd(qpp, kp3, vp3, dop)
    d_ft, d_bf = _fgate_bwd(f_t, w["ev_b_fgate"][e][:, None], dqp[:, :, _COL_CQ], -dkp[:, :, _COL_CK])
    dz = jnp.pad(d_ft.T.astype(BF16), ((0, 0), (_F_BLOCK * ATT_W, IN_COLS_PAD - IN_COLS)))
    b_s = w["ev_b_spatial"][e][:, :, None]
    v_gain = w["ev_v_norm"][e:e + 1]
    dz, d_ws, d_bs, d_vg = _gmlp_bwd(z, d_ymix, w["ev_w_spatial"][e], b_s, v_gain, dz)
    twice = lambda g: jnp.concatenate([g, g], axis=1)
    dz, d_gq = _pair_bwd(dqp, dz, _Q_BLOCK, z=z, in_block=_Q_BLOCK, gain=twice(w["ev_q_norm"][e:e + 1]),
                         scale=_ATT_SCALE, name="attn_qnorm_bwd")
    dz, d_gk = _pair_bwd(dkp, dz, _K_BLOCK, z=z, in_block=_K_BLOCK, gain=twice(w["ev_k_norm"][e:e + 1]),
                         name="attn_knorm_bwd")
    dz, _ = _pair_bwd(dvp, dz, _V_BLOCK, name="attn_dv_out")
    dx0, d_nm = _mm_rms_bwd(dz, w["ev_w_in"][e], x0, w["norm_mix"][i:i + 1], dx1, name="ev_in_dx")
    d_win_t = _mm(hb, dz, ta=True, name="ev_in_dw", bn=1344)[:, :IN_COLS].T
    hd = B_HEAD_DIM
    grads = {"ev_w_in": d_win_t, "ev_w_out": d_wout, "ev_b_fgate": d_bf[:, 0],
             "ev_q_norm": d_gq[0, :hd] + d_gq[0, hd:], "ev_k_norm": d_gk[0, :hd] + d_gk[0, hd:], "ev_v_norm": d_vg[0],
             "ev_w_spatial": d_ws, "ev_b_spatial": d_bs[:, :, 0], "norm_mix": d_nm[0]}
    return dx0, grads


def _s5_disc(a_re, a_im, log_dt, b_re, b_im):
    dt = jnp.exp(log_dt)[:, None]
    lr, li = a_re, a_im
    mag = jnp.exp(lr * dt)
    ab_re, ab_im = mag * jnp.cos(li * dt), mag * jnp.sin(li * dt)
    den = lr * lr + li * li
    nr, ni = ab_re - 1.0, ab_im
    cr = (nr * lr + ni * li) / den
    ci = (ni * lr - nr * li) / den
    bb_re = cr[..., None] * b_re - ci[..., None] * b_im
    bb_im = cr[..., None] * b_im + ci[..., None] * b_re
    return ab_re, ab_im, bb_re, bb_im


_GPB = S5_GROUPS // S5_BLOCKS


def _blockdiag(t):
    a, b = t.shape[1:]
    eye = jnp.eye(_GPB, dtype=t.dtype)
    t = t.reshape(S5_BLOCKS, _GPB, a, 1, b) * eye[None, :, None, :, None]
    return t.reshape(S5_BLOCKS, _GPB * a, _GPB * b)


def _blockdiag_extract(m, a, b):
    eye = jnp.eye(_GPB, dtype=m.dtype)
    m = m.reshape(S5_BLOCKS, _GPB, a, _GPB, b) * eye[None, :, None, :, None]
    return jnp.sum(m, axis=3).reshape(S5_GROUPS, a, b)


_SCAN_ROWS = 8


def _cmul(a_r, a_i, b_r, b_i):
    return a_r * b_r - a_i * b_i, a_r * b_i + a_i * b_r


def _scan_tiles(yr_ref, yi_ref, a_r, a_i, c_r, c_i, *, rev, x_refs=None):
    tb, wl = yr_ref.shape
    ntile = tb // _SCAN_ROWS
    with_acc = x_refs is not None
    if rev:
        a_i = -a_i
    pw = [(a_r, a_i)]
    for _ in range(_SCAN_ROWS - 1):
        pw.append(_cmul(*pw[-1], a_r, a_i))
    sub = lax.broadcasted_iota(jnp.int32, (_SCAN_ROWS, wl), 0)
    dist = (_SCAN_ROWS - 1 - sub) if rev else sub
    zero = jnp.zeros((_SCAN_ROWS, wl), F32)
    steps = []
    for kk in (1, 2, 4):
        steps.append(((_SCAN_ROWS - kk) if rev else kk, jnp.where(dist >= kk, pw[kk - 1][0], zero),
                      jnp.where(dist >= kk, pw[kk - 1][1], zero)))
    e_r, e_i = zero, zero
    for d in range(_SCAN_ROWS):
        e_r = jnp.where(dist == d, pw[d][0], e_r)
        e_i = jnp.where(dist == d, pw[d][1], e_i)
    exit_row = 0 if rev else _SCAN_ROWS - 1

    def tile(q, carry):
        idx = (ntile - 1 - q) if rev else q
        rows = pl.ds(pl.multiple_of(idx * _SCAN_ROWS, _SCAN_ROWS), _SCAN_ROWS)
        y_r, y_i = yr_ref[rows, :], yi_ref[rows, :]
        for sh, k_r, k_i in steps:
            t_r, t_i = _cmul(k_r, k_i, pltpu.roll(y_r, sh, 0), pltpu.roll(y_i, sh, 0))
            y_r, y_i = y_r + t_r, y_i + t_i
        cb_r = jnp.broadcast_to(carry[0], (_SCAN_ROWS, wl))
        cb_i = jnp.broadcast_to(carry[1], (_SCAN_ROWS, wl))
        t_r, t_i = _cmul(e_r, e_i, cb_r, cb_i)
        y_r, y_i = y_r + t_r, y_i + t_i
        yr_ref[rows, :] = y_r
        yi_ref[rows, :] = y_i
        out = (y_r[exit_row:exit_row + 1], y_i[exit_row:exit_row + 1])
        if with_acc:
            n_r = jnp.where(sub == _SCAN_ROWS - 1, cb_r, pltpu.roll(y_r, _SCAN_ROWS - 1, 0))
            n_i = jnp.where(sub == _SCAN_ROWS - 1, cb_i, pltpu.roll(y_i, _SCAN_ROWS - 1, 0))
            s_r, s_i = x_refs[0][rows, :], x_refs[1][rows, :]
            out += (carry[2] + (n_r * s_r + n_i * s_i), carry[3] + (n_i * s_r - n_r * s_i))
        return out

    return lax.fori_loop(0, ntile, tile, (c_r, c_i) + ((zero, zero) if with_acc else ()), unroll=2)


_S5_WL = S5_N // S5_BLOCKS
_S5_CW = D_MODEL // S5_BLOCKS
_TN = (((0,), (0,)), ((), ()))


def _s5_fwd(hb, mats, *, tb=1024):
    s = hb.shape[0]
    tb = min(tb, s)

    def body(h_ref, wbr_ref, wbi_ref, wcr_ref, wci_ref, ar_ref, ai_ref, xr_ref, xi_ref, cp_ref, cr, ci):
        @pl.when(pl.program_id(1) == 0)
        def _():
            cr[...] = jnp.zeros_like(cr)
            ci[...] = jnp.zeros_like(ci)

        hv = h_ref[...]
        xr_ref[...] = jnp.dot(hv, wbr_ref[0], preferred_element_type=F32)
        xi_ref[...] = jnp.dot(hv, wbi_ref[0], preferred_element_type=F32)
        cr[...], ci[...] = _scan_tiles(xr_ref, xi_ref, ar_ref[...], ai_ref[...], cr[...], ci[...], rev=False)
        cp_ref[...] = (jnp.dot(xr_ref[...].astype(BF16), wcr_ref[0], preferred_element_type=F32)
                       + jnp.dot(xi_ref[...].astype(BF16), wci_ref[0], preferred_element_type=F32))

    wide = pl.BlockSpec((tb, _S5_WL), lambda c, t: (t, c))
    narrow = pl.BlockSpec((tb, _S5_CW), lambda c, t: (t, c))
    w_in = pl.BlockSpec((1, _S5_CW, _S5_WL), lambda c, t: (c, 0, 0))
    w_out = pl.BlockSpec((1, _S5_WL, _S5_CW), lambda c, t: (c, 0, 0))
    vec = pl.BlockSpec((1, _S5_WL), lambda c, t: (0, c))
    return pl.pallas_call(
        body, grid=(S5_BLOCKS, s // tb), in_specs=[narrow, w_in, w_in, w_out, w_out, vec, vec],
        out_specs=[wide, wide, narrow],
        out_shape=[jax.ShapeDtypeStruct((s, S5_N), F32)] * 2 + [jax.ShapeDtypeStruct((s, D_MODEL), F32)],
        scratch_shapes=[pltpu.VMEM((1, _S5_WL), F32)] * 2, name="s5_fwd",
        compiler_params=_params(("parallel", "arbitrary")),
    )(hb, mats["wb_re"], mats["wb_im"], mats["wc_re"], mats["wc_imn"], mats["ar"], mats["ai"])


def _s5_bwd(dyb, hb, xr, xi, dhd, mats, *, tb=1024):
    s = hb.shape[0]
    tb = min(tb, s)
    nt = s // tb

    def body(dy_ref, h_ref, xr_ref, xi_ref, dhd_ref, wctr_ref, wcti_ref, wbtr_ref, wbti_ref, ar_ref, ai_ref,
             dh_ref, dar_ref, dai_ref, dwbr_ref, dwbi_ref, dwcr_ref, dwci_ref, lr, li, cr, ci):
        @pl.when(pl.program_id(1) == 0)
        def _():
            for r in (cr, ci, dar_ref, dai_ref, dwbr_ref, dwbi_ref, dwcr_ref, dwci_ref):
                r[...] = jnp.zeros_like(r)

        dyv, hv = dy_ref[...], h_ref[...]
        lr[...] = jnp.dot(dyv, wctr_ref[0], preferred_element_type=F32)
        li[...] = jnp.dot(dyv, wcti_ref[0], preferred_element_type=F32)
        cr[...], ci[...], acc_r, acc_i = _scan_tiles(lr, li, ar_ref[...], ai_ref[...], cr[...], ci[...], rev=True,
                                                     x_refs=(xr_ref, xi_ref))
        dar_ref[...] += jnp.sum(acc_r, axis=0, keepdims=True)
        dai_ref[...] += jnp.sum(acc_i, axis=0, keepdims=True)
        lrb, lib = lr[...].astype(BF16), li[...].astype(BF16)
        dh_ref[...] = (jnp.dot(lrb, wbtr_ref[0], preferred_element_type=F32) + dhd_ref[...]
                       + jnp.dot(lib, wbti_ref[0], preferred_element_type=F32))
        dwbr_ref[0] += lax.dot_general(hv, lrb, _TN, preferred_element_type=F32)
        dwbi_ref[0] += lax.dot_general(hv, lib, _TN, preferred_element_type=F32)
        dwcr_ref[0] += lax.dot_general(dyv, xr_ref[...].astype(BF16), _TN, preferred_element_type=F32)
        dwci_ref[0] += lax.dot_general(dyv, xi_ref[...].astype(BF16), _TN, preferred_element_type=F32)

    wide = pl.BlockSpec((tb, _S5_WL), lambda c, t: (nt - 1 - t, c))
    narrow = pl.BlockSpec((tb, _S5_CW), lambda c, t: (nt - 1 - t, c))
    w_in = pl.BlockSpec((1, _S5_CW, _S5_WL), lambda c, t: (c, 0, 0))
    w_out = pl.BlockSpec((1, _S5_WL, _S5_CW), lambda c, t: (c, 0, 0))
    vec = pl.BlockSpec((1, _S5_WL), lambda c, t: (0, c))
    acc_shape = jax.ShapeDtypeStruct((S5_BLOCKS, _S5_CW, _S5_WL), F32)
    return pl.pallas_call(
        body, grid=(S5_BLOCKS, nt), in_specs=[narrow, narrow, wide, wide, narrow, w_in, w_in, w_out, w_out, vec, vec],
        out_specs=[narrow, vec, vec, w_in, w_in, w_in, w_in],
        out_shape=[jax.ShapeDtypeStruct((s, D_MODEL), F32)] + [jax.ShapeDtypeStruct((1, S5_N), F32)] * 2
        + [acc_shape] * 4,
        scratch_shapes=[pltpu.VMEM((tb, _S5_WL), F32)] * 2 + [pltpu.VMEM((1, _S5_WL), F32)] * 2, name="s5_bwd",
        compiler_params=_params(("parallel", "arbitrary")),
    )(dyb, hb, xr, xi, dhd, mats["wct_re"], mats["wct_imn"], mats["wbt_re"], mats["wbt_im"], mats["ar"], mats["ai"])


def _s5_mats(w, o):
    ab_re, ab_im, bb_re, bb_im = _s5_disc(w["od_a_re"][o], w["od_a_im"][o], w["od_log_dt"][o], w["od_b_re"][o],
                                          w["od_b_im"][o])
    c_re, c_im = w["od_c_re"][o], w["od_c_im"][o]
    tr = lambda t: t.transpose(0, 2, 1)
    bd = lambda t: _blockdiag(t).astype(BF16)
    return {
        "ar": ab_re.reshape(1, S5_N), "ai": ab_im.reshape(1, S5_N),
        "wb_re": bd(tr(bb_re)), "wb_im": bd(tr(bb_im)), "wc_re": bd(tr(c_re)), "wc_imn": bd(-tr(c_im)),
        "wct_re": bd(c_re), "wct_imn": bd(-c_im), "wbt_re": bd(bb_re), "wbt_im": bd(bb_im),
    }


def _odd_fwd(x0, w, o, i):
    mats = _s5_mats(w, o)
    nm = w["norm_mix"][i:i + 1]
    d_row = w["od_d"][o:o + 1]
    hb = _rms_fwd(x0, nm, name="mix_norm")
    xr, xi, cp = _s5_fwd(hb, mats)

    def fn(xv, cpv, gv, dv):
        y = cpv + dv * (xv * _rstd(xv) * gv)
        return y, _gelu(y)

    y, gy = _rw(fn, [x0, cp], [nm, d_row], [(D_MODEL, F32), (D_MODEL, BF16)], tr=512, name="s5_out")
    gg = _mm(gy, w["od_w_glu"][o], tb=True, name="od_glu")
    x1 = _rw(lambda xv, g: xv + g[:, :D_MODEL] * _sigmoid(g[:, D_MODEL:]), [x0, gg], [], [(D_MODEL, F32)], tr=512,
             name="od_glu_out")[0]
    return x1, (hb, xr, xi, y, gy, gg)


def _odd_bwd(dx1, x0, saved, w, o, i):
    hb, xr, xi, y, gy, gg = saved
    mats = _s5_mats(w, o)
    nm = w["norm_mix"][i:i + 1]
    d_row = w["od_d"][o:o + 1]

    def fn_glu(dv, g):
        ga, gb = g[:, :D_MODEL], g[:, D_MODEL:]
        sg = _sigmoid(gb)
        return jnp.concatenate([dv * sg, dv * ga * (sg * (1.0 - sg))], axis=1)

    dgg = _rw(fn_glu, [dx1, gg], [], [(2 * D_MODEL, BF16)], tr=512, name="od_glu_out_bwd")[0]
    d_wglu = _mm(gy, dgg, ta=True, name="od_glu_dw", bn=1024).T
    d_gy = _mm(dgg, w["od_w_glu"][o], name="od_glu_dx")

    def fn_y(dg, yv, xv, gv, dv):
        dy = dg * _gelu_grad(yv)
        h = xv * _rstd(xv) * gv
        return dy, dv * dy, jnp.sum(dy * h, axis=0, keepdims=True)

    dyb, dhd, d_d = _rw(fn_y, [d_gy, y, x0], [nm, d_row], [(D_MODEL, BF16), (D_MODEL, F32)], [(1, D_MODEL)], tr=512,
                        name="s5_out_bwd")
    dh, d_ar, d_ai, d_wb_re, d_wb_im, d_wc_re, d_wc_im = _s5_bwd(dyb, hb, xr, xi, dhd, mats)
    gc, gp = S5_GROUP_CH, S5_STATE
    d_c_re = _blockdiag_extract(d_wc_re, gc, gp)
    d_c_im = -_blockdiag_extract(d_wc_im, gc, gp)
    d_bb_re = _blockdiag_extract(d_wb_re, gc, gp).transpose(0, 2, 1)
    d_bb_im = _blockdiag_extract(d_wb_im, gc, gp).transpose(0, 2, 1)
    dx0, d_nm = _rms_bwd_call(x0, dh, nm, dx1, name="mix_norm_bwd")
    _, vjp = jax.vjp(_s5_disc, w["od_a_re"][o], w["od_a_im"][o], w["od_log_dt"][o], w["od_b_re"][o], w["od_b_im"][o])
    d_a_re, d_a_im, d_log_dt, d_b_re, d_b_im = vjp(
        (d_ar.reshape(S5_GROUPS, S5_STATE), d_ai.reshape(S5_GROUPS, S5_STATE), d_bb_re, d_bb_im))
    grads = {"od_a_re": d_a_re, "od_a_im": d_a_im, "od_log_dt": d_log_dt, "od_b_re": d_b_re, "od_b_im": d_b_im,
             "od_c_re": d_c_re, "od_c_im": d_c_im, "od_d": d_d[0], "od_w_glu": d_wglu, "norm_mix": d_nm[0]}
    return dx0, grads


_HBM = pl.BlockSpec(memory_space=pltpu.HBM)


def _mesh_pos():
    return lax.axis_index("x"), lax.axis_index("y"), lax.axis_index("c")


def _slot(px, py, pc):
    return 4 * px + 2 * py + pc


def _all_gather(x, *, name):
    def body(x_ref, out_ref, send_sems, recv_sems, local_sem):
        mx, my, mc = _mesh_pos()
        me, sibling = (mx, my, mc), (mx, my, 1 - mc)
        chips = [(1 - mx, my), (mx, 1 - my), (1 - mx, 1 - my)]

        def copy(k, block, to, src=None):
            dst = out_ref.at[_slot(*block)]
            return pltpu.make_async_remote_copy(
                src_ref=dst if src is None else src, dst_ref=dst, send_sem=send_sems.at[k], recv_sem=recv_sems.at[k],
                device_id=to, device_id_type=MESH)

        mine = pltpu.make_async_copy(x_ref, out_ref.at[_slot(*me)], local_sem)
        mine.start()
        first = [copy(0, me, sibling, src=x_ref)]
        first += [copy(1 + j, me, (*chip, mc), src=x_ref) for j, chip in enumerate(chips)]
        for cp in first:
            cp.start()
        passed = [copy(4 + j, (*chip, mc), sibling) for j, chip in enumerate(chips)]
        for j, chip in enumerate(chips):
            copy(1 + j, (*chip, mc), me).wait_recv()
            passed[j].start()
        copy(0, sibling, me).wait_recv()
        for j, chip in enumerate(chips):
            copy(4 + j, (*chip, 1 - mc), me).wait_recv()
        for cp in first + passed:
            cp.wait_send()
        mine.wait()

    return pl.pallas_call(
        body, out_shape=jax.ShapeDtypeStruct((N_DEV,) + x.shape, x.dtype), in_specs=[_HBM], out_specs=_HBM,
        scratch_shapes=[pltpu.SemaphoreType.DMA((7,)), pltpu.SemaphoreType.DMA((7,)), pltpu.SemaphoreType.DMA],
        name=name,
    )(x)


_SEMS = pltpu.SemaphoreType.DMA((N_DEV - 1,))
_SEM_SPEC = pl.BlockSpec(memory_space=pltpu.SEMAPHORE)
_SPLIT_COPY = pltpu.CompilerParams(has_side_effects=pltpu.SideEffectType.DATAFLOW_SIDE_EFFECTING)


def _exchange_copies(x_refs, land_refs, send_sems, recv_sems, gather):
    mx, my, mc = _mesh_pos()
    my_slot = _slot(mx, my, mc)
    copies = []
    for a, (x_ref, land_ref) in enumerate(zip(x_refs, land_refs)):
        for k in range(1, N_DEV):
            peer = (1 - mx if k & 4 else mx, 1 - my if k & 2 else my, 1 - mc if k & 1 else mc)
            copies.append(pltpu.make_async_remote_copy(
                src_ref=x_ref if gather else x_ref.at[_slot(*peer)], dst_ref=land_ref.at[my_slot],
                send_sem=send_sems[a].at[k - 1], recv_sem=recv_sems[a].at[k - 1], device_id=peer, device_id_type=MESH))
    return copies


def _exchange_start(xs, carry, *, gather, name):
    n = len(xs)

    def body(*refs):
        outs = refs[2 * n + 1:]
        for cp in _exchange_copies(refs[:n], refs[n:2 * n], outs[:n], outs[n:2 * n], gather):
            cp.start()

    hbm = lambda a: pltpu.HBM(a.shape, a.dtype)
    lands = [pltpu.with_memory_space_constraint(lax.empty(((N_DEV,) if gather else ()) + x.shape, x.dtype), pltpu.HBM)
             for x in xs]
    ins = [pltpu.with_memory_space_constraint(a, pltpu.HBM) for a in list(xs) + [carry]]
    res = pl.pallas_call(
        body, name=name, in_specs=[_HBM] * (2 * n + 1), out_specs=[_SEM_SPEC] * (2 * n) + [_HBM] * (2 * n + 1),
        out_shape=[_SEMS] * (2 * n) + [hbm(a) for a in list(xs) + lands] + [hbm(carry)],
        input_output_aliases={**{a: 2 * n + a for a in range(2 * n)}, 2 * n: 4 * n}, compiler_params=_SPLIT_COPY,
    )(*ins[:n], *lands, ins[n])
    return (res[:n], res[n:2 * n], res[2 * n:3 * n], res[3 * n:4 * n]), res[4 * n]


def _exchange_wait(handles, after, *, gather, name):
    send_sems, recv_sems, xs, lands = handles
    n = len(xs)
    after = list(after) if isinstance(after, (list, tuple)) else [after]

    def body(*refs):
        for cp in _exchange_copies(refs[:n], refs[n:2 * n], refs[2 * n:3 * n], refs[3 * n:4 * n], gather):
            cp.wait_send()
            cp.wait_recv()

    hbm = lambda a: pltpu.HBM(a.shape, a.dtype)
    res = pl.pallas_call(
        body, name=name, in_specs=[_HBM] * (2 * n) + [_SEM_SPEC] * (2 * n) + [_ANY] * len(after),
        out_specs=[_HBM] * (2 * n), out_shape=[hbm(a) for a in list(xs) + list(lands)],
        input_output_aliases={a: a for a in range(2 * n)}, compiler_params=_SPLIT_COPY,
    )(*xs, *lands, *send_sems, *recv_sems, *after)
    return res[:n], res[n:]


_ADAMW_PARTS_BLOCK_BYTES = 8 * 1024 * 1024


def _adamw(parts, w, m, v, *, name, own=None):
    r, c = w.shape
    tr = r
    for cand in range(8, r + 1, 8):
        if r % cand == 0 and N_DEV * cand * c * 4 <= _ADAMW_PARTS_BLOCK_BYTES:
            tr = cand
    has_own = own is not None

    def body(*refs):
        p_ref, w_ref, m_ref, v_ref = refs[0], refs[1 + has_own], refs[2 + has_own], refs[3 + has_own]
        g_ref, d_ref, nm_ref, nv_ref = refs[4 + has_own:]
        if has_own:
            me = _slot(*_mesh_pos())
            part = lambda k: jnp.where(me == k, refs[1][...], p_ref[k])
        else:
            part = lambda k: p_ref[k]
        g = part(0)
        for k in range(1, N_DEV):
            g = g + part(k)
        nm = ADAM_B1 * m_ref[...] + (1.0 - ADAM_B1) * g
        nv = ADAM_B2 * v_ref[...] + (1.0 - ADAM_B2) * (g * g)
        m_hat = nm / (1.0 - ADAM_B1 ** ADAM_STEP)
        v_hat = nv / (1.0 - ADAM_B2 ** ADAM_STEP)
        g_ref[...] = g
        d_ref[...] = -ADAM_LR * (m_hat / (jnp.sqrt(v_hat) + ADAM_EPS) + ADAM_WD * w_ref[...])
        nm_ref[...] = nm
        nv_ref[...] = nv

    blk = pl.BlockSpec((tr, c), lambda i: (i, 0))
    return pl.pallas_call(
        body, grid=(r // tr,),
        in_specs=[pl.BlockSpec((N_DEV, tr, c), lambda i: (0, i, 0))] + [blk] * (3 + has_own),
        out_specs=[blk] * 4, out_shape=[jax.ShapeDtypeStruct((r, c), F32)] * 4, name=name,
        compiler_params=_params(("parallel",)),
    )(parts, *([own] if has_own else []), w, m, v)


_FLAT_COLS = 1024
_FLAT_ROW_ALIGN = 128

_SHARD_AXIS = {
    "norm_mix": None, "norm_ffn": None, "ev_w_in": 2, "ev_b_fgate": None, "ev_q_norm": None, "ev_k_norm": None,
    "ev_v_norm": None, "ev_w_spatial": None, "ev_b_spatial": None, "ev_w_out": 1, "od_a_re": None, "od_a_im": None,
    "od_log_dt": None, "od_b_re": None, "od_b_im": None, "od_c_re": None, "od_c_im": None, "od_d": 1, "od_w_glu": 2,
    "ffn_w_up": 2, "ffn_conv_w": 2, "ffn_conv_b": None, "ffn_w_down": 1, "ple_w_proj": 2, "ple_w_gate": 1,
}
_WEIGHTS = list(_SHARD_AXIS)
_REPLICATED = [n for n in _WEIGHTS if _SHARD_AXIS[n] is None]
_COL_SHARDED = ["ev_w_in", "od_w_glu", "ffn_w_up", "ple_w_proj"]
_ROW_SHARDED = ["ev_w_out", "ffn_w_down", "ple_w_gate"]
_SMALL_SHARDED = ["od_d", "ffn_conv_w"]


def _pack(arrays, lead=()):
    nl = len(lead)
    flat = jnp.concatenate([a.reshape(lead + (-1,)) for a in arrays], axis=nl)
    n = flat.shape[nl]
    chunk = _FLAT_COLS * _FLAT_ROW_ALIGN
    total = -(-n // chunk) * chunk
    flat = jnp.pad(flat, [(0, 0)] * nl + [(0, total - n)])
    return flat.reshape(lead + (total // _FLAT_COLS, _FLAT_COLS))


def _unpack(buf, shapes, lead=()):
    nl = len(lead)
    flat = buf.reshape(lead + (-1,))
    out, off = [], 0
    for shp in shapes:
        n = math.prod(shp)
        out.append(lax.slice_in_dim(flat, off, off + n, axis=nl).reshape(lead + tuple(shp)))
        off += n
    return out


def _join_shards(g, axis):
    g = jnp.moveaxis(g, 0, axis)
    shp = g.shape
    return g.reshape(shp[:axis] + (shp[axis] * shp[axis + 1],) + shp[axis + 2:])


def _split_shards(full, axis):
    shp = full.shape
    g = full.reshape(shp[:axis] + (N_DEV, shp[axis] // N_DEV) + shp[axis + 1:])
    return jnp.moveaxis(g, axis, 0)


def _local_step(x, p, tgt, w, before_part, on_grads):
    saved = []
    h = x
    for i in range(DEPTH):
        x0 = before_part(i, 0, h)
        if i % 2 == 0:
            x1, sv_mix = _even_fwd(x0, w, i // 2, i)
        else:
            x1, sv_mix = _odd_fwd(x0, w, i // 2, i)
        x1 = before_part(i, 1, x1)
        x2, sv_ffn = _ffn_fwd(x1, w, i)
        x3, sv_ple = _ple_fwd(x2, p[i], w, i)
        saved.append((x0, x1, x2, sv_mix, sv_ffn, sv_ple))
        h = x3
    loss, dh = _loss_and_grad(h, tgt)
    per_layer = {}
    for i in reversed(range(DEPTH)):
        x0, x1, x2, sv_mix, sv_ffn, sv_ple = saved[i]
        dx2, dx2b, g_ple = _ple_bwd(dh, x2, p[i], sv_ple, w, i)
        dx1, dx1b, g_ffn = _ffn_bwd(dx2, dx2b, x1, sv_ffn, w, i)
        dx1 = on_grads(i, 0, {**g_ple, **g_ffn}, dx1)
        if i % 2 == 0:
            dh, g_mix = _even_bwd(dx1, dx1b, x0, sv_mix, w, i // 2, i)
        else:
            dh, g_mix = _odd_bwd(dx1, x0, sv_mix, w, i // 2, i)
        dh = on_grads(i, 1, g_mix, dh)
        layer_grads = {**g_ple, **g_ffn, **g_mix}
        for name, g in layer_grads.items():
            per_layer.setdefault(name, []).append(g)
    grads = {name: gs[::-1] for name, gs in per_layer.items()}
    return loss, dh, grads


def kernel(x, p, norm_mix, norm_ffn, ev_w_in, ev_b_fgate, ev_q_norm, ev_k_norm, ev_v_norm, ev_w_spatial, ev_b_spatial, ev_w_out, od_a_re, od_a_im, od_log_dt, od_b_re, od_b_im, od_c_re, od_c_im, od_d, od_w_glu, ffn_w_up, ffn_conv_w, ffn_conv_b, ffn_w_down, ple_w_proj, ple_w_gate, loss_target, m_norm_mix, m_norm_ffn, m_ev_w_in, m_ev_b_fgate, m_ev_q_norm, m_ev_k_norm, m_ev_v_norm, m_ev_w_spatial, m_ev_b_spatial, m_ev_w_out, m_od_a_re, m_od_a_im, m_od_log_dt, m_od_b_re, m_od_b_im, m_od_c_re, m_od_c_im, m_od_d, m_od_w_glu, m_ffn_w_up, m_ffn_conv_w, m_ffn_conv_b, m_ffn_w_down, m_ple_w_proj, m_ple_w_gate, v_norm_mix, v_norm_ffn, v_ev_w_in, v_ev_b_fgate, v_ev_q_norm, v_ev_k_norm, v_ev_v_norm, v_ev_w_spatial, v_ev_b_spatial, v_ev_w_out, v_od_a_re, v_od_a_im, v_od_log_dt, v_od_b_re, v_od_b_im, v_od_c_re, v_od_c_im, v_od_d, v_od_w_glu, v_ffn_w_up, v_ffn_conv_w, v_ffn_conv_b, v_ffn_w_down, v_ple_w_proj, v_ple_w_gate):
    args = locals()
    wl = {n: args[n] for n in _WEIGHTS}
    ml = {n: args["m_" + n] for n in _WEIGHTS}
    vl = {n: args["v_" + n] for n in _WEIGHTS}

    kinds = ("g", "d", "m", "v")

    def local(n, layer, a):
        return a[n][layer].T if n in _COL_SHARDED else a[n][layer]

    me = _slot(*_mesh_pos())
    w = {n: wl[n] for n in _REPLICATED}
    small = _all_gather(_pack([wl[n] for n in _SMALL_SHARDED]), name="gather_small_weights")
    for n, g in zip(_SMALL_SHARDED, _unpack(small, [wl[n].shape for n in _SMALL_SHARDED], lead=(N_DEV,))):
        w[n] = _join_shards(g, _SHARD_AXIS[n])

    def layer_of(n, i):
        return i if wl[n].shape[0] == DEPTH else i // 2

    def weights_of(i, part):
        if part == 0:
            return ["ev_w_in", "ev_w_out"] if i % 2 == 0 else ["od_w_glu"]
        return ["ffn_w_up", "ffn_w_down", "ple_w_proj", "ple_w_gate"]

    gathers = {}
    for n in _COL_SHARDED + _ROW_SHARDED:
        w[n] = {}

    def start_gather(i, carry):
        for part in (0, 1):
            blocks = [local(n, layer_of(n, i), wl).astype(BF16) for n in weights_of(i, part)]
            gathers[i, part], carry = _exchange_start(blocks, carry, gather=True,
                                                      name="gather_start_layer%d_part%d" % (i, part))
        return carry

    def finish_gather(i, part, h):
        sent, landed = _exchange_wait(gathers[i, part], h, gather=True, name="gather_wait_layer%d_part%d" % (i, part))
        for n, block, land in zip(weights_of(i, part), sent, landed):
            full = lax.dynamic_update_slice(land, block[None], (me, 0, 0)).reshape(N_DEV * block.shape[0], block.shape[1])
            if n == "ev_w_in":
                full = jnp.pad(full, ((0, IN_COLS_PAD - IN_COLS), (0, 0)))
            w[n][layer_of(n, i)] = full
        if (i, part) == (0, 0):
            for later in range(1, DEPTH):
                h = start_gather(later, h)
        return h

    x_in = start_gather(0, x[0])

    in_flight = {}

    def start_scatter(i, part, part_grads, dh):
        names = [n for n in _COL_SHARDED + _ROW_SHARDED if n in part_grads]
        xs = [part_grads[n].reshape(N_DEV, part_grads[n].shape[0] // N_DEV, part_grads[n].shape[1]) for n in names]
        handles, dh = _exchange_start(xs, dh, gather=False, name="scatter_start_layer%d_part%d" % (i, part))
        in_flight[i, part] = (names, handles)
        for n in rep_early:
            if n in part_grads:
                s5_grads.setdefault(n, {})[i // 2] = part_grads[n]
        if (i, part) == (1, 1):
            send = _pack([jnp.stack([s5_grads[n][o] for o in sorted(s5_grads[n])]) for n in rep_early])
            s5_handles[0], dh = _exchange_start([send], dh, gather=True, name="gather_start_s5_grads")
        return dh

    rep_early = [n for n in _REPLICATED if n.startswith("od_")]
    rep_late = [n for n in _REPLICATED if not n.startswith("od_")]
    s5_grads, s5_handles = {}, [None]
    loss, grad_x, grads = _local_step(x_in, p[:, 0], loss_target[0], w, finish_gather, start_scatter)
    loss = lax.psum(loss, ("x", "y", "c"))

    rep_send = _pack([jnp.stack(grads[n]) for n in rep_late])
    small_send = _pack([_split_shards(jnp.stack(grads[n]), _SHARD_AXIS[n]) for n in _SMALL_SHARDED], lead=(N_DEV,))
    rep_handles, grad_x = _exchange_start([rep_send], grad_x, gather=True, name="gather_start_replicated_grads")
    small_handles, grad_x = _exchange_start([small_send], grad_x, gather=False, name="scatter_start_small_grads")

    out = {kind: {} for kind in kinds}
    done = {}
    behind = [grad_x]
    for i, part in sorted(in_flight, key=lambda ip: (-ip[0], ip[1])):
        names, handles = in_flight[i, part]
        sent, landed = _exchange_wait(handles, behind, gather=False, name="scatter_wait_layer%d_part%d" % (i, part))
        behind = []
        for n, x_sent, parts in zip(names, sent, landed):
            layer = layer_of(n, i)
            own = lax.dynamic_index_in_dim(x_sent, me, axis=0, keepdims=False)
            res = _adamw(parts, local(n, layer, wl), local(n, layer, ml), local(n, layer, vl), name="adamw_" + n, own=own)
            behind.append(res[0])
            done.setdefault(n, {})[layer] = [r.T if n in _COL_SHARDED else r for r in res]
    for n, layers in done.items():
        for k, kind in enumerate(kinds):
            out[kind][n] = jnp.stack([layers[layer][k] for layer in sorted(layers)])

    last = behind
    (small_sent,), (parts,) = _exchange_wait(small_handles, last, gather=False, name="scatter_wait_small_grads")
    shapes = [wl[n].shape for n in _SMALL_SHARDED]
    res = _adamw(parts, _pack([wl[n] for n in _SMALL_SHARDED]), _pack([ml[n] for n in _SMALL_SHARDED]),
                 _pack([vl[n] for n in _SMALL_SHARDED]), name="adamw_small",
                 own=lax.dynamic_index_in_dim(small_sent, me, axis=0, keepdims=False))
    for kind, buf in zip(kinds, res):
        out[kind].update(zip(_SMALL_SHARDED, _unpack(buf, shapes)))

    for group, handles, tag in ((rep_early, s5_handles[0], "s5"), (rep_late, rep_handles, "replicated")):
        (rep_sent,), (parts,) = _exchange_wait(handles, last, gather=True, name="gather_wait_%s_grads" % tag)
        shapes = [wl[n].shape for n in group]
        res = _adamw(parts, _pack([wl[n] for n in group]), _pack([ml[n] for n in group]),
                     _pack([vl[n] for n in group]), name="adamw_" + tag, own=rep_sent)
        for kind, buf in zip(kinds, res):
            out[kind].update(zip(group, _unpack(buf, shapes)))

    return (loss, grad_x[None], *[out["g"][n] for n in _WEIGHTS], *[out["d"][n] for n in _WEIGHTS],
            *[out["m"][n] for n in _WEIGHTS], *[out["v"][n] for n in _WEIGHTS])
```

```python
import functools
import math

import jax
import jax.numpy as jnp
from jax import lax
from jax.experimental import pallas as pl
from jax.experimental.pallas import tpu as pltpu

F32 = jnp.float32
BF16 = jnp.bfloat16

D_MODEL = 1024
DEPTH = 4
A_GROUPS = 4
A_CHUNK = 128
A_WIDTH = 512
B_HEADS = 8
B_HEAD_DIM = 64
B_WIDTH = 512
IN_COLS = 2 * A_WIDTH + 3 * B_WIDTH + B_HEADS
IN_COLS_PAD = 2688
S5_GROUP_CH = 16
S5_GROUPS = 64
S5_STATE = 64
S5_N = S5_GROUPS * S5_STATE
S5_BLOCKS = 8
D_FF = 2816
PLE_DIM = 256
EPS = 1e-6
NEG_INF = -1e30
N_DEV = 8

ADAM_LR = 0.001
ADAM_B1 = 0.9
ADAM_B2 = 0.999
ADAM_EPS = 1e-08
ADAM_WD = 0.01
ADAM_STEP = 10

VMEM_LIMIT_BYTES = 56 * 1024 * 1024
MESH = pl.DeviceIdType.MESH


def _params(sem, vmem=VMEM_LIMIT_BYTES):
    return pltpu.CompilerParams(dimension_semantics=sem, vmem_limit_bytes=vmem)


_GELU_K = math.sqrt(2.0 / math.pi)


def _gelu(x):
    return x * (0.5 * (1.0 + jnp.tanh(_GELU_K * (x + 0.044715 * (x * x * x)))))


def _gelu_grad(x):
    t = jnp.tanh(_GELU_K * (x + 0.044715 * (x * x * x)))
    return 0.5 * (1.0 + t) + 0.5 * x * (1.0 - t * t) * (_GELU_K * (1.0 + 3.0 * 0.044715 * (x * x)))


def _sigmoid(x):
    return 0.5 * jnp.tanh(0.5 * x) + 0.5


def _rstd(x):
    return lax.rsqrt(jnp.mean(x * x, axis=-1, keepdims=True) + EPS)


def _rms_bwd(x, dy, g):
    r = _rstd(x)
    xh = x * r
    dyg = dy if g is None else dy * g
    dx = r * (dyg - xh * jnp.mean(dyg * xh, axis=-1, keepdims=True))
    return dx, dy * xh


def _shift_down(blk, halo, k):
    tr = blk.shape[0]
    r = pltpu.roll(blk, k, 0)
    hr = pltpu.roll(halo, k, 0)
    first = jnp.where(lax.broadcasted_iota(jnp.int32, hr.shape, 0) < k, hr, r[0:8])
    return jnp.concatenate([first, r[8:tr]], axis=0)


def _rw(fn, rows, consts, outs, accs=(), *, tr, name, prev=(), nxt=(), widths=None):
    s = rows[0].shape[0]
    tr = min(tr, s)
    n = s // tr
    nr, nc, npv, nnx, no, na = len(rows), len(consts), len(prev), len(nxt), len(outs), len(accs)
    widths = widths or [None] * nr

    def body(*refs):
        ins, out_refs = refs[:nr + nc + npv + nnx], refs[nr + nc + npv + nnx:]
        i = pl.program_id(0)
        vals = [r[...] for r in ins[:nr + nc]]
        vals += [jnp.where(i == 0, 0.0, r[...]) for r in ins[nr + nc:nr + nc + npv]]
        vals += [jnp.where(i == n - 1, 0.0, r[...]) for r in ins[nr + nc + npv:]]
        res = fn(*vals)
        if not isinstance(res, (tuple, list)):
            res = (res,)
        for k in range(no):
            out_refs[k][...] = res[k].astype(out_refs[k].dtype)
        if na:
            @pl.when(i == 0)
            def _():
                for k in range(na):
                    out_refs[no + k][...] = jnp.zeros_like(out_refs[no + k])

            for k in range(na):
                out_refs[no + k][...] += res[no + k]

    in_specs = []
    for a, w in zip(rows, widths):
        if w is None:
            in_specs.append(pl.BlockSpec((tr, a.shape[1]), lambda i: (i, 0)))
        else:
            in_specs.append(pl.BlockSpec((tr, w[1]), functools.partial(lambda i, cb: (i, cb), cb=w[0])))
    for c in consts:
        in_specs.append(pl.BlockSpec(c.shape, functools.partial(lambda i, nd: (0,) * nd, nd=c.ndim)))
    t8 = tr // 8
    for a in prev:
        in_specs.append(pl.BlockSpec((8, a.shape[1]), lambda i: (jnp.maximum(i * t8 - 1, 0), 0)))
    for a in nxt:
        in_specs.append(pl.BlockSpec((8, a.shape[1]), lambda i: (jnp.minimum((i + 1) * t8, s // 8 - 1), 0)))
    out_shape = [jax.ShapeDtypeStruct((s, w), dt) for w, dt in outs]
    out_specs = [pl.BlockSpec((tr, w), lambda i: (i, 0)) for w, _ in outs]
    out_shape += [jax.ShapeDtypeStruct(a, F32) for a in accs]
    out_specs += [pl.BlockSpec(a, lambda i: (0, 0)) for a in accs]
    res = pl.pallas_call(
        body, grid=(n,), in_specs=in_specs, out_specs=out_specs, out_shape=out_shape, name=name,
        compiler_params=_params(("arbitrary",) if na else ("parallel",)),
    )(*rows, *consts, *prev, *nxt)
    return res


def _pick(n, cap):
    if n <= cap:
        return n
    best = None
    for d in range(128, cap + 1, 128):
        if n % d == 0:
            best = d
    assert best is not None, (n, cap)
    return best


def _mm(a, b, *, ta=False, tb=False, out_dtype=F32, add=None, epilogue=None, name, bm=1024, bn=512, bk=None):
    (k_dim, m) = a.shape if ta else a.shape[::-1]
    n = b.shape[0] if tb else b.shape[1]
    assert (b.shape[1] if tb else b.shape[0]) == k_dim
    if bk is None:
        bk = 1024 if ta else 2816
    bm, bn, bk = _pick(m, bm), _pick(n, bn), _pick(k_dim, bk)
    nk = k_dim // bk
    dims = (((0 if ta else 1,), (1 if tb else 0,)), ((), ()))
    assert add is None or epilogue is None
    epi_fn, extra, out_dtypes, vecs, n_sums = (tuple(epilogue or (None, [add] if add is not None else [], (out_dtype,)))
                                               + ((), 0))[:5]
    n_extra, n_vec, n_out = len(extra), len(vecs), len(out_dtypes)

    def body(*refs):
        a_ref, b_ref = refs[0], refs[1]
        extra_refs = refs[2:2 + n_extra + n_vec]
        out_refs = refs[2 + n_extra + n_vec:2 + n_extra + n_vec + n_out + n_sums]
        p = lax.dot_general(a_ref[...].astype(BF16), b_ref[...].astype(BF16), dims, preferred_element_type=F32)

        def finish(acc):
            if epi_fn is not None:
                res = epi_fn(acc, *[r[...] for r in extra_refs])
            else:
                res = (acc + extra_refs[0][...],) if n_extra else (acc,)
            for o_ref, r in zip(out_refs, res):
                o_ref[...] = r.reshape(o_ref.shape).astype(o_ref.dtype)

        if nk == 1:
            finish(p)
        else:
            acc_ref = refs[2 + n_extra + n_vec + n_out + n_sums]
            kk = pl.program_id(2)

            @pl.when(kk == 0)
            def _():
                acc_ref[...] = p

            @pl.when(kk > 0)
            def _():
                acc_ref[...] += p

            @pl.when(kk == nk - 1)
            def _():
                finish(acc_ref[...])

    a_spec = pl.BlockSpec((bk, bm), lambda i, j, k: (k, i)) if ta else pl.BlockSpec((bm, bk), lambda i, j, k: (i, k))
    b_spec = pl.BlockSpec((bn, bk), lambda i, j, k: (j, k)) if tb else pl.BlockSpec((bk, bn), lambda i, j, k: (k, j))
    tile = pl.BlockSpec((bm, bn), lambda i, j, k: (i, j))
    vec = pl.BlockSpec((1, bn), lambda i, j, k: (0, j))
    sums = pl.BlockSpec((1, 1, bn), lambda i, j, k: (i, 0, j))
    res = pl.pallas_call(
        body, grid=(m // bm, n // bn, nk), in_specs=[a_spec, b_spec] + [tile] * n_extra + [vec] * n_vec,
        out_specs=[tile] * n_out + [sums] * n_sums,
        out_shape=[jax.ShapeDtypeStruct((m, n), dt) for dt in out_dtypes]
        + [jax.ShapeDtypeStruct((m // bm, 1, n), F32)] * n_sums,
        scratch_shapes=[pltpu.VMEM((bm, bn), F32)] if nk > 1 else [],
        name=name, compiler_params=_params(("parallel", "parallel", "arbitrary")),
    )(a, b, *extra, *vecs)
    return res if epilogue is not None else res[0]


def _mm_rms_bwd(a, b, x, g, dx_add, *, tb=False, name, bf16_copy=False):
    n = x.shape[1]
    has_g = g is not None

    def fn(acc, xv, addv, *gv):
        dx, dg = _rms_bwd(xv, acc, gv[0] if has_g else None)
        dx = dx + addv
        return (dx,) + ((dx,) if bf16_copy else ()) + ((jnp.sum(dg, axis=0, keepdims=True),) if has_g else ())

    outs = (F32,) + ((BF16,) if bf16_copy else ())
    res = _mm(a, b, tb=tb, name=name, bm=512, bn=n, epilogue=(fn, [x, dx_add], outs, [g] if has_g else [], int(has_g)))
    if has_g:
        return tuple(res[:-1]) + (jnp.sum(res[-1], axis=0),)
    return tuple(res) + (None,)


def _rms_fwd(x, g, *, name):
    if g is None:
        return _rw(lambda xv: xv * _rstd(xv), [x], [], [(x.shape[1], BF16)], tr=512, name=name)[0]
    return _rw(lambda xv, gv: xv * _rstd(xv) * gv, [x], [g], [(x.shape[1], BF16)], tr=512, name=name)[0]


def _rms_bwd_call(x, dy, g, dx_add, *, name, tr=512):
    w = x.shape[1]

    def fn(xv, dyv, addv, gv):
        dx, dg = _rms_bwd(xv, dyv, gv)
        return dx + addv, jnp.sum(dg, axis=0, keepdims=True)

    return _rw(fn, [x, dy, dx_add], [g], [(w, F32)], [(1, w)], tr=tr, name=name)


def _conv_taps(hup, halo):
    return _shift_down(hup, halo, 1), _shift_down(hup, halo, 2)


def _ffn_fwd(x1, w, i):
    h2b = _rms_fwd(x1, w["norm_ffn"][i:i + 1], name="ffn_norm")
    hup = _mm(h2b, w["ffn_w_up"][i], tb=True, name="ffn_up", bn=1408)
    cw, cb = w["ffn_conv_w"][i], w["ffn_conv_b"][i:i + 1]

    def fn(hv, cwv, cbv, halo):
        h1, h2 = _conv_taps(hv, halo)
        hc = cbv + cwv[0:1] * h2
        hc = hc + cwv[1:2] * h1
        hc = hc + cwv[2:3] * hv
        g, up = hc[:, :D_FF], hc[:, D_FF:]
        return (g * _sigmoid(g)) * up

    a = _rw(fn, [hup], [cw, cb], [(D_FF, BF16)], tr=128, name="ffn_conv_gate", prev=[hup])[0]
    x2 = _mm(a, w["ffn_w_down"][i], add=x1, name="ffn_down")
    return x2, (h2b, hup, a)


def _ffn_bwd(dx2, dx2b, x1, saved, w, i):
    h2b, hup, a = saved
    cw, cb = w["ffn_conv_w"][i], w["ffn_conv_b"][i:i + 1]
    da = _mm(dx2b, w["ffn_w_down"][i], tb=True, name="ffn_down_dx", bn=1408)
    d_wdown = _mm(a, dx2b, ta=True, name="ffn_down_dw", bm=1408)

    def fn1(hv, dav, cwv, cbv, h_prev, h_next, da_next):
        tr = hv.shape[0]
        he = jnp.concatenate([hv, h_next], axis=0)
        dae = jnp.concatenate([dav, da_next], axis=0)
        h1, h2 = _conv_taps(he, h_prev)
        hc = cbv + cwv[0:1] * h2
        hc = hc + cwv[1:2] * h1
        hc = hc + cwv[2:3] * he
        g, up = hc[:, :D_FF], hc[:, D_FF:]
        sg = _sigmoid(g)
        d_up = dae * (g * sg)
        d_g = dae * up * (sg * (1.0 + g * (1.0 - sg)))
        d_hce = jnp.concatenate([d_g, d_up], axis=1)
        rows_e = tr + 8
        d_hup = (cwv[2:3] * d_hce[:tr] + cwv[1:2] * pltpu.roll(d_hce, rows_e - 1, 0)[:tr]
                 + cwv[0:1] * pltpu.roll(d_hce, rows_e - 2, 0)[:tr])
        d_hc = d_hce[:tr]
        col = lambda v: jnp.sum(v, axis=0, keepdims=True)
        return d_hup, col(d_hc), col(d_hc * h2[:tr]), col(d_hc * h1[:tr]), col(d_hc * hv)

    w2 = 2 * D_FF
    d_hup, d_cb, d_cw0, d_cw1, d_cw2 = _rw(fn1, [hup, da], [cw, cb], [(w2, BF16)], [(1, w2)] * 4, tr=128,
                                           name="ffn_conv_gate_bwd", prev=[hup], nxt=[hup, da])
    dx1, dx1b, d_norm = _mm_rms_bwd(d_hup, w["ffn_w_up"][i], x1, w["norm_ffn"][i:i + 1], dx2, name="ffn_up_dx",
                                    bf16_copy=True)
    d_wup_t = _mm(h2b, d_hup, ta=True, name="ffn_up_dw", bn=1408).T
    grads = {"ffn_w_up": d_wup_t, "ffn_w_down": d_wdown, "ffn_conv_b": d_cb[0],
             "ffn_conv_w": jnp.concatenate([d_cw0, d_cw1, d_cw2], axis=0), "norm_ffn": d_norm[0]}
    return dx1, dx1b, grads


def _ple_fwd(x2, p_i, w, i):
    rn = _rms_fwd(x2, None, name="ple_norm")
    gl = _mm(rn, w["ple_w_gate"][i], name="ple_gate")
    x3 = _mm(p_i, w["ple_w_proj"][i], tb=True, name="ple_proj",
             epilogue=(lambda pp, xv, g: (xv + _sigmoid(g) * pp,), [x2, gl], (F32,)))[0]
    return x3, (rn, gl)


def _ple_bwd(dx3, x2, p_i, saved, w, i):
    rn, gl = saved

    def fn(pp, dv, g):
        sg = _sigmoid(g)
        return dv * sg, dv * pp * (sg * (1.0 - sg))

    d_pp, d_pre = _mm(p_i, w["ple_w_proj"][i], tb=True, name="ple_proj_bwd", epilogue=(fn, [dx3, gl], (BF16, BF16)))
    d_wproj_t = _mm(p_i, d_pp, ta=True, name="ple_proj_dw", bn=1024).T
    d_wgate = _mm(rn, d_pre, ta=True, name="ple_gate_dw")
    dx2, dx2b, _ = _mm_rms_bwd(d_pre, w["ple_w_gate"][i], x2, None, dx3, tb=True, name="ple_gate_dx", bf16_copy=True)
    return dx2, dx2b, {"ple_w_proj": d_wproj_t, "ple_w_gate": d_wgate}


def _loss_and_grad(xf, tgt):
    def fn(xv, tv):
        d = xv - tv
        part = 0.5 * jnp.sum(jnp.mean(d * d, axis=-1, keepdims=True), axis=0, keepdims=True)
        return d * (1.0 / D_MODEL), part

    dx, loss = _rw(fn, [xf, tgt], [], [(D_MODEL, F32)], [(1, 1)], tr=512, name="loss_head")
    return loss[0, 0], dx


def _tril_mask():
    shp = (A_CHUNK, A_CHUNK)
    return lax.broadcasted_iota(jnp.int32, shp, 0) >= lax.broadcasted_iota(jnp.int32, shp, 1)


def _gmlp_fwd(z, w_s, b_s, v_gain, *, tr=512):
    s = z.shape[0]
    tr = min(tr, s)
    gw = A_WIDTH // A_GROUPS

    def body(z_ref, w_ref, b_ref, g_ref, y_ref):
        tril = _tril_mask()
        for g in range(A_GROUPS):
            wg = jnp.where(tril, w_ref[g], 0.0).astype(BF16)
            cols = slice(g * gw, (g + 1) * gw)
            vcols = slice(A_WIDTH + g * gw, A_WIDTH + (g + 1) * gw)
            for c in range(tr // A_CHUNK):
                rows = slice(c * A_CHUNK, (c + 1) * A_CHUNK)
                v = _gelu(z_ref[rows, vcols])
                vn = v * _rstd(v) * g_ref[:, cols]
                sv = jnp.dot(wg, vn.astype(BF16), preferred_element_type=F32) + b_ref[g]
                y_ref[rows, cols] = (_gelu(z_ref[rows, cols]) * sv).astype(BF16)

    return pl.pallas_call(
        body, grid=(s // tr,),
        in_specs=[pl.BlockSpec((tr, 2 * A_WIDTH), lambda i: (i, 0)),
                  pl.BlockSpec(w_s.shape, lambda i: (0, 0, 0)), pl.BlockSpec(b_s.shape, lambda i: (0, 0, 0)),
                  pl.BlockSpec(v_gain.shape, lambda i: (0, 0))],
        out_specs=pl.BlockSpec((tr, A_WIDTH), lambda i: (i, 0)),
        out_shape=jax.ShapeDtypeStruct((s, A_WIDTH + B_WIDTH), BF16), name="gmlp_fwd",
        compiler_params=_params(("parallel",)),
    )(z, w_s, b_s, v_gain)


def _gmlp_bwd(z, d_ymix, w_s, b_s, v_gain, dz, *, tr=512):
    s = z.shape[0]
    tr = min(tr, s)
    gw = A_WIDTH // A_GROUPS

    def body(z_ref, dy_ref, w_ref, b_ref, g_ref, _, dz_ref, dw_ref, db_ref, dg_ref):
        @pl.when(pl.program_id(0) == 0)
        def _():
            dw_ref[...] = jnp.zeros_like(dw_ref)
            db_ref[...] = jnp.zeros_like(db_ref)
            dg_ref[...] = jnp.zeros_like(dg_ref)

        tril = _tril_mask()
        for g in range(A_GROUPS):
            wg = jnp.where(tril, w_ref[g], 0.0).astype(BF16)
            cols = slice(g * gw, (g + 1) * gw)
            vcols = slice(A_WIDTH + g * gw, A_WIDTH + (g + 1) * gw)
            gain = g_ref[:, cols]
            for c in range(tr // A_CHUNK):
                rows = slice(c * A_CHUNK, (c + 1) * A_CHUNK)
                va, ua = z_ref[rows, vcols], z_ref[rows, cols]
                v = _gelu(va)
                r = _rstd(v)
                vh = v * r
                vnb = (vh * gain).astype(BF16)
                sv = jnp.dot(wg, vnb, preferred_element_type=F32) + b_ref[g]
                dy = dy_ref[rows, cols]
                d_sv = dy * _gelu(ua)
                dz_ref[rows, cols] = (dy * sv * _gelu_grad(ua)).astype(BF16)
                d_svb = d_sv.astype(BF16)
                d_vn = lax.dot_general(wg, d_svb, (((0,), (0,)), ((), ())), preferred_element_type=F32)
                dwp = lax.dot_general(d_svb, vnb, (((1,), (1,)), ((), ())), preferred_element_type=F32)
                dw_ref[g] += jnp.where(tril, dwp, 0.0)
                db_ref[g] += jnp.sum(d_sv, axis=1, keepdims=True)
                dg_ref[:, cols] += jnp.sum(d_vn * vh, axis=0, keepdims=True)
                d_vh = d_vn * gain
                d_v = r * (d_vh - vh * jnp.mean(d_vh * vh, axis=-1, keepdims=True))
                dz_ref[rows, vcols] = (d_v * _gelu_grad(va)).astype(BF16)

    return pl.pallas_call(
        body, grid=(s // tr,),
        in_specs=[pl.BlockSpec((tr, 2 * A_WIDTH), lambda i: (i, 0)), pl.BlockSpec((tr, A_WIDTH), lambda i: (i, 0)),
                  pl.BlockSpec(w_s.shape, lambda i: (0, 0, 0)), pl.BlockSpec(b_s.shape, lambda i: (0, 0, 0)),
                  pl.BlockSpec(v_gain.shape, lambda i: (0, 0)), _ANY],
        out_specs=[pl.BlockSpec((tr, 2 * A_WIDTH), lambda i: (i, 0)), pl.BlockSpec(w_s.shape, lambda i: (0, 0, 0)),
                   pl.BlockSpec(b_s.shape, lambda i: (0, 0, 0)), pl.BlockSpec(v_gain.shape, lambda i: (0, 0))],
        out_shape=[jax.ShapeDtypeStruct(dz.shape, dz.dtype), jax.ShapeDtypeStruct(w_s.shape, F32),
                   jax.ShapeDtypeStruct(b_s.shape, F32), jax.ShapeDtypeStruct(v_gain.shape, F32)],
        input_output_aliases={5: 0}, name="gmlp_bwd", compiler_params=_params(("arbitrary",)),
    )(z, d_ymix, w_s, b_s, v_gain, dz)


def _dot3(x, ub):
    x1 = x.astype(BF16)
    r1 = x - x1.astype(F32)
    x2 = r1.astype(BF16)
    x3 = (r1 - x2.astype(F32)).astype(BF16)
    d = lambda a: jnp.dot(a, ub, preferred_element_type=F32)
    return d(x1) + d(x2) + d(x3)


def _log_sigmoid(x):
    return jnp.minimum(x, 0.0) - jnp.log(1.0 + jnp.exp(-jnp.abs(x)))


def _fgate_fwd(f_t, b_col, *, tb=256):
    h, s = f_t.shape
    tb = min(tb, s)

    def body(f_ref, b_ref, c_ref, carry):
        @pl.when(pl.program_id(0) == 0)
        def _():
            carry[...] = jnp.zeros_like(carry)

        lf = _log_sigmoid(f_ref[...] + b_ref[...])
        shp = (tb, tb)
        upper = (lax.broadcasted_iota(jnp.int32, shp, 0) <= lax.broadcasted_iota(jnp.int32, shp, 1)).astype(BF16)
        c_ref[...] = _dot3(lf, upper) + carry[...]
        carry[...] += jnp.sum(lf, axis=1, keepdims=True)

    return pl.pallas_call(
        body, grid=(s // tb,),
        in_specs=[pl.BlockSpec((h, tb), lambda i: (0, i)), pl.BlockSpec((h, 1), lambda i: (0, 0))],
        out_specs=pl.BlockSpec((h, tb), lambda i: (0, i)), out_shape=jax.ShapeDtypeStruct((h, s), F32),
        scratch_shapes=[pltpu.VMEM((h, 1), F32)], name="fgate_fwd", compiler_params=_params(("arbitrary",)),
    )(f_t, b_col)


def _fgate_bwd(f_t, b_col, dc_a, dc_b, *, tb=256):
    h, s = f_t.shape
    tb = min(tb, s)
    n = s // tb

    def body(f_ref, b_ref, da_ref, db_ref, df_ref, dbias_ref, carry):
        @pl.when(pl.program_id(0) == 0)
        def _():
            carry[...] = jnp.zeros_like(carry)
            dbias_ref[...] = jnp.zeros_like(dbias_ref)

        dc = da_ref[...] + db_ref[...]
        shp = (tb, tb)
        lower = (lax.broadcasted_iota(jnp.int32, shp, 0) >= lax.broadcasted_iota(jnp.int32, shp, 1)).astype(BF16)
        d_lf = _dot3(dc, lower) + carry[...]
        carry[...] += jnp.sum(dc, axis=1, keepdims=True)
        df = d_lf * (1.0 - _sigmoid(f_ref[...] + b_ref[...]))
        df_ref[...] = df
        dbias_ref[...] += jnp.sum(df, axis=1, keepdims=True)

    blk = pl.BlockSpec((h, tb), lambda i: (0, n - 1 - i))
    return pl.pallas_call(
        body, grid=(n,), in_specs=[blk, pl.BlockSpec((h, 1), lambda i: (0, 0)), blk, blk],
        out_specs=[blk, pl.BlockSpec((h, 1), lambda i: (0, 0))],
        out_shape=[jax.ShapeDtypeStruct((h, s), F32), jax.ShapeDtypeStruct((h, 1), F32)],
        scratch_shapes=[pltpu.VMEM((h, 1), F32)], name="fgate_bwd", compiler_params=_params(("arbitrary",)),
    )(f_t, b_col, dc_a, dc_b)


_NT = (((1,), (1,)), ((), ()))


def _causal(shape, row0, col0, transposed):
    r = lax.broadcasted_iota(jnp.int32, shape, 0) + row0
    c = lax.broadcasted_iota(jnp.int32, shape, 1) + col0
    return (r <= c) if transposed else (c <= r)


_ATT_SCALE = B_HEAD_DIM ** -0.5
ATT_W = 128
_COL_CQ, _COL_CK, _COL_LSE, _COL_DELTA = 64, 67, 70, 64
_ATT_BLK = 2048
_ATT_SUB = 512


def _split3(x):
    h = x.astype(BF16).astype(F32)
    r = x - h
    m = r.astype(BF16).astype(F32)
    return h, m, (r - m).astype(BF16).astype(F32)


def _put_cols(base, lane, col0, parts):
    for t, pv in enumerate(parts):
        base = jnp.where(lane == col0 + t, pv, base)
    return base


_HEAD_PAIRS = B_HEADS // 2
_Q_BLOCK, _K_BLOCK, _V_BLOCK = 8, 12, 16
_F_BLOCK = 20
_O_BLOCK = 4
_ANY = pl.BlockSpec(memory_space=pl.ANY)


def _head_half(ref, h):
    x = ref[...]
    x = jnp.where((h & 1) == 1, pltpu.roll(x, B_HEAD_DIM, 1), x)
    return jnp.where(lax.broadcasted_iota(jnp.int32, x.shape, 1) < B_HEAD_DIM, x, 0.0)


def _pair(a, b):
    lane = lax.broadcasted_iota(jnp.int32, a.shape, 1)
    return jnp.where(lane < B_HEAD_DIM, a, pltpu.roll(b, B_HEAD_DIM, 1))


def _attn_prep(z, c_col, gq, gk, *, tr=1024):
    s = z.shape[0]
    tr = min(tr, s)
    n = s // tr

    def body(q_ref, k_ref, v_ref, c_ref, gq_ref, gk_ref, qo_ref, ko_ref, vo_ref):
        h = pl.program_id(0)
        q, k, v = _head_half(q_ref, h), _head_half(k_ref, h), _head_half(v_ref, h)
        lane = lax.broadcasted_iota(jnp.int32, q.shape, 1)
        rq = lax.rsqrt(jnp.sum(q * q, axis=-1, keepdims=True) * (1.0 / B_HEAD_DIM) + EPS)
        rk = lax.rsqrt(jnp.sum(k * k, axis=-1, keepdims=True) * (1.0 / B_HEAD_DIM) + EPS)
        c = c_ref[...]
        ch, cm, cl = _split3(c)
        one = jnp.ones_like(c)
        qq = _put_cols(q * rq * gq_ref[...] * _ATT_SCALE, lane, _COL_CQ, (ch, cm, cl))
        qq = _put_cols(qq, lane, _COL_CK, (one, one, one))
        kk = _put_cols(k * rk * gk_ref[...], lane, _COL_CQ, (one, one, one))
        kk = _put_cols(kk, lane, _COL_CK, (-ch, -cm, -cl))
        kk = _put_cols(kk, lane, _COL_LSE, (one, one, one))
        qo_ref[...] = qq.astype(BF16)
        ko_ref[...] = kk.astype(BF16)
        vo_ref[...] = _put_cols(v, lane, _COL_DELTA, (one, one, one)).astype(BF16)

    def zcol(base):
        return pl.BlockSpec((tr, ATT_W), lambda h, i: (i, base + (h >> 1)))

    rows = pl.BlockSpec((tr, ATT_W), lambda h, i: (h * n + i, 0))
    gain = pl.BlockSpec((1, ATT_W), lambda h, i: (0, 0))
    return pl.pallas_call(
        body, grid=(B_HEADS, n),
        in_specs=[zcol(_Q_BLOCK), zcol(_K_BLOCK), zcol(_V_BLOCK), pl.BlockSpec((tr, 1), lambda h, i: (h * n + i, 0)),
                  gain, gain],
        out_specs=[rows] * 3, out_shape=[jax.ShapeDtypeStruct((B_HEADS * s, ATT_W), BF16)] * 3, name="attn_prep",
        compiler_params=_params(("parallel", "parallel")),
    )(z, z, z, c_col, gq, gk)


def _attn_bwd_prep(d_ymix, o, qp, lse, *, tr=1024):
    s = d_ymix.shape[0]
    tr = min(tr, s)
    n = s // tr

    def body(do_ref, o_ref, q_ref, l_ref, dop_ref, qpp_ref):
        dov = _head_half(do_ref, pl.program_id(0))
        lane = lax.broadcasted_iota(jnp.int32, dov.shape, 1)
        dh, dm, dl = _split3(jnp.sum(dov * o_ref[...], axis=-1, keepdims=True))
        lh, lm, ll = _split3(l_ref[...])
        dop_ref[...] = _put_cols(dov, lane, _COL_DELTA, (-dh, -dm, -dl)).astype(BF16)
        qpp_ref[...] = _put_cols(q_ref[...].astype(F32), lane, _COL_LSE, (-lh, -lm, -ll)).astype(BF16)

    rows = pl.BlockSpec((tr, ATT_W), lambda h, i: (h * n + i, 0))
    return pl.pallas_call(
        body, grid=(B_HEADS, n),
        in_specs=[pl.BlockSpec((tr, ATT_W), lambda h, i: (i, _O_BLOCK + (h >> 1))), rows, rows,
                  pl.BlockSpec((tr, 1), lambda h, i: (h * n + i, 0))],
        out_specs=[rows] * 2, out_shape=[jax.ShapeDtypeStruct((B_HEADS * s, ATT_W), BF16)] * 2,
        name="attn_bwd_prep", compiler_params=_params(("parallel", "parallel")),
    )(d_ymix, o, qp, lse)


def _attn_merge_out(o, ymix, *, tr=1024):
    s = o.shape[1]
    tr = min(tr, s)

    def body(a_ref, b_ref, buf_ref, out_ref):
        out_ref[...] = _pair(a_ref[0], b_ref[0]).astype(BF16)

    return pl.pallas_call(
        body, grid=(_HEAD_PAIRS, s // tr),
        in_specs=[pl.BlockSpec((1, tr, ATT_W), lambda j, i: (2 * j, i, 0)),
                  pl.BlockSpec((1, tr, ATT_W), lambda j, i: (2 * j + 1, i, 0)), _ANY],
        out_specs=pl.BlockSpec((tr, ATT_W), lambda j, i: (i, _O_BLOCK + j)),
        out_shape=jax.ShapeDtypeStruct(ymix.shape, ymix.dtype), input_output_aliases={2: 0}, name="attn_merge_out",
        compiler_params=_params(("parallel", "parallel")),
    )(o, o, ymix)


def _pair_bwd(d_heads, dz, out_block, *, z=None, in_block=None, gain=None, scale=None, name, tr=1024):
    s = d_heads.shape[1]
    tr = min(tr, s)
    norm = gain is not None

    def body(*refs):
        a_ref, b_ref = refs[0], refs[1]
        d = _pair(a_ref[0], b_ref[0])
        if norm:
            x_ref, g_ref, _, out_ref, dg_ref = refs[2:]

            @pl.when((pl.program_id(0) == 0) & (pl.program_id(1) == 0))
            def _():
                dg_ref[...] = jnp.zeros_like(dg_ref)

            x = x_ref[...]
            lo = lax.broadcasted_iota(jnp.int32, x.shape, 1) < B_HEAD_DIM

            def half_mean(v):
                s_lo = jnp.sum(jnp.where(lo, v, 0.0), axis=-1, keepdims=True)
                s_hi = jnp.sum(jnp.where(lo, 0.0, v), axis=-1, keepdims=True)
                return jnp.where(lo, s_lo, s_hi) * (1.0 / B_HEAD_DIM)

            r = lax.rsqrt(half_mean(x * x) + EPS)
            xh = x * r
            dy = d * scale if scale is not None else d
            dyg = dy * g_ref[...]
            d = r * (dyg - xh * half_mean(dyg * xh))
            dg_ref[...] += jnp.sum(dy * xh, axis=0, keepdims=True)
        else:
            out_ref = refs[3]
        out_ref[...] = d.astype(BF16)

    heads = [pl.BlockSpec((1, tr, ATT_W), lambda j, i: (2 * j, i, 0)),
             pl.BlockSpec((1, tr, ATT_W), lambda j, i: (2 * j + 1, i, 0))]
    out_spec = pl.BlockSpec((tr, ATT_W), lambda j, i: (i, out_block + j))
    dz_shape = jax.ShapeDtypeStruct(dz.shape, dz.dtype)
    if norm:
        res = pl.pallas_call(
            body, grid=(_HEAD_PAIRS, s // tr),
            in_specs=heads + [pl.BlockSpec((tr, ATT_W), lambda j, i: (i, in_block + j)),
                              pl.BlockSpec((1, ATT_W), lambda j, i: (0, 0)), _ANY],
            out_specs=[out_spec, pl.BlockSpec((1, ATT_W), lambda j, i: (0, 0))],
            out_shape=[dz_shape, jax.ShapeDtypeStruct((1, ATT_W), F32)], input_output_aliases={4: 0}, name=name,
            compiler_params=_params(("arbitrary", "arbitrary")),
        )(d_heads, d_heads, z, gain, dz)
        return res[0], res[1]
    return pl.pallas_call(
        body, grid=(_HEAD_PAIRS, s // tr), in_specs=heads + [_ANY], out_specs=out_spec, out_shape=dz_shape,
        input_output_aliases={2: 0}, name=name, compiler_params=_params(("parallel", "parallel")),
    )(d_heads, d_heads, dz), None


def _attn_fwd(qp, kp, vp, *, blk=_ATT_BLK):
    h, s, d = qp.shape
    b = min(blk, s)
    n = s // b
    sub = min(_ATT_SUB, b)
    ns = b // sub

    def body(q_ref, k_ref, v_ref, o_ref, lse_ref):
        i = pl.program_id(1)
        qs = [q_ref[0, t * sub:(t + 1) * sub, :] for t in range(ns)]

        def tile(j, state, masked):
            off = pl.multiple_of(j * b, b)
            kk = k_ref[0, pl.ds(off, b), :]
            vv = v_ref[0, pl.ds(off, b), :]
            new = []
            for t in range(ns):
                m, l, acc = state[3 * t:3 * t + 3]
                nk = (t + 1) * sub if masked else b
                sc = lax.dot_general(qs[t], kk[:nk], _NT, preferred_element_type=F32)
                if masked:
                    sc = jnp.where(_causal((sub, nk), t * sub, 0, False), sc, NEG_INF)
                m_new = jnp.maximum(m, jnp.max(sc, axis=1, keepdims=True))
                alpha = jnp.exp(m - m_new)
                p = jnp.exp(sc - m_new)
                new += [m_new, alpha * l + jnp.sum(p, axis=1, keepdims=True),
                        alpha * acc + jnp.dot(p.astype(BF16), vv[:nk], preferred_element_type=F32)]
            return tuple(new)

        init = (jnp.full((sub, 1), NEG_INF, F32), jnp.zeros((sub, 1), F32), jnp.zeros((sub, d), F32)) * ns
        state = lax.fori_loop(0, i, lambda j, st: tile(j, st, False), init)
        state = tile(i, state, True)
        for t in range(ns):
            m, l, acc = state[3 * t:3 * t + 3]
            o_ref[0, t * sub:(t + 1) * sub, :] = acc / l
            lse_ref[0, t * sub:(t + 1) * sub, :] = m + jnp.log(l)

    blk_spec = pl.BlockSpec((1, b, d), lambda hh, i: (hh, i, 0))
    full = pl.BlockSpec((1, s, d), lambda hh, i: (hh, 0, 0))
    return pl.pallas_call(
        body, grid=(h, n), in_specs=[blk_spec, full, full],
        out_specs=[blk_spec, pl.BlockSpec((1, b, 1), lambda hh, i: (hh, i, 0))],
        out_shape=[jax.ShapeDtypeStruct((h, s, d), F32), jax.ShapeDtypeStruct((h, s, 1), F32)],
        name="attn_fwd", compiler_params=_params(("parallel", "arbitrary")),
    )(qp, kp, vp)


_ATT_BWD_BLK = 1024


def _attn_bwd(qpp, kp, vp, dop, *, blk=_ATT_BWD_BLK):
    h, s, d = qpp.shape
    b = min(blk, s)
    n = s // b
    sub = min(_ATT_SUB, b)
    ns = b // sub

    def body(k_ref, v_ref, q_ref, do_ref, dq_ref, dk_ref, dv_ref):
        j = pl.program_id(1)

        @pl.when(j == 0)
        def _():
            dq_ref[...] = jnp.zeros_like(dq_ref)

        ks = [k_ref[0, t * sub:(t + 1) * sub, :] for t in range(ns)]
        vs = [v_ref[0, t * sub:(t + 1) * sub, :] for t in range(ns)]

        def tile(i, state, masked):
            off = pl.multiple_of(i * b, b)
            qq = q_ref[0, pl.ds(off, b), :]
            dd = do_ref[0, pl.ds(off, b), :]
            new = []
            dq = None
            for t in range(ns):
                q0 = t * sub if masked else 0
                qt, dt = qq[q0:], dd[q0:]
                pt = jnp.exp(lax.dot_general(ks[t], qt, _NT, preferred_element_type=F32))
                if masked:
                    pt = jnp.where(_causal((sub, b - q0), q0, q0, True), pt, 0.0)
                dst = (pt * lax.dot_general(vs[t], dt, _NT, preferred_element_type=F32)).astype(BF16)
                new += [state[2 * t] + jnp.dot(dst, qt, preferred_element_type=F32),
                        state[2 * t + 1] + jnp.dot(pt.astype(BF16), dt, preferred_element_type=F32)]
                part = lax.dot_general(dst, ks[t], _TN, preferred_element_type=F32)
                if q0:
                    part = jnp.concatenate([jnp.zeros((q0, d), F32), part], axis=0)
                dq = part if dq is None else dq + part
            dq_ref[0, pl.ds(off, b), :] += dq
            return tuple(new)

        state = tile(j, (jnp.zeros((sub, d), F32),) * (2 * ns), True)
        state = lax.fori_loop(j + 1, n, lambda i, st: tile(i, st, False), state)
        for t in range(ns):
            dk_ref[0, t * sub:(t + 1) * sub, :] = state[2 * t]
            dv_ref[0, t * sub:(t + 1) * sub, :] = state[2 * t + 1]

    blk_spec = pl.BlockSpec((1, b, d), lambda hh, j: (hh, j, 0))
    full = pl.BlockSpec((1, s, d), lambda hh, j: (hh, 0, 0))
    return pl.pallas_call(
        body, grid=(h, n), in_specs=[blk_spec, blk_spec, full, full], out_specs=[full, blk_spec, blk_spec],
        out_shape=[jax.ShapeDtypeStruct((h, s, d), F32)] * 3, name="attn_bwd",
        compiler_params=_params(("parallel", "arbitrary")),
    )(kp, vp, qpp, dop)


def _pad_head(g):
    return jnp.pad(g, ((0, 0), (0, ATT_W - B_HEAD_DIM)))


def _even_fwd(x0, w, e, i):
    s = x0.shape[0]
    hs = B_HEADS * s
    hb = _rms_fwd(x0, w["norm_mix"][i:i + 1], name="mix_norm")
    z = _mm(hb, w["ev_w_in"][e], tb=True, name="ev_in", bn=896)
    b_s = w["ev_b_spatial"][e][:, :, None]
    v_gain = w["ev_v_norm"][e:e + 1]
    ymix = _gmlp_fwd(z, w["ev_w_spatial"][e], b_s, v_gain)
    f_t = z[:, IN_COLS - B_HEADS:IN_COLS].T
    c = _fgate_fwd(f_t, w["ev_b_fgate"][e][:, None])
    qp, kp, vp = _attn_prep(z, c.reshape(hs, 1), _pad_head(w["ev_q_norm"][e:e + 1]),
                            _pad_head(w["ev_k_norm"][e:e + 1]))
    shp = (B_HEADS, s, ATT_W)
    o, lse = _attn_fwd(qp.reshape(shp), kp.reshape(shp), vp.reshape(shp))
    ymix = _attn_merge_out(o, ymix)
    x1 = _mm(ymix, w["ev_w_out"][e], add=x0, name="ev_out")
    return x1, (hb, z, ymix, qp, kp, vp, f_t, o, lse)


def _even_bwd(dx1, dx1b, x0, saved, w, e, i):
    hb, z, ymix, qp, kp, vp, f_t, o, lse = saved
    s = x0.shape[0]
    hs = B_HEADS * s
    d_ymix = _mm(dx1b, w["ev_w_out"][e], tb=True, name="ev_out_dx")
    d_wout = _mm(ymix, dx1b, ta=True, name="ev_out_dw")
    dop, qpp = _attn_bwd_prep(d_ymix, o.reshape(hs, ATT_W), qp, lse.reshape(hs, 1))
    shp = (B_HEADS, s, ATT_W)
    qpp, dop, kp3, vp3 = qpp.reshape(shp), dop.reshape(shp), kp.reshape(shp), vp.reshape(shp)
    dqp, dkp, dvp = _attn_bwd(qpp, kp3, vp3, dop)
    d_ft, d_bf = _fgate_bwd(f_t, w["ev_b_fgate"][e][:, None], dqp[:, :, _COL_CQ], -dkp[:, :, _COL_CK])
    dz = jnp.pad(d_ft.T.astype(BF16), ((0, 0), (_F_BLOCK * ATT_W, IN_COLS_PAD - IN_COLS)))
    b_s = w["ev_b_spatial"][e][:, :, None]
    v_gain = w["ev_v_norm"][e:e + 1]
    dz, d_ws, d_bs, d_vg = _gmlp_bwd(z, d_ymix, w["ev_w_spatial"][e], b_s, v_gain, dz)
    twice = lambda g: jnp.concatenate([g, g], axis=1)
    dz, d_gq = _pair_bwd(dqp, dz, _Q_BLOCK, z=z, in_block=_Q_BLOCK, gain=twice(w["ev_q_norm"][e:e + 1]),
                         scale=_ATT_SCALE, name="attn_qnorm_bwd")
    dz, d_gk = _pair_bwd(dkp, dz, _K_BLOCK, z=z, in_block=_K_BLOCK, gain=twice(w["ev_k_norm"][e:e + 1]),
                         name="attn_knorm_bwd")
    dz, _ = _pair_bwd(dvp, dz, _V_BLOCK, name="attn_dv_out")
    dx0, d_nm = _mm_rms_bwd(dz, w["ev_w_in"][e], x0, w["norm_mix"][i:i + 1], dx1, name="ev_in_dx")
    d_win_t = _mm(hb, dz, ta=True, name="ev_in_dw", bn=1344)[:, :IN_COLS].T
    hd = B_HEAD_DIM
    grads = {"ev_w_in": d_win_t, "ev_w_out": d_wout, "ev_b_fgate": d_bf[:, 0],
             "ev_q_norm": d_gq[0, :hd] + d_gq[0, hd:], "ev_k_norm": d_gk[0, :hd] + d_gk[0, hd:], "ev_v_norm": d_vg[0],
             "ev_w_spatial": d_ws, "ev_b_spatial": d_bs[:, :, 0], "norm_mix": d_nm[0]}
    return dx0, grads


def _s5_disc(a_re, a_im, log_dt, b_re, b_im):
    dt = jnp.exp(log_dt)[:, None]
    lr, li = a_re, a_im
    mag = jnp.exp(lr * dt)
    ab_re, ab_im = mag * jnp.cos(li * dt), mag * jnp.sin(li * dt)
    den = lr * lr + li * li
    nr, ni = ab_re - 1.0, ab_im
    cr = (nr * lr + ni * li) / den
    ci = (ni * lr - nr * li) / den
    bb_re = cr[..., None] * b_re - ci[..., None] * b_im
    bb_im = cr[..., None] * b_im + ci[..., None] * b_re
    return ab_re, ab_im, bb_re, bb_im


_GPB = S5_GROUPS // S5_BLOCKS


def _blockdiag(t):
    a, b = t.shape[1:]
    eye = jnp.eye(_GPB, dtype=t.dtype)
    t = t.reshape(S5_BLOCKS, _GPB, a, 1, b) * eye[None, :, None, :, None]
    return t.reshape(S5_BLOCKS, _GPB * a, _GPB * b)


def _blockdiag_extract(m, a, b):
    eye = jnp.eye(_GPB, dtype=m.dtype)
    m = m.reshape(S5_BLOCKS, _GPB, a, _GPB, b) * eye[None, :, None, :, None]
    return jnp.sum(m, axis=3).reshape(S5_GROUPS, a, b)


_SCAN_ROWS = 8


def _cmul(a_r, a_i, b_r, b_i):
    return a_r * b_r - a_i * b_i, a_r * b_i + a_i * b_r


def _scan_tiles(yr_ref, yi_ref, a_r, a_i, c_r, c_i, *, rev, x_refs=None):
    tb, wl = yr_ref.shape
    ntile = tb // _SCAN_ROWS
    with_acc = x_refs is not None
    if rev:
        a_i = -a_i
    pw = [(a_r, a_i)]
    for _ in range(_SCAN_ROWS - 1):
        pw.append(_cmul(*pw[-1], a_r, a_i))
    sub = lax.broadcasted_iota(jnp.int32, (_SCAN_ROWS, wl), 0)
    dist = (_SCAN_ROWS - 1 - sub) if rev else sub
    zero = jnp.zeros((_SCAN_ROWS, wl), F32)
    steps = []
    for kk in (1, 2, 4):
        steps.append(((_SCAN_ROWS - kk) if rev else kk, jnp.where(dist >= kk, pw[kk - 1][0], zero),
                      jnp.where(dist >= kk, pw[kk - 1][1], zero)))
    e_r, e_i = zero, zero
    for d in range(_SCAN_ROWS):
        e_r = jnp.where(dist == d, pw[d][0], e_r)
        e_i = jnp.where(dist == d, pw[d][1], e_i)
    exit_row = 0 if rev else _SCAN_ROWS - 1

    def tile(q, carry):
        idx = (ntile - 1 - q) if rev else q
        rows = pl.ds(pl.multiple_of(idx * _SCAN_ROWS, _SCAN_ROWS), _SCAN_ROWS)
        y_r, y_i = yr_ref[rows, :], yi_ref[rows, :]
        for sh, k_r, k_i in steps:
            t_r, t_i = _cmul(k_r, k_i, pltpu.roll(y_r, sh, 0), pltpu.roll(y_i, sh, 0))
            y_r, y_i = y_r + t_r, y_i + t_i
        cb_r = jnp.broadcast_to(carry[0], (_SCAN_ROWS, wl))
        cb_i = jnp.broadcast_to(carry[1], (_SCAN_ROWS, wl))
        t_r, t_i = _cmul(e_r, e_i, cb_r, cb_i)
        y_r, y_i = y_r + t_r, y_i + t_i
        yr_ref[rows, :] = y_r
        yi_ref[rows, :] = y_i
        out = (y_r[exit_row:exit_row + 1], y_i[exit_row:exit_row + 1])
        if with_acc:
            n_r = jnp.where(sub == _SCAN_ROWS - 1, cb_r, pltpu.roll(y_r, _SCAN_ROWS - 1, 0))
            n_i = jnp.where(sub == _SCAN_ROWS - 1, cb_i, pltpu.roll(y_i, _SCAN_ROWS - 1, 0))
            s_r, s_i = x_refs[0][rows, :], x_refs[1][rows, :]
            out += (carry[2] + (n_r * s_r + n_i * s_i), carry[3] + (n_i * s_r - n_r * s_i))
        return out

    return lax.fori_loop(0, ntile, tile, (c_r, c_i) + ((zero, zero) if with_acc else ()), unroll=2)


_S5_WL = S5_N // S5_BLOCKS
_S5_CW = D_MODEL // S5_BLOCKS
_TN = (((0,), (0,)), ((), ()))


def _s5_fwd(hb, mats, *, tb=1024):
    s = hb.shape[0]
    tb = min(tb, s)

    def body(h_ref, wbr_ref, wbi_ref, wcr_ref, wci_ref, ar_ref, ai_ref, xr_ref, xi_ref, cp_ref, cr, ci):
        @pl.when(pl.program_id(1) == 0)
        def _():
            cr[...] = jnp.zeros_like(cr)
            ci[...] = jnp.zeros_like(ci)

        hv = h_ref[...]
        xr_ref[...] = jnp.dot(hv, wbr_ref[0], preferred_element_type=F32)
        xi_ref[...] = jnp.dot(hv, wbi_ref[0], preferred_element_type=F32)
        cr[...], ci[...] = _scan_tiles(xr_ref, xi_ref, ar_ref[...], ai_ref[...], cr[...], ci[...], rev=False)
        cp_ref[...] = (jnp.dot(xr_ref[...].astype(BF16), wcr_ref[0], preferred_element_type=F32)
                       + jnp.dot(xi_ref[...].astype(BF16), wci_ref[0], preferred_element_type=F32))

    wide = pl.BlockSpec((tb, _S5_WL), lambda c, t: (t, c))
    narrow = pl.BlockSpec((tb, _S5_CW), lambda c, t: (t, c))
    w_in = pl.BlockSpec((1, _S5_CW, _S5_WL), lambda c, t: (c, 0, 0))
    w_out = pl.BlockSpec((1, _S5_WL, _S5_CW), lambda c, t: (c, 0, 0))
    vec = pl.BlockSpec((1, _S5_WL), lambda c, t: (0, c))
    return pl.pallas_call(
        body, grid=(S5_BLOCKS, s // tb), in_specs=[narrow, w_in, w_in, w_out, w_out, vec, vec],
        out_specs=[wide, wide, narrow],
        out_shape=[jax.ShapeDtypeStruct((s, S5_N), F32)] * 2 + [jax.ShapeDtypeStruct((s, D_MODEL), F32)],
        scratch_shapes=[pltpu.VMEM((1, _S5_WL), F32)] * 2, name="s5_fwd",
        compiler_params=_params(("parallel", "arbitrary")),
    )(hb, mats["wb_re"], mats["wb_im"], mats["wc_re"], mats["wc_imn"], mats["ar"], mats["ai"])


def _s5_bwd(dyb, hb, xr, xi, dhd, mats, *, tb=1024):
    s = hb.shape[0]
    tb = min(tb, s)
    nt = s // tb

    def body(dy_ref, h_ref, xr_ref, xi_ref, dhd_ref, wctr_ref, wcti_ref, wbtr_ref, wbti_ref, ar_ref, ai_ref,
             dh_ref, dar_ref, dai_ref, dwbr_ref, dwbi_ref, dwcr_ref, dwci_ref, lr, li, cr, ci):
        @pl.when(pl.program_id(1) == 0)
        def _():
            for r in (cr, ci, dar_ref, dai_ref, dwbr_ref, dwbi_ref, dwcr_ref, dwci_ref):
                r[...] = jnp.zeros_like(r)

        dyv, hv = dy_ref[...], h_ref[...]
        lr[...] = jnp.dot(dyv, wctr_ref[0], preferred_element_type=F32)
        li[...] = jnp.dot(dyv, wcti_ref[0], preferred_element_type=F32)
        cr[...], ci[...], acc_r, acc_i = _scan_tiles(lr, li, ar_ref[...], ai_ref[...], cr[...], ci[...], rev=True,
                                                     x_refs=(xr_ref, xi_ref))
        dar_ref[...] += jnp.sum(acc_r, axis=0, keepdims=True)
        dai_ref[...] += jnp.sum(acc_i, axis=0, keepdims=True)
        lrb, lib = lr[...].astype(BF16), li[...].astype(BF16)
        dh_ref[...] = (jnp.dot(lrb, wbtr_ref[0], preferred_element_type=F32) + dhd_ref[...]
                       + jnp.dot(lib, wbti_ref[0], preferred_element_type=F32))
        dwbr_ref[0] += lax.dot_general(hv, lrb, _TN, preferred_element_type=F32)
        dwbi_ref[0] += lax.dot_general(hv, lib, _TN, preferred_element_type=F32)
        dwcr_ref[0] += lax.dot_general(dyv, xr_ref[...].astype(BF16), _TN, preferred_element_type=F32)
        dwci_ref[0] += lax.dot_general(dyv, xi_ref[...].astype(BF16), _TN, preferred_element_type=F32)

    wide = pl.BlockSpec((tb, _S5_WL), lambda c, t: (nt - 1 - t, c))
    narrow = pl.BlockSpec((tb, _S5_CW), lambda c, t: (nt - 1 - t, c))
    w_in = pl.BlockSpec((1, _S5_CW, _S5_WL), lambda c, t: (c, 0, 0))
    w_out = pl.BlockSpec((1, _S5_WL, _S5_CW), lambda c, t: (c, 0, 0))
    vec = pl.BlockSpec((1, _S5_WL), lambda c, t: (0, c))
    acc_shape = jax.ShapeDtypeStruct((S5_BLOCKS, _S5_CW, _S5_WL), F32)
    return pl.pallas_call(
        body, grid=(S5_BLOCKS, nt), in_specs=[narrow, narrow, wide, wide, narrow, w_in, w_in, w_out, w_out, vec, vec],
        out_specs=[narrow, vec, vec, w_in, w_in, w_in, w_in],
        out_shape=[jax.ShapeDtypeStruct((s, D_MODEL), F32)] + [jax.ShapeDtypeStruct((1, S5_N), F32)] * 2
        + [acc_shape] * 4,
        scratch_shapes=[pltpu.VMEM((tb, _S5_WL), F32)] * 2 + [pltpu.VMEM((1, _S5_WL), F32)] * 2, name="s5_bwd",
        compiler_params=_params(("parallel", "arbitrary")),
    )(dyb, hb, xr, xi, dhd, mats["wct_re"], mats["wct_imn"], mats["wbt_re"], mats["wbt_im"], mats["ar"], mats["ai"])


def _s5_mats(w, o):
    ab_re, ab_im, bb_re, bb_im = _s5_disc(w["od_a_re"][o], w["od_a_im"][o], w["od_log_dt"][o], w["od_b_re"][o],
                                          w["od_b_im"][o])
    c_re, c_im = w["od_c_re"][o], w["od_c_im"][o]
    tr = lambda t: t.transpose(0, 2, 1)
    bd = lambda t: _blockdiag(t).astype(BF16)
    return {
        "ar": ab_re.reshape(1, S5_N), "ai": ab_im.reshape(1, S5_N),
        "wb_re": bd(tr(bb_re)), "wb_im": bd(tr(bb_im)), "wc_re": bd(tr(c_re)), "wc_imn": bd(-tr(c_im)),
        "wct_re": bd(c_re), "wct_imn": bd(-c_im), "wbt_re": bd(bb_re), "wbt_im": bd(bb_im),
    }


def _odd_fwd(x0, w, o, i):
    mats = _s5_mats(w, o)
    nm = w["norm_mix"][i:i + 1]
    d_row = w["od_d"][o:o + 1]
    hb = _rms_fwd(x0, nm, name="mix_norm")
    xr, xi, cp = _s5_fwd(hb, mats)

    def fn(xv, cpv, gv, dv):
        y = cpv + dv * (xv * _rstd(xv) * gv)
        return y, _gelu(y)

    y, gy = _rw(fn, [x0, cp], [nm, d_row], [(D_MODEL, F32), (D_MODEL, BF16)], tr=512, name="s5_out")
    gg = _mm(gy, w["od_w_glu"][o], tb=True, name="od_glu")
    x1 = _rw(lambda xv, g: xv + g[:, :D_MODEL] * _sigmoid(g[:, D_MODEL:]), [x0, gg], [], [(D_MODEL, F32)], tr=512,
             name="od_glu_out")[0]
    return x1, (hb, xr, xi, y, gy, gg)


def _odd_bwd(dx1, x0, saved, w, o, i):
    hb, xr, xi, y, gy, gg = saved
    mats = _s5_mats(w, o)
    nm = w["norm_mix"][i:i + 1]
    d_row = w["od_d"][o:o + 1]

    def fn_glu(dv, g):
        ga, gb = g[:, :D_MODEL], g[:, D_MODEL:]
        sg = _sigmoid(gb)
        return jnp.concatenate([dv * sg, dv * ga * (sg * (1.0 - sg))], axis=1)

    dgg = _rw(fn_glu, [dx1, gg], [], [(2 * D_MODEL, BF16)], tr=512, name="od_glu_out_bwd")[0]
    d_wglu = _mm(gy, dgg, ta=True, name="od_glu_dw", bn=1024).T
    d_gy = _mm(dgg, w["od_w_glu"][o], name="od_glu_dx")

    def fn_y(dg, yv, xv, gv, dv):
        dy = dg * _gelu_grad(yv)
        h = xv * _rstd(xv) * gv
        return dy, dv * dy, jnp.sum(dy * h, axis=0, keepdims=True)

    dyb, dhd, d_d = _rw(fn_y, [d_gy, y, x0], [nm, d_row], [(D_MODEL, BF16), (D_MODEL, F32)], [(1, D_MODEL)], tr=512,
                        name="s5_out_bwd")
    dh, d_ar, d_ai, d_wb_re, d_wb_im, d_wc_re, d_wc_im = _s5_bwd(dyb, hb, xr, xi, dhd, mats)
    gc, gp = S5_GROUP_CH, S5_STATE
    d_c_re = _blockdiag_extract(d_wc_re, gc, gp)
    d_c_im = -_blockdiag_extract(d_wc_im, gc, gp)
    d_bb_re = _blockdiag_extract(d_wb_re, gc, gp).transpose(0, 2, 1)
    d_bb_im = _blockdiag_extract(d_wb_im, gc, gp).transpose(0, 2, 1)
    dx0, d_nm = _rms_bwd_call(x0, dh, nm, dx1, name="mix_norm_bwd")
    _, vjp = jax.vjp(_s5_disc, w["od_a_re"][o], w["od_a_im"][o], w["od_log_dt"][o], w["od_b_re"][o], w["od_b_im"][o])
    d_a_re, d_a_im, d_log_dt, d_b_re, d_b_im = vjp(
        (d_ar.reshape(S5_GROUPS, S5_STATE), d_ai.reshape(S5_GROUPS, S5_STATE), d_bb_re, d_bb_im))
    grads = {"od_a_re": d_a_re, "od_a_im": d_a_im, "od_log_dt": d_log_dt, "od_b_re": d_b_re, "od_b_im": d_b_im,
             "od_c_re": d_c_re, "od_c_im": d_c_im, "od_d": d_d[0], "od_w_glu": d_wglu, "norm_mix": d_nm[0]}
    return dx0, grads


_HBM = pl.BlockSpec(memory_space=pltpu.HBM)


def _mesh_pos():
    return lax.axis_index("x"), lax.axis_index("y"), lax.axis_index("c")


def _slot(px, py, pc):
    return 4 * px + 2 * py + pc


def _all_gather(x, *, name):
    def body(x_ref, out_ref, send_sems, recv_sems, local_sem):
        mx, my, mc = _mesh_pos()
        me, sibling = (mx, my, mc), (mx, my, 1 - mc)
        chips = [(1 - mx, my), (mx, 1 - my), (1 - mx, 1 - my)]

        def copy(k, block, to, src=None):
            dst = out_ref.at[_slot(*block)]
            return pltpu.make_async_remote_copy(
                src_ref=dst if src is None else src, dst_ref=dst, send_sem=send_sems.at[k], recv_sem=recv_sems.at[k],
                device_id=to, device_id_type=MESH)

        mine = pltpu.make_async_copy(x_ref, out_ref.at[_slot(*me)], local_sem)
        mine.start()
        first = [copy(0, me, sibling, src=x_ref)]
        first += [copy(1 + j, me, (*chip, mc), src=x_ref) for j, chip in enumerate(chips)]
        for cp in first:
            cp.start()
        passed = [copy(4 + j, (*chip, mc), sibling) for j, chip in enumerate(chips)]
        for j, chip in enumerate(chips):
            copy(1 + j, (*chip, mc), me).wait_recv()
            passed[j].start()
        copy(0, sibling, me).wait_recv()
        for j, chip in enumerate(chips):
            copy(4 + j, (*chip, 1 - mc), me).wait_recv()
        for cp in first + passed:
            cp.wait_send()
        mine.wait()

    return pl.pallas_call(
        body, out_shape=jax.ShapeDtypeStruct((N_DEV,) + x.shape, x.dtype), in_specs=[_HBM], out_specs=_HBM,
        scratch_shapes=[pltpu.SemaphoreType.DMA((7,)), pltpu.SemaphoreType.DMA((7,)), pltpu.SemaphoreType.DMA],
        name=name,
    )(x)


_SEMS = pltpu.SemaphoreType.DMA((N_DEV - 1,))
_SEM_SPEC = pl.BlockSpec(memory_space=pltpu.SEMAPHORE)
_SPLIT_COPY = pltpu.CompilerParams(has_side_effects=pltpu.SideEffectType.DATAFLOW_SIDE_EFFECTING)


def _exchange_copies(x_refs, land_refs, send_sems, recv_sems, gather):
    mx, my, mc = _mesh_pos()
    my_slot = _slot(mx, my, mc)
    copies = []
    for a, (x_ref, land_ref) in enumerate(zip(x_refs, land_refs)):
        for k in range(1, N_DEV):
            peer = (1 - mx if k & 4 else mx, 1 - my if k & 2 else my, 1 - mc if k & 1 else mc)
            copies.append(pltpu.make_async_remote_copy(
                src_ref=x_ref if gather else x_ref.at[_slot(*peer)], dst_ref=land_ref.at[my_slot],
                send_sem=send_sems[a].at[k - 1], recv_sem=recv_sems[a].at[k - 1], device_id=peer, device_id_type=MESH))
    return copies


def _exchange_start(xs, carry, *, gather, name):
    n = len(xs)

    def body(*refs):
        outs = refs[2 * n + 1:]
        for cp in _exchange_copies(refs[:n], refs[n:2 * n], outs[:n], outs[n:2 * n], gather):
            cp.start()

    hbm = lambda a: pltpu.HBM(a.shape, a.dtype)
    lands = [pltpu.with_memory_space_constraint(lax.empty(((N_DEV,) if gather else ()) + x.shape, x.dtype), pltpu.HBM)
             for x in xs]
    ins = [pltpu.with_memory_space_constraint(a, pltpu.HBM) for a in list(xs) + [carry]]
    res = pl.pallas_call(
        body, name=name, in_specs=[_HBM] * (2 * n + 1), out_specs=[_SEM_SPEC] * (2 * n) + [_HBM] * (2 * n + 1),
        out_shape=[_SEMS] * (2 * n) + [hbm(a) for a in list(xs) + lands] + [hbm(carry)],
        input_output_aliases={**{a: 2 * n + a for a in range(2 * n)}, 2 * n: 4 * n}, compiler_params=_SPLIT_COPY,
    )(*ins[:n], *lands, ins[n])
    return (res[:n], res[n:2 * n], res[2 * n:3 * n], res[3 * n:4 * n]), res[4 * n]


def _exchange_wait(handles, after, *, gather, name):
    send_sems, recv_sems, xs, lands = handles
    n = len(xs)
    after = list(after) if isinstance(after, (list, tuple)) else [after]

    def body(*refs):
        for cp in _exchange_copies(refs[:n], refs[n:2 * n], refs[2 * n:3 * n], refs[3 * n:4 * n], gather):
            cp.wait_send()
            cp.wait_recv()

    hbm = lambda a: pltpu.HBM(a.shape, a.dtype)
    res = pl.pallas_call(
        body, name=name, in_specs=[_HBM] * (2 * n) + [_SEM_SPEC] * (2 * n) + [_ANY] * len(after),
        out_specs=[_HBM] * (2 * n), out_shape=[hbm(a) for a in list(xs) + list(lands)],
        input_output_aliases={a: a for a in range(2 * n)}, compiler_params=_SPLIT_COPY,
    )(*xs, *lands, *send_sems, *recv_sems, *after)
    return res[:n], res[n:]


_ADAMW_PARTS_BLOCK_BYTES = 8 * 1024 * 1024


def _adamw(parts, w, m, v, *, name, own=None):
    r, c = w.shape
    tr = r
    for cand in range(8, r + 1, 8):
        if r % cand == 0 and N_DEV * cand * c * 4 <= _ADAMW_PARTS_BLOCK_BYTES:
            tr = cand
    has_own = own is not None

    def body(*refs):
        p_ref, w_ref, m_ref, v_ref = refs[0], refs[1 + has_own], refs[2 + has_own], refs[3 + has_own]
        g_ref, d_ref, nm_ref, nv_ref = refs[4 + has_own:]
        if has_own:
            me = _slot(*_mesh_pos())
            part = lambda k: jnp.where(me == k, refs[1][...], p_ref[k])
        else:
            part = lambda k: p_ref[k]
        g = part(0)
        for k in range(1, N_DEV):
            g = g + part(k)
        nm = ADAM_B1 * m_ref[...] + (1.0 - ADAM_B1) * g
        nv = ADAM_B2 * v_ref[...] + (1.0 - ADAM_B2) * (g * g)
        m_hat = nm / (1.0 - ADAM_B1 ** ADAM_STEP)
        v_hat = nv / (1.0 - ADAM_B2 ** ADAM_STEP)
        g_ref[...] = g
        d_ref[...] = -ADAM_LR * (m_hat / (jnp.sqrt(v_hat) + ADAM_EPS) + ADAM_WD * w_ref[...])
        nm_ref[...] = nm
        nv_ref[...] = nv

    blk = pl.BlockSpec((tr, c), lambda i: (i, 0))
    return pl.pallas_call(
        body, grid=(r // tr,),
        in_specs=[pl.BlockSpec((N_DEV, tr, c), lambda i: (0, i, 0))] + [blk] * (3 + has_own),
        out_specs=[blk] * 4, out_shape=[jax.ShapeDtypeStruct((r, c), F32)] * 4, name=name,
        compiler_params=_params(("parallel",)),
    )(parts, *([own] if has_own else []), w, m, v)


_FLAT_COLS = 1024
_FLAT_ROW_ALIGN = 128

_SHARD_AXIS = {
    "norm_mix": None, "norm_ffn": None, "ev_w_in": 2, "ev_b_fgate": None, "ev_q_norm": None, "ev_k_norm": None,
    "ev_v_norm": None, "ev_w_spatial": None, "ev_b_spatial": None, "ev_w_out": 1, "od_a_re": None, "od_a_im": None,
    "od_log_dt": None, "od_b_re": None, "od_b_im": None, "od_c_re": None, "od_c_im": None, "od_d": 1, "od_w_glu": 2,
    "ffn_w_up": 2, "ffn_conv_w": 2, "ffn_conv_b": None, "ffn_w_down": 1, "ple_w_proj": 2, "ple_w_gate": 1,
}
_WEIGHTS = list(_SHARD_AXIS)
_REPLICATED = [n for n in _WEIGHTS if _SHARD_AXIS[n] is None]
_COL_SHARDED = ["ev_w_in", "od_w_glu", "ffn_w_up", "ple_w_proj"]
_ROW_SHARDED = ["ev_w_out", "ffn_w_down", "ple_w_gate"]
_SMALL_SHARDED = ["od_d", "ffn_conv_w"]


def _pack(arrays, lead=()):
    nl = len(lead)
    flat = jnp.concatenate([a.reshape(lead + (-1,)) for a in arrays], axis=nl)
    n = flat.shape[nl]
    chunk = _FLAT_COLS * _FLAT_ROW_ALIGN
    total = -(-n // chunk) * chunk
    flat = jnp.pad(flat, [(0, 0)] * nl + [(0, total - n)])
    return flat.reshape(lead + (total // _FLAT_COLS, _FLAT_COLS))


def _unpack(buf, shapes, lead=()):
    nl = len(lead)
    flat = buf.reshape(lead + (-1,))
    out, off = [], 0
    for shp in shapes:
        n = math.prod(shp)
        out.append(lax.slice_in_dim(flat, off, off + n, axis=nl).reshape(lead + tuple(shp)))
        off += n
    return out


def _join_shards(g, axis):
    g = jnp.moveaxis(g, 0, axis)
    shp = g.shape
    return g.reshape(shp[:axis] + (shp[axis] * shp[axis + 1],) + shp[axis + 2:])


def _split_shards(full, axis):
    shp = full.shape
    g = full.reshape(shp[:axis] + (N_DEV, shp[axis] // N_DEV) + shp[axis + 1:])
    return jnp.moveaxis(g, axis, 0)


def _local_step(x, p, tgt, w, before_part, on_grads):
    saved = []
    h = x
    for i in range(DEPTH):
        x0 = before_part(i, 0, h)
        if i % 2 == 0:
            x1, sv_mix = _even_fwd(x0, w, i // 2, i)
        else:
            x1, sv_mix = _odd_fwd(x0, w, i // 2, i)
        x1 = before_part(i, 1, x1)
        x2, sv_ffn = _ffn_fwd(x1, w, i)
        x3, sv_ple = _ple_fwd(x2, p[i], w, i)
        saved.append((x0, x1, x2, sv_mix, sv_ffn, sv_ple))
        h = x3
    loss, dh = _loss_and_grad(h, tgt)
    per_layer = {}
    for i in reversed(range(DEPTH)):
        x0, x1, x2, sv_mix, sv_ffn, sv_ple = saved[i]
        dx2, dx2b, g_ple = _ple_bwd(dh, x2, p[i], sv_ple, w, i)
        dx1, dx1b, g_ffn = _ffn_bwd(dx2, dx2b, x1, sv_ffn, w, i)
        dx1 = on_grads(i, 0, {**g_ple, **g_ffn}, dx1)
        if i % 2 == 0:
            dh, g_mix = _even_bwd(dx1, dx1b, x0, sv_mix, w, i // 2, i)
        else:
            dh, g_mix = _odd_bwd(dx1, x0, sv_mix, w, i // 2, i)
        dh = on_grads(i, 1, g_mix, dh)
        layer_grads = {**g_ple, **g_ffn, **g_mix}
        for name, g in layer_grads.items():
            per_layer.setdefault(name, []).append(g)
    grads = {name: gs[::-1] for name, gs in per_layer.items()}
    return loss, dh, grads


def kernel(x, p, norm_mix, norm_ffn, ev_w_in, ev_b_fgate, ev_q_norm, ev_k_norm, ev_v_norm, ev_w_spatial, ev_b_spatial, ev_w_out, od_a_re, od_a_im, od_log_dt, od_b_re, od_b_im, od_c_re, od_c_im, od_d, od_w_glu, ffn_w_up, ffn_conv_w, ffn_conv_b, ffn_w_down, ple_w_proj, ple_w_gate, loss_target, m_norm_mix, m_norm_ffn, m_ev_w_in, m_ev_b_fgate, m_ev_q_norm, m_ev_k_norm, m_ev_v_norm, m_ev_w_spatial, m_ev_b_spatial, m_ev_w_out, m_od_a_re, m_od_a_im, m_od_log_dt, m_od_b_re, m_od_b_im, m_od_c_re, m_od_c_im, m_od_d, m_od_w_glu, m_ffn_w_up, m_ffn_conv_w, m_ffn_conv_b, m_ffn_w_down, m_ple_w_proj, m_ple_w_gate, v_norm_mix, v_norm_ffn, v_ev_w_in, v_ev_b_fgate, v_ev_q_norm, v_ev_k_norm, v_ev_v_norm, v_ev_w_spatial, v_ev_b_spatial, v_ev_w_out, v_od_a_re, v_od_a_im, v_od_log_dt, v_od_b_re, v_od_b_im, v_od_c_re, v_od_c_im, v_od_d, v_od_w_glu, v_ffn_w_up, v_ffn_conv_w, v_ffn_conv_b, v_ffn_w_down, v_ple_w_proj, v_ple_w_gate):
    args = locals()
    wl = {n: args[n] for n in _WEIGHTS}
    ml = {n: args["m_" + n] for n in _WEIGHTS}
    vl = {n: args["v_" + n] for n in _WEIGHTS}

    kinds = ("g", "d", "m", "v")

    def local(n, layer, a):
        return a[n][layer].T if n in _COL_SHARDED else a[n][layer]

    me = _slot(*_mesh_pos())
    w = {n: wl[n] for n in _REPLICATED}
    small = _all_gather(_pack([wl[n] for n in _SMALL_SHARDED]), name="gather_small_weights")
    for n, g in zip(_SMALL_SHARDED, _unpack(small, [wl[n].shape for n in _SMALL_SHARDED], lead=(N_DEV,))):
        w[n] = _join_shards(g, _SHARD_AXIS[n])

    def layer_of(n, i):
        return i if wl[n].shape[0] == DEPTH else i // 2

    def weights_of(i, part):
        if part == 0:
            return ["ev_w_in", "ev_w_out"] if i % 2 == 0 else ["od_w_glu"]
        return ["ffn_w_up", "ffn_w_down", "ple_w_proj", "ple_w_gate"]

    gathers = {}
    for n in _COL_SHARDED + _ROW_SHARDED:
        w[n] = {}

    def start_gather(i, carry):
        for part in (0, 1):
            blocks = [local(n, layer_of(n, i), wl).astype(BF16) for n in weights_of(i, part)]
            gathers[i, part], carry = _exchange_start(blocks, carry, gather=True,
                                                      name="gather_start_layer%d_part%d" % (i, part))
        return carry

    def finish_gather(i, part, h):
        sent, landed = _exchange_wait(gathers[i, part], h, gather=True, name="gather_wait_layer%d_part%d" % (i, part))
        for n, block, land in zip(weights_of(i, part), sent, landed):
            full = lax.dynamic_update_slice(land, block[None], (me, 0, 0)).reshape(N_DEV * block.shape[0], block.shape[1])
            if n == "ev_w_in":
                full = jnp.pad(full, ((0, IN_COLS_PAD - IN_COLS), (0, 0)))
            w[n][layer_of(n, i)] = full
        if (i, part) == (0, 0):
            for later in range(1, DEPTH):
                h = start_gather(later, h)
        return h

    x_in = start_gather(0, x[0])

    in_flight = {}

    def start_scatter(i, part, part_grads, dh):
        names = [n for n in _COL_SHARDED + _ROW_SHARDED if n in part_grads]
        xs = [part_grads[n].reshape(N_DEV, part_grads[n].shape[0] // N_DEV, part_grads[n].shape[1]) for n in names]
        handles, dh = _exchange_start(xs, dh, gather=False, name="scatter_start_layer%d_part%d" % (i, part))
        in_flight[i, part] = (names, handles)
        for n in rep_early:
            if n in part_grads:
                s5_grads.setdefault(n, {})[i // 2] = part_grads[n]
        if (i, part) == (1, 1):
            send = _pack([jnp.stack([s5_grads[n][o] for o in sorted(s5_grads[n])]) for n in rep_early])
            s5_handles[0], dh = _exchange_start([send], dh, gather=True, name="gather_start_s5_grads")
        return dh

    rep_early = [n for n in _REPLICATED if n.startswith("od_")]
    rep_late = [n for n in _REPLICATED if not n.startswith("od_")]
    s5_grads, s5_handles = {}, [None]
    loss, grad_x, grads = _local_step(x_in, p[:, 0], loss_target[0], w, finish_gather, start_scatter)
    loss = lax.psum(loss, ("x", "y", "c"))

    rep_send = _pack([jnp.stack(grads[n]) for n in rep_late])
    small_send = _pack([_split_shards(jnp.stack(grads[n]), _SHARD_AXIS[n]) for n in _SMALL_SHARDED], lead=(N_DEV,))
    rep_handles, grad_x = _exchange_start([rep_send], grad_x, gather=True, name="gather_start_replicated_grads")
    small_handles, grad_x = _exchange_start([small_send], grad_x, gather=False, name="scatter_start_small_grads")

    out = {kind: {} for kind in kinds}
    done = {}
    def update_replicated(group, handles, tag, after):
        (rep_sent,), (parts,) = _exchange_wait(handles, after, gather=True, name="gather_wait_%s_grads" % tag)
        shapes = [wl[n].shape for n in group]
        res = _adamw(parts, _pack([wl[n] for n in group]), _pack([ml[n] for n in group]),
                     _pack([vl[n] for n in group]), name="adamw_" + tag, own=rep_sent)
        for kind, buf in zip(kinds, res):
            out[kind].update(zip(group, _unpack(buf, shapes)))
        return res[0]

    behind = [grad_x]
    for i, part in sorted(in_flight, key=lambda ip: (-ip[0], ip[1])):
        if (i, part) == (0, 0):
            behind.append(update_replicated(rep_early, s5_handles[0], "s5", behind))
        names, handles = in_flight[i, part]
        sent, landed = _exchange_wait(handles, behind, gather=False, name="scatter_wait_layer%d_part%d" % (i, part))
        behind = []
        for n, x_sent, parts in zip(names, sent, landed):
            layer = layer_of(n, i)
            own = lax.dynamic_index_in_dim(x_sent, me, axis=0, keepdims=False)
            res = _adamw(parts, local(n, layer, wl), local(n, layer, ml), local(n, layer, vl), name="adamw_" + n, own=own)
            behind.append(res[0])
            done.setdefault(n, {})[layer] = [r.T if n in _COL_SHARDED else r for r in res]
    for n, layers in done.items():
        for k, kind in enumerate(kinds):
            out[kind][n] = jnp.stack([layers[layer][k] for layer in sorted(layers)])

    last = behind
    (small_sent,), (parts,) = _exchange_wait(small_handles, last, gather=False, name="scatter_wait_small_grads")
    shapes = [wl[n].shape for n in _SMALL_SHARDED]
    res = _adamw(parts, _pack([wl[n] for n in _SMALL_SHARDED]), _pack([ml[n] for n in _SMALL_SHARDED]),
                 _pack([vl[n] for n in _SMALL_SHARDED]), name="adamw_small",
                 own=lax.dynamic_index_in_dim(small_sent, me, axis=0, keepdims=False))
    for kind, buf in zip(kinds, res):
        out[kind].update(zip(_SMALL_SHARDED, _unpack(buf, shapes)))

    update_replicated(rep_late, rep_handles, "replicated", last)

    return (loss, grad_x[None], *[out["g"][n] for n in _WEIGHTS], *[out["d"][n] for n in _WEIGHTS],
            *[out["m"][n] for n in _WEIGHTS], *[out["v"][n] for n in _WEIGHTS])
```
